```python
import jax, jax.numpy as jnp
from jax import lax
import numpy as np

D_MODEL = 1024
BATCH = 8
SEQ = 4096
DEPTH = 4

N_HEADS = 16
HEAD_DIM = D_MODEL // N_HEADS
D_FF = ((8 * D_MODEL // 3 + 127) // 128) * 128
CONV_WIDTH = 3
Q_BLOCK = 128
N_MIXERS = 2
N_MOD = 6
RMS_EPS = 1e-6
N_SB_LAYERS = (DEPTH + 1) // 2
N_FOX_LAYERS = DEPTH // 2

kernel_name = "hybrid_stickbreak_fox_convffn"


def rmsnorm(x, g):
    xf = x.astype(jnp.float32)
    y = xf * lax.rsqrt(jnp.mean(xf * xf, axis=-1, keepdims=True) + RMS_EPS)
    return (y * g.astype(jnp.float32)).astype(x.dtype)


def modulate(h, shift, scale):
    return h * (1.0 + scale[:, None, :]) + shift[:, None, :]


def split_heads(t):
    b, s, _ = t.shape
    return t.reshape(b, s, N_HEADS, HEAD_DIM).transpose(0, 2, 1, 3)


def merge_heads(t):
    b, h, s, d = t.shape
    return t.transpose(0, 2, 1, 3).reshape(b, s, h * d)


def stick_breaking_attention(q, k, v):
    s_len = q.shape[2]
    scale = HEAD_DIM ** -0.5
    outs = []
    for blk in range(s_len // Q_BLOCK):
        q0 = blk * Q_BLOCK
        k_end = q0 + Q_BLOCK
        qb, kb, vb = q[:, :, q0:k_end], k[:, :, :k_end], v[:, :, :k_end]
        z = jnp.einsum('bhqd,bhkd->bhqk', qb, kb).astype(jnp.float32) * scale
        t_idx = q0 + jnp.arange(Q_BLOCK)[:, None]
        s_idx = jnp.arange(k_end)[None, :]
        strict = s_idx < t_idx
        log_beta = jax.nn.log_sigmoid(z)
        log_1mb = jnp.where(strict, jax.nn.log_sigmoid(-z), 0.0)
        after = lax.cumsum(log_1mb, axis=3, reverse=True) - log_1mb
        a = jnp.where(strict, jnp.exp(log_beta + after), 0.0)
        outs.append(jnp.einsum('bhqk,bhkd->bhqd', a.astype(vb.dtype), vb))
    return jnp.concatenate(outs, axis=2)


def forgetting_attention(q, k, v, log_f):
    s_len = q.shape[2]
    scale = HEAD_DIM ** -0.5
    cum = lax.cumsum(log_f, axis=2)
    outs = []
    for blk in range(s_len // Q_BLOCK):
        q0 = blk * Q_BLOCK
        k_end = q0 + Q_BLOCK
        qb, kb, vb = q[:, :, q0:k_end], k[:, :, :k_end], v[:, :, :k_end]
        z = jnp.einsum('bhqd,bhkd->bhqk', qb, kb).astype(jnp.float32) * scale
        decay = cum[:, :, q0:k_end, None] - cum[:, :, None, :k_end]
        t_idx = q0 + jnp.arange(Q_BLOCK)[:, None]
        s_idx = jnp.arange(k_end)[None, :]
        logits = jnp.where(s_idx <= t_idx, z + decay, -jnp.inf)
        p = jax.nn.softmax(logits, axis=-1)
        outs.append(jnp.einsum('bhqk,bhkd->bhqd', p.astype(vb.dtype), vb))
    return jnp.concatenate(outs, axis=2)


def causal_depthwise_conv(h, w, b):
    s_len = h.shape[1]
    hp = jnp.pad(h, ((0, 0), (CONV_WIDTH - 1, 0), (0, 0)))
    out = b[None, None, :]
    for kk in range(CONV_WIDTH):
        out = out + w[kk][None, None, :] * hp[:, kk:kk + s_len]
    return out


def _fwd_setup_inputs(seed: int = 0) -> dict:
    key = jax.random.key(seed)
    ks = jax.random.split(key, 20)
    f32 = jnp.float32
    d, f, h = D_MODEL, D_FF, N_HEADS

    def nrm(k, shape, s):
        return jax.random.normal(k, shape, f32) * s

    return {
        "x": nrm(ks[0], (BATCH, SEQ, d), 1.0),
        "c": nrm(ks[1], (BATCH, d), 1.0),
        "w_mod": nrm(ks[2], (DEPTH, d, N_MOD * d), 0.5 * d ** -0.5),
        "b_mod": nrm(ks[3], (DEPTH, N_MOD * d), 0.02),
        "g_mix_pre": 1.0 + nrm(ks[4], (DEPTH, d), 0.05),
        "g_mix_post": 1.0 + nrm(ks[5], (DEPTH, d), 0.05),
        "w_qkv": nrm(ks[6], (DEPTH, d, 3 * d), d ** -0.5),
        "w_o": nrm(ks[7], (DEPTH, d, d), d ** -0.5),
        "w_fg": nrm(ks[8], (N_FOX_LAYERS, d, h), d ** -0.5),
        "b_fg": 3.0 + nrm(ks[9], (N_FOX_LAYERS, h), 0.5),
        "g_ffn_pre": 1.0 + nrm(ks[10], (DEPTH, d), 0.05),
        "g_ffn_post": 1.0 + nrm(ks[11], (DEPTH, d), 0.05),
        "w_ffn_gate": nrm(ks[12], (DEPTH, d, f), d ** -0.5),
        "w_ffn_up": nrm(ks[13], (DEPTH, d, f), d ** -0.5),
        "w_conv": nrm(ks[14], (DEPTH, CONV_WIDTH, f), CONV_WIDTH ** -0.5),
        "b_conv": nrm(ks[15], (DEPTH, f), 0.02),
        "w_ffn_down": nrm(ks[16], (DEPTH, f, d), f ** -0.5),
    }


def _fwd_reference(x, c, w_mod, b_mod, g_mix_pre, g_mix_post, w_qkv, w_o, w_fg, b_fg,
              g_ffn_pre, g_ffn_post, w_ffn_gate, w_ffn_up, w_conv, b_conv, w_ffn_down):
    c_act = jax.nn.silu(c)
    for i in range(DEPTH):
        mod = jnp.einsum('bd,dm->bm', c_act, w_mod[i]) + b_mod[i]
        sh_a, sc_a, gt_a, sh_f, sc_f, gt_f = jnp.split(mod, N_MOD, axis=-1)

        h = modulate(rmsnorm(x, g_mix_pre[i]), sh_a, sc_a)
        qkv = jnp.einsum('bsd,de->bse', h, w_qkv[i])
        q, k, v = (split_heads(t) for t in jnp.split(qkv, 3, axis=-1))
        if i % N_MIXERS == 0:
            o = stick_breaking_attention(q, k, v)
        else:
            j = i // N_MIXERS
            f_logit = jnp.einsum('bsd,dh->bhs', h, w_fg[j]) + b_fg[j][None, :, None]
            log_f = jax.nn.log_sigmoid(f_logit.astype(jnp.float32))
            o = forgetting_attention(q, k, v, log_f)
        o = jnp.einsum('bse,ed->bsd', merge_heads(o), w_o[i])
        x = x + gt_a[:, None, :] * rmsnorm(o, g_mix_post[i])

        h = modulate(rmsnorm(x, g_ffn_pre[i]), sh_f, sc_f)
        gate = jnp.einsum('bsd,df->bsf', h, w_ffn_gate[i])
        up = jnp.einsum('bsd,df->bsf', h, w_ffn_up[i])
        gate = causal_depthwise_conv(gate, w_conv[i], b_conv[i])
        y = jnp.einsum('bsf,fd->bsd', jax.nn.silu(gate) * up, w_ffn_down[i])
        x = x + gt_f[:, None, :] * rmsnorm(y, g_ffn_post[i])
    return x


import jax as _jax
import jax.numpy as _jnp

TWIN_FORMAT = 'train_step'
FWD_PARAMS = ['x', 'c', 'w_mod', 'b_mod', 'g_mix_pre', 'g_mix_post', 'w_qkv', 'w_o', 'w_fg', 'b_fg', 'g_ffn_pre', 'g_ffn_post', 'w_ffn_gate', 'w_ffn_up', 'w_conv', 'b_conv', 'w_ffn_down']
TWIN_WEIGHTS = ['w_mod', 'b_mod', 'g_mix_pre', 'g_mix_post', 'w_qkv', 'w_o', 'w_fg', 'b_fg', 'g_ffn_pre', 'g_ffn_post', 'w_ffn_gate', 'w_ffn_up', 'w_conv', 'b_conv', 'w_ffn_down']
TWIN_DIFF_INPUT = 'x'
TWIN_INPUTS = ['x', 'c', 'w_mod', 'b_mod', 'g_mix_pre', 'g_mix_post', 'w_qkv', 'w_o', 'w_fg', 'b_fg', 'g_ffn_pre', 'g_ffn_post', 'w_ffn_gate', 'w_ffn_up', 'w_conv', 'b_conv', 'w_ffn_down', 'loss_target', 'm_w_mod', 'm_b_mod', 'm_g_mix_pre', 'm_g_mix_post', 'm_w_qkv', 'm_w_o', 'm_w_fg', 'm_b_fg', 'm_g_ffn_pre', 'm_g_ffn_post', 'm_w_ffn_gate', 'm_w_ffn_up', 'm_w_conv', 'm_b_conv', 'm_w_ffn_down', 'v_w_mod', 'v_b_mod', 'v_g_mix_pre', 'v_g_mix_post', 'v_w_qkv', 'v_w_o', 'v_w_fg', 'v_b_fg', 'v_g_ffn_pre', 'v_g_ffn_post', 'v_w_ffn_gate', 'v_w_ffn_up', 'v_w_conv', 'v_b_conv', 'v_w_ffn_down']
TWIN_OUTPUTS = ['loss', 'grad_x', 'grad_w_mod', 'grad_b_mod', 'grad_g_mix_pre', 'grad_g_mix_post', 'grad_w_qkv', 'grad_w_o', 'grad_w_fg', 'grad_b_fg', 'grad_g_ffn_pre', 'grad_g_ffn_post', 'grad_w_ffn_gate', 'grad_w_ffn_up', 'grad_w_conv', 'grad_b_conv', 'grad_w_ffn_down', 'delta_w_mod', 'delta_b_mod', 'delta_g_mix_pre', 'delta_g_mix_post', 'delta_w_qkv', 'delta_w_o', 'delta_w_fg', 'delta_b_fg', 'delta_g_ffn_pre', 'delta_g_ffn_post', 'delta_w_ffn_gate', 'delta_w_ffn_up', 'delta_w_conv', 'delta_b_conv', 'delta_w_ffn_down', 'new_m_w_mod', 'new_m_b_mod', 'new_m_g_mix_pre', 'new_m_g_mix_post', 'new_m_w_qkv', 'new_m_w_o', 'new_m_w_fg', 'new_m_b_fg', 'new_m_g_ffn_pre', 'new_m_g_ffn_post', 'new_m_w_ffn_gate', 'new_m_w_ffn_up', 'new_m_w_conv', 'new_m_b_conv', 'new_m_w_ffn_down', 'new_v_w_mod', 'new_v_b_mod', 'new_v_g_mix_pre', 'new_v_g_mix_post', 'new_v_w_qkv', 'new_v_w_o', 'new_v_w_fg', 'new_v_b_fg', 'new_v_g_ffn_pre', 'new_v_g_ffn_post', 'new_v_w_ffn_gate', 'new_v_w_ffn_up', 'new_v_w_conv', 'new_v_b_conv', 'new_v_w_ffn_down']
TWIN_LEAF_KINDS = {'loss': 'loss', 'grad_x': 'grad_x', 'grad_w_mod': 'grad_w', 'grad_b_mod': 'grad_w', 'grad_g_mix_pre': 'grad_w', 'grad_g_mix_post': 'grad_w', 'grad_w_qkv': 'grad_w', 'grad_w_o': 'grad_w', 'grad_w_fg': 'grad_w', 'grad_b_fg': 'grad_w', 'grad_g_ffn_pre': 'grad_w', 'grad_g_ffn_post': 'grad_w', 'grad_w_ffn_gate': 'grad_w', 'grad_w_ffn_up': 'grad_w', 'grad_w_conv': 'grad_w', 'grad_b_conv': 'grad_w', 'grad_w_ffn_down': 'grad_w', 'delta_w_mod': 'delta_w', 'delta_b_mod': 'delta_w', 'delta_g_mix_pre': 'delta_w', 'delta_g_mix_post': 'delta_w', 'delta_w_qkv': 'delta_w', 'delta_w_o': 'delta_w', 'delta_w_fg': 'delta_w', 'delta_b_fg': 'delta_w', 'delta_g_ffn_pre': 'delta_w', 'delta_g_ffn_post': 'delta_w', 'delta_w_ffn_gate': 'delta_w', 'delta_w_ffn_up': 'delta_w', 'delta_w_conv': 'delta_w', 'delta_b_conv': 'delta_w', 'delta_w_ffn_down': 'delta_w', 'new_m_w_mod': 'new_m', 'new_m_b_mod': 'new_m', 'new_m_g_mix_pre': 'new_m', 'new_m_g_mix_post': 'new_m', 'new_m_w_qkv': 'new_m', 'new_m_w_o': 'new_m', 'new_m_w_fg': 'new_m', 'new_m_b_fg': 'new_m', 'new_m_g_ffn_pre': 'new_m', 'new_m_g_ffn_post': 'new_m', 'new_m_w_ffn_gate': 'new_m', 'new_m_w_ffn_up': 'new_m', 'new_m_w_conv': 'new_m', 'new_m_b_conv': 'new_m', 'new_m_w_ffn_down': 'new_m', 'new_v_w_mod': 'new_v', 'new_v_b_mod': 'new_v', 'new_v_g_mix_pre': 'new_v', 'new_v_g_mix_post': 'new_v', 'new_v_w_qkv': 'new_v', 'new_v_w_o': 'new_v', 'new_v_w_fg': 'new_v', 'new_v_b_fg': 'new_v', 'new_v_g_ffn_pre': 'new_v', 'new_v_g_ffn_post': 'new_v', 'new_v_w_ffn_gate': 'new_v', 'new_v_w_ffn_up': 'new_v', 'new_v_w_conv': 'new_v', 'new_v_b_conv': 'new_v', 'new_v_w_ffn_down': 'new_v'}


def _forward(args):
    return _fwd_reference(*[args[k] for k in FWD_PARAMS])


def _output_shape():
    out = _jax.eval_shape(lambda: _forward(_fwd_setup_inputs(0)))
    return out.shape, out.dtype

N_MICROBATCH = 1
ADAM_LR = 0.001
ADAM_B1 = 0.9
ADAM_B2 = 0.999
ADAM_EPS = 1e-08
ADAM_WD = 0.01
ADAM_STEP = 10
PER_EXAMPLE_BATCH_AXIS = {'x': 0, 'c': 0, 'loss_target': 0}
SHARED_INPUTS = []
_WEIGHT_DTYPES = {'w_mod': _jnp.float32, 'b_mod': _jnp.float32, 'g_mix_pre': _jnp.float32, 'g_mix_post': _jnp.float32, 'w_qkv': _jnp.float32, 'w_o': _jnp.float32, 'w_fg': _jnp.float32, 'b_fg': _jnp.float32, 'g_ffn_pre': _jnp.float32, 'g_ffn_post': _jnp.float32, 'w_ffn_gate': _jnp.float32, 'w_ffn_up': _jnp.float32, 'w_conv': _jnp.float32, 'b_conv': _jnp.float32, 'w_ffn_down': _jnp.float32}
MOMENT_SCALE = {'w_mod': 2.506282e+00, 'b_mod': 4.397176e+00, 'g_mix_pre': 6.381730e-01, 'g_mix_post': 4.524201e+00, 'w_qkv': 8.518018e-01, 'w_o': 1.565268e+00, 'w_fg': 2.759278e-01, 'b_fg': 6.290194e-01, 'g_ffn_pre': 2.479683e-01, 'g_ffn_post': 3.784406e+00, 'w_ffn_gate': 1.218804e-01, 'w_ffn_up': 1.969455e-01, 'w_conv': 1.711247e-01, 'b_conv': 3.317415e-01, 'w_ffn_down': 3.294783e-01}


def _to_microbatches(a, axis):
    t = _jnp.moveaxis(a, axis, 0)
    t = t.reshape((N_MICROBATCH, t.shape[0] // N_MICROBATCH) + t.shape[1:])
    return _jnp.moveaxis(t, 1, axis + 1)


def setup_inputs(seed: int = 0) -> dict:
    inp = _fwd_setup_inputs(seed)
    key = _jax.random.fold_in(_jax.random.key(seed), 7919)
    shape, _ = _output_shape()
    out = dict(inp)
    out["loss_target"] = _jax.random.normal(_jax.random.fold_in(key, 0), shape, _jnp.float32)
    for i, name in enumerate(TWIN_WEIGHTS):
        w = inp[name].astype(_jnp.float32)
        if MOMENT_SCALE is None:
            s = _jnp.sqrt(_jnp.mean(_jnp.square(w)) + 1e-30)
        else:
            s = MOMENT_SCALE[name]
        km, kv = _jax.random.split(_jax.random.fold_in(key, i + 1))
        out[name] = w
        out["m_" + name] = s * _jax.random.normal(km, w.shape, _jnp.float32)
        out["v_" + name] = (s * s) * _jax.random.uniform(kv, w.shape, _jnp.float32, 0.5, 1.5)
    if N_MICROBATCH > 1:
        for name, axis in PER_EXAMPLE_BATCH_AXIS.items():
            out[name] = _to_microbatches(out[name], axis)
    return {'x': out['x'], 'c': out['c'], 'w_mod': out['w_mod'], 'b_mod': out['b_mod'], 'g_mix_pre': out['g_mix_pre'], 'g_mix_post': out['g_mix_post'], 'w_qkv': out['w_qkv'], 'w_o': out['w_o'], 'w_fg': out['w_fg'], 'b_fg': out['b_fg'], 'g_ffn_pre': out['g_ffn_pre'], 'g_ffn_post': out['g_ffn_post'], 'w_ffn_gate': out['w_ffn_gate'], 'w_ffn_up': out['w_ffn_up'], 'w_conv': out['w_conv'], 'b_conv': out['b_conv'], 'w_ffn_down': out['w_ffn_down'], 'loss_target': out['loss_target'], 'm_w_mod': out['m_w_mod'], 'm_b_mod': out['m_b_mod'], 'm_g_mix_pre': out['m_g_mix_pre'], 'm_g_mix_post': out['m_g_mix_post'], 'm_w_qkv': out['m_w_qkv'], 'm_w_o': out['m_w_o'], 'm_w_fg': out['m_w_fg'], 'm_b_fg': out['m_b_fg'], 'm_g_ffn_pre': out['m_g_ffn_pre'], 'm_g_ffn_post': out['m_g_ffn_post'], 'm_w_ffn_gate': out['m_w_ffn_gate'], 'm_w_ffn_up': out['m_w_ffn_up'], 'm_w_conv': out['m_w_conv'], 'm_b_conv': out['m_b_conv'], 'm_w_ffn_down': out['m_w_ffn_down'], 'v_w_mod': out['v_w_mod'], 'v_b_mod': out['v_b_mod'], 'v_g_mix_pre': out['v_g_mix_pre'], 'v_g_mix_post': out['v_g_mix_post'], 'v_w_qkv': out['v_w_qkv'], 'v_w_o': out['v_w_o'], 'v_w_fg': out['v_w_fg'], 'v_b_fg': out['v_b_fg'], 'v_g_ffn_pre': out['v_g_ffn_pre'], 'v_g_ffn_post': out['v_g_ffn_post'], 'v_w_ffn_gate': out['v_w_ffn_gate'], 'v_w_ffn_up': out['v_w_ffn_up'], 'v_w_conv': out['v_w_conv'], 'v_b_conv': out['v_b_conv'], 'v_w_ffn_down': out['v_w_ffn_down']}


def _loss(weights, diff, rest, loss_target):
    with _jax.named_scope("forward"):
        args = {**rest, TWIN_DIFF_INPUT: diff, **{k: w.astype(_WEIGHT_DTYPES[k]) for k, w in weights.items()}}
        y = _forward(args)
    with _jax.named_scope("loss_head"):
        err = _jnp.square(y.astype(_jnp.float32) - loss_target)
        return 0.5 * _jnp.sum(_jnp.mean(err, axis=-1)) if err.ndim else 0.5 * err


def _adamw(w, g, m, v):
    m = ADAM_B1 * m + (1.0 - ADAM_B1) * g
    v = ADAM_B2 * v + (1.0 - ADAM_B2) * _jnp.square(g)
    m_hat = m / (1.0 - ADAM_B1 ** ADAM_STEP)
    v_hat = v / (1.0 - ADAM_B2 ** ADAM_STEP)
    delta = -ADAM_LR * (m_hat / (_jnp.sqrt(v_hat) + ADAM_EPS) + ADAM_WD * w)
    return delta, m, v


def reference(x, c, w_mod, b_mod, g_mix_pre, g_mix_post, w_qkv, w_o, w_fg, b_fg, g_ffn_pre, g_ffn_post, w_ffn_gate, w_ffn_up, w_conv, b_conv, w_ffn_down, loss_target, m_w_mod, m_b_mod, m_g_mix_pre, m_g_mix_post, m_w_qkv, m_w_o, m_w_fg, m_b_fg, m_g_ffn_pre, m_g_ffn_post, m_w_ffn_gate, m_w_ffn_up, m_w_conv, m_b_conv, m_w_ffn_down, v_w_mod, v_b_mod, v_g_mix_pre, v_g_mix_post, v_w_qkv, v_w_o, v_w_fg, v_b_fg, v_g_ffn_pre, v_g_ffn_post, v_w_ffn_gate, v_w_ffn_up, v_w_conv, v_b_conv, v_w_ffn_down):
    given = dict(x=x, c=c, w_mod=w_mod, b_mod=b_mod, g_mix_pre=g_mix_pre, g_mix_post=g_mix_post, w_qkv=w_qkv, w_o=w_o, w_fg=w_fg, b_fg=b_fg, g_ffn_pre=g_ffn_pre, g_ffn_post=g_ffn_post, w_ffn_gate=w_ffn_gate, w_ffn_up=w_ffn_up, w_conv=w_conv, b_conv=b_conv, w_ffn_down=w_ffn_down, loss_target=loss_target, m_w_mod=m_w_mod, m_b_mod=m_b_mod, m_g_mix_pre=m_g_mix_pre, m_g_mix_post=m_g_mix_post, m_w_qkv=m_w_qkv, m_w_o=m_w_o, m_w_fg=m_w_fg, m_b_fg=m_b_fg, m_g_ffn_pre=m_g_ffn_pre, m_g_ffn_post=m_g_ffn_post, m_w_ffn_gate=m_w_ffn_gate, m_w_ffn_up=m_w_ffn_up, m_w_conv=m_w_conv, m_b_conv=m_b_conv, m_w_ffn_down=m_w_ffn_down, v_w_mod=v_w_mod, v_b_mod=v_b_mod, v_g_mix_pre=v_g_mix_pre, v_g_mix_post=v_g_mix_post, v_w_qkv=v_w_qkv, v_w_o=v_w_o, v_w_fg=v_w_fg, v_b_fg=v_b_fg, v_g_ffn_pre=v_g_ffn_pre, v_g_ffn_post=v_g_ffn_post, v_w_ffn_gate=v_w_ffn_gate, v_w_ffn_up=v_w_ffn_up, v_w_conv=v_w_conv, v_b_conv=v_b_conv, v_w_ffn_down=v_w_ffn_down)
    weights = {n: given[n] for n in TWIN_WEIGHTS}
    shared = {n: given[n] for n in SHARED_INPUTS}
    per_example = {n: given[n] for n in ['x', 'c']}
    grad_fn = _jax.value_and_grad(_loss, argnums=(0, 1))

    def one_microbatch(ex, loss_target):
        ex = dict(ex)
        diff = ex.pop(TWIN_DIFF_INPUT)
        return grad_fn(weights, diff, {**shared, **ex}, loss_target)

    if N_MICROBATCH == 1:
        loss, (grad_w, grad_x) = one_microbatch(per_example, given["loss_target"])
    else:
        def body(carry, xs):
            loss_sum, grad_sum = carry
            l_k, (gw_k, gx_k) = one_microbatch(xs[0], xs[1])
            with _jax.named_scope("update"):
                return (loss_sum + l_k, _jax.tree.map(_jnp.add, grad_sum, gw_k)), gx_k

        init = (_jnp.zeros((), _jnp.float32), _jax.tree.map(_jnp.zeros_like, weights))
        (loss, grad_w), grad_x = _jax.lax.scan(body, init, (per_example, given["loss_target"]))
    with _jax.named_scope("update"):
        delta_w, new_m, new_v = {}, {}, {}
        for n in TWIN_WEIGHTS:
            delta_w[n], new_m[n], new_v[n] = _adamw(weights[n], grad_w[n], given["m_" + n], given["v_" + n])
    return (loss, grad_x, *[grad_w[n] for n in TWIN_WEIGHTS], *[delta_w[n] for n in TWIN_WEIGHTS],
            *[new_m[n] for n in TWIN_WEIGHTS], *[new_v[n] for n in TWIN_WEIGHTS])
```

```python
import functools

import jax
import jax.numpy as jnp
from jax import lax
from jax.experimental import pallas as pl
from jax.experimental.pallas import tpu as pltpu

F32 = jnp.float32
BF16 = jnp.bfloat16

N_DEV = 8
HEAD_DIM = 64
LANES = 128
SUBLANES = 8
RMS_EPS = 1e-6
ADAM_LR = 0.001
ADAM_B1 = 0.9
ADAM_B2 = 0.999
ADAM_EPS = 1e-08
ADAM_WD = 0.01
ADAM_STEP = 10
VMEM_LIMIT_BYTES = 56 * 1024 * 1024

NN = (((1,), (0,)), ((), ()))
NT = (((1,), (1,)), ((), ()))
TN = (((0,), (0,)), ((), ()))
MESH = pl.DeviceIdType.MESH


def _params(*sem):
    return pltpu.CompilerParams(dimension_semantics=sem, vmem_limit_bytes=VMEM_LIMIT_BYTES)


def _pick(n, pref, quantum):
    if n <= pref:
        return n
    t = (pref // quantum) * quantum
    while n % t:
        t -= quantum
    return t


def _mm(a, b, *, dims, grid, a_spec, b_spec, o_spec, out_shape, name, add=None):
    nk = grid[2]
    acc_shape = tuple(d for d in o_spec.block_shape if d is not None)
    o_dtype = out_shape.dtype

    def body(*refs):
        if add is None:
            a_ref, b_ref, o_ref, acc_ref = refs
            add_ref = None
        else:
            a_ref, b_ref, add_ref, o_ref, acc_ref = refs
        k = pl.program_id(2)
        part = lax.dot_general(a_ref[...], b_ref[...], dims, preferred_element_type=F32)

        def finish(total):
            if add_ref is not None:
                total = total + add_ref[...]
            o_ref[...] = total.astype(o_dtype)

        if nk == 1:
            finish(part)
            return

        @pl.when(k == 0)
        def _():
            acc_ref[...] = part

        @pl.when(jnp.logical_and(k > 0, k < nk - 1))
        def _():
            acc_ref[...] += part

        @pl.when(k == nk - 1)
        def _():
            finish(acc_ref[...] + part)

    operands = [a, b] if add is None else [a, b, add]
    in_specs = [a_spec, b_spec] if add is None else [a_spec, b_spec, o_spec]
    return pl.pallas_call(
        body, name=name, grid=grid, in_specs=in_specs, out_specs=o_spec, out_shape=out_shape,
        scratch_shapes=[pltpu.VMEM(acc_shape, F32)],
        compiler_params=_params("parallel", "parallel", "arbitrary"),
    )(*operands)


def _mm_nn(a, w, l, out_dtype, name, *, col0=0, n=None, split_out=None):
    m, kdim = a.shape
    n = w.shape[2] if n is None else n
    tm, tk = _pick(m, 512, SUBLANES), _pick(kdim, 1024, LANES)
    tn = _pick(n if split_out is None else split_out, 1024, LANES)
    jb = col0 // tn
    grid = (m // tm, n // tn, kdim // tk)
    a_spec = pl.BlockSpec((tm, tk), lambda i, j, k: (i, k))
    b_spec = pl.BlockSpec((None, tk, tn), lambda i, j, k: (l, k, j + jb))
    if split_out is None:
        o_spec = pl.BlockSpec((tm, tn), lambda i, j, k: (i, j))
        shape = jax.ShapeDtypeStruct((m, n), out_dtype)
    else:
        nj1 = split_out // tn
        o_spec = pl.BlockSpec((None, tm, tn), lambda i, j, k: (j // nj1, i, j % nj1))
        shape = jax.ShapeDtypeStruct((n // split_out, m, split_out), out_dtype)
    return _mm(a, w, dims=NN, grid=grid, a_spec=a_spec, b_spec=b_spec, o_spec=o_spec, out_shape=shape, name=name)


def _mm_nt(a, w, l, out_dtype, name, *, add=None):
    n, kdim = w.shape[1], w.shape[2]
    if a.ndim == 2:
        m = a.shape[0]
        tk = _pick(kdim, 1024, LANES)
        a_spec_of = lambda tm: pl.BlockSpec((tm, tk), lambda i, j, k: (i, k))
    else:
        m, seg = a.shape[1], a.shape[2]
        tk = _pick(seg, 1024, LANES)
        nk1 = seg // tk
        a_spec_of = lambda tm: pl.BlockSpec((None, tm, tk), lambda i, j, k: (k // nk1, i, k % nk1))
    tm, tn = _pick(m, 512, SUBLANES), _pick(n, 1024, LANES)
    grid = (m // tm, n // tn, kdim // tk)
    b_spec = pl.BlockSpec((None, tn, tk), lambda i, j, k: (l, j, k))
    o_spec = pl.BlockSpec((tm, tn), lambda i, j, k: (i, j))
    return _mm(a, w, dims=NT, grid=grid, a_spec=a_spec_of(tm), b_spec=b_spec, o_spec=o_spec,
               out_shape=jax.ShapeDtypeStruct((m, n), out_dtype), name=name, add=add)


def _mm_tn(a, b, out_dtype, name):
    kdim, m = a.shape
    tk, tm = _pick(kdim, 1024, SUBLANES), _pick(m, 512, LANES)
    if b.ndim == 2:
        n = b.shape[1]
        tn = _pick(n, 1024, LANES)
        b_spec = pl.BlockSpec((tk, tn), lambda i, j, k: (k, j))
    else:
        seg = b.shape[2]
        n = b.shape[0] * seg
        tn = _pick(seg, 1024, LANES)
        nj1 = seg // tn
        b_spec = pl.BlockSpec((None, tk, tn), lambda i, j, k: (j // nj1, k, j % nj1))
    grid = (m // tm, n // tn, kdim // tk)
    a_spec = pl.BlockSpec((tk, tm), lambda i, j, k: (k, i))
    o_spec = pl.BlockSpec((tm, tn), lambda i, j, k: (i, j))
    return _mm(a, b, dims=TN, grid=grid, a_spec=a_spec, b_spec=b_spec, o_spec=o_spec,
               out_shape=jax.ShapeDtypeStruct((m, n), out_dtype), name=name)


def _rstd(x):
    return lax.rsqrt(jnp.mean(x * x, axis=-1, keepdims=True) + RMS_EPS)


def _norm_mod_fwd(x, g, shift, scale, name):
    s, d = x.shape
    tm = _pick(s, 512, SUBLANES)

    def body(x_ref, g_ref, sh_ref, sc_ref, h_ref):
        xv = x_ref[...]
        y = (xv * _rstd(xv)) * g_ref[...]
        h_ref[...] = (y * (1.0 + sc_ref[...]) + sh_ref[...]).astype(BF16)

    row = pl.BlockSpec((tm, d), lambda i: (i, 0))
    vec = pl.BlockSpec((1, d), lambda i: (0, 0))
    return pl.pallas_call(body, name=name, grid=(s // tm,), in_specs=[row, vec, vec, vec], out_specs=row,
                          out_shape=jax.ShapeDtypeStruct((s, d), BF16), compiler_params=_params("parallel"))(x, g, shift, scale)


def _norm_mod_bwd(dh, x, g, scale, dres, name):
    s, d = x.shape
    tm = _pick(s, 256, SUBLANES)

    def body(dh_ref, x_ref, g_ref, sc_ref, dres_ref, dx_ref, dsh_ref, dsc_ref, dg_ref):
        i = pl.program_id(0)
        xv, dhv, gv = x_ref[...], dh_ref[...], g_ref[...]
        r = _rstd(xv)
        xh = xv * r
        dn = dhv * (1.0 + sc_ref[...])
        gd = dn * gv
        dx_ref[...] = dres_ref[...] + r * (gd - xh * jnp.mean(gd * xh, axis=-1, keepdims=True))

        @pl.when(i == 0)
        def _():
            dsh_ref[...] = jnp.zeros_like(dsh_ref)
            dsc_ref[...] = jnp.zeros_like(dsc_ref)
            dg_ref[...] = jnp.zeros_like(dg_ref)

        dsh_ref[...] += jnp.sum(dhv, axis=0, keepdims=True)
        dsc_ref[...] += jnp.sum(dhv * (xh * gv), axis=0, keepdims=True)
        dg_ref[...] += jnp.sum(dn * xh, axis=0, keepdims=True)

    row = pl.BlockSpec((tm, d), lambda i: (i, 0))
    vec = pl.BlockSpec((1, d), lambda i: (0, 0))
    vshape = jax.ShapeDtypeStruct((1, d), F32)
    return pl.pallas_call(body, name=name, grid=(s // tm,), in_specs=[row, row, vec, vec, row],
                          out_specs=[row, vec, vec, vec],
                          out_shape=[jax.ShapeDtypeStruct((s, d), F32), vshape, vshape, vshape],
                          compiler_params=_params("arbitrary"))(dh, x, g, scale, dres)


def _post_fwd(x, u, g, gate, name):
    s, d = x.shape
    tm = _pick(s, 512, SUBLANES)

    def body(x_ref, u_ref, g_ref, gt_ref, o_ref):
        uv = u_ref[...]
        o_ref[...] = x_ref[...] + gt_ref[...] * ((uv * _rstd(uv)) * g_ref[...])

    row = pl.BlockSpec((tm, d), lambda i: (i, 0))
    vec = pl.BlockSpec((1, d), lambda i: (0, 0))
    return pl.pallas_call(body, name=name, grid=(s // tm,), in_specs=[row, row, vec, vec], out_specs=row,
                          out_shape=jax.ShapeDtypeStruct((s, d), F32), compiler_params=_params("parallel"))(x, u, g, gate)


def _post_bwd(dx, u, g, gate, name):
    s, d = u.shape
    tm = _pick(s, 256, SUBLANES)

    def body(dx_ref, u_ref, g_ref, gt_ref, du_ref, dgt_ref, dg_ref):
        i = pl.program_id(0)
        uv, dxv, gv = u_ref[...], dx_ref[...], g_ref[...]
        r = _rstd(uv)
        uh = uv * r
        dn = dxv * gt_ref[...]
        gd = dn * gv
        du_ref[...] = (r * (gd - uh * jnp.mean(gd * uh, axis=-1, keepdims=True))).astype(BF16)

        @pl.when(i == 0)
        def _():
            dgt_ref[...] = jnp.zeros_like(dgt_ref)
            dg_ref[...] = jnp.zeros_like(dg_ref)

        dgt_ref[...] += jnp.sum(dxv * (uh * gv), axis=0, keepdims=True)
        dg_ref[...] += jnp.sum(dn * uh, axis=0, keepdims=True)

    row = pl.BlockSpec((tm, d), lambda i: (i, 0))
    vec = pl.BlockSpec((1, d), lambda i: (0, 0))
    vshape = jax.ShapeDtypeStruct((1, d), F32)
    return pl.pallas_call(body, name=name, grid=(s // tm,), in_specs=[row, row, vec, vec],
                          out_specs=[row, vec, vec],
                          out_shape=[jax.ShapeDtypeStruct((s, d), BF16), vshape, vshape],
                          compiler_params=_params("arbitrary"))(dx, u, g, gate)


def _shift_rows(cur, prev8, k):
    rolled = pltpu.roll(cur, k, axis=0)
    rolled_prev = pltpu.roll(prev8, k, axis=0)
    i8 = lax.broadcasted_iota(jnp.int32, prev8.shape, 0)
    top = jnp.where(i8 < k, rolled_prev, rolled[:SUBLANES])
    return jnp.concatenate([top, rolled[SUBLANES:]], axis=0)


def _conv_pre(g, prev8, wc_ref, bc_ref):
    s1 = _shift_rows(g, prev8, 1)
    s2 = _shift_rows(g, prev8, 2)
    gc = bc_ref[...] + wc_ref[0:1, :] * s2 + wc_ref[1:2, :] * s1 + wc_ref[2:3, :] * g
    return gc, s1, s2


def _conv_act_fwd(gp, up, wc, bc, name):
    s, f = gp.shape
    tm, tf = _pick(s, 512, SUBLANES), _pick(f, 768, LANES)
    r8 = tm // SUBLANES

    def body(g_ref, gprev_ref, up_ref, wc_ref, bc_ref, a_ref):
        i = pl.program_id(1)
        prev = jnp.where(i == 0, 0.0, gprev_ref[...])
        gc, _, _ = _conv_pre(g_ref[...], prev, wc_ref, bc_ref)
        a_ref[...] = ((gc * jax.nn.sigmoid(gc)) * up_ref[...]).astype(BF16)

    tile = pl.BlockSpec((tm, tf), lambda j, i: (i, j))
    prev = pl.BlockSpec((SUBLANES, tf), lambda j, i: (jnp.maximum(i * r8 - 1, 0), j))
    return pl.pallas_call(body, name=name, grid=(f // tf, s // tm),
                          in_specs=[tile, prev, tile, pl.BlockSpec((3, tf), lambda j, i: (0, j)),
                                    pl.BlockSpec((1, tf), lambda j, i: (0, j))],
                          out_specs=tile, out_shape=jax.ShapeDtypeStruct((s, f), BF16),
                          compiler_params=_params("parallel", "parallel"))(gp, gp, up, wc, bc)


def _conv_act_bwd(da, gp, up, wc, bc, name):
    s, f = gp.shape
    tm, tf = _pick(s, 256, SUBLANES), _pick(f, 768, LANES)
    r8 = tm // SUBLANES
    nrow = s // tm

    def body(da_ref, dan_ref, g_ref, gprev_ref, gn_ref, up_ref, upn_ref, wc_ref, bc_ref,
             dgp_ref, dup_ref, dwc_ref, dbc_ref):
        i = pl.program_id(1)
        last = i == nrow - 1
        prev = jnp.where(i == 0, 0.0, gprev_ref[...])
        g_ext = jnp.concatenate([g_ref[...], gn_ref[...]], axis=0)
        up_ext = jnp.concatenate([up_ref[...], upn_ref[...]], axis=0)
        da_ext = jnp.concatenate([da_ref[...], jnp.where(last, 0.0, dan_ref[...])], axis=0)
        gc, s1, s2 = _conv_pre(g_ext, prev, wc_ref, bc_ref)
        sg = jax.nn.sigmoid(gc)
        dup_ref[...] = (da_ext * (gc * sg))[:tm].astype(BF16)
        dgc = da_ext * up_ext * (sg * (1.0 + gc * (1.0 - sg)))
        ext = tm + SUBLANES
        dgp = (wc_ref[2:3, :] * dgc + wc_ref[1:2, :] * pltpu.roll(dgc, ext - 1, axis=0)
               + wc_ref[0:1, :] * pltpu.roll(dgc, ext - 2, axis=0))
        dgp_ref[...] = dgp[:tm].astype(BF16)

        @pl.when(i == 0)
        def _():
            dwc_ref[...] = jnp.zeros_like(dwc_ref)
            dbc_ref[...] = jnp.zeros_like(dbc_ref)

        d0 = dgc[:tm]
        dwc_ref[0:1, :] += jnp.sum(d0 * s2[:tm], axis=0, keepdims=True)
        dwc_ref[1:2, :] += jnp.sum(d0 * s1[:tm], axis=0, keepdims=True)
        dwc_ref[2:3, :] += jnp.sum(d0 * g_ext[:tm], axis=0, keepdims=True)
        dbc_ref[...] += jnp.sum(d0, axis=0, keepdims=True)

    tile = pl.BlockSpec((tm, tf), lambda j, i: (i, j))
    prev = pl.BlockSpec((SUBLANES, tf), lambda j, i: (jnp.maximum(i * r8 - 1, 0), j))
    nxt = pl.BlockSpec((SUBLANES, tf), lambda j, i: (jnp.minimum((i + 1) * r8, s // SUBLANES - 1), j))
    return pl.pallas_call(body, name=name, grid=(f // tf, nrow),
                          in_specs=[tile, nxt, tile, prev, nxt, tile, nxt,
                                    pl.BlockSpec((3, tf), lambda j, i: (0, j)), pl.BlockSpec((1, tf), lambda j, i: (0, j))],
                          out_specs=[tile, tile, pl.BlockSpec((3, tf), lambda j, i: (0, j)),
                                     pl.BlockSpec((1, tf), lambda j, i: (0, j))],
                          out_shape=[jax.ShapeDtypeStruct((s, f), BF16), jax.ShapeDtypeStruct((s, f), BF16),
                                     jax.ShapeDtypeStruct((3, f), F32), jax.ShapeDtypeStruct((1, f), F32)],
                          compiler_params=_params("parallel", "arbitrary"))(da, da, gp, gp, gp, up, up, wc, bc)


def _split_bf16(v, parts):
    out, rem = [], v
    for _ in range(parts):
        t = rem.astype(BF16)
        out.append(t)
        rem = rem - t.astype(F32)
    return jnp.concatenate(out, axis=1)


def _tri(t, cmp, reps):
    row = lax.broadcasted_iota(jnp.int32, (t, t), 0)
    col = lax.broadcasted_iota(jnp.int32, (t, t), 1)
    m = cmp(row, col).astype(BF16)
    return jnp.concatenate([m] * reps, axis=0)


def _logsig_pair(z):
    e = jnp.exp(-jnp.abs(z))
    lp = jnp.log1p(e)
    return jnp.minimum(z, 0.0) - lp, jnp.minimum(-z, 0.0) - lp, e


def _attn_specs(s, t):
    q_spec = pl.BlockSpec((None, t, LANES), lambda hp, qi: (0, qi, hp))
    k_spec = pl.BlockSpec((None, s, LANES), lambda hp, qi: (1, 0, hp))
    v_spec = pl.BlockSpec((None, s, LANES), lambda hp, qi: (2, 0, hp))
    return q_spec, k_spec, v_spec


def _sb_fwd(qkv, t, name):
    _, s, d = qkv.shape
    hp_n, nq = d // LANES, s // t
    scale = HEAD_DIM ** -0.5

    def body(q_ref, k_ref, v_ref, o_ref, lt_ref):
        qi = pl.program_id(1)
        lane = lax.broadcasted_iota(jnp.int32, (t, LANES), 1)
        row = lax.broadcasted_iota(jnp.int32, (t, t), 0)
        col = lax.broadcasted_iota(jnp.int32, (t, t), 1)
        strict = col < row
        after = _tri(t, lambda j, sidx: j > sidx, 2)
        q = q_ref[...]
        outs, tots = [], []
        for h in range(2):
            hm = (lane < HEAD_DIM) if h == 0 else (lane >= HEAD_DIM)
            qh = jnp.where(hm, q, jnp.zeros_like(q))

            def block(kb, carry, masked):
                acc, run = carry
                ks = pl.multiple_of(kb * t, t)
                kblk, vblk = k_ref[pl.ds(ks, t), :], v_ref[pl.ds(ks, t), :]
                z = lax.dot_general(qh, kblk, NT, preferred_element_type=F32) * scale
                lb, l1, _ = _logsig_pair(z)
                if masked:
                    l1 = jnp.where(strict, l1, 0.0)
                c = jnp.dot(_split_bf16(l1, 2), after, preferred_element_type=F32)
                a = jnp.exp(lb + c + run)
                if masked:
                    a = jnp.where(strict, a, 0.0)
                acc = acc + jnp.dot(a.astype(BF16), vblk, preferred_element_type=F32)
                return acc, run + jnp.sum(l1, axis=1, keepdims=True)

            carry = block(qi, (jnp.zeros((t, LANES), F32), jnp.zeros((t, 1), F32)), True)
            carry = lax.fori_loop(0, qi, lambda j, cr: block(qi - 1 - j, cr, False), carry)
            outs.append(carry[0])
            tots.append(carry[1])
        o_ref[...] = jnp.where(lane < HEAD_DIM, outs[0], outs[1]).astype(BF16)
        lt_ref[...] = jnp.where(lane < HEAD_DIM, tots[0], tots[1])

    q_spec, k_spec, v_spec = _attn_specs(s, t)
    return pl.pallas_call(
        body, name=name, grid=(hp_n, nq), in_specs=[q_spec, k_spec, v_spec],
        out_specs=[pl.BlockSpec((t, LANES), lambda hp, qi: (qi, hp)),
                   pl.BlockSpec((None, t, LANES), lambda hp, qi: (hp, qi, 0))],
        out_shape=[jax.ShapeDtypeStruct((s, d), BF16), jax.ShapeDtypeStruct((hp_n, s, LANES), F32)],
        compiler_params=_params("parallel", "parallel"))(qkv, qkv, qkv)


def _sb_bwd(qkv, ltot, do, t, name):
    _, s, d = qkv.shape
    hp_n, nq = d // LANES, s // t
    scale = HEAD_DIM ** -0.5

    def body(q_ref, k_ref, v_ref, lt_ref, do_ref, dqkv_ref, dk_acc, dv_acc):
        qi = pl.program_id(1)

        @pl.when(qi == 0)
        def _():
            dk_acc[...] = jnp.zeros_like(dk_acc)
            dv_acc[...] = jnp.zeros_like(dv_acc)

        lane = lax.broadcasted_iota(jnp.int32, (t, LANES), 1)
        row = lax.broadcasted_iota(jnp.int32, (t, t), 0)
        col = lax.broadcasted_iota(jnp.int32, (t, t), 1)
        strict = col < row
        after = _tri(t, lambda j, sidx: j > sidx, 2)
        before = _tri(t, lambda j, sidx: j < sidx, 3)
        q, dov, ltv = q_ref[...], do_ref[...], lt_ref[...]
        dq_tot = jnp.zeros((t, LANES), F32)
        for h in range(2):
            hm = (lane < HEAD_DIM) if h == 0 else (lane >= HEAD_DIM)
            qh = jnp.where(hm, q, jnp.zeros_like(q))
            doh = jnp.where(hm, dov, jnp.zeros_like(dov))
            lt = ltv[:, h * HEAD_DIM:h * HEAD_DIM + 1]

            def block(kb, carry, masked):
                dq, lpre, gpre = carry
                ks = pl.multiple_of(kb * t, t)
                kblk, vblk = k_ref[pl.ds(ks, t), :], v_ref[pl.ds(ks, t), :]
                kh = jnp.where(hm, kblk, jnp.zeros_like(kblk))
                z = lax.dot_general(qh, kblk, NT, preferred_element_type=F32) * scale
                lb, l1, e = _logsig_pair(z)
                if masked:
                    l1 = jnp.where(strict, l1, 0.0)
                lpre = lpre + jnp.sum(l1, axis=1, keepdims=True)
                c = jnp.dot(_split_bf16(l1, 2), after, preferred_element_type=F32)
                a = jnp.exp(lb + c + (lt - lpre))
                if masked:
                    a = jnp.where(strict, a, 0.0)
                da = lax.dot_general(doh, vblk, NT, preferred_element_type=F32)
                g = da * a
                p = jnp.dot(_split_bf16(g, 3), before, preferred_element_type=F32) + gpre
                inv = 1.0 / (1.0 + e)
                pos = z >= 0.0
                beta = jnp.where(pos, 1.0, e) * inv
                one_m_beta = jnp.where(pos, e, 1.0) * inv
                dz = g * one_m_beta - p * beta
                if masked:
                    dz = jnp.where(strict, dz, 0.0)
                dzs = (dz * scale).astype(BF16)
                dq = dq + jnp.dot(dzs, kh, preferred_element_type=F32)
                dk_acc[pl.ds(ks, t), :] += lax.dot_general(dzs, qh, TN, preferred_element_type=F32)
                dv_acc[pl.ds(ks, t), :] += lax.dot_general(a.astype(BF16), doh, TN, preferred_element_type=F32)
                return dq, lpre, gpre + jnp.sum(g, axis=1, keepdims=True)

            zero1 = jnp.zeros((t, 1), F32)
            carry = lax.fori_loop(0, qi, lambda j, cr: block(j, cr, False), (jnp.zeros((t, LANES), F32), zero1, zero1))
            carry = block(qi, carry, True)
            dq_tot = dq_tot + carry[0]
        dqkv_ref[0, pl.ds(pl.multiple_of(qi * t, t), t), :] = dq_tot.astype(BF16)

        @pl.when(qi == nq - 1)
        def _():
            dqkv_ref[1] = dk_acc[...].astype(BF16)
            dqkv_ref[2] = dv_acc[...].astype(BF16)

    q_spec, k_spec, v_spec = _attn_specs(s, t)
    return pl.pallas_call(
        body, name=name, grid=(hp_n, nq),
        in_specs=[q_spec, k_spec, v_spec, pl.BlockSpec((None, t, LANES), lambda hp, qi: (hp, qi, 0)),
                  pl.BlockSpec((t, LANES), lambda hp, qi: (qi, hp))],
        out_specs=pl.BlockSpec((3, s, LANES), lambda hp, qi: (0, 0, hp)),
        out_shape=jax.ShapeDtypeStruct((3, s, d), BF16),
        scratch_shapes=[pltpu.VMEM((s, LANES), F32), pltpu.VMEM((s, LANES), F32)],
        compiler_params=_params("parallel", "arbitrary"))(qkv, qkv, qkv, ltot, do)


def _fox_prep(fl, bias, name):
    s, w = fl.shape
    tb = _pick(s, 512, SUBLANES)

    def body(fl_ref, b_ref, cum_ref, carry_ref):
        i = pl.program_id(0)

        @pl.when(i == 0)
        def _():
            carry_ref[...] = jnp.zeros_like(carry_ref)

        logf, _, _ = _logsig_pair(fl_ref[...] + b_ref[...])
        row = lax.broadcasted_iota(jnp.int32, (tb, tb), 0)
        col = lax.broadcasted_iota(jnp.int32, (tb, tb), 1)
        incl = (col <= row).astype(BF16)
        tot = carry_ref[...]
        rem = logf
        for _ in range(3):
            part = rem.astype(BF16)
            tot = tot + jnp.dot(incl, part, preferred_element_type=F32)
            rem = rem - part.astype(F32)
        cum_ref[...] = tot
        carry_ref[...] = tot[tb - 1:tb, :]

    blk = pl.BlockSpec((tb, w), lambda i: (i, 0))
    return pl.pallas_call(body, name=name, grid=(s // tb,), in_specs=[blk, pl.BlockSpec((1, w), lambda i: (0, 0))],
                          out_specs=blk, out_shape=jax.ShapeDtypeStruct((s, w), F32),
                          scratch_shapes=[pltpu.VMEM((1, w), F32)], compiler_params=_params("arbitrary"))(fl, bias)


def _fox_gate_bwd(dcum, fl, bias, n_heads, name):
    s, w = fl.shape
    tb = _pick(s, 512, SUBLANES)
    nb = s // tb

    def body(dc_ref, fl_ref, b_ref, dfl_ref, db_ref, carry_ref):
        i = pl.program_id(0)

        @pl.when(i == 0)
        def _():
            carry_ref[...] = jnp.zeros_like(carry_ref)
            db_ref[...] = jnp.zeros_like(db_ref)

        row = lax.broadcasted_iota(jnp.int32, (tb, tb), 0)
        col = lax.broadcasted_iota(jnp.int32, (tb, tb), 1)
        incl = (col >= row).astype(BF16)
        tot = jnp.broadcast_to(carry_ref[...], (tb, w))
        rem = dc_ref[...]
        for _ in range(3):
            part = rem.astype(BF16)
            tot = tot + jnp.dot(incl, part, preferred_element_type=F32)
            rem = rem - part.astype(F32)
        carry_ref[...] = tot[0:1, :]
        xg = fl_ref[...] + b_ref[...]
        e = jnp.exp(-jnp.abs(xg))
        sig_neg = jnp.where(xg >= 0.0, e, 1.0) / (1.0 + e)
        lane = lax.broadcasted_iota(jnp.int32, (tb, w), 1)
        dfl = jnp.where(lane < n_heads, tot * sig_neg, 0.0)
        dfl_ref[...] = dfl.astype(BF16)
        db_ref[...] += jnp.sum(dfl, axis=0, keepdims=True)

    blk = pl.BlockSpec((tb, w), lambda i: (nb - 1 - i, 0))
    vec = pl.BlockSpec((1, w), lambda i: (0, 0))
    return pl.pallas_call(body, name=name, grid=(nb,), in_specs=[blk, blk, vec], out_specs=[blk, vec],
                          out_shape=[jax.ShapeDtypeStruct((s, w), BF16), jax.ShapeDtypeStruct((1, w), F32)],
                          scratch_shapes=[pltpu.VMEM((1, w), F32)], compiler_params=_params("arbitrary"))(dcum, fl, bias)


def _fox_fwd(qkv, cq, ck, t, name):
    _, s, d = qkv.shape
    hp_n, nq = d // LANES, s // t
    scale = HEAD_DIM ** -0.5

    def body(q_ref, k_ref, v_ref, cq_ref, ck_ref, o_ref, lse_ref):
        qi = pl.program_id(1)
        lane = lax.broadcasted_iota(jnp.int32, (t, LANES), 1)
        row = lax.broadcasted_iota(jnp.int32, (t, t), 0)
        col = lax.broadcasted_iota(jnp.int32, (t, t), 1)
        causal = col <= row
        q, cqv = q_ref[...], cq_ref[...]
        outs, lses = [], []
        for h in range(2):
            hm = (lane < HEAD_DIM) if h == 0 else (lane >= HEAD_DIM)
            qh = jnp.where(hm, q, jnp.zeros_like(q))
            cqh = cqv[:, h * HEAD_DIM:h * HEAD_DIM + 1]

            def block(kb, carry, masked):
                m, lsum, acc = carry
                ks = pl.multiple_of(kb * t, t)
                kblk, vblk = k_ref[pl.ds(ks, t), :], v_ref[pl.ds(ks, t), :]
                z = lax.dot_general(qh, kblk, NT, preferred_element_type=F32) * scale
                sc = z + (cqh - ck_ref[h:h + 1, pl.ds(ks, t)])
                if masked:
                    sc = jnp.where(causal, sc, -jnp.inf)
                m_new = jnp.maximum(m, jnp.max(sc, axis=1, keepdims=True))
                alpha = jnp.exp(m - m_new)
                p = jnp.exp(sc - m_new)
                lsum = alpha * lsum + jnp.sum(p, axis=1, keepdims=True)
                acc = alpha * acc + jnp.dot(p.astype(BF16), vblk, preferred_element_type=F32)
                return m_new, lsum, acc

            init = (jnp.full((t, 1), -jnp.inf, F32), jnp.zeros((t, 1), F32), jnp.zeros((t, LANES), F32))
            carry = block(qi, init, True)
            m, lsum, acc = lax.fori_loop(0, qi, lambda j, cr: block(qi - 1 - j, cr, False), carry)
            outs.append(acc / lsum)
            lses.append(m + jnp.log(lsum))
        o_ref[...] = jnp.where(lane < HEAD_DIM, outs[0], outs[1]).astype(BF16)
        lse_ref[...] = jnp.where(lane < HEAD_DIM, lses[0], lses[1])

    q_spec, k_spec, v_spec = _attn_specs(s, t)
    pair_rows = pl.BlockSpec((None, t, LANES), lambda hp, qi: (hp, qi, 0))
    return pl.pallas_call(
        body, name=name, grid=(hp_n, nq),
        in_specs=[q_spec, k_spec, v_spec, pair_rows, pl.BlockSpec((None, 2, s), lambda hp, qi: (hp, 0, 0))],
        out_specs=[pl.BlockSpec((t, LANES), lambda hp, qi: (qi, hp)), pair_rows],
        out_shape=[jax.ShapeDtypeStruct((s, d), BF16), jax.ShapeDtypeStruct((hp_n, s, LANES), F32)],
        compiler_params=_params("parallel", "parallel"))(qkv, qkv, qkv, cq, ck)


def _fox_bwd(qkv, o, do, lse, cq, ck, t, name):
    _, s, d = qkv.shape
    hp_n, nq = d // LANES, s // t
    scale = HEAD_DIM ** -0.5

    def body(q_ref, k_ref, v_ref, o_ref, do_ref, lse_ref, cq_ref, ck_ref, dqkv_ref, dcq_ref, dck_ref, dk_acc, dv_acc):
        qi = pl.program_id(1)

        @pl.when(qi == 0)
        def _():
            dk_acc[...] = jnp.zeros_like(dk_acc)
            dv_acc[...] = jnp.zeros_like(dv_acc)
            dck_ref[...] = jnp.zeros_like(dck_ref)

        lane = lax.broadcasted_iota(jnp.int32, (t, LANES), 1)
        row = lax.broadcasted_iota(jnp.int32, (t, t), 0)
        col = lax.broadcasted_iota(jnp.int32, (t, t), 1)
        causal = col <= row
        q, dov, cqv, lsev = q_ref[...], do_ref[...], cq_ref[...], lse_ref[...]
        prod = dov.astype(F32) * o_ref[...].astype(F32)
        dq_tot = jnp.zeros((t, LANES), F32)
        rowsums = []
        for h in range(2):
            hm = (lane < HEAD_DIM) if h == 0 else (lane >= HEAD_DIM)
            qh = jnp.where(hm, q, jnp.zeros_like(q))
            doh = jnp.where(hm, dov, jnp.zeros_like(dov))
            delta = jnp.sum(jnp.where(hm, prod, 0.0), axis=1, keepdims=True)
            cqh = cqv[:, h * HEAD_DIM:h * HEAD_DIM + 1]
            lseh = lsev[:, h * HEAD_DIM:h * HEAD_DIM + 1]

            def block(kb, carry, masked):
                dq, rowsum = carry
                ks = pl.multiple_of(kb * t, t)
                kblk, vblk = k_ref[pl.ds(ks, t), :], v_ref[pl.ds(ks, t), :]
                kh = jnp.where(hm, kblk, jnp.zeros_like(kblk))
                z = lax.dot_general(qh, kblk, NT, preferred_element_type=F32) * scale
                sc = z + (cqh - ck_ref[h:h + 1, pl.ds(ks, t)])
                p = jnp.exp(sc - lseh)
                if masked:
                    p = jnp.where(causal, p, 0.0)
                dp = lax.dot_general(doh, vblk, NT, preferred_element_type=F32)
                ds = p * (dp - delta)
                dss = (ds * scale).astype(BF16)
                dq = dq + jnp.dot(dss, kh, preferred_element_type=F32)
                dk_acc[pl.ds(ks, t), :] += lax.dot_general(dss, qh, TN, preferred_element_type=F32)
                dv_acc[pl.ds(ks, t), :] += lax.dot_general(p.astype(BF16), doh, TN, preferred_element_type=F32)
                dck_ref[h:h + 1, pl.ds(ks, t)] -= jnp.sum(ds, axis=0, keepdims=True)
                return dq, rowsum + jnp.sum(ds, axis=1, keepdims=True)

            init = (jnp.zeros((t, LANES), F32), jnp.zeros((t, 1), F32))
            dq, rowsum = block(qi, lax.fori_loop(0, qi, lambda j, cr: block(j, cr, False), init), True)
            dq_tot = dq_tot + dq
            rowsums.append(rowsum)
        dqkv_ref[0, pl.ds(pl.multiple_of(qi * t, t), t), :] = dq_tot.astype(BF16)
        dcq_ref[...] = jnp.where(lane < HEAD_DIM, rowsums[0], rowsums[1])

        @pl.when(qi == nq - 1)
        def _():
            dqkv_ref[1] = dk_acc[...].astype(BF16)
            dqkv_ref[2] = dv_acc[...].astype(BF16)

    q_spec, k_spec, v_spec = _attn_specs(s, t)
    pair_rows = pl.BlockSpec((None, t, LANES), lambda hp, qi: (hp, qi, 0))
    tile = pl.BlockSpec((t, LANES), lambda hp, qi: (qi, hp))
    keys = pl.BlockSpec((None, 2, s), lambda hp, qi: (hp, 0, 0))
    return pl.pallas_call(
        body, name=name, grid=(hp_n, nq),
        in_specs=[q_spec, k_spec, v_spec, tile, tile, pair_rows, pair_rows, keys],
        out_specs=[pl.BlockSpec((3, s, LANES), lambda hp, qi: (0, 0, hp)), pair_rows, keys],
        out_shape=[jax.ShapeDtypeStruct((3, s, d), BF16), jax.ShapeDtypeStruct((hp_n, s, LANES), F32),
                   jax.ShapeDtypeStruct((hp_n, 2, s), F32)],
        scratch_shapes=[pltpu.VMEM((s, LANES), F32), pltpu.VMEM((s, LANES), F32)],
        compiler_params=_params("parallel", "arbitrary"))(qkv, qkv, qkv, o, do, lse, cq, ck)


def _loss_head(y, target, name):
    s, d = y.shape
    tm = _pick(s, 512, SUBLANES)

    def body(y_ref, t_ref, dy_ref, sq_ref):
        i = pl.program_id(0)
        diff = y_ref[...] - t_ref[...]
        dy_ref[...] = diff / d

        @pl.when(i == 0)
        def _():
            sq_ref[...] = jnp.zeros_like(sq_ref)

        sq_ref[...] += jnp.sum(diff * diff, axis=0, keepdims=True)

    row = pl.BlockSpec((tm, d), lambda i: (i, 0))
    vec = pl.BlockSpec((1, d), lambda i: (0, 0))
    return pl.pallas_call(body, name=name, grid=(s // tm,), in_specs=[row, row], out_specs=[row, vec],
                          out_shape=[jax.ShapeDtypeStruct((s, d), F32), jax.ShapeDtypeStruct((1, d), F32)],
                          compiler_params=_params("arbitrary"))(y, target)


def _mod_fwd(c_all, w_mod, b_mod_cols, name):
    nl, d, cols = w_mod.shape
    nb = c_all.shape[0]

    def body(c_ref, w_ref, b_ref, o_ref):
        cv = c_ref[...]
        act = (cv * jax.nn.sigmoid(cv)).astype(BF16)
        o_ref[...] = jnp.dot(act, w_ref[...].astype(BF16), preferred_element_type=F32) + b_ref[...]

    return pl.pallas_call(
        body, name=name, grid=(nl,),
        in_specs=[pl.BlockSpec((nb, d), lambda l: (0, 0)), pl.BlockSpec((None, d, cols), lambda l: (l, 0, 0)),
                  pl.BlockSpec((None, 1, cols), lambda l: (l, 0, 0))],
        out_specs=pl.BlockSpec((None, nb, cols), lambda l: (l, 0, 0)),
        out_shape=jax.ShapeDtypeStruct((nl, nb, cols), F32), compiler_params=_params("parallel"))(c_all, w_mod, b_mod_cols)


def _wmod_grad(c_all_t, dmod, name):
    d, nb = c_all_t.shape
    _, nl, cols = dmod.shape

    def body(c_ref, dm_ref, o_ref):
        cv = c_ref[...]
        act = cv * jax.nn.sigmoid(cv)
        tot = act[:, 0:1] * dm_ref[0]
        for b in range(1, nb):
            tot = tot + act[:, b:b + 1] * dm_ref[b]
        o_ref[...] = tot

    return pl.pallas_call(
        body, name=name, grid=(nl,),
        in_specs=[pl.BlockSpec((d, nb), lambda l: (0, 0)), pl.BlockSpec((nb, None, 1, cols), lambda l: (0, l, 0, 0))],
        out_specs=pl.BlockSpec((None, d, cols), lambda l: (l, 0, 0)),
        out_shape=jax.ShapeDtypeStruct((nl, d, cols), F32), compiler_params=_params("parallel"))(
            c_all_t, dmod.reshape(nb, nl, 1, cols))


def _adamw(recv, w, m, v, name, *, tr=256, tc=512):
    nq, nl, rp, cp = recv.shape
    _, r, c = w.shape
    br = _pick(r, tr, SUBLANES) if rp == r else r
    bc = _pick(c, tc, LANES) if cp == c else c
    rbr = br if rp == r else rp
    rbc = bc if cp == c else cp

    def body(rv_ref, w_ref, m_ref, v_ref, g_ref, d_ref, nm_ref, nv_ref):
        g = rv_ref[0, :br, :bc].astype(F32)
        for qd in range(1, nq):
            g = g + rv_ref[qd, :br, :bc].astype(F32)
        m_new = ADAM_B1 * m_ref[...] + (1.0 - ADAM_B1) * g
        v_new = ADAM_B2 * v_ref[...] + (1.0 - ADAM_B2) * jnp.square(g)
        m_hat = m_new / (1.0 - ADAM_B1 ** ADAM_STEP)
        v_hat = v_new / (1.0 - ADAM_B2 ** ADAM_STEP)
        g_ref[...] = g
        d_ref[...] = -ADAM_LR * (m_hat / (jnp.sqrt(v_hat) + ADAM_EPS) + ADAM_WD * w_ref[...])
        nm_ref[...] = m_new
        nv_ref[...] = v_new

    blk = pl.BlockSpec((None, br, bc), lambda l, i, j: (l, i, j))
    rblk = pl.BlockSpec((nq, None, rbr, rbc), lambda l, i, j: (0, l, i, j))
    shape = jax.ShapeDtypeStruct(w.shape, F32)
    return pl.pallas_call(body, name=name, grid=(nl, r // br, c // bc), in_specs=[rblk, blk, blk, blk],
                          out_specs=[blk] * 4, out_shape=[shape] * 4,
                          compiler_params=_params("parallel", "parallel", "parallel"))(recv, w, m, v)


def _mesh_pos():
    return lax.axis_index("x"), lax.axis_index("y"), lax.axis_index("c")


def _allgather_small(block, name):
    m_per, n = block.shape

    def body(x_ref, out_ref, send_sems, recv_sems, local_sem):
        x, y, c = _mesh_pos()
        me, sibling = (x, y, c), (x, y, 1 - c)
        chips = [(1 - x, y), (x, 1 - y), (1 - x, 1 - y)]

        def rows(px, py, pc):
            return out_ref.at[pl.ds((4 * px + 2 * py + pc) * m_per, m_per), :]

        def copy(k, blk, to, src=None):
            return pltpu.make_async_remote_copy(
                src_ref=rows(*blk) if src is None else src, dst_ref=rows(*blk),
                send_sem=send_sems.at[k], recv_sem=recv_sems.at[k], device_id=to, device_id_type=MESH)

        mine = pltpu.make_async_copy(x_ref, rows(*me), local_sem)
        mine.start()
        first = [copy(0, me, sibling, src=x_ref)]
        first += [copy(1 + j, me, (*chip, c), src=x_ref) for j, chip in enumerate(chips)]
        for cp in first:
            cp.start()
        passed = [copy(4 + j, (*chip, c), sibling) for j, chip in enumerate(chips)]
        for j, chip in enumerate(chips):
            copy(1 + j, (*chip, c), me).wait_recv()
            passed[j].start()
        copy(0, sibling, me).wait_recv()
        for j, chip in enumerate(chips):
            copy(4 + j, (*chip, 1 - c), me).wait_recv()
        for cp in first + passed:
            cp.wait_send()
        mine.wait()

    return pl.pallas_call(
        body, name=name, out_shape=jax.ShapeDtypeStruct((N_DEV * m_per, n), block.dtype),
        in_specs=[pl.BlockSpec(memory_space=pltpu.VMEM)], out_specs=pl.BlockSpec(memory_space=pltpu.VMEM),
        scratch_shapes=[pltpu.SemaphoreType.DMA((7,)), pltpu.SemaphoreType.DMA((7,)), pltpu.SemaphoreType.DMA],
        compiler_params=pltpu.CompilerParams(vmem_limit_bytes=VMEM_LIMIT_BYTES))(block)


def _exchange(srcs, dst_shapes, jobs, name):
    n_src, n_job = len(srcs), len(jobs)

    def body(*refs):
        src_refs, dst_refs = refs[:n_src], refs[n_src:n_src + len(dst_shapes)]
        send_sems, recv_sems, local_sems = refs[n_src + len(dst_shapes):]
        x, y, c = _mesh_pos()
        me = 4 * x + 2 * y + c
        pending = []
        for t, (si, di, src_slice, dst_slice) in enumerate(jobs):
            src, dst = src_refs[si], dst_refs[di]
            lc = pltpu.make_async_copy(src_slice(src, me), dst_slice(dst, me), local_sems.at[t])
            lc.start()
            pending.append(lc)
            for dd in range(1, N_DEV):
                px = 1 - x if dd & 4 else x
                py = 1 - y if dd & 2 else y
                pc = 1 - c if dd & 1 else c
                cp = pltpu.make_async_remote_copy(
                    src_ref=src_slice(src, 4 * px + 2 * py + pc), dst_ref=dst_slice(dst, me),
                    send_sem=send_sems.at[t, dd - 1], recv_sem=recv_sems.at[t, dd - 1],
                    device_id=(px, py, pc), device_id_type=MESH)
                cp.start()
                pending.append(cp)
        for cp in pending:
            cp.wait()

    hbm = pl.BlockSpec(memory_space=pl.ANY)
    return pl.pallas_call(
        body, name=name, out_shape=list(dst_shapes), in_specs=[hbm] * n_src, out_specs=[hbm] * len(dst_shapes),
        scratch_shapes=[pltpu.SemaphoreType.DMA((n_job, N_DEV - 1)), pltpu.SemaphoreType.DMA((n_job, N_DEV - 1)),
                        pltpu.SemaphoreType.DMA((n_job,))])(*srcs)


def _whole(ref, p):
    return ref


def _cols_of(width):
    def take(ref, p):
        lead = (slice(None),) * (len(ref.shape) - 1)
        return ref.at[lead + (pl.ds(pl.multiple_of(p * width, LANES), width),)]
    return take


def _rows_of(height):
    def take(ref, p):
        lead = (slice(None),) * (len(ref.shape) - 2)
        return ref.at[lead + (pl.ds(pl.multiple_of(p * height, SUBLANES), height), slice(None))]
    return take


def _slot(layer=None):
    if layer is None:
        return lambda ref, p: ref.at[p]
    return lambda ref, p: ref.at[p, layer]


def _local_step(x0, target, mod, g_mix_pre, g_mix_post, g_ffn_pre, g_ffn_post, wqkv, wo, wg, wu, wd,
                wfg, bfg, wconv, bconv, t_attn):
    s, d = x0.shape
    nl = wqkv.shape[0]
    n_heads = d // HEAD_DIM
    hp_n = d // LANES
    vec = lambda a: a.reshape(1, -1)
    saved = []
    xcur = x0
    for l in range(nl):
        sh_a, sc_a, gt_a, sh_f, sc_f, gt_f = (vec(mod[l, i * d:(i + 1) * d]) for i in range(6))
        fox = l % 2 == 1
        h1 = _norm_mod_fwd(xcur, vec(g_mix_pre[l]), sh_a, sc_a, f"norm_mix_fwd_{l}")
        qkv = _mm_nn(h1, wqkv, l, BF16, f"qkv_fwd_{l}", split_out=d)
        if fox:
            j = l // 2
            fl = _mm_nn(h1, wfg, j, F32, f"fgate_fwd_{l}")
            cum = _fox_prep(fl, bfg[j], f"fox_prep_{l}")
            cum_h = cum[:, :n_heads]
            cq = jnp.repeat(cum_h.reshape(s, hp_n, 2).transpose(1, 0, 2), HEAD_DIM, axis=2)
            ck = cum_h.T.reshape(hp_n, 2, s)
            o, stat = _fox_fwd(qkv, cq, ck, t_attn, f"fox_fwd_{l}")
            extra = (fl, cq, ck)
        else:
            o, stat = _sb_fwd(qkv, t_attn, f"sb_fwd_{l}")
            extra = None
        u = _mm_nn(o, wo, l, F32, f"attn_out_fwd_{l}")
        x2 = _post_fwd(xcur, u, vec(g_mix_post[l]), gt_a, f"post_mix_fwd_{l}")
        h2 = _norm_mod_fwd(x2, vec(g_ffn_pre[l]), sh_f, sc_f, f"norm_ffn_fwd_{l}")
        gp = _mm_nn(h2, wg, l, F32, f"ffn_gate_fwd_{l}")
        up = _mm_nn(h2, wu, l, F32, f"ffn_up_fwd_{l}")
        act = _conv_act_fwd(gp, up, wconv[l], bconv[l], f"conv_act_fwd_{l}")
        yv = _mm_nn(act, wd, l, F32, f"ffn_down_fwd_{l}")
        x3 = _post_fwd(x2, yv, vec(g_ffn_post[l]), gt_f, f"post_ffn_fwd_{l}")
        saved.append((xcur, h1, qkv, o, stat, extra, u, x2, h2, gp, up, act, yv))
        xcur = x3

    dx, sq = _loss_head(xcur, target, "loss_head")
    small, big = [None] * nl, [None] * nl
    for l in reversed(range(nl)):
        xin, h1, qkv, o, stat, extra, u, x2, h2, gp, up, act, yv = saved[l]
        sc_a, gt_a, sc_f, gt_f = (vec(mod[l, i * d:(i + 1) * d]) for i in (1, 2, 4, 5))
        fox = l % 2 == 1
        dy, dgt_f, dg4 = _post_bwd(dx, yv, vec(g_ffn_post[l]), gt_f, f"post_ffn_bwd_{l}")
        dact = _mm_nt(dy, wd, l, F32, f"ffn_down_dx_{l}")
        dwd = _mm_tn(act, dy, BF16, f"ffn_down_dw_{l}")
        dgp, dup, dwc, dbc = _conv_act_bwd(dact, gp, up, wconv[l], bconv[l], f"conv_act_bwd_{l}")
        dh2 = _mm_nt(dgp, wg, l, F32, f"ffn_gate_dx_{l}")
        dh2 = _mm_nt(dup, wu, l, F32, f"ffn_up_dx_{l}", add=dh2)
        dwg = _mm_tn(h2, dgp, BF16, f"ffn_gate_dw_{l}")
        dwu = _mm_tn(h2, dup, BF16, f"ffn_up_dw_{l}")
        dx2, dsh_f, dsc_f, dg3 = _norm_mod_bwd(dh2, x2, vec(g_ffn_pre[l]), sc_f, dx, f"norm_ffn_bwd_{l}")
        du, dgt_a, dg2 = _post_bwd(dx2, u, vec(g_mix_post[l]), gt_a, f"post_mix_bwd_{l}")
        do = _mm_nt(du, wo, l, BF16, f"attn_out_dx_{l}")
        dwo = _mm_tn(o, du, BF16, f"attn_out_dw_{l}")
        if fox:
            j = l // 2
            fl, cq, ck = extra
            dqkv, dcq, dck = _fox_bwd(qkv, o, do, stat, cq, ck, t_attn, f"fox_bwd_{l}")
            dcum = dcq[:, :, ::HEAD_DIM].transpose(1, 0, 2).reshape(s, n_heads) + dck.reshape(n_heads, s).T
            dcum = jnp.pad(dcum, ((0, 0), (0, LANES - n_heads)))
            dfl, dbfg = _fox_gate_bwd(dcum, fl, bfg[j], n_heads, f"fox_gate_bwd_{l}")
            dh1 = _mm_nt(dfl, wfg, j, F32, f"fgate_dx_{l}")
            dh1 = _mm_nt(dqkv, wqkv, l, F32, f"qkv_dx_{l}", add=dh1)
            dwfg = _mm_tn(h1, dfl, F32, f"fgate_dw_{l}")[:, :n_heads]
            dbfg = dbfg[0, :n_heads]
        else:
            dqkv = _sb_bwd(qkv, stat, do, t_attn, f"sb_bwd_{l}")
            dh1 = _mm_nt(dqkv, wqkv, l, F32, f"qkv_dx_{l}")
            dwfg = dbfg = None
        dwqkv = _mm_tn(h1, dqkv, BF16, f"qkv_dw_{l}")
        dx, dsh_a, dsc_a, dg1 = _norm_mod_bwd(dh1, xin, vec(g_mix_pre[l]), sc_a, dx2, f"norm_mix_bwd_{l}")
        dmod = jnp.concatenate([dsh_a, dsc_a, dgt_a, dsh_f, dsc_f, dgt_f], axis=1)[0]
        small[l] = dict(dmod=dmod, dg1=dg1[0], dg2=dg2[0], dg3=dg3[0], dg4=dg4[0], dbc=dbc[0], dwc=dwc,
                        dbfg=dbfg, dwfg=dwfg)
        big[l] = dict(qkv=dwqkv, o=dwo, gate=dwg, up=dwu, down=dwd)
    return sq, dx, small, big


def _rows128(a, rows):
    flat = a.reshape(-1)
    return jnp.pad(flat, (0, rows * LANES - flat.shape[0])).reshape(rows, LANES)


def _ceil8(n_elems):
    rows = -(-n_elems // LANES)
    return -(-rows // SUBLANES) * SUBLANES


def kernel(x, c, w_mod, b_mod, g_mix_pre, g_mix_post, w_qkv, w_o, w_fg, b_fg, g_ffn_pre, g_ffn_post, w_ffn_gate, w_ffn_up, w_conv, b_conv, w_ffn_down, loss_target, m_w_mod, m_b_mod, m_g_mix_pre, m_g_mix_post, m_w_qkv, m_w_o, m_w_fg, m_b_fg, m_g_ffn_pre, m_g_ffn_post, m_w_ffn_gate, m_w_ffn_up, m_w_conv, m_b_conv, m_w_ffn_down, v_w_mod, v_b_mod, v_g_mix_pre, v_g_mix_post, v_w_qkv, v_w_o, v_w_fg, v_b_fg, v_g_ffn_pre, v_g_ffn_post, v_w_ffn_gate, v_w_ffn_up, v_w_conv, v_b_conv, v_w_ffn_down):
    _, s, d = x.shape
    nl = w_qkv.shape[0]
    nf = w_fg.shape[0]
    n_heads = w_fg.shape[2]
    fs = w_ffn_gate.shape[2]
    fp = -(-fs // LANES) * LANES
    f_full, f_pad = N_DEV * fs, N_DEV * fp
    mod_cols = w_mod.shape[2]
    qs, orows = w_qkv.shape[2], w_o.shape[1]
    xi, yi, ci = _mesh_pos()
    me = 4 * xi + 2 * yi + ci

    c_rows = d // LANES
    c_all = _allgather_small(jnp.pad(c.reshape(1, d), ((0, SUBLANES - 1), (0, 0))).reshape(SUBLANES * c_rows, LANES),
                             "gather_cond")
    c_all = c_all.reshape(N_DEV, SUBLANES, d)[:, 0, :]
    b_mod_cols = lax.dynamic_slice_in_dim(b_mod, me * mod_cols, mod_cols, axis=1).reshape(nl, 1, mod_cols)
    mod_part = _mod_fwd(c_all, w_mod, b_mod_cols, "mod_fwd")

    conv_pad = jnp.pad(w_conv, ((0, 0), (0, 0), (0, fp - fs)))
    r_mod, r_conv, r_fg = _ceil8(mod_part.size), _ceil8(conv_pad.size), _ceil8(w_fg.size)
    payload = jnp.concatenate([_rows128(mod_part, r_mod), _rows128(conv_pad, r_conv), _rows128(w_fg, r_fg)], axis=0)
    got = _allgather_small(payload, "gather_small_weights").reshape(N_DEV, r_mod + r_conv + r_fg, LANES)
    mod_g = got[:, :r_mod].reshape(N_DEV, -1)[:, :mod_part.size].reshape(N_DEV, nl, N_DEV, mod_cols)
    mod = lax.dynamic_index_in_dim(mod_g, me, axis=2, keepdims=False).transpose(1, 0, 2).reshape(nl, N_DEV * mod_cols)
    conv_g = got[:, r_mod:r_mod + r_conv].reshape(N_DEV, -1)[:, :conv_pad.size].reshape(N_DEV, nl, 3, fp)
    wconv_full = conv_g.transpose(1, 2, 0, 3).reshape(nl, 3, f_pad)
    fg_g = got[:, r_mod + r_conv:].reshape(N_DEV, -1)[:, :w_fg.size].reshape(N_DEV, nf, orows, n_heads)
    wfg_full = fg_g.transpose(1, 0, 2, 3).reshape(nf, d, n_heads)
    wfg_full = jnp.pad(wfg_full, ((0, 0), (0, 0), (0, LANES - n_heads))).astype(BF16)
    bfg_full = jnp.pad(b_fg, ((0, 0), (0, LANES - n_heads))).reshape(nf, 1, LANES)
    bconv_full = jnp.pad(b_conv.reshape(nl, N_DEV, fs), ((0, 0), (0, 0), (0, fp - fs))).reshape(nl, 1, f_pad)

    gate_sh = jnp.pad(w_ffn_gate, ((0, 0), (0, 0), (0, fp - fs))).astype(BF16)
    up_sh = jnp.pad(w_ffn_up, ((0, 0), (0, 0), (0, fp - fs))).astype(BF16)
    down_sh = jnp.pad(w_ffn_down, ((0, 0), (0, fp - fs), (0, 0))).astype(BF16)
    shards = [w_qkv.astype(BF16), w_o.astype(BF16), gate_sh, up_sh, down_sh]
    full_shapes = [jax.ShapeDtypeStruct((nl, d, N_DEV * qs), BF16), jax.ShapeDtypeStruct((nl, d, d), BF16),
                   jax.ShapeDtypeStruct((nl, d, f_pad), BF16), jax.ShapeDtypeStruct((nl, d, f_pad), BF16),
                   jax.ShapeDtypeStruct((nl, f_pad, d), BF16)]
    place = [_cols_of(qs), _rows_of(orows), _cols_of(fp), _cols_of(fp), _rows_of(fp)]
    wqkv, wo, wg, wu, wd = _exchange(shards, full_shapes, [(i, i, _whole, place[i]) for i in range(5)],
                                     "gather_weights")

    sq, dx, small, big = _local_step(x[0], loss_target[0], mod, g_mix_pre, g_mix_post, g_ffn_pre, g_ffn_post,
                                     wqkv, wo, wg, wu, wd, wfg_full, bfg_full, wconv_full, bconv_full, 256)
    loss = lax.psum(0.5 * jnp.sum(sq) / d, ("x", "y", "c"))

    order = ["qkv", "o", "gate", "up", "down"]
    send = {"qkv": _cols_of(qs), "o": _rows_of(orows), "gate": _cols_of(fp), "up": _cols_of(fp), "down": _rows_of(fp)}
    recv_shapes = [jax.ShapeDtypeStruct((N_DEV, nl, d, qs), BF16), jax.ShapeDtypeStruct((N_DEV, nl, orows, d), BF16),
                   jax.ShapeDtypeStruct((N_DEV, nl, d, fp), BF16), jax.ShapeDtypeStruct((N_DEV, nl, d, fp), BF16),
                   jax.ShapeDtypeStruct((N_DEV, nl, fp, d), BF16)]
    srcs, jobs = [], []
    for l in range(nl):
        for wi, nm in enumerate(order):
            jobs.append((len(srcs), wi, send[nm], _slot(l)))
            srcs.append(big[l][nm])
    recv = _exchange(srcs, recv_shapes, jobs, "scatter_weight_grads")
    upd = {}
    for nm, rv, wt, mt, vt in zip(order, recv, [w_qkv, w_o, w_ffn_gate, w_ffn_up, w_ffn_down],
                                  [m_w_qkv, m_w_o, m_w_ffn_gate, m_w_ffn_up, m_w_ffn_down],
                                  [v_w_qkv, v_w_o, v_w_ffn_gate, v_w_ffn_up, v_w_ffn_down]):
        upd[nm] = _adamw(rv, wt, mt, vt, f"adamw_{nm}")

    stack = lambda key: jnp.stack([small[l][key] for l in range(nl)])
    dmod = stack("dmod")
    dgs = [stack(k) for k in ("dg1", "dg2", "dg3", "dg4")]
    dbc = stack("dbc").reshape(nl, N_DEV, fp)[:, :, :fs].reshape(nl, f_full)
    dbfg = jnp.stack([small[l]["dbfg"] for l in range(nl) if l % 2 == 1])
    dwc = stack("dwc").reshape(nl, 3, N_DEV, fp)[:, :, :, :fs].reshape(nl, 3, f_full)
    dwfg = jnp.stack([small[l]["dwfg"] for l in range(nl) if l % 2 == 1])
    rep_parts = [dmod] + dgs + [dbc, dbfg]
    rep_rows = [_ceil8(p.size) for p in rep_parts]
    r_rep, r_wc, r_wfg = sum(rep_rows), _ceil8(dwc.size), _ceil8(dwfg.size)
    payload = jnp.concatenate([_rows128(p, r) for p, r in zip(rep_parts, rep_rows)]
                              + [_rows128(dwc, r_wc), _rows128(dwfg, r_wfg)], axis=0)
    gsm = _allgather_small(payload, "gather_small_grads").reshape(N_DEV, 1, r_rep + r_wc + r_wfg, LANES)

    def pack(parts):
        return jnp.concatenate([_rows128(p, r) for p, r in zip(parts, rep_rows)], axis=0).reshape(1, r_rep, LANES)

    rep_w = [b_mod, g_mix_pre, g_mix_post, g_ffn_pre, g_ffn_post, b_conv, b_fg]
    rep_m = [m_b_mod, m_g_mix_pre, m_g_mix_post, m_g_ffn_pre, m_g_ffn_post, m_b_conv, m_b_fg]
    rep_v = [v_b_mod, v_g_mix_pre, v_g_mix_post, v_g_ffn_pre, v_g_ffn_post, v_b_conv, v_b_fg]
    rep_out = _adamw(gsm[:, :, :r_rep], pack(rep_w), pack(rep_m), pack(rep_v), "adamw_replicated", tr=r_rep, tc=LANES)

    def unpack(packed):
        outs, at = [], 0
        for p, r in zip(rep_w, rep_rows):
            outs.append(packed[0, at:at + r].reshape(-1)[:p.size].reshape(p.shape))
            at += r
        return outs

    rep_g, rep_d, rep_nm, rep_nv = (unpack(a) for a in rep_out)

    wc_all = gsm[:, 0, r_rep:r_rep + r_wc].reshape(N_DEV, -1)[:, :dwc.size].reshape(N_DEV, 1, nl * 3, f_full)
    wc_mine = lax.dynamic_slice_in_dim(wc_all, me * fs, fs, axis=3)
    wc_out = _adamw(wc_mine, w_conv.reshape(1, nl * 3, fs), m_w_conv.reshape(1, nl * 3, fs),
                    v_w_conv.reshape(1, nl * 3, fs), "adamw_conv", tr=nl * 3, tc=fs)
    wc_out = [a.reshape(nl, 3, fs) for a in wc_out]
    wfg_all = gsm[:, 0, r_rep + r_wc:].reshape(N_DEV, -1)[:, :dwfg.size].reshape(N_DEV, nf, d, n_heads)
    wfg_mine = lax.dynamic_slice_in_dim(wfg_all, me * orows, orows, axis=2)
    wfg_out = _adamw(wfg_mine, w_fg, m_w_fg, v_w_fg, "adamw_fgate", tr=orows, tc=n_heads)

    dmod_all = gsm[:, 0, :rep_rows[0]].reshape(N_DEV, -1)[:, :dmod.size].reshape(N_DEV, nl, N_DEV * mod_cols)
    dmod_mine = lax.dynamic_slice_in_dim(dmod_all, me * mod_cols, mod_cols, axis=2)
    gwmod = _wmod_grad(c_all.T, dmod_mine, "wmod_grad")
    wmod_out = _adamw(gwmod.reshape(1, nl, d, mod_cols), w_mod, m_w_mod, v_w_mod, "adamw_mod")

    per_weight = [wmod_out, None, None, None, upd["qkv"], upd["o"], wfg_out, None, None, None,
                  upd["gate"], upd["up"], wc_out, None, upd["down"]]
    rep_index = {1: 0, 2: 1, 3: 2, 8: 3, 9: 4, 13: 5, 7: 6}
    outs = [[], [], [], []]
    for pos, res in enumerate(per_weight):
        for kind in range(4):
            if res is None:
                outs[kind].append((rep_g, rep_d, rep_nm, rep_nv)[kind][rep_index[pos]])
            else:
                outs[kind].append(res[kind])
    return (loss, dx.reshape(1, s, d), *outs[0], *outs[1], *outs[2], *outs[3])
```

```python
import functools

import jax
import jax.numpy as jnp
from jax import lax
from jax.experimental import pallas as pl
from jax.experimental.pallas import tpu as pltpu

F32 = jnp.float32
BF16 = jnp.bfloat16

N_DEV = 8
HEAD_DIM = 64
LANES = 128
SUBLANES = 8
RMS_EPS = 1e-6
ADAM_LR = 0.001
ADAM_B1 = 0.9
ADAM_B2 = 0.999
ADAM_EPS = 1e-08
ADAM_WD = 0.01
ADAM_STEP = 10
VMEM_LIMIT_BYTES = 56 * 1024 * 1024
ATTN_TILES = {"sb_fwd": (512, 256), "sb_bwd": (512, 256), "fox_fwd": (512, 256), "fox_bwd": (256, 256)}

NN = (((1,), (0,)), ((), ()))
NT = (((1,), (1,)), ((), ()))
TN = (((0,), (0,)), ((), ()))
MESH = pl.DeviceIdType.MESH


def _params(*sem):
    return pltpu.CompilerParams(dimension_semantics=sem, vmem_limit_bytes=VMEM_LIMIT_BYTES)


def _pick(n, pref, quantum):
    if n <= pref:
        return n
    t = (pref // quantum) * quantum
    while n % t:
        t -= quantum
    return t


def _mm(a, b, *, dims, grid, a_spec, b_spec, o_spec, out_shape, name, add=None):
    nk = grid[2]
    acc_shape = tuple(d for d in o_spec.block_shape if d is not None)
    o_dtype = out_shape.dtype

    def body(*refs):
        if add is None:
            a_ref, b_ref, o_ref, acc_ref = refs
            add_ref = None
        else:
            a_ref, b_ref, add_ref, o_ref, acc_ref = refs
        k = pl.program_id(2)
        part = lax.dot_general(a_ref[...], b_ref[...], dims, preferred_element_type=F32)

        def finish(total):
            if add_ref is not None:
                total = total + add_ref[...]
            o_ref[...] = total.astype(o_dtype)

        if nk == 1:
            finish(part)
            return

        @pl.when(k == 0)
        def _():
            acc_ref[...] = part

        @pl.when(jnp.logical_and(k > 0, k < nk - 1))
        def _():
            acc_ref[...] += part

        @pl.when(k == nk - 1)
        def _():
            finish(acc_ref[...] + part)

    operands = [a, b] if add is None else [a, b, add]
    in_specs = [a_spec, b_spec] if add is None else [a_spec, b_spec, o_spec]
    return pl.pallas_call(
        body, name=name, grid=grid, in_specs=in_specs, out_specs=o_spec, out_shape=out_shape,
        scratch_shapes=[pltpu.VMEM(acc_shape, F32)],
        compiler_params=_params("parallel", "parallel", "arbitrary"),
    )(*operands)


def _mm_nn(a, w, l, out_dtype, name, *, col0=0, n=None, split_out=None):
    m, kdim = a.shape
    n = w.shape[2] if n is None else n
    tm, tk = _pick(m, 512, SUBLANES), _pick(kdim, 1024, LANES)
    tn = _pick(n if split_out is None else split_out, 1024, LANES)
    jb = col0 // tn
    grid = (m // tm, n // tn, kdim // tk)
    a_spec = pl.BlockSpec((tm, tk), lambda i, j, k: (i, k))
    b_spec = pl.BlockSpec((None, tk, tn), lambda i, j, k: (l, k, j + jb))
    if split_out is None:
        o_spec = pl.BlockSpec((tm, tn), lambda i, j, k: (i, j))
        shape = jax.ShapeDtypeStruct((m, n), out_dtype)
    else:
        nj1 = split_out // tn
        o_spec = pl.BlockSpec((None, tm, tn), lambda i, j, k: (j // nj1, i, j % nj1))
        shape = jax.ShapeDtypeStruct((n // split_out, m, split_out), out_dtype)
    return _mm(a, w, dims=NN, grid=grid, a_spec=a_spec, b_spec=b_spec, o_spec=o_spec, out_shape=shape, name=name)


def _mm_nt(a, w, l, out_dtype, name, *, add=None):
    n, kdim = w.shape[1], w.shape[2]
    if a.ndim == 2:
        m = a.shape[0]
        tk = _pick(kdim, 1024, LANES)
        a_spec_of = lambda tm: pl.BlockSpec((tm, tk), lambda i, j, k: (i, k))
    else:
        m, seg = a.shape[1], a.shape[2]
        tk = _pick(seg, 1024, LANES)
        nk1 = seg // tk
        a_spec_of = lambda tm: pl.BlockSpec((None, tm, tk), lambda i, j, k: (k // nk1, i, k % nk1))
    tm, tn = _pick(m, 512, SUBLANES), _pick(n, 1024, LANES)
    grid = (m // tm, n // tn, kdim // tk)
    b_spec = pl.BlockSpec((None, tn, tk), lambda i, j, k: (l, j, k))
    o_spec = pl.BlockSpec((tm, tn), lambda i, j, k: (i, j))
    return _mm(a, w, dims=NT, grid=grid, a_spec=a_spec_of(tm), b_spec=b_spec, o_spec=o_spec,
               out_shape=jax.ShapeDtypeStruct((m, n), out_dtype), name=name, add=add)


def _mm_tn(a, b, out_dtype, name):
    kdim, m = a.shape
    tk, tm = _pick(kdim, 1024, SUBLANES), _pick(m, 512, LANES)
    if b.ndim == 2:
        n = b.shape[1]
        tn = _pick(n, 1024, LANES)
        b_spec = pl.BlockSpec((tk, tn), lambda i, j, k: (k, j))
    else:
        seg = b.shape[2]
        n = b.shape[0] * seg
        tn = _pick(seg, 1024, LANES)
        nj1 = seg // tn
        b_spec = pl.BlockSpec((None, tk, tn), lambda i, j, k: (j // nj1, k, j % nj1))
    grid = (m // tm, n // tn, kdim // tk)
    a_spec = pl.BlockSpec((tk, tm), lambda i, j, k: (k, i))
    o_spec = pl.BlockSpec((tm, tn), lambda i, j, k: (i, j))
    return _mm(a, b, dims=TN, grid=grid, a_spec=a_spec, b_spec=b_spec, o_spec=o_spec,
               out_shape=jax.ShapeDtypeStruct((m, n), out_dtype), name=name)


def _rstd(x):
    return lax.rsqrt(jnp.mean(x * x, axis=-1, keepdims=True) + RMS_EPS)


def _norm_mod_fwd(x, g, shift, scale, name):
    s, d = x.shape
    tm = _pick(s, 512, SUBLANES)

    def body(x_ref, g_ref, sh_ref, sc_ref, h_ref):
        xv = x_ref[...]
        y = (xv * _rstd(xv)) * g_ref[...]
        h_ref[...] = (y * (1.0 + sc_ref[...]) + sh_ref[...]).astype(BF16)

    row = pl.BlockSpec((tm, d), lambda i: (i, 0))
    vec = pl.BlockSpec((1, d), lambda i: (0, 0))
    return pl.pallas_call(body, name=name, grid=(s // tm,), in_specs=[row, vec, vec, vec], out_specs=row,
                          out_shape=jax.ShapeDtypeStruct((s, d), BF16), compiler_params=_params("parallel"))(x, g, shift, scale)


def _norm_mod_bwd(dh, x, g, scale, dres, name):
    s, d = x.shape
    tm = _pick(s, 256, SUBLANES)

    def body(dh_ref, x_ref, g_ref, sc_ref, dres_ref, dx_ref, dsh_ref, dsc_ref, dg_ref):
        i = pl.program_id(0)
        xv, dhv, gv = x_ref[...], dh_ref[...], g_ref[...]
        r = _rstd(xv)
        xh = xv * r
        dn = dhv * (1.0 + sc_ref[...])
        gd = dn * gv
        dx_ref[...] = dres_ref[...] + r * (gd - xh * jnp.mean(gd * xh, axis=-1, keepdims=True))

        @pl.when(i == 0)
        def _():
            dsh_ref[...] = jnp.zeros_like(dsh_ref)
            dsc_ref[...] = jnp.zeros_like(dsc_ref)
            dg_ref[...] = jnp.zeros_like(dg_ref)

        dsh_ref[...] += jnp.sum(dhv, axis=0, keepdims=True)
        dsc_ref[...] += jnp.sum(dhv * (xh * gv), axis=0, keepdims=True)
        dg_ref[...] += jnp.sum(dn * xh, axis=0, keepdims=True)

    row = pl.BlockSpec((tm, d), lambda i: (i, 0))
    vec = pl.BlockSpec((1, d), lambda i: (0, 0))
    vshape = jax.ShapeDtypeStruct((1, d), F32)
    return pl.pallas_call(body, name=name, grid=(s // tm,), in_specs=[row, row, vec, vec, row],
                          out_specs=[row, vec, vec, vec],
                          out_shape=[jax.ShapeDtypeStruct((s, d), F32), vshape, vshape, vshape],
                          compiler_params=_params("arbitrary"))(dh, x, g, scale, dres)


def _post_fwd(x, u, g, gate, name):
    s, d = x.shape
    tm = _pick(s, 512, SUBLANES)

    def body(x_ref, u_ref, g_ref, gt_ref, o_ref):
        uv = u_ref[...]
        o_ref[...] = x_ref[...] + gt_ref[...] * ((uv * _rstd(uv)) * g_ref[...])

    row = pl.BlockSpec((tm, d), lambda i: (i, 0))
    vec = pl.BlockSpec((1, d), lambda i: (0, 0))
    return pl.pallas_call(body, name=name, grid=(s // tm,), in_specs=[row, row, vec, vec], out_specs=row,
                          out_shape=jax.ShapeDtypeStruct((s, d), F32), compiler_params=_params("parallel"))(x, u, g, gate)


def _post_bwd(dx, u, g, gate, name):
    s, d = u.shape
    tm = _pick(s, 256, SUBLANES)

    def body(dx_ref, u_ref, g_ref, gt_ref, du_ref, dgt_ref, dg_ref):
        i = pl.program_id(0)
        uv, dxv, gv = u_ref[...], dx_ref[...], g_ref[...]
        r = _rstd(uv)
        uh = uv * r
        dn = dxv * gt_ref[...]
        gd = dn * gv
        du_ref[...] = (r * (gd - uh * jnp.mean(gd * uh, axis=-1, keepdims=True))).astype(BF16)

        @pl.when(i == 0)
        def _():
            dgt_ref[...] = jnp.zeros_like(dgt_ref)
            dg_ref[...] = jnp.zeros_like(dg_ref)

        dgt_ref[...] += jnp.sum(dxv * (uh * gv), axis=0, keepdims=True)
        dg_ref[...] += jnp.sum(dn * uh, axis=0, keepdims=True)

    row = pl.BlockSpec((tm, d), lambda i: (i, 0))
    vec = pl.BlockSpec((1, d), lambda i: (0, 0))
    vshape = jax.ShapeDtypeStruct((1, d), F32)
    return pl.pallas_call(body, name=name, grid=(s // tm,), in_specs=[row, row, vec, vec],
                          out_specs=[row, vec, vec],
                          out_shape=[jax.ShapeDtypeStruct((s, d), BF16), vshape, vshape],
                          compiler_params=_params("arbitrary"))(dx, u, g, gate)


def _shift_rows(cur, prev8, k):
    rolled = pltpu.roll(cur, k, axis=0)
    rolled_prev = pltpu.roll(prev8, k, axis=0)
    i8 = lax.broadcasted_iota(jnp.int32, prev8.shape, 0)
    top = jnp.where(i8 < k, rolled_prev, rolled[:SUBLANES])
    return jnp.concatenate([top, rolled[SUBLANES:]], axis=0)


def _conv_pre(g, prev8, wc_ref, bc_ref):
    s1 = _shift_rows(g, prev8, 1)
    s2 = _shift_rows(g, prev8, 2)
    gc = bc_ref[...] + wc_ref[0:1, :] * s2 + wc_ref[1:2, :] * s1 + wc_ref[2:3, :] * g
    return gc, s1, s2


def _conv_act_fwd(gp, up, wc, bc, name):
    s, f = gp.shape
    tm, tf = _pick(s, 512, SUBLANES), _pick(f, 768, LANES)
    r8 = tm // SUBLANES

    def body(g_ref, gprev_ref, up_ref, wc_ref, bc_ref, a_ref):
        i = pl.program_id(1)
        prev = jnp.where(i == 0, 0.0, gprev_ref[...])
        gc, _, _ = _conv_pre(g_ref[...], prev, wc_ref, bc_ref)
        a_ref[...] = ((gc * jax.nn.sigmoid(gc)) * up_ref[...]).astype(BF16)

    tile = pl.BlockSpec((tm, tf), lambda j, i: (i, j))
    prev = pl.BlockSpec((SUBLANES, tf), lambda j, i: (jnp.maximum(i * r8 - 1, 0), j))
    return pl.pallas_call(body, name=name, grid=(f // tf, s // tm),
                          in_specs=[tile, prev, tile, pl.BlockSpec((3, tf), lambda j, i: (0, j)),
                                    pl.BlockSpec((1, tf), lambda j, i: (0, j))],
                          out_specs=tile, out_shape=jax.ShapeDtypeStruct((s, f), BF16),
                          compiler_params=_params("parallel", "parallel"))(gp, gp, up, wc, bc)


def _conv_act_bwd(da, gp, up, wc, bc, name):
    s, f = gp.shape
    tm, tf = _pick(s, 256, SUBLANES), _pick(f, 768, LANES)
    r8 = tm // SUBLANES
    nrow = s // tm

    def body(da_ref, dan_ref, g_ref, gprev_ref, gn_ref, up_ref, upn_ref, wc_ref, bc_ref,
             dgp_ref, dup_ref, dwc_ref, dbc_ref):
        i = pl.program_id(1)
        last = i == nrow - 1
        prev = jnp.where(i == 0, 0.0, gprev_ref[...])
        g_ext = jnp.concatenate([g_ref[...], gn_ref[...]], axis=0)
        up_ext = jnp.concatenate([up_ref[...], upn_ref[...]], axis=0)
        da_ext = jnp.concatenate([da_ref[...], jnp.where(last, 0.0, dan_ref[...])], axis=0)
        gc, s1, s2 = _conv_pre(g_ext, prev, wc_ref, bc_ref)
        sg = jax.nn.sigmoid(gc)
        dup_ref[...] = (da_ext * (gc * sg))[:tm].astype(BF16)
        dgc = da_ext * up_ext * (sg * (1.0 + gc * (1.0 - sg)))
        ext = tm + SUBLANES
        dgp = (wc_ref[2:3, :] * dgc + wc_ref[1:2, :] * pltpu.roll(dgc, ext - 1, axis=0)
               + wc_ref[0:1, :] * pltpu.roll(dgc, ext - 2, axis=0))
        dgp_ref[...] = dgp[:tm].astype(BF16)

        @pl.when(i == 0)
        def _():
            dwc_ref[...] = jnp.zeros_like(dwc_ref)
            dbc_ref[...] = jnp.zeros_like(dbc_ref)

        d0 = dgc[:tm]
        dwc_ref[0:1, :] += jnp.sum(d0 * s2[:tm], axis=0, keepdims=True)
        dwc_ref[1:2, :] += jnp.sum(d0 * s1[:tm], axis=0, keepdims=True)
        dwc_ref[2:3, :] += jnp.sum(d0 * g_ext[:tm], axis=0, keepdims=True)
        dbc_ref[...] += jnp.sum(d0, axis=0, keepdims=True)

    tile = pl.BlockSpec((tm, tf), lambda j, i: (i, j))
    prev = pl.BlockSpec((SUBLANES, tf), lambda j, i: (jnp.maximum(i * r8 - 1, 0), j))
    nxt = pl.BlockSpec((SUBLANES, tf), lambda j, i: (jnp.minimum((i + 1) * r8, s // SUBLANES - 1), j))
    return pl.pallas_call(body, name=name, grid=(f // tf, nrow),
                          in_specs=[tile, nxt, tile, prev, nxt, tile, nxt,
                                    pl.BlockSpec((3, tf), lambda j, i: (0, j)), pl.BlockSpec((1, tf), lambda j, i: (0, j))],
                          out_specs=[tile, tile, pl.BlockSpec((3, tf), lambda j, i: (0, j)),
                                     pl.BlockSpec((1, tf), lambda j, i: (0, j))],
                          out_shape=[jax.ShapeDtypeStruct((s, f), BF16), jax.ShapeDtypeStruct((s, f), BF16),
                                     jax.ShapeDtypeStruct((3, f), F32), jax.ShapeDtypeStruct((1, f), F32)],
                          compiler_params=_params("parallel", "arbitrary"))(da, da, gp, gp, gp, up, up, wc, bc)


def _split_bf16(v, parts):
    out, rem = [], v
    for _ in range(parts):
        t = rem.astype(BF16)
        out.append(t)
        rem = rem - t.astype(F32)
    return jnp.concatenate(out, axis=1)


def _tri(t, cmp, reps):
    row = lax.broadcasted_iota(jnp.int32, (t, t), 0)
    col = lax.broadcasted_iota(jnp.int32, (t, t), 1)
    m = cmp(row, col).astype(BF16)
    return jnp.concatenate([m] * reps, axis=0)


def _log_sigmoid(z):
    mn = jnp.minimum(z, 0.0)
    return mn - jnp.log(1.0 + jnp.exp(mn + (mn - z)))


def _log_one_minus_sigmoid(z):
    mn = jnp.minimum(z, 0.0)
    neg = mn - z
    return neg - jnp.log(1.0 + jnp.exp(mn + neg))


def _pair(xv):
    lane = lax.broadcasted_iota(jnp.int32, xv.shape, 1)
    zero = jnp.zeros_like(xv)
    return jnp.where(lane < HEAD_DIM, xv, zero), jnp.where(lane < HEAD_DIM, zero, xv)


def _attn_specs(s, tq):
    q_spec = pl.BlockSpec((None, tq, LANES), lambda hp, qi: (0, qi, hp))
    k_spec = pl.BlockSpec((None, s, LANES), lambda hp, qi: (1, 0, hp))
    v_spec = pl.BlockSpec((None, s, LANES), lambda hp, qi: (2, 0, hp))
    return q_spec, k_spec, v_spec


def _rel_mask(tq, tk, i, strict):
    row = lax.broadcasted_iota(jnp.int32, (tq, tk), 0)
    col = lax.broadcasted_iota(jnp.int32, (tq, tk), 1) + i * tk
    return col < row if strict else col <= row


def _sb_fwd(qkv, tq, tk, name):
    _, s, d = qkv.shape
    hp_n, nq, nd = d // LANES, s // tq, tq // tk
    scale = HEAD_DIM ** -0.5

    def body(q_ref, k_ref, v_ref, o_ref, lt_ref):
        qi = pl.program_id(1)
        lane = lax.broadcasted_iota(jnp.int32, (tq, LANES), 1)
        onward = _tri(tk, lambda j, sidx: j >= sidx, 2)
        qpair = _pair(q_ref[...] * scale)

        def block(kb, carry, mask):
            acc, runs = carry[0], carry[1:]
            ks = pl.multiple_of(kb * tk, tk)
            kblk = k_ref[pl.ds(ks, tk), :]
            vpair = _pair(v_ref[pl.ds(ks, tk), :])
            probs, new_runs = [], []
            for qh, run in zip(qpair, runs):
                z = lax.dot_general(qh, kblk, NT, preferred_element_type=F32)
                l1 = _log_one_minus_sigmoid(z)
                if mask is not None:
                    l1 = jnp.where(mask, l1, 0.0)
                c = jnp.dot(_split_bf16(l1, 2), onward, preferred_element_type=F32)
                a = jnp.exp(z + c + run)
                if mask is not None:
                    a = jnp.where(mask, a, 0.0)
                probs.append(a.astype(BF16))
                new_runs.append(run + jnp.sum(l1, axis=1, keepdims=True))
            acc = acc + jnp.dot(jnp.concatenate(probs, axis=1), jnp.concatenate(vpair, axis=0),
                                preferred_element_type=F32)
            return (acc, *new_runs)

        zero1 = jnp.zeros((tq, 1), F32)
        carry = (jnp.zeros((tq, LANES), F32), zero1, zero1)
        for i in reversed(range(nd)):
            carry = block(qi * nd + i, carry, _rel_mask(tq, tk, i, True))
        acc, run0, run1 = lax.fori_loop(0, qi * nd, lambda j, cr: block(qi * nd - 1 - j, cr, None), carry)
        o_ref[...] = acc.astype(BF16)
        lt_ref[...] = jnp.where(lane < HEAD_DIM, run0, run1)

    q_spec, k_spec, v_spec = _attn_specs(s, tq)
    return pl.pallas_call(
        body, name=name, grid=(hp_n, nq), in_specs=[q_spec, k_spec, v_spec],
        out_specs=[pl.BlockSpec((tq, LANES), lambda hp, qi: (qi, hp)),
                   pl.BlockSpec((None, tq, LANES), lambda hp, qi: (hp, qi, 0))],
        out_shape=[jax.ShapeDtypeStruct((s, d), BF16), jax.ShapeDtypeStruct((hp_n, s, LANES), F32)],
        compiler_params=_params("parallel", "parallel"))(qkv, qkv, qkv)


def _sb_bwd(qkv, ltot, do, tq, tk, name):
    _, s, d = qkv.shape
    hp_n, nq, nd = d // LANES, s // tq, tq // tk
    scale = HEAD_DIM ** -0.5

    def body(q_ref, k_ref, v_ref, lt_ref, do_ref, dqkv_ref, dk_acc, dv_acc):
        qi = pl.program_id(1)

        @pl.when(qi == 0)
        def _():
            dk_acc[...] = jnp.zeros_like(dk_acc)
            dv_acc[...] = jnp.zeros_like(dv_acc)

        onward = _tri(tk, lambda j, sidx: j >= sidx, 2)
        before = _tri(tk, lambda j, sidx: j < sidx, 2)
        qpair = _pair(q_ref[...] * scale)
        dopair = _pair(do_ref[...])
        ltv = lt_ref[...]
        lts = (ltv[:, 0:1], ltv[:, HEAD_DIM:HEAD_DIM + 1])

        def block(kb, carry, mask):
            dq, lpres, gpres = carry[0], carry[1:3], carry[3:5]
            ks = pl.multiple_of(kb * tk, tk)
            kblk, vblk = k_ref[pl.ds(ks, tk), :], v_ref[pl.ds(ks, tk), :]
            kpair = _pair(kblk * scale)
            dzs, probs, new_l, new_g = [], [], [], []
            for qh, doh, lt, lpre, gpre in zip(qpair, dopair, lts, lpres, gpres):
                z = lax.dot_general(qh, kblk, NT, preferred_element_type=F32)
                l1 = _log_one_minus_sigmoid(z)
                if mask is not None:
                    l1 = jnp.where(mask, l1, 0.0)
                lpre = lpre + jnp.sum(l1, axis=1, keepdims=True)
                c = jnp.dot(_split_bf16(l1, 2), onward, preferred_element_type=F32)
                a = jnp.exp(z + c + (lt - lpre))
                if mask is not None:
                    a = jnp.where(mask, a, 0.0)
                da = lax.dot_general(doh, vblk, NT, preferred_element_type=F32)
                g = da * a
                p = jnp.dot(_split_bf16(g, 2), before, preferred_element_type=F32) + gpre
                dz = g - jnp.exp(z + l1) * (g + p)
                if mask is not None:
                    dz = jnp.where(mask, dz, 0.0)
                dzs.append(dz.astype(BF16))
                probs.append(a.astype(BF16))
                new_l.append(lpre)
                new_g.append(gpre + jnp.sum(g, axis=1, keepdims=True))
            dq = dq + jnp.dot(jnp.concatenate(dzs, axis=1), jnp.concatenate(kpair, axis=0),
                              preferred_element_type=F32)
            dk_acc[pl.ds(ks, tk), :] += lax.dot_general(jnp.concatenate(dzs, axis=0), jnp.concatenate(qpair, axis=0),
                                                        TN, preferred_element_type=F32)
            dv_acc[pl.ds(ks, tk), :] += lax.dot_general(jnp.concatenate(probs, axis=0), jnp.concatenate(dopair, axis=0),
                                                        TN, preferred_element_type=F32)
            return (dq, *new_l, *new_g)

        zero1 = jnp.zeros((tq, 1), F32)
        init = (jnp.zeros((tq, LANES), F32), zero1, zero1, zero1, zero1)
        carry = lax.fori_loop(0, qi * nd, lambda j, cr: block(j, cr, None), init)
        for i in range(nd):
            carry = block(qi * nd + i, carry, _rel_mask(tq, tk, i, True))
        dqkv_ref[0, pl.ds(pl.multiple_of(qi * tq, tq), tq), :] = carry[0].astype(BF16)

        @pl.when(qi == nq - 1)
        def _():
            dqkv_ref[1] = dk_acc[...].astype(BF16)
            dqkv_ref[2] = dv_acc[...].astype(BF16)

    q_spec, k_spec, v_spec = _attn_specs(s, tq)
    return pl.pallas_call(
        body, name=name, grid=(hp_n, nq),
        in_specs=[q_spec, k_spec, v_spec, pl.BlockSpec((None, tq, LANES), lambda hp, qi: (hp, qi, 0)),
                  pl.BlockSpec((tq, LANES), lambda hp, qi: (qi, hp))],
        out_specs=pl.BlockSpec((3, s, LANES), lambda hp, qi: (0, 0, hp)),
        out_shape=jax.ShapeDtypeStruct((3, s, d), BF16),
        scratch_shapes=[pltpu.VMEM((s, LANES), F32), pltpu.VMEM((s, LANES), F32)],
        compiler_params=_params("parallel", "arbitrary"))(qkv, qkv, qkv, ltot, do)


def _fox_prep(fl, bias, name):
    s, w = fl.shape
    tb = _pick(s, 512, SUBLANES)

    def body(fl_ref, b_ref, cum_ref, carry_ref):
        i = pl.program_id(0)

        @pl.when(i == 0)
        def _():
            carry_ref[...] = jnp.zeros_like(carry_ref)

        logf = _log_sigmoid(fl_ref[...] + b_ref[...])
        row = lax.broadcasted_iota(jnp.int32, (tb, tb), 0)
        col = lax.broadcasted_iota(jnp.int32, (tb, tb), 1)
        incl = (col <= row).astype(BF16)
        tot = carry_ref[...]
        rem = logf
        for _ in range(3):
            part = rem.astype(BF16)
            tot = tot + jnp.dot(incl, part, preferred_element_type=F32)
            rem = rem - part.astype(F32)
        cum_ref[...] = tot
        carry_ref[...] = tot[tb - 1:tb, :]

    blk = pl.BlockSpec((tb, w), lambda i: (i, 0))
    return pl.pallas_call(body, name=name, grid=(s // tb,), in_specs=[blk, pl.BlockSpec((1, w), lambda i: (0, 0))],
                          out_specs=blk, out_shape=jax.ShapeDtypeStruct((s, w), F32),
                          scratch_shapes=[pltpu.VMEM((1, w), F32)], compiler_params=_params("arbitrary"))(fl, bias)


def _fox_gate_bwd(dcum, fl, bias, n_heads, name):
    s, w = fl.shape
    tb = _pick(s, 512, SUBLANES)
    nb = s // tb

    def body(dc_ref, fl_ref, b_ref, dfl_ref, db_ref, carry_ref):
        i = pl.program_id(0)

        @pl.when(i == 0)
        def _():
            carry_ref[...] = jnp.zeros_like(carry_ref)
            db_ref[...] = jnp.zeros_like(db_ref)

        row = lax.broadcasted_iota(jnp.int32, (tb, tb), 0)
        col = lax.broadcasted_iota(jnp.int32, (tb, tb), 1)
        incl = (col >= row).astype(BF16)
        tot = jnp.broadcast_to(carry_ref[...], (tb, w))
        rem = dc_ref[...]
        for _ in range(3):
            part = rem.astype(BF16)
            tot = tot + jnp.dot(incl, part, preferred_element_type=F32)
            rem = rem - part.astype(F32)
        carry_ref[...] = tot[0:1, :]
        xg = fl_ref[...] + b_ref[...]
        e = jnp.exp(-jnp.abs(xg))
        sig_neg = jnp.where(xg >= 0.0, e, 1.0) / (1.0 + e)
        lane = lax.broadcasted_iota(jnp.int32, (tb, w), 1)
        dfl = jnp.where(lane < n_heads, tot * sig_neg, 0.0)
        dfl_ref[...] = dfl.astype(BF16)
        db_ref[...] += jnp.sum(dfl, axis=0, keepdims=True)

    blk = pl.BlockSpec((tb, w), lambda i: (nb - 1 - i, 0))
    vec = pl.BlockSpec((1, w), lambda i: (0, 0))
    return pl.pallas_call(body, name=name, grid=(nb,), in_specs=[blk, blk, vec], out_specs=[blk, vec],
                          out_shape=[jax.ShapeDtypeStruct((s, w), BF16), jax.ShapeDtypeStruct((1, w), F32)],
                          scratch_shapes=[pltpu.VMEM((1, w), F32)], compiler_params=_params("arbitrary"))(dcum, fl, bias)


def _fox_fwd(qkv, cq, ck, tq, tk, name):
    _, s, d = qkv.shape
    hp_n, nq, nd = d // LANES, s // tq, tq // tk
    scale = HEAD_DIM ** -0.5

    def body(q_ref, k_ref, v_ref, cq_ref, ck_ref, o_ref, lse_ref):
        qi = pl.program_id(1)
        low = lax.broadcasted_iota(jnp.int32, (tq, LANES), 1) < HEAD_DIM
        qpair = _pair(q_ref[...] * scale)
        cqv = cq_ref[...]
        cqs = (cqv[:, 0:1], cqv[:, HEAD_DIM:HEAD_DIM + 1])

        def block(kb, carry, mask):
            acc, stats = carry[0], carry[1:]
            ks = pl.multiple_of(kb * tk, tk)
            kblk = k_ref[pl.ds(ks, tk), :]
            vpair = _pair(v_ref[pl.ds(ks, tk), :])
            probs, alphas, new = [], [], []
            for h, (qh, cqh) in enumerate(zip(qpair, cqs)):
                m, lsum = stats[2 * h], stats[2 * h + 1]
                z = lax.dot_general(qh, kblk, NT, preferred_element_type=F32)
                sc = z + (cqh - ck_ref[h:h + 1, pl.ds(ks, tk)])
                if mask is not None:
                    sc = jnp.where(mask, sc, -jnp.inf)
                m_new = jnp.maximum(m, jnp.max(sc, axis=1, keepdims=True))
                alpha = jnp.exp(m - m_new)
                p = jnp.exp(sc - m_new)
                new += [m_new, alpha * lsum + jnp.sum(p, axis=1, keepdims=True)]
                probs.append(p.astype(BF16))
                alphas.append(alpha)
            acc = jnp.where(low, alphas[0], alphas[1]) * acc + jnp.dot(
                jnp.concatenate(probs, axis=1), jnp.concatenate(vpair, axis=0), preferred_element_type=F32)
            return (acc, *new)

        neg = jnp.full((tq, 1), -jnp.inf, F32)
        zero1 = jnp.zeros((tq, 1), F32)
        carry = (jnp.zeros((tq, LANES), F32), neg, zero1, neg, zero1)
        for i in range(nd):
            carry = block(qi * nd + i, carry, _rel_mask(tq, tk, i, False))
        acc, m0, l0, m1, l1 = lax.fori_loop(0, qi * nd, lambda j, cr: block(qi * nd - 1 - j, cr, None), carry)
        o_ref[...] = (acc / jnp.where(low, l0, l1)).astype(BF16)
        lse_ref[...] = jnp.where(low, m0 + jnp.log(l0), m1 + jnp.log(l1))

    q_spec, k_spec, v_spec = _attn_specs(s, tq)
    pair_rows = pl.BlockSpec((None, tq, LANES), lambda hp, qi: (hp, qi, 0))
    return pl.pallas_call(
        body, name=name, grid=(hp_n, nq),
        in_specs=[q_spec, k_spec, v_spec, pair_rows, pl.BlockSpec((None, 2, s), lambda hp, qi: (hp, 0, 0))],
        out_specs=[pl.BlockSpec((tq, LANES), lambda hp, qi: (qi, hp)), pair_rows],
        out_shape=[jax.ShapeDtypeStruct((s, d), BF16), jax.ShapeDtypeStruct((hp_n, s, LANES), F32)],
        compiler_params=_params("parallel", "parallel"))(qkv, qkv, qkv, cq, ck)


def _fox_bwd(qkv, o, do, lse, cq, ck, tq, tk, name):
    _, s, d = qkv.shape
    hp_n, nq, nd = d // LANES, s // tq, tq // tk
    scale = HEAD_DIM ** -0.5

    def body(q_ref, k_ref, v_ref, o_ref, do_ref, lse_ref, cq_ref, ck_ref, dqkv_ref, dcq_ref, dck_ref, dk_acc, dv_acc):
        qi = pl.program_id(1)

        @pl.when(qi == 0)
        def _():
            dk_acc[...] = jnp.zeros_like(dk_acc)
            dv_acc[...] = jnp.zeros_like(dv_acc)
            dck_ref[...] = jnp.zeros_like(dck_ref)

        low = lax.broadcasted_iota(jnp.int32, (tq, LANES), 1) < HEAD_DIM
        dov, cqv, lsev = do_ref[...], cq_ref[...], lse_ref[...]
        qpair = _pair(q_ref[...] * scale)
        dopair = _pair(dov)
        prod = dov.astype(F32) * o_ref[...].astype(F32)
        deltas = (jnp.sum(jnp.where(low, prod, 0.0), axis=1, keepdims=True),
                  jnp.sum(jnp.where(low, 0.0, prod), axis=1, keepdims=True))
        cqs = (cqv[:, 0:1], cqv[:, HEAD_DIM:HEAD_DIM + 1])
        lses = (lsev[:, 0:1], lsev[:, HEAD_DIM:HEAD_DIM + 1])

        def block(kb, carry, mask):
            dq, rowsums = carry[0], carry[1:]
            ks = pl.multiple_of(kb * tk, tk)
            kblk, vblk = k_ref[pl.ds(ks, tk), :], v_ref[pl.ds(ks, tk), :]
            kpair = _pair(kblk * scale)
            dss, probs, new_rows = [], [], []
            for h, (qh, doh) in enumerate(zip(qpair, dopair)):
                z = lax.dot_general(qh, kblk, NT, preferred_element_type=F32)
                sc = z + (cqs[h] - ck_ref[h:h + 1, pl.ds(ks, tk)])
                p = jnp.exp(sc - lses[h])
                if mask is not None:
                    p = jnp.where(mask, p, 0.0)
                dp = lax.dot_general(doh, vblk, NT, preferred_element_type=F32)
                ds = p * (dp - deltas[h])
                dck_ref[h:h + 1, pl.ds(ks, tk)] -= jnp.sum(ds, axis=0, keepdims=True)
                new_rows.append(rowsums[h] + jnp.sum(ds, axis=1, keepdims=True))
                dss.append(ds.astype(BF16))
                probs.append(p.astype(BF16))
            dq = dq + jnp.dot(jnp.concatenate(dss, axis=1), jnp.concatenate(kpair, axis=0),
                              preferred_element_type=F32)
            dk_acc[pl.ds(ks, tk), :] += lax.dot_general(jnp.concatenate(dss, axis=0), jnp.concatenate(qpair, axis=0),
                                                       TN, preferred_element_type=F32)
            dv_acc[pl.ds(ks, tk), :] += lax.dot_general(jnp.concatenate(probs, axis=0), jnp.concatenate(dopair, axis=0),
                                                       TN, preferred_element_type=F32)
            return (dq, *new_rows)

        zero1 = jnp.zeros((tq, 1), F32)
        carry = lax.fori_loop(0, qi * nd, lambda j, cr: block(j, cr, None), (jnp.zeros((tq, LANES), F32), zero1, zero1))
        for i in range(nd):
            carry = block(qi * nd + i, carry, _rel_mask(tq, tk, i, False))
        dq, rs0, rs1 = carry
        dqkv_ref[0, pl.ds(pl.multiple_of(qi * tq, tq), tq), :] = dq.astype(BF16)
        dcq_ref[...] = jnp.where(low, rs0, rs1)

        @pl.when(qi == nq - 1)
        def _():
            dqkv_ref[1] = dk_acc[...].astype(BF16)
            dqkv_ref[2] = dv_acc[...].astype(BF16)

    q_spec, k_spec, v_spec = _attn_specs(s, tq)
    pair_rows = pl.BlockSpec((None, tq, LANES), lambda hp, qi: (hp, qi, 0))
    tile = pl.BlockSpec((tq, LANES), lambda hp, qi: (qi, hp))
    keys = pl.BlockSpec((None, 2, s), lambda hp, qi: (hp, 0, 0))
    return pl.pallas_call(
        body, name=name, grid=(hp_n, nq),
        in_specs=[q_spec, k_spec, v_spec, tile, tile, pair_rows, pair_rows, keys],
        out_specs=[pl.BlockSpec((3, s, LANES), lambda hp, qi: (0, 0, hp)), pair_rows, keys],
        out_shape=[jax.ShapeDtypeStruct((3, s, d), BF16), jax.ShapeDtypeStruct((hp_n, s, LANES), F32),
                   jax.ShapeDtypeStruct((hp_n, 2, s), F32)],
        scratch_shapes=[pltpu.VMEM((s, LANES), F32), pltpu.VMEM((s, LANES), F32)],
        compiler_params=_params("parallel", "arbitrary"))(qkv, qkv, qkv, o, do, lse, cq, ck)


def _loss_head(y, target, name):
    s, d = y.shape
    tm = _pick(s, 512, SUBLANES)

    def body(y_ref, t_ref, dy_ref, sq_ref):
        i = pl.program_id(0)
        diff = y_ref[...] - t_ref[...]
        dy_ref[...] = diff / d

        @pl.when(i == 0)
        def _():
            sq_ref[...] = jnp.zeros_like(sq_ref)

        sq_ref[...] += jnp.sum(diff * diff, axis=0, keepdims=True)

    row = pl.BlockSpec((tm, d), lambda i: (i, 0))
    vec = pl.BlockSpec((1, d), lambda i: (0, 0))
    return pl.pallas_call(body, name=name, grid=(s // tm,), in_specs=[row, row], out_specs=[row, vec],
                          out_shape=[jax.ShapeDtypeStruct((s, d), F32), jax.ShapeDtypeStruct((1, d), F32)],
                          compiler_params=_params("arbitrary"))(y, target)


def _mod_fwd(c_all, w_mod, b_mod_cols, name):
    nl, d, cols = w_mod.shape
    nb = c_all.shape[0]

    def body(c_ref, w_ref, b_ref, o_ref):
        cv = c_ref[...]
        act = (cv * jax.nn.sigmoid(cv)).astype(BF16)
        o_ref[...] = jnp.dot(act, w_ref[...].astype(BF16), preferred_element_type=F32) + b_ref[...]

    return pl.pallas_call(
        body, name=name, grid=(nl,),
        in_specs=[pl.BlockSpec((nb, d), lambda l: (0, 0)), pl.BlockSpec((None, d, cols), lambda l: (l, 0, 0)),
                  pl.BlockSpec((None, 1, cols), lambda l: (l, 0, 0))],
        out_specs=pl.BlockSpec((None, nb, cols), lambda l: (l, 0, 0)),
        out_shape=jax.ShapeDtypeStruct((nl, nb, cols), F32), compiler_params=_params("parallel"))(c_all, w_mod, b_mod_cols)


def _wmod_grad(c_all_t, dmod, name):
    d, nb = c_all_t.shape
    _, nl, cols = dmod.shape

    def body(c_ref, dm_ref, o_ref):
        cv = c_ref[...]
        act = cv * jax.nn.sigmoid(cv)
        tot = act[:, 0:1] * dm_ref[0]
        for b in range(1, nb):
            tot = tot + act[:, b:b + 1] * dm_ref[b]
        o_ref[...] = tot

    return pl.pallas_call(
        body, name=name, grid=(nl,),
        in_specs=[pl.BlockSpec((d, nb), lambda l: (0, 0)), pl.BlockSpec((nb, None, 1, cols), lambda l: (0, l, 0, 0))],
        out_specs=pl.BlockSpec((None, d, cols), lambda l: (l, 0, 0)),
        out_shape=jax.ShapeDtypeStruct((nl, d, cols), F32), compiler_params=_params("parallel"))(
            c_all_t, dmod.reshape(nb, nl, 1, cols))


def _adamw(recv, w, m, v, name, *, tr=256, tc=512):
    nq, nl, rp, cp = recv.shape
    _, r, c = w.shape
    br = _pick(r, tr, SUBLANES) if rp == r else r
    bc = _pick(c, tc, LANES) if cp == c else c
    rbr = br if rp == r else rp
    rbc = bc if cp == c else cp

    def body(rv_ref, w_ref, m_ref, v_ref, g_ref, d_ref, nm_ref, nv_ref):
        g = rv_ref[0, :br, :bc].astype(F32)
        for qd in range(1, nq):
            g = g + rv_ref[qd, :br, :bc].astype(F32)
        m_new = ADAM_B1 * m_ref[...] + (1.0 - ADAM_B1) * g
        v_new = ADAM_B2 * v_ref[...] + (1.0 - ADAM_B2) * jnp.square(g)
        m_hat = m_new / (1.0 - ADAM_B1 ** ADAM_STEP)
        v_hat = v_new / (1.0 - ADAM_B2 ** ADAM_STEP)
        g_ref[...] = g
        d_ref[...] = -ADAM_LR * (m_hat / (jnp.sqrt(v_hat) + ADAM_EPS) + ADAM_WD * w_ref[...])
        nm_ref[...] = m_new
        nv_ref[...] = v_new

    blk = pl.BlockSpec((None, br, bc), lambda l, i, j: (l, i, j))
    rblk = pl.BlockSpec((nq, None, rbr, rbc), lambda l, i, j: (0, l, i, j))
    shape = jax.ShapeDtypeStruct(w.shape, F32)
    return pl.pallas_call(body, name=name, grid=(nl, r // br, c // bc), in_specs=[rblk, blk, blk, blk],
                          out_specs=[blk] * 4, out_shape=[shape] * 4,
                          compiler_params=_params("parallel", "parallel", "parallel"))(recv, w, m, v)


def _mesh_pos():
    return lax.axis_index("x"), lax.axis_index("y"), lax.axis_index("c")


def _allgather_small(block, name):
    m_per, n = block.shape

    def body(x_ref, out_ref, send_sems, recv_sems, local_sem):
        x, y, c = _mesh_pos()
        me, sibling = (x, y, c), (x, y, 1 - c)
        chips = [(1 - x, y), (x, 1 - y), (1 - x, 1 - y)]

        def rows(px, py, pc):
            return out_ref.at[pl.ds((4 * px + 2 * py + pc) * m_per, m_per), :]

        def copy(k, blk, to, src=None):
            return pltpu.make_async_remote_copy(
                src_ref=rows(*blk) if src is None else src, dst_ref=rows(*blk),
                send_sem=send_sems.at[k], recv_sem=recv_sems.at[k], device_id=to, device_id_type=MESH)

        mine = pltpu.make_async_copy(x_ref, rows(*me), local_sem)
        mine.start()
        first = [copy(0, me, sibling, src=x_ref)]
        first += [copy(1 + j, me, (*chip, c), src=x_ref) for j, chip in enumerate(chips)]
        for cp in first:
            cp.start()
        passed = [copy(4 + j, (*chip, c), sibling) for j, chip in enumerate(chips)]
        for j, chip in enumerate(chips):
            copy(1 + j, (*chip, c), me).wait_recv()
            passed[j].start()
        copy(0, sibling, me).wait_recv()
        for j, chip in enumerate(chips):
            copy(4 + j, (*chip, 1 - c), me).wait_recv()
        for cp in first + passed:
            cp.wait_send()
        mine.wait()

    return pl.pallas_call(
        body, name=name, out_shape=jax.ShapeDtypeStruct((N_DEV * m_per, n), block.dtype),
        in_specs=[pl.BlockSpec(memory_space=pltpu.VMEM)], out_specs=pl.BlockSpec(memory_space=pltpu.VMEM),
        scratch_shapes=[pltpu.SemaphoreType.DMA((7,)), pltpu.SemaphoreType.DMA((7,)), pltpu.SemaphoreType.DMA],
        compiler_params=pltpu.CompilerParams(vmem_limit_bytes=VMEM_LIMIT_BYTES))(block)


def _exchange(srcs, dst_shapes, jobs, name):
    n_src, n_job = len(srcs), len(jobs)

    def body(*refs):
        src_refs, dst_refs = refs[:n_src], refs[n_src:n_src + len(dst_shapes)]
        send_sems, recv_sems, local_sems = refs[n_src + len(dst_shapes):]
        x, y, c = _mesh_pos()
        me = 4 * x + 2 * y + c
        pending = []
        for t, (si, di, src_slice, dst_slice) in enumerate(jobs):
            src, dst = src_refs[si], dst_refs[di]
            lc = pltpu.make_async_copy(src_slice(src, me), dst_slice(dst, me), local_sems.at[t])
            lc.start()
            pending.append(lc)
            for dd in range(1, N_DEV):
                px = 1 - x if dd & 4 else x
                py = 1 - y if dd & 2 else y
                pc = 1 - c if dd & 1 else c
                cp = pltpu.make_async_remote_copy(
                    src_ref=src_slice(src, 4 * px + 2 * py + pc), dst_ref=dst_slice(dst, me),
                    send_sem=send_sems.at[t, dd - 1], recv_sem=recv_sems.at[t, dd - 1],
                    device_id=(px, py, pc), device_id_type=MESH)
                cp.start()
                pending.append(cp)
        for cp in pending:
            cp.wait()

    hbm = pl.BlockSpec(memory_space=pl.ANY)
    return pl.pallas_call(
        body, name=name, out_shape=list(dst_shapes), in_specs=[hbm] * n_src, out_specs=[hbm] * len(dst_shapes),
        scratch_shapes=[pltpu.SemaphoreType.DMA((n_job, N_DEV - 1)), pltpu.SemaphoreType.DMA((n_job, N_DEV - 1)),
                        pltpu.SemaphoreType.DMA((n_job,))])(*srcs)


def _whole(ref, p):
    return ref


def _cols_of(width):
    def take(ref, p):
        lead = (slice(None),) * (len(ref.shape) - 1)
        return ref.at[lead + (pl.ds(pl.multiple_of(p * width, LANES), width),)]
    return take


def _rows_of(height):
    def take(ref, p):
        lead = (slice(None),) * (len(ref.shape) - 2)
        return ref.at[lead + (pl.ds(pl.multiple_of(p * height, SUBLANES), height), slice(None))]
    return take


def _slot(layer=None):
    if layer is None:
        return lambda ref, p: ref.at[p]
    return lambda ref, p: ref.at[p, layer]


def _local_step(x0, target, mod, g_mix_pre, g_mix_post, g_ffn_pre, g_ffn_post, wqkv, wo, wg, wu, wd,
                wfg, bfg, wconv, bconv, tiles):
    s, d = x0.shape
    nl = wqkv.shape[0]
    n_heads = d // HEAD_DIM
    hp_n = d // LANES
    vec = lambda a: a.reshape(1, -1)
    saved = []
    xcur = x0
    for l in range(nl):
        sh_a, sc_a, gt_a, sh_f, sc_f, gt_f = (vec(mod[l, i * d:(i + 1) * d]) for i in range(6))
        fox = l % 2 == 1
        h1 = _norm_mod_fwd(xcur, vec(g_mix_pre[l]), sh_a, sc_a, f"norm_mix_fwd_{l}")
        qkv = _mm_nn(h1, wqkv, l, BF16, f"qkv_fwd_{l}", split_out=d)
        if fox:
            j = l // 2
            fl = _mm_nn(h1, wfg, j, F32, f"fgate_fwd_{l}")
            cum = _fox_prep(fl, bfg[j], f"fox_prep_{l}")
            cum_h = cum[:, :n_heads]
            cq = jnp.repeat(cum_h.reshape(s, hp_n, 2).transpose(1, 0, 2), HEAD_DIM, axis=2)
            ck = cum_h.T.reshape(hp_n, 2, s)
            o, stat = _fox_fwd(qkv, cq, ck, *tiles["fox_fwd"], f"fox_fwd_{l}")
            extra = (fl, cq, ck)
        else:
            o, stat = _sb_fwd(qkv, *tiles["sb_fwd"], f"sb_fwd_{l}")
            extra = None
        u = _mm_nn(o, wo, l, F32, f"attn_out_fwd_{l}")
        x2 = _post_fwd(xcur, u, vec(g_mix_post[l]), gt_a, f"post_mix_fwd_{l}")
        h2 = _norm_mod_fwd(x2, vec(g_ffn_pre[l]), sh_f, sc_f, f"norm_ffn_fwd_{l}")
        gp = _mm_nn(h2, wg, l, F32, f"ffn_gate_fwd_{l}")
        up = _mm_nn(h2, wu, l, F32, f"ffn_up_fwd_{l}")
        act = _conv_act_fwd(gp, up, wconv[l], bconv[l], f"conv_act_fwd_{l}")
        yv = _mm_nn(act, wd, l, F32, f"ffn_down_fwd_{l}")
        x3 = _post_fwd(x2, yv, vec(g_ffn_post[l]), gt_f, f"post_ffn_fwd_{l}")
        saved.append((xcur, h1, qkv, o, stat, extra, u, x2, h2, gp, up, act, yv))
        xcur = x3

    dx, sq = _loss_head(xcur, target, "loss_head")
    small, big = [None] * nl, [None] * nl
    for l in reversed(range(nl)):
        xin, h1, qkv, o, stat, extra, u, x2, h2, gp, up, act, yv = saved[l]
        sc_a, gt_a, sc_f, gt_f = (vec(mod[l, i * d:(i + 1) * d]) for i in (1, 2, 4, 5))
        fox = l % 2 == 1
        dy, dgt_f, dg4 = _post_bwd(dx, yv, vec(g_ffn_post[l]), gt_f, f"post_ffn_bwd_{l}")
        dact = _mm_nt(dy, wd, l, F32, f"ffn_down_dx_{l}")
        dwd = _mm_tn(act, dy, BF16, f"ffn_down_dw_{l}")
        dgp, dup, dwc, dbc = _conv_act_bwd(dact, gp, up, wconv[l], bconv[l], f"conv_act_bwd_{l}")
        dh2 = _mm_nt(dgp, wg, l, F32, f"ffn_gate_dx_{l}")
        dh2 = _mm_nt(dup, wu, l, F32, f"ffn_up_dx_{l}", add=dh2)
        dwg = _mm_tn(h2, dgp, BF16, f"ffn_gate_dw_{l}")
        dwu = _mm_tn(h2, dup, BF16, f"ffn_up_dw_{l}")
        dx2, dsh_f, dsc_f, dg3 = _norm_mod_bwd(dh2, x2, vec(g_ffn_pre[l]), sc_f, dx, f"norm_ffn_bwd_{l}")
        du, dgt_a, dg2 = _post_bwd(dx2, u, vec(g_mix_post[l]), gt_a, f"post_mix_bwd_{l}")
        do = _mm_nt(du, wo, l, BF16, f"attn_out_dx_{l}")
        dwo = _mm_tn(o, du, BF16, f"attn_out_dw_{l}")
        if fox:
            j = l // 2
            fl, cq, ck = extra
            dqkv, dcq, dck = _fox_bwd(qkv, o, do, stat, cq, ck, *tiles["fox_bwd"], f"fox_bwd_{l}")
            dcq = jnp.max(dcq.reshape(hp_n, s, 2, HEAD_DIM), axis=3)
            dcum = dcq.transpose(1, 0, 2).reshape(s, n_heads) + dck.reshape(n_heads, s).T
            dcum = jnp.pad(dcum, ((0, 0), (0, LANES - n_heads)))
            dfl, dbfg = _fox_gate_bwd(dcum, fl, bfg[j], n_heads, f"fox_gate_bwd_{l}")
            dh1 = _mm_nt(dfl, wfg, j, F32, f"fgate_dx_{l}")
            dh1 = _mm_nt(dqkv, wqkv, l, F32, f"qkv_dx_{l}", add=dh1)
            dwfg = _mm_tn(h1, dfl, F32, f"fgate_dw_{l}")[:, :n_heads]
            dbfg = dbfg[0, :n_heads]
        else:
            dqkv = _sb_bwd(qkv, stat, do, *tiles["sb_bwd"], f"sb_bwd_{l}")
            dh1 = _mm_nt(dqkv, wqkv, l, F32, f"qkv_dx_{l}")
            dwfg = dbfg = None
        dwqkv = _mm_tn(h1, dqkv, BF16, f"qkv_dw_{l}")
        dx, dsh_a, dsc_a, dg1 = _norm_mod_bwd(dh1, xin, vec(g_mix_pre[l]), sc_a, dx2, f"norm_mix_bwd_{l}")
        dmod = jnp.concatenate([dsh_a, dsc_a, dgt_a, dsh_f, dsc_f, dgt_f], axis=1)[0]
        small[l] = dict(dmod=dmod, dg1=dg1[0], dg2=dg2[0], dg3=dg3[0], dg4=dg4[0], dbc=dbc[0], dwc=dwc,
                        dbfg=dbfg, dwfg=dwfg)
        big[l] = dict(qkv=dwqkv, o=dwo, gate=dwg, up=dwu, down=dwd)
    return sq, dx, small, big


def _rows128(a, rows):
    flat = a.reshape(-1)
    return jnp.pad(flat, (0, rows * LANES - flat.shape[0])).reshape(rows, LANES)


def _ceil8(n_elems):
    rows = -(-n_elems // LANES)
    return -(-rows // SUBLANES) * SUBLANES


def kernel(x, c, w_mod, b_mod, g_mix_pre, g_mix_post, w_qkv, w_o, w_fg, b_fg, g_ffn_pre, g_ffn_post, w_ffn_gate, w_ffn_up, w_conv, b_conv, w_ffn_down, loss_target, m_w_mod, m_b_mod, m_g_mix_pre, m_g_mix_post, m_w_qkv, m_w_o, m_w_fg, m_b_fg, m_g_ffn_pre, m_g_ffn_post, m_w_ffn_gate, m_w_ffn_up, m_w_conv, m_b_conv, m_w_ffn_down, v_w_mod, v_b_mod, v_g_mix_pre, v_g_mix_post, v_w_qkv, v_w_o, v_w_fg, v_b_fg, v_g_ffn_pre, v_g_ffn_post, v_w_ffn_gate, v_w_ffn_up, v_w_conv, v_b_conv, v_w_ffn_down):
    _, s, d = x.shape
    nl = w_qkv.shape[0]
    nf = w_fg.shape[0]
    n_heads = w_fg.shape[2]
    fs = w_ffn_gate.shape[2]
    fp = -(-fs // LANES) * LANES
    f_full, f_pad = N_DEV * fs, N_DEV * fp
    mod_cols = w_mod.shape[2]
    qs, orows = w_qkv.shape[2], w_o.shape[1]
    xi, yi, ci = _mesh_pos()
    me = 4 * xi + 2 * yi + ci

    c_rows = d // LANES
    c_all = _allgather_small(jnp.pad(c.reshape(1, d), ((0, SUBLANES - 1), (0, 0))).reshape(SUBLANES * c_rows, LANES),
                             "gather_cond")
    c_all = c_all.reshape(N_DEV, SUBLANES, d)[:, 0, :]
    b_mod_cols = lax.dynamic_slice_in_dim(b_mod, me * mod_cols, mod_cols, axis=1).reshape(nl, 1, mod_cols)
    mod_part = _mod_fwd(c_all, w_mod, b_mod_cols, "mod_fwd")

    conv_pad = jnp.pad(w_conv, ((0, 0), (0, 0), (0, fp - fs)))
    r_mod, r_conv, r_fg = _ceil8(mod_part.size), _ceil8(conv_pad.size), _ceil8(w_fg.size)
    payload = jnp.concatenate([_rows128(mod_part, r_mod), _rows128(conv_pad, r_conv), _rows128(w_fg, r_fg)], axis=0)
    got = _allgather_small(payload, "gather_small_weights").reshape(N_DEV, r_mod + r_conv + r_fg, LANES)
    mod_g = got[:, :r_mod].reshape(N_DEV, -1)[:, :mod_part.size].reshape(N_DEV, nl, N_DEV, mod_cols)
    mod = lax.dynamic_index_in_dim(mod_g, me, axis=2, keepdims=False).transpose(1, 0, 2).reshape(nl, N_DEV * mod_cols)
    conv_g = got[:, r_mod:r_mod + r_conv].reshape(N_DEV, -1)[:, :conv_pad.size].reshape(N_DEV, nl, 3, fp)
    wconv_full = conv_g.transpose(1, 2, 0, 3).reshape(nl, 3, f_pad)
    fg_g = got[:, r_mod + r_conv:].reshape(N_DEV, -1)[:, :w_fg.size].reshape(N_DEV, nf, orows, n_heads)
    wfg_full = fg_g.transpose(1, 0, 2, 3).reshape(nf, d, n_heads)
    wfg_full = jnp.pad(wfg_full, ((0, 0), (0, 0), (0, LANES - n_heads))).astype(BF16)
    bfg_full = jnp.pad(b_fg, ((0, 0), (0, LANES - n_heads))).reshape(nf, 1, LANES)
    bconv_full = jnp.pad(b_conv.reshape(nl, N_DEV, fs), ((0, 0), (0, 0), (0, fp - fs))).reshape(nl, 1, f_pad)

    gate_sh = jnp.pad(w_ffn_gate, ((0, 0), (0, 0), (0, fp - fs))).astype(BF16)
    up_sh = jnp.pad(w_ffn_up, ((0, 0), (0, 0), (0, fp - fs))).astype(BF16)
    down_sh = jnp.pad(w_ffn_down, ((0, 0), (0, fp - fs), (0, 0))).astype(BF16)
    shards = [w_qkv.astype(BF16), w_o.astype(BF16), gate_sh, up_sh, down_sh]
    full_shapes = [jax.ShapeDtypeStruct((nl, d, N_DEV * qs), BF16), jax.ShapeDtypeStruct((nl, d, d), BF16),
                   jax.ShapeDtypeStruct((nl, d, f_pad), BF16), jax.ShapeDtypeStruct((nl, d, f_pad), BF16),
                   jax.ShapeDtypeStruct((nl, f_pad, d), BF16)]
    place = [_cols_of(qs), _rows_of(orows), _cols_of(fp), _cols_of(fp), _rows_of(fp)]
    wqkv, wo, wg, wu, wd = _exchange(shards, full_shapes, [(i, i, _whole, place[i]) for i in range(5)],
                                     "gather_weights")

    sq, dx, small, big = _local_step(x[0], loss_target[0], mod, g_mix_pre, g_mix_post, g_ffn_pre, g_ffn_post,
                                     wqkv, wo, wg, wu, wd, wfg_full, bfg_full, wconv_full, bconv_full, ATTN_TILES)
    loss = lax.psum(0.5 * jnp.sum(sq) / d, ("x", "y", "c"))

    order = ["qkv", "o", "gate", "up", "down"]
    send = {"qkv": _cols_of(qs), "o": _rows_of(orows), "gate": _cols_of(fp), "up": _cols_of(fp), "down": _rows_of(fp)}
    recv_shapes = [jax.ShapeDtypeStruct((N_DEV, nl, d, qs), BF16), jax.ShapeDtypeStruct((N_DEV, nl, orows, d), BF16),
                   jax.ShapeDtypeStruct((N_DEV, nl, d, fp), BF16), jax.ShapeDtypeStruct((N_DEV, nl, d, fp), BF16),
                   jax.ShapeDtypeStruct((N_DEV, nl, fp, d), BF16)]
    srcs, jobs = [], []
    for l in range(nl):
        for wi, nm in enumerate(order):
            jobs.append((len(srcs), wi, send[nm], _slot(l)))
            srcs.append(big[l][nm])
    recv = _exchange(srcs, recv_shapes, jobs, "scatter_weight_grads")
    upd = {}
    for nm, rv, wt, mt, vt in zip(order, recv, [w_qkv, w_o, w_ffn_gate, w_ffn_up, w_ffn_down],
                                  [m_w_qkv, m_w_o, m_w_ffn_gate, m_w_ffn_up, m_w_ffn_down],
                                  [v_w_qkv, v_w_o, v_w_ffn_gate, v_w_ffn_up, v_w_ffn_down]):
        upd[nm] = _adamw(rv, wt, mt, vt, f"adamw_{nm}")

    stack = lambda key: jnp.stack([small[l][key] for l in range(nl)])
    dmod = stack("dmod")
    dgs = [stack(k) for k in ("dg1", "dg2", "dg3", "dg4")]
    dbc = stack("dbc").reshape(nl, N_DEV, fp)[:, :, :fs].reshape(nl, f_full)
    dbfg = jnp.stack([small[l]["dbfg"] for l in range(nl) if l % 2 == 1])
    dwc = stack("dwc").reshape(nl, 3, N_DEV, fp)[:, :, :, :fs].reshape(nl, 3, f_full)
    dwfg = jnp.stack([small[l]["dwfg"] for l in range(nl) if l % 2 == 1])
    rep_parts = [dmod] + dgs + [dbc, dbfg]
    rep_rows = [_ceil8(p.size) for p in rep_parts]
    r_rep, r_wc, r_wfg = sum(rep_rows), _ceil8(dwc.size), _ceil8(dwfg.size)
    payload = jnp.concatenate([_rows128(p, r) for p, r in zip(rep_parts, rep_rows)]
                              + [_rows128(dwc, r_wc), _rows128(dwfg, r_wfg)], axis=0)
    gsm = _allgather_small(payload, "gather_small_grads").reshape(N_DEV, 1, r_rep + r_wc + r_wfg, LANES)

    def pack(parts):
        return jnp.concatenate([_rows128(p, r) for p, r in zip(parts, rep_rows)], axis=0).reshape(1, r_rep, LANES)

    rep_w = [b_mod, g_mix_pre, g_mix_post, g_ffn_pre, g_ffn_post, b_conv, b_fg]
    rep_m = [m_b_mod, m_g_mix_pre, m_g_mix_post, m_g_ffn_pre, m_g_ffn_post, m_b_conv, m_b_fg]
    rep_v = [v_b_mod, v_g_mix_pre, v_g_mix_post, v_g_ffn_pre, v_g_ffn_post, v_b_conv, v_b_fg]
    rep_out = _adamw(gsm[:, :, :r_rep], pack(rep_w), pack(rep_m), pack(rep_v), "adamw_replicated", tr=r_rep, tc=LANES)

    def unpack(packed):
        outs, at = [], 0
        for p, r in zip(rep_w, rep_rows):
            outs.append(packed[0, at:at + r].reshape(-1)[:p.size].reshape(p.shape))
            at += r
        return outs

    rep_g, rep_d, rep_nm, rep_nv = (unpack(a) for a in rep_out)

    wc_all = gsm[:, 0, r_rep:r_rep + r_wc].reshape(N_DEV, -1)[:, :dwc.size].reshape(N_DEV, 1, nl * 3, f_full)
    wc_mine = lax.dynamic_slice_in_dim(wc_all, me * fs, fs, axis=3)
    wc_out = _adamw(wc_mine, w_conv.reshape(1, nl * 3, fs), m_w_conv.reshape(1, nl * 3, fs),
                    v_w_conv.reshape(1, nl * 3, fs), "adamw_conv", tr=nl * 3, tc=fs)
    wc_out = [a.reshape(nl, 3, fs) for a in wc_out]
    wfg_all = gsm[:, 0, r_rep + r_wc:].reshape(N_DEV, -1)[:, :dwfg.size].reshape(N_DEV, nf, d, n_heads)
    wfg_mine = lax.dynamic_slice_in_dim(wfg_all, me * orows, orows, axis=2)
    wfg_out = _adamw(wfg_mine, w_fg, m_w_fg, v_w_fg, "adamw_fgate", tr=orows, tc=n_heads)

    dmod_all = gsm[:, 0, :rep_rows[0]].reshape(N_DEV, -1)[:, :dmod.size].reshape(N_DEV, nl, N_DEV * mod_cols)
    dmod_mine = lax.dynamic_slice_in_dim(dmod_all, me * mod_cols, mod_cols, axis=2)
    gwmod = _wmod_grad(c_all.T, dmod_mine, "wmod_grad")
    wmod_out = _adamw(gwmod.reshape(1, nl, d, mod_cols), w_mod, m_w_mod, v_w_mod, "adamw_mod")

    per_weight = [wmod_out, None, None, None, upd["qkv"], upd["o"], wfg_out, None, None, None,
                  upd["gate"], upd["up"], wc_out, None, upd["down"]]
    rep_index = {1: 0, 2: 1, 3: 2, 8: 3, 9: 4, 13: 5, 7: 6}
    outs = [[], [], [], []]
    for pos, res in enumerate(per_weight):
        for kind in range(4):
            if res is None:
                outs[kind].append((rep_g, rep_d, rep_nm, rep_nv)[kind][rep_index[pos]])
            else:
                outs[kind].append(res[kind])
    return (loss, dx.reshape(1, s, d), *outs[0], *outs[1], *outs[2], *outs[3])
```

```python
import functools

import jax
import jax.numpy as jnp
from jax import lax
from jax.experimental import pallas as pl
from jax.experimental.pallas import tpu as pltpu

F32 = jnp.float32
BF16 = jnp.bfloat16

N_DEV = 8
HEAD_DIM = 64
LANES = 128
SUBLANES = 8
RMS_EPS = 1e-6
ADAM_LR = 0.001
ADAM_B1 = 0.9
ADAM_B2 = 0.999
ADAM_EPS = 1e-08
ADAM_WD = 0.01
ADAM_STEP = 10
VMEM_LIMIT_BYTES = 56 * 1024 * 1024
MM_ROWS = 1024
ATTN_TILES = {"sb_fwd": (512, 256), "sb_bwd": (512, 256), "fox_fwd": (512, 256), "fox_bwd": (256, 256)}

NN = (((1,), (0,)), ((), ()))
NT = (((1,), (1,)), ((), ()))
TN = (((0,), (0,)), ((), ()))
MESH = pl.DeviceIdType.MESH


def _params(*sem):
    return pltpu.CompilerParams(dimension_semantics=sem, vmem_limit_bytes=VMEM_LIMIT_BYTES)


def _pick(n, pref, quantum):
    if n <= pref:
        return n
    t = (pref // quantum) * quantum
    while n % t:
        t -= quantum
    return t


def _mm(a, b, *, dims, grid, a_spec, b_spec, o_spec, out_shape, name, add=None):
    nk = grid[2]
    acc_shape = tuple(d for d in o_spec.block_shape if d is not None)
    o_dtype = out_shape.dtype

    def body(*refs):
        if add is None:
            a_ref, b_ref, o_ref, acc_ref = refs
            add_ref = None
        else:
            a_ref, b_ref, add_ref, o_ref, acc_ref = refs
        k = pl.program_id(2)
        part = lax.dot_general(a_ref[...], b_ref[...], dims, preferred_element_type=F32)

        def finish(total):
            if add_ref is not None:
                total = total + add_ref[...]
            o_ref[...] = total.astype(o_dtype)

        if nk == 1:
            finish(part)
            return

        @pl.when(k == 0)
        def _():
            acc_ref[...] = part

        @pl.when(jnp.logical_and(k > 0, k < nk - 1))
        def _():
            acc_ref[...] += part

        @pl.when(k == nk - 1)
        def _():
            finish(acc_ref[...] + part)

    operands = [a, b] if add is None else [a, b, add]
    in_specs = [a_spec, b_spec] if add is None else [a_spec, b_spec, o_spec]
    return pl.pallas_call(
        body, name=name, grid=grid, in_specs=in_specs, out_specs=o_spec, out_shape=out_shape,
        scratch_shapes=[pltpu.VMEM(acc_shape, F32)],
        compiler_params=_params("parallel", "parallel", "arbitrary"),
    )(*operands)


def _mm_nn(a, w, l, out_dtype, name, *, col0=0, n=None, split_out=None):
    m, kdim = a.shape
    n = w.shape[2] if n is None else n
    tm, tk = _pick(m, MM_ROWS, SUBLANES), _pick(kdim, 1024, LANES)
    tn = _pick(n if split_out is None else split_out, 1024, LANES)
    jb = col0 // tn
    grid = (m // tm, n // tn, kdim // tk)
    a_spec = pl.BlockSpec((tm, tk), lambda i, j, k: (i, k))
    b_spec = pl.BlockSpec((None, tk, tn), lambda i, j, k: (l, k, j + jb))
    if split_out is None:
        o_spec = pl.BlockSpec((tm, tn), lambda i, j, k: (i, j))
        shape = jax.ShapeDtypeStruct((m, n), out_dtype)
    else:
        nj1 = split_out // tn
        o_spec = pl.BlockSpec((None, tm, tn), lambda i, j, k: (j // nj1, i, j % nj1))
        shape = jax.ShapeDtypeStruct((n // split_out, m, split_out), out_dtype)
    return _mm(a, w, dims=NN, grid=grid, a_spec=a_spec, b_spec=b_spec, o_spec=o_spec, out_shape=shape, name=name)


def _mm_nt(a, w, l, out_dtype, name, *, add=None):
    n, kdim = w.shape[1], w.shape[2]
    if a.ndim == 2:
        m = a.shape[0]
        tk = _pick(kdim, 1024, LANES)
        a_spec_of = lambda tm: pl.BlockSpec((tm, tk), lambda i, j, k: (i, k))
    else:
        m, seg = a.shape[1], a.shape[2]
        tk = _pick(seg, 1024, LANES)
        nk1 = seg // tk
        a_spec_of = lambda tm: pl.BlockSpec((None, tm, tk), lambda i, j, k: (k // nk1, i, k % nk1))
    tm, tn = _pick(m, MM_ROWS, SUBLANES), _pick(n, 1024, LANES)
    grid = (m // tm, n // tn, kdim // tk)
    b_spec = pl.BlockSpec((None, tn, tk), lambda i, j, k: (l, j, k))
    o_spec = pl.BlockSpec((tm, tn), lambda i, j, k: (i, j))
    return _mm(a, w, dims=NT, grid=grid, a_spec=a_spec_of(tm), b_spec=b_spec, o_spec=o_spec,
               out_shape=jax.ShapeDtypeStruct((m, n), out_dtype), name=name, add=add)


def _mm_tn(a, b, out_dtype, name):
    kdim, m = a.shape
    tk, tm = _pick(kdim, 1024, SUBLANES), _pick(m, MM_ROWS, LANES)
    if b.ndim == 2:
        n = b.shape[1]
        tn = _pick(n, 1024, LANES)
        b_spec = pl.BlockSpec((tk, tn), lambda i, j, k: (k, j))
    else:
        seg = b.shape[2]
        n = b.shape[0] * seg
        tn = _pick(seg, 1024, LANES)
        nj1 = seg // tn
        b_spec = pl.BlockSpec((None, tk, tn), lambda i, j, k: (j // nj1, k, j % nj1))
    grid = (m // tm, n // tn, kdim // tk)
    a_spec = pl.BlockSpec((tk, tm), lambda i, j, k: (k, i))
    o_spec = pl.BlockSpec((tm, tn), lambda i, j, k: (i, j))
    return _mm(a, b, dims=TN, grid=grid, a_spec=a_spec, b_spec=b_spec, o_spec=o_spec,
               out_shape=jax.ShapeDtypeStruct((m, n), out_dtype), name=name)


def _rstd(x):
    return lax.rsqrt(jnp.mean(x * x, axis=-1, keepdims=True) + RMS_EPS)


def _norm_mod_fwd(x, g, shift, scale, name):
    s, d = x.shape
    tm = _pick(s, 512, SUBLANES)

    def body(x_ref, g_ref, sh_ref, sc_ref, h_ref):
        xv = x_ref[...]
        y = (xv * _rstd(xv)) * g_ref[...]
        h_ref[...] = (y * (1.0 + sc_ref[...]) + sh_ref[...]).astype(BF16)

    row = pl.BlockSpec((tm, d), lambda i: (i, 0))
    vec = pl.BlockSpec((1, d), lambda i: (0, 0))
    return pl.pallas_call(body, name=name, grid=(s // tm,), in_specs=[row, vec, vec, vec], out_specs=row,
                          out_shape=jax.ShapeDtypeStruct((s, d), BF16), compiler_params=_params("parallel"))(x, g, shift, scale)


def _norm_mod_bwd(dh, x, g, scale, dres, name):
    s, d = x.shape
    tm = _pick(s, 256, SUBLANES)

    def body(dh_ref, x_ref, g_ref, sc_ref, dres_ref, dx_ref, dsh_ref, dsc_ref, dg_ref):
        i = pl.program_id(0)
        xv, dhv, gv = x_ref[...], dh_ref[...], g_ref[...]
        r = _rstd(xv)
        xh = xv * r
        dn = dhv * (1.0 + sc_ref[...])
        gd = dn * gv
        dx_ref[...] = dres_ref[...] + r * (gd - xh * jnp.mean(gd * xh, axis=-1, keepdims=True))

        @pl.when(i == 0)
        def _():
            dsh_ref[...] = jnp.zeros_like(dsh_ref)
            dsc_ref[...] = jnp.zeros_like(dsc_ref)
            dg_ref[...] = jnp.zeros_like(dg_ref)

        dsh_ref[...] += jnp.sum(dhv, axis=0, keepdims=True)
        dsc_ref[...] += jnp.sum(dhv * (xh * gv), axis=0, keepdims=True)
        dg_ref[...] += jnp.sum(dn * xh, axis=0, keepdims=True)

    row = pl.BlockSpec((tm, d), lambda i: (i, 0))
    vec = pl.BlockSpec((1, d), lambda i: (0, 0))
    vshape = jax.ShapeDtypeStruct((1, d), F32)
    return pl.pallas_call(body, name=name, grid=(s // tm,), in_specs=[row, row, vec, vec, row],
                          out_specs=[row, vec, vec, vec],
                          out_shape=[jax.ShapeDtypeStruct((s, d), F32), vshape, vshape, vshape],
                          compiler_params=_params("arbitrary"))(dh, x, g, scale, dres)


def _post_fwd(x, u, g, gate, name):
    s, d = x.shape
    tm = _pick(s, 512, SUBLANES)

    def body(x_ref, u_ref, g_ref, gt_ref, o_ref):
        uv = u_ref[...]
        o_ref[...] = x_ref[...] + gt_ref[...] * ((uv * _rstd(uv)) * g_ref[...])

    row = pl.BlockSpec((tm, d), lambda i: (i, 0))
    vec = pl.BlockSpec((1, d), lambda i: (0, 0))
    return pl.pallas_call(body, name=name, grid=(s // tm,), in_specs=[row, row, vec, vec], out_specs=row,
                          out_shape=jax.ShapeDtypeStruct((s, d), F32), compiler_params=_params("parallel"))(x, u, g, gate)


def _post_bwd(dx, u, g, gate, name):
    s, d = u.shape
    tm = _pick(s, 256, SUBLANES)

    def body(dx_ref, u_ref, g_ref, gt_ref, du_ref, dgt_ref, dg_ref):
        i = pl.program_id(0)
        uv, dxv, gv = u_ref[...], dx_ref[...], g_ref[...]
        r = _rstd(uv)
        uh = uv * r
        dn = dxv * gt_ref[...]
        gd = dn * gv
        du_ref[...] = (r * (gd - uh * jnp.mean(gd * uh, axis=-1, keepdims=True))).astype(BF16)

        @pl.when(i == 0)
        def _():
            dgt_ref[...] = jnp.zeros_like(dgt_ref)
            dg_ref[...] = jnp.zeros_like(dg_ref)

        dgt_ref[...] += jnp.sum(dxv * (uh * gv), axis=0, keepdims=True)
        dg_ref[...] += jnp.sum(dn * uh, axis=0, keepdims=True)

    row = pl.BlockSpec((tm, d), lambda i: (i, 0))
    vec = pl.BlockSpec((1, d), lambda i: (0, 0))
    vshape = jax.ShapeDtypeStruct((1, d), F32)
    return pl.pallas_call(body, name=name, grid=(s // tm,), in_specs=[row, row, vec, vec],
                          out_specs=[row, vec, vec],
                          out_shape=[jax.ShapeDtypeStruct((s, d), BF16), vshape, vshape],
                          compiler_params=_params("arbitrary"))(dx, u, g, gate)


def _shift_rows(cur, prev8, k):
    rolled = pltpu.roll(cur, k, axis=0)
    rolled_prev = pltpu.roll(prev8, k, axis=0)
    i8 = lax.broadcasted_iota(jnp.int32, prev8.shape, 0)
    top = jnp.where(i8 < k, rolled_prev, rolled[:SUBLANES])
    return jnp.concatenate([top, rolled[SUBLANES:]], axis=0)


def _conv_pre(g, prev8, wc_ref, bc_ref):
    s1 = _shift_rows(g, prev8, 1)
    s2 = _shift_rows(g, prev8, 2)
    gc = bc_ref[...] + wc_ref[0:1, :] * s2 + wc_ref[1:2, :] * s1 + wc_ref[2:3, :] * g
    return gc, s1, s2


def _conv_act_fwd(gp, up, wc, bc, name):
    s, f = gp.shape
    tm, tf = _pick(s, 512, SUBLANES), _pick(f, 768, LANES)
    r8 = tm // SUBLANES

    def body(g_ref, gprev_ref, up_ref, wc_ref, bc_ref, a_ref):
        i = pl.program_id(1)
        prev = jnp.where(i == 0, 0.0, gprev_ref[...])
        gc, _, _ = _conv_pre(g_ref[...], prev, wc_ref, bc_ref)
        a_ref[...] = ((gc * jax.nn.sigmoid(gc)) * up_ref[...]).astype(BF16)

    tile = pl.BlockSpec((tm, tf), lambda j, i: (i, j))
    prev = pl.BlockSpec((SUBLANES, tf), lambda j, i: (jnp.maximum(i * r8 - 1, 0), j))
    return pl.pallas_call(body, name=name, grid=(f // tf, s // tm),
                          in_specs=[tile, prev, tile, pl.BlockSpec((3, tf), lambda j, i: (0, j)),
                                    pl.BlockSpec((1, tf), lambda j, i: (0, j))],
                          out_specs=tile, out_shape=jax.ShapeDtypeStruct((s, f), BF16),
                          compiler_params=_params("parallel", "parallel"))(gp, gp, up, wc, bc)


def _conv_act_bwd(da, gp, up, wc, bc, name):
    s, f = gp.shape
    tm, tf = _pick(s, 256, SUBLANES), _pick(f, 768, LANES)
    r8 = tm // SUBLANES
    nrow = s // tm

    def body(da_ref, dan_ref, g_ref, gprev_ref, gn_ref, up_ref, upn_ref, wc_ref, bc_ref,
             dgp_ref, dup_ref, dwc_ref, dbc_ref):
        i = pl.program_id(1)
        last = i == nrow - 1
        prev = jnp.where(i == 0, 0.0, gprev_ref[...])
        g_ext = jnp.concatenate([g_ref[...], gn_ref[...]], axis=0)
        up_ext = jnp.concatenate([up_ref[...], upn_ref[...]], axis=0)
        da_ext = jnp.concatenate([da_ref[...], jnp.where(last, 0.0, dan_ref[...])], axis=0)
        gc, s1, s2 = _conv_pre(g_ext, prev, wc_ref, bc_ref)
        sg = jax.nn.sigmoid(gc)
        dup_ref[...] = (da_ext * (gc * sg))[:tm].astype(BF16)
        dgc = da_ext * up_ext * (sg * (1.0 + gc * (1.0 - sg)))
        ext = tm + SUBLANES
        dgp = (wc_ref[2:3, :] * dgc + wc_ref[1:2, :] * pltpu.roll(dgc, ext - 1, axis=0)
               + wc_ref[0:1, :] * pltpu.roll(dgc, ext - 2, axis=0))
        dgp_ref[...] = dgp[:tm].astype(BF16)

        @pl.when(i == 0)
        def _():
            dwc_ref[...] = jnp.zeros_like(dwc_ref)
            dbc_ref[...] = jnp.zeros_like(dbc_ref)

        d0 = dgc[:tm]
        dwc_ref[0:1, :] += jnp.sum(d0 * s2[:tm], axis=0, keepdims=True)
        dwc_ref[1:2, :] += jnp.sum(d0 * s1[:tm], axis=0, keepdims=True)
        dwc_ref[2:3, :] += jnp.sum(d0 * g_ext[:tm], axis=0, keepdims=True)
        dbc_ref[...] += jnp.sum(d0, axis=0, keepdims=True)

    tile = pl.BlockSpec((tm, tf), lambda j, i: (i, j))
    prev = pl.BlockSpec((SUBLANES, tf), lambda j, i: (jnp.maximum(i * r8 - 1, 0), j))
    nxt = pl.BlockSpec((SUBLANES, tf), lambda j, i: (jnp.minimum((i + 1) * r8, s // SUBLANES - 1), j))
    return pl.pallas_call(body, name=name, grid=(f // tf, nrow),
                          in_specs=[tile, nxt, tile, prev, nxt, tile, nxt,
                                    pl.BlockSpec((3, tf), lambda j, i: (0, j)), pl.BlockSpec((1, tf), lambda j, i: (0, j))],
                          out_specs=[tile, tile, pl.BlockSpec((3, tf), lambda j, i: (0, j)),
                                     pl.BlockSpec((1, tf), lambda j, i: (0, j))],
                          out_shape=[jax.ShapeDtypeStruct((s, f), BF16), jax.ShapeDtypeStruct((s, f), BF16),
                                     jax.ShapeDtypeStruct((3, f), F32), jax.ShapeDtypeStruct((1, f), F32)],
                          compiler_params=_params("parallel", "arbitrary"))(da, da, gp, gp, gp, up, up, wc, bc)


def _split_bf16(v, parts):
    out, rem = [], v
    for _ in range(parts):
        t = rem.astype(BF16)
        out.append(t)
        rem = rem - t.astype(F32)
    return jnp.concatenate(out, axis=1)


def _tri(t, cmp, reps):
    row = lax.broadcasted_iota(jnp.int32, (t, t), 0)
    col = lax.broadcasted_iota(jnp.int32, (t, t), 1)
    m = cmp(row, col).astype(BF16)
    return jnp.concatenate([m] * reps, axis=0)


def _log_sigmoid(z):
    mn = jnp.minimum(z, 0.0)
    return mn - jnp.log(1.0 + jnp.exp(mn + (mn - z)))


def _log_one_minus_sigmoid(z):
    mn = jnp.minimum(z, 0.0)
    neg = mn - z
    return neg - jnp.log(1.0 + jnp.exp(mn + neg))


def _pair(xv):
    lane = lax.broadcasted_iota(jnp.int32, xv.shape, 1)
    zero = jnp.zeros_like(xv)
    return jnp.where(lane < HEAD_DIM, xv, zero), jnp.where(lane < HEAD_DIM, zero, xv)


def _attn_specs(s, tq):
    q_spec = pl.BlockSpec((None, tq, LANES), lambda hp, qi: (0, qi, hp))
    k_spec = pl.BlockSpec((None, s, LANES), lambda hp, qi: (1, 0, hp))
    v_spec = pl.BlockSpec((None, s, LANES), lambda hp, qi: (2, 0, hp))
    return q_spec, k_spec, v_spec


def _rel_mask(tq, tk, i, strict):
    row = lax.broadcasted_iota(jnp.int32, (tq, tk), 0)
    col = lax.broadcasted_iota(jnp.int32, (tq, tk), 1) + i * tk
    return col < row if strict else col <= row


def _sb_fwd(qkv, tq, tk, name):
    _, s, d = qkv.shape
    hp_n, nq, nd = d // LANES, s // tq, tq // tk
    scale = HEAD_DIM ** -0.5

    def body(q_ref, k_ref, v_ref, o_ref, lt_ref):
        qi = pl.program_id(1)
        lane = lax.broadcasted_iota(jnp.int32, (tq, LANES), 1)
        onward = _tri(tk, lambda j, sidx: j >= sidx, 2)
        qpair = _pair(q_ref[...] * scale)

        def block(kb, carry, mask):
            acc, runs = carry[0], carry[1:]
            ks = pl.multiple_of(kb * tk, tk)
            kblk = k_ref[pl.ds(ks, tk), :]
            vpair = _pair(v_ref[pl.ds(ks, tk), :])
            probs, new_runs = [], []
            for qh, run in zip(qpair, runs):
                z = lax.dot_general(qh, kblk, NT, preferred_element_type=F32)
                l1 = _log_one_minus_sigmoid(z)
                if mask is not None:
                    l1 = jnp.where(mask, l1, 0.0)
                c = jnp.dot(_split_bf16(l1, 2), onward, preferred_element_type=F32)
                a = jnp.exp(z + c + run)
                if mask is not None:
                    a = jnp.where(mask, a, 0.0)
                probs.append(a.astype(BF16))
                new_runs.append(run + jnp.sum(l1, axis=1, keepdims=True))
            acc = acc + jnp.dot(jnp.concatenate(probs, axis=1), jnp.concatenate(vpair, axis=0),
                                preferred_element_type=F32)
            return (acc, *new_runs)

        zero1 = jnp.zeros((tq, 1), F32)
        carry = (jnp.zeros((tq, LANES), F32), zero1, zero1)
        for i in reversed(range(nd)):
            carry = block(qi * nd + i, carry, _rel_mask(tq, tk, i, True))
        acc, run0, run1 = lax.fori_loop(0, qi * nd, lambda j, cr: block(qi * nd - 1 - j, cr, None), carry)
        o_ref[...] = acc.astype(BF16)
        lt_ref[...] = jnp.where(lane < HEAD_DIM, run0, run1)

    q_spec, k_spec, v_spec = _attn_specs(s, tq)
    return pl.pallas_call(
        body, name=name, grid=(hp_n, nq), in_specs=[q_spec, k_spec, v_spec],
        out_specs=[pl.BlockSpec((tq, LANES), lambda hp, qi: (qi, hp)),
                   pl.BlockSpec((None, tq, LANES), lambda hp, qi: (hp, qi, 0))],
        out_shape=[jax.ShapeDtypeStruct((s, d), BF16), jax.ShapeDtypeStruct((hp_n, s, LANES), F32)],
        compiler_params=_params("parallel", "parallel"))(qkv, qkv, qkv)


def _sb_bwd(qkv, ltot, do, tq, tk, name):
    _, s, d = qkv.shape
    hp_n, nq, nd = d // LANES, s // tq, tq // tk
    scale = HEAD_DIM ** -0.5

    def body(q_ref, k_ref, v_ref, lt_ref, do_ref, dqkv_ref, dk_acc, dv_acc):
        qi = pl.program_id(1)

        @pl.when(qi == 0)
        def _():
            dk_acc[...] = jnp.zeros_like(dk_acc)
            dv_acc[...] = jnp.zeros_like(dv_acc)

        onward = _tri(tk, lambda j, sidx: j >= sidx, 2)
        before = _tri(tk, lambda j, sidx: j < sidx, 2)
        qpair = _pair(q_ref[...] * scale)
        dopair = _pair(do_ref[...])
        ltv = lt_ref[...]
        lts = (ltv[:, 0:1], ltv[:, HEAD_DIM:HEAD_DIM + 1])

        def block(kb, carry, mask):
            dq, lpres, gpres = carry[0], carry[1:3], carry[3:5]
            ks = pl.multiple_of(kb * tk, tk)
            kblk, vblk = k_ref[pl.ds(ks, tk), :], v_ref[pl.ds(ks, tk), :]
            kpair = _pair(kblk * scale)
            dzs, probs, new_l, new_g = [], [], [], []
            for qh, doh, lt, lpre, gpre in zip(qpair, dopair, lts, lpres, gpres):
                z = lax.dot_general(qh, kblk, NT, preferred_element_type=F32)
                l1 = _log_one_minus_sigmoid(z)
                if mask is not None:
                    l1 = jnp.where(mask, l1, 0.0)
                lpre = lpre + jnp.sum(l1, axis=1, keepdims=True)
                c = jnp.dot(_split_bf16(l1, 2), onward, preferred_element_type=F32)
                a = jnp.exp(z + c + (lt - lpre))
                if mask is not None:
                    a = jnp.where(mask, a, 0.0)
                da = lax.dot_general(doh, vblk, NT, preferred_element_type=F32)
                g = da * a
                p = jnp.dot(_split_bf16(g, 2), before, preferred_element_type=F32) + gpre
                dz = g - jnp.exp(z + l1) * (g + p)
                if mask is not None:
                    dz = jnp.where(mask, dz, 0.0)
                dzs.append(dz.astype(BF16))
                probs.append(a.astype(BF16))
                new_l.append(lpre)
                new_g.append(gpre + jnp.sum(g, axis=1, keepdims=True))
            dq = dq + jnp.dot(jnp.concatenate(dzs, axis=1), jnp.concatenate(kpair, axis=0),
                              preferred_element_type=F32)
            dk_acc[pl.ds(ks, tk), :] += lax.dot_general(jnp.concatenate(dzs, axis=0), jnp.concatenate(qpair, axis=0),
                                                        TN, preferred_element_type=F32)
            dv_acc[pl.ds(ks, tk), :] += lax.dot_general(jnp.concatenate(probs, axis=0), jnp.concatenate(dopair, axis=0),
                                                        TN, preferred_element_type=F32)
            return (dq, *new_l, *new_g)

        zero1 = jnp.zeros((tq, 1), F32)
        init = (jnp.zeros((tq, LANES), F32), zero1, zero1, zero1, zero1)
        carry = lax.fori_loop(0, qi * nd, lambda j, cr: block(j, cr, None), init)
        for i in range(nd):
            carry = block(qi * nd + i, carry, _rel_mask(tq, tk, i, True))
        dqkv_ref[0, pl.ds(pl.multiple_of(qi * tq, tq), tq), :] = carry[0].astype(BF16)

        @pl.when(qi == nq - 1)
        def _():
            dqkv_ref[1] = dk_acc[...].astype(BF16)
            dqkv_ref[2] = dv_acc[...].astype(BF16)

    q_spec, k_spec, v_spec = _attn_specs(s, tq)
    return pl.pallas_call(
        body, name=name, grid=(hp_n, nq),
        in_specs=[q_spec, k_spec, v_spec, pl.BlockSpec((None, tq, LANES), lambda hp, qi: (hp, qi, 0)),
                  pl.BlockSpec((tq, LANES), lambda hp, qi: (qi, hp))],
        out_specs=pl.BlockSpec((3, s, LANES), lambda hp, qi: (0, 0, hp)),
        out_shape=jax.ShapeDtypeStruct((3, s, d), BF16),
        scratch_shapes=[pltpu.VMEM((s, LANES), F32), pltpu.VMEM((s, LANES), F32)],
        compiler_params=_params("parallel", "arbitrary"))(qkv, qkv, qkv, ltot, do)


def _fox_prep(fl, bias, name):
    s, w = fl.shape
    tb = _pick(s, 512, SUBLANES)

    def body(fl_ref, b_ref, cum_ref, carry_ref):
        i = pl.program_id(0)

        @pl.when(i == 0)
        def _():
            carry_ref[...] = jnp.zeros_like(carry_ref)

        logf = _log_sigmoid(fl_ref[...] + b_ref[...])
        row = lax.broadcasted_iota(jnp.int32, (tb, tb), 0)
        col = lax.broadcasted_iota(jnp.int32, (tb, tb), 1)
        incl = (col <= row).astype(BF16)
        tot = carry_ref[...]
        rem = logf
        for _ in range(3):
            part = rem.astype(BF16)
            tot = tot + jnp.dot(incl, part, preferred_element_type=F32)
            rem = rem - part.astype(F32)
        cum_ref[...] = tot
        carry_ref[...] = tot[tb - 1:tb, :]

    blk = pl.BlockSpec((tb, w), lambda i: (i, 0))
    return pl.pallas_call(body, name=name, grid=(s // tb,), in_specs=[blk, pl.BlockSpec((1, w), lambda i: (0, 0))],
                          out_specs=blk, out_shape=jax.ShapeDtypeStruct((s, w), F32),
                          scratch_shapes=[pltpu.VMEM((1, w), F32)], compiler_params=_params("arbitrary"))(fl, bias)


def _fox_gate_bwd(dcum, fl, bias, n_heads, name):
    s, w = fl.shape
    tb = _pick(s, 512, SUBLANES)
    nb = s // tb

    def body(dc_ref, fl_ref, b_ref, dfl_ref, db_ref, carry_ref):
        i = pl.program_id(0)

        @pl.when(i == 0)
        def _():
            carry_ref[...] = jnp.zeros_like(carry_ref)
            db_ref[...] = jnp.zeros_like(db_ref)

        row = lax.broadcasted_iota(jnp.int32, (tb, tb), 0)
        col = lax.broadcasted_iota(jnp.int32, (tb, tb), 1)
        incl = (col >= row).astype(BF16)
        tot = jnp.broadcast_to(carry_ref[...], (tb, w))
        rem = dc_ref[...]
        for _ in range(3):
            part = rem.astype(BF16)
            tot = tot + jnp.dot(incl, part, preferred_element_type=F32)
            rem = rem - part.astype(F32)
        carry_ref[...] = tot[0:1, :]
        xg = fl_ref[...] + b_ref[...]
        e = jnp.exp(-jnp.abs(xg))
        sig_neg = jnp.where(xg >= 0.0, e, 1.0) / (1.0 + e)
        lane = lax.broadcasted_iota(jnp.int32, (tb, w), 1)
        dfl = jnp.where(lane < n_heads, tot * sig_neg, 0.0)
        dfl_ref[...] = dfl.astype(BF16)
        db_ref[...] += jnp.sum(dfl, axis=0, keepdims=True)

    blk = pl.BlockSpec((tb, w), lambda i: (nb - 1 - i, 0))
    vec = pl.BlockSpec((1, w), lambda i: (0, 0))
    return pl.pallas_call(body, name=name, grid=(nb,), in_specs=[blk, blk, vec], out_specs=[blk, vec],
                          out_shape=[jax.ShapeDtypeStruct((s, w), BF16), jax.ShapeDtypeStruct((1, w), F32)],
                          scratch_shapes=[pltpu.VMEM((1, w), F32)], compiler_params=_params("arbitrary"))(dcum, fl, bias)


def _fox_fwd(qkv, cq, ck, tq, tk, name):
    _, s, d = qkv.shape
    hp_n, nq, nd = d // LANES, s // tq, tq // tk
    scale = HEAD_DIM ** -0.5

    def body(q_ref, k_ref, v_ref, cq_ref, ck_ref, o_ref, lse_ref):
        qi = pl.program_id(1)
        low = lax.broadcasted_iota(jnp.int32, (tq, LANES), 1) < HEAD_DIM
        qpair = _pair(q_ref[...] * scale)
        cqv = cq_ref[...]
        cqs = (cqv[:, 0:1], cqv[:, HEAD_DIM:HEAD_DIM + 1])

        def block(kb, carry, mask):
            acc, stats = carry[0], carry[1:]
            ks = pl.multiple_of(kb * tk, tk)
            kblk = k_ref[pl.ds(ks, tk), :]
            vpair = _pair(v_ref[pl.ds(ks, tk), :])
            probs, alphas, new = [], [], []
            for h, (qh, cqh) in enumerate(zip(qpair, cqs)):
                m, lsum = stats[2 * h], stats[2 * h + 1]
                z = lax.dot_general(qh, kblk, NT, preferred_element_type=F32)
                sc = z + (cqh - ck_ref[h:h + 1, pl.ds(ks, tk)])
                if mask is not None:
                    sc = jnp.where(mask, sc, -jnp.inf)
                m_new = jnp.maximum(m, jnp.max(sc, axis=1, keepdims=True))
                alpha = jnp.exp(m - m_new)
                p = jnp.exp(sc - m_new)
                new += [m_new, alpha * lsum + jnp.sum(p, axis=1, keepdims=True)]
                probs.append(p.astype(BF16))
                alphas.append(alpha)
            acc = jnp.where(low, alphas[0], alphas[1]) * acc + jnp.dot(
                jnp.concatenate(probs, axis=1), jnp.concatenate(vpair, axis=0), preferred_element_type=F32)
            return (acc, *new)

        neg = jnp.full((tq, 1), -jnp.inf, F32)
        zero1 = jnp.zeros((tq, 1), F32)
        carry = (jnp.zeros((tq, LANES), F32), neg, zero1, neg, zero1)
        for i in range(nd):
            carry = block(qi * nd + i, carry, _rel_mask(tq, tk, i, False))
        acc, m0, l0, m1, l1 = lax.fori_loop(0, qi * nd, lambda j, cr: block(qi * nd - 1 - j, cr, None), carry)
        o_ref[...] = (acc / jnp.where(low, l0, l1)).astype(BF16)
        lse_ref[...] = jnp.where(low, m0 + jnp.log(l0), m1 + jnp.log(l1))

    q_spec, k_spec, v_spec = _attn_specs(s, tq)
    pair_rows = pl.BlockSpec((None, tq, LANES), lambda hp, qi: (hp, qi, 0))
    return pl.pallas_call(
        body, name=name, grid=(hp_n, nq),
        in_specs=[q_spec, k_spec, v_spec, pair_rows, pl.BlockSpec((None, 2, s), lambda hp, qi: (hp, 0, 0))],
        out_specs=[pl.BlockSpec((tq, LANES), lambda hp, qi: (qi, hp)), pair_rows],
        out_shape=[jax.ShapeDtypeStruct((s, d), BF16), jax.ShapeDtypeStruct((hp_n, s, LANES), F32)],
        compiler_params=_params("parallel", "parallel"))(qkv, qkv, qkv, cq, ck)


def _fox_bwd(qkv, o, do, lse, cq, ck, tq, tk, name):
    _, s, d = qkv.shape
    hp_n, nq, nd = d // LANES, s // tq, tq // tk
    scale = HEAD_DIM ** -0.5

    def body(q_ref, k_ref, v_ref, o_ref, do_ref, lse_ref, cq_ref, ck_ref, dqkv_ref, dcq_ref, dck_ref, dk_acc, dv_acc):
        qi = pl.program_id(1)

        @pl.when(qi == 0)
        def _():
            dk_acc[...] = jnp.zeros_like(dk_acc)
            dv_acc[...] = jnp.zeros_like(dv_acc)
            dck_ref[...] = jnp.zeros_like(dck_ref)

        low = lax.broadcasted_iota(jnp.int32, (tq, LANES), 1) < HEAD_DIM
        dov, cqv, lsev = do_ref[...], cq_ref[...], lse_ref[...]
        qpair = _pair(q_ref[...] * scale)
        dopair = _pair(dov)
        prod = dov.astype(F32) * o_ref[...].astype(F32)
        deltas = (jnp.sum(jnp.where(low, prod, 0.0), axis=1, keepdims=True),
                  jnp.sum(jnp.where(low, 0.0, prod), axis=1, keepdims=True))
        cqs = (cqv[:, 0:1], cqv[:, HEAD_DIM:HEAD_DIM + 1])
        lses = (lsev[:, 0:1], lsev[:, HEAD_DIM:HEAD_DIM + 1])

        def block(kb, carry, mask):
            dq, rowsums = carry[0], carry[1:]
            ks = pl.multiple_of(kb * tk, tk)
            kblk, vblk = k_ref[pl.ds(ks, tk), :], v_ref[pl.ds(ks, tk), :]
            kpair = _pair(kblk * scale)
            dss, probs, new_rows = [], [], []
            for h, (qh, doh) in enumerate(zip(qpair, dopair)):
                z = lax.dot_general(qh, kblk, NT, preferred_element_type=F32)
                sc = z + (cqs[h] - ck_ref[h:h + 1, pl.ds(ks, tk)])
                p = jnp.exp(sc - lses[h])
                if mask is not None:
                    p = jnp.where(mask, p, 0.0)
                dp = lax.dot_general(doh, vblk, NT, preferred_element_type=F32)
                ds = p * (dp - deltas[h])
                dck_ref[h:h + 1, pl.ds(ks, tk)] -= jnp.sum(ds, axis=0, keepdims=True)
                new_rows.append(rowsums[h] + jnp.sum(ds, axis=1, keepdims=True))
                dss.append(ds.astype(BF16))
                probs.append(p.astype(BF16))
            dq = dq + jnp.dot(jnp.concatenate(dss, axis=1), jnp.concatenate(kpair, axis=0),
                              preferred_element_type=F32)
            dk_acc[pl.ds(ks, tk), :] += lax.dot_general(jnp.concatenate(dss, axis=0), jnp.concatenate(qpair, axis=0),
                                                       TN, preferred_element_type=F32)
            dv_acc[pl.ds(ks, tk), :] += lax.dot_general(jnp.concatenate(probs, axis=0), jnp.concatenate(dopair, axis=0),
                                                       TN, preferred_element_type=F32)
            return (dq, *new_rows)

        zero1 = jnp.zeros((tq, 1), F32)
        carry = lax.fori_loop(0, qi * nd, lambda j, cr: block(j, cr, None), (jnp.zeros((tq, LANES), F32), zero1, zero1))
        for i in range(nd):
            carry = block(qi * nd + i, carry, _rel_mask(tq, tk, i, False))
        dq, rs0, rs1 = carry
        dqkv_ref[0, pl.ds(pl.multiple_of(qi * tq, tq), tq), :] = dq.astype(BF16)
        dcq_ref[...] = jnp.where(low, rs0, rs1)

        @pl.when(qi == nq - 1)
        def _():
            dqkv_ref[1] = dk_acc[...].astype(BF16)
            dqkv_ref[2] = dv_acc[...].astype(BF16)

    q_spec, k_spec, v_spec = _attn_specs(s, tq)
    pair_rows = pl.BlockSpec((None, tq, LANES), lambda hp, qi: (hp, qi, 0))
    tile = pl.BlockSpec((tq, LANES), lambda hp, qi: (qi, hp))
    keys = pl.BlockSpec((None, 2, s), lambda hp, qi: (hp, 0, 0))
    return pl.pallas_call(
        body, name=name, grid=(hp_n, nq),
        in_specs=[q_spec, k_spec, v_spec, tile, tile, pair_rows, pair_rows, keys],
        out_specs=[pl.BlockSpec((3, s, LANES), lambda hp, qi: (0, 0, hp)), pair_rows, keys],
        out_shape=[jax.ShapeDtypeStruct((3, s, d), BF16), jax.ShapeDtypeStruct((hp_n, s, LANES), F32),
                   jax.ShapeDtypeStruct((hp_n, 2, s), F32)],
        scratch_shapes=[pltpu.VMEM((s, LANES), F32), pltpu.VMEM((s, LANES), F32)],
        compiler_params=_params("parallel", "arbitrary"))(qkv, qkv, qkv, o, do, lse, cq, ck)


def _loss_head(y, target, name):
    s, d = y.shape
    tm = _pick(s, 512, SUBLANES)

    def body(y_ref, t_ref, dy_ref, sq_ref):
        i = pl.program_id(0)
        diff = y_ref[...] - t_ref[...]
        dy_ref[...] = diff / d

        @pl.when(i == 0)
        def _():
            sq_ref[...] = jnp.zeros_like(sq_ref)

        sq_ref[...] += jnp.sum(diff * diff, axis=0, keepdims=True)

    row = pl.BlockSpec((tm, d), lambda i: (i, 0))
    vec = pl.BlockSpec((1, d), lambda i: (0, 0))
    return pl.pallas_call(body, name=name, grid=(s // tm,), in_specs=[row, row], out_specs=[row, vec],
                          out_shape=[jax.ShapeDtypeStruct((s, d), F32), jax.ShapeDtypeStruct((1, d), F32)],
                          compiler_params=_params("arbitrary"))(y, target)


def _mod_fwd(c_all, w_mod, b_mod_cols, name):
    nl, d, cols = w_mod.shape
    nb = c_all.shape[0]

    def body(c_ref, w_ref, b_ref, o_ref):
        cv = c_ref[...]
        act = (cv * jax.nn.sigmoid(cv)).astype(BF16)
        o_ref[...] = jnp.dot(act, w_ref[...].astype(BF16), preferred_element_type=F32) + b_ref[...]

    return pl.pallas_call(
        body, name=name, grid=(nl,),
        in_specs=[pl.BlockSpec((nb, d), lambda l: (0, 0)), pl.BlockSpec((None, d, cols), lambda l: (l, 0, 0)),
                  pl.BlockSpec((None, 1, cols), lambda l: (l, 0, 0))],
        out_specs=pl.BlockSpec((None, nb, cols), lambda l: (l, 0, 0)),
        out_shape=jax.ShapeDtypeStruct((nl, nb, cols), F32), compiler_params=_params("parallel"))(c_all, w_mod, b_mod_cols)


def _wmod_grad(c_all_t, dmod, name):
    d, nb = c_all_t.shape
    _, nl, cols = dmod.shape

    def body(c_ref, dm_ref, o_ref):
        cv = c_ref[...]
        act = cv * jax.nn.sigmoid(cv)
        tot = act[:, 0:1] * dm_ref[0]
        for b in range(1, nb):
            tot = tot + act[:, b:b + 1] * dm_ref[b]
        o_ref[...] = tot

    return pl.pallas_call(
        body, name=name, grid=(nl,),
        in_specs=[pl.BlockSpec((d, nb), lambda l: (0, 0)), pl.BlockSpec((nb, None, 1, cols), lambda l: (0, l, 0, 0))],
        out_specs=pl.BlockSpec((None, d, cols), lambda l: (l, 0, 0)),
        out_shape=jax.ShapeDtypeStruct((nl, d, cols), F32), compiler_params=_params("parallel"))(
            c_all_t, dmod.reshape(nb, nl, 1, cols))


def _adamw(recv, w, m, v, name, *, tr=256, tc=512):
    nq, nl, rp, cp = recv.shape
    _, r, c = w.shape
    br = _pick(r, tr, SUBLANES) if rp == r else r
    bc = _pick(c, tc, LANES) if cp == c else c
    rbr = br if rp == r else rp
    rbc = bc if cp == c else cp

    def body(rv_ref, w_ref, m_ref, v_ref, g_ref, d_ref, nm_ref, nv_ref):
        g = rv_ref[0, :br, :bc].astype(F32)
        for qd in range(1, nq):
            g = g + rv_ref[qd, :br, :bc].astype(F32)
        m_new = ADAM_B1 * m_ref[...] + (1.0 - ADAM_B1) * g
        v_new = ADAM_B2 * v_ref[...] + (1.0 - ADAM_B2) * jnp.square(g)
        m_hat = m_new / (1.0 - ADAM_B1 ** ADAM_STEP)
        v_hat = v_new / (1.0 - ADAM_B2 ** ADAM_STEP)
        g_ref[...] = g
        d_ref[...] = -ADAM_LR * (m_hat / (jnp.sqrt(v_hat) + ADAM_EPS) + ADAM_WD * w_ref[...])
        nm_ref[...] = m_new
        nv_ref[...] = v_new

    blk = pl.BlockSpec((None, br, bc), lambda l, i, j: (l, i, j))
    rblk = pl.BlockSpec((nq, None, rbr, rbc), lambda l, i, j: (0, l, i, j))
    shape = jax.ShapeDtypeStruct(w.shape, F32)
    return pl.pallas_call(body, name=name, grid=(nl, r // br, c // bc), in_specs=[rblk, blk, blk, blk],
                          out_specs=[blk] * 4, out_shape=[shape] * 4,
                          compiler_params=_params("parallel", "parallel", "parallel"))(recv, w, m, v)


def _mesh_pos():
    return lax.axis_index("x"), lax.axis_index("y"), lax.axis_index("c")


def _allgather_small(block, name):
    m_per, n = block.shape

    def body(x_ref, out_ref, send_sems, recv_sems, local_sem):
        x, y, c = _mesh_pos()
        me, sibling = (x, y, c), (x, y, 1 - c)
        chips = [(1 - x, y), (x, 1 - y), (1 - x, 1 - y)]

        def rows(px, py, pc):
            return out_ref.at[pl.ds((4 * px + 2 * py + pc) * m_per, m_per), :]

        def copy(k, blk, to, src=None):
            return pltpu.make_async_remote_copy(
                src_ref=rows(*blk) if src is None else src, dst_ref=rows(*blk),
                send_sem=send_sems.at[k], recv_sem=recv_sems.at[k], device_id=to, device_id_type=MESH)

        mine = pltpu.make_async_copy(x_ref, rows(*me), local_sem)
        mine.start()
        first = [copy(0, me, sibling, src=x_ref)]
        first += [copy(1 + j, me, (*chip, c), src=x_ref) for j, chip in enumerate(chips)]
        for cp in first:
            cp.start()
        passed = [copy(4 + j, (*chip, c), sibling) for j, chip in enumerate(chips)]
        for j, chip in enumerate(chips):
            copy(1 + j, (*chip, c), me).wait_recv()
            passed[j].start()
        copy(0, sibling, me).wait_recv()
        for j, chip in enumerate(chips):
            copy(4 + j, (*chip, 1 - c), me).wait_recv()
        for cp in first + passed:
            cp.wait_send()
        mine.wait()

    return pl.pallas_call(
        body, name=name, out_shape=jax.ShapeDtypeStruct((N_DEV * m_per, n), block.dtype),
        in_specs=[pl.BlockSpec(memory_space=pltpu.VMEM)], out_specs=pl.BlockSpec(memory_space=pltpu.VMEM),
        scratch_shapes=[pltpu.SemaphoreType.DMA((7,)), pltpu.SemaphoreType.DMA((7,)), pltpu.SemaphoreType.DMA],
        compiler_params=pltpu.CompilerParams(vmem_limit_bytes=VMEM_LIMIT_BYTES))(block)


def _exchange(srcs, dst_shapes, jobs, name):
    n_src, n_job = len(srcs), len(jobs)

    def body(*refs):
        src_refs, dst_refs = refs[:n_src], refs[n_src:n_src + len(dst_shapes)]
        send_sems, recv_sems, local_sems = refs[n_src + len(dst_shapes):]
        x, y, c = _mesh_pos()
        me = 4 * x + 2 * y + c
        pending = []
        for t, (si, di, src_slice, dst_slice) in enumerate(jobs):
            src, dst = src_refs[si], dst_refs[di]
            lc = pltpu.make_async_copy(src_slice(src, me), dst_slice(dst, me), local_sems.at[t])
            lc.start()
            pending.append(lc)
            for dd in range(1, N_DEV):
                px = 1 - x if dd & 4 else x
                py = 1 - y if dd & 2 else y
                pc = 1 - c if dd & 1 else c
                cp = pltpu.make_async_remote_copy(
                    src_ref=src_slice(src, 4 * px + 2 * py + pc), dst_ref=dst_slice(dst, me),
                    send_sem=send_sems.at[t, dd - 1], recv_sem=recv_sems.at[t, dd - 1],
                    device_id=(px, py, pc), device_id_type=MESH)
                cp.start()
                pending.append(cp)
        for cp in pending:
            cp.wait()

    hbm = pl.BlockSpec(memory_space=pl.ANY)
    return pl.pallas_call(
        body, name=name, out_shape=list(dst_shapes), in_specs=[hbm] * n_src, out_specs=[hbm] * len(dst_shapes),
        scratch_shapes=[pltpu.SemaphoreType.DMA((n_job, N_DEV - 1)), pltpu.SemaphoreType.DMA((n_job, N_DEV - 1)),
                        pltpu.SemaphoreType.DMA((n_job,))])(*srcs)


def _peer(x, y, c, dd):
    return (1 - x if dd & 4 else x, 1 - y if dd & 2 else y, 1 - c if dd & 1 else c)


def _place_own(srcs, dsts, jobs, name):
    n_src, n_dst = len(srcs), len(dsts)

    def body(*refs):
        src_refs, dst_refs, sems = refs[:n_src], refs[n_src:n_src + n_dst], refs[-1]
        x, y, c = _mesh_pos()
        me = 4 * x + 2 * y + c
        copies = [pltpu.make_async_copy(src_slice(src_refs[si], me), dst_slice(dst_refs[di], me), sems.at[t])
                  for t, (si, di, src_slice, dst_slice) in enumerate(jobs)]
        for cp in copies:
            cp.start()
        for cp in copies:
            cp.wait()

    hbm = pl.BlockSpec(memory_space=pl.ANY)
    return pl.pallas_call(
        body, name=name, in_specs=[hbm] * (n_src + n_dst), out_specs=[hbm] * n_dst,
        out_shape=[jax.ShapeDtypeStruct(a.shape, a.dtype) for a in dsts],
        input_output_aliases={n_src + i: i for i in range(n_dst)},
        scratch_shapes=[pltpu.SemaphoreType.DMA((len(jobs),))])(*srcs, *dsts)


def _remote_copies(jobs, src_refs, dst_refs, send_sems, recv_sems, sending):
    x, y, c = _mesh_pos()
    me = 4 * x + 2 * y + c
    out = []
    for t, (si, di, src_slice, dst_slice) in enumerate(jobs):
        for dd in range(1, N_DEV):
            px, py, pc = _peer(x, y, c, dd)
            p = 4 * px + 2 * py + pc
            out.append(pltpu.make_async_remote_copy(
                src_ref=src_slice(src_refs[si], p), dst_ref=dst_slice(dst_refs[di], me if sending else p),
                send_sem=send_sems.at[t * (N_DEV - 1) + dd - 1], recv_sem=recv_sems.at[t * (N_DEV - 1) + dd - 1],
                device_id=(px, py, pc), device_id_type=MESH))
    return out


def _exchange_start(srcs, dsts, jobs, name):
    n_src, n_dst, n_job = len(srcs), len(dsts), len(jobs)

    def body(*refs):
        src_refs, dst_refs = refs[:n_src], refs[n_src:n_src + n_dst]
        send_sems, recv_sems = refs[n_src + n_dst], refs[n_src + n_dst + 1]
        for cp in _remote_copies(jobs, src_refs, dst_refs, send_sems, recv_sems, True):
            cp.start()
        refs[-1][...] = jnp.zeros((SUBLANES, LANES), F32)

    hbm = pl.BlockSpec(memory_space=pltpu.HBM)
    sem = pl.BlockSpec(memory_space=pltpu.SEMAPHORE)
    operands = [pltpu.with_memory_space_constraint(a, pltpu.HBM) for a in (*srcs, *dsts)]
    res = pl.pallas_call(
        body, name=name, in_specs=[hbm] * (n_src + n_dst),
        out_specs=(sem, sem, *[hbm] * (n_src + n_dst), pl.BlockSpec(memory_space=pltpu.VMEM)),
        out_shape=(pltpu.SemaphoreType.DMA((n_job * (N_DEV - 1),)), pltpu.SemaphoreType.DMA((n_job * (N_DEV - 1),)),
                   *[pltpu.HBM(a.shape, a.dtype) for a in (*srcs, *dsts)],
                   jax.ShapeDtypeStruct((SUBLANES, LANES), F32)),
        input_output_aliases={i: 2 + i for i in range(n_src + n_dst)},
        compiler_params=pltpu.CompilerParams(has_side_effects=pltpu.SideEffectType.DATAFLOW_SIDE_EFFECTING))(*operands)
    return res[0], res[1], list(res[2:2 + n_src]), list(res[2 + n_src:2 + n_src + n_dst]), res[-1]


def _exchange_wait(parts, dsts, after, name):
    n_dst = len(dsts)
    counts = [len(p[2]) for p in parts]
    n_src = sum(counts)

    def body(*refs):
        dst_refs = refs[n_src:n_src + n_dst]
        sem_refs = refs[n_src + n_dst:n_src + n_dst + 2 * len(parts)]
        at = 0
        for k, (_, _, part_srcs, jobs) in enumerate(parts):
            src_refs = refs[at:at + len(part_srcs)]
            at += len(part_srcs)
            for cp in _remote_copies(jobs, src_refs, dst_refs, sem_refs[2 * k], sem_refs[2 * k + 1], False):
                cp.wait_send()
                cp.wait_recv()

    hbm = pl.BlockSpec(memory_space=pltpu.HBM)
    sem = pl.BlockSpec(memory_space=pltpu.SEMAPHORE)
    srcs = [a for p in parts for a in p[2]]
    sems = [s for p in parts for s in (p[0], p[1])]
    res = pl.pallas_call(
        body, name=name, in_specs=[hbm] * (n_src + n_dst) + [sem] * len(sems) + [pl.BlockSpec(memory_space=pl.ANY)],
        out_specs=[hbm] * (n_src + n_dst),
        out_shape=[pltpu.HBM(a.shape, a.dtype) for a in (*srcs, *dsts)],
        input_output_aliases={i: i for i in range(n_src + n_dst)},
        compiler_params=pltpu.CompilerParams(has_side_effects=pltpu.SideEffectType.DATAFLOW_SIDE_EFFECTING))(
            *srcs, *dsts, *sems, after)
    return list(res[:n_src]), list(res[n_src:])


def _whole(ref, p):
    return ref


def _layer_of(layer, inner):
    return lambda ref, p: inner(ref.at[layer], p)


def _cols_of(width):
    def take(ref, p):
        lead = (slice(None),) * (len(ref.shape) - 1)
        return ref.at[lead + (pl.ds(pl.multiple_of(p * width, LANES), width),)]
    return take


def _rows_of(height):
    def take(ref, p):
        lead = (slice(None),) * (len(ref.shape) - 2)
        return ref.at[lead + (pl.ds(pl.multiple_of(p * height, SUBLANES), height), slice(None))]
    return take


def _slot(layer=None):
    if layer is None:
        return lambda ref, p: ref.at[p]
    return lambda ref, p: ref.at[p, layer]


def _local_step(x0, target, mod, g_mix_pre, g_mix_post, g_ffn_pre, g_ffn_post, wqkv, wo, wg, wu, wd,
                wfg, bfg, wconv, bconv, tiles, fetch=None, on_grads=None, zero=0.0):
    s, d = x0.shape
    nl = mod.shape[0]
    n_heads = d // HEAD_DIM
    hp_n = d // LANES
    vec = lambda a: a.reshape(1, -1)
    saved = []
    xcur = x0
    for l in range(nl):
        if fetch is not None:
            wqkv, wo, wg, wu, wd = fetch(l, xcur)
        sh_a, sc_a, gt_a, sh_f, sc_f, gt_f = (vec(mod[l, i * d:(i + 1) * d]) for i in range(6))
        fox = l % 2 == 1
        g1 = vec(g_mix_pre[l])
        if l == 0 and fetch is not None:
            g1 = g1 + zero
        h1 = _norm_mod_fwd(xcur, g1, sh_a, sc_a, f"norm_mix_fwd_{l}")
        qkv = _mm_nn(h1, wqkv, l, BF16, f"qkv_fwd_{l}", split_out=d)
        if fox:
            j = l // 2
            fl = _mm_nn(h1, wfg, j, F32, f"fgate_fwd_{l}")
            cum = _fox_prep(fl, bfg[j], f"fox_prep_{l}")
            cum_h = cum[:, :n_heads]
            cq = jnp.repeat(cum_h.reshape(s, hp_n, 2).transpose(1, 0, 2), HEAD_DIM, axis=2)
            ck = cum_h.T.reshape(hp_n, 2, s)
            o, stat = _fox_fwd(qkv, cq, ck, *tiles["fox_fwd"], f"fox_fwd_{l}")
            extra = (fl, cq, ck)
        else:
            o, stat = _sb_fwd(qkv, *tiles["sb_fwd"], f"sb_fwd_{l}")
            extra = None
        u = _mm_nn(o, wo, l, F32, f"attn_out_fwd_{l}")
        x2 = _post_fwd(xcur, u, vec(g_mix_post[l]), gt_a, f"post_mix_fwd_{l}")
        h2 = _norm_mod_fwd(x2, vec(g_ffn_pre[l]), sh_f, sc_f, f"norm_ffn_fwd_{l}")
        gp = _mm_nn(h2, wg, l, F32, f"ffn_gate_fwd_{l}")
        up = _mm_nn(h2, wu, l, F32, f"ffn_up_fwd_{l}")
        act = _conv_act_fwd(gp, up, wconv[l], bconv[l], f"conv_act_fwd_{l}")
        yv = _mm_nn(act, wd, l, F32, f"ffn_down_fwd_{l}")
        x3 = _post_fwd(x2, yv, vec(g_ffn_post[l]), gt_f, f"post_ffn_fwd_{l}")
        saved.append((xcur, h1, qkv, o, stat, extra, u, x2, h2, gp, up, act, yv))
        xcur = x3

    dx, sq = _loss_head(xcur, target, "loss_head")
    small, big = [None] * nl, [None] * nl
    nudge = None
    for l in reversed(range(nl)):
        xin, h1, qkv, o, stat, extra, u, x2, h2, gp, up, act, yv = saved[l]
        sc_a, gt_a, sc_f, gt_f = (vec(mod[l, i * d:(i + 1) * d]) for i in (1, 2, 4, 5))
        fox = l % 2 == 1
        g4 = vec(g_ffn_post[l])
        if nudge is not None:
            g4 = g4 + nudge
        dy, dgt_f, dg4 = _post_bwd(dx, yv, g4, gt_f, f"post_ffn_bwd_{l}")
        dact = _mm_nt(dy, wd, l, F32, f"ffn_down_dx_{l}")
        dwd = _mm_tn(act, dy, BF16, f"ffn_down_dw_{l}")
        dgp, dup, dwc, dbc = _conv_act_bwd(dact, gp, up, wconv[l], bconv[l], f"conv_act_bwd_{l}")
        dh2 = _mm_nt(dgp, wg, l, F32, f"ffn_gate_dx_{l}")
        dh2 = _mm_nt(dup, wu, l, F32, f"ffn_up_dx_{l}", add=dh2)
        dwg = _mm_tn(h2, dgp, BF16, f"ffn_gate_dw_{l}")
        dwu = _mm_tn(h2, dup, BF16, f"ffn_up_dw_{l}")
        dx2, dsh_f, dsc_f, dg3 = _norm_mod_bwd(dh2, x2, vec(g_ffn_pre[l]), sc_f, dx, f"norm_ffn_bwd_{l}")
        du, dgt_a, dg2 = _post_bwd(dx2, u, vec(g_mix_post[l]), gt_a, f"post_mix_bwd_{l}")
        do = _mm_nt(du, wo, l, BF16, f"attn_out_dx_{l}")
        dwo = _mm_tn(o, du, BF16, f"attn_out_dw_{l}")
        if fox:
            j = l // 2
            fl, cq, ck = extra
            dqkv, dcq, dck = _fox_bwd(qkv, o, do, stat, cq, ck, *tiles["fox_bwd"], f"fox_bwd_{l}")
            dcq = jnp.max(dcq.reshape(hp_n, s, 2, HEAD_DIM), axis=3)
            dcum = dcq.transpose(1, 0, 2).reshape(s, n_heads) + dck.reshape(n_heads, s).T
            dcum = jnp.pad(dcum, ((0, 0), (0, LANES - n_heads)))
            dfl, dbfg = _fox_gate_bwd(dcum, fl, bfg[j], n_heads, f"fox_gate_bwd_{l}")
            dh1 = _mm_nt(dfl, wfg, j, F32, f"fgate_dx_{l}")
            dh1 = _mm_nt(dqkv, wqkv, l, F32, f"qkv_dx_{l}", add=dh1)
            dwfg = _mm_tn(h1, dfl, F32, f"fgate_dw_{l}")[:, :n_heads]
            dbfg = dbfg[0, :n_heads]
        else:
            dqkv = _sb_bwd(qkv, stat, do, *tiles["sb_bwd"], f"sb_bwd_{l}")
            dh1 = _mm_nt(dqkv, wqkv, l, F32, f"qkv_dx_{l}")
            dwfg = dbfg = None
        dwqkv = _mm_tn(h1, dqkv, BF16, f"qkv_dw_{l}")
        dx, dsh_a, dsc_a, dg1 = _norm_mod_bwd(dh1, xin, vec(g_mix_pre[l]), sc_a, dx2, f"norm_mix_bwd_{l}")
        dmod = jnp.concatenate([dsh_a, dsc_a, dgt_a, dsh_f, dsc_f, dgt_f], axis=1)[0]
        small[l] = dict(dmod=dmod, dg1=dg1[0], dg2=dg2[0], dg3=dg3[0], dg4=dg4[0], dbc=dbc[0], dwc=dwc,
                        dbfg=dbfg, dwfg=dwfg)
        big[l] = dict(qkv=dwqkv, o=dwo, gate=dwg, up=dwu, down=dwd)
        if on_grads is not None:
            nudge = on_grads(l, big[l])
    return sq, dx, small, big


def _rows128(a, rows):
    flat = a.reshape(-1)
    return jnp.pad(flat, (0, rows * LANES - flat.shape[0])).reshape(rows, LANES)


def _ceil8(n_elems):
    rows = -(-n_elems // LANES)
    return -(-rows // SUBLANES) * SUBLANES


def kernel(x, c, w_mod, b_mod, g_mix_pre, g_mix_post, w_qkv, w_o, w_fg, b_fg, g_ffn_pre, g_ffn_post, w_ffn_gate, w_ffn_up, w_conv, b_conv, w_ffn_down, loss_target, m_w_mod, m_b_mod, m_g_mix_pre, m_g_mix_post, m_w_qkv, m_w_o, m_w_fg, m_b_fg, m_g_ffn_pre, m_g_ffn_post, m_w_ffn_gate, m_w_ffn_up, m_w_conv, m_b_conv, m_w_ffn_down, v_w_mod, v_b_mod, v_g_mix_pre, v_g_mix_post, v_w_qkv, v_w_o, v_w_fg, v_b_fg, v_g_ffn_pre, v_g_ffn_post, v_w_ffn_gate, v_w_ffn_up, v_w_conv, v_b_conv, v_w_ffn_down):
    _, s, d = x.shape
    nl = w_qkv.shape[0]
    nf = w_fg.shape[0]
    n_heads = w_fg.shape[2]
    fs = w_ffn_gate.shape[2]
    fp = -(-fs // LANES) * LANES
    f_full, f_pad = N_DEV * fs, N_DEV * fp
    mod_cols = w_mod.shape[2]
    qs, orows = w_qkv.shape[2], w_o.shape[1]
    xi, yi, ci = _mesh_pos()
    me = 4 * xi + 2 * yi + ci

    c_rows = d // LANES
    c_all = _allgather_small(jnp.pad(c.reshape(1, d), ((0, SUBLANES - 1), (0, 0))).reshape(SUBLANES * c_rows, LANES),
                             "gather_cond")
    c_all = c_all.reshape(N_DEV, SUBLANES, d)[:, 0, :]
    b_mod_cols = lax.dynamic_slice_in_dim(b_mod, me * mod_cols, mod_cols, axis=1).reshape(nl, 1, mod_cols)
    mod_part = _mod_fwd(c_all, w_mod, b_mod_cols, "mod_fwd")

    conv_pad = jnp.pad(w_conv, ((0, 0), (0, 0), (0, fp - fs)))
    r_mod, r_conv, r_fg = _ceil8(mod_part.size), _ceil8(conv_pad.size), _ceil8(w_fg.size)
    payload = jnp.concatenate([_rows128(mod_part, r_mod), _rows128(conv_pad, r_conv), _rows128(w_fg, r_fg)], axis=0)
    got = _allgather_small(payload, "gather_small_weights").reshape(N_DEV, r_mod + r_conv + r_fg, LANES)
    mod_g = got[:, :r_mod].reshape(N_DEV, -1)[:, :mod_part.size].reshape(N_DEV, nl, N_DEV, mod_cols)
    mod = lax.dynamic_index_in_dim(mod_g, me, axis=2, keepdims=False).transpose(1, 0, 2).reshape(nl, N_DEV * mod_cols)
    conv_g = got[:, r_mod:r_mod + r_conv].reshape(N_DEV, -1)[:, :conv_pad.size].reshape(N_DEV, nl, 3, fp)
    wconv_full = conv_g.transpose(1, 2, 0, 3).reshape(nl, 3, f_pad)
    fg_g = got[:, r_mod + r_conv:].reshape(N_DEV, -1)[:, :w_fg.size].reshape(N_DEV, nf, orows, n_heads)
    wfg_full = fg_g.transpose(1, 0, 2, 3).reshape(nf, d, n_heads)
    wfg_full = jnp.pad(wfg_full, ((0, 0), (0, 0), (0, LANES - n_heads))).astype(BF16)
    bfg_full = jnp.pad(b_fg, ((0, 0), (0, LANES - n_heads))).reshape(nf, 1, LANES)
    bconv_full = jnp.pad(b_conv.reshape(nl, N_DEV, fs), ((0, 0), (0, 0), (0, fp - fs))).reshape(nl, 1, f_pad)

    gate_sh = jnp.pad(w_ffn_gate, ((0, 0), (0, 0), (0, fp - fs))).astype(BF16)
    up_sh = jnp.pad(w_ffn_up, ((0, 0), (0, 0), (0, fp - fs))).astype(BF16)
    down_sh = jnp.pad(w_ffn_down, ((0, 0), (0, fp - fs), (0, 0))).astype(BF16)
    shards = [w_qkv.astype(BF16), w_o.astype(BF16), gate_sh, up_sh, down_sh]
    full_shapes = [jax.ShapeDtypeStruct((nl, d, N_DEV * qs), BF16), jax.ShapeDtypeStruct((nl, d, d), BF16),
                   jax.ShapeDtypeStruct((nl, d, f_pad), BF16), jax.ShapeDtypeStruct((nl, d, f_pad), BF16),
                   jax.ShapeDtypeStruct((nl, f_pad, d), BF16)]
    place = [_cols_of(qs), _rows_of(orows), _cols_of(fp), _cols_of(fp), _rows_of(fp)]
    full = _place_own(shards, [lax.empty(sh.shape, sh.dtype) for sh in full_shapes],
                      [(i, i, _whole, place[i]) for i in range(5)], "place_own_weights")
    gathers, zero = [], 0.0
    for l in range(nl):
        jobs_l = [(i, i, _layer_of(l, _whole), _layer_of(l, place[i])) for i in range(5)]
        send_s, recv_s, shards, full, token = _exchange_start(shards, full, jobs_l, f"gather_start_{l}")
        gathers.append((send_s, recv_s, jobs_l))
        zero = zero + token[0, 0]
    state = {"shards": shards, "full": full}

    def fetch(l, after):
        send_s, recv_s, jobs_l = gathers[l]
        state["shards"], state["full"] = _exchange_wait([(send_s, recv_s, state["shards"], jobs_l)], state["full"],
                                                        after, f"gather_wait_{l}")
        return state["full"]

    order = ["qkv", "o", "gate", "up", "down"]
    send = {"qkv": _cols_of(qs), "o": _rows_of(orows), "gate": _cols_of(fp), "up": _cols_of(fp), "down": _rows_of(fp)}
    recv_shapes = [(N_DEV, nl, d, qs), (N_DEV, nl, orows, d), (N_DEV, nl, d, fp), (N_DEV, nl, d, fp), (N_DEV, nl, fp, d)]
    scatters = []
    landing = {"recv": [lax.empty(sh, BF16) for sh in recv_shapes]}

    def on_grads(l, grads):
        srcs = [grads[nm] for nm in order]
        jobs_l = [(wi, wi, send[nm], _slot(l)) for wi, nm in enumerate(order)]
        own = _place_own(srcs, landing["recv"], jobs_l, f"place_own_grads_{l}")
        send_s, recv_s, srcs, landing["recv"], token = _exchange_start(srcs, own, jobs_l, f"scatter_start_{l}")
        scatters.append((send_s, recv_s, srcs, jobs_l))
        return token[0, 0]

    sq, dx, small, big = _local_step(x[0], loss_target[0], mod, g_mix_pre, g_mix_post, g_ffn_pre, g_ffn_post,
                                     None, None, None, None, None, wfg_full, bfg_full, wconv_full, bconv_full,
                                     ATTN_TILES, fetch=fetch, on_grads=on_grads, zero=zero)
    loss = lax.psum(0.5 * jnp.sum(sq) / d, ("x", "y", "c"))
    _, recv = _exchange_wait(scatters, landing["recv"], dx, "scatter_wait")
    upd = {}
    for nm, rv, wt, mt, vt in zip(order, recv, [w_qkv, w_o, w_ffn_gate, w_ffn_up, w_ffn_down],
                                  [m_w_qkv, m_w_o, m_w_ffn_gate, m_w_ffn_up, m_w_ffn_down],
                                  [v_w_qkv, v_w_o, v_w_ffn_gate, v_w_ffn_up, v_w_ffn_down]):
        upd[nm] = _adamw(rv, wt, mt, vt, f"adamw_{nm}")

    stack = lambda key: jnp.stack([small[l][key] for l in range(nl)])
    dmod = stack("dmod")
    dgs = [stack(k) for k in ("dg1", "dg2", "dg3", "dg4")]
    dbc = stack("dbc").reshape(nl, N_DEV, fp)[:, :, :fs].reshape(nl, f_full)
    dbfg = jnp.stack([small[l]["dbfg"] for l in range(nl) if l % 2 == 1])
    dwc = stack("dwc").reshape(nl, 3, N_DEV, fp)[:, :, :, :fs].reshape(nl, 3, f_full)
    dwfg = jnp.stack([small[l]["dwfg"] for l in range(nl) if l % 2 == 1])
    rep_parts = [dmod] + dgs + [dbc, dbfg]
    rep_rows = [_ceil8(p.size) for p in rep_parts]
    r_rep, r_wc, r_wfg = sum(rep_rows), _ceil8(dwc.size), _ceil8(dwfg.size)
    payload = jnp.concatenate([_rows128(p, r) for p, r in zip(rep_parts, rep_rows)]
                              + [_rows128(dwc, r_wc), _rows128(dwfg, r_wfg)], axis=0)
    gsm = _allgather_small(payload, "gather_small_grads").reshape(N_DEV, 1, r_rep + r_wc + r_wfg, LANES)

    def pack(parts):
        return jnp.concatenate([_rows128(p, r) for p, r in zip(parts, rep_rows)], axis=0).reshape(1, r_rep, LANES)

    rep_w = [b_mod, g_mix_pre, g_mix_post, g_ffn_pre, g_ffn_post, b_conv, b_fg]
    rep_m = [m_b_mod, m_g_mix_pre, m_g_mix_post, m_g_ffn_pre, m_g_ffn_post, m_b_conv, m_b_fg]
    rep_v = [v_b_mod, v_g_mix_pre, v_g_mix_post, v_g_ffn_pre, v_g_ffn_post, v_b_conv, v_b_fg]
    rep_out = _adamw(gsm[:, :, :r_rep], pack(rep_w), pack(rep_m), pack(rep_v), "adamw_replicated", tr=r_rep, tc=LANES)

    def unpack(packed):
        outs, at = [], 0
        for p, r in zip(rep_w, rep_rows):
            outs.append(packed[0, at:at + r].reshape(-1)[:p.size].reshape(p.shape))
            at += r
        return outs

    rep_g, rep_d, rep_nm, rep_nv = (unpack(a) for a in rep_out)

    wc_all = gsm[:, 0, r_rep:r_rep + r_wc].reshape(N_DEV, -1)[:, :dwc.size].reshape(N_DEV, 1, nl * 3, f_full)
    wc_mine = lax.dynamic_slice_in_dim(wc_all, me * fs, fs, axis=3)
    wc_out = _adamw(wc_mine, w_conv.reshape(1, nl * 3, fs), m_w_conv.reshape(1, nl * 3, fs),
                    v_w_conv.reshape(1, nl * 3, fs), "adamw_conv", tr=nl * 3, tc=fs)
    wc_out = [a.reshape(nl, 3, fs) for a in wc_out]
    wfg_all = gsm[:, 0, r_rep + r_wc:].reshape(N_DEV, -1)[:, :dwfg.size].reshape(N_DEV, nf, d, n_heads)
    wfg_mine = lax.dynamic_slice_in_dim(wfg_all, me * orows, orows, axis=2)
    wfg_out = _adamw(wfg_mine, w_fg, m_w_fg, v_w_fg, "adamw_fgate", tr=orows, tc=n_heads)

    dmod_all = gsm[:, 0, :rep_rows[0]].reshape(N_DEV, -1)[:, :dmod.size].reshape(N_DEV, nl, N_DEV * mod_cols)
    dmod_mine = lax.dynamic_slice_in_dim(dmod_all, me * mod_cols, mod_cols, axis=2)
    gwmod = _wmod_grad(c_all.T, dmod_mine, "wmod_grad")
    wmod_out = _adamw(gwmod.reshape(1, nl, d, mod_cols), w_mod, m_w_mod, v_w_mod, "adamw_mod")

    per_weight = [wmod_out, None, None, None, upd["qkv"], upd["o"], wfg_out, None, None, None,
                  upd["gate"], upd["up"], wc_out, None, upd["down"]]
    rep_index = {1: 0, 2: 1, 3: 2, 8: 3, 9: 4, 13: 5, 7: 6}
    outs = [[], [], [], []]
    for pos, res in enumerate(per_weight):
        for kind in range(4):
            if res is None:
                outs[kind].append((rep_g, rep_d, rep_nm, rep_nv)[kind][rep_index[pos]])
            else:
                outs[kind].append(res[kind])
    return (loss, dx.reshape(1, s, d), *outs[0], *outs[1], *outs[2], *outs[3])
```

```python
import functools

import jax
import jax.numpy as jnp
from jax import lax
from jax.experimental import pallas as pl
from jax.experimental.pallas import tpu as pltpu

F32 = jnp.float32
BF16 = jnp.bfloat16

N_DEV = 8
HEAD_DIM = 64
LANES = 128
SUBLANES = 8
RMS_EPS = 1e-6
ADAM_LR = 0.001
ADAM_B1 = 0.9
ADAM_B2 = 0.999
ADAM_EPS = 1e-08
ADAM_WD = 0.01
ADAM_STEP = 10
VMEM_LIMIT_BYTES = 56 * 1024 * 1024
MM_ROWS = 1024
ATTN_TILES = {"sb_fwd": (512, 256), "sb_bwd": (512, 256), "fox_fwd": (512, 256), "fox_bwd": (256, 256)}

NN = (((1,), (0,)), ((), ()))
NT = (((1,), (1,)), ((), ()))
TN = (((0,), (0,)), ((), ()))
MESH = pl.DeviceIdType.MESH


def _params(*sem):
    return pltpu.CompilerParams(dimension_semantics=sem, vmem_limit_bytes=VMEM_LIMIT_BYTES)


def _pick(n, pref, quantum):
    if n <= pref:
        return n
    t = (pref // quantum) * quantum
    while n % t:
        t -= quantum
    return t


def _mm(a, b, *, dims, grid, a_spec, b_spec, o_spec, out_shape, name, add=None):
    nk = grid[2]
    acc_shape = tuple(d for d in o_spec.block_shape if d is not None)
    o_dtype = out_shape.dtype

    def body(*refs):
        if add is None:
            a_ref, b_ref, o_ref, acc_ref = refs
            add_ref = None
        else:
            a_ref, b_ref, add_ref, o_ref, acc_ref = refs
        k = pl.program_id(2)
        part = lax.dot_general(a_ref[...], b_ref[...], dims, preferred_element_type=F32)

        def finish(total):
            if add_ref is not None:
                total = total + add_ref[...]
            o_ref[...] = total.astype(o_dtype)

        if nk == 1:
            finish(part)
            return

        @pl.when(k == 0)
        def _():
            acc_ref[...] = part

        @pl.when(jnp.logical_and(k > 0, k < nk - 1))
        def _():
            acc_ref[...] += part

        @pl.when(k == nk - 1)
        def _():
            finish(acc_ref[...] + part)

    operands = [a, b] if add is None else [a, b, add]
    in_specs = [a_spec, b_spec] if add is None else [a_spec, b_spec, o_spec]
    return pl.pallas_call(
        body, name=name, grid=grid, in_specs=in_specs, out_specs=o_spec, out_shape=out_shape,
        scratch_shapes=[pltpu.VMEM(acc_shape, F32)],
        compiler_params=_params("parallel", "parallel", "arbitrary"),
    )(*operands)


def _mm_nn(a, w, l, out_dtype, name, *, col0=0, n=None, split_out=None):
    m, kdim = a.shape
    n = w.shape[2] if n is None else n
    tm, tk = _pick(m, MM_ROWS, SUBLANES), _pick(kdim, 1024, LANES)
    tn = _pick(n if split_out is None else split_out, 1024, LANES)
    jb = col0 // tn
    grid = (m // tm, n // tn, kdim // tk)
    a_spec = pl.BlockSpec((tm, tk), lambda i, j, k: (i, k))
    b_spec = pl.BlockSpec((None, tk, tn), lambda i, j, k: (l, k, j + jb))
    if split_out is None:
        o_spec = pl.BlockSpec((tm, tn), lambda i, j, k: (i, j))
        shape = jax.ShapeDtypeStruct((m, n), out_dtype)
    else:
        nj1 = split_out // tn
        o_spec = pl.BlockSpec((None, tm, tn), lambda i, j, k: (j // nj1, i, j % nj1))
        shape = jax.ShapeDtypeStruct((n // split_out, m, split_out), out_dtype)
    return _mm(a, w, dims=NN, grid=grid, a_spec=a_spec, b_spec=b_spec, o_spec=o_spec, out_shape=shape, name=name)


def _mm_nt(a, w, l, out_dtype, name, *, add=None):
    n, kdim = w.shape[1], w.shape[2]
    if a.ndim == 2:
        m = a.shape[0]
        tk = _pick(kdim, 1024, LANES)
        a_spec_of = lambda tm: pl.BlockSpec((tm, tk), lambda i, j, k: (i, k))
    else:
        m, seg = a.shape[1], a.shape[2]
        tk = _pick(seg, 1024, LANES)
        nk1 = seg // tk
        a_spec_of = lambda tm: pl.BlockSpec((None, tm, tk), lambda i, j, k: (k // nk1, i, k % nk1))
    tm, tn = _pick(m, MM_ROWS, SUBLANES), _pick(n, 1024, LANES)
    grid = (m // tm, n // tn, kdim // tk)
    b_spec = pl.BlockSpec((None, tn, tk), lambda i, j, k: (l, j, k))
    o_spec = pl.BlockSpec((tm, tn), lambda i, j, k: (i, j))
    return _mm(a, w, dims=NT, grid=grid, a_spec=a_spec_of(tm), b_spec=b_spec, o_spec=o_spec,
               out_shape=jax.ShapeDtypeStruct((m, n), out_dtype), name=name, add=add)


def _mm_tn(a, b, out_dtype, name):
    kdim, m = a.shape
    tk, tm = _pick(kdim, 1024, SUBLANES), _pick(m, MM_ROWS, LANES)
    if b.ndim == 2:
        n = b.shape[1]
        tn = _pick(n, 1024, LANES)
        b_spec = pl.BlockSpec((tk, tn), lambda i, j, k: (k, j))
    else:
        seg = b.shape[2]
        n = b.shape[0] * seg
        tn = _pick(seg, 1024, LANES)
        nj1 = seg // tn
        b_spec = pl.BlockSpec((None, tk, tn), lambda i, j, k: (j // nj1, k, j % nj1))
    grid = (m // tm, n // tn, kdim // tk)
    a_spec = pl.BlockSpec((tk, tm), lambda i, j, k: (k, i))
    o_spec = pl.BlockSpec((tm, tn), lambda i, j, k: (i, j))
    return _mm(a, b, dims=TN, grid=grid, a_spec=a_spec, b_spec=b_spec, o_spec=o_spec,
               out_shape=jax.ShapeDtypeStruct((m, n), out_dtype), name=name)


def _rstd(x):
    return lax.rsqrt(jnp.mean(x * x, axis=-1, keepdims=True) + RMS_EPS)


def _norm_mod_fwd(x, g, shift, scale, name):
    s, d = x.shape
    tm = _pick(s, 512, SUBLANES)

    def body(x_ref, g_ref, sh_ref, sc_ref, h_ref):
        xv = x_ref[...]
        y = (xv * _rstd(xv)) * g_ref[...]
        h_ref[...] = (y * (1.0 + sc_ref[...]) + sh_ref[...]).astype(BF16)

    row = pl.BlockSpec((tm, d), lambda i: (i, 0))
    vec = pl.BlockSpec((1, d), lambda i: (0, 0))
    return pl.pallas_call(body, name=name, grid=(s // tm,), in_specs=[row, vec, vec, vec], out_specs=row,
                          out_shape=jax.ShapeDtypeStruct((s, d), BF16), compiler_params=_params("parallel"))(x, g, shift, scale)


def _norm_mod_bwd(dh, x, g, scale, dres, name):
    s, d = x.shape
    tm = _pick(s, 256, SUBLANES)

    def body(dh_ref, x_ref, g_ref, sc_ref, dres_ref, dx_ref, dsh_ref, dsc_ref, dg_ref):
        i = pl.program_id(0)
        xv, dhv, gv = x_ref[...], dh_ref[...], g_ref[...]
        r = _rstd(xv)
        xh = xv * r
        dn = dhv * (1.0 + sc_ref[...])
        gd = dn * gv
        dx_ref[...] = dres_ref[...] + r * (gd - xh * jnp.mean(gd * xh, axis=-1, keepdims=True))

        @pl.when(i == 0)
        def _():
            dsh_ref[...] = jnp.zeros_like(dsh_ref)
            dsc_ref[...] = jnp.zeros_like(dsc_ref)
            dg_ref[...] = jnp.zeros_like(dg_ref)

        dsh_ref[...] += jnp.sum(dhv, axis=0, keepdims=True)
        dsc_ref[...] += jnp.sum(dhv * (xh * gv), axis=0, keepdims=True)
        dg_ref[...] += jnp.sum(dn * xh, axis=0, keepdims=True)

    row = pl.BlockSpec((tm, d), lambda i: (i, 0))
    vec = pl.BlockSpec((1, d), lambda i: (0, 0))
    vshape = jax.ShapeDtypeStruct((1, d), F32)
    return pl.pallas_call(body, name=name, grid=(s // tm,), in_specs=[row, row, vec, vec, row],
                          out_specs=[row, vec, vec, vec],
                          out_shape=[jax.ShapeDtypeStruct((s, d), F32), vshape, vshape, vshape],
                          compiler_params=_params("arbitrary"))(dh, x, g, scale, dres)


def _post_fwd(x, u, g, gate, name):
    s, d = x.shape
    tm = _pick(s, 512, SUBLANES)

    def body(x_ref, u_ref, g_ref, gt_ref, o_ref):
        uv = u_ref[...]
        o_ref[...] = x_ref[...] + gt_ref[...] * ((uv * _rstd(uv)) * g_ref[...])

    row = pl.BlockSpec((tm, d), lambda i: (i, 0))
    vec = pl.BlockSpec((1, d), lambda i: (0, 0))
    return pl.pallas_call(body, name=name, grid=(s // tm,), in_specs=[row, row, vec, vec], out_specs=row,
                          out_shape=jax.ShapeDtypeStruct((s, d), F32), compiler_params=_params("parallel"))(x, u, g, gate)


def _post_bwd(dx, u, g, gate, name):
    s, d = u.shape
    tm = _pick(s, 256, SUBLANES)

    def body(dx_ref, u_ref, g_ref, gt_ref, du_ref, dgt_ref, dg_ref):
        i = pl.program_id(0)
        uv, dxv, gv = u_ref[...], dx_ref[...], g_ref[...]
        r = _rstd(uv)
        uh = uv * r
        dn = dxv * gt_ref[...]
        gd = dn * gv
        du_ref[...] = (r * (gd - uh * jnp.mean(gd * uh, axis=-1, keepdims=True))).astype(BF16)

        @pl.when(i == 0)
        def _():
            dgt_ref[...] = jnp.zeros_like(dgt_ref)
            dg_ref[...] = jnp.zeros_like(dg_ref)

        dgt_ref[...] += jnp.sum(dxv * (uh * gv), axis=0, keepdims=True)
        dg_ref[...] += jnp.sum(dn * uh, axis=0, keepdims=True)

    row = pl.BlockSpec((tm, d), lambda i: (i, 0))
    vec = pl.BlockSpec((1, d), lambda i: (0, 0))
    vshape = jax.ShapeDtypeStruct((1, d), F32)
    return pl.pallas_call(body, name=name, grid=(s // tm,), in_specs=[row, row, vec, vec],
                          out_specs=[row, vec, vec],
                          out_shape=[jax.ShapeDtypeStruct((s, d), BF16), vshape, vshape],
                          compiler_params=_params("arbitrary"))(dx, u, g, gate)


def _shift_rows(cur, prev8, k):
    rolled = pltpu.roll(cur, k, axis=0)
    rolled_prev = pltpu.roll(prev8, k, axis=0)
    i8 = lax.broadcasted_iota(jnp.int32, prev8.shape, 0)
    top = jnp.where(i8 < k, rolled_prev, rolled[:SUBLANES])
    return jnp.concatenate([top, rolled[SUBLANES:]], axis=0)


def _conv_pre(g, prev8, wc_ref, bc_ref):
    s1 = _shift_rows(g, prev8, 1)
    s2 = _shift_rows(g, prev8, 2)
    gc = bc_ref[...] + wc_ref[0:1, :] * s2 + wc_ref[1:2, :] * s1 + wc_ref[2:3, :] * g
    return gc, s1, s2


def _conv_act_fwd(gp, up, wc, bc, name):
    s, f = gp.shape
    tm, tf = _pick(s, 512, SUBLANES), _pick(f, 768, LANES)
    r8 = tm // SUBLANES

    def body(g_ref, gprev_ref, up_ref, wc_ref, bc_ref, a_ref):
        i = pl.program_id(1)
        prev = jnp.where(i == 0, 0.0, gprev_ref[...])
        gc, _, _ = _conv_pre(g_ref[...], prev, wc_ref, bc_ref)
        a_ref[...] = ((gc * jax.nn.sigmoid(gc)) * up_ref[...]).astype(BF16)

    tile = pl.BlockSpec((tm, tf), lambda j, i: (i, j))
    prev = pl.BlockSpec((SUBLANES, tf), lambda j, i: (jnp.maximum(i * r8 - 1, 0), j))
    return pl.pallas_call(body, name=name, grid=(f // tf, s // tm),
                          in_specs=[tile, prev, tile, pl.BlockSpec((3, tf), lambda j, i: (0, j)),
                                    pl.BlockSpec((1, tf), lambda j, i: (0, j))],
                          out_specs=tile, out_shape=jax.ShapeDtypeStruct((s, f), BF16),
                          compiler_params=_params("parallel", "parallel"))(gp, gp, up, wc, bc)


def _conv_act_bwd(da, gp, up, wc, bc, name):
    s, f = gp.shape
    tm, tf = _pick(s, 256, SUBLANES), _pick(f, 768, LANES)
    r8 = tm // SUBLANES
    nrow = s // tm

    def body(da_ref, dan_ref, g_ref, gprev_ref, gn_ref, up_ref, upn_ref, wc_ref, bc_ref,
             dgp_ref, dup_ref, dwc_ref, dbc_ref):
        i = pl.program_id(1)
        last = i == nrow - 1
        prev = jnp.where(i == 0, 0.0, gprev_ref[...])
        g_ext = jnp.concatenate([g_ref[...], gn_ref[...]], axis=0)
        up_ext = jnp.concatenate([up_ref[...], upn_ref[...]], axis=0)
        da_ext = jnp.concatenate([da_ref[...], jnp.where(last, 0.0, dan_ref[...])], axis=0)
        gc, s1, s2 = _conv_pre(g_ext, prev, wc_ref, bc_ref)
        sg = jax.nn.sigmoid(gc)
        dup_ref[...] = (da_ext * (gc * sg))[:tm].astype(BF16)
        dgc = da_ext * up_ext * (sg * (1.0 + gc * (1.0 - sg)))
        ext = tm + SUBLANES
        dgp = (wc_ref[2:3, :] * dgc + wc_ref[1:2, :] * pltpu.roll(dgc, ext - 1, axis=0)
               + wc_ref[0:1, :] * pltpu.roll(dgc, ext - 2, axis=0))
        dgp_ref[...] = dgp[:tm].astype(BF16)

        @pl.when(i == 0)
        def _():
            dwc_ref[...] = jnp.zeros_like(dwc_ref)
            dbc_ref[...] = jnp.zeros_like(dbc_ref)

        d0 = dgc[:tm]
        dwc_ref[0:1, :] += jnp.sum(d0 * s2[:tm], axis=0, keepdims=True)
        dwc_ref[1:2, :] += jnp.sum(d0 * s1[:tm], axis=0, keepdims=True)
        dwc_ref[2:3, :] += jnp.sum(d0 * g_ext[:tm], axis=0, keepdims=True)
        dbc_ref[...] += jnp.sum(d0, axis=0, keepdims=True)

    tile = pl.BlockSpec((tm, tf), lambda j, i: (i, j))
    prev = pl.BlockSpec((SUBLANES, tf), lambda j, i: (jnp.maximum(i * r8 - 1, 0), j))
    nxt = pl.BlockSpec((SUBLANES, tf), lambda j, i: (jnp.minimum((i + 1) * r8, s // SUBLANES - 1), j))
    return pl.pallas_call(body, name=name, grid=(f // tf, nrow),
                          in_specs=[tile, nxt, tile, prev, nxt, tile, nxt,
                                    pl.BlockSpec((3, tf), lambda j, i: (0, j)), pl.BlockSpec((1, tf), lambda j, i: (0, j))],
                          out_specs=[tile, tile, pl.BlockSpec((3, tf), lambda j, i: (0, j)),
                                     pl.BlockSpec((1, tf), lambda j, i: (0, j))],
                          out_shape=[jax.ShapeDtypeStruct((s, f), BF16), jax.ShapeDtypeStruct((s, f), BF16),
                                     jax.ShapeDtypeStruct((3, f), F32), jax.ShapeDtypeStruct((1, f), F32)],
                          compiler_params=_params("parallel", "arbitrary"))(da, da, gp, gp, gp, up, up, wc, bc)


def _split_bf16(v, parts):
    out, rem = [], v
    for _ in range(parts):
        t = rem.astype(BF16)
        out.append(t)
        rem = rem - t.astype(F32)
    return jnp.concatenate(out, axis=1)


def _tri(t, cmp, reps):
    row = lax.broadcasted_iota(jnp.int32, (t, t), 0)
    col = lax.broadcasted_iota(jnp.int32, (t, t), 1)
    m = cmp(row, col).astype(BF16)
    return jnp.concatenate([m] * reps, axis=0)


def _log_sigmoid(z):
    mn = jnp.minimum(z, 0.0)
    return mn - jnp.log(1.0 + jnp.exp(mn + (mn - z)))


def _log_one_minus_sigmoid(z):
    mn = jnp.minimum(z, 0.0)
    neg = mn - z
    return neg - jnp.log(1.0 + jnp.exp(mn + neg))


def _pair(xv):
    lane = lax.broadcasted_iota(jnp.int32, xv.shape, 1)
    zero = jnp.zeros_like(xv)
    return jnp.where(lane < HEAD_DIM, xv, zero), jnp.where(lane < HEAD_DIM, zero, xv)


def _attn_specs(s, tq):
    q_spec = pl.BlockSpec((None, tq, LANES), lambda hp, qi: (0, qi, hp))
    k_spec = pl.BlockSpec((None, s, LANES), lambda hp, qi: (1, 0, hp))
    v_spec = pl.BlockSpec((None, s, LANES), lambda hp, qi: (2, 0, hp))
    return q_spec, k_spec, v_spec


def _rel_mask(tq, tk, i, strict):
    row = lax.broadcasted_iota(jnp.int32, (tq, tk), 0)
    col = lax.broadcasted_iota(jnp.int32, (tq, tk), 1) + i * tk
    return col < row if strict else col <= row


def _sb_fwd(qkv, tq, tk, name):
    _, s, d = qkv.shape
    hp_n, nq, nd = d // LANES, s // tq, tq // tk
    scale = HEAD_DIM ** -0.5

    def body(q_ref, k_ref, v_ref, o_ref, lt_ref):
        qi = pl.program_id(1)
        lane = lax.broadcasted_iota(jnp.int32, (tq, LANES), 1)
        onward = _tri(tk, lambda j, sidx: j >= sidx, 2)
        qpair = _pair(q_ref[...] * scale)

        def block(kb, carry, mask):
            acc, runs = carry[0], carry[1:]
            ks = pl.multiple_of(kb * tk, tk)
            kblk = k_ref[pl.ds(ks, tk), :]
            vpair = _pair(v_ref[pl.ds(ks, tk), :])
            probs, new_runs = [], []
            for qh, run in zip(qpair, runs):
                z = lax.dot_general(qh, kblk, NT, preferred_element_type=F32)
                l1 = _log_one_minus_sigmoid(z)
                if mask is not None:
                    l1 = jnp.where(mask, l1, 0.0)
                c = jnp.dot(_split_bf16(l1, 2), onward, preferred_element_type=F32)
                a = jnp.exp(z + c + run)
                if mask is not None:
                    a = jnp.where(mask, a, 0.0)
                probs.append(a.astype(BF16))
                new_runs.append(run + jnp.sum(l1, axis=1, keepdims=True))
            acc = acc + jnp.dot(jnp.concatenate(probs, axis=1), jnp.concatenate(vpair, axis=0),
                                preferred_element_type=F32)
            return (acc, *new_runs)

        zero1 = jnp.zeros((tq, 1), F32)
        carry = (jnp.zeros((tq, LANES), F32), zero1, zero1)
        for i in reversed(range(nd)):
            carry = block(qi * nd + i, carry, _rel_mask(tq, tk, i, True))
        acc, run0, run1 = lax.fori_loop(0, qi * nd, lambda j, cr: block(qi * nd - 1 - j, cr, None), carry)
        o_ref[...] = acc.astype(BF16)
        lt_ref[...] = jnp.where(lane < HEAD_DIM, run0, run1)

    q_spec, k_spec, v_spec = _attn_specs(s, tq)
    return pl.pallas_call(
        body, name=name, grid=(hp_n, nq), in_specs=[q_spec, k_spec, v_spec],
        out_specs=[pl.BlockSpec((tq, LANES), lambda hp, qi: (qi, hp)),
                   pl.BlockSpec((None, tq, LANES), lambda hp, qi: (hp, qi, 0))],
        out_shape=[jax.ShapeDtypeStruct((s, d), BF16), jax.ShapeDtypeStruct((hp_n, s, LANES), F32)],
        compiler_params=_params("parallel", "parallel"))(qkv, qkv, qkv)


def _sb_bwd(qkv, ltot, do, tq, tk, name):
    _, s, d = qkv.shape
    hp_n, nq, nd = d // LANES, s // tq, tq // tk
    scale = HEAD_DIM ** -0.5

    def body(q_ref, k_ref, v_ref, lt_ref, do_ref, dqkv_ref, dk_acc, dv_acc):
        qi = pl.program_id(1)

        @pl.when(qi == 0)
        def _():
            dk_acc[...] = jnp.zeros_like(dk_acc)
            dv_acc[...] = jnp.zeros_like(dv_acc)

        onward = _tri(tk, lambda j, sidx: j >= sidx, 2)
        before = _tri(tk, lambda j, sidx: j < sidx, 2)
        qpair = _pair(q_ref[...] * scale)
        dopair = _pair(do_ref[...])
        ltv = lt_ref[...]
        lts = (ltv[:, 0:1], ltv[:, HEAD_DIM:HEAD_DIM + 1])

        def block(kb, carry, mask):
            dq, lpres, gpres = carry[0], carry[1:3], carry[3:5]
            ks = pl.multiple_of(kb * tk, tk)
            kblk, vblk = k_ref[pl.ds(ks, tk), :], v_ref[pl.ds(ks, tk), :]
            kpair = _pair(kblk * scale)
            dzs, probs, new_l, new_g = [], [], [], []
            for qh, doh, lt, lpre, gpre in zip(qpair, dopair, lts, lpres, gpres):
                z = lax.dot_general(qh, kblk, NT, preferred_element_type=F32)
                l1 = _log_one_minus_sigmoid(z)
                if mask is not None:
                    l1 = jnp.where(mask, l1, 0.0)
                lpre = lpre + jnp.sum(l1, axis=1, keepdims=True)
                c = jnp.dot(_split_bf16(l1, 2), onward, preferred_element_type=F32)
                a = jnp.exp(z + c + (lt - lpre))
                if mask is not None:
                    a = jnp.where(mask, a, 0.0)
                da = lax.dot_general(doh, vblk, NT, preferred_element_type=F32)
                g = da * a
                p = jnp.dot(_split_bf16(g, 2), before, preferred_element_type=F32) + gpre
                dz = g - jnp.exp(z + l1) * (g + p)
                if mask is not None:
                    dz = jnp.where(mask, dz, 0.0)
                dzs.append(dz.astype(BF16))
                probs.append(a.astype(BF16))
                new_l.append(lpre)
                new_g.append(gpre + jnp.sum(g, axis=1, keepdims=True))
            dq = dq + jnp.dot(jnp.concatenate(dzs, axis=1), jnp.concatenate(kpair, axis=0),
                              preferred_element_type=F32)
            dk_acc[pl.ds(ks, tk), :] += lax.dot_general(jnp.concatenate(dzs, axis=0), jnp.concatenate(qpair, axis=0),
                                                        TN, preferred_element_type=F32)
            dv_acc[pl.ds(ks, tk), :] += lax.dot_general(jnp.concatenate(probs, axis=0), jnp.concatenate(dopair, axis=0),
                                                        TN, preferred_element_type=F32)
            return (dq, *new_l, *new_g)

        zero1 = jnp.zeros((tq, 1), F32)
        init = (jnp.zeros((tq, LANES), F32), zero1, zero1, zero1, zero1)
        carry = lax.fori_loop(0, qi * nd, lambda j, cr: block(j, cr, None), init)
        for i in range(nd):
            carry = block(qi * nd + i, carry, _rel_mask(tq, tk, i, True))
        dqkv_ref[0, pl.ds(pl.multiple_of(qi * tq, tq), tq), :] = carry[0].astype(BF16)

        @pl.when(qi == nq - 1)
        def _():
            dqkv_ref[1] = dk_acc[...].astype(BF16)
            dqkv_ref[2] = dv_acc[...].astype(BF16)

    q_spec, k_spec, v_spec = _attn_specs(s, tq)
    return pl.pallas_call(
        body, name=name, grid=(hp_n, nq),
        in_specs=[q_spec, k_spec, v_spec, pl.BlockSpec((None, tq, LANES), lambda hp, qi: (hp, qi, 0)),
                  pl.BlockSpec((tq, LANES), lambda hp, qi: (qi, hp))],
        out_specs=pl.BlockSpec((3, s, LANES), lambda hp, qi: (0, 0, hp)),
        out_shape=jax.ShapeDtypeStruct((3, s, d), BF16),
        scratch_shapes=[pltpu.VMEM((s, LANES), F32), pltpu.VMEM((s, LANES), F32)],
        compiler_params=_params("parallel", "arbitrary"))(qkv, qkv, qkv, ltot, do)


def _fox_prep(fl, bias, name):
    s, w = fl.shape
    tb = _pick(s, 512, SUBLANES)

    def body(fl_ref, b_ref, cum_ref, carry_ref):
        i = pl.program_id(0)

        @pl.when(i == 0)
        def _():
            carry_ref[...] = jnp.zeros_like(carry_ref)

        logf = _log_sigmoid(fl_ref[...] + b_ref[...])
        row = lax.broadcasted_iota(jnp.int32, (tb, tb), 0)
        col = lax.broadcasted_iota(jnp.int32, (tb, tb), 1)
        incl = (col <= row).astype(BF16)
        tot = carry_ref[...]
        rem = logf
        for _ in range(3):
            part = rem.astype(BF16)
            tot = tot + jnp.dot(incl, part, preferred_element_type=F32)
            rem = rem - part.astype(F32)
        cum_ref[...] = tot
        carry_ref[...] = tot[tb - 1:tb, :]

    blk = pl.BlockSpec((tb, w), lambda i: (i, 0))
    return pl.pallas_call(body, name=name, grid=(s // tb,), in_specs=[blk, pl.BlockSpec((1, w), lambda i: (0, 0))],
                          out_specs=blk, out_shape=jax.ShapeDtypeStruct((s, w), F32),
                          scratch_shapes=[pltpu.VMEM((1, w), F32)], compiler_params=_params("arbitrary"))(fl, bias)


def _fox_gate_bwd(dcum, fl, bias, n_heads, name):
    s, w = fl.shape
    tb = _pick(s, 512, SUBLANES)
    nb = s // tb

    def body(dc_ref, fl_ref, b_ref, dfl_ref, db_ref, carry_ref):
        i = pl.program_id(0)

        @pl.when(i == 0)
        def _():
            carry_ref[...] = jnp.zeros_like(carry_ref)
            db_ref[...] = jnp.zeros_like(db_ref)

        row = lax.broadcasted_iota(jnp.int32, (tb, tb), 0)
        col = lax.broadcasted_iota(jnp.int32, (tb, tb), 1)
        incl = (col >= row).astype(BF16)
        tot = jnp.broadcast_to(carry_ref[...], (tb, w))
        rem = dc_ref[...]
        for _ in range(3):
            part = rem.astype(BF16)
            tot = tot + jnp.dot(incl, part, preferred_element_type=F32)
            rem = rem - part.astype(F32)
        carry_ref[...] = tot[0:1, :]
        xg = fl_ref[...] + b_ref[...]
        e = jnp.exp(-jnp.abs(xg))
        sig_neg = jnp.where(xg >= 0.0, e, 1.0) / (1.0 + e)
        lane = lax.broadcasted_iota(jnp.int32, (tb, w), 1)
        dfl = jnp.where(lane < n_heads, tot * sig_neg, 0.0)
        dfl_ref[...] = dfl.astype(BF16)
        db_ref[...] += jnp.sum(dfl, axis=0, keepdims=True)

    blk = pl.BlockSpec((tb, w), lambda i: (nb - 1 - i, 0))
    vec = pl.BlockSpec((1, w), lambda i: (0, 0))
    return pl.pallas_call(body, name=name, grid=(nb,), in_specs=[blk, blk, vec], out_specs=[blk, vec],
                          out_shape=[jax.ShapeDtypeStruct((s, w), BF16), jax.ShapeDtypeStruct((1, w), F32)],
                          scratch_shapes=[pltpu.VMEM((1, w), F32)], compiler_params=_params("arbitrary"))(dcum, fl, bias)


def _fox_fwd(qkv, cq, ck, tq, tk, name):
    _, s, d = qkv.shape
    hp_n, nq, nd = d // LANES, s // tq, tq // tk
    scale = HEAD_DIM ** -0.5

    def body(q_ref, k_ref, v_ref, cq_ref, ck_ref, o_ref, lse_ref):
        qi = pl.program_id(1)
        low = lax.broadcasted_iota(jnp.int32, (tq, LANES), 1) < HEAD_DIM
        qpair = _pair(q_ref[...] * scale)
        cqv = cq_ref[...]
        cqs = (cqv[:, 0:1], cqv[:, HEAD_DIM:HEAD_DIM + 1])

        def block(kb, carry, mask):
            acc, stats = carry[0], carry[1:]
            ks = pl.multiple_of(kb * tk, tk)
            kblk = k_ref[pl.ds(ks, tk), :]
            vpair = _pair(v_ref[pl.ds(ks, tk), :])
            probs, alphas, new = [], [], []
            for h, (qh, cqh) in enumerate(zip(qpair, cqs)):
                m, lsum = stats[2 * h], stats[2 * h + 1]
                z = lax.dot_general(qh, kblk, NT, preferred_element_type=F32)
                sc = z + (cqh - ck_ref[h:h + 1, pl.ds(ks, tk)])
                if mask is not None:
                    sc = jnp.where(mask, sc, -jnp.inf)
                m_new = jnp.maximum(m, jnp.max(sc, axis=1, keepdims=True))
                alpha = jnp.exp(m - m_new)
                p = jnp.exp(sc - m_new)
                new += [m_new, alpha * lsum + jnp.sum(p, axis=1, keepdims=True)]
                probs.append(p.astype(BF16))
                alphas.append(alpha)
            acc = jnp.where(low, alphas[0], alphas[1]) * acc + jnp.dot(
                jnp.concatenate(probs, axis=1), jnp.concatenate(vpair, axis=0), preferred_element_type=F32)
            return (acc, *new)

        neg = jnp.full((tq, 1), -jnp.inf, F32)
        zero1 = jnp.zeros((tq, 1), F32)
        carry = (jnp.zeros((tq, LANES), F32), neg, zero1, neg, zero1)
        for i in range(nd):
            carry = block(qi * nd + i, carry, _rel_mask(tq, tk, i, False))
        acc, m0, l0, m1, l1 = lax.fori_loop(0, qi * nd, lambda j, cr: block(qi * nd - 1 - j, cr, None), carry)
        o_ref[...] = (acc / jnp.where(low, l0, l1)).astype(BF16)
        lse_ref[...] = jnp.where(low, m0 + jnp.log(l0), m1 + jnp.log(l1))

    q_spec, k_spec, v_spec = _attn_specs(s, tq)
    pair_rows = pl.BlockSpec((None, tq, LANES), lambda hp, qi: (hp, qi, 0))
    return pl.pallas_call(
        body, name=name, grid=(hp_n, nq),
        in_specs=[q_spec, k_spec, v_spec, pair_rows, pl.BlockSpec((None, 2, s), lambda hp, qi: (hp, 0, 0))],
        out_specs=[pl.BlockSpec((tq, LANES), lambda hp, qi: (qi, hp)), pair_rows],
        out_shape=[jax.ShapeDtypeStruct((s, d), BF16), jax.ShapeDtypeStruct((hp_n, s, LANES), F32)],
        compiler_params=_params("parallel", "parallel"))(qkv, qkv, qkv, cq, ck)


def _fox_bwd(qkv, o, do, lse, cq, ck, tq, tk, name):
    _, s, d = qkv.shape
    hp_n, nq, nd = d // LANES, s // tq, tq // tk
    scale = HEAD_DIM ** -0.5

    def body(q_ref, k_ref, v_ref, o_ref, do_ref, lse_ref, cq_ref, ck_ref, dqkv_ref, dcq_ref, dck_ref, dk_acc, dv_acc):
        qi = pl.program_id(1)

        @pl.when(qi == 0)
        def _():
            dk_acc[...] = jnp.zeros_like(dk_acc)
            dv_acc[...] = jnp.zeros_like(dv_acc)
            dck_ref[...] = jnp.zeros_like(dck_ref)

        low = lax.broadcasted_iota(jnp.int32, (tq, LANES), 1) < HEAD_DIM
        dov, cqv, lsev = do_ref[...], cq_ref[...], lse_ref[...]
        qpair = _pair(q_ref[...] * scale)
        dopair = _pair(dov)
        prod = dov.astype(F32) * o_ref[...].astype(F32)
        deltas = (jnp.sum(jnp.where(low, prod, 0.0), axis=1, keepdims=True),
                  jnp.sum(jnp.where(low, 0.0, prod), axis=1, keepdims=True))
        cqs = (cqv[:, 0:1], cqv[:, HEAD_DIM:HEAD_DIM + 1])
        lses = (lsev[:, 0:1], lsev[:, HEAD_DIM:HEAD_DIM + 1])

        def block(kb, carry, mask):
            dq, rowsums = carry[0], carry[1:]
            ks = pl.multiple_of(kb * tk, tk)
            kblk, vblk = k_ref[pl.ds(ks, tk), :], v_ref[pl.ds(ks, tk), :]
            kpair = _pair(kblk * scale)
            dss, probs, new_rows = [], [], []
            for h, (qh, doh) in enumerate(zip(qpair, dopair)):
                z = lax.dot_general(qh, kblk, NT, preferred_element_type=F32)
                sc = z + (cqs[h] - ck_ref[h:h + 1, pl.ds(ks, tk)])
                p = jnp.exp(sc - lses[h])
                if mask is not None:
                    p = jnp.where(mask, p, 0.0)
                dp = lax.dot_general(doh, vblk, NT, preferred_element_type=F32)
                ds = p * (dp - deltas[h])
                dck_ref[h:h + 1, pl.ds(ks, tk)] -= jnp.sum(ds, axis=0, keepdims=True)
                new_rows.append(rowsums[h] + jnp.sum(ds, axis=1, keepdims=True))
                dss.append(ds.astype(BF16))
                probs.append(p.astype(BF16))
            dq = dq + jnp.dot(jnp.concatenate(dss, axis=1), jnp.concatenate(kpair, axis=0),
                              preferred_element_type=F32)
            dk_acc[pl.ds(ks, tk), :] += lax.dot_general(jnp.concatenate(dss, axis=0), jnp.concatenate(qpair, axis=0),
                                                       TN, preferred_element_type=F32)
            dv_acc[pl.ds(ks, tk), :] += lax.dot_general(jnp.concatenate(probs, axis=0), jnp.concatenate(dopair, axis=0),
                                                       TN, preferred_element_type=F32)
            return (dq, *new_rows)

        zero1 = jnp.zeros((tq, 1), F32)
        carry = lax.fori_loop(0, qi * nd, lambda j, cr: block(j, cr, None), (jnp.zeros((tq, LANES), F32), zero1, zero1))
        for i in range(nd):
            carry = block(qi * nd + i, carry, _rel_mask(tq, tk, i, False))
        dq, rs0, rs1 = carry
        dqkv_ref[0, pl.ds(pl.multiple_of(qi * tq, tq), tq), :] = dq.astype(BF16)
        dcq_ref[...] = jnp.where(low, rs0, rs1)

        @pl.when(qi == nq - 1)
        def _():
            dqkv_ref[1] = dk_acc[...].astype(BF16)
            dqkv_ref[2] = dv_acc[...].astype(BF16)

    q_spec, k_spec, v_spec = _attn_specs(s, tq)
    pair_rows = pl.BlockSpec((None, tq, LANES), lambda hp, qi: (hp, qi, 0))
    tile = pl.BlockSpec((tq, LANES), lambda hp, qi: (qi, hp))
    keys = pl.BlockSpec((None, 2, s), lambda hp, qi: (hp, 0, 0))
    return pl.pallas_call(
        body, name=name, grid=(hp_n, nq),
        in_specs=[q_spec, k_spec, v_spec, tile, tile, pair_rows, pair_rows, keys],
        out_specs=[pl.BlockSpec((3, s, LANES), lambda hp, qi: (0, 0, hp)), pair_rows, keys],
        out_shape=[jax.ShapeDtypeStruct((3, s, d), BF16), jax.ShapeDtypeStruct((hp_n, s, LANES), F32),
                   jax.ShapeDtypeStruct((hp_n, 2, s), F32)],
        scratch_shapes=[pltpu.VMEM((s, LANES), F32), pltpu.VMEM((s, LANES), F32)],
        compiler_params=_params("parallel", "arbitrary"))(qkv, qkv, qkv, o, do, lse, cq, ck)


def _loss_head(y, target, name):
    s, d = y.shape
    tm = _pick(s, 512, SUBLANES)

    def body(y_ref, t_ref, dy_ref, sq_ref):
        i = pl.program_id(0)
        diff = y_ref[...] - t_ref[...]
        dy_ref[...] = diff / d

        @pl.when(i == 0)
        def _():
            sq_ref[...] = jnp.zeros_like(sq_ref)

        sq_ref[...] += jnp.sum(diff * diff, axis=0, keepdims=True)

    row = pl.BlockSpec((tm, d), lambda i: (i, 0))
    vec = pl.BlockSpec((1, d), lambda i: (0, 0))
    return pl.pallas_call(body, name=name, grid=(s // tm,), in_specs=[row, row], out_specs=[row, vec],
                          out_shape=[jax.ShapeDtypeStruct((s, d), F32), jax.ShapeDtypeStruct((1, d), F32)],
                          compiler_params=_params("arbitrary"))(y, target)


def _mod_fwd(c_all, w_mod, b_mod_cols, name):
    nl, d, cols = w_mod.shape
    nb = c_all.shape[0]

    def body(c_ref, w_ref, b_ref, o_ref):
        cv = c_ref[...]
        act = (cv * jax.nn.sigmoid(cv)).astype(BF16)
        o_ref[...] = jnp.dot(act, w_ref[...].astype(BF16), preferred_element_type=F32) + b_ref[...]

    return pl.pallas_call(
        body, name=name, grid=(nl,),
        in_specs=[pl.BlockSpec((nb, d), lambda l: (0, 0)), pl.BlockSpec((None, d, cols), lambda l: (l, 0, 0)),
                  pl.BlockSpec((None, 1, cols), lambda l: (l, 0, 0))],
        out_specs=pl.BlockSpec((None, nb, cols), lambda l: (l, 0, 0)),
        out_shape=jax.ShapeDtypeStruct((nl, nb, cols), F32), compiler_params=_params("parallel"))(c_all, w_mod, b_mod_cols)


def _wmod_grad(c_all_t, dmod, name):
    d, nb = c_all_t.shape
    _, nl, cols = dmod.shape

    def body(c_ref, dm_ref, o_ref):
        cv = c_ref[...]
        act = cv * jax.nn.sigmoid(cv)
        tot = act[:, 0:1] * dm_ref[0]
        for b in range(1, nb):
            tot = tot + act[:, b:b + 1] * dm_ref[b]
        o_ref[...] = tot

    return pl.pallas_call(
        body, name=name, grid=(nl,),
        in_specs=[pl.BlockSpec((d, nb), lambda l: (0, 0)), pl.BlockSpec((nb, None, 1, cols), lambda l: (0, l, 0, 0))],
        out_specs=pl.BlockSpec((None, d, cols), lambda l: (l, 0, 0)),
        out_shape=jax.ShapeDtypeStruct((nl, d, cols), F32), compiler_params=_params("parallel"))(
            c_all_t, dmod.reshape(nb, nl, 1, cols))


def _adamw(recv, w, m, v, name, *, tr=256, tc=512):
    nq, nl, rp, cp = recv.shape
    _, r, c = w.shape
    br = _pick(r, tr, SUBLANES) if rp == r else r
    bc = _pick(c, tc, LANES) if cp == c else c
    rbr = br if rp == r else rp
    rbc = bc if cp == c else cp

    def body(rv_ref, w_ref, m_ref, v_ref, g_ref, d_ref, nm_ref, nv_ref):
        g = rv_ref[0, :br, :bc].astype(F32)
        for qd in range(1, nq):
            g = g + rv_ref[qd, :br, :bc].astype(F32)
        m_new = ADAM_B1 * m_ref[...] + (1.0 - ADAM_B1) * g
        v_new = ADAM_B2 * v_ref[...] + (1.0 - ADAM_B2) * jnp.square(g)
        m_hat = m_new / (1.0 - ADAM_B1 ** ADAM_STEP)
        v_hat = v_new / (1.0 - ADAM_B2 ** ADAM_STEP)
        g_ref[...] = g
        d_ref[...] = -ADAM_LR * (m_hat / (jnp.sqrt(v_hat) + ADAM_EPS) + ADAM_WD * w_ref[...])
        nm_ref[...] = m_new
        nv_ref[...] = v_new

    blk = pl.BlockSpec((None, br, bc), lambda l, i, j: (l, i, j))
    rblk = pl.BlockSpec((nq, None, rbr, rbc), lambda l, i, j: (0, l, i, j))
    shape = jax.ShapeDtypeStruct(w.shape, F32)
    return pl.pallas_call(body, name=name, grid=(nl, r // br, c // bc), in_specs=[rblk, blk, blk, blk],
                          out_specs=[blk] * 4, out_shape=[shape] * 4,
                          compiler_params=_params("parallel", "parallel", "parallel"))(recv, w, m, v)


def _mesh_pos():
    return lax.axis_index("x"), lax.axis_index("y"), lax.axis_index("c")


def _allgather_small(block, name):
    m_per, n = block.shape

    def body(x_ref, out_ref, send_sems, recv_sems, local_sem):
        x, y, c = _mesh_pos()
        me, sibling = (x, y, c), (x, y, 1 - c)
        chips = [(1 - x, y), (x, 1 - y), (1 - x, 1 - y)]

        def rows(px, py, pc):
            return out_ref.at[pl.ds((4 * px + 2 * py + pc) * m_per, m_per), :]

        def copy(k, blk, to, src=None):
            return pltpu.make_async_remote_copy(
                src_ref=rows(*blk) if src is None else src, dst_ref=rows(*blk),
                send_sem=send_sems.at[k], recv_sem=recv_sems.at[k], device_id=to, device_id_type=MESH)

        mine = pltpu.make_async_copy(x_ref, rows(*me), local_sem)
        mine.start()
        first = [copy(0, me, sibling, src=x_ref)]
        first += [copy(1 + j, me, (*chip, c), src=x_ref) for j, chip in enumerate(chips)]
        for cp in first:
            cp.start()
        passed = [copy(4 + j, (*chip, c), sibling) for j, chip in enumerate(chips)]
        for j, chip in enumerate(chips):
            copy(1 + j, (*chip, c), me).wait_recv()
            passed[j].start()
        copy(0, sibling, me).wait_recv()
        for j, chip in enumerate(chips):
            copy(4 + j, (*chip, 1 - c), me).wait_recv()
        for cp in first + passed:
            cp.wait_send()
        mine.wait()

    return pl.pallas_call(
        body, name=name, out_shape=jax.ShapeDtypeStruct((N_DEV * m_per, n), block.dtype),
        in_specs=[pl.BlockSpec(memory_space=pltpu.VMEM)], out_specs=pl.BlockSpec(memory_space=pltpu.VMEM),
        scratch_shapes=[pltpu.SemaphoreType.DMA((7,)), pltpu.SemaphoreType.DMA((7,)), pltpu.SemaphoreType.DMA],
        compiler_params=pltpu.CompilerParams(vmem_limit_bytes=VMEM_LIMIT_BYTES))(block)


def _exchange(srcs, dst_shapes, jobs, name):
    n_src, n_job = len(srcs), len(jobs)

    def body(*refs):
        src_refs, dst_refs = refs[:n_src], refs[n_src:n_src + len(dst_shapes)]
        send_sems, recv_sems, local_sems = refs[n_src + len(dst_shapes):]
        x, y, c = _mesh_pos()
        me = 4 * x + 2 * y + c
        pending = []
        for t, (si, di, src_slice, dst_slice) in enumerate(jobs):
            src, dst = src_refs[si], dst_refs[di]
            lc = pltpu.make_async_copy(src_slice(src, me), dst_slice(dst, me), local_sems.at[t])
            lc.start()
            pending.append(lc)
            for dd in range(1, N_DEV):
                px = 1 - x if dd & 4 else x
                py = 1 - y if dd & 2 else y
                pc = 1 - c if dd & 1 else c
                cp = pltpu.make_async_remote_copy(
                    src_ref=src_slice(src, 4 * px + 2 * py + pc), dst_ref=dst_slice(dst, me),
                    send_sem=send_sems.at[t, dd - 1], recv_sem=recv_sems.at[t, dd - 1],
                    device_id=(px, py, pc), device_id_type=MESH)
                cp.start()
                pending.append(cp)
        for cp in pending:
            cp.wait()

    hbm = pl.BlockSpec(memory_space=pl.ANY)
    return pl.pallas_call(
        body, name=name, out_shape=list(dst_shapes), in_specs=[hbm] * n_src, out_specs=[hbm] * len(dst_shapes),
        scratch_shapes=[pltpu.SemaphoreType.DMA((n_job, N_DEV - 1)), pltpu.SemaphoreType.DMA((n_job, N_DEV - 1)),
                        pltpu.SemaphoreType.DMA((n_job,))])(*srcs)


def _peer(x, y, c, dd):
    return (1 - x if dd & 4 else x, 1 - y if dd & 2 else y, 1 - c if dd & 1 else c)


def _exchange_copies(jobs, src_refs, dst_refs, send_sems, recv_sems, sending, only=None):
    x, y, c = _mesh_pos()
    me = 4 * x + 2 * y + c
    local, remote = [], []
    for t, (si, di, src_slice, dst_slice) in enumerate(jobs):
        if only is not None and t not in only:
            continue
        local.append(pltpu.make_async_copy(src_slice(src_refs[si], me), dst_slice(dst_refs[di], me),
                                           send_sems.at[t * N_DEV]))
        for dd in range(1, N_DEV):
            px, py, pc = _peer(x, y, c, dd)
            p = 4 * px + 2 * py + pc
            remote.append(pltpu.make_async_remote_copy(
                src_ref=src_slice(src_refs[si], p), dst_ref=dst_slice(dst_refs[di], me if sending else p),
                send_sem=send_sems.at[t * N_DEV + dd], recv_sem=recv_sems.at[t * N_DEV + dd],
                device_id=(px, py, pc), device_id_type=MESH))
    return local, remote


def _exchange_start(srcs, dsts, jobs, name, after=None):
    n_src, n_dst, n_job = len(srcs), len(dsts), len(jobs)
    n_in = n_src + n_dst + (after is not None)

    def body(*refs):
        src_refs, dst_refs = refs[:n_src], refs[n_src:n_src + n_dst]
        send_sems, recv_sems = refs[n_in], refs[n_in + 1]
        local, remote = _exchange_copies(jobs, src_refs, dst_refs, send_sems, recv_sems, True)
        for cp in remote + local:
            cp.start()
        refs[-1][...] = jnp.zeros((SUBLANES, LANES), F32)

    hbm = pl.BlockSpec(memory_space=pltpu.HBM)
    sem = pl.BlockSpec(memory_space=pltpu.SEMAPHORE)
    operands = [pltpu.with_memory_space_constraint(a, pltpu.HBM) for a in (*srcs, *dsts)]
    extra_specs = [] if after is None else [pl.BlockSpec(memory_space=pl.ANY)]
    extra = [] if after is None else [after]
    res = pl.pallas_call(
        body, name=name, in_specs=[hbm] * (n_src + n_dst) + extra_specs,
        out_specs=(sem, sem, *[hbm] * (n_src + n_dst), pl.BlockSpec(memory_space=pltpu.VMEM)),
        out_shape=(pltpu.SemaphoreType.DMA((n_job * N_DEV,)), pltpu.SemaphoreType.DMA((n_job * N_DEV,)),
                   *[pltpu.HBM(a.shape, a.dtype) for a in (*srcs, *dsts)],
                   jax.ShapeDtypeStruct((SUBLANES, LANES), F32)),
        input_output_aliases={i: 2 + i for i in range(n_src + n_dst)},
        compiler_params=pltpu.CompilerParams(has_side_effects=pltpu.SideEffectType.DATAFLOW_SIDE_EFFECTING))(
            *operands, *extra)
    return res[0], res[1], list(res[2:2 + n_src]), list(res[2 + n_src:2 + n_src + n_dst]), res[-1]


def _exchange_wait(parts, dsts, after, name):
    n_dst = len(dsts)
    n_src = sum(len(p[2]) for p in parts)

    def body(*refs):
        dst_refs = refs[n_src:n_src + n_dst]
        sem_refs = refs[n_src + n_dst:n_src + n_dst + 2 * len(parts)]
        at = 0
        for k, (_, _, part_srcs, jobs, only) in enumerate(parts):
            src_refs = refs[at:at + len(part_srcs)]
            at += len(part_srcs)
            local, remote = _exchange_copies(jobs, src_refs, dst_refs, sem_refs[2 * k], sem_refs[2 * k + 1], False, only)
            for cp in local:
                cp.wait()
            for cp in remote:
                cp.wait_send()
                cp.wait_recv()

    hbm = pl.BlockSpec(memory_space=pltpu.HBM)
    sem = pl.BlockSpec(memory_space=pltpu.SEMAPHORE)
    srcs = [a for p in parts for a in p[2]]
    sems = [s for p in parts for s in (p[0], p[1])]
    res = pl.pallas_call(
        body, name=name, in_specs=[hbm] * (n_src + n_dst) + [sem] * len(sems) + [pl.BlockSpec(memory_space=pl.ANY)],
        out_specs=[hbm] * (n_src + n_dst),
        out_shape=[pltpu.HBM(a.shape, a.dtype) for a in (*srcs, *dsts)],
        input_output_aliases={i: i for i in range(n_src + n_dst)},
        compiler_params=pltpu.CompilerParams(has_side_effects=pltpu.SideEffectType.DATAFLOW_SIDE_EFFECTING))(
            *srcs, *dsts, *sems, after)
    return list(res[:n_src]), list(res[n_src:])


def _whole(ref, p):
    return ref


def _layer_of(layer, inner):
    return lambda ref, p: inner(ref.at[layer], p)


def _cols_of(width):
    def take(ref, p):
        lead = (slice(None),) * (len(ref.shape) - 1)
        return ref.at[lead + (pl.ds(pl.multiple_of(p * width, LANES), width),)]
    return take


def _rows_of(height):
    def take(ref, p):
        lead = (slice(None),) * (len(ref.shape) - 2)
        return ref.at[lead + (pl.ds(pl.multiple_of(p * height, SUBLANES), height), slice(None))]
    return take


def _slot(layer=None):
    if layer is None:
        return lambda ref, p: ref.at[p]
    return lambda ref, p: ref.at[p, layer]


def _local_step(x0, target, mod, g_mix_pre, g_mix_post, g_ffn_pre, g_ffn_post, wqkv, wo, wg, wu, wd,
                wfg, bfg, wconv, bconv, tiles, fetch=None, on_grads=None, zero=0.0):
    s, d = x0.shape
    nl = mod.shape[0]
    n_heads = d // HEAD_DIM
    hp_n = d // LANES
    vec = lambda a: a.reshape(1, -1)
    saved = []
    xcur = x0
    for l in range(nl):
        if fetch is not None:
            wqkv = fetch(l, 0, xcur)[0]
        sh_a, sc_a, gt_a, sh_f, sc_f, gt_f = (vec(mod[l, i * d:(i + 1) * d]) for i in range(6))
        fox = l % 2 == 1
        g1 = vec(g_mix_pre[l])
        if l == 0 and fetch is not None:
            g1 = g1 + zero
        h1 = _norm_mod_fwd(xcur, g1, sh_a, sc_a, f"norm_mix_fwd_{l}")
        qkv = _mm_nn(h1, wqkv, l, BF16, f"qkv_fwd_{l}", split_out=d)
        if fox:
            j = l // 2
            fl = _mm_nn(h1, wfg, j, F32, f"fgate_fwd_{l}")
            cum = _fox_prep(fl, bfg[j], f"fox_prep_{l}")
            cum_h = cum[:, :n_heads]
            cq = jnp.repeat(cum_h.reshape(s, hp_n, 2).transpose(1, 0, 2), HEAD_DIM, axis=2)
            ck = cum_h.T.reshape(hp_n, 2, s)
            o, stat = _fox_fwd(qkv, cq, ck, *tiles["fox_fwd"], f"fox_fwd_{l}")
            extra = (fl, cq, ck)
        else:
            o, stat = _sb_fwd(qkv, *tiles["sb_fwd"], f"sb_fwd_{l}")
            extra = None
        if fetch is not None:
            wqkv, wo, wg, wu, wd = fetch(l, 1, o)
        u = _mm_nn(o, wo, l, F32, f"attn_out_fwd_{l}")
        x2 = _post_fwd(xcur, u, vec(g_mix_post[l]), gt_a, f"post_mix_fwd_{l}")
        h2 = _norm_mod_fwd(x2, vec(g_ffn_pre[l]), sh_f, sc_f, f"norm_ffn_fwd_{l}")
        gp = _mm_nn(h2, wg, l, F32, f"ffn_gate_fwd_{l}")
        up = _mm_nn(h2, wu, l, F32, f"ffn_up_fwd_{l}")
        act = _conv_act_fwd(gp, up, wconv[l], bconv[l], f"conv_act_fwd_{l}")
        yv = _mm_nn(act, wd, l, F32, f"ffn_down_fwd_{l}")
        x3 = _post_fwd(x2, yv, vec(g_ffn_post[l]), gt_f, f"post_ffn_fwd_{l}")
        saved.append((xcur, h1, qkv, o, stat, extra, u, x2, h2, gp, up, act, yv))
        xcur = x3

    dx, sq = _loss_head(xcur, target, "loss_head")
    small, big = [None] * nl, [None] * nl
    nudge = None
    for l in reversed(range(nl)):
        xin, h1, qkv, o, stat, extra, u, x2, h2, gp, up, act, yv = saved[l]
        sc_a, gt_a, sc_f, gt_f = (vec(mod[l, i * d:(i + 1) * d]) for i in (1, 2, 4, 5))
        fox = l % 2 == 1
        g4 = vec(g_ffn_post[l])
        if nudge is not None:
            g4 = g4 + nudge
        dy, dgt_f, dg4 = _post_bwd(dx, yv, g4, gt_f, f"post_ffn_bwd_{l}")
        dact = _mm_nt(dy, wd, l, F32, f"ffn_down_dx_{l}")
        dwd = _mm_tn(act, dy, BF16, f"ffn_down_dw_{l}")
        dgp, dup, dwc, dbc = _conv_act_bwd(dact, gp, up, wconv[l], bconv[l], f"conv_act_bwd_{l}")
        dh2 = _mm_nt(dgp, wg, l, F32, f"ffn_gate_dx_{l}")
        dh2 = _mm_nt(dup, wu, l, F32, f"ffn_up_dx_{l}", add=dh2)
        dwg = _mm_tn(h2, dgp, BF16, f"ffn_gate_dw_{l}")
        dwu = _mm_tn(h2, dup, BF16, f"ffn_up_dw_{l}")
        g3 = vec(g_ffn_pre[l])
        if on_grads is not None:
            g3 = g3 + on_grads(l, dict(gate=dwg, up=dwu, down=dwd))
        dx2, dsh_f, dsc_f, dg3 = _norm_mod_bwd(dh2, x2, g3, sc_f, dx, f"norm_ffn_bwd_{l}")
        du, dgt_a, dg2 = _post_bwd(dx2, u, vec(g_mix_post[l]), gt_a, f"post_mix_bwd_{l}")
        do = _mm_nt(du, wo, l, BF16, f"attn_out_dx_{l}")
        dwo = _mm_tn(o, du, BF16, f"attn_out_dw_{l}")
        if fox:
            j = l // 2
            fl, cq, ck = extra
            dqkv, dcq, dck = _fox_bwd(qkv, o, do, stat, cq, ck, *tiles["fox_bwd"], f"fox_bwd_{l}")
            dcq = jnp.max(dcq.reshape(hp_n, s, 2, HEAD_DIM), axis=3)
            dcum = dcq.transpose(1, 0, 2).reshape(s, n_heads) + dck.reshape(n_heads, s).T
            dcum = jnp.pad(dcum, ((0, 0), (0, LANES - n_heads)))
            dfl, dbfg = _fox_gate_bwd(dcum, fl, bfg[j], n_heads, f"fox_gate_bwd_{l}")
            dh1 = _mm_nt(dfl, wfg, j, F32, f"fgate_dx_{l}")
            dh1 = _mm_nt(dqkv, wqkv, l, F32, f"qkv_dx_{l}", add=dh1)
            dwfg = _mm_tn(h1, dfl, F32, f"fgate_dw_{l}")[:, :n_heads]
            dbfg = dbfg[0, :n_heads]
        else:
            dqkv = _sb_bwd(qkv, stat, do, *tiles["sb_bwd"], f"sb_bwd_{l}")
            dh1 = _mm_nt(dqkv, wqkv, l, F32, f"qkv_dx_{l}")
            dwfg = dbfg = None
        dwqkv = _mm_tn(h1, dqkv, BF16, f"qkv_dw_{l}")
        dx, dsh_a, dsc_a, dg1 = _norm_mod_bwd(dh1, xin, vec(g_mix_pre[l]), sc_a, dx2, f"norm_mix_bwd_{l}")
        dmod = jnp.concatenate([dsh_a, dsc_a, dgt_a, dsh_f, dsc_f, dgt_f], axis=1)[0]
        small[l] = dict(dmod=dmod, dg1=dg1[0], dg2=dg2[0], dg3=dg3[0], dg4=dg4[0], dbc=dbc[0], dwc=dwc,
                        dbfg=dbfg, dwfg=dwfg)
        big[l] = dict(qkv=dwqkv, o=dwo, gate=dwg, up=dwu, down=dwd)
        if on_grads is not None:
            nudge = on_grads(l, dict(qkv=dwqkv, o=dwo))
    return sq, dx, small, big


def _rows128(a, rows):
    flat = a.reshape(-1)
    return jnp.pad(flat, (0, rows * LANES - flat.shape[0])).reshape(rows, LANES)


def _ceil8(n_elems):
    rows = -(-n_elems // LANES)
    return -(-rows // SUBLANES) * SUBLANES


def kernel(x, c, w_mod, b_mod, g_mix_pre, g_mix_post, w_qkv, w_o, w_fg, b_fg, g_ffn_pre, g_ffn_post, w_ffn_gate, w_ffn_up, w_conv, b_conv, w_ffn_down, loss_target, m_w_mod, m_b_mod, m_g_mix_pre, m_g_mix_post, m_w_qkv, m_w_o, m_w_fg, m_b_fg, m_g_ffn_pre, m_g_ffn_post, m_w_ffn_gate, m_w_ffn_up, m_w_conv, m_b_conv, m_w_ffn_down, v_w_mod, v_b_mod, v_g_mix_pre, v_g_mix_post, v_w_qkv, v_w_o, v_w_fg, v_b_fg, v_g_ffn_pre, v_g_ffn_post, v_w_ffn_gate, v_w_ffn_up, v_w_conv, v_b_conv, v_w_ffn_down):
    _, s, d = x.shape
    nl = w_qkv.shape[0]
    nf = w_fg.shape[0]
    n_heads = w_fg.shape[2]
    fs = w_ffn_gate.shape[2]
    fp = -(-fs // LANES) * LANES
    f_full, f_pad = N_DEV * fs, N_DEV * fp
    mod_cols = w_mod.shape[2]
    qs, orows = w_qkv.shape[2], w_o.shape[1]
    xi, yi, ci = _mesh_pos()
    me = 4 * xi + 2 * yi + ci

    c_rows = d // LANES
    c_all = _allgather_small(jnp.pad(c.reshape(1, d), ((0, SUBLANES - 1), (0, 0))).reshape(SUBLANES * c_rows, LANES),
                             "gather_cond")
    c_all = c_all.reshape(N_DEV, SUBLANES, d)[:, 0, :]
    b_mod_cols = lax.dynamic_slice_in_dim(b_mod, me * mod_cols, mod_cols, axis=1).reshape(nl, 1, mod_cols)
    mod_part = _mod_fwd(c_all, w_mod, b_mod_cols, "mod_fwd")

    conv_pad = jnp.pad(w_conv, ((0, 0), (0, 0), (0, fp - fs)))
    r_mod, r_conv, r_fg = _ceil8(mod_part.size), _ceil8(conv_pad.size), _ceil8(w_fg.size)
    payload = jnp.concatenate([_rows128(mod_part, r_mod), _rows128(conv_pad, r_conv), _rows128(w_fg, r_fg)], axis=0)
    got = _allgather_small(payload, "gather_small_weights").reshape(N_DEV, r_mod + r_conv + r_fg, LANES)
    mod_g = got[:, :r_mod].reshape(N_DEV, -1)[:, :mod_part.size].reshape(N_DEV, nl, N_DEV, mod_cols)
    mod = lax.dynamic_index_in_dim(mod_g, me, axis=2, keepdims=False).transpose(1, 0, 2).reshape(nl, N_DEV * mod_cols)
    conv_g = got[:, r_mod:r_mod + r_conv].reshape(N_DEV, -1)[:, :conv_pad.size].reshape(N_DEV, nl, 3, fp)
    wconv_full = conv_g.transpose(1, 2, 0, 3).reshape(nl, 3, f_pad)
    fg_g = got[:, r_mod + r_conv:].reshape(N_DEV, -1)[:, :w_fg.size].reshape(N_DEV, nf, orows, n_heads)
    wfg_full = fg_g.transpose(1, 0, 2, 3).reshape(nf, d, n_heads)
    wfg_full = jnp.pad(wfg_full, ((0, 0), (0, 0), (0, LANES - n_heads))).astype(BF16)
    bfg_full = jnp.pad(b_fg, ((0, 0), (0, LANES - n_heads))).reshape(nf, 1, LANES)
    bconv_full = jnp.pad(b_conv.reshape(nl, N_DEV, fs), ((0, 0), (0, 0), (0, fp - fs))).reshape(nl, 1, f_pad)

    gate_sh = jnp.pad(w_ffn_gate, ((0, 0), (0, 0), (0, fp - fs))).astype(BF16)
    up_sh = jnp.pad(w_ffn_up, ((0, 0), (0, 0), (0, fp - fs))).astype(BF16)
    down_sh = jnp.pad(w_ffn_down, ((0, 0), (0, fp - fs), (0, 0))).astype(BF16)
    shards = [w_qkv.astype(BF16), w_o.astype(BF16), gate_sh, up_sh, down_sh]
    full_shapes = [jax.ShapeDtypeStruct((nl, d, N_DEV * qs), BF16), jax.ShapeDtypeStruct((nl, d, d), BF16),
                   jax.ShapeDtypeStruct((nl, d, f_pad), BF16), jax.ShapeDtypeStruct((nl, d, f_pad), BF16),
                   jax.ShapeDtypeStruct((nl, f_pad, d), BF16)]
    place = [_cols_of(qs), _rows_of(orows), _cols_of(fp), _cols_of(fp), _rows_of(fp)]
    full = [lax.empty(sh.shape, sh.dtype) for sh in full_shapes]
    gathers, zero = [], 0.0
    for l in range(nl):
        jobs_l = [(i, i, _layer_of(l, _whole), _layer_of(l, place[i])) for i in range(5)]
        send_s, recv_s, shards, full, token = _exchange_start(shards, full, jobs_l, f"gather_start_{l}",
                                                              after=got if l == 0 else None)
        gathers.append((send_s, recv_s, jobs_l))
        zero = zero + token[0, 0]
    state = {"shards": shards, "full": full}

    def fetch(l, part, after):
        send_s, recv_s, jobs_l = gathers[l]
        only = (0,) if part == 0 else (1, 2, 3, 4)
        state["shards"], state["full"] = _exchange_wait([(send_s, recv_s, state["shards"], jobs_l, only)],
                                                        state["full"], after, f"gather_wait_{l}_{part}")
        return state["full"]

    order = ["qkv", "o", "gate", "up", "down"]
    send = {"qkv": _cols_of(qs), "o": _rows_of(orows), "gate": _cols_of(fp), "up": _cols_of(fp), "down": _rows_of(fp)}
    recv_shapes = [(N_DEV, nl, d, qs), (N_DEV, nl, orows, d), (N_DEV, nl, d, fp), (N_DEV, nl, d, fp), (N_DEV, nl, fp, d)]
    scatters = []
    landing = {"recv": [lax.empty(sh, BF16) for sh in recv_shapes]}

    def on_grads(l, grads):
        names = [nm for nm in order if nm in grads]
        jobs_l = [(k, order.index(nm), send[nm], _slot(l)) for k, nm in enumerate(names)]
        send_s, recv_s, srcs, landing["recv"], token = _exchange_start(
            [grads[nm] for nm in names], landing["recv"], jobs_l, f"scatter_start_{l}_{names[0]}")
        scatters.append((send_s, recv_s, srcs, jobs_l, None))
        return token[0, 0]

    sq, dx, small, big = _local_step(x[0], loss_target[0], mod, g_mix_pre, g_mix_post, g_ffn_pre, g_ffn_post,
                                     None, None, None, None, None, wfg_full, bfg_full, wconv_full, bconv_full,
                                     ATTN_TILES, fetch=fetch, on_grads=on_grads, zero=zero)
    loss = lax.psum(0.5 * jnp.sum(sq) / d, ("x", "y", "c"))
    _, recv = _exchange_wait(scatters, landing["recv"], dx, "scatter_wait")
    upd = {}
    for nm, rv, wt, mt, vt in zip(order, recv, [w_qkv, w_o, w_ffn_gate, w_ffn_up, w_ffn_down],
                                  [m_w_qkv, m_w_o, m_w_ffn_gate, m_w_ffn_up, m_w_ffn_down],
                                  [v_w_qkv, v_w_o, v_w_ffn_gate, v_w_ffn_up, v_w_ffn_down]):
        upd[nm] = _adamw(rv, wt, mt, vt, f"adamw_{nm}")

    stack = lambda key: jnp.stack([small[l][key] for l in range(nl)])
    dmod = stack("dmod")
    dgs = [stack(k) for k in ("dg1", "dg2", "dg3", "dg4")]
    dbc = stack("dbc").reshape(nl, N_DEV, fp)[:, :, :fs].reshape(nl, f_full)
    dbfg = jnp.stack([small[l]["dbfg"] for l in range(nl) if l % 2 == 1])
    dwc = stack("dwc").reshape(nl, 3, N_DEV, fp)[:, :, :, :fs].reshape(nl, 3, f_full)
    dwfg = jnp.stack([small[l]["dwfg"] for l in range(nl) if l % 2 == 1])
    rep_parts = [dmod] + dgs + [dbc, dbfg]
    rep_rows = [_ceil8(p.size) for p in rep_parts]
    r_rep, r_wc, r_wfg = sum(rep_rows), _ceil8(dwc.size), _ceil8(dwfg.size)
    payload = jnp.concatenate([_rows128(p, r) for p, r in zip(rep_parts, rep_rows)]
                              + [_rows128(dwc, r_wc), _rows128(dwfg, r_wfg)], axis=0)
    gsm = _allgather_small(payload, "gather_small_grads").reshape(N_DEV, 1, r_rep + r_wc + r_wfg, LANES)

    def pack(parts):
        return jnp.concatenate([_rows128(p, r) for p, r in zip(parts, rep_rows)], axis=0).reshape(1, r_rep, LANES)

    rep_w = [b_mod, g_mix_pre, g_mix_post, g_ffn_pre, g_ffn_post, b_conv, b_fg]
    rep_m = [m_b_mod, m_g_mix_pre, m_g_mix_post, m_g_ffn_pre, m_g_ffn_post, m_b_conv, m_b_fg]
    rep_v = [v_b_mod, v_g_mix_pre, v_g_mix_post, v_g_ffn_pre, v_g_ffn_post, v_b_conv, v_b_fg]
    rep_out = _adamw(gsm[:, :, :r_rep], pack(rep_w), pack(rep_m), pack(rep_v), "adamw_replicated", tr=r_rep, tc=LANES)

    def unpack(packed):
        outs, at = [], 0
        for p, r in zip(rep_w, rep_rows):
            outs.append(packed[0, at:at + r].reshape(-1)[:p.size].reshape(p.shape))
            at += r
        return outs

    rep_g, rep_d, rep_nm, rep_nv = (unpack(a) for a in rep_out)

    wc_all = gsm[:, 0, r_rep:r_rep + r_wc].reshape(N_DEV, -1)[:, :dwc.size].reshape(N_DEV, 1, nl * 3, f_full)
    wc_mine = lax.dynamic_slice_in_dim(wc_all, me * fs, fs, axis=3)
    wc_out = _adamw(wc_mine, w_conv.reshape(1, nl * 3, fs), m_w_conv.reshape(1, nl * 3, fs),
                    v_w_conv.reshape(1, nl * 3, fs), "adamw_conv", tr=nl * 3, tc=fs)
    wc_out = [a.reshape(nl, 3, fs) for a in wc_out]
    wfg_all = gsm[:, 0, r_rep + r_wc:].reshape(N_DEV, -1)[:, :dwfg.size].reshape(N_DEV, nf, d, n_heads)
    wfg_mine = lax.dynamic_slice_in_dim(wfg_all, me * orows, orows, axis=2)
    wfg_out = _adamw(wfg_mine, w_fg, m_w_fg, v_w_fg, "adamw_fgate", tr=orows, tc=n_heads)

    dmod_all = gsm[:, 0, :rep_rows[0]].reshape(N_DEV, -1)[:, :dmod.size].reshape(N_DEV, nl, N_DEV * mod_cols)
    dmod_mine = lax.dynamic_slice_in_dim(dmod_all, me * mod_cols, mod_cols, axis=2)
    gwmod = _wmod_grad(c_all.T, dmod_mine, "wmod_grad")
    wmod_out = _adamw(gwmod.reshape(1, nl, d, mod_cols), w_mod, m_w_mod, v_w_mod, "adamw_mod")

    per_weight = [wmod_out, None, None, None, upd["qkv"], upd["o"], wfg_out, None, None, None,
                  upd["gate"], upd["up"], wc_out, None, upd["down"]]
    rep_index = {1: 0, 2: 1, 3: 2, 8: 3, 9: 4, 13: 5, 7: 6}
    outs = [[], [], [], []]
    for pos, res in enumerate(per_weight):
        for kind in range(4):
            if res is None:
                outs[kind].append((rep_g, rep_d, rep_nm, rep_nv)[kind][rep_index[pos]])
            else:
                outs[kind].append(res[kind])
    return (loss, dx.reshape(1, s, d), *outs[0], *outs[1], *outs[2], *outs[3])
```

```python
import functools

import jax
import jax.numpy as jnp
from jax import lax
from jax.experimental import pallas as pl
from jax.experimental.pallas import tpu as pltpu

F32 = jnp.float32
BF16 = jnp.bfloat16

N_DEV = 8
HEAD_DIM = 64
LANES = 128
SUBLANES = 8
RMS_EPS = 1e-6
ADAM_LR = 0.001
ADAM_B1 = 0.9
ADAM_B2 = 0.999
ADAM_EPS = 1e-08
ADAM_WD = 0.01
ADAM_STEP = 10
VMEM_LIMIT_BYTES = 56 * 1024 * 1024
SUM_TERMS = 1
HALO = 16
MM_ROWS = 2048
ATTN_TILES = {"sb_fwd": (512, 256), "sb_bwd": (512, 256), "fox_fwd": (512, 256), "fox_bwd": (256, 256)}

NN = (((1,), (0,)), ((), ()))
NT = (((1,), (1,)), ((), ()))
TN = (((0,), (0,)), ((), ()))
MESH = pl.DeviceIdType.MESH


def _params(*sem):
    return pltpu.CompilerParams(dimension_semantics=sem, vmem_limit_bytes=VMEM_LIMIT_BYTES)


def _pick(n, pref, quantum):
    if n <= pref:
        return n
    t = (pref // quantum) * quantum
    while n % t:
        t -= quantum
    return t


def _mm(a, b, *, dims, grid, a_spec, b_spec, o_spec, out_shape, name, add=None):
    nk = grid[2]
    acc_shape = tuple(d for d in o_spec.block_shape if d is not None)
    o_dtype = out_shape.dtype

    def body(*refs):
        if add is None:
            a_ref, b_ref, o_ref, acc_ref = refs
            add_ref = None
        else:
            a_ref, b_ref, add_ref, o_ref, acc_ref = refs
        k = pl.program_id(2)
        part = lax.dot_general(a_ref[...], b_ref[...], dims, preferred_element_type=F32)

        def finish(total):
            if add_ref is not None:
                total = total + add_ref[...]
            o_ref[...] = total.astype(o_dtype)

        if nk == 1:
            finish(part)
            return

        @pl.when(k == 0)
        def _():
            acc_ref[...] = part

        @pl.when(jnp.logical_and(k > 0, k < nk - 1))
        def _():
            acc_ref[...] += part

        @pl.when(k == nk - 1)
        def _():
            finish(acc_ref[...] + part)

    operands = [a, b] if add is None else [a, b, add]
    in_specs = [a_spec, b_spec] if add is None else [a_spec, b_spec, o_spec]
    return pl.pallas_call(
        body, name=name, grid=grid, in_specs=in_specs, out_specs=o_spec, out_shape=out_shape,
        scratch_shapes=[pltpu.VMEM(acc_shape, F32)],
        compiler_params=_params("parallel", "parallel", "arbitrary"),
    )(*operands)


def _mm_rows(out_dtype, has_add):
    return MM_ROWS if (jnp.dtype(out_dtype).itemsize == 2 and not has_add) else MM_ROWS // 2


def _mm_nn(a, w, l, out_dtype, name, *, col0=0, n=None, split_out=None):
    m, kdim = a.shape
    n = w.shape[2] if n is None else n
    tm, tk = _pick(m, _mm_rows(out_dtype, False), SUBLANES), _pick(kdim, 1024, LANES)
    tn = _pick(n if split_out is None else split_out, 1024, LANES)
    jb = col0 // tn
    grid = (m // tm, n // tn, kdim // tk)
    a_spec = pl.BlockSpec((tm, tk), lambda i, j, k: (i, k))
    b_spec = pl.BlockSpec((None, tk, tn), lambda i, j, k: (l, k, j + jb))
    if split_out is None:
        o_spec = pl.BlockSpec((tm, tn), lambda i, j, k: (i, j))
        shape = jax.ShapeDtypeStruct((m, n), out_dtype)
    else:
        nj1 = split_out // tn
        o_spec = pl.BlockSpec((None, tm, tn), lambda i, j, k: (j // nj1, i, j % nj1))
        shape = jax.ShapeDtypeStruct((n // split_out, m, split_out), out_dtype)
    return _mm(a, w, dims=NN, grid=grid, a_spec=a_spec, b_spec=b_spec, o_spec=o_spec, out_shape=shape, name=name)


def _mm_nt(a, w, l, out_dtype, name, *, add=None):
    n, kdim = w.shape[1], w.shape[2]
    if a.ndim == 2:
        m = a.shape[0]
        tk = _pick(kdim, 1024, LANES)
        a_spec_of = lambda tm: pl.BlockSpec((tm, tk), lambda i, j, k: (i, k))
    else:
        m, seg = a.shape[1], a.shape[2]
        tk = _pick(seg, 1024, LANES)
        nk1 = seg // tk
        a_spec_of = lambda tm: pl.BlockSpec((None, tm, tk), lambda i, j, k: (k // nk1, i, k % nk1))
    tm, tn = _pick(m, _mm_rows(out_dtype, add is not None), SUBLANES), _pick(n, 1024, LANES)
    grid = (m // tm, n // tn, kdim // tk)
    b_spec = pl.BlockSpec((None, tn, tk), lambda i, j, k: (l, j, k))
    o_spec = pl.BlockSpec((tm, tn), lambda i, j, k: (i, j))
    return _mm(a, w, dims=NT, grid=grid, a_spec=a_spec_of(tm), b_spec=b_spec, o_spec=o_spec,
               out_shape=jax.ShapeDtypeStruct((m, n), out_dtype), name=name, add=add)


def _mm_tn(a, b, out_dtype, name):
    kdim, m = a.shape
    tk, tm = _pick(kdim, 1024, SUBLANES), _pick(m, _mm_rows(out_dtype, False), LANES)
    if b.ndim == 2:
        n = b.shape[1]
        tn = _pick(n, 1024, LANES)
        b_spec = pl.BlockSpec((tk, tn), lambda i, j, k: (k, j))
    else:
        seg = b.shape[2]
        n = b.shape[0] * seg
        tn = _pick(seg, 1024, LANES)
        nj1 = seg // tn
        b_spec = pl.BlockSpec((None, tk, tn), lambda i, j, k: (j // nj1, k, j % nj1))
    grid = (m // tm, n // tn, kdim // tk)
    a_spec = pl.BlockSpec((tk, tm), lambda i, j, k: (k, i))
    o_spec = pl.BlockSpec((tm, tn), lambda i, j, k: (i, j))
    return _mm(a, b, dims=TN, grid=grid, a_spec=a_spec, b_spec=b_spec, o_spec=o_spec,
               out_shape=jax.ShapeDtypeStruct((m, n), out_dtype), name=name)


def _rstd(x):
    return lax.rsqrt(jnp.mean(x * x, axis=-1, keepdims=True) + RMS_EPS)


def _norm_mod_fwd(x, g, shift, scale, name):
    s, d = x.shape
    tm = _pick(s, 512, SUBLANES)

    def body(x_ref, g_ref, sh_ref, sc_ref, h_ref):
        xv = x_ref[...]
        y = (xv * _rstd(xv)) * g_ref[...]
        h_ref[...] = (y * (1.0 + sc_ref[...]) + sh_ref[...]).astype(BF16)

    row = pl.BlockSpec((tm, d), lambda i: (i, 0))
    vec = pl.BlockSpec((1, d), lambda i: (0, 0))
    return pl.pallas_call(body, name=name, grid=(s // tm,), in_specs=[row, vec, vec, vec], out_specs=row,
                          out_shape=jax.ShapeDtypeStruct((s, d), BF16), compiler_params=_params("parallel"))(x, g, shift, scale)


def _norm_mod_bwd(dh, x, g, scale, dres, name):
    s, d = x.shape
    tm = _pick(s, 512, SUBLANES)

    def body(dh_ref, x_ref, g_ref, sc_ref, dres_ref, dx_ref, dsh_ref, dsc_ref, dg_ref):
        i = pl.program_id(0)
        xv, dhv, gv = x_ref[...], dh_ref[...], g_ref[...]
        r = _rstd(xv)
        xh = xv * r
        dn = dhv * (1.0 + sc_ref[...])
        gd = dn * gv
        dx_ref[...] = dres_ref[...] + r * (gd - xh * jnp.mean(gd * xh, axis=-1, keepdims=True))

        @pl.when(i == 0)
        def _():
            dsh_ref[...] = jnp.zeros_like(dsh_ref)
            dsc_ref[...] = jnp.zeros_like(dsc_ref)
            dg_ref[...] = jnp.zeros_like(dg_ref)

        dsh_ref[...] += jnp.sum(dhv, axis=0, keepdims=True)
        dsc_ref[...] += jnp.sum(dhv * (xh * gv), axis=0, keepdims=True)
        dg_ref[...] += jnp.sum(dn * xh, axis=0, keepdims=True)

    row = pl.BlockSpec((tm, d), lambda i: (i, 0))
    vec = pl.BlockSpec((1, d), lambda i: (0, 0))
    vshape = jax.ShapeDtypeStruct((1, d), F32)
    return pl.pallas_call(body, name=name, grid=(s // tm,), in_specs=[row, row, vec, vec, row],
                          out_specs=[row, vec, vec, vec],
                          out_shape=[jax.ShapeDtypeStruct((s, d), F32), vshape, vshape, vshape],
                          compiler_params=_params("arbitrary"))(dh, x, g, scale, dres)


def _post_fwd(x, u, g, gate, name):
    s, d = x.shape
    tm = _pick(s, 512, SUBLANES)

    def body(x_ref, u_ref, g_ref, gt_ref, o_ref):
        uv = u_ref[...]
        o_ref[...] = x_ref[...] + gt_ref[...] * ((uv * _rstd(uv)) * g_ref[...])

    row = pl.BlockSpec((tm, d), lambda i: (i, 0))
    vec = pl.BlockSpec((1, d), lambda i: (0, 0))
    return pl.pallas_call(body, name=name, grid=(s // tm,), in_specs=[row, row, vec, vec], out_specs=row,
                          out_shape=jax.ShapeDtypeStruct((s, d), F32), compiler_params=_params("parallel"))(x, u, g, gate)


def _post_bwd(dx, u, g, gate, name):
    s, d = u.shape
    tm = _pick(s, 512, SUBLANES)

    def body(dx_ref, u_ref, g_ref, gt_ref, du_ref, dgt_ref, dg_ref):
        i = pl.program_id(0)
        uv, dxv, gv = u_ref[...], dx_ref[...], g_ref[...]
        r = _rstd(uv)
        uh = uv * r
        dn = dxv * gt_ref[...]
        gd = dn * gv
        du_ref[...] = (r * (gd - uh * jnp.mean(gd * uh, axis=-1, keepdims=True))).astype(BF16)

        @pl.when(i == 0)
        def _():
            dgt_ref[...] = jnp.zeros_like(dgt_ref)
            dg_ref[...] = jnp.zeros_like(dg_ref)

        dgt_ref[...] += jnp.sum(dxv * (uh * gv), axis=0, keepdims=True)
        dg_ref[...] += jnp.sum(dn * uh, axis=0, keepdims=True)

    row = pl.BlockSpec((tm, d), lambda i: (i, 0))
    vec = pl.BlockSpec((1, d), lambda i: (0, 0))
    vshape = jax.ShapeDtypeStruct((1, d), F32)
    return pl.pallas_call(body, name=name, grid=(s // tm,), in_specs=[row, row, vec, vec],
                          out_specs=[row, vec, vec],
                          out_shape=[jax.ShapeDtypeStruct((s, d), BF16), vshape, vshape],
                          compiler_params=_params("arbitrary"))(dx, u, g, gate)


def _shift_rows(cur, prev8, k):
    rolled = pltpu.roll(cur, k, axis=0)
    rolled_prev = pltpu.roll(prev8, k, axis=0)
    i8 = lax.broadcasted_iota(jnp.int32, prev8.shape, 0)
    top = jnp.where(i8 < k, rolled_prev, rolled[:SUBLANES])
    return jnp.concatenate([top, rolled[SUBLANES:]], axis=0)


def _conv_pre(g, prev8, wc_ref, bc_ref):
    s1 = _shift_rows(g, prev8, 1)
    s2 = _shift_rows(g, prev8, 2)
    gc = bc_ref[...] + wc_ref[0:1, :] * s2 + wc_ref[1:2, :] * s1 + wc_ref[2:3, :] * g
    return gc, s1, s2


def _conv_act_fwd(gp, up, wc, bc, name):
    s, f = gp.shape
    tm, tf = _pick(s, 512, SUBLANES), _pick(f, 768, LANES)
    rh = tm // HALO

    def body(g_ref, gprev_ref, up_ref, wc_ref, bc_ref, a_ref):
        i = pl.program_id(1)
        prev = jnp.where(i == 0, 0.0, gprev_ref[...].astype(F32)[HALO - SUBLANES:])
        gc, _, _ = _conv_pre(g_ref[...].astype(F32), prev, wc_ref, bc_ref)
        a_ref[...] = ((gc * jax.nn.sigmoid(gc)) * up_ref[...].astype(F32)).astype(BF16)

    tile = pl.BlockSpec((tm, tf), lambda j, i: (i, j))
    prev = pl.BlockSpec((HALO, tf), lambda j, i: (jnp.maximum(i * rh - 1, 0), j))
    return pl.pallas_call(body, name=name, grid=(f // tf, s // tm),
                          in_specs=[tile, prev, tile, pl.BlockSpec((3, tf), lambda j, i: (0, j)),
                                    pl.BlockSpec((1, tf), lambda j, i: (0, j))],
                          out_specs=tile, out_shape=jax.ShapeDtypeStruct((s, f), BF16),
                          compiler_params=_params("parallel", "parallel"))(gp, gp, up, wc, bc)


def _conv_act_bwd(da, gp, up, wc, bc, name):
    s, f = gp.shape
    tm, tf = _pick(s, 512, SUBLANES), _pick(f, 768, LANES)
    rh = tm // HALO
    nrow = s // tm

    def body(da_ref, dan_ref, g_ref, gprev_ref, gn_ref, up_ref, upn_ref, wc_ref, bc_ref,
             dgp_ref, dup_ref, dwc_ref, dbc_ref):
        i = pl.program_id(1)
        last = i == nrow - 1
        head = lambda ref: ref[...].astype(F32)[:SUBLANES]
        prev = jnp.where(i == 0, 0.0, gprev_ref[...].astype(F32)[HALO - SUBLANES:])
        g_ext = jnp.concatenate([g_ref[...].astype(F32), head(gn_ref)], axis=0)
        up_ext = jnp.concatenate([up_ref[...].astype(F32), head(upn_ref)], axis=0)
        da_ext = jnp.concatenate([da_ref[...].astype(F32), jnp.where(last, 0.0, head(dan_ref))], axis=0)
        gc, s1, s2 = _conv_pre(g_ext, prev, wc_ref, bc_ref)
        sg = jax.nn.sigmoid(gc)
        dup_ref[...] = (da_ext * (gc * sg))[:tm].astype(BF16)
        dgc = da_ext * up_ext * (sg * (1.0 + gc * (1.0 - sg)))
        ext = tm + SUBLANES
        dgp = (wc_ref[2:3, :] * dgc + wc_ref[1:2, :] * pltpu.roll(dgc, ext - 1, axis=0)
               + wc_ref[0:1, :] * pltpu.roll(dgc, ext - 2, axis=0))
        dgp_ref[...] = dgp[:tm].astype(BF16)

        @pl.when(i == 0)
        def _():
            dwc_ref[...] = jnp.zeros_like(dwc_ref)
            dbc_ref[...] = jnp.zeros_like(dbc_ref)

        d0 = dgc[:tm]
        dwc_ref[0:1, :] += jnp.sum(d0 * s2[:tm], axis=0, keepdims=True)
        dwc_ref[1:2, :] += jnp.sum(d0 * s1[:tm], axis=0, keepdims=True)
        dwc_ref[2:3, :] += jnp.sum(d0 * g_ext[:tm], axis=0, keepdims=True)
        dbc_ref[...] += jnp.sum(d0, axis=0, keepdims=True)

    tile = pl.BlockSpec((tm, tf), lambda j, i: (i, j))
    prev = pl.BlockSpec((HALO, tf), lambda j, i: (jnp.maximum(i * rh - 1, 0), j))
    nxt = pl.BlockSpec((HALO, tf), lambda j, i: (jnp.minimum((i + 1) * rh, s // HALO - 1), j))
    return pl.pallas_call(body, name=name, grid=(f // tf, nrow),
                          in_specs=[tile, nxt, tile, prev, nxt, tile, nxt,
                                    pl.BlockSpec((3, tf), lambda j, i: (0, j)), pl.BlockSpec((1, tf), lambda j, i: (0, j))],
                          out_specs=[tile, tile, pl.BlockSpec((3, tf), lambda j, i: (0, j)),
                                     pl.BlockSpec((1, tf), lambda j, i: (0, j))],
                          out_shape=[jax.ShapeDtypeStruct((s, f), BF16), jax.ShapeDtypeStruct((s, f), BF16),
                                     jax.ShapeDtypeStruct((3, f), F32), jax.ShapeDtypeStruct((1, f), F32)],
                          compiler_params=_params("parallel", "arbitrary"))(da, da, gp, gp, gp, up, up, wc, bc)


def _split_bf16(v, parts):
    out, rem = [], v
    for _ in range(parts):
        t = rem.astype(BF16)
        out.append(t)
        rem = rem - t.astype(F32)
    return jnp.concatenate(out, axis=1)


def _tri(t, cmp, reps):
    row = lax.broadcasted_iota(jnp.int32, (t, t), 0)
    col = lax.broadcasted_iota(jnp.int32, (t, t), 1)
    m = cmp(row, col).astype(BF16)
    return jnp.concatenate([m] * reps, axis=0)


def _log_sigmoid(z):
    mn = jnp.minimum(z, 0.0)
    return mn - jnp.log(1.0 + jnp.exp(mn + (mn - z)))


def _log_one_minus_sigmoid(z):
    mn = jnp.minimum(z, 0.0)
    neg = mn - z
    return neg - jnp.log(1.0 + jnp.exp(mn + neg))


def _pair(xv):
    lane = lax.broadcasted_iota(jnp.int32, xv.shape, 1)
    zero = jnp.zeros_like(xv)
    return jnp.where(lane < HEAD_DIM, xv, zero), jnp.where(lane < HEAD_DIM, zero, xv)


def _attn_specs(s, tq):
    q_spec = pl.BlockSpec((None, tq, LANES), lambda hp, qi: (0, qi, hp))
    k_spec = pl.BlockSpec((None, s, LANES), lambda hp, qi: (1, 0, hp))
    v_spec = pl.BlockSpec((None, s, LANES), lambda hp, qi: (2, 0, hp))
    return q_spec, k_spec, v_spec


def _rel_mask(tq, tk, i, strict):
    row = lax.broadcasted_iota(jnp.int32, (tq, tk), 0)
    col = lax.broadcasted_iota(jnp.int32, (tq, tk), 1) + i * tk
    return col < row if strict else col <= row


def _sb_fwd(qkv, tq, tk, name):
    _, s, d = qkv.shape
    hp_n, nq, nd = d // LANES, s // tq, tq // tk
    scale = HEAD_DIM ** -0.5

    def body(q_ref, k_ref, v_ref, o_ref, lt_ref):
        qi = pl.program_id(1)
        lane = lax.broadcasted_iota(jnp.int32, (tq, LANES), 1)
        onward = _tri(tk, lambda j, sidx: j >= sidx, SUM_TERMS)
        qpair = _pair(q_ref[...] * scale)

        def block(kb, carry, mask):
            acc, runs = carry[0], carry[1:]
            ks = pl.multiple_of(kb * tk, tk)
            kblk = k_ref[pl.ds(ks, tk), :]
            vpair = _pair(v_ref[pl.ds(ks, tk), :])
            probs, new_runs = [], []
            for qh, run in zip(qpair, runs):
                z = lax.dot_general(qh, kblk, NT, preferred_element_type=F32)
                l1 = _log_one_minus_sigmoid(z)
                if mask is not None:
                    l1 = jnp.where(mask, l1, 0.0)
                c = jnp.dot(_split_bf16(l1, SUM_TERMS), onward, preferred_element_type=F32)
                a = jnp.exp(z + c + run)
                if mask is not None:
                    a = jnp.where(mask, a, 0.0)
                probs.append(a.astype(BF16))
                new_runs.append(run + jnp.sum(l1, axis=1, keepdims=True))
            acc = acc + jnp.dot(jnp.concatenate(probs, axis=1), jnp.concatenate(vpair, axis=0),
                                preferred_element_type=F32)
            return (acc, *new_runs)

        zero1 = jnp.zeros((tq, 1), F32)
        carry = (jnp.zeros((tq, LANES), F32), zero1, zero1)
        for i in reversed(range(nd)):
            carry = block(qi * nd + i, carry, _rel_mask(tq, tk, i, True))
        acc, run0, run1 = lax.fori_loop(0, qi * nd, lambda j, cr: block(qi * nd - 1 - j, cr, None), carry)
        o_ref[...] = acc.astype(BF16)
        lt_ref[...] = jnp.where(lane < HEAD_DIM, run0, run1)

    q_spec, k_spec, v_spec = _attn_specs(s, tq)
    return pl.pallas_call(
        body, name=name, grid=(hp_n, nq), in_specs=[q_spec, k_spec, v_spec],
        out_specs=[pl.BlockSpec((tq, LANES), lambda hp, qi: (qi, hp)),
                   pl.BlockSpec((None, tq, LANES), lambda hp, qi: (hp, qi, 0))],
        out_shape=[jax.ShapeDtypeStruct((s, d), BF16), jax.ShapeDtypeStruct((hp_n, s, LANES), F32)],
        compiler_params=_params("parallel", "parallel"))(qkv, qkv, qkv)


def _sb_bwd(qkv, ltot, do, tq, tk, name):
    _, s, d = qkv.shape
    hp_n, nq, nd = d // LANES, s // tq, tq // tk
    scale = HEAD_DIM ** -0.5

    def body(q_ref, k_ref, v_ref, lt_ref, do_ref, dqkv_ref, dk_acc, dv_acc):
        qi = pl.program_id(1)

        @pl.when(qi == 0)
        def _():
            dk_acc[...] = jnp.zeros_like(dk_acc)
            dv_acc[...] = jnp.zeros_like(dv_acc)

        onward = _tri(tk, lambda j, sidx: j >= sidx, SUM_TERMS)
        before = _tri(tk, lambda j, sidx: j < sidx, SUM_TERMS)
        qpair = _pair(q_ref[...] * scale)
        dopair = _pair(do_ref[...])
        ltv = lt_ref[...]
        lts = (ltv[:, 0:1], ltv[:, HEAD_DIM:HEAD_DIM + 1])

        def block(kb, carry, mask):
            dq, lpres, gpres = carry[0], carry[1:3], carry[3:5]
            ks = pl.multiple_of(kb * tk, tk)
            kblk, vblk = k_ref[pl.ds(ks, tk), :], v_ref[pl.ds(ks, tk), :]
            kpair = _pair(kblk * scale)
            dzs, probs, new_l, new_g = [], [], [], []
            for qh, doh, lt, lpre, gpre in zip(qpair, dopair, lts, lpres, gpres):
                z = lax.dot_general(qh, kblk, NT, preferred_element_type=F32)
                l1 = _log_one_minus_sigmoid(z)
                if mask is not None:
                    l1 = jnp.where(mask, l1, 0.0)
                lpre = lpre + jnp.sum(l1, axis=1, keepdims=True)
                c = jnp.dot(_split_bf16(l1, SUM_TERMS), onward, preferred_element_type=F32)
                a = jnp.exp(z + c + (lt - lpre))
                if mask is not None:
                    a = jnp.where(mask, a, 0.0)
                da = lax.dot_general(doh, vblk, NT, preferred_element_type=F32)
                g = da * a
                p = jnp.dot(_split_bf16(g, SUM_TERMS), before, preferred_element_type=F32) + gpre
                dz = g - jnp.exp(z + l1) * (g + p)
                if mask is not None:
                    dz = jnp.where(mask, dz, 0.0)
                dzs.append(dz.astype(BF16))
                probs.append(a.astype(BF16))
                new_l.append(lpre)
                new_g.append(gpre + jnp.sum(g, axis=1, keepdims=True))
            dq = dq + jnp.dot(jnp.concatenate(dzs, axis=1), jnp.concatenate(kpair, axis=0),
                              preferred_element_type=F32)
            dk_acc[pl.ds(ks, tk), :] += lax.dot_general(jnp.concatenate(dzs, axis=0), jnp.concatenate(qpair, axis=0),
                                                        TN, preferred_element_type=F32)
            dv_acc[pl.ds(ks, tk), :] += lax.dot_general(jnp.concatenate(probs, axis=0), jnp.concatenate(dopair, axis=0),
                                                        TN, preferred_element_type=F32)
            return (dq, *new_l, *new_g)

        zero1 = jnp.zeros((tq, 1), F32)
        init = (jnp.zeros((tq, LANES), F32), zero1, zero1, zero1, zero1)
        carry = lax.fori_loop(0, qi * nd, lambda j, cr: block(j, cr, None), init)
        for i in range(nd):
            carry = block(qi * nd + i, carry, _rel_mask(tq, tk, i, True))
        dqkv_ref[0, pl.ds(pl.multiple_of(qi * tq, tq), tq), :] = carry[0].astype(BF16)

        @pl.when(qi == nq - 1)
        def _():
            dqkv_ref[1] = dk_acc[...].astype(BF16)
            dqkv_ref[2] = dv_acc[...].astype(BF16)

    q_spec, k_spec, v_spec = _attn_specs(s, tq)
    return pl.pallas_call(
        body, name=name, grid=(hp_n, nq),
        in_specs=[q_spec, k_spec, v_spec, pl.BlockSpec((None, tq, LANES), lambda hp, qi: (hp, qi, 0)),
                  pl.BlockSpec((tq, LANES), lambda hp, qi: (qi, hp))],
        out_specs=pl.BlockSpec((3, s, LANES), lambda hp, qi: (0, 0, hp)),
        out_shape=jax.ShapeDtypeStruct((3, s, d), BF16),
        scratch_shapes=[pltpu.VMEM((s, LANES), F32), pltpu.VMEM((s, LANES), F32)],
        compiler_params=_params("parallel", "arbitrary"))(qkv, qkv, qkv, ltot, do)


def _fox_prep(fl, bias, name):
    s, w = fl.shape
    tb = _pick(s, 512, SUBLANES)

    def body(fl_ref, b_ref, cum_ref, carry_ref):
        i = pl.program_id(0)

        @pl.when(i == 0)
        def _():
            carry_ref[...] = jnp.zeros_like(carry_ref)

        logf = _log_sigmoid(fl_ref[...] + b_ref[...])
        row = lax.broadcasted_iota(jnp.int32, (tb, tb), 0)
        col = lax.broadcasted_iota(jnp.int32, (tb, tb), 1)
        incl = (col <= row).astype(BF16)
        tot = carry_ref[...]
        rem = logf
        for _ in range(3):
            part = rem.astype(BF16)
            tot = tot + jnp.dot(incl, part, preferred_element_type=F32)
            rem = rem - part.astype(F32)
        cum_ref[...] = tot
        carry_ref[...] = tot[tb - 1:tb, :]

    blk = pl.BlockSpec((tb, w), lambda i: (i, 0))
    return pl.pallas_call(body, name=name, grid=(s // tb,), in_specs=[blk, pl.BlockSpec((1, w), lambda i: (0, 0))],
                          out_specs=blk, out_shape=jax.ShapeDtypeStruct((s, w), F32),
                          scratch_shapes=[pltpu.VMEM((1, w), F32)], compiler_params=_params("arbitrary"))(fl, bias)


def _fox_gate_bwd(dcum, fl, bias, n_heads, name):
    s, w = fl.shape
    tb = _pick(s, 512, SUBLANES)
    nb = s // tb

    def body(dc_ref, fl_ref, b_ref, dfl_ref, db_ref, carry_ref):
        i = pl.program_id(0)

        @pl.when(i == 0)
        def _():
            carry_ref[...] = jnp.zeros_like(carry_ref)
            db_ref[...] = jnp.zeros_like(db_ref)

        row = lax.broadcasted_iota(jnp.int32, (tb, tb), 0)
        col = lax.broadcasted_iota(jnp.int32, (tb, tb), 1)
        incl = (col >= row).astype(BF16)
        tot = jnp.broadcast_to(carry_ref[...], (tb, w))
        rem = dc_ref[...]
        for _ in range(3):
            part = rem.astype(BF16)
            tot = tot + jnp.dot(incl, part, preferred_element_type=F32)
            rem = rem - part.astype(F32)
        carry_ref[...] = tot[0:1, :]
        xg = fl_ref[...] + b_ref[...]
        e = jnp.exp(-jnp.abs(xg))
        sig_neg = jnp.where(xg >= 0.0, e, 1.0) / (1.0 + e)
        lane = lax.broadcasted_iota(jnp.int32, (tb, w), 1)
        dfl = jnp.where(lane < n_heads, tot * sig_neg, 0.0)
        dfl_ref[...] = dfl.astype(BF16)
        db_ref[...] += jnp.sum(dfl, axis=0, keepdims=True)

    blk = pl.BlockSpec((tb, w), lambda i: (nb - 1 - i, 0))
    vec = pl.BlockSpec((1, w), lambda i: (0, 0))
    return pl.pallas_call(body, name=name, grid=(nb,), in_specs=[blk, blk, vec], out_specs=[blk, vec],
                          out_shape=[jax.ShapeDtypeStruct((s, w), BF16), jax.ShapeDtypeStruct((1, w), F32)],
                          scratch_shapes=[pltpu.VMEM((1, w), F32)], compiler_params=_params("arbitrary"))(dcum, fl, bias)


def _fox_fwd(qkv, cq, ck, tq, tk, name):
    _, s, d = qkv.shape
    hp_n, nq, nd = d // LANES, s // tq, tq // tk
    scale = HEAD_DIM ** -0.5

    def body(q_ref, k_ref, v_ref, cq_ref, ck_ref, o_ref, lse_ref):
        qi = pl.program_id(1)
        low = lax.broadcasted_iota(jnp.int32, (tq, LANES), 1) < HEAD_DIM
        qpair = _pair(q_ref[...] * scale)
        cqv = cq_ref[...]
        cqs = (cqv[:, 0:1], cqv[:, HEAD_DIM:HEAD_DIM + 1])

        def block(kb, carry, mask):
            acc, stats = carry[0], carry[1:]
            ks = pl.multiple_of(kb * tk, tk)
            kblk = k_ref[pl.ds(ks, tk), :]
            vpair = _pair(v_ref[pl.ds(ks, tk), :])
            probs, alphas, new = [], [], []
            for h, (qh, cqh) in enumerate(zip(qpair, cqs)):
                m, lsum = stats[2 * h], stats[2 * h + 1]
                z = lax.dot_general(qh, kblk, NT, preferred_element_type=F32)
                sc = z + (cqh - ck_ref[h:h + 1, pl.ds(ks, tk)])
                if mask is not None:
                    sc = jnp.where(mask, sc, -jnp.inf)
                m_new = jnp.maximum(m, jnp.max(sc, axis=1, keepdims=True))
                alpha = jnp.exp(m - m_new)
                p = jnp.exp(sc - m_new)
                new += [m_new, alpha * lsum + jnp.sum(p, axis=1, keepdims=True)]
                probs.append(p.astype(BF16))
                alphas.append(alpha)
            acc = jnp.where(low, alphas[0], alphas[1]) * acc + jnp.dot(
                jnp.concatenate(probs, axis=1), jnp.concatenate(vpair, axis=0), preferred_element_type=F32)
            return (acc, *new)

        neg = jnp.full((tq, 1), -jnp.inf, F32)
        zero1 = jnp.zeros((tq, 1), F32)
        carry = (jnp.zeros((tq, LANES), F32), neg, zero1, neg, zero1)
        for i in range(nd):
            carry = block(qi * nd + i, carry, _rel_mask(tq, tk, i, False))
        acc, m0, l0, m1, l1 = lax.fori_loop(0, qi * nd, lambda j, cr: block(qi * nd - 1 - j, cr, None), carry)
        o_ref[...] = (acc / jnp.where(low, l0, l1)).astype(BF16)
        lse_ref[...] = jnp.where(low, m0 + jnp.log(l0), m1 + jnp.log(l1))

    q_spec, k_spec, v_spec = _attn_specs(s, tq)
    pair_rows = pl.BlockSpec((None, tq, LANES), lambda hp, qi: (hp, qi, 0))
    return pl.pallas_call(
        body, name=name, grid=(hp_n, nq),
        in_specs=[q_spec, k_spec, v_spec, pair_rows, pl.BlockSpec((None, 2, s), lambda hp, qi: (hp, 0, 0))],
        out_specs=[pl.BlockSpec((tq, LANES), lambda hp, qi: (qi, hp)), pair_rows],
        out_shape=[jax.ShapeDtypeStruct((s, d), BF16), jax.ShapeDtypeStruct((hp_n, s, LANES), F32)],
        compiler_params=_params("parallel", "parallel"))(qkv, qkv, qkv, cq, ck)


def _fox_bwd(qkv, o, do, lse, cq, ck, tq, tk, name):
    _, s, d = qkv.shape
    hp_n, nq, nd = d // LANES, s // tq, tq // tk
    scale = HEAD_DIM ** -0.5

    def body(q_ref, k_ref, v_ref, o_ref, do_ref, lse_ref, cq_ref, ck_ref, dqkv_ref, dcq_ref, dck_ref, dk_acc, dv_acc):
        qi = pl.program_id(1)

        @pl.when(qi == 0)
        def _():
            dk_acc[...] = jnp.zeros_like(dk_acc)
            dv_acc[...] = jnp.zeros_like(dv_acc)
            dck_ref[...] = jnp.zeros_like(dck_ref)

        low = lax.broadcasted_iota(jnp.int32, (tq, LANES), 1) < HEAD_DIM
        dov, cqv, lsev = do_ref[...], cq_ref[...], lse_ref[...]
        qpair = _pair(q_ref[...] * scale)
        dopair = _pair(dov)
        prod = dov.astype(F32) * o_ref[...].astype(F32)
        deltas = (jnp.sum(jnp.where(low, prod, 0.0), axis=1, keepdims=True),
                  jnp.sum(jnp.where(low, 0.0, prod), axis=1, keepdims=True))
        cqs = (cqv[:, 0:1], cqv[:, HEAD_DIM:HEAD_DIM + 1])
        lses = (lsev[:, 0:1], lsev[:, HEAD_DIM:HEAD_DIM + 1])

        def block(kb, carry, mask):
            dq, rowsums = carry[0], carry[1:]
            ks = pl.multiple_of(kb * tk, tk)
            kblk, vblk = k_ref[pl.ds(ks, tk), :], v_ref[pl.ds(ks, tk), :]
            kpair = _pair(kblk * scale)
            dss, probs, new_rows = [], [], []
            for h, (qh, doh) in enumerate(zip(qpair, dopair)):
                z = lax.dot_general(qh, kblk, NT, preferred_element_type=F32)
                sc = z + (cqs[h] - ck_ref[h:h + 1, pl.ds(ks, tk)])
                p = jnp.exp(sc - lses[h])
                if mask is not None:
                    p = jnp.where(mask, p, 0.0)
                dp = lax.dot_general(doh, vblk, NT, preferred_element_type=F32)
                ds = p * (dp - deltas[h])
                dck_ref[h:h + 1, pl.ds(ks, tk)] -= jnp.sum(ds, axis=0, keepdims=True)
                new_rows.append(rowsums[h] + jnp.sum(ds, axis=1, keepdims=True))
                dss.append(ds.astype(BF16))
                probs.append(p.astype(BF16))
            dq = dq + jnp.dot(jnp.concatenate(dss, axis=1), jnp.concatenate(kpair, axis=0),
                              preferred_element_type=F32)
            dk_acc[pl.ds(ks, tk), :] += lax.dot_general(jnp.concatenate(dss, axis=0), jnp.concatenate(qpair, axis=0),
                                                       TN, preferred_element_type=F32)
            dv_acc[pl.ds(ks, tk), :] += lax.dot_general(jnp.concatenate(probs, axis=0), jnp.concatenate(dopair, axis=0),
                                                       TN, preferred_element_type=F32)
            return (dq, *new_rows)

        zero1 = jnp.zeros((tq, 1), F32)
        carry = lax.fori_loop(0, qi * nd, lambda j, cr: block(j, cr, None), (jnp.zeros((tq, LANES), F32), zero1, zero1))
        for i in range(nd):
            carry = block(qi * nd + i, carry, _rel_mask(tq, tk, i, False))
        dq, rs0, rs1 = carry
        dqkv_ref[0, pl.ds(pl.multiple_of(qi * tq, tq), tq), :] = dq.astype(BF16)
        dcq_ref[...] = jnp.where(low, rs0, rs1)

        @pl.when(qi == nq - 1)
        def _():
            dqkv_ref[1] = dk_acc[...].astype(BF16)
            dqkv_ref[2] = dv_acc[...].astype(BF16)

    q_spec, k_spec, v_spec = _attn_specs(s, tq)
    pair_rows = pl.BlockSpec((None, tq, LANES), lambda hp, qi: (hp, qi, 0))
    tile = pl.BlockSpec((tq, LANES), lambda hp, qi: (qi, hp))
    keys = pl.BlockSpec((None, 2, s), lambda hp, qi: (hp, 0, 0))
    return pl.pallas_call(
        body, name=name, grid=(hp_n, nq),
        in_specs=[q_spec, k_spec, v_spec, tile, tile, pair_rows, pair_rows, keys],
        out_specs=[pl.BlockSpec((3, s, LANES), lambda hp, qi: (0, 0, hp)), pair_rows, keys],
        out_shape=[jax.ShapeDtypeStruct((3, s, d), BF16), jax.ShapeDtypeStruct((hp_n, s, LANES), F32),
                   jax.ShapeDtypeStruct((hp_n, 2, s), F32)],
        scratch_shapes=[pltpu.VMEM((s, LANES), F32), pltpu.VMEM((s, LANES), F32)],
        compiler_params=_params("parallel", "arbitrary"))(qkv, qkv, qkv, o, do, lse, cq, ck)


def _loss_head(y, target, name):
    s, d = y.shape
    tm = _pick(s, 512, SUBLANES)

    def body(y_ref, t_ref, dy_ref, sq_ref):
        i = pl.program_id(0)
        diff = y_ref[...] - t_ref[...]
        dy_ref[...] = diff / d

        @pl.when(i == 0)
        def _():
            sq_ref[...] = jnp.zeros_like(sq_ref)

        sq_ref[...] += jnp.sum(diff * diff, axis=0, keepdims=True)

    row = pl.BlockSpec((tm, d), lambda i: (i, 0))
    vec = pl.BlockSpec((1, d), lambda i: (0, 0))
    return pl.pallas_call(body, name=name, grid=(s // tm,), in_specs=[row, row], out_specs=[row, vec],
                          out_shape=[jax.ShapeDtypeStruct((s, d), F32), jax.ShapeDtypeStruct((1, d), F32)],
                          compiler_params=_params("arbitrary"))(y, target)


def _mod_fwd(c_all, w_mod, b_mod_cols, name):
    nl, d, cols = w_mod.shape
    nb = c_all.shape[0]

    def body(c_ref, w_ref, b_ref, o_ref):
        cv = c_ref[...]
        act = (cv * jax.nn.sigmoid(cv)).astype(BF16)
        o_ref[...] = jnp.dot(act, w_ref[...].astype(BF16), preferred_element_type=F32) + b_ref[...]

    return pl.pallas_call(
        body, name=name, grid=(nl,),
        in_specs=[pl.BlockSpec((nb, d), lambda l: (0, 0)), pl.BlockSpec((None, d, cols), lambda l: (l, 0, 0)),
                  pl.BlockSpec((None, 1, cols), lambda l: (l, 0, 0))],
        out_specs=pl.BlockSpec((None, nb, cols), lambda l: (l, 0, 0)),
        out_shape=jax.ShapeDtypeStruct((nl, nb, cols), F32), compiler_params=_params("parallel"))(c_all, w_mod, b_mod_cols)


def _wmod_grad(c_all_t, dmod, name):
    d, nb = c_all_t.shape
    _, nl, cols = dmod.shape

    def body(c_ref, dm_ref, o_ref):
        cv = c_ref[...]
        act = cv * jax.nn.sigmoid(cv)
        tot = act[:, 0:1] * dm_ref[0]
        for b in range(1, nb):
            tot = tot + act[:, b:b + 1] * dm_ref[b]
        o_ref[...] = tot

    return pl.pallas_call(
        body, name=name, grid=(nl,),
        in_specs=[pl.BlockSpec((d, nb), lambda l: (0, 0)), pl.BlockSpec((nb, None, 1, cols), lambda l: (0, l, 0, 0))],
        out_specs=pl.BlockSpec((None, d, cols), lambda l: (l, 0, 0)),
        out_shape=jax.ShapeDtypeStruct((nl, d, cols), F32), compiler_params=_params("parallel"))(
            c_all_t, dmod.reshape(nb, nl, 1, cols))


def _adamw(recv, w, m, v, name, *, tr=256, tc=512):
    nq, nl, rp, cp = recv.shape
    _, r, c = w.shape
    br = _pick(r, tr, SUBLANES) if rp == r else r
    bc = _pick(c, tc, LANES) if cp == c else c
    rbr = br if rp == r else rp
    rbc = bc if cp == c else cp

    def body(rv_ref, w_ref, m_ref, v_ref, g_ref, d_ref, nm_ref, nv_ref):
        g = rv_ref[0, :br, :bc].astype(F32)
        for qd in range(1, nq):
            g = g + rv_ref[qd, :br, :bc].astype(F32)
        m_new = ADAM_B1 * m_ref[...] + (1.0 - ADAM_B1) * g
        v_new = ADAM_B2 * v_ref[...] + (1.0 - ADAM_B2) * jnp.square(g)
        m_hat = m_new / (1.0 - ADAM_B1 ** ADAM_STEP)
        v_hat = v_new / (1.0 - ADAM_B2 ** ADAM_STEP)
        g_ref[...] = g
        d_ref[...] = -ADAM_LR * (m_hat / (jnp.sqrt(v_hat) + ADAM_EPS) + ADAM_WD * w_ref[...])
        nm_ref[...] = m_new
        nv_ref[...] = v_new

    blk = pl.BlockSpec((None, br, bc), lambda l, i, j: (l, i, j))
    rblk = pl.BlockSpec((nq, None, rbr, rbc), lambda l, i, j: (0, l, i, j))
    shape = jax.ShapeDtypeStruct(w.shape, F32)
    return pl.pallas_call(body, name=name, grid=(nl, r // br, c // bc), in_specs=[rblk, blk, blk, blk],
                          out_specs=[blk] * 4, out_shape=[shape] * 4,
                          compiler_params=_params("parallel", "parallel", "parallel"))(recv, w, m, v)


def _mesh_pos():
    return lax.axis_index("x"), lax.axis_index("y"), lax.axis_index("c")


def _allgather_small(block, name):
    m_per, n = block.shape

    def body(x_ref, out_ref, send_sems, recv_sems, local_sem):
        x, y, c = _mesh_pos()
        me, sibling = (x, y, c), (x, y, 1 - c)
        chips = [(1 - x, y), (x, 1 - y), (1 - x, 1 - y)]

        def rows(px, py, pc):
            return out_ref.at[pl.ds((4 * px + 2 * py + pc) * m_per, m_per), :]

        def copy(k, blk, to, src=None):
            return pltpu.make_async_remote_copy(
                src_ref=rows(*blk) if src is None else src, dst_ref=rows(*blk),
                send_sem=send_sems.at[k], recv_sem=recv_sems.at[k], device_id=to, device_id_type=MESH)

        mine = pltpu.make_async_copy(x_ref, rows(*me), local_sem)
        mine.start()
        first = [copy(0, me, sibling, src=x_ref)]
        first += [copy(1 + j, me, (*chip, c), src=x_ref) for j, chip in enumerate(chips)]
        for cp in first:
            cp.start()
        passed = [copy(4 + j, (*chip, c), sibling) for j, chip in enumerate(chips)]
        for j, chip in enumerate(chips):
            copy(1 + j, (*chip, c), me).wait_recv()
            passed[j].start()
        copy(0, sibling, me).wait_recv()
        for j, chip in enumerate(chips):
            copy(4 + j, (*chip, 1 - c), me).wait_recv()
        for cp in first + passed:
            cp.wait_send()
        mine.wait()

    return pl.pallas_call(
        body, name=name, out_shape=jax.ShapeDtypeStruct((N_DEV * m_per, n), block.dtype),
        in_specs=[pl.BlockSpec(memory_space=pltpu.VMEM)], out_specs=pl.BlockSpec(memory_space=pltpu.VMEM),
        scratch_shapes=[pltpu.SemaphoreType.DMA((7,)), pltpu.SemaphoreType.DMA((7,)), pltpu.SemaphoreType.DMA],
        compiler_params=pltpu.CompilerParams(vmem_limit_bytes=VMEM_LIMIT_BYTES))(block)


def _exchange(srcs, dst_shapes, jobs, name):
    n_src, n_job = len(srcs), len(jobs)

    def body(*refs):
        src_refs, dst_refs = refs[:n_src], refs[n_src:n_src + len(dst_shapes)]
        send_sems, recv_sems, local_sems = refs[n_src + len(dst_shapes):]
        x, y, c = _mesh_pos()
        me = 4 * x + 2 * y + c
        pending = []
        for t, (si, di, src_slice, dst_slice) in enumerate(jobs):
            src, dst = src_refs[si], dst_refs[di]
            lc = pltpu.make_async_copy(src_slice(src, me), dst_slice(dst, me), local_sems.at[t])
            lc.start()
            pending.append(lc)
            for dd in range(1, N_DEV):
                px = 1 - x if dd & 4 else x
                py = 1 - y if dd & 2 else y
                pc = 1 - c if dd & 1 else c
                cp = pltpu.make_async_remote_copy(
                    src_ref=src_slice(src, 4 * px + 2 * py + pc), dst_ref=dst_slice(dst, me),
                    send_sem=send_sems.at[t, dd - 1], recv_sem=recv_sems.at[t, dd - 1],
                    device_id=(px, py, pc), device_id_type=MESH)
                cp.start()
                pending.append(cp)
        for cp in pending:
            cp.wait()

    hbm = pl.BlockSpec(memory_space=pl.ANY)
    return pl.pallas_call(
        body, name=name, out_shape=list(dst_shapes), in_specs=[hbm] * n_src, out_specs=[hbm] * len(dst_shapes),
        scratch_shapes=[pltpu.SemaphoreType.DMA((n_job, N_DEV - 1)), pltpu.SemaphoreType.DMA((n_job, N_DEV - 1)),
                        pltpu.SemaphoreType.DMA((n_job,))])(*srcs)


def _peer(x, y, c, dd):
    return (1 - x if dd & 4 else x, 1 - y if dd & 2 else y, 1 - c if dd & 1 else c)


def _exchange_copies(jobs, src_refs, dst_refs, send_sems, recv_sems, sending, only=None):
    x, y, c = _mesh_pos()
    me = 4 * x + 2 * y + c
    local, remote = [], []
    for t, (si, di, src_slice, dst_slice) in enumerate(jobs):
        if only is not None and t not in only:
            continue
        local.append(pltpu.make_async_copy(src_slice(src_refs[si], me), dst_slice(dst_refs[di], me),
                                           send_sems.at[t * N_DEV]))
        for dd in range(1, N_DEV):
            px, py, pc = _peer(x, y, c, dd)
            p = 4 * px + 2 * py + pc
            remote.append(pltpu.make_async_remote_copy(
                src_ref=src_slice(src_refs[si], p), dst_ref=dst_slice(dst_refs[di], me if sending else p),
                send_sem=send_sems.at[t * N_DEV + dd], recv_sem=recv_sems.at[t * N_DEV + dd],
                device_id=(px, py, pc), device_id_type=MESH))
    return local, remote


def _exchange_start(srcs, dsts, jobs, name, after=None):
    n_src, n_dst, n_job = len(srcs), len(dsts), len(jobs)
    n_in = n_src + n_dst + (after is not None)

    def body(*refs):
        src_refs, dst_refs = refs[:n_src], refs[n_src:n_src + n_dst]
        send_sems, recv_sems = refs[n_in], refs[n_in + 1]
        local, remote = _exchange_copies(jobs, src_refs, dst_refs, send_sems, recv_sems, True)
        for cp in remote + local:
            cp.start()
        refs[-1][...] = jnp.zeros((SUBLANES, LANES), F32)

    hbm = pl.BlockSpec(memory_space=pltpu.HBM)
    sem = pl.BlockSpec(memory_space=pltpu.SEMAPHORE)
    operands = [pltpu.with_memory_space_constraint(a, pltpu.HBM) for a in (*srcs, *dsts)]
    extra_specs = [] if after is None else [pl.BlockSpec(memory_space=pl.ANY)]
    extra = [] if after is None else [after]
    res = pl.pallas_call(
        body, name=name, in_specs=[hbm] * (n_src + n_dst) + extra_specs,
        out_specs=(sem, sem, *[hbm] * (n_src + n_dst), pl.BlockSpec(memory_space=pltpu.VMEM)),
        out_shape=(pltpu.SemaphoreType.DMA((n_job * N_DEV,)), pltpu.SemaphoreType.DMA((n_job * N_DEV,)),
                   *[pltpu.HBM(a.shape, a.dtype) for a in (*srcs, *dsts)],
                   jax.ShapeDtypeStruct((SUBLANES, LANES), F32)),
        input_output_aliases={i: 2 + i for i in range(n_src + n_dst)},
        compiler_params=pltpu.CompilerParams(has_side_effects=pltpu.SideEffectType.DATAFLOW_SIDE_EFFECTING))(
            *operands, *extra)
    return res[0], res[1], list(res[2:2 + n_src]), list(res[2 + n_src:2 + n_src + n_dst]), res[-1]


def _exchange_wait(parts, dsts, after, name):
    n_dst = len(dsts)
    n_src = sum(len(p[2]) for p in parts)

    def body(*refs):
        dst_refs = refs[n_src:n_src + n_dst]
        sem_refs = refs[n_src + n_dst:n_src + n_dst + 2 * len(parts)]
        at = 0
        for k, (_, _, part_srcs, jobs, only) in enumerate(parts):
            src_refs = refs[at:at + len(part_srcs)]
            at += len(part_srcs)
            local, remote = _exchange_copies(jobs, src_refs, dst_refs, sem_refs[2 * k], sem_refs[2 * k + 1], False, only)
            for cp in local:
                cp.wait()
            for cp in remote:
                cp.wait_send()
                cp.wait_recv()

    hbm = pl.BlockSpec(memory_space=pltpu.HBM)
    sem = pl.BlockSpec(memory_space=pltpu.SEMAPHORE)
    srcs = [a for p in parts for a in p[2]]
    sems = [s for p in parts for s in (p[0], p[1])]
    res = pl.pallas_call(
        body, name=name, in_specs=[hbm] * (n_src + n_dst) + [sem] * len(sems) + [pl.BlockSpec(memory_space=pl.ANY)],
        out_specs=[hbm] * (n_src + n_dst),
        out_shape=[pltpu.HBM(a.shape, a.dtype) for a in (*srcs, *dsts)],
        input_output_aliases={i: i for i in range(n_src + n_dst)},
        compiler_params=pltpu.CompilerParams(has_side_effects=pltpu.SideEffectType.DATAFLOW_SIDE_EFFECTING))(
            *srcs, *dsts, *sems, after)
    return list(res[:n_src]), list(res[n_src:])


def _whole(ref, p):
    return ref


def _layer_of(layer, inner):
    return lambda ref, p: inner(ref.at[layer], p)


def _cols_of(width):
    def take(ref, p):
        lead = (slice(None),) * (len(ref.shape) - 1)
        return ref.at[lead + (pl.ds(pl.multiple_of(p * width, LANES), width),)]
    return take


def _rows_of(height):
    def take(ref, p):
        lead = (slice(None),) * (len(ref.shape) - 2)
        return ref.at[lead + (pl.ds(pl.multiple_of(p * height, SUBLANES), height), slice(None))]
    return take


def _slot(layer=None):
    if layer is None:
        return lambda ref, p: ref.at[p]
    return lambda ref, p: ref.at[p, layer]


def _local_step(x0, target, mod, g_mix_pre, g_mix_post, g_ffn_pre, g_ffn_post, wqkv, wo, wg, wu, wd,
                wfg, bfg, wconv, bconv, tiles, fetch=None, on_grads=None, zero=0.0):
    s, d = x0.shape
    nl = mod.shape[0]
    n_heads = d // HEAD_DIM
    hp_n = d // LANES
    vec = lambda a: a.reshape(1, -1)
    saved = []
    xcur = x0
    for l in range(nl):
        if fetch is not None:
            wqkv = fetch(l, 0, xcur)[0]
        sh_a, sc_a, gt_a, sh_f, sc_f, gt_f = (vec(mod[l, i * d:(i + 1) * d]) for i in range(6))
        fox = l % 2 == 1
        g1 = vec(g_mix_pre[l])
        if l == 0 and fetch is not None:
            g1 = g1 + zero
        h1 = _norm_mod_fwd(xcur, g1, sh_a, sc_a, f"norm_mix_fwd_{l}")
        qkv = _mm_nn(h1, wqkv, l, BF16, f"qkv_fwd_{l}", split_out=d)
        if fox:
            j = l // 2
            fl = _mm_nn(h1, wfg, j, F32, f"fgate_fwd_{l}")
            cum = _fox_prep(fl, bfg[j], f"fox_prep_{l}")
            cum_h = cum[:, :n_heads]
            cq = jnp.repeat(cum_h.reshape(s, hp_n, 2).transpose(1, 0, 2), HEAD_DIM, axis=2)
            ck = cum_h.T.reshape(hp_n, 2, s)
            o, stat = _fox_fwd(qkv, cq, ck, *tiles["fox_fwd"], f"fox_fwd_{l}")
            extra = (fl, cq, ck)
        else:
            o, stat = _sb_fwd(qkv, *tiles["sb_fwd"], f"sb_fwd_{l}")
            extra = None
        if fetch is not None:
            wqkv, wo, wg, wu, wd = fetch(l, 1, o)
        u = _mm_nn(o, wo, l, F32, f"attn_out_fwd_{l}")
        x2 = _post_fwd(xcur, u, vec(g_mix_post[l]), gt_a, f"post_mix_fwd_{l}")
        h2 = _norm_mod_fwd(x2, vec(g_ffn_pre[l]), sh_f, sc_f, f"norm_ffn_fwd_{l}")
        gp = _mm_nn(h2, wg, l, BF16, f"ffn_gate_fwd_{l}")
        up = _mm_nn(h2, wu, l, BF16, f"ffn_up_fwd_{l}")
        act = _conv_act_fwd(gp, up, wconv[l], bconv[l], f"conv_act_fwd_{l}")
        yv = _mm_nn(act, wd, l, F32, f"ffn_down_fwd_{l}")
        x3 = _post_fwd(x2, yv, vec(g_ffn_post[l]), gt_f, f"post_ffn_fwd_{l}")
        saved.append((xcur, h1, qkv, o, stat, extra, u, x2, h2, gp, up, act, yv))
        xcur = x3

    dx, sq = _loss_head(xcur, target, "loss_head")
    small, big = [None] * nl, [None] * nl
    nudge = None
    for l in reversed(range(nl)):
        xin, h1, qkv, o, stat, extra, u, x2, h2, gp, up, act, yv = saved[l]
        sc_a, gt_a, sc_f, gt_f = (vec(mod[l, i * d:(i + 1) * d]) for i in (1, 2, 4, 5))
        fox = l % 2 == 1
        g4 = vec(g_ffn_post[l])
        if nudge is not None:
            g4 = g4 + nudge
        dy, dgt_f, dg4 = _post_bwd(dx, yv, g4, gt_f, f"post_ffn_bwd_{l}")
        dact = _mm_nt(dy, wd, l, BF16, f"ffn_down_dx_{l}")
        dwd = _mm_tn(act, dy, BF16, f"ffn_down_dw_{l}")
        dgp, dup, dwc, dbc = _conv_act_bwd(dact, gp, up, wconv[l], bconv[l], f"conv_act_bwd_{l}")
        dh2 = _mm_nt(dgp, wg, l, F32, f"ffn_gate_dx_{l}")
        dh2 = _mm_nt(dup, wu, l, F32, f"ffn_up_dx_{l}", add=dh2)
        dwg = _mm_tn(h2, dgp, BF16, f"ffn_gate_dw_{l}")
        dwu = _mm_tn(h2, dup, BF16, f"ffn_up_dw_{l}")
        g3 = vec(g_ffn_pre[l])
        if on_grads is not None:
            g3 = g3 + on_grads(l, dict(gate=dwg, up=dwu, down=dwd))
        dx2, dsh_f, dsc_f, dg3 = _norm_mod_bwd(dh2, x2, g3, sc_f, dx, f"norm_ffn_bwd_{l}")
        du, dgt_a, dg2 = _post_bwd(dx2, u, vec(g_mix_post[l]), gt_a, f"post_mix_bwd_{l}")
        do = _mm_nt(du, wo, l, BF16, f"attn_out_dx_{l}")
        dwo = _mm_tn(o, du, BF16, f"attn_out_dw_{l}")
        if fox:
            j = l // 2
            fl, cq, ck = extra
            dqkv, dcq, dck = _fox_bwd(qkv, o, do, stat, cq, ck, *tiles["fox_bwd"], f"fox_bwd_{l}")
            dcq = jnp.max(dcq.reshape(hp_n, s, 2, HEAD_DIM), axis=3)
            dcum = dcq.transpose(1, 0, 2).reshape(s, n_heads) + dck.reshape(n_heads, s).T
            dcum = jnp.pad(dcum, ((0, 0), (0, LANES - n_heads)))
            dfl, dbfg = _fox_gate_bwd(dcum, fl, bfg[j], n_heads, f"fox_gate_bwd_{l}")
            dh1 = _mm_nt(dfl, wfg, j, F32, f"fgate_dx_{l}")
            dh1 = _mm_nt(dqkv, wqkv, l, F32, f"qkv_dx_{l}", add=dh1)
            dwfg = _mm_tn(h1, dfl, F32, f"fgate_dw_{l}")[:, :n_heads]
            dbfg = dbfg[0, :n_heads]
        else:
            dqkv = _sb_bwd(qkv, stat, do, *tiles["sb_bwd"], f"sb_bwd_{l}")
            dh1 = _mm_nt(dqkv, wqkv, l, F32, f"qkv_dx_{l}")
            dwfg = dbfg = None
        dwqkv = _mm_tn(h1, dqkv, BF16, f"qkv_dw_{l}")
        dx, dsh_a, dsc_a, dg1 = _norm_mod_bwd(dh1, xin, vec(g_mix_pre[l]), sc_a, dx2, f"norm_mix_bwd_{l}")
        dmod = jnp.concatenate([dsh_a, dsc_a, dgt_a, dsh_f, dsc_f, dgt_f], axis=1)[0]
        small[l] = dict(dmod=dmod, dg1=dg1[0], dg2=dg2[0], dg3=dg3[0], dg4=dg4[0], dbc=dbc[0], dwc=dwc,
                        dbfg=dbfg, dwfg=dwfg)
        big[l] = dict(qkv=dwqkv, o=dwo, gate=dwg, up=dwu, down=dwd)
        if on_grads is not None:
            nudge = on_grads(l, dict(qkv=dwqkv, o=dwo))
    return sq, dx, small, big


def _rows128(a, rows):
    flat = a.reshape(-1)
    return jnp.pad(flat, (0, rows * LANES - flat.shape[0])).reshape(rows, LANES)


def _ceil8(n_elems):
    rows = -(-n_elems // LANES)
    return -(-rows // SUBLANES) * SUBLANES


def kernel(x, c, w_mod, b_mod, g_mix_pre, g_mix_post, w_qkv, w_o, w_fg, b_fg, g_ffn_pre, g_ffn_post, w_ffn_gate, w_ffn_up, w_conv, b_conv, w_ffn_down, loss_target, m_w_mod, m_b_mod, m_g_mix_pre, m_g_mix_post, m_w_qkv, m_w_o, m_w_fg, m_b_fg, m_g_ffn_pre, m_g_ffn_post, m_w_ffn_gate, m_w_ffn_up, m_w_conv, m_b_conv, m_w_ffn_down, v_w_mod, v_b_mod, v_g_mix_pre, v_g_mix_post, v_w_qkv, v_w_o, v_w_fg, v_b_fg, v_g_ffn_pre, v_g_ffn_post, v_w_ffn_gate, v_w_ffn_up, v_w_conv, v_b_conv, v_w_ffn_down):
    _, s, d = x.shape
    nl = w_qkv.shape[0]
    nf = w_fg.shape[0]
    n_heads = w_fg.shape[2]
    fs = w_ffn_gate.shape[2]
    fp = -(-fs // LANES) * LANES
    f_full, f_pad = N_DEV * fs, N_DEV * fp
    mod_cols = w_mod.shape[2]
    qs, orows = w_qkv.shape[2], w_o.shape[1]
    xi, yi, ci = _mesh_pos()
    me = 4 * xi + 2 * yi + ci

    c_rows = d // LANES
    c_all = _allgather_small(jnp.pad(c.reshape(1, d), ((0, SUBLANES - 1), (0, 0))).reshape(SUBLANES * c_rows, LANES),
                             "gather_cond")
    c_all = c_all.reshape(N_DEV, SUBLANES, d)[:, 0, :]
    b_mod_cols = lax.dynamic_slice_in_dim(b_mod, me * mod_cols, mod_cols, axis=1).reshape(nl, 1, mod_cols)
    mod_part = _mod_fwd(c_all, w_mod, b_mod_cols, "mod_fwd")

    conv_pad = jnp.pad(w_conv, ((0, 0), (0, 0), (0, fp - fs)))
    r_mod, r_conv, r_fg = _ceil8(mod_part.size), _ceil8(conv_pad.size), _ceil8(w_fg.size)
    payload = jnp.concatenate([_rows128(mod_part, r_mod), _rows128(conv_pad, r_conv), _rows128(w_fg, r_fg)], axis=0)
    got = _allgather_small(payload, "gather_small_weights").reshape(N_DEV, r_mod + r_conv + r_fg, LANES)
    mod_g = got[:, :r_mod].reshape(N_DEV, -1)[:, :mod_part.size].reshape(N_DEV, nl, N_DEV, mod_cols)
    mod = lax.dynamic_index_in_dim(mod_g, me, axis=2, keepdims=False).transpose(1, 0, 2).reshape(nl, N_DEV * mod_cols)
    conv_g = got[:, r_mod:r_mod + r_conv].reshape(N_DEV, -1)[:, :conv_pad.size].reshape(N_DEV, nl, 3, fp)
    wconv_full = conv_g.transpose(1, 2, 0, 3).reshape(nl, 3, f_pad)
    fg_g = got[:, r_mod + r_conv:].reshape(N_DEV, -1)[:, :w_fg.size].reshape(N_DEV, nf, orows, n_heads)
    wfg_full = fg_g.transpose(1, 0, 2, 3).reshape(nf, d, n_heads)
    wfg_full = jnp.pad(wfg_full, ((0, 0), (0, 0), (0, LANES - n_heads))).astype(BF16)
    bfg_full = jnp.pad(b_fg, ((0, 0), (0, LANES - n_heads))).reshape(nf, 1, LANES)
    bconv_full = jnp.pad(b_conv.reshape(nl, N_DEV, fs), ((0, 0), (0, 0), (0, fp - fs))).reshape(nl, 1, f_pad)

    gate_sh = jnp.pad(w_ffn_gate, ((0, 0), (0, 0), (0, fp - fs))).astype(BF16)
    up_sh = jnp.pad(w_ffn_up, ((0, 0), (0, 0), (0, fp - fs))).astype(BF16)
    down_sh = jnp.pad(w_ffn_down, ((0, 0), (0, fp - fs), (0, 0))).astype(BF16)
    shards = [w_qkv.astype(BF16), w_o.astype(BF16), gate_sh, up_sh, down_sh]
    full_shapes = [jax.ShapeDtypeStruct((nl, d, N_DEV * qs), BF16), jax.ShapeDtypeStruct((nl, d, d), BF16),
                   jax.ShapeDtypeStruct((nl, d, f_pad), BF16), jax.ShapeDtypeStruct((nl, d, f_pad), BF16),
                   jax.ShapeDtypeStruct((nl, f_pad, d), BF16)]
    place = [_cols_of(qs), _rows_of(orows), _cols_of(fp), _cols_of(fp), _rows_of(fp)]
    full = [lax.empty(sh.shape, sh.dtype) for sh in full_shapes]
    gathers, zero = [], 0.0
    for l in range(nl):
        jobs_l = [(i, i, _layer_of(l, _whole), _layer_of(l, place[i])) for i in range(5)]
        send_s, recv_s, shards, full, token = _exchange_start(shards, full, jobs_l, f"gather_start_{l}",
                                                              after=got if l == 0 else None)
        gathers.append((send_s, recv_s, jobs_l))
        zero = zero + token[0, 0]
    state = {"shards": shards, "full": full}

    def fetch(l, part, after):
        send_s, recv_s, jobs_l = gathers[l]
        only = (0,) if part == 0 else (1, 2, 3, 4)
        state["shards"], state["full"] = _exchange_wait([(send_s, recv_s, state["shards"], jobs_l, only)],
                                                        state["full"], after, f"gather_wait_{l}_{part}")
        return state["full"]

    order = ["qkv", "o", "gate", "up", "down"]
    send = {"qkv": _cols_of(qs), "o": _rows_of(orows), "gate": _cols_of(fp), "up": _cols_of(fp), "down": _rows_of(fp)}
    recv_shapes = [(N_DEV, nl, d, qs), (N_DEV, nl, orows, d), (N_DEV, nl, d, fp), (N_DEV, nl, d, fp), (N_DEV, nl, fp, d)]
    scatters = []
    landing = {"recv": [lax.empty(sh, BF16) for sh in recv_shapes]}

    def on_grads(l, grads):
        names = [nm for nm in order if nm in grads]
        jobs_l = [(k, order.index(nm), send[nm], _slot(l)) for k, nm in enumerate(names)]
        send_s, recv_s, srcs, landing["recv"], token = _exchange_start(
            [grads[nm] for nm in names], landing["recv"], jobs_l, f"scatter_start_{l}_{names[0]}")
        scatters.append((send_s, recv_s, srcs, jobs_l, None))
        return token[0, 0]

    sq, dx, small, big = _local_step(x[0], loss_target[0], mod, g_mix_pre, g_mix_post, g_ffn_pre, g_ffn_post,
                                     None, None, None, None, None, wfg_full, bfg_full, wconv_full, bconv_full,
                                     ATTN_TILES, fetch=fetch, on_grads=on_grads, zero=zero)
    loss = lax.psum(0.5 * jnp.sum(sq) / d, ("x", "y", "c"))
    _, recv = _exchange_wait(scatters, landing["recv"], dx, "scatter_wait")
    upd = {}
    for nm, rv, wt, mt, vt in zip(order, recv, [w_qkv, w_o, w_ffn_gate, w_ffn_up, w_ffn_down],
                                  [m_w_qkv, m_w_o, m_w_ffn_gate, m_w_ffn_up, m_w_ffn_down],
                                  [v_w_qkv, v_w_o, v_w_ffn_gate, v_w_ffn_up, v_w_ffn_down]):
        upd[nm] = _adamw(rv, wt, mt, vt, f"adamw_{nm}")

    stack = lambda key: jnp.stack([small[l][key] for l in range(nl)])
    dmod = stack("dmod")
    dgs = [stack(k) for k in ("dg1", "dg2", "dg3", "dg4")]
    dbc = stack("dbc").reshape(nl, N_DEV, fp)[:, :, :fs].reshape(nl, f_full)
    dbfg = jnp.stack([small[l]["dbfg"] for l in range(nl) if l % 2 == 1])
    dwc = stack("dwc").reshape(nl, 3, N_DEV, fp)[:, :, :, :fs].reshape(nl, 3, f_full)
    dwfg = jnp.stack([small[l]["dwfg"] for l in range(nl) if l % 2 == 1])
    rep_parts = [dmod] + dgs + [dbc, dbfg]
    rep_rows = [_ceil8(p.size) for p in rep_parts]
    r_rep, r_wc, r_wfg = sum(rep_rows), _ceil8(dwc.size), _ceil8(dwfg.size)
    payload = jnp.concatenate([_rows128(p, r) for p, r in zip(rep_parts, rep_rows)]
                              + [_rows128(dwc, r_wc), _rows128(dwfg, r_wfg)], axis=0)
    gsm = _allgather_small(payload, "gather_small_grads").reshape(N_DEV, 1, r_rep + r_wc + r_wfg, LANES)

    def pack(parts):
        return jnp.concatenate([_rows128(p, r) for p, r in zip(parts, rep_rows)], axis=0).reshape(1, r_rep, LANES)

    rep_w = [b_mod, g_mix_pre, g_mix_post, g_ffn_pre, g_ffn_post, b_conv, b_fg]
    rep_m = [m_b_mod, m_g_mix_pre, m_g_mix_post, m_g_ffn_pre, m_g_ffn_post, m_b_conv, m_b_fg]
    rep_v = [v_b_mod, v_g_mix_pre, v_g_mix_post, v_g_ffn_pre, v_g_ffn_post, v_b_conv, v_b_fg]
    rep_out = _adamw(gsm[:, :, :r_rep], pack(rep_w), pack(rep_m), pack(rep_v), "adamw_replicated", tr=r_rep, tc=LANES)

    def unpack(packed):
        outs, at = [], 0
        for p, r in zip(rep_w, rep_rows):
            outs.append(packed[0, at:at + r].reshape(-1)[:p.size].reshape(p.shape))
            at += r
        return outs

    rep_g, rep_d, rep_nm, rep_nv = (unpack(a) for a in rep_out)

    wc_all = gsm[:, 0, r_rep:r_rep + r_wc].reshape(N_DEV, -1)[:, :dwc.size].reshape(N_DEV, 1, nl * 3, f_full)
    wc_mine = lax.dynamic_slice_in_dim(wc_all, me * fs, fs, axis=3)
    wc_out = _adamw(wc_mine, w_conv.reshape(1, nl * 3, fs), m_w_conv.reshape(1, nl * 3, fs),
                    v_w_conv.reshape(1, nl * 3, fs), "adamw_conv", tr=nl * 3, tc=fs)
    wc_out = [a.reshape(nl, 3, fs) for a in wc_out]
    wfg_all = gsm[:, 0, r_rep + r_wc:].reshape(N_DEV, -1)[:, :dwfg.size].reshape(N_DEV, nf, d, n_heads)
    wfg_mine = lax.dynamic_slice_in_dim(wfg_all, me * orows, orows, axis=2)
    wfg_out = _adamw(wfg_mine, w_fg, m_w_fg, v_w_fg, "adamw_fgate", tr=orows, tc=n_heads)

    dmod_all = gsm[:, 0, :rep_rows[0]].reshape(N_DEV, -1)[:, :dmod.size].reshape(N_DEV, nl, N_DEV * mod_cols)
    dmod_mine = lax.dynamic_slice_in_dim(dmod_all, me * mod_cols, mod_cols, axis=2)
    gwmod = _wmod_grad(c_all.T, dmod_mine, "wmod_grad")
    wmod_out = _adamw(gwmod.reshape(1, nl, d, mod_cols), w_mod, m_w_mod, v_w_mod, "adamw_mod")

    per_weight = [wmod_out, None, None, None, upd["qkv"], upd["o"], wfg_out, None, None, None,
                  upd["gate"], upd["up"], wc_out, None, upd["down"]]
    rep_index = {1: 0, 2: 1, 3: 2, 8: 3, 9: 4, 13: 5, 7: 6}
    outs = [[], [], [], []]
    for pos, res in enumerate(per_weight):
        for kind in range(4):
            if res is None:
                outs[kind].append((rep_g, rep_d, rep_nm, rep_nv)[kind][rep_index[pos]])
            else:
                outs[kind].append(res[kind])
    return (loss, dx.reshape(1, s, d), *outs[0], *outs[1], *outs[2], *outs[3])
```

```python
import functools

import jax
import jax.numpy as jnp
from jax import lax
from jax.experimental import pallas as pl
from jax.experimental.pallas import tpu as pltpu

F32 = jnp.float32
BF16 = jnp.bfloat16

N_DEV = 8
HEAD_DIM = 64
LANES = 128
SUBLANES = 8
RMS_EPS = 1e-6
ADAM_LR = 0.001
ADAM_B1 = 0.9
ADAM_B2 = 0.999
ADAM_EPS = 1e-08
ADAM_WD = 0.01
ADAM_STEP = 10
VMEM_LIMIT_BYTES = 56 * 1024 * 1024
SUM_TERMS = 1
HALO = 16
MM_ROWS = 2048
ATTN_TILES = {"sb_fwd": (1024, 256), "sb_bwd": (1024, 256), "fox_fwd": (1024, 256), "fox_bwd": (1024, 256)}

NN = (((1,), (0,)), ((), ()))
NT = (((1,), (1,)), ((), ()))
TN = (((0,), (0,)), ((), ()))
MESH = pl.DeviceIdType.MESH


def _params(*sem):
    return pltpu.CompilerParams(dimension_semantics=sem, vmem_limit_bytes=VMEM_LIMIT_BYTES)


def _pick(n, pref, quantum):
    if n <= pref:
        return n
    t = (pref // quantum) * quantum
    while n % t:
        t -= quantum
    return t


def _mm(a, b, *, dims, grid, a_spec, b_spec, o_spec, out_shape, name, add=None):
    nk = grid[2]
    acc_shape = tuple(d for d in o_spec.block_shape if d is not None)
    o_dtype = out_shape.dtype

    def body(*refs):
        if add is None:
            a_ref, b_ref, o_ref, acc_ref = refs
            add_ref = None
        else:
            a_ref, b_ref, add_ref, o_ref, acc_ref = refs
        k = pl.program_id(2)
        part = lax.dot_general(a_ref[...], b_ref[...], dims, preferred_element_type=F32)

        def finish(total):
            if add_ref is not None:
                total = total + add_ref[...]
            o_ref[...] = total.astype(o_dtype)

        if nk == 1:
            finish(part)
            return

        @pl.when(k == 0)
        def _():
            acc_ref[...] = part

        @pl.when(jnp.logical_and(k > 0, k < nk - 1))
        def _():
            acc_ref[...] += part

        @pl.when(k == nk - 1)
        def _():
            finish(acc_ref[...] + part)

    operands = [a, b] if add is None else [a, b, add]
    in_specs = [a_spec, b_spec] if add is None else [a_spec, b_spec, o_spec]
    return pl.pallas_call(
        body, name=name, grid=grid, in_specs=in_specs, out_specs=o_spec, out_shape=out_shape,
        scratch_shapes=[pltpu.VMEM(acc_shape, F32)],
        compiler_params=_params("parallel", "parallel", "arbitrary"),
    )(*operands)


def _mm_rows(out_dtype, has_add):
    return MM_ROWS if (jnp.dtype(out_dtype).itemsize == 2 and not has_add) else MM_ROWS // 2


def _mm_nn(a, w, l, out_dtype, name, *, col0=0, n=None, split_out=None):
    m, kdim = a.shape
    n = w.shape[2] if n is None else n
    tm, tk = _pick(m, _mm_rows(out_dtype, False), SUBLANES), _pick(kdim, 1024, LANES)
    tn = _pick(n if split_out is None else split_out, 1024, LANES)
    jb = col0 // tn
    grid = (m // tm, n // tn, kdim // tk)
    a_spec = pl.BlockSpec((tm, tk), lambda i, j, k: (i, k))
    b_spec = pl.BlockSpec((None, tk, tn), lambda i, j, k: (l, k, j + jb))
    if split_out is None:
        o_spec = pl.BlockSpec((tm, tn), lambda i, j, k: (i, j))
        shape = jax.ShapeDtypeStruct((m, n), out_dtype)
    else:
        nj1 = split_out // tn
        o_spec = pl.BlockSpec((None, tm, tn), lambda i, j, k: (j // nj1, i, j % nj1))
        shape = jax.ShapeDtypeStruct((n // split_out, m, split_out), out_dtype)
    return _mm(a, w, dims=NN, grid=grid, a_spec=a_spec, b_spec=b_spec, o_spec=o_spec, out_shape=shape, name=name)


def _mm_nt(a, w, l, out_dtype, name, *, add=None):
    n, kdim = w.shape[1], w.shape[2]
    if a.ndim == 2:
        m = a.shape[0]
        tk = _pick(kdim, 1024, LANES)
        a_spec_of = lambda tm: pl.BlockSpec((tm, tk), lambda i, j, k: (i, k))
    else:
        m, seg = a.shape[1], a.shape[2]
        tk = _pick(seg, 1024, LANES)
        nk1 = seg // tk
        a_spec_of = lambda tm: pl.BlockSpec((None, tm, tk), lambda i, j, k: (k // nk1, i, k % nk1))
    tm, tn = _pick(m, _mm_rows(out_dtype, add is not None), SUBLANES), _pick(n, 1024, LANES)
    grid = (m // tm, n // tn, kdim // tk)
    b_spec = pl.BlockSpec((None, tn, tk), lambda i, j, k: (l, j, k))
    o_spec = pl.BlockSpec((tm, tn), lambda i, j, k: (i, j))
    return _mm(a, w, dims=NT, grid=grid, a_spec=a_spec_of(tm), b_spec=b_spec, o_spec=o_spec,
               out_shape=jax.ShapeDtypeStruct((m, n), out_dtype), name=name, add=add)


def _mm_tn(a, b, out_dtype, name):
    kdim, m = a.shape
    tk, tm = _pick(kdim, 1024, SUBLANES), _pick(m, _mm_rows(out_dtype, False), LANES)
    if b.ndim == 2:
        n = b.shape[1]
        tn = _pick(n, 1024, LANES)
        b_spec = pl.BlockSpec((tk, tn), lambda i, j, k: (k, j))
    else:
        seg = b.shape[2]
        n = b.shape[0] * seg
        tn = _pick(seg, 1024, LANES)
        nj1 = seg // tn
        b_spec = pl.BlockSpec((None, tk, tn), lambda i, j, k: (j // nj1, k, j % nj1))
    grid = (m // tm, n // tn, kdim // tk)
    a_spec = pl.BlockSpec((tk, tm), lambda i, j, k: (k, i))
    o_spec = pl.BlockSpec((tm, tn), lambda i, j, k: (i, j))
    return _mm(a, b, dims=TN, grid=grid, a_spec=a_spec, b_spec=b_spec, o_spec=o_spec,
               out_shape=jax.ShapeDtypeStruct((m, n), out_dtype), name=name)


def _rstd(x):
    return lax.rsqrt(jnp.mean(x * x, axis=-1, keepdims=True) + RMS_EPS)


def _norm_mod_fwd(x, g, shift, scale, name):
    s, d = x.shape
    tm = _pick(s, 512, SUBLANES)

    def body(x_ref, g_ref, sh_ref, sc_ref, h_ref):
        xv = x_ref[...]
        y = (xv * _rstd(xv)) * g_ref[...]
        h_ref[...] = (y * (1.0 + sc_ref[...]) + sh_ref[...]).astype(BF16)

    row = pl.BlockSpec((tm, d), lambda i: (i, 0))
    vec = pl.BlockSpec((1, d), lambda i: (0, 0))
    return pl.pallas_call(body, name=name, grid=(s // tm,), in_specs=[row, vec, vec, vec], out_specs=row,
                          out_shape=jax.ShapeDtypeStruct((s, d), BF16), compiler_params=_params("parallel"))(x, g, shift, scale)


def _norm_mod_bwd(dh, x, g, scale, dres, name):
    s, d = x.shape
    tm = _pick(s, 512, SUBLANES)

    def body(dh_ref, x_ref, g_ref, sc_ref, dres_ref, dx_ref, dsh_ref, dsc_ref, dg_ref):
        i = pl.program_id(0)
        xv, dhv, gv = x_ref[...], dh_ref[...], g_ref[...]
        r = _rstd(xv)
        xh = xv * r
        dn = dhv * (1.0 + sc_ref[...])
        gd = dn * gv
        dx_ref[...] = dres_ref[...] + r * (gd - xh * jnp.mean(gd * xh, axis=-1, keepdims=True))

        @pl.when(i == 0)
        def _():
            dsh_ref[...] = jnp.zeros_like(dsh_ref)
            dsc_ref[...] = jnp.zeros_like(dsc_ref)
            dg_ref[...] = jnp.zeros_like(dg_ref)

        dsh_ref[...] += jnp.sum(dhv, axis=0, keepdims=True)
        dsc_ref[...] += jnp.sum(dhv * (xh * gv), axis=0, keepdims=True)
        dg_ref[...] += jnp.sum(dn * xh, axis=0, keepdims=True)

    row = pl.BlockSpec((tm, d), lambda i: (i, 0))
    vec = pl.BlockSpec((1, d), lambda i: (0, 0))
    vshape = jax.ShapeDtypeStruct((1, d), F32)
    return pl.pallas_call(body, name=name, grid=(s // tm,), in_specs=[row, row, vec, vec, row],
                          out_specs=[row, vec, vec, vec],
                          out_shape=[jax.ShapeDtypeStruct((s, d), F32), vshape, vshape, vshape],
                          compiler_params=_params("arbitrary"))(dh, x, g, scale, dres)


def _post_fwd(x, u, g, gate, name):
    s, d = x.shape
    tm = _pick(s, 512, SUBLANES)

    def body(x_ref, u_ref, g_ref, gt_ref, o_ref):
        uv = u_ref[...]
        o_ref[...] = x_ref[...] + gt_ref[...] * ((uv * _rstd(uv)) * g_ref[...])

    row = pl.BlockSpec((tm, d), lambda i: (i, 0))
    vec = pl.BlockSpec((1, d), lambda i: (0, 0))
    return pl.pallas_call(body, name=name, grid=(s // tm,), in_specs=[row, row, vec, vec], out_specs=row,
                          out_shape=jax.ShapeDtypeStruct((s, d), F32), compiler_params=_params("parallel"))(x, u, g, gate)


def _post_bwd(dx, u, g, gate, name):
    s, d = u.shape
    tm = _pick(s, 512, SUBLANES)

    def body(dx_ref, u_ref, g_ref, gt_ref, du_ref, dgt_ref, dg_ref):
        i = pl.program_id(0)
        uv, dxv, gv = u_ref[...], dx_ref[...], g_ref[...]
        r = _rstd(uv)
        uh = uv * r
        dn = dxv * gt_ref[...]
        gd = dn * gv
        du_ref[...] = (r * (gd - uh * jnp.mean(gd * uh, axis=-1, keepdims=True))).astype(BF16)

        @pl.when(i == 0)
        def _():
            dgt_ref[...] = jnp.zeros_like(dgt_ref)
            dg_ref[...] = jnp.zeros_like(dg_ref)

        dgt_ref[...] += jnp.sum(dxv * (uh * gv), axis=0, keepdims=True)
        dg_ref[...] += jnp.sum(dn * uh, axis=0, keepdims=True)

    row = pl.BlockSpec((tm, d), lambda i: (i, 0))
    vec = pl.BlockSpec((1, d), lambda i: (0, 0))
    vshape = jax.ShapeDtypeStruct((1, d), F32)
    return pl.pallas_call(body, name=name, grid=(s // tm,), in_specs=[row, row, vec, vec],
                          out_specs=[row, vec, vec],
                          out_shape=[jax.ShapeDtypeStruct((s, d), BF16), vshape, vshape],
                          compiler_params=_params("arbitrary"))(dx, u, g, gate)


def _shift_rows(cur, prev8, k):
    rolled = pltpu.roll(cur, k, axis=0)
    rolled_prev = pltpu.roll(prev8, k, axis=0)
    i8 = lax.broadcasted_iota(jnp.int32, prev8.shape, 0)
    top = jnp.where(i8 < k, rolled_prev, rolled[:SUBLANES])
    return jnp.concatenate([top, rolled[SUBLANES:]], axis=0)


def _conv_pre(g, prev8, wc_ref, bc_ref):
    s1 = _shift_rows(g, prev8, 1)
    s2 = _shift_rows(g, prev8, 2)
    gc = bc_ref[...] + wc_ref[0:1, :] * s2 + wc_ref[1:2, :] * s1 + wc_ref[2:3, :] * g
    return gc, s1, s2


def _conv_act_fwd(gp, up, wc, bc, name):
    s, f = gp.shape
    tm, tf = _pick(s, 512, SUBLANES), _pick(f, 768, LANES)
    rh = tm // HALO

    def body(g_ref, gprev_ref, up_ref, wc_ref, bc_ref, a_ref):
        i = pl.program_id(1)
        prev = jnp.where(i == 0, 0.0, gprev_ref[...].astype(F32)[HALO - SUBLANES:])
        gc, _, _ = _conv_pre(g_ref[...].astype(F32), prev, wc_ref, bc_ref)
        a_ref[...] = ((gc * jax.nn.sigmoid(gc)) * up_ref[...].astype(F32)).astype(BF16)

    tile = pl.BlockSpec((tm, tf), lambda j, i: (i, j))
    prev = pl.BlockSpec((HALO, tf), lambda j, i: (jnp.maximum(i * rh - 1, 0), j))
    return pl.pallas_call(body, name=name, grid=(f // tf, s // tm),
                          in_specs=[tile, prev, tile, pl.BlockSpec((3, tf), lambda j, i: (0, j)),
                                    pl.BlockSpec((1, tf), lambda j, i: (0, j))],
                          out_specs=tile, out_shape=jax.ShapeDtypeStruct((s, f), BF16),
                          compiler_params=_params("parallel", "parallel"))(gp, gp, up, wc, bc)


def _conv_act_bwd(da, gp, up, wc, bc, name):
    s, f = gp.shape
    tm, tf = _pick(s, 512, SUBLANES), _pick(f, 768, LANES)
    rh = tm // HALO
    nrow = s // tm

    def body(da_ref, dan_ref, g_ref, gprev_ref, gn_ref, up_ref, upn_ref, wc_ref, bc_ref,
             dgp_ref, dup_ref, dwc_ref, dbc_ref):
        i = pl.program_id(1)
        last = i == nrow - 1
        head = lambda ref: ref[...].astype(F32)[:SUBLANES]
        prev = jnp.where(i == 0, 0.0, gprev_ref[...].astype(F32)[HALO - SUBLANES:])
        g_ext = jnp.concatenate([g_ref[...].astype(F32), head(gn_ref)], axis=0)
        up_ext = jnp.concatenate([up_ref[...].astype(F32), head(upn_ref)], axis=0)
        da_ext = jnp.concatenate([da_ref[...].astype(F32), jnp.where(last, 0.0, head(dan_ref))], axis=0)
        gc, s1, s2 = _conv_pre(g_ext, prev, wc_ref, bc_ref)
        sg = jax.nn.sigmoid(gc)
        dup_ref[...] = (da_ext * (gc * sg))[:tm].astype(BF16)
        dgc = da_ext * up_ext * (sg * (1.0 + gc * (1.0 - sg)))
        ext = tm + SUBLANES
        dgp = (wc_ref[2:3, :] * dgc + wc_ref[1:2, :] * pltpu.roll(dgc, ext - 1, axis=0)
               + wc_ref[0:1, :] * pltpu.roll(dgc, ext - 2, axis=0))
        dgp_ref[...] = dgp[:tm].astype(BF16)

        @pl.when(i == 0)
        def _():
            dwc_ref[...] = jnp.zeros_like(dwc_ref)
            dbc_ref[...] = jnp.zeros_like(dbc_ref)

        d0 = dgc[:tm]
        dwc_ref[0:1, :] += jnp.sum(d0 * s2[:tm], axis=0, keepdims=True)
        dwc_ref[1:2, :] += jnp.sum(d0 * s1[:tm], axis=0, keepdims=True)
        dwc_ref[2:3, :] += jnp.sum(d0 * g_ext[:tm], axis=0, keepdims=True)
        dbc_ref[...] += jnp.sum(d0, axis=0, keepdims=True)

    tile = pl.BlockSpec((tm, tf), lambda j, i: (i, j))
    prev = pl.BlockSpec((HALO, tf), lambda j, i: (jnp.maximum(i * rh - 1, 0), j))
    nxt = pl.BlockSpec((HALO, tf), lambda j, i: (jnp.minimum((i + 1) * rh, s // HALO - 1), j))
    return pl.pallas_call(body, name=name, grid=(f // tf, nrow),
                          in_specs=[tile, nxt, tile, prev, nxt, tile, nxt,
                                    pl.BlockSpec((3, tf), lambda j, i: (0, j)), pl.BlockSpec((1, tf), lambda j, i: (0, j))],
                          out_specs=[tile, tile, pl.BlockSpec((3, tf), lambda j, i: (0, j)),
                                     pl.BlockSpec((1, tf), lambda j, i: (0, j))],
                          out_shape=[jax.ShapeDtypeStruct((s, f), BF16), jax.ShapeDtypeStruct((s, f), BF16),
                                     jax.ShapeDtypeStruct((3, f), F32), jax.ShapeDtypeStruct((1, f), F32)],
                          compiler_params=_params("parallel", "arbitrary"))(da, da, gp, gp, gp, up, up, wc, bc)


def _split_bf16(v, parts):
    out, rem = [], v
    for _ in range(parts):
        t = rem.astype(BF16)
        out.append(t)
        rem = rem - t.astype(F32)
    return jnp.concatenate(out, axis=1)


def _tri(t, cmp, reps):
    row = lax.broadcasted_iota(jnp.int32, (t, t), 0)
    col = lax.broadcasted_iota(jnp.int32, (t, t), 1)
    m = cmp(row, col).astype(BF16)
    return jnp.concatenate([m] * reps, axis=0)


def _log_sigmoid(z):
    mn = jnp.minimum(z, 0.0)
    return mn - jnp.log(1.0 + jnp.exp(mn + (mn - z)))


def _log_one_minus_sigmoid(z):
    mn = jnp.minimum(z, 0.0)
    neg = mn - z
    return neg - jnp.log(1.0 + jnp.exp(mn + neg))


def _pair(xv):
    lane = lax.broadcasted_iota(jnp.int32, xv.shape, 1)
    zero = jnp.zeros_like(xv)
    return jnp.where(lane < HEAD_DIM, xv, zero), jnp.where(lane < HEAD_DIM, zero, xv)


def _attn_specs(s, tq):
    q_spec = pl.BlockSpec((None, tq, LANES), lambda hp, qi: (0, qi, hp))
    k_spec = pl.BlockSpec((None, s, LANES), lambda hp, qi: (1, 0, hp))
    v_spec = pl.BlockSpec((None, s, LANES), lambda hp, qi: (2, 0, hp))
    return q_spec, k_spec, v_spec


def _rel_mask(tq, tk, i, strict):
    row = lax.broadcasted_iota(jnp.int32, (tq, tk), 0)
    col = lax.broadcasted_iota(jnp.int32, (tq, tk), 1) + i * tk
    return col < row if strict else col <= row


def _sb_fwd(qkv, tq, tk, name):
    _, s, d = qkv.shape
    hp_n, nq, nd = d // LANES, s // tq, tq // tk
    scale = HEAD_DIM ** -0.5

    def body(q_ref, k_ref, v_ref, o_ref, lt_ref):
        qi = pl.program_id(1)
        lane = lax.broadcasted_iota(jnp.int32, (tq, LANES), 1)
        onward = _tri(tk, lambda j, sidx: j >= sidx, SUM_TERMS)
        qpair = _pair(q_ref[...] * scale)

        def block(kb, carry, mask):
            acc, runs = carry[0], carry[1:]
            ks = pl.multiple_of(kb * tk, tk)
            kblk = k_ref[pl.ds(ks, tk), :]
            vpair = _pair(v_ref[pl.ds(ks, tk), :])
            probs, new_runs = [], []
            for qh, run in zip(qpair, runs):
                z = lax.dot_general(qh, kblk, NT, preferred_element_type=F32)
                l1 = _log_one_minus_sigmoid(z)
                if mask is not None:
                    l1 = jnp.where(mask, l1, 0.0)
                c = jnp.dot(_split_bf16(l1, SUM_TERMS), onward, preferred_element_type=F32)
                a = jnp.exp(z + c + run)
                if mask is not None:
                    a = jnp.where(mask, a, 0.0)
                probs.append(a.astype(BF16))
                new_runs.append(run + jnp.sum(l1, axis=1, keepdims=True))
            acc = acc + jnp.dot(jnp.concatenate(probs, axis=1), jnp.concatenate(vpair, axis=0),
                                preferred_element_type=F32)
            return (acc, *new_runs)

        zero1 = jnp.zeros((tq, 1), F32)
        carry = (jnp.zeros((tq, LANES), F32), zero1, zero1)
        for i in reversed(range(nd)):
            carry = block(qi * nd + i, carry, _rel_mask(tq, tk, i, True))
        acc, run0, run1 = lax.fori_loop(0, qi * nd, lambda j, cr: block(qi * nd - 1 - j, cr, None), carry)
        o_ref[...] = acc.astype(BF16)
        lt_ref[...] = jnp.where(lane < HEAD_DIM, run0, run1)

    q_spec, k_spec, v_spec = _attn_specs(s, tq)
    return pl.pallas_call(
        body, name=name, grid=(hp_n, nq), in_specs=[q_spec, k_spec, v_spec],
        out_specs=[pl.BlockSpec((tq, LANES), lambda hp, qi: (qi, hp)),
                   pl.BlockSpec((None, tq, LANES), lambda hp, qi: (hp, qi, 0))],
        out_shape=[jax.ShapeDtypeStruct((s, d), BF16), jax.ShapeDtypeStruct((hp_n, s, LANES), F32)],
        compiler_params=_params("parallel", "parallel"))(qkv, qkv, qkv)


def _sb_bwd(qkv, ltot, do, tq, tk, name):
    _, s, d = qkv.shape
    hp_n, nq, nd = d // LANES, s // tq, tq // tk
    scale = HEAD_DIM ** -0.5

    def body(q_ref, k_ref, v_ref, lt_ref, do_ref, dqkv_ref, dk_acc, dv_acc):
        qi = pl.program_id(1)

        @pl.when(qi == 0)
        def _():
            dk_acc[...] = jnp.zeros_like(dk_acc)
            dv_acc[...] = jnp.zeros_like(dv_acc)

        onward = _tri(tk, lambda j, sidx: j >= sidx, SUM_TERMS)
        before = _tri(tk, lambda j, sidx: j < sidx, SUM_TERMS)
        qpair = _pair(q_ref[...] * scale)
        dopair = _pair(do_ref[...])
        ltv = lt_ref[...]
        lts = (ltv[:, 0:1], ltv[:, HEAD_DIM:HEAD_DIM + 1])

        def block(kb, carry, mask):
            dq, lpres, gpres = carry[0], carry[1:3], carry[3:5]
            ks = pl.multiple_of(kb * tk, tk)
            kblk, vblk = k_ref[pl.ds(ks, tk), :], v_ref[pl.ds(ks, tk), :]
            kpair = _pair(kblk * scale)
            dzs, probs, new_l, new_g = [], [], [], []
            for qh, doh, lt, lpre, gpre in zip(qpair, dopair, lts, lpres, gpres):
                z = lax.dot_general(qh, kblk, NT, preferred_element_type=F32)
                l1 = _log_one_minus_sigmoid(z)
                if mask is not None:
                    l1 = jnp.where(mask, l1, 0.0)
                lpre = lpre + jnp.sum(l1, axis=1, keepdims=True)
                c = jnp.dot(_split_bf16(l1, SUM_TERMS), onward, preferred_element_type=F32)
                a = jnp.exp(z + c + (lt - lpre))
                if mask is not None:
                    a = jnp.where(mask, a, 0.0)
                da = lax.dot_general(doh, vblk, NT, preferred_element_type=F32)
                g = da * a
                p = jnp.dot(_split_bf16(g, SUM_TERMS), before, preferred_element_type=F32) + gpre
                dz = g - jnp.exp(z + l1) * (g + p)
                if mask is not None:
                    dz = jnp.where(mask, dz, 0.0)
                dzs.append(dz.astype(BF16))
                probs.append(a.astype(BF16))
                new_l.append(lpre)
                new_g.append(gpre + jnp.sum(g, axis=1, keepdims=True))
            dq = dq + jnp.dot(jnp.concatenate(dzs, axis=1), jnp.concatenate(kpair, axis=0),
                              preferred_element_type=F32)
            dk_acc[pl.ds(ks, tk), :] += lax.dot_general(jnp.concatenate(dzs, axis=0), jnp.concatenate(qpair, axis=0),
                                                        TN, preferred_element_type=F32)
            dv_acc[pl.ds(ks, tk), :] += lax.dot_general(jnp.concatenate(probs, axis=0), jnp.concatenate(dopair, axis=0),
                                                        TN, preferred_element_type=F32)
            return (dq, *new_l, *new_g)

        zero1 = jnp.zeros((tq, 1), F32)
        init = (jnp.zeros((tq, LANES), F32), zero1, zero1, zero1, zero1)
        carry = lax.fori_loop(0, qi * nd, lambda j, cr: block(j, cr, None), init)
        for i in range(nd):
            carry = block(qi * nd + i, carry, _rel_mask(tq, tk, i, True))
        dqkv_ref[0, pl.ds(pl.multiple_of(qi * tq, tq), tq), :] = carry[0].astype(BF16)

        @pl.when(qi == nq - 1)
        def _():
            dqkv_ref[1] = dk_acc[...].astype(BF16)
            dqkv_ref[2] = dv_acc[...].astype(BF16)

    q_spec, k_spec, v_spec = _attn_specs(s, tq)
    return pl.pallas_call(
        body, name=name, grid=(hp_n, nq),
        in_specs=[q_spec, k_spec, v_spec, pl.BlockSpec((None, tq, LANES), lambda hp, qi: (hp, qi, 0)),
                  pl.BlockSpec((tq, LANES), lambda hp, qi: (qi, hp))],
        out_specs=pl.BlockSpec((3, s, LANES), lambda hp, qi: (0, 0, hp)),
        out_shape=jax.ShapeDtypeStruct((3, s, d), BF16),
        scratch_shapes=[pltpu.VMEM((s, LANES), F32), pltpu.VMEM((s, LANES), F32)],
        compiler_params=_params("parallel", "arbitrary"))(qkv, qkv, qkv, ltot, do)


def _fox_prep(fl, bias, name):
    s, w = fl.shape
    tb = _pick(s, 512, SUBLANES)

    def body(fl_ref, b_ref, cum_ref, carry_ref):
        i = pl.program_id(0)

        @pl.when(i == 0)
        def _():
            carry_ref[...] = jnp.zeros_like(carry_ref)

        logf = _log_sigmoid(fl_ref[...] + b_ref[...])
        row = lax.broadcasted_iota(jnp.int32, (tb, tb), 0)
        col = lax.broadcasted_iota(jnp.int32, (tb, tb), 1)
        incl = (col <= row).astype(BF16)
        tot = carry_ref[...]
        rem = logf
        for _ in range(3):
            part = rem.astype(BF16)
            tot = tot + jnp.dot(incl, part, preferred_element_type=F32)
            rem = rem - part.astype(F32)
        cum_ref[...] = tot
        carry_ref[...] = tot[tb - 1:tb, :]

    blk = pl.BlockSpec((tb, w), lambda i: (i, 0))
    return pl.pallas_call(body, name=name, grid=(s // tb,), in_specs=[blk, pl.BlockSpec((1, w), lambda i: (0, 0))],
                          out_specs=blk, out_shape=jax.ShapeDtypeStruct((s, w), F32),
                          scratch_shapes=[pltpu.VMEM((1, w), F32)], compiler_params=_params("arbitrary"))(fl, bias)


def _fox_gate_bwd(dcum, fl, bias, n_heads, name):
    s, w = fl.shape
    tb = _pick(s, 512, SUBLANES)
    nb = s // tb

    def body(dc_ref, fl_ref, b_ref, dfl_ref, db_ref, carry_ref):
        i = pl.program_id(0)

        @pl.when(i == 0)
        def _():
            carry_ref[...] = jnp.zeros_like(carry_ref)
            db_ref[...] = jnp.zeros_like(db_ref)

        row = lax.broadcasted_iota(jnp.int32, (tb, tb), 0)
        col = lax.broadcasted_iota(jnp.int32, (tb, tb), 1)
        incl = (col >= row).astype(BF16)
        tot = jnp.broadcast_to(carry_ref[...], (tb, w))
        rem = dc_ref[...]
        for _ in range(3):
            part = rem.astype(BF16)
            tot = tot + jnp.dot(incl, part, preferred_element_type=F32)
            rem = rem - part.astype(F32)
        carry_ref[...] = tot[0:1, :]
        xg = fl_ref[...] + b_ref[...]
        e = jnp.exp(-jnp.abs(xg))
        sig_neg = jnp.where(xg >= 0.0, e, 1.0) / (1.0 + e)
        lane = lax.broadcasted_iota(jnp.int32, (tb, w), 1)
        dfl = jnp.where(lane < n_heads, tot * sig_neg, 0.0)
        dfl_ref[...] = dfl.astype(BF16)
        db_ref[...] += jnp.sum(dfl, axis=0, keepdims=True)

    blk = pl.BlockSpec((tb, w), lambda i: (nb - 1 - i, 0))
    vec = pl.BlockSpec((1, w), lambda i: (0, 0))
    return pl.pallas_call(body, name=name, grid=(nb,), in_specs=[blk, blk, vec], out_specs=[blk, vec],
                          out_shape=[jax.ShapeDtypeStruct((s, w), BF16), jax.ShapeDtypeStruct((1, w), F32)],
                          scratch_shapes=[pltpu.VMEM((1, w), F32)], compiler_params=_params("arbitrary"))(dcum, fl, bias)


def _head_columns(cum_blk, hp):
    lane = lax.broadcasted_iota(jnp.int32, cum_blk.shape, 1)
    return tuple(jnp.sum(jnp.where(lane == 2 * hp + h, cum_blk, 0.0), axis=1, keepdims=True) for h in range(2))


def _fox_fwd(qkv, cum, ck, tq, tk, name):
    _, s, d = qkv.shape
    hp_n, nq, nd = d // LANES, s // tq, tq // tk
    scale = HEAD_DIM ** -0.5

    def body(q_ref, k_ref, v_ref, cq_ref, ck_ref, o_ref, lse_ref):
        qi = pl.program_id(1)
        low = lax.broadcasted_iota(jnp.int32, (tq, LANES), 1) < HEAD_DIM
        qpair = _pair(q_ref[...] * scale)
        cqs = _head_columns(cq_ref[...], pl.program_id(0))

        def block(kb, carry, mask):
            acc, stats = carry[0], carry[1:]
            ks = pl.multiple_of(kb * tk, tk)
            kblk = k_ref[pl.ds(ks, tk), :]
            vpair = _pair(v_ref[pl.ds(ks, tk), :])
            probs, alphas, new = [], [], []
            for h, (qh, cqh) in enumerate(zip(qpair, cqs)):
                m, lsum = stats[2 * h], stats[2 * h + 1]
                z = lax.dot_general(qh, kblk, NT, preferred_element_type=F32)
                sc = z + (cqh - ck_ref[h:h + 1, pl.ds(ks, tk)])
                if mask is not None:
                    sc = jnp.where(mask, sc, -jnp.inf)
                m_new = jnp.maximum(m, jnp.max(sc, axis=1, keepdims=True))
                alpha = jnp.exp(m - m_new)
                p = jnp.exp(sc - m_new)
                new += [m_new, alpha * lsum + jnp.sum(p, axis=1, keepdims=True)]
                probs.append(p.astype(BF16))
                alphas.append(alpha)
            acc = jnp.where(low, alphas[0], alphas[1]) * acc + jnp.dot(
                jnp.concatenate(probs, axis=1), jnp.concatenate(vpair, axis=0), preferred_element_type=F32)
            return (acc, *new)

        neg = jnp.full((tq, 1), -jnp.inf, F32)
        zero1 = jnp.zeros((tq, 1), F32)
        carry = (jnp.zeros((tq, LANES), F32), neg, zero1, neg, zero1)
        for i in range(nd):
            carry = block(qi * nd + i, carry, _rel_mask(tq, tk, i, False))
        acc, m0, l0, m1, l1 = lax.fori_loop(0, qi * nd, lambda j, cr: block(qi * nd - 1 - j, cr, None), carry)
        o_ref[...] = (acc / jnp.where(low, l0, l1)).astype(BF16)
        lse_ref[...] = jnp.where(low, m0 + jnp.log(l0), m1 + jnp.log(l1))

    q_spec, k_spec, v_spec = _attn_specs(s, tq)
    pair_rows = pl.BlockSpec((None, tq, LANES), lambda hp, qi: (hp, qi, 0))
    return pl.pallas_call(
        body, name=name, grid=(hp_n, nq),
        in_specs=[q_spec, k_spec, v_spec, pl.BlockSpec((tq, LANES), lambda hp, qi: (qi, 0)),
                  pl.BlockSpec((None, 2, s), lambda hp, qi: (hp, 0, 0))],
        out_specs=[pl.BlockSpec((tq, LANES), lambda hp, qi: (qi, hp)), pair_rows],
        out_shape=[jax.ShapeDtypeStruct((s, d), BF16), jax.ShapeDtypeStruct((hp_n, s, LANES), F32)],
        compiler_params=_params("parallel", "parallel"))(qkv, qkv, qkv, cum, ck)


def _fox_bwd(qkv, o, do, lse, cum, ck, tq, tk, name):
    _, s, d = qkv.shape
    hp_n, nq, nd = d // LANES, s // tq, tq // tk
    scale = HEAD_DIM ** -0.5

    def body(q_ref, k_ref, v_ref, o_ref, do_ref, lse_ref, cq_ref, ck_ref, dqkv_ref, dcq_ref, dck_ref, dk_acc, dv_acc):
        qi = pl.program_id(1)

        @pl.when(qi == 0)
        def _():
            dk_acc[...] = jnp.zeros_like(dk_acc)
            dv_acc[...] = jnp.zeros_like(dv_acc)
            dck_ref[...] = jnp.zeros_like(dck_ref)

        low = lax.broadcasted_iota(jnp.int32, (tq, LANES), 1) < HEAD_DIM
        dov, lsev = do_ref[...], lse_ref[...]
        qpair = _pair(q_ref[...] * scale)
        dopair = _pair(dov)
        prod = dov.astype(F32) * o_ref[...].astype(F32)
        deltas = (jnp.sum(jnp.where(low, prod, 0.0), axis=1, keepdims=True),
                  jnp.sum(jnp.where(low, 0.0, prod), axis=1, keepdims=True))
        cqs = _head_columns(cq_ref[...], pl.program_id(0))
        lses = (lsev[:, 0:1], lsev[:, HEAD_DIM:HEAD_DIM + 1])

        def block(kb, carry, mask):
            dq, rowsums = carry[0], carry[1:]
            ks = pl.multiple_of(kb * tk, tk)
            kblk, vblk = k_ref[pl.ds(ks, tk), :], v_ref[pl.ds(ks, tk), :]
            kpair = _pair(kblk * scale)
            dss, probs, new_rows = [], [], []
            for h, (qh, doh) in enumerate(zip(qpair, dopair)):
                z = lax.dot_general(qh, kblk, NT, preferred_element_type=F32)
                sc = z + (cqs[h] - ck_ref[h:h + 1, pl.ds(ks, tk)])
                p = jnp.exp(sc - lses[h])
                if mask is not None:
                    p = jnp.where(mask, p, 0.0)
                dp = lax.dot_general(doh, vblk, NT, preferred_element_type=F32)
                ds = p * (dp - deltas[h])
                dck_ref[h:h + 1, pl.ds(ks, tk)] -= jnp.sum(ds, axis=0, keepdims=True)
                new_rows.append(rowsums[h] + jnp.sum(ds, axis=1, keepdims=True))
                dss.append(ds.astype(BF16))
                probs.append(p.astype(BF16))
            dq = dq + jnp.dot(jnp.concatenate(dss, axis=1), jnp.concatenate(kpair, axis=0),
                              preferred_element_type=F32)
            dk_acc[pl.ds(ks, tk), :] += lax.dot_general(jnp.concatenate(dss, axis=0), jnp.concatenate(qpair, axis=0),
                                                       TN, preferred_element_type=F32)
            dv_acc[pl.ds(ks, tk), :] += lax.dot_general(jnp.concatenate(probs, axis=0), jnp.concatenate(dopair, axis=0),
                                                       TN, preferred_element_type=F32)
            return (dq, *new_rows)

        zero1 = jnp.zeros((tq, 1), F32)
        carry = lax.fori_loop(0, qi * nd, lambda j, cr: block(j, cr, None), (jnp.zeros((tq, LANES), F32), zero1, zero1))
        for i in range(nd):
            carry = block(qi * nd + i, carry, _rel_mask(tq, tk, i, False))
        dq, rs0, rs1 = carry
        dqkv_ref[0, pl.ds(pl.multiple_of(qi * tq, tq), tq), :] = dq.astype(BF16)
        dcq_ref[...] = jnp.where(low, rs0, rs1)

        @pl.when(qi == nq - 1)
        def _():
            dqkv_ref[1] = dk_acc[...].astype(BF16)
            dqkv_ref[2] = dv_acc[...].astype(BF16)

    q_spec, k_spec, v_spec = _attn_specs(s, tq)
    pair_rows = pl.BlockSpec((None, tq, LANES), lambda hp, qi: (hp, qi, 0))
    tile = pl.BlockSpec((tq, LANES), lambda hp, qi: (qi, hp))
    keys = pl.BlockSpec((None, 2, s), lambda hp, qi: (hp, 0, 0))
    return pl.pallas_call(
        body, name=name, grid=(hp_n, nq),
        in_specs=[q_spec, k_spec, v_spec, tile, tile, pair_rows, pl.BlockSpec((tq, LANES), lambda hp, qi: (qi, 0)), keys],
        out_specs=[pl.BlockSpec((3, s, LANES), lambda hp, qi: (0, 0, hp)), pair_rows, keys],
        out_shape=[jax.ShapeDtypeStruct((3, s, d), BF16), jax.ShapeDtypeStruct((hp_n, s, LANES), F32),
                   jax.ShapeDtypeStruct((hp_n, 2, s), F32)],
        scratch_shapes=[pltpu.VMEM((s, LANES), F32), pltpu.VMEM((s, LANES), F32)],
        compiler_params=_params("parallel", "arbitrary"))(qkv, qkv, qkv, o, do, lse, cum, ck)


def _loss_head(y, target, name):
    s, d = y.shape
    tm = _pick(s, 512, SUBLANES)

    def body(y_ref, t_ref, dy_ref, sq_ref):
        i = pl.program_id(0)
        diff = y_ref[...] - t_ref[...]
        dy_ref[...] = diff / d

        @pl.when(i == 0)
        def _():
            sq_ref[...] = jnp.zeros_like(sq_ref)

        sq_ref[...] += jnp.sum(diff * diff, axis=0, keepdims=True)

    row = pl.BlockSpec((tm, d), lambda i: (i, 0))
    vec = pl.BlockSpec((1, d), lambda i: (0, 0))
    return pl.pallas_call(body, name=name, grid=(s // tm,), in_specs=[row, row], out_specs=[row, vec],
                          out_shape=[jax.ShapeDtypeStruct((s, d), F32), jax.ShapeDtypeStruct((1, d), F32)],
                          compiler_params=_params("arbitrary"))(y, target)


def _mod_fwd(c_all, w_mod, b_mod_cols, name):
    nl, d, cols = w_mod.shape
    nb = c_all.shape[0]

    def body(c_ref, w_ref, b_ref, o_ref):
        cv = c_ref[...]
        act = (cv * jax.nn.sigmoid(cv)).astype(BF16)
        o_ref[...] = jnp.dot(act, w_ref[...].astype(BF16), preferred_element_type=F32) + b_ref[...]

    return pl.pallas_call(
        body, name=name, grid=(nl,),
        in_specs=[pl.BlockSpec((nb, d), lambda l: (0, 0)), pl.BlockSpec((None, d, cols), lambda l: (l, 0, 0)),
                  pl.BlockSpec((None, 1, cols), lambda l: (l, 0, 0))],
        out_specs=pl.BlockSpec((None, nb, cols), lambda l: (l, 0, 0)),
        out_shape=jax.ShapeDtypeStruct((nl, nb, cols), F32), compiler_params=_params("parallel"))(c_all, w_mod, b_mod_cols)


def _wmod_grad(c_all_t, dmod, name):
    d, nb = c_all_t.shape
    _, nl, cols = dmod.shape

    def body(c_ref, dm_ref, o_ref):
        cv = c_ref[...]
        act = cv * jax.nn.sigmoid(cv)
        tot = act[:, 0:1] * dm_ref[0]
        for b in range(1, nb):
            tot = tot + act[:, b:b + 1] * dm_ref[b]
        o_ref[...] = tot

    return pl.pallas_call(
        body, name=name, grid=(nl,),
        in_specs=[pl.BlockSpec((d, nb), lambda l: (0, 0)), pl.BlockSpec((nb, None, 1, cols), lambda l: (0, l, 0, 0))],
        out_specs=pl.BlockSpec((None, d, cols), lambda l: (l, 0, 0)),
        out_shape=jax.ShapeDtypeStruct((nl, d, cols), F32), compiler_params=_params("parallel"))(
            c_all_t, dmod.reshape(nb, nl, 1, cols))


def _adamw(recv, w, m, v, name, *, tr=256, tc=512):
    nq, nl, rp, cp = recv.shape
    _, r, c = w.shape
    br = _pick(r, tr, SUBLANES) if rp == r else r
    bc = _pick(c, tc, LANES) if cp == c else c
    rbr = br if rp == r else rp
    rbc = bc if cp == c else cp

    def body(rv_ref, w_ref, m_ref, v_ref, g_ref, d_ref, nm_ref, nv_ref):
        g = rv_ref[0, :br, :bc].astype(F32)
        for qd in range(1, nq):
            g = g + rv_ref[qd, :br, :bc].astype(F32)
        m_new = ADAM_B1 * m_ref[...] + (1.0 - ADAM_B1) * g
        v_new = ADAM_B2 * v_ref[...] + (1.0 - ADAM_B2) * jnp.square(g)
        m_hat = m_new / (1.0 - ADAM_B1 ** ADAM_STEP)
        v_hat = v_new / (1.0 - ADAM_B2 ** ADAM_STEP)
        g_ref[...] = g
        d_ref[...] = -ADAM_LR * (m_hat / (jnp.sqrt(v_hat) + ADAM_EPS) + ADAM_WD * w_ref[...])
        nm_ref[...] = m_new
        nv_ref[...] = v_new

    blk = pl.BlockSpec((None, br, bc), lambda l, i, j: (l, i, j))
    rblk = pl.BlockSpec((nq, None, rbr, rbc), lambda l, i, j: (0, l, i, j))
    shape = jax.ShapeDtypeStruct(w.shape, F32)
    return pl.pallas_call(body, name=name, grid=(nl, r // br, c // bc), in_specs=[rblk, blk, blk, blk],
                          out_specs=[blk] * 4, out_shape=[shape] * 4,
                          compiler_params=_params("parallel", "parallel", "parallel"))(recv, w, m, v)


def _mesh_pos():
    return lax.axis_index("x"), lax.axis_index("y"), lax.axis_index("c")


def _allgather_small(block, name):
    m_per, n = block.shape

    def body(x_ref, out_ref, send_sems, recv_sems, local_sem):
        x, y, c = _mesh_pos()
        me, sibling = (x, y, c), (x, y, 1 - c)
        chips = [(1 - x, y), (x, 1 - y), (1 - x, 1 - y)]

        def rows(px, py, pc):
            return out_ref.at[pl.ds((4 * px + 2 * py + pc) * m_per, m_per), :]

        def copy(k, blk, to, src=None):
            return pltpu.make_async_remote_copy(
                src_ref=rows(*blk) if src is None else src, dst_ref=rows(*blk),
                send_sem=send_sems.at[k], recv_sem=recv_sems.at[k], device_id=to, device_id_type=MESH)

        mine = pltpu.make_async_copy(x_ref, rows(*me), local_sem)
        mine.start()
        first = [copy(0, me, sibling, src=x_ref)]
        first += [copy(1 + j, me, (*chip, c), src=x_ref) for j, chip in enumerate(chips)]
        for cp in first:
            cp.start()
        passed = [copy(4 + j, (*chip, c), sibling) for j, chip in enumerate(chips)]
        for j, chip in enumerate(chips):
            copy(1 + j, (*chip, c), me).wait_recv()
            passed[j].start()
        copy(0, sibling, me).wait_recv()
        for j, chip in enumerate(chips):
            copy(4 + j, (*chip, 1 - c), me).wait_recv()
        for cp in first + passed:
            cp.wait_send()
        mine.wait()

    return pl.pallas_call(
        body, name=name, out_shape=jax.ShapeDtypeStruct((N_DEV * m_per, n), block.dtype),
        in_specs=[pl.BlockSpec(memory_space=pltpu.VMEM)], out_specs=pl.BlockSpec(memory_space=pltpu.VMEM),
        scratch_shapes=[pltpu.SemaphoreType.DMA((7,)), pltpu.SemaphoreType.DMA((7,)), pltpu.SemaphoreType.DMA],
        compiler_params=pltpu.CompilerParams(vmem_limit_bytes=VMEM_LIMIT_BYTES))(block)


def _exchange(srcs, dst_shapes, jobs, name):
    n_src, n_job = len(srcs), len(jobs)

    def body(*refs):
        src_refs, dst_refs = refs[:n_src], refs[n_src:n_src + len(dst_shapes)]
        send_sems, recv_sems, local_sems = refs[n_src + len(dst_shapes):]
        x, y, c = _mesh_pos()
        me = 4 * x + 2 * y + c
        pending = []
        for t, (si, di, src_slice, dst_slice) in enumerate(jobs):
            src, dst = src_refs[si], dst_refs[di]
            lc = pltpu.make_async_copy(src_slice(src, me), dst_slice(dst, me), local_sems.at[t])
            lc.start()
            pending.append(lc)
            for dd in range(1, N_DEV):
                px = 1 - x if dd & 4 else x
                py = 1 - y if dd & 2 else y
                pc = 1 - c if dd & 1 else c
                cp = pltpu.make_async_remote_copy(
                    src_ref=src_slice(src, 4 * px + 2 * py + pc), dst_ref=dst_slice(dst, me),
                    send_sem=send_sems.at[t, dd - 1], recv_sem=recv_sems.at[t, dd - 1],
                    device_id=(px, py, pc), device_id_type=MESH)
                cp.start()
                pending.append(cp)
        for cp in pending:
            cp.wait()

    hbm = pl.BlockSpec(memory_space=pl.ANY)
    return pl.pallas_call(
        body, name=name, out_shape=list(dst_shapes), in_specs=[hbm] * n_src, out_specs=[hbm] * len(dst_shapes),
        scratch_shapes=[pltpu.SemaphoreType.DMA((n_job, N_DEV - 1)), pltpu.SemaphoreType.DMA((n_job, N_DEV - 1)),
                        pltpu.SemaphoreType.DMA((n_job,))])(*srcs)


def _peer(x, y, c, dd):
    return (1 - x if dd & 4 else x, 1 - y if dd & 2 else y, 1 - c if dd & 1 else c)


def _exchange_copies(jobs, src_refs, dst_refs, send_sems, recv_sems, sending, only=None):
    x, y, c = _mesh_pos()
    me = 4 * x + 2 * y + c
    local, remote = [], []
    for t, (si, di, src_slice, dst_slice) in enumerate(jobs):
        if only is not None and t not in only:
            continue
        local.append(pltpu.make_async_copy(src_slice(src_refs[si], me), dst_slice(dst_refs[di], me),
                                           send_sems.at[t * N_DEV]))
        for dd in range(1, N_DEV):
            px, py, pc = _peer(x, y, c, dd)
            p = 4 * px + 2 * py + pc
            remote.append(pltpu.make_async_remote_copy(
                src_ref=src_slice(src_refs[si], p), dst_ref=dst_slice(dst_refs[di], me if sending else p),
                send_sem=send_sems.at[t * N_DEV + dd], recv_sem=recv_sems.at[t * N_DEV + dd],
                device_id=(px, py, pc), device_id_type=MESH))
    return local, remote


def _exchange_start(srcs, dsts, jobs, name, after=None):
    n_src, n_dst, n_job = len(srcs), len(dsts), len(jobs)
    n_in = n_src + n_dst + (after is not None)

    def body(*refs):
        src_refs, dst_refs = refs[:n_src], refs[n_src:n_src + n_dst]
        send_sems, recv_sems = refs[n_in], refs[n_in + 1]
        local, remote = _exchange_copies(jobs, src_refs, dst_refs, send_sems, recv_sems, True)
        for cp in remote + local:
            cp.start()
        refs[-1][...] = jnp.zeros((SUBLANES, LANES), F32)

    hbm = pl.BlockSpec(memory_space=pltpu.HBM)
    sem = pl.BlockSpec(memory_space=pltpu.SEMAPHORE)
    operands = [pltpu.with_memory_space_constraint(a, pltpu.HBM) for a in (*srcs, *dsts)]
    extra_specs = [] if after is None else [pl.BlockSpec(memory_space=pl.ANY)]
    extra = [] if after is None else [after]
    res = pl.pallas_call(
        body, name=name, in_specs=[hbm] * (n_src + n_dst) + extra_specs,
        out_specs=(sem, sem, *[hbm] * (n_src + n_dst), pl.BlockSpec(memory_space=pltpu.VMEM)),
        out_shape=(pltpu.SemaphoreType.DMA((n_job * N_DEV,)), pltpu.SemaphoreType.DMA((n_job * N_DEV,)),
                   *[pltpu.HBM(a.shape, a.dtype) for a in (*srcs, *dsts)],
                   jax.ShapeDtypeStruct((SUBLANES, LANES), F32)),
        input_output_aliases={i: 2 + i for i in range(n_src + n_dst)},
        compiler_params=pltpu.CompilerParams(has_side_effects=pltpu.SideEffectType.DATAFLOW_SIDE_EFFECTING))(
            *operands, *extra)
    return res[0], res[1], list(res[2:2 + n_src]), list(res[2 + n_src:2 + n_src + n_dst]), res[-1]


def _exchange_wait(parts, dsts, after, name):
    n_dst = len(dsts)
    n_src = sum(len(p[2]) for p in parts)

    def body(*refs):
        dst_refs = refs[n_src:n_src + n_dst]
        sem_refs = refs[n_src + n_dst:n_src + n_dst + 2 * len(parts)]
        at = 0
        for k, (_, _, part_srcs, jobs, only) in enumerate(parts):
            src_refs = refs[at:at + len(part_srcs)]
            at += len(part_srcs)
            local, remote = _exchange_copies(jobs, src_refs, dst_refs, sem_refs[2 * k], sem_refs[2 * k + 1], False, only)
            for cp in local:
                cp.wait()
            for cp in remote:
                cp.wait_send()
                cp.wait_recv()

    hbm = pl.BlockSpec(memory_space=pltpu.HBM)
    sem = pl.BlockSpec(memory_space=pltpu.SEMAPHORE)
    srcs = [a for p in parts for a in p[2]]
    sems = [s for p in parts for s in (p[0], p[1])]
    res = pl.pallas_call(
        body, name=name, in_specs=[hbm] * (n_src + n_dst) + [sem] * len(sems) + [pl.BlockSpec(memory_space=pl.ANY)],
        out_specs=[hbm] * (n_src + n_dst),
        out_shape=[pltpu.HBM(a.shape, a.dtype) for a in (*srcs, *dsts)],
        input_output_aliases={i: i for i in range(n_src + n_dst)},
        compiler_params=pltpu.CompilerParams(has_side_effects=pltpu.SideEffectType.DATAFLOW_SIDE_EFFECTING))(
            *srcs, *dsts, *sems, after)
    return list(res[:n_src]), list(res[n_src:])


def _whole(ref, p):
    return ref


def _layer_of(layer, inner):
    return lambda ref, p: inner(ref.at[layer], p)


def _cols_of(width):
    def take(ref, p):
        lead = (slice(None),) * (len(ref.shape) - 1)
        return ref.at[lead + (pl.ds(pl.multiple_of(p * width, LANES), width),)]
    return take


def _rows_of(height):
    def take(ref, p):
        lead = (slice(None),) * (len(ref.shape) - 2)
        return ref.at[lead + (pl.ds(pl.multiple_of(p * height, SUBLANES), height), slice(None))]
    return take


def _slot(layer=None):
    if layer is None:
        return lambda ref, p: ref.at[p]
    return lambda ref, p: ref.at[p, layer]


def _local_step(x0, target, mod, g_mix_pre, g_mix_post, g_ffn_pre, g_ffn_post, wqkv, wo, wg, wu, wd,
                wfg, bfg, wconv, bconv, tiles, fetch=None, on_grads=None, zero=0.0):
    s, d = x0.shape
    nl = mod.shape[0]
    n_heads = d // HEAD_DIM
    hp_n = d // LANES
    vec = lambda a: a.reshape(1, -1)
    saved = []
    xcur = x0
    for l in range(nl):
        if fetch is not None:
            wqkv = fetch(l, 0, xcur)[0]
        sh_a, sc_a, gt_a, sh_f, sc_f, gt_f = (vec(mod[l, i * d:(i + 1) * d]) for i in range(6))
        fox = l % 2 == 1
        g1 = vec(g_mix_pre[l])
        if l == 0 and fetch is not None:
            g1 = g1 + zero
        h1 = _norm_mod_fwd(xcur, g1, sh_a, sc_a, f"norm_mix_fwd_{l}")
        qkv = _mm_nn(h1, wqkv, l, BF16, f"qkv_fwd_{l}", split_out=d)
        if fox:
            j = l // 2
            fl = _mm_nn(h1, wfg, j, F32, f"fgate_fwd_{l}")
            cum = _fox_prep(fl, bfg[j], f"fox_prep_{l}")
            ck = cum[:, :n_heads].T.reshape(hp_n, 2, s)
            o, stat = _fox_fwd(qkv, cum, ck, *tiles["fox_fwd"], f"fox_fwd_{l}")
            extra = (fl, cum, ck)
        else:
            o, stat = _sb_fwd(qkv, *tiles["sb_fwd"], f"sb_fwd_{l}")
            extra = None
        if fetch is not None:
            wqkv, wo, wg, wu, wd = fetch(l, 1, o)
        u = _mm_nn(o, wo, l, F32, f"attn_out_fwd_{l}")
        x2 = _post_fwd(xcur, u, vec(g_mix_post[l]), gt_a, f"post_mix_fwd_{l}")
        h2 = _norm_mod_fwd(x2, vec(g_ffn_pre[l]), sh_f, sc_f, f"norm_ffn_fwd_{l}")
        gp = _mm_nn(h2, wg, l, BF16, f"ffn_gate_fwd_{l}")
        up = _mm_nn(h2, wu, l, BF16, f"ffn_up_fwd_{l}")
        act = _conv_act_fwd(gp, up, wconv[l], bconv[l], f"conv_act_fwd_{l}")
        yv = _mm_nn(act, wd, l, F32, f"ffn_down_fwd_{l}")
        x3 = _post_fwd(x2, yv, vec(g_ffn_post[l]), gt_f, f"post_ffn_fwd_{l}")
        saved.append((xcur, h1, qkv, o, stat, extra, u, x2, h2, gp, up, act, yv))
        xcur = x3

    dx, sq = _loss_head(xcur, target, "loss_head")
    small, big = [None] * nl, [None] * nl
    nudge = None
    for l in reversed(range(nl)):
        xin, h1, qkv, o, stat, extra, u, x2, h2, gp, up, act, yv = saved[l]
        sc_a, gt_a, sc_f, gt_f = (vec(mod[l, i * d:(i + 1) * d]) for i in (1, 2, 4, 5))
        fox = l % 2 == 1
        g4 = vec(g_ffn_post[l])
        if nudge is not None:
            g4 = g4 + nudge
        dy, dgt_f, dg4 = _post_bwd(dx, yv, g4, gt_f, f"post_ffn_bwd_{l}")
        dact = _mm_nt(dy, wd, l, BF16, f"ffn_down_dx_{l}")
        dwd = _mm_tn(act, dy, BF16, f"ffn_down_dw_{l}")
        dgp, dup, dwc, dbc = _conv_act_bwd(dact, gp, up, wconv[l], bconv[l], f"conv_act_bwd_{l}")
        dh2 = _mm_nt(dgp, wg, l, F32, f"ffn_gate_dx_{l}")
        dh2 = _mm_nt(dup, wu, l, F32, f"ffn_up_dx_{l}", add=dh2)
        dwg = _mm_tn(h2, dgp, BF16, f"ffn_gate_dw_{l}")
        dwu = _mm_tn(h2, dup, BF16, f"ffn_up_dw_{l}")
        g3 = vec(g_ffn_pre[l])
        if on_grads is not None:
            g3 = g3 + on_grads(l, dict(gate=dwg, up=dwu, down=dwd))
        dx2, dsh_f, dsc_f, dg3 = _norm_mod_bwd(dh2, x2, g3, sc_f, dx, f"norm_ffn_bwd_{l}")
        du, dgt_a, dg2 = _post_bwd(dx2, u, vec(g_mix_post[l]), gt_a, f"post_mix_bwd_{l}")
        do = _mm_nt(du, wo, l, BF16, f"attn_out_dx_{l}")
        dwo = _mm_tn(o, du, BF16, f"attn_out_dw_{l}")
        if fox:
            j = l // 2
            fl, cum, ck = extra
            dqkv, dcq, dck = _fox_bwd(qkv, o, do, stat, cum, ck, *tiles["fox_bwd"], f"fox_bwd_{l}")
            dcq = jnp.max(dcq.reshape(hp_n, s, 2, HEAD_DIM), axis=3)
            dcum = dcq.transpose(1, 0, 2).reshape(s, n_heads) + dck.reshape(n_heads, s).T
            dcum = jnp.pad(dcum, ((0, 0), (0, LANES - n_heads)))
            dfl, dbfg = _fox_gate_bwd(dcum, fl, bfg[j], n_heads, f"fox_gate_bwd_{l}")
            dh1 = _mm_nt(dfl, wfg, j, F32, f"fgate_dx_{l}")
            dh1 = _mm_nt(dqkv, wqkv, l, F32, f"qkv_dx_{l}", add=dh1)
            dwfg = _mm_tn(h1, dfl, F32, f"fgate_dw_{l}")[:, :n_heads]
            dbfg = dbfg[0, :n_heads]
        else:
            dqkv = _sb_bwd(qkv, stat, do, *tiles["sb_bwd"], f"sb_bwd_{l}")
            dh1 = _mm_nt(dqkv, wqkv, l, F32, f"qkv_dx_{l}")
            dwfg = dbfg = None
        dwqkv = _mm_tn(h1, dqkv, BF16, f"qkv_dw_{l}")
        dx, dsh_a, dsc_a, dg1 = _norm_mod_bwd(dh1, xin, vec(g_mix_pre[l]), sc_a, dx2, f"norm_mix_bwd_{l}")
        dmod = jnp.concatenate([dsh_a, dsc_a, dgt_a, dsh_f, dsc_f, dgt_f], axis=1)[0]
        small[l] = dict(dmod=dmod, dg1=dg1[0], dg2=dg2[0], dg3=dg3[0], dg4=dg4[0], dbc=dbc[0], dwc=dwc,
                        dbfg=dbfg, dwfg=dwfg)
        big[l] = dict(qkv=dwqkv, o=dwo, gate=dwg, up=dwu, down=dwd)
        if on_grads is not None:
            nudge = on_grads(l, dict(qkv=dwqkv, o=dwo))
    return sq, dx, small, big


def _rows128(a, rows):
    flat = a.reshape(-1)
    return jnp.pad(flat, (0, rows * LANES - flat.shape[0])).reshape(rows, LANES)


def _ceil8(n_elems):
    rows = -(-n_elems // LANES)
    return -(-rows // SUBLANES) * SUBLANES


def kernel(x, c, w_mod, b_mod, g_mix_pre, g_mix_post, w_qkv, w_o, w_fg, b_fg, g_ffn_pre, g_ffn_post, w_ffn_gate, w_ffn_up, w_conv, b_conv, w_ffn_down, loss_target, m_w_mod, m_b_mod, m_g_mix_pre, m_g_mix_post, m_w_qkv, m_w_o, m_w_fg, m_b_fg, m_g_ffn_pre, m_g_ffn_post, m_w_ffn_gate, m_w_ffn_up, m_w_conv, m_b_conv, m_w_ffn_down, v_w_mod, v_b_mod, v_g_mix_pre, v_g_mix_post, v_w_qkv, v_w_o, v_w_fg, v_b_fg, v_g_ffn_pre, v_g_ffn_post, v_w_ffn_gate, v_w_ffn_up, v_w_conv, v_b_conv, v_w_ffn_down):
    _, s, d = x.shape
    nl = w_qkv.shape[0]
    nf = w_fg.shape[0]
    n_heads = w_fg.shape[2]
    fs = w_ffn_gate.shape[2]
    fp = -(-fs // LANES) * LANES
    f_full, f_pad = N_DEV * fs, N_DEV * fp
    mod_cols = w_mod.shape[2]
    qs, orows = w_qkv.shape[2], w_o.shape[1]
    xi, yi, ci = _mesh_pos()
    me = 4 * xi + 2 * yi + ci

    c_rows = d // LANES
    c_all = _allgather_small(jnp.pad(c.reshape(1, d), ((0, SUBLANES - 1), (0, 0))).reshape(SUBLANES * c_rows, LANES),
                             "gather_cond")
    c_all = c_all.reshape(N_DEV, SUBLANES, d)[:, 0, :]
    b_mod_cols = lax.dynamic_slice_in_dim(b_mod, me * mod_cols, mod_cols, axis=1).reshape(nl, 1, mod_cols)
    mod_part = _mod_fwd(c_all, w_mod, b_mod_cols, "mod_fwd")

    conv_pad = jnp.pad(w_conv, ((0, 0), (0, 0), (0, fp - fs)))
    r_mod, r_conv, r_fg = _ceil8(mod_part.size), _ceil8(conv_pad.size), _ceil8(w_fg.size)
    payload = jnp.concatenate([_rows128(mod_part, r_mod), _rows128(conv_pad, r_conv), _rows128(w_fg, r_fg)], axis=0)
    got = _allgather_small(payload, "gather_small_weights").reshape(N_DEV, r_mod + r_conv + r_fg, LANES)
    mod_g = got[:, :r_mod].reshape(N_DEV, -1)[:, :mod_part.size].reshape(N_DEV, nl, N_DEV, mod_cols)
    mod = lax.dynamic_index_in_dim(mod_g, me, axis=2, keepdims=False).transpose(1, 0, 2).reshape(nl, N_DEV * mod_cols)
    conv_g = got[:, r_mod:r_mod + r_conv].reshape(N_DEV, -1)[:, :conv_pad.size].reshape(N_DEV, nl, 3, fp)
    wconv_full = conv_g.transpose(1, 2, 0, 3).reshape(nl, 3, f_pad)
    fg_g = got[:, r_mod + r_conv:].reshape(N_DEV, -1)[:, :w_fg.size].reshape(N_DEV, nf, orows, n_heads)
    wfg_full = fg_g.transpose(1, 0, 2, 3).reshape(nf, d, n_heads)
    wfg_full = jnp.pad(wfg_full, ((0, 0), (0, 0), (0, LANES - n_heads))).astype(BF16)
    bfg_full = jnp.pad(b_fg, ((0, 0), (0, LANES - n_heads))).reshape(nf, 1, LANES)
    bconv_full = jnp.pad(b_conv.reshape(nl, N_DEV, fs), ((0, 0), (0, 0), (0, fp - fs))).reshape(nl, 1, f_pad)

    gate_sh = jnp.pad(w_ffn_gate, ((0, 0), (0, 0), (0, fp - fs))).astype(BF16)
    up_sh = jnp.pad(w_ffn_up, ((0, 0), (0, 0), (0, fp - fs))).astype(BF16)
    down_sh = jnp.pad(w_ffn_down, ((0, 0), (0, fp - fs), (0, 0))).astype(BF16)
    shards = [w_qkv.astype(BF16), w_o.astype(BF16), gate_sh, up_sh, down_sh]
    full_shapes = [jax.ShapeDtypeStruct((nl, d, N_DEV * qs), BF16), jax.ShapeDtypeStruct((nl, d, d), BF16),
                   jax.ShapeDtypeStruct((nl, d, f_pad), BF16), jax.ShapeDtypeStruct((nl, d, f_pad), BF16),
                   jax.ShapeDtypeStruct((nl, f_pad, d), BF16)]
    place = [_cols_of(qs), _rows_of(orows), _cols_of(fp), _cols_of(fp), _rows_of(fp)]
    full = [lax.empty(sh.shape, sh.dtype) for sh in full_shapes]
    gathers, zero = [], 0.0
    for l in range(nl):
        jobs_l = [(i, i, _layer_of(l, _whole), _layer_of(l, place[i])) for i in range(5)]
        send_s, recv_s, shards, full, token = _exchange_start(shards, full, jobs_l, f"gather_start_{l}",
                                                              after=got if l == 0 else None)
        gathers.append((send_s, recv_s, jobs_l))
        zero = zero + token[0, 0]
    state = {"shards": shards, "full": full}

    def fetch(l, part, after):
        send_s, recv_s, jobs_l = gathers[l]
        only = (0,) if part == 0 else (1, 2, 3, 4)
        state["shards"], state["full"] = _exchange_wait([(send_s, recv_s, state["shards"], jobs_l, only)],
                                                        state["full"], after, f"gather_wait_{l}_{part}")
        return state["full"]

    order = ["qkv", "o", "gate", "up", "down"]
    send = {"qkv": _cols_of(qs), "o": _rows_of(orows), "gate": _cols_of(fp), "up": _cols_of(fp), "down": _rows_of(fp)}
    recv_shapes = [(N_DEV, nl, d, qs), (N_DEV, nl, orows, d), (N_DEV, nl, d, fp), (N_DEV, nl, d, fp), (N_DEV, nl, fp, d)]
    scatters = []
    landing = {"recv": [lax.empty(sh, BF16) for sh in recv_shapes]}

    def on_grads(l, grads):
        names = [nm for nm in order if nm in grads]
        jobs_l = [(k, order.index(nm), send[nm], _slot(l)) for k, nm in enumerate(names)]
        send_s, recv_s, srcs, landing["recv"], token = _exchange_start(
            [grads[nm] for nm in names], landing["recv"], jobs_l, f"scatter_start_{l}_{names[0]}")
        scatters.append((send_s, recv_s, srcs, jobs_l, None))
        return token[0, 0]

    sq, dx, small, big = _local_step(x[0], loss_target[0], mod, g_mix_pre, g_mix_post, g_ffn_pre, g_ffn_post,
                                     None, None, None, None, None, wfg_full, bfg_full, wconv_full, bconv_full,
                                     ATTN_TILES, fetch=fetch, on_grads=on_grads, zero=zero)
    loss = lax.psum(0.5 * jnp.sum(sq) / d, ("x", "y", "c"))
    _, recv = _exchange_wait(scatters, landing["recv"], dx, "scatter_wait")
    upd = {}
    for nm, rv, wt, mt, vt in zip(order, recv, [w_qkv, w_o, w_ffn_gate, w_ffn_up, w_ffn_down],
                                  [m_w_qkv, m_w_o, m_w_ffn_gate, m_w_ffn_up, m_w_ffn_down],
                                  [v_w_qkv, v_w_o, v_w_ffn_gate, v_w_ffn_up, v_w_ffn_down]):
        upd[nm] = _adamw(rv, wt, mt, vt, f"adamw_{nm}")

    stack = lambda key: jnp.stack([small[l][key] for l in range(nl)])
    dmod = stack("dmod")
    dgs = [stack(k) for k in ("dg1", "dg2", "dg3", "dg4")]
    dbc = stack("dbc").reshape(nl, N_DEV, fp)[:, :, :fs].reshape(nl, f_full)
    dbfg = jnp.stack([small[l]["dbfg"] for l in range(nl) if l % 2 == 1])
    dwc = stack("dwc").reshape(nl, 3, N_DEV, fp)[:, :, :, :fs].reshape(nl, 3, f_full)
    dwfg = jnp.stack([small[l]["dwfg"] for l in range(nl) if l % 2 == 1])
    rep_parts = [dmod] + dgs + [dbc, dbfg]
    rep_rows = [_ceil8(p.size) for p in rep_parts]
    r_rep, r_wc, r_wfg = sum(rep_rows), _ceil8(dwc.size), _ceil8(dwfg.size)
    payload = jnp.concatenate([_rows128(p, r) for p, r in zip(rep_parts, rep_rows)]
                              + [_rows128(dwc, r_wc), _rows128(dwfg, r_wfg)], axis=0)
    gsm = _allgather_small(payload, "gather_small_grads").reshape(N_DEV, 1, r_rep + r_wc + r_wfg, LANES)

    def pack(parts):
        return jnp.concatenate([_rows128(p, r) for p, r in zip(parts, rep_rows)], axis=0).reshape(1, r_rep, LANES)

    rep_w = [b_mod, g_mix_pre, g_mix_post, g_ffn_pre, g_ffn_post, b_conv, b_fg]
    rep_m = [m_b_mod, m_g_mix_pre, m_g_mix_post, m_g_ffn_pre, m_g_ffn_post, m_b_conv, m_b_fg]
    rep_v = [v_b_mod, v_g_mix_pre, v_g_mix_post, v_g_ffn_pre, v_g_ffn_post, v_b_conv, v_b_fg]
    rep_out = _adamw(gsm[:, :, :r_rep], pack(rep_w), pack(rep_m), pack(rep_v), "adamw_replicated", tr=r_rep, tc=LANES)

    def unpack(packed):
        outs, at = [], 0
        for p, r in zip(rep_w, rep_rows):
            outs.append(packed[0, at:at + r].reshape(-1)[:p.size].reshape(p.shape))
            at += r
        return outs

    rep_g, rep_d, rep_nm, rep_nv = (unpack(a) for a in rep_out)

    wc_all = gsm[:, 0, r_rep:r_rep + r_wc].reshape(N_DEV, -1)[:, :dwc.size].reshape(N_DEV, 1, nl * 3, f_full)
    wc_mine = lax.dynamic_slice_in_dim(wc_all, me * fs, fs, axis=3)
    wc_out = _adamw(wc_mine, w_conv.reshape(1, nl * 3, fs), m_w_conv.reshape(1, nl * 3, fs),
                    v_w_conv.reshape(1, nl * 3, fs), "adamw_conv", tr=nl * 3, tc=fs)
    wc_out = [a.reshape(nl, 3, fs) for a in wc_out]
    wfg_all = gsm[:, 0, r_rep + r_wc:].reshape(N_DEV, -1)[:, :dwfg.size].reshape(N_DEV, nf, d, n_heads)
    wfg_mine = lax.dynamic_slice_in_dim(wfg_all, me * orows, orows, axis=2)
    wfg_out = _adamw(wfg_mine, w_fg, m_w_fg, v_w_fg, "adamw_fgate", tr=orows, tc=n_heads)

    dmod_all = gsm[:, 0, :rep_rows[0]].reshape(N_DEV, -1)[:, :dmod.size].reshape(N_DEV, nl, N_DEV * mod_cols)
    dmod_mine = lax.dynamic_slice_in_dim(dmod_all, me * mod_cols, mod_cols, axis=2)
    gwmod = _wmod_grad(c_all.T, dmod_mine, "wmod_grad")
    wmod_out = _adamw(gwmod.reshape(1, nl, d, mod_cols), w_mod, m_w_mod, v_w_mod, "adamw_mod")

    per_weight = [wmod_out, None, None, None, upd["qkv"], upd["o"], wfg_out, None, None, None,
                  upd["gate"], upd["up"], wc_out, None, upd["down"]]
    rep_index = {1: 0, 2: 1, 3: 2, 8: 3, 9: 4, 13: 5, 7: 6}
    outs = [[], [], [], []]
    for pos, res in enumerate(per_weight):
        for kind in range(4):
            if res is None:
                outs[kind].append((rep_g, rep_d, rep_nm, rep_nv)[kind][rep_index[pos]])
            else:
                outs[kind].append(res[kind])
    return (loss, dx.reshape(1, s, d), *outs[0], *outs[1], *outs[2], *outs[3])
```

```python
import functools

import jax
import jax.numpy as jnp
from jax import lax
from jax.experimental import pallas as pl
from jax.experimental.pallas import tpu as pltpu

F32 = jnp.float32
BF16 = jnp.bfloat16

N_DEV = 8
HEAD_DIM = 64
LANES = 128
SUBLANES = 8
RMS_EPS = 1e-6
ADAM_LR = 0.001
ADAM_B1 = 0.9
ADAM_B2 = 0.999
ADAM_EPS = 1e-08
ADAM_WD = 0.01
ADAM_STEP = 10
VMEM_LIMIT_BYTES = 56 * 1024 * 1024
EXP_UNDERFLOW = -110.0
SUM_TERMS = 1
HALO = 16
MM_ROWS = 2048
ATTN_TILES = {"sb_fwd": (1024, 256), "sb_bwd": (1024, 256), "fox_fwd": (1024, 256), "fox_bwd": (1024, 256)}

NN = (((1,), (0,)), ((), ()))
NT = (((1,), (1,)), ((), ()))
TN = (((0,), (0,)), ((), ()))
MESH = pl.DeviceIdType.MESH


def _params(*sem):
    return pltpu.CompilerParams(dimension_semantics=sem, vmem_limit_bytes=VMEM_LIMIT_BYTES)


def _pick(n, pref, quantum):
    if n <= pref:
        return n
    t = (pref // quantum) * quantum
    while n % t:
        t -= quantum
    return t


def _mm(a, b, *, dims, grid, a_spec, b_spec, o_spec, out_shape, name, add=None):
    nk = grid[2]
    acc_shape = tuple(d for d in o_spec.block_shape if d is not None)
    o_dtype = out_shape.dtype

    def body(*refs):
        if add is None:
            a_ref, b_ref, o_ref, acc_ref = refs
            add_ref = None
        else:
            a_ref, b_ref, add_ref, o_ref, acc_ref = refs
        k = pl.program_id(2)
        part = lax.dot_general(a_ref[...], b_ref[...], dims, preferred_element_type=F32)

        def finish(total):
            if add_ref is not None:
                total = total + add_ref[...]
            o_ref[...] = total.astype(o_dtype)

        if nk == 1:
            finish(part)
            return

        @pl.when(k == 0)
        def _():
            acc_ref[...] = part

        @pl.when(jnp.logical_and(k > 0, k < nk - 1))
        def _():
            acc_ref[...] += part

        @pl.when(k == nk - 1)
        def _():
            finish(acc_ref[...] + part)

    operands = [a, b] if add is None else [a, b, add]
    in_specs = [a_spec, b_spec] if add is None else [a_spec, b_spec, o_spec]
    return pl.pallas_call(
        body, name=name, grid=grid, in_specs=in_specs, out_specs=o_spec, out_shape=out_shape,
        scratch_shapes=[pltpu.VMEM(acc_shape, F32)],
        compiler_params=_params("parallel", "parallel", "arbitrary"),
    )(*operands)


def _mm_rows(out_dtype, has_add):
    return MM_ROWS if (jnp.dtype(out_dtype).itemsize == 2 and not has_add) else MM_ROWS // 2


def _mm_nn(a, w, l, out_dtype, name, *, col0=0, n=None, split_out=None):
    m, kdim = a.shape
    n = w.shape[2] if n is None else n
    tm, tk = _pick(m, _mm_rows(out_dtype, False), SUBLANES), _pick(kdim, 1024, LANES)
    tn = _pick(n if split_out is None else split_out, 1024, LANES)
    jb = col0 // tn
    grid = (m // tm, n // tn, kdim // tk)
    a_spec = pl.BlockSpec((tm, tk), lambda i, j, k: (i, k))
    b_spec = pl.BlockSpec((None, tk, tn), lambda i, j, k: (l, k, j + jb))
    if split_out is None:
        o_spec = pl.BlockSpec((tm, tn), lambda i, j, k: (i, j))
        shape = jax.ShapeDtypeStruct((m, n), out_dtype)
    else:
        nj1 = split_out // tn
        o_spec = pl.BlockSpec((None, tm, tn), lambda i, j, k: (j // nj1, i, j % nj1))
        shape = jax.ShapeDtypeStruct((n // split_out, m, split_out), out_dtype)
    return _mm(a, w, dims=NN, grid=grid, a_spec=a_spec, b_spec=b_spec, o_spec=o_spec, out_shape=shape, name=name)


def _mm_nt(a, w, l, out_dtype, name, *, add=None):
    n, kdim = w.shape[1], w.shape[2]
    if a.ndim == 2:
        m = a.shape[0]
        tk = _pick(kdim, 1024, LANES)
        a_spec_of = lambda tm: pl.BlockSpec((tm, tk), lambda i, j, k: (i, k))
    else:
        m, seg = a.shape[1], a.shape[2]
        tk = _pick(seg, 1024, LANES)
        nk1 = seg // tk
        a_spec_of = lambda tm: pl.BlockSpec((None, tm, tk), lambda i, j, k: (k // nk1, i, k % nk1))
    tm, tn = _pick(m, _mm_rows(out_dtype, add is not None), SUBLANES), _pick(n, 1024, LANES)
    grid = (m // tm, n // tn, kdim // tk)
    b_spec = pl.BlockSpec((None, tn, tk), lambda i, j, k: (l, j, k))
    o_spec = pl.BlockSpec((tm, tn), lambda i, j, k: (i, j))
    return _mm(a, w, dims=NT, grid=grid, a_spec=a_spec_of(tm), b_spec=b_spec, o_spec=o_spec,
               out_shape=jax.ShapeDtypeStruct((m, n), out_dtype), name=name, add=add)


def _mm_tn(a, b, out_dtype, name):
    kdim, m = a.shape
    tk, tm = _pick(kdim, 1024, SUBLANES), _pick(m, _mm_rows(out_dtype, False), LANES)
    if b.ndim == 2:
        n = b.shape[1]
        tn = _pick(n, 1024, LANES)
        b_spec = pl.BlockSpec((tk, tn), lambda i, j, k: (k, j))
    else:
        seg = b.shape[2]
        n = b.shape[0] * seg
        tn = _pick(seg, 1024, LANES)
        nj1 = seg // tn
        b_spec = pl.BlockSpec((None, tk, tn), lambda i, j, k: (j // nj1, k, j % nj1))
    grid = (m // tm, n // tn, kdim // tk)
    a_spec = pl.BlockSpec((tk, tm), lambda i, j, k: (k, i))
    o_spec = pl.BlockSpec((tm, tn), lambda i, j, k: (i, j))
    return _mm(a, b, dims=TN, grid=grid, a_spec=a_spec, b_spec=b_spec, o_spec=o_spec,
               out_shape=jax.ShapeDtypeStruct((m, n), out_dtype), name=name)


def _rstd(x):
    return lax.rsqrt(jnp.mean(x * x, axis=-1, keepdims=True) + RMS_EPS)


def _norm_mod_fwd(x, g, shift, scale, name):
    s, d = x.shape
    tm = _pick(s, 512, SUBLANES)

    def body(x_ref, g_ref, sh_ref, sc_ref, h_ref):
        xv = x_ref[...]
        y = (xv * _rstd(xv)) * g_ref[...]
        h_ref[...] = (y * (1.0 + sc_ref[...]) + sh_ref[...]).astype(BF16)

    row = pl.BlockSpec((tm, d), lambda i: (i, 0))
    vec = pl.BlockSpec((1, d), lambda i: (0, 0))
    return pl.pallas_call(body, name=name, grid=(s // tm,), in_specs=[row, vec, vec, vec], out_specs=row,
                          out_shape=jax.ShapeDtypeStruct((s, d), BF16), compiler_params=_params("parallel"))(x, g, shift, scale)


def _norm_mod_bwd(dh, x, g, scale, dres, name):
    s, d = x.shape
    tm = _pick(s, 512, SUBLANES)

    def body(dh_ref, x_ref, g_ref, sc_ref, dres_ref, dx_ref, dsh_ref, dsc_ref, dg_ref):
        i = pl.program_id(0)
        xv, dhv, gv = x_ref[...], dh_ref[...], g_ref[...]
        r = _rstd(xv)
        xh = xv * r
        dn = dhv * (1.0 + sc_ref[...])
        gd = dn * gv
        dx_ref[...] = dres_ref[...] + r * (gd - xh * jnp.mean(gd * xh, axis=-1, keepdims=True))

        @pl.when(i == 0)
        def _():
            dsh_ref[...] = jnp.zeros_like(dsh_ref)
            dsc_ref[...] = jnp.zeros_like(dsc_ref)
            dg_ref[...] = jnp.zeros_like(dg_ref)

        dsh_ref[...] += jnp.sum(dhv, axis=0, keepdims=True)
        dsc_ref[...] += jnp.sum(dhv * (xh * gv), axis=0, keepdims=True)
        dg_ref[...] += jnp.sum(dn * xh, axis=0, keepdims=True)

    row = pl.BlockSpec((tm, d), lambda i: (i, 0))
    vec = pl.BlockSpec((1, d), lambda i: (0, 0))
    vshape = jax.ShapeDtypeStruct((1, d), F32)
    return pl.pallas_call(body, name=name, grid=(s // tm,), in_specs=[row, row, vec, vec, row],
                          out_specs=[row, vec, vec, vec],
                          out_shape=[jax.ShapeDtypeStruct((s, d), F32), vshape, vshape, vshape],
                          compiler_params=_params("arbitrary"))(dh, x, g, scale, dres)


def _post_fwd(x, u, g, gate, name):
    s, d = x.shape
    tm = _pick(s, 512, SUBLANES)

    def body(x_ref, u_ref, g_ref, gt_ref, o_ref):
        uv = u_ref[...]
        o_ref[...] = x_ref[...] + gt_ref[...] * ((uv * _rstd(uv)) * g_ref[...])

    row = pl.BlockSpec((tm, d), lambda i: (i, 0))
    vec = pl.BlockSpec((1, d), lambda i: (0, 0))
    return pl.pallas_call(body, name=name, grid=(s // tm,), in_specs=[row, row, vec, vec], out_specs=row,
                          out_shape=jax.ShapeDtypeStruct((s, d), F32), compiler_params=_params("parallel"))(x, u, g, gate)


def _post_bwd(dx, u, g, gate, name):
    s, d = u.shape
    tm = _pick(s, 512, SUBLANES)

    def body(dx_ref, u_ref, g_ref, gt_ref, du_ref, dgt_ref, dg_ref):
        i = pl.program_id(0)
        uv, dxv, gv = u_ref[...], dx_ref[...], g_ref[...]
        r = _rstd(uv)
        uh = uv * r
        dn = dxv * gt_ref[...]
        gd = dn * gv
        du_ref[...] = (r * (gd - uh * jnp.mean(gd * uh, axis=-1, keepdims=True))).astype(BF16)

        @pl.when(i == 0)
        def _():
            dgt_ref[...] = jnp.zeros_like(dgt_ref)
            dg_ref[...] = jnp.zeros_like(dg_ref)

        dgt_ref[...] += jnp.sum(dxv * (uh * gv), axis=0, keepdims=True)
        dg_ref[...] += jnp.sum(dn * uh, axis=0, keepdims=True)

    row = pl.BlockSpec((tm, d), lambda i: (i, 0))
    vec = pl.BlockSpec((1, d), lambda i: (0, 0))
    vshape = jax.ShapeDtypeStruct((1, d), F32)
    return pl.pallas_call(body, name=name, grid=(s // tm,), in_specs=[row, row, vec, vec],
                          out_specs=[row, vec, vec],
                          out_shape=[jax.ShapeDtypeStruct((s, d), BF16), vshape, vshape],
                          compiler_params=_params("arbitrary"))(dx, u, g, gate)


def _shift_rows(cur, prev8, k):
    rolled = pltpu.roll(cur, k, axis=0)
    rolled_prev = pltpu.roll(prev8, k, axis=0)
    i8 = lax.broadcasted_iota(jnp.int32, prev8.shape, 0)
    top = jnp.where(i8 < k, rolled_prev, rolled[:SUBLANES])
    return jnp.concatenate([top, rolled[SUBLANES:]], axis=0)


def _conv_pre(g, prev8, wc_ref, bc_ref):
    s1 = _shift_rows(g, prev8, 1)
    s2 = _shift_rows(g, prev8, 2)
    gc = bc_ref[...] + wc_ref[0:1, :] * s2 + wc_ref[1:2, :] * s1 + wc_ref[2:3, :] * g
    return gc, s1, s2


def _conv_act_fwd(gp, up, wc, bc, name):
    s, f = gp.shape
    tm, tf = _pick(s, 512, SUBLANES), _pick(f, 768, LANES)
    rh = tm // HALO

    def body(g_ref, gprev_ref, up_ref, wc_ref, bc_ref, a_ref):
        i = pl.program_id(1)
        prev = jnp.where(i == 0, 0.0, gprev_ref[...].astype(F32)[HALO - SUBLANES:])
        gc, _, _ = _conv_pre(g_ref[...].astype(F32), prev, wc_ref, bc_ref)
        a_ref[...] = ((gc * jax.nn.sigmoid(gc)) * up_ref[...].astype(F32)).astype(BF16)

    tile = pl.BlockSpec((tm, tf), lambda j, i: (i, j))
    prev = pl.BlockSpec((HALO, tf), lambda j, i: (jnp.maximum(i * rh - 1, 0), j))
    return pl.pallas_call(body, name=name, grid=(f // tf, s // tm),
                          in_specs=[tile, prev, tile, pl.BlockSpec((3, tf), lambda j, i: (0, j)),
                                    pl.BlockSpec((1, tf), lambda j, i: (0, j))],
                          out_specs=tile, out_shape=jax.ShapeDtypeStruct((s, f), BF16),
                          compiler_params=_params("parallel", "parallel"))(gp, gp, up, wc, bc)


def _conv_act_bwd(da, gp, up, wc, bc, name):
    s, f = gp.shape
    tm, tf = _pick(s, 512, SUBLANES), _pick(f, 768, LANES)
    rh = tm // HALO
    nrow = s // tm

    def body(da_ref, dan_ref, g_ref, gprev_ref, gn_ref, up_ref, upn_ref, wc_ref, bc_ref,
             dgp_ref, dup_ref, dwc_ref, dbc_ref):
        i = pl.program_id(1)
        last = i == nrow - 1
        head = lambda ref: ref[...].astype(F32)[:SUBLANES]
        prev = jnp.where(i == 0, 0.0, gprev_ref[...].astype(F32)[HALO - SUBLANES:])
        g_ext = jnp.concatenate([g_ref[...].astype(F32), head(gn_ref)], axis=0)
        up_ext = jnp.concatenate([up_ref[...].astype(F32), head(upn_ref)], axis=0)
        da_ext = jnp.concatenate([da_ref[...].astype(F32), jnp.where(last, 0.0, head(dan_ref))], axis=0)
        gc, s1, s2 = _conv_pre(g_ext, prev, wc_ref, bc_ref)
        sg = jax.nn.sigmoid(gc)
        dup_ref[...] = (da_ext * (gc * sg))[:tm].astype(BF16)
        dgc = da_ext * up_ext * (sg * (1.0 + gc * (1.0 - sg)))
        ext = tm + SUBLANES
        dgp = (wc_ref[2:3, :] * dgc + wc_ref[1:2, :] * pltpu.roll(dgc, ext - 1, axis=0)
               + wc_ref[0:1, :] * pltpu.roll(dgc, ext - 2, axis=0))
        dgp_ref[...] = dgp[:tm].astype(BF16)

        @pl.when(i == 0)
        def _():
            dwc_ref[...] = jnp.zeros_like(dwc_ref)
            dbc_ref[...] = jnp.zeros_like(dbc_ref)

        d0 = dgc[:tm]
        dwc_ref[0:1, :] += jnp.sum(d0 * s2[:tm], axis=0, keepdims=True)
        dwc_ref[1:2, :] += jnp.sum(d0 * s1[:tm], axis=0, keepdims=True)
        dwc_ref[2:3, :] += jnp.sum(d0 * g_ext[:tm], axis=0, keepdims=True)
        dbc_ref[...] += jnp.sum(d0, axis=0, keepdims=True)

    tile = pl.BlockSpec((tm, tf), lambda j, i: (i, j))
    prev = pl.BlockSpec((HALO, tf), lambda j, i: (jnp.maximum(i * rh - 1, 0), j))
    nxt = pl.BlockSpec((HALO, tf), lambda j, i: (jnp.minimum((i + 1) * rh, s // HALO - 1), j))
    return pl.pallas_call(body, name=name, grid=(f // tf, nrow),
                          in_specs=[tile, nxt, tile, prev, nxt, tile, nxt,
                                    pl.BlockSpec((3, tf), lambda j, i: (0, j)), pl.BlockSpec((1, tf), lambda j, i: (0, j))],
                          out_specs=[tile, tile, pl.BlockSpec((3, tf), lambda j, i: (0, j)),
                                     pl.BlockSpec((1, tf), lambda j, i: (0, j))],
                          out_shape=[jax.ShapeDtypeStruct((s, f), BF16), jax.ShapeDtypeStruct((s, f), BF16),
                                     jax.ShapeDtypeStruct((3, f), F32), jax.ShapeDtypeStruct((1, f), F32)],
                          compiler_params=_params("parallel", "arbitrary"))(da, da, gp, gp, gp, up, up, wc, bc)


def _split_bf16(v, parts):
    out, rem = [], v
    for _ in range(parts):
        t = rem.astype(BF16)
        out.append(t)
        rem = rem - t.astype(F32)
    return jnp.concatenate(out, axis=1)


def _tri(t, cmp, reps):
    row = lax.broadcasted_iota(jnp.int32, (t, t), 0)
    col = lax.broadcasted_iota(jnp.int32, (t, t), 1)
    m = cmp(row, col).astype(BF16)
    return jnp.concatenate([m] * reps, axis=0)


def _log_sigmoid(z):
    mn = jnp.minimum(z, 0.0)
    return mn - jnp.log(1.0 + jnp.exp(mn + (mn - z)))


def _log_one_minus_sigmoid(z):
    mn = jnp.minimum(z, 0.0)
    neg = mn - z
    return neg - jnp.log(1.0 + jnp.exp(mn + neg))


def _pair(xv):
    lane = lax.broadcasted_iota(jnp.int32, xv.shape, 1)
    zero = jnp.zeros_like(xv)
    return jnp.where(lane < HEAD_DIM, xv, zero), jnp.where(lane < HEAD_DIM, zero, xv)


def _attn_specs(s, tq):
    q_spec = pl.BlockSpec((None, tq, LANES), lambda hp, qi: (0, qi, hp))
    k_spec = pl.BlockSpec((None, s, LANES), lambda hp, qi: (1, 0, hp))
    v_spec = pl.BlockSpec((None, s, LANES), lambda hp, qi: (2, 0, hp))
    return q_spec, k_spec, v_spec


def _rel_mask(tq, tk, i, strict):
    row = lax.broadcasted_iota(jnp.int32, (tq, tk), 0)
    col = lax.broadcasted_iota(jnp.int32, (tq, tk), 1) + i * tk
    return col < row if strict else col <= row


def _key_bound(qkv, name):
    _, s, d = qkv.shape

    def body(k_ref, o_ref):
        sq = jnp.square(k_ref[...].astype(F32))
        low = lax.broadcasted_iota(jnp.int32, sq.shape, 1) < HEAD_DIM
        n0 = jnp.max(jnp.sum(jnp.where(low, sq, 0.0), axis=1, keepdims=True))
        n1 = jnp.max(jnp.sum(jnp.where(low, 0.0, sq), axis=1, keepdims=True))
        low8 = lax.broadcasted_iota(jnp.int32, (SUBLANES, LANES), 1) < HEAD_DIM
        o_ref[...] = jnp.sqrt(jnp.where(low8, n0, n1))

    return pl.pallas_call(body, name=name, grid=(d // LANES,),
                          in_specs=[pl.BlockSpec((None, s, LANES), lambda hp: (1, 0, hp))],
                          out_specs=pl.BlockSpec((None, SUBLANES, LANES), lambda hp: (hp, 0, 0)),
                          out_shape=jax.ShapeDtypeStruct((d // LANES, SUBLANES, LANES), F32),
                          compiler_params=_params("parallel"))(qkv)


def _sb_fwd(qkv, kbound, tq, tk, name, rider=None):
    _, s, d = qkv.shape
    hp_n, nq, nd = d // LANES, s // tq, tq // tk
    scale = HEAD_DIM ** -0.5

    def body(q_ref, k_ref, v_ref, kb_ref, o_ref, lt_ref, nb_ref):
        qi = pl.program_id(1)
        lane = lax.broadcasted_iota(jnp.int32, (tq, LANES), 1)
        onward = _tri(tk, lambda j, sidx: j >= sidx, SUM_TERMS)
        qpair = _pair(q_ref[...] * scale)
        kbv = kb_ref[...]
        z_max = [jnp.sqrt(jnp.sum(jnp.square(qh.astype(F32)), axis=1, keepdims=True)) * (1.01 * km)
                 for qh, km in zip(qpair, (kbv[0:1, 0:1], kbv[0:1, HEAD_DIM:HEAD_DIM + 1]))]

        def alive(runs):
            return jnp.logical_or(jnp.max(runs[0] + z_max[0]) >= EXP_UNDERFLOW,
                                  jnp.max(runs[1] + z_max[1]) >= EXP_UNDERFLOW)

        def block(kb, carry, mask):
            acc, runs = carry[0], carry[1:]
            ks = pl.multiple_of(kb * tk, tk)
            kblk = k_ref[pl.ds(ks, tk), :]
            vpair = _pair(v_ref[pl.ds(ks, tk), :])
            probs, new_runs = [], []
            for qh, run in zip(qpair, runs):
                z = lax.dot_general(qh, kblk, NT, preferred_element_type=F32)
                l1 = _log_one_minus_sigmoid(z)
                if mask is not None:
                    l1 = jnp.where(mask, l1, 0.0)
                c = jnp.dot(_split_bf16(l1, SUM_TERMS), onward, preferred_element_type=F32)
                a = jnp.exp(z + c + run)
                if mask is not None:
                    a = jnp.where(mask, a, 0.0)
                probs.append(a.astype(BF16))
                new_runs.append(run + jnp.sum(l1, axis=1, keepdims=True))
            acc = acc + jnp.dot(jnp.concatenate(probs, axis=1), jnp.concatenate(vpair, axis=0),
                                preferred_element_type=F32)
            return (acc, *new_runs)

        zero1 = jnp.zeros((tq, 1), F32)
        carry = (jnp.zeros((tq, LANES), F32), zero1, zero1)
        for i in reversed(range(nd)):
            carry = block(qi * nd + i, carry, _rel_mask(tq, tk, i, True))

        def step(state):
            new = block(qi * nd - 1 - state[0], state[2:], None)
            return (state[0] + 1, alive(new[1:]), *new)

        done, _, acc, run0, run1 = lax.while_loop(lambda st: jnp.logical_and(st[0] < qi * nd, st[1]), step,
                                                  (jnp.int32(0), alive(carry[1:]), *carry))
        o_ref[...] = acc.astype(BF16)
        lt_ref[...] = jnp.where(lane < HEAD_DIM, run0, run1)
        nb_ref[...] = jnp.zeros((SUBLANES, LANES), F32) + done.astype(F32)

    q_spec, k_spec, v_spec = _attn_specs(s, tq)
    return _attn_call(
        body, name, (hp_n, nq),
        [q_spec, k_spec, v_spec, pl.BlockSpec((None, SUBLANES, LANES), lambda hp, qi: (hp, 0, 0))],
        [pl.BlockSpec((tq, LANES), lambda hp, qi: (qi, hp)),
         pl.BlockSpec((None, tq, LANES), lambda hp, qi: (hp, qi, 0)),
         pl.BlockSpec((None, None, SUBLANES, LANES), lambda hp, qi: (hp, qi, 0, 0))],
        [jax.ShapeDtypeStruct((s, d), BF16), jax.ShapeDtypeStruct((hp_n, s, LANES), F32),
         jax.ShapeDtypeStruct((hp_n, nq, SUBLANES, LANES), F32)],
        [], ("parallel", "parallel"), (qkv, qkv, qkv, kbound), rider)


def _sb_bwd(qkv, ltot, visited, do, tq, tk, name, rider=None):
    _, s, d = qkv.shape
    hp_n, nq, nd = d // LANES, s // tq, tq // tk
    scale = HEAD_DIM ** -0.5

    def body(q_ref, k_ref, v_ref, lt_ref, nb_ref, do_ref, dqkv_ref, dk_acc, dv_acc):
        qi = pl.program_id(1)
        first = qi * nd - jnp.max(nb_ref[...]).astype(jnp.int32)

        @pl.when(qi == 0)
        def _():
            dk_acc[...] = jnp.zeros_like(dk_acc)
            dv_acc[...] = jnp.zeros_like(dv_acc)

        onward = _tri(tk, lambda j, sidx: j >= sidx, SUM_TERMS)
        before = _tri(tk, lambda j, sidx: j < sidx, SUM_TERMS)
        qpair = _pair(q_ref[...] * scale)
        dopair = _pair(do_ref[...])
        ltv = lt_ref[...]
        lts = (ltv[:, 0:1], ltv[:, HEAD_DIM:HEAD_DIM + 1])

        def block(kb, carry, mask):
            dq, lpres, gpres = carry[0], carry[1:3], carry[3:5]
            ks = pl.multiple_of(kb * tk, tk)
            kblk, vblk = k_ref[pl.ds(ks, tk), :], v_ref[pl.ds(ks, tk), :]
            kpair = _pair(kblk * scale)
            dzs, probs, new_l, new_g = [], [], [], []
            for qh, doh, lt, lpre, gpre in zip(qpair, dopair, lts, lpres, gpres):
                z = lax.dot_general(qh, kblk, NT, preferred_element_type=F32)
                l1 = _log_one_minus_sigmoid(z)
                if mask is not None:
                    l1 = jnp.where(mask, l1, 0.0)
                lpre = lpre + jnp.sum(l1, axis=1, keepdims=True)
                c = jnp.dot(_split_bf16(l1, SUM_TERMS), onward, preferred_element_type=F32)
                a = jnp.exp(z + c + (lt - lpre))
                if mask is not None:
                    a = jnp.where(mask, a, 0.0)
                da = lax.dot_general(doh, vblk, NT, preferred_element_type=F32)
                g = da * a
                p = jnp.dot(_split_bf16(g, SUM_TERMS), before, preferred_element_type=F32) + gpre
                dz = g - jnp.exp(z + l1) * (g + p)
                if mask is not None:
                    dz = jnp.where(mask, dz, 0.0)
                dzs.append(dz.astype(BF16))
                probs.append(a.astype(BF16))
                new_l.append(lpre)
                new_g.append(gpre + jnp.sum(g, axis=1, keepdims=True))
            dq = dq + jnp.dot(jnp.concatenate(dzs, axis=1), jnp.concatenate(kpair, axis=0),
                              preferred_element_type=F32)
            dk_acc[pl.ds(ks, tk), :] += lax.dot_general(jnp.concatenate(dzs, axis=0), jnp.concatenate(qpair, axis=0),
                                                        TN, preferred_element_type=F32)
            dv_acc[pl.ds(ks, tk), :] += lax.dot_general(jnp.concatenate(probs, axis=0), jnp.concatenate(dopair, axis=0),
                                                        TN, preferred_element_type=F32)
            return (dq, *new_l, *new_g)

        zero1 = jnp.zeros((tq, 1), F32)
        init = (jnp.zeros((tq, LANES), F32), zero1, zero1, zero1, zero1)
        carry = lax.fori_loop(first, qi * nd, lambda j, cr: block(j, cr, None), init)
        for i in range(nd):
            carry = block(qi * nd + i, carry, _rel_mask(tq, tk, i, True))
        dqkv_ref[0, pl.ds(pl.multiple_of(qi * tq, tq), tq), :] = carry[0].astype(BF16)

        @pl.when(qi == nq - 1)
        def _():
            dqkv_ref[1] = dk_acc[...].astype(BF16)
            dqkv_ref[2] = dv_acc[...].astype(BF16)

    q_spec, k_spec, v_spec = _attn_specs(s, tq)
    return _attn_call(
        body, name, (hp_n, nq),
        [q_spec, k_spec, v_spec, pl.BlockSpec((None, tq, LANES), lambda hp, qi: (hp, qi, 0)),
         pl.BlockSpec((None, None, SUBLANES, LANES), lambda hp, qi: (hp, qi, 0, 0)),
         pl.BlockSpec((tq, LANES), lambda hp, qi: (qi, hp))],
        [pl.BlockSpec((3, s, LANES), lambda hp, qi: (0, 0, hp))], [jax.ShapeDtypeStruct((3, s, d), BF16)],
        [pltpu.VMEM((s, LANES), F32), pltpu.VMEM((s, LANES), F32)],
        ("parallel", "arbitrary"), (qkv, qkv, qkv, ltot, visited, do), rider)


def _fox_prep(fl, bias, name):
    s, w = fl.shape
    tb = _pick(s, 512, SUBLANES)

    def body(fl_ref, b_ref, cum_ref, carry_ref):
        i = pl.program_id(0)

        @pl.when(i == 0)
        def _():
            carry_ref[...] = jnp.zeros_like(carry_ref)

        logf = _log_sigmoid(fl_ref[...] + b_ref[...])
        row = lax.broadcasted_iota(jnp.int32, (tb, tb), 0)
        col = lax.broadcasted_iota(jnp.int32, (tb, tb), 1)
        incl = (col <= row).astype(BF16)
        tot = carry_ref[...]
        rem = logf
        for _ in range(3):
            part = rem.astype(BF16)
            tot = tot + jnp.dot(incl, part, preferred_element_type=F32)
            rem = rem - part.astype(F32)
        cum_ref[...] = tot
        carry_ref[...] = tot[tb - 1:tb, :]

    blk = pl.BlockSpec((tb, w), lambda i: (i, 0))
    return pl.pallas_call(body, name=name, grid=(s // tb,), in_specs=[blk, pl.BlockSpec((1, w), lambda i: (0, 0))],
                          out_specs=blk, out_shape=jax.ShapeDtypeStruct((s, w), F32),
                          scratch_shapes=[pltpu.VMEM((1, w), F32)], compiler_params=_params("arbitrary"))(fl, bias)


def _fox_gate_bwd(dcum, fl, bias, n_heads, name):
    s, w = fl.shape
    tb = _pick(s, 512, SUBLANES)
    nb = s // tb

    def body(dc_ref, fl_ref, b_ref, dfl_ref, db_ref, carry_ref):
        i = pl.program_id(0)

        @pl.when(i == 0)
        def _():
            carry_ref[...] = jnp.zeros_like(carry_ref)
            db_ref[...] = jnp.zeros_like(db_ref)

        row = lax.broadcasted_iota(jnp.int32, (tb, tb), 0)
        col = lax.broadcasted_iota(jnp.int32, (tb, tb), 1)
        incl = (col >= row).astype(BF16)
        tot = jnp.broadcast_to(carry_ref[...], (tb, w))
        rem = dc_ref[...]
        for _ in range(3):
            part = rem.astype(BF16)
            tot = tot + jnp.dot(incl, part, preferred_element_type=F32)
            rem = rem - part.astype(F32)
        carry_ref[...] = tot[0:1, :]
        xg = fl_ref[...] + b_ref[...]
        e = jnp.exp(-jnp.abs(xg))
        sig_neg = jnp.where(xg >= 0.0, e, 1.0) / (1.0 + e)
        lane = lax.broadcasted_iota(jnp.int32, (tb, w), 1)
        dfl = jnp.where(lane < n_heads, tot * sig_neg, 0.0)
        dfl_ref[...] = dfl.astype(BF16)
        db_ref[...] += jnp.sum(dfl, axis=0, keepdims=True)

    blk = pl.BlockSpec((tb, w), lambda i: (nb - 1 - i, 0))
    vec = pl.BlockSpec((1, w), lambda i: (0, 0))
    return pl.pallas_call(body, name=name, grid=(nb,), in_specs=[blk, blk, vec], out_specs=[blk, vec],
                          out_shape=[jax.ShapeDtypeStruct((s, w), BF16), jax.ShapeDtypeStruct((1, w), F32)],
                          scratch_shapes=[pltpu.VMEM((1, w), F32)], compiler_params=_params("arbitrary"))(dcum, fl, bias)


def _head_columns(cum_blk, hp):
    lane = lax.broadcasted_iota(jnp.int32, cum_blk.shape, 1)
    return tuple(jnp.sum(jnp.where(lane == 2 * hp + h, cum_blk, 0.0), axis=1, keepdims=True) for h in range(2))


def _fox_fwd(qkv, cum, ck, tq, tk, name, rider=None):
    _, s, d = qkv.shape
    hp_n, nq, nd = d // LANES, s // tq, tq // tk
    scale = HEAD_DIM ** -0.5

    def body(q_ref, k_ref, v_ref, cq_ref, ck_ref, o_ref, lse_ref):
        qi = pl.program_id(1)
        low = lax.broadcasted_iota(jnp.int32, (tq, LANES), 1) < HEAD_DIM
        qpair = _pair(q_ref[...] * scale)
        cqs = _head_columns(cq_ref[...], pl.program_id(0))

        def block(kb, carry, mask):
            acc, stats = carry[0], carry[1:]
            ks = pl.multiple_of(kb * tk, tk)
            kblk = k_ref[pl.ds(ks, tk), :]
            vpair = _pair(v_ref[pl.ds(ks, tk), :])
            probs, alphas, new = [], [], []
            for h, (qh, cqh) in enumerate(zip(qpair, cqs)):
                m, lsum = stats[2 * h], stats[2 * h + 1]
                z = lax.dot_general(qh, kblk, NT, preferred_element_type=F32)
                sc = z + (cqh - ck_ref[h:h + 1, pl.ds(ks, tk)])
                if mask is not None:
                    sc = jnp.where(mask, sc, -jnp.inf)
                m_new = jnp.maximum(m, jnp.max(sc, axis=1, keepdims=True))
                alpha = jnp.exp(m - m_new)
                p = jnp.exp(sc - m_new)
                new += [m_new, alpha * lsum + jnp.sum(p, axis=1, keepdims=True)]
                probs.append(p.astype(BF16))
                alphas.append(alpha)
            acc = jnp.where(low, alphas[0], alphas[1]) * acc + jnp.dot(
                jnp.concatenate(probs, axis=1), jnp.concatenate(vpair, axis=0), preferred_element_type=F32)
            return (acc, *new)

        neg = jnp.full((tq, 1), -jnp.inf, F32)
        zero1 = jnp.zeros((tq, 1), F32)
        carry = (jnp.zeros((tq, LANES), F32), neg, zero1, neg, zero1)
        for i in range(nd):
            carry = block(qi * nd + i, carry, _rel_mask(tq, tk, i, False))
        acc, m0, l0, m1, l1 = lax.fori_loop(0, qi * nd, lambda j, cr: block(qi * nd - 1 - j, cr, None), carry)
        o_ref[...] = (acc / jnp.where(low, l0, l1)).astype(BF16)
        lse_ref[...] = jnp.where(low, m0 + jnp.log(l0), m1 + jnp.log(l1))

    q_spec, k_spec, v_spec = _attn_specs(s, tq)
    pair_rows = pl.BlockSpec((None, tq, LANES), lambda hp, qi: (hp, qi, 0))
    return _attn_call(
        body, name, (hp_n, nq),
        [q_spec, k_spec, v_spec, pl.BlockSpec((tq, LANES), lambda hp, qi: (qi, 0)),
         pl.BlockSpec((None, 2, s), lambda hp, qi: (hp, 0, 0))],
        [pl.BlockSpec((tq, LANES), lambda hp, qi: (qi, hp)), pair_rows],
        [jax.ShapeDtypeStruct((s, d), BF16), jax.ShapeDtypeStruct((hp_n, s, LANES), F32)],
        [], ("parallel", "parallel"), (qkv, qkv, qkv, cum, ck), rider)


def _fox_bwd(qkv, o, do, lse, cum, ck, tq, tk, name, rider=None):
    _, s, d = qkv.shape
    hp_n, nq, nd = d // LANES, s // tq, tq // tk
    scale = HEAD_DIM ** -0.5

    def body(q_ref, k_ref, v_ref, o_ref, do_ref, lse_ref, cq_ref, ck_ref, dqkv_ref, dcq_ref, dck_ref, dk_acc, dv_acc):
        qi = pl.program_id(1)

        @pl.when(qi == 0)
        def _():
            dk_acc[...] = jnp.zeros_like(dk_acc)
            dv_acc[...] = jnp.zeros_like(dv_acc)
            dck_ref[...] = jnp.zeros_like(dck_ref)

        low = lax.broadcasted_iota(jnp.int32, (tq, LANES), 1) < HEAD_DIM
        dov, lsev = do_ref[...], lse_ref[...]
        qpair = _pair(q_ref[...] * scale)
        dopair = _pair(dov)
        prod = dov.astype(F32) * o_ref[...].astype(F32)
        deltas = (jnp.sum(jnp.where(low, prod, 0.0), axis=1, keepdims=True),
                  jnp.sum(jnp.where(low, 0.0, prod), axis=1, keepdims=True))
        cqs = _head_columns(cq_ref[...], pl.program_id(0))
        lses = (lsev[:, 0:1], lsev[:, HEAD_DIM:HEAD_DIM + 1])

        def block(kb, carry, mask):
            dq, rowsums = carry[0], carry[1:]
            ks = pl.multiple_of(kb * tk, tk)
            kblk, vblk = k_ref[pl.ds(ks, tk), :], v_ref[pl.ds(ks, tk), :]
            kpair = _pair(kblk * scale)
            dss, probs, new_rows = [], [], []
            for h, (qh, doh) in enumerate(zip(qpair, dopair)):
                z = lax.dot_general(qh, kblk, NT, preferred_element_type=F32)
                sc = z + (cqs[h] - ck_ref[h:h + 1, pl.ds(ks, tk)])
                p = jnp.exp(sc - lses[h])
                if mask is not None:
                    p = jnp.where(mask, p, 0.0)
                dp = lax.dot_general(doh, vblk, NT, preferred_element_type=F32)
                ds = p * (dp - deltas[h])
                dck_ref[h:h + 1, pl.ds(ks, tk)] -= jnp.sum(ds, axis=0, keepdims=True)
                new_rows.append(rowsums[h] + jnp.sum(ds, axis=1, keepdims=True))
                dss.append(ds.astype(BF16))
                probs.append(p.astype(BF16))
            dq = dq + jnp.dot(jnp.concatenate(dss, axis=1), jnp.concatenate(kpair, axis=0),
                              preferred_element_type=F32)
            dk_acc[pl.ds(ks, tk), :] += lax.dot_general(jnp.concatenate(dss, axis=0), jnp.concatenate(qpair, axis=0),
                                                       TN, preferred_element_type=F32)
            dv_acc[pl.ds(ks, tk), :] += lax.dot_general(jnp.concatenate(probs, axis=0), jnp.concatenate(dopair, axis=0),
                                                       TN, preferred_element_type=F32)
            return (dq, *new_rows)

        zero1 = jnp.zeros((tq, 1), F32)
        carry = lax.fori_loop(0, qi * nd, lambda j, cr: block(j, cr, None), (jnp.zeros((tq, LANES), F32), zero1, zero1))
        for i in range(nd):
            carry = block(qi * nd + i, carry, _rel_mask(tq, tk, i, False))
        dq, rs0, rs1 = carry
        dqkv_ref[0, pl.ds(pl.multiple_of(qi * tq, tq), tq), :] = dq.astype(BF16)
        dcq_ref[...] = jnp.where(low, rs0, rs1)

        @pl.when(qi == nq - 1)
        def _():
            dqkv_ref[1] = dk_acc[...].astype(BF16)
            dqkv_ref[2] = dv_acc[...].astype(BF16)

    q_spec, k_spec, v_spec = _attn_specs(s, tq)
    pair_rows = pl.BlockSpec((None, tq, LANES), lambda hp, qi: (hp, qi, 0))
    tile = pl.BlockSpec((tq, LANES), lambda hp, qi: (qi, hp))
    keys = pl.BlockSpec((None, 2, s), lambda hp, qi: (hp, 0, 0))
    return _attn_call(
        body, name, (hp_n, nq),
        [q_spec, k_spec, v_spec, tile, tile, pair_rows, pl.BlockSpec((tq, LANES), lambda hp, qi: (qi, 0)), keys],
        [pl.BlockSpec((3, s, LANES), lambda hp, qi: (0, 0, hp)), pair_rows, keys],
        [jax.ShapeDtypeStruct((3, s, d), BF16), jax.ShapeDtypeStruct((hp_n, s, LANES), F32),
         jax.ShapeDtypeStruct((hp_n, 2, s), F32)],
        [pltpu.VMEM((s, LANES), F32), pltpu.VMEM((s, LANES), F32)],
        ("parallel", "arbitrary"), (qkv, qkv, qkv, o, do, lse, cum, ck), rider)


def _loss_head(y, target, name):
    s, d = y.shape
    tm = _pick(s, 512, SUBLANES)

    def body(y_ref, t_ref, dy_ref, sq_ref):
        i = pl.program_id(0)
        diff = y_ref[...] - t_ref[...]
        dy_ref[...] = diff / d

        @pl.when(i == 0)
        def _():
            sq_ref[...] = jnp.zeros_like(sq_ref)

        sq_ref[...] += jnp.sum(diff * diff, axis=0, keepdims=True)

    row = pl.BlockSpec((tm, d), lambda i: (i, 0))
    vec = pl.BlockSpec((1, d), lambda i: (0, 0))
    return pl.pallas_call(body, name=name, grid=(s // tm,), in_specs=[row, row], out_specs=[row, vec],
                          out_shape=[jax.ShapeDtypeStruct((s, d), F32), jax.ShapeDtypeStruct((1, d), F32)],
                          compiler_params=_params("arbitrary"))(y, target)


def _mod_fwd(c_all, w_mod, b_mod_cols, name):
    nl, d, cols = w_mod.shape
    nb = c_all.shape[0]

    def body(c_ref, w_ref, b_ref, o_ref):
        cv = c_ref[...]
        act = (cv * jax.nn.sigmoid(cv)).astype(BF16)
        o_ref[...] = jnp.dot(act, w_ref[...].astype(BF16), preferred_element_type=F32) + b_ref[...]

    return pl.pallas_call(
        body, name=name, grid=(nl,),
        in_specs=[pl.BlockSpec((nb, d), lambda l: (0, 0)), pl.BlockSpec((None, d, cols), lambda l: (l, 0, 0)),
                  pl.BlockSpec((None, 1, cols), lambda l: (l, 0, 0))],
        out_specs=pl.BlockSpec((None, nb, cols), lambda l: (l, 0, 0)),
        out_shape=jax.ShapeDtypeStruct((nl, nb, cols), F32), compiler_params=_params("parallel"))(c_all, w_mod, b_mod_cols)


def _wmod_grad(c_all_t, dmod, name):
    d, nb = c_all_t.shape
    _, nl, cols = dmod.shape

    def body(c_ref, dm_ref, o_ref):
        cv = c_ref[...]
        act = cv * jax.nn.sigmoid(cv)
        tot = act[:, 0:1] * dm_ref[0]
        for b in range(1, nb):
            tot = tot + act[:, b:b + 1] * dm_ref[b]
        o_ref[...] = tot

    return pl.pallas_call(
        body, name=name, grid=(nl,),
        in_specs=[pl.BlockSpec((d, nb), lambda l: (0, 0)), pl.BlockSpec((nb, None, 1, cols), lambda l: (0, l, 0, 0))],
        out_specs=pl.BlockSpec((None, d, cols), lambda l: (l, 0, 0)),
        out_shape=jax.ShapeDtypeStruct((nl, d, cols), F32), compiler_params=_params("parallel"))(
            c_all_t, dmod.reshape(nb, nl, 1, cols))


def _adamw(recv, w, m, v, name, *, tr=256, tc=512):
    nq, nl, rp, cp = recv.shape
    _, r, c = w.shape
    br = _pick(r, tr, SUBLANES) if rp == r else r
    bc = _pick(c, tc, LANES) if cp == c else c
    rbr = br if rp == r else rp
    rbc = bc if cp == c else cp

    def body(rv_ref, w_ref, m_ref, v_ref, g_ref, d_ref, nm_ref, nv_ref):
        g = rv_ref[0, :br, :bc].astype(F32)
        for qd in range(1, nq):
            g = g + rv_ref[qd, :br, :bc].astype(F32)
        m_new = ADAM_B1 * m_ref[...] + (1.0 - ADAM_B1) * g
        v_new = ADAM_B2 * v_ref[...] + (1.0 - ADAM_B2) * jnp.square(g)
        m_hat = m_new / (1.0 - ADAM_B1 ** ADAM_STEP)
        v_hat = v_new / (1.0 - ADAM_B2 ** ADAM_STEP)
        g_ref[...] = g
        d_ref[...] = -ADAM_LR * (m_hat / (jnp.sqrt(v_hat) + ADAM_EPS) + ADAM_WD * w_ref[...])
        nm_ref[...] = m_new
        nv_ref[...] = v_new

    blk = pl.BlockSpec((None, br, bc), lambda l, i, j: (l, i, j))
    rblk = pl.BlockSpec((nq, None, rbr, rbc), lambda l, i, j: (0, l, i, j))
    shape = jax.ShapeDtypeStruct(w.shape, F32)
    return pl.pallas_call(body, name=name, grid=(nl, r // br, c // bc), in_specs=[rblk, blk, blk, blk],
                          out_specs=[blk] * 4, out_shape=[shape] * 4,
                          compiler_params=_params("parallel", "parallel", "parallel"))(recv, w, m, v)


def _mesh_pos():
    return lax.axis_index("x"), lax.axis_index("y"), lax.axis_index("c")


def _allgather_small(block, name):
    m_per, n = block.shape

    def body(x_ref, out_ref, send_sems, recv_sems, local_sem):
        x, y, c = _mesh_pos()
        me, sibling = (x, y, c), (x, y, 1 - c)
        chips = [(1 - x, y), (x, 1 - y), (1 - x, 1 - y)]

        def rows(px, py, pc):
            return out_ref.at[pl.ds((4 * px + 2 * py + pc) * m_per, m_per), :]

        def copy(k, blk, to, src=None):
            return pltpu.make_async_remote_copy(
                src_ref=rows(*blk) if src is None else src, dst_ref=rows(*blk),
                send_sem=send_sems.at[k], recv_sem=recv_sems.at[k], device_id=to, device_id_type=MESH)

        mine = pltpu.make_async_copy(x_ref, rows(*me), local_sem)
        mine.start()
        first = [copy(0, me, sibling, src=x_ref)]
        first += [copy(1 + j, me, (*chip, c), src=x_ref) for j, chip in enumerate(chips)]
        for cp in first:
            cp.start()
        passed = [copy(4 + j, (*chip, c), sibling) for j, chip in enumerate(chips)]
        for j, chip in enumerate(chips):
            copy(1 + j, (*chip, c), me).wait_recv()
            passed[j].start()
        copy(0, sibling, me).wait_recv()
        for j, chip in enumerate(chips):
            copy(4 + j, (*chip, 1 - c), me).wait_recv()
        for cp in first + passed:
            cp.wait_send()
        mine.wait()

    return pl.pallas_call(
        body, name=name, out_shape=jax.ShapeDtypeStruct((N_DEV * m_per, n), block.dtype),
        in_specs=[pl.BlockSpec(memory_space=pltpu.VMEM)], out_specs=pl.BlockSpec(memory_space=pltpu.VMEM),
        scratch_shapes=[pltpu.SemaphoreType.DMA((7,)), pltpu.SemaphoreType.DMA((7,)), pltpu.SemaphoreType.DMA],
        compiler_params=pltpu.CompilerParams(vmem_limit_bytes=VMEM_LIMIT_BYTES))(block)


def _exchange(srcs, dsts, jobs, name):
    n_src, n_job, n_dst = len(srcs), len(jobs), len(dsts)

    def body(*refs):
        src_refs, dst_refs = refs[:n_src], refs[n_src + n_dst:n_src + 2 * n_dst]
        send_sems, recv_sems, local_sems = refs[n_src + 2 * n_dst:]
        x, y, c = _mesh_pos()
        me = 4 * x + 2 * y + c
        pending = []
        for t, (si, di, src_slice, dst_slice) in enumerate(jobs):
            src, dst = src_refs[si], dst_refs[di]
            lc = pltpu.make_async_copy(src_slice(src, me), dst_slice(dst, me), local_sems.at[t])
            lc.start()
            pending.append(lc)
            for dd in range(1, N_DEV):
                px = 1 - x if dd & 4 else x
                py = 1 - y if dd & 2 else y
                pc = 1 - c if dd & 1 else c
                cp = pltpu.make_async_remote_copy(
                    src_ref=src_slice(src, 4 * px + 2 * py + pc), dst_ref=dst_slice(dst, me),
                    send_sem=send_sems.at[t, dd - 1], recv_sem=recv_sems.at[t, dd - 1],
                    device_id=(px, py, pc), device_id_type=MESH)
                cp.start()
                pending.append(cp)
        for cp in pending:
            cp.wait()

    hbm = pl.BlockSpec(memory_space=pl.ANY)
    return pl.pallas_call(
        body, name=name, out_shape=[jax.ShapeDtypeStruct(a.shape, a.dtype) for a in dsts],
        in_specs=[hbm] * (n_src + n_dst), out_specs=[hbm] * n_dst,
        input_output_aliases={n_src + i: i for i in range(n_dst)},
        scratch_shapes=[pltpu.SemaphoreType.DMA((n_job, N_DEV - 1)), pltpu.SemaphoreType.DMA((n_job, N_DEV - 1)),
                        pltpu.SemaphoreType.DMA((n_job,))])(*srcs, *dsts)


def _attn_call(body, name, grid, in_specs, out_specs, out_shape, scratch_shapes, semantics, operands, rider):
    out_specs, out_shape, scratch_shapes = list(out_specs), list(out_shape), list(scratch_shapes)
    if rider is None:
        res = pl.pallas_call(body, name=name, grid=grid, in_specs=list(in_specs), out_specs=out_specs,
                             out_shape=out_shape, scratch_shapes=scratch_shapes,
                             compiler_params=_params(*semantics))(*operands)
        return list(res), None
    srcs, dsts, jobs = rider
    n_in, n_out, n_scr, n_src, n_dst = len(operands), len(out_shape), len(scratch_shapes), len(srcs), len(dsts)

    def carrying(*refs):
        src_refs = refs[n_in:n_in + n_src]
        at = n_in + n_src + n_dst
        dst_refs = refs[at + n_out:at + n_out + n_dst]
        scratch = refs[at + n_out + n_dst:at + n_out + n_dst + n_scr]
        send_sems, recv_sems = refs[-2], refs[-1]
        ids = [pl.program_id(a) for a in range(len(grid))]
        first = functools.reduce(jnp.logical_and, [i == 0 for i in ids])
        last = functools.reduce(jnp.logical_and, [i == g - 1 for i, g in zip(ids, grid)])

        @pl.when(first)
        def _():
            local, remote = _exchange_copies(jobs, src_refs, dst_refs, send_sems, recv_sems, True)
            for cp in remote + local:
                cp.start()

        body(*refs[:n_in], *refs[at:at + n_out], *scratch)

        @pl.when(last)
        def _():
            local, remote = _exchange_copies(jobs, src_refs, dst_refs, send_sems, recv_sems, False)
            for cp in local:
                cp.wait()
            for cp in remote:
                cp.wait_send()
                cp.wait_recv()

    hbm = pl.BlockSpec(memory_space=pl.ANY)
    sems = pltpu.SemaphoreType.DMA((len(jobs) * N_DEV,))
    res = pl.pallas_call(
        carrying, name=name, grid=grid, in_specs=list(in_specs) + [hbm] * (n_src + n_dst),
        out_specs=out_specs + [hbm] * n_dst,
        out_shape=out_shape + [jax.ShapeDtypeStruct(a.shape, a.dtype) for a in dsts],
        input_output_aliases={n_in + n_src + i: n_out + i for i in range(n_dst)},
        scratch_shapes=scratch_shapes + [sems, sems],
        compiler_params=_params(*["arbitrary"] * len(grid)))(*operands, *srcs, *dsts)
    return list(res[:n_out]), list(res[n_out:])


def _peer(x, y, c, dd):
    return (1 - x if dd & 4 else x, 1 - y if dd & 2 else y, 1 - c if dd & 1 else c)


def _exchange_copies(jobs, src_refs, dst_refs, send_sems, recv_sems, sending, only=None):
    x, y, c = _mesh_pos()
    me = 4 * x + 2 * y + c
    local, remote = [], []
    for t, (si, di, src_slice, dst_slice) in enumerate(jobs):
        if only is not None and t not in only:
            continue
        local.append(pltpu.make_async_copy(src_slice(src_refs[si], me), dst_slice(dst_refs[di], me),
                                           send_sems.at[t * N_DEV]))
        for dd in range(1, N_DEV):
            px, py, pc = _peer(x, y, c, dd)
            p = 4 * px + 2 * py + pc
            remote.append(pltpu.make_async_remote_copy(
                src_ref=src_slice(src_refs[si], p), dst_ref=dst_slice(dst_refs[di], me if sending else p),
                send_sem=send_sems.at[t * N_DEV + dd], recv_sem=recv_sems.at[t * N_DEV + dd],
                device_id=(px, py, pc), device_id_type=MESH))
    return local, remote


def _exchange_start(srcs, dsts, jobs, name, after=None):
    n_src, n_dst, n_job = len(srcs), len(dsts), len(jobs)
    n_in = n_src + n_dst + (after is not None)

    def body(*refs):
        src_refs, dst_refs = refs[:n_src], refs[n_src:n_src + n_dst]
        send_sems, recv_sems = refs[n_in], refs[n_in + 1]
        local, remote = _exchange_copies(jobs, src_refs, dst_refs, send_sems, recv_sems, True)
        for cp in remote + local:
            cp.start()
        refs[-1][...] = jnp.zeros((SUBLANES, LANES), F32)

    hbm = pl.BlockSpec(memory_space=pltpu.HBM)
    sem = pl.BlockSpec(memory_space=pltpu.SEMAPHORE)
    operands = [pltpu.with_memory_space_constraint(a, pltpu.HBM) for a in (*srcs, *dsts)]
    extra_specs = [] if after is None else [pl.BlockSpec(memory_space=pl.ANY)]
    extra = [] if after is None else [after]
    res = pl.pallas_call(
        body, name=name, in_specs=[hbm] * (n_src + n_dst) + extra_specs,
        out_specs=(sem, sem, *[hbm] * (n_src + n_dst), pl.BlockSpec(memory_space=pltpu.VMEM)),
        out_shape=(pltpu.SemaphoreType.DMA((n_job * N_DEV,)), pltpu.SemaphoreType.DMA((n_job * N_DEV,)),
                   *[pltpu.HBM(a.shape, a.dtype) for a in (*srcs, *dsts)],
                   jax.ShapeDtypeStruct((SUBLANES, LANES), F32)),
        input_output_aliases={i: 2 + i for i in range(n_src + n_dst)},
        compiler_params=pltpu.CompilerParams(has_side_effects=pltpu.SideEffectType.DATAFLOW_SIDE_EFFECTING))(
            *operands, *extra)
    return res[0], res[1], list(res[2:2 + n_src]), list(res[2 + n_src:2 + n_src + n_dst]), res[-1]


def _exchange_wait(parts, dsts, after, name):
    n_dst = len(dsts)
    n_src = sum(len(p[2]) for p in parts)

    def body(*refs):
        dst_refs = refs[n_src:n_src + n_dst]
        sem_refs = refs[n_src + n_dst:n_src + n_dst + 2 * len(parts)]
        at = 0
        for k, (_, _, part_srcs, jobs, only) in enumerate(parts):
            src_refs = refs[at:at + len(part_srcs)]
            at += len(part_srcs)
            local, remote = _exchange_copies(jobs, src_refs, dst_refs, sem_refs[2 * k], sem_refs[2 * k + 1], False, only)
            for cp in local:
                cp.wait()
            for cp in remote:
                cp.wait_send()
                cp.wait_recv()

    hbm = pl.BlockSpec(memory_space=pltpu.HBM)
    sem = pl.BlockSpec(memory_space=pltpu.SEMAPHORE)
    srcs = [a for p in parts for a in p[2]]
    sems = [s for p in parts for s in (p[0], p[1])]
    res = pl.pallas_call(
        body, name=name, in_specs=[hbm] * (n_src + n_dst) + [sem] * len(sems) + [pl.BlockSpec(memory_space=pl.ANY)],
        out_specs=[hbm] * (n_src + n_dst),
        out_shape=[pltpu.HBM(a.shape, a.dtype) for a in (*srcs, *dsts)],
        input_output_aliases={i: i for i in range(n_src + n_dst)},
        compiler_params=pltpu.CompilerParams(has_side_effects=pltpu.SideEffectType.DATAFLOW_SIDE_EFFECTING))(
            *srcs, *dsts, *sems, after)
    return list(res[:n_src]), list(res[n_src:])


def _whole(ref, p):
    return ref


def _layer_of(layer, inner):
    return lambda ref, p: inner(ref.at[layer], p)


def _cols_of(width):
    def take(ref, p):
        lead = (slice(None),) * (len(ref.shape) - 1)
        return ref.at[lead + (pl.ds(pl.multiple_of(p * width, LANES), width),)]
    return take


def _rows_of(height):
    def take(ref, p):
        lead = (slice(None),) * (len(ref.shape) - 2)
        return ref.at[lead + (pl.ds(pl.multiple_of(p * height, SUBLANES), height), slice(None))]
    return take


def _slot(layer=None):
    if layer is None:
        return lambda ref, p: ref.at[p]
    return lambda ref, p: ref.at[p, layer]


def _local_step(x0, target, mod, g_mix_pre, g_mix_post, g_ffn_pre, g_ffn_post, wqkv, wo, wg, wu, wd,
                wfg, bfg, wconv, bconv, tiles, comm=None):
    s, d = x0.shape
    nl = mod.shape[0]
    n_heads = d // HEAD_DIM
    hp_n = d // LANES
    vec = lambda a: a.reshape(1, -1)
    saved = []
    xcur = x0
    for l in range(nl):
        if comm is not None:
            wqkv, wo, wg, wu, wd = comm["weights"]()
        sh_a, sc_a, gt_a, sh_f, sc_f, gt_f = (vec(mod[l, i * d:(i + 1) * d]) for i in range(6))
        fox = l % 2 == 1
        h1 = _norm_mod_fwd(xcur, vec(g_mix_pre[l]), sh_a, sc_a, f"norm_mix_fwd_{l}")
        qkv = _mm_nn(h1, wqkv, l, BF16, f"qkv_fwd_{l}", split_out=d)
        rider = None if comm is None else comm["fwd_rider"](l)
        if fox:
            j = l // 2
            fl = _mm_nn(h1, wfg, j, F32, f"fgate_fwd_{l}")
            cum = _fox_prep(fl, bfg[j], f"fox_prep_{l}")
            ck = cum[:, :n_heads].T.reshape(hp_n, 2, s)
            (o, stat), moved = _fox_fwd(qkv, cum, ck, *tiles["fox_fwd"], f"fox_fwd_{l}", rider)
            extra = (fl, cum, ck)
        else:
            (o, *stat), moved = _sb_fwd(qkv, _key_bound(qkv, f"key_bound_{l}"), *tiles["sb_fwd"], f"sb_fwd_{l}", rider)
            extra = None
        if moved is not None:
            comm["done"](moved)
            wqkv, wo, wg, wu, wd = comm["weights"]()
        u = _mm_nn(o, wo, l, F32, f"attn_out_fwd_{l}")
        x2 = _post_fwd(xcur, u, vec(g_mix_post[l]), gt_a, f"post_mix_fwd_{l}")
        h2 = _norm_mod_fwd(x2, vec(g_ffn_pre[l]), sh_f, sc_f, f"norm_ffn_fwd_{l}")
        gp = _mm_nn(h2, wg, l, BF16, f"ffn_gate_fwd_{l}")
        up = _mm_nn(h2, wu, l, BF16, f"ffn_up_fwd_{l}")
        act = _conv_act_fwd(gp, up, wconv[l], bconv[l], f"conv_act_fwd_{l}")
        yv = _mm_nn(act, wd, l, F32, f"ffn_down_fwd_{l}")
        x3 = _post_fwd(x2, yv, vec(g_ffn_post[l]), gt_f, f"post_ffn_fwd_{l}")
        saved.append((xcur, h1, qkv, o, stat, extra, u, x2, h2, gp, up, act, yv))
        xcur = x3

    dx, sq = _loss_head(xcur, target, "loss_head")
    small, big = [None] * nl, [None] * nl
    for l in reversed(range(nl)):
        xin, h1, qkv, o, stat, extra, u, x2, h2, gp, up, act, yv = saved[l]
        sc_a, gt_a, sc_f, gt_f = (vec(mod[l, i * d:(i + 1) * d]) for i in (1, 2, 4, 5))
        fox = l % 2 == 1
        dy, dgt_f, dg4 = _post_bwd(dx, yv, vec(g_ffn_post[l]), gt_f, f"post_ffn_bwd_{l}")
        dact = _mm_nt(dy, wd, l, BF16, f"ffn_down_dx_{l}")
        dwd = _mm_tn(act, dy, BF16, f"ffn_down_dw_{l}")
        dgp, dup, dwc, dbc = _conv_act_bwd(dact, gp, up, wconv[l], bconv[l], f"conv_act_bwd_{l}")
        dh2 = _mm_nt(dgp, wg, l, F32, f"ffn_gate_dx_{l}")
        dh2 = _mm_nt(dup, wu, l, F32, f"ffn_up_dx_{l}", add=dh2)
        dwg = _mm_tn(h2, dgp, BF16, f"ffn_gate_dw_{l}")
        dwu = _mm_tn(h2, dup, BF16, f"ffn_up_dw_{l}")
        dx2, dsh_f, dsc_f, dg3 = _norm_mod_bwd(dh2, x2, vec(g_ffn_pre[l]), sc_f, dx, f"norm_ffn_bwd_{l}")
        du, dgt_a, dg2 = _post_bwd(dx2, u, vec(g_mix_post[l]), gt_a, f"post_mix_bwd_{l}")
        do = _mm_nt(du, wo, l, BF16, f"attn_out_dx_{l}")
        dwo = _mm_tn(o, du, BF16, f"attn_out_dw_{l}")
        rider = None
        if comm is not None:
            comm["grads"](l, dict(gate=dwg, up=dwu, down=dwd))
            rider = comm["bwd_rider"]()
        if fox:
            j = l // 2
            fl, cum, ck = extra
            (dqkv, dcq, dck), moved = _fox_bwd(qkv, o, do, stat, cum, ck, *tiles["fox_bwd"], f"fox_bwd_{l}", rider)
            dcq = jnp.max(dcq.reshape(hp_n, s, 2, HEAD_DIM), axis=3)
            dcum = dcq.transpose(1, 0, 2).reshape(s, n_heads) + dck.reshape(n_heads, s).T
            dcum = jnp.pad(dcum, ((0, 0), (0, LANES - n_heads)))
            dfl, dbfg = _fox_gate_bwd(dcum, fl, bfg[j], n_heads, f"fox_gate_bwd_{l}")
            dh1 = _mm_nt(dfl, wfg, j, F32, f"fgate_dx_{l}")
            dh1 = _mm_nt(dqkv, wqkv, l, F32, f"qkv_dx_{l}", add=dh1)
            dwfg = _mm_tn(h1, dfl, F32, f"fgate_dw_{l}")[:, :n_heads]
            dbfg = dbfg[0, :n_heads]
        else:
            (dqkv,), moved = _sb_bwd(qkv, *stat, do, *tiles["sb_bwd"], f"sb_bwd_{l}", rider)
            dh1 = _mm_nt(dqkv, wqkv, l, F32, f"qkv_dx_{l}")
            dwfg = dbfg = None
        if moved is not None:
            comm["done"](moved)
        dwqkv = _mm_tn(h1, dqkv, BF16, f"qkv_dw_{l}")
        dx, dsh_a, dsc_a, dg1 = _norm_mod_bwd(dh1, xin, vec(g_mix_pre[l]), sc_a, dx2, f"norm_mix_bwd_{l}")
        dmod = jnp.concatenate([dsh_a, dsc_a, dgt_a, dsh_f, dsc_f, dgt_f], axis=1)[0]
        small[l] = dict(dmod=dmod, dg1=dg1[0], dg2=dg2[0], dg3=dg3[0], dg4=dg4[0], dbc=dbc[0], dwc=dwc,
                        dbfg=dbfg, dwfg=dwfg)
        big[l] = dict(qkv=dwqkv, o=dwo, gate=dwg, up=dwu, down=dwd)
        if comm is not None:
            comm["grads"](l, dict(qkv=dwqkv, o=dwo))
    return sq, dx, small, big


def _rows128(a, rows):
    flat = a.reshape(-1)
    return jnp.pad(flat, (0, rows * LANES - flat.shape[0])).reshape(rows, LANES)


def _ceil8(n_elems):
    rows = -(-n_elems // LANES)
    return -(-rows // SUBLANES) * SUBLANES


def kernel(x, c, w_mod, b_mod, g_mix_pre, g_mix_post, w_qkv, w_o, w_fg, b_fg, g_ffn_pre, g_ffn_post, w_ffn_gate, w_ffn_up, w_conv, b_conv, w_ffn_down, loss_target, m_w_mod, m_b_mod, m_g_mix_pre, m_g_mix_post, m_w_qkv, m_w_o, m_w_fg, m_b_fg, m_g_ffn_pre, m_g_ffn_post, m_w_ffn_gate, m_w_ffn_up, m_w_conv, m_b_conv, m_w_ffn_down, v_w_mod, v_b_mod, v_g_mix_pre, v_g_mix_post, v_w_qkv, v_w_o, v_w_fg, v_b_fg, v_g_ffn_pre, v_g_ffn_post, v_w_ffn_gate, v_w_ffn_up, v_w_conv, v_b_conv, v_w_ffn_down):
    _, s, d = x.shape
    nl = w_qkv.shape[0]
    nf = w_fg.shape[0]
    n_heads = w_fg.shape[2]
    fs = w_ffn_gate.shape[2]
    fp = -(-fs // LANES) * LANES
    f_full, f_pad = N_DEV * fs, N_DEV * fp
    mod_cols = w_mod.shape[2]
    qs, orows = w_qkv.shape[2], w_o.shape[1]
    xi, yi, ci = _mesh_pos()
    me = 4 * xi + 2 * yi + ci

    c_rows = d // LANES
    c_all = _allgather_small(jnp.pad(c.reshape(1, d), ((0, SUBLANES - 1), (0, 0))).reshape(SUBLANES * c_rows, LANES),
                             "gather_cond")
    c_all = c_all.reshape(N_DEV, SUBLANES, d)[:, 0, :]
    b_mod_cols = lax.dynamic_slice_in_dim(b_mod, me * mod_cols, mod_cols, axis=1).reshape(nl, 1, mod_cols)
    mod_part = _mod_fwd(c_all, w_mod, b_mod_cols, "mod_fwd")

    conv_pad = jnp.pad(w_conv, ((0, 0), (0, 0), (0, fp - fs)))
    r_mod, r_conv, r_fg = _ceil8(mod_part.size), _ceil8(conv_pad.size), _ceil8(w_fg.size)
    payload = jnp.concatenate([_rows128(mod_part, r_mod), _rows128(conv_pad, r_conv), _rows128(w_fg, r_fg)], axis=0)
    got = _allgather_small(payload, "gather_small_weights").reshape(N_DEV, r_mod + r_conv + r_fg, LANES)
    mod_g = got[:, :r_mod].reshape(N_DEV, -1)[:, :mod_part.size].reshape(N_DEV, nl, N_DEV, mod_cols)
    mod = lax.dynamic_index_in_dim(mod_g, me, axis=2, keepdims=False).transpose(1, 0, 2).reshape(nl, N_DEV * mod_cols)
    conv_g = got[:, r_mod:r_mod + r_conv].reshape(N_DEV, -1)[:, :conv_pad.size].reshape(N_DEV, nl, 3, fp)
    wconv_full = conv_g.transpose(1, 2, 0, 3).reshape(nl, 3, f_pad)
    fg_g = got[:, r_mod + r_conv:].reshape(N_DEV, -1)[:, :w_fg.size].reshape(N_DEV, nf, orows, n_heads)
    wfg_full = fg_g.transpose(1, 0, 2, 3).reshape(nf, d, n_heads)
    wfg_full = jnp.pad(wfg_full, ((0, 0), (0, 0), (0, LANES - n_heads))).astype(BF16)
    bfg_full = jnp.pad(b_fg, ((0, 0), (0, LANES - n_heads))).reshape(nf, 1, LANES)
    bconv_full = jnp.pad(b_conv.reshape(nl, N_DEV, fs), ((0, 0), (0, 0), (0, fp - fs))).reshape(nl, 1, f_pad)

    gate_sh = jnp.pad(w_ffn_gate, ((0, 0), (0, 0), (0, fp - fs))).astype(BF16)
    up_sh = jnp.pad(w_ffn_up, ((0, 0), (0, 0), (0, fp - fs))).astype(BF16)
    down_sh = jnp.pad(w_ffn_down, ((0, 0), (0, fp - fs), (0, 0))).astype(BF16)
    shards = [w_qkv.astype(BF16), w_o.astype(BF16), gate_sh, up_sh, down_sh]
    full_shapes = [jax.ShapeDtypeStruct((nl, d, N_DEV * qs), BF16), jax.ShapeDtypeStruct((nl, d, d), BF16),
                   jax.ShapeDtypeStruct((nl, d, f_pad), BF16), jax.ShapeDtypeStruct((nl, d, f_pad), BF16),
                   jax.ShapeDtypeStruct((nl, f_pad, d), BF16)]
    place = [_cols_of(qs), _rows_of(orows), _cols_of(fp), _cols_of(fp), _rows_of(fp)]
    def gather_jobs(l):
        return [(i, i, _layer_of(l, _whole), _layer_of(l, place[i])) for i in range(5)]

    order = ["qkv", "o", "gate", "up", "down"]
    send = {"qkv": _cols_of(qs), "o": _rows_of(orows), "gate": _cols_of(fp), "up": _cols_of(fp), "down": _rows_of(fp)}
    recv_shapes = [(N_DEV, nl, d, qs), (N_DEV, nl, orows, d), (N_DEV, nl, d, fp), (N_DEV, nl, d, fp), (N_DEV, nl, fp, d)]
    state = {"full": _exchange(shards, [lax.empty(sh.shape, sh.dtype) for sh in full_shapes], gather_jobs(0),
                               "gather_weights_0"),
             "recv": [lax.empty(sh, BF16) for sh in recv_shapes], "pending": [], "moving": None}

    def fwd_rider(l):
        if l + 1 == nl:
            return None
        state["moving"] = "full"
        return shards, state["full"], gather_jobs(l + 1)

    def bwd_rider():
        waiting, state["pending"], state["moving"] = state["pending"], [], "recv"
        if not waiting:
            return None
        jobs = [(k, order.index(nm), send[nm], _slot(l)) for k, (l, nm, _) in enumerate(waiting)]
        return [g for _, _, g in waiting], state["recv"], jobs

    def done(dsts):
        state[state["moving"]] = dsts

    def grads(l, new):
        state["pending"] += [(l, nm, new[nm]) for nm in order if nm in new]

    comm = dict(weights=lambda: state["full"], fwd_rider=fwd_rider, bwd_rider=bwd_rider, done=done, grads=grads)
    sq, dx, small, big = _local_step(x[0], loss_target[0], mod, g_mix_pre, g_mix_post, g_ffn_pre, g_ffn_post,
                                     None, None, None, None, None, wfg_full, bfg_full, wconv_full, bconv_full,
                                     ATTN_TILES, comm)
    loss = lax.psum(0.5 * jnp.sum(sq) / d, ("x", "y", "c"))
    recv = _exchange(*bwd_rider(), "scatter_grads_last")
    upd = {}
    for nm, rv, wt, mt, vt in zip(order, recv, [w_qkv, w_o, w_ffn_gate, w_ffn_up, w_ffn_down],
                                  [m_w_qkv, m_w_o, m_w_ffn_gate, m_w_ffn_up, m_w_ffn_down],
                                  [v_w_qkv, v_w_o, v_w_ffn_gate, v_w_ffn_up, v_w_ffn_down]):
        upd[nm] = _adamw(rv, wt, mt, vt, f"adamw_{nm}")

    stack = lambda key: jnp.stack([small[l][key] for l in range(nl)])
    dmod = stack("dmod")
    dgs = [stack(k) for k in ("dg1", "dg2", "dg3", "dg4")]
    dbc = stack("dbc").reshape(nl, N_DEV, fp)[:, :, :fs].reshape(nl, f_full)
    dbfg = jnp.stack([small[l]["dbfg"] for l in range(nl) if l % 2 == 1])
    dwc = stack("dwc").reshape(nl, 3, N_DEV, fp)[:, :, :, :fs].reshape(nl, 3, f_full)
    dwfg = jnp.stack([small[l]["dwfg"] for l in range(nl) if l % 2 == 1])
    rep_parts = [dmod] + dgs + [dbc, dbfg]
    rep_rows = [_ceil8(p.size) for p in rep_parts]
    r_rep, r_wc, r_wfg = sum(rep_rows), _ceil8(dwc.size), _ceil8(dwfg.size)
    payload = jnp.concatenate([_rows128(p, r) for p, r in zip(rep_parts, rep_rows)]
                              + [_rows128(dwc, r_wc), _rows128(dwfg, r_wfg)], axis=0)
    gsm = _allgather_small(payload, "gather_small_grads").reshape(N_DEV, 1, r_rep + r_wc + r_wfg, LANES)

    def pack(parts):
        return jnp.concatenate([_rows128(p, r) for p, r in zip(parts, rep_rows)], axis=0).reshape(1, r_rep, LANES)

    rep_w = [b_mod, g_mix_pre, g_mix_post, g_ffn_pre, g_ffn_post, b_conv, b_fg]
    rep_m = [m_b_mod, m_g_mix_pre, m_g_mix_post, m_g_ffn_pre, m_g_ffn_post, m_b_conv, m_b_fg]
    rep_v = [v_b_mod, v_g_mix_pre, v_g_mix_post, v_g_ffn_pre, v_g_ffn_post, v_b_conv, v_b_fg]
    rep_out = _adamw(gsm[:, :, :r_rep], pack(rep_w), pack(rep_m), pack(rep_v), "adamw_replicated", tr=r_rep, tc=LANES)

    def unpack(packed):
        outs, at = [], 0
        for p, r in zip(rep_w, rep_rows):
            outs.append(packed[0, at:at + r].reshape(-1)[:p.size].reshape(p.shape))
            at += r
        return outs

    rep_g, rep_d, rep_nm, rep_nv = (unpack(a) for a in rep_out)

    wc_all = gsm[:, 0, r_rep:r_rep + r_wc].reshape(N_DEV, -1)[:, :dwc.size].reshape(N_DEV, 1, nl * 3, f_full)
    wc_mine = lax.dynamic_slice_in_dim(wc_all, me * fs, fs, axis=3)
    wc_out = _adamw(wc_mine, w_conv.reshape(1, nl * 3, fs), m_w_conv.reshape(1, nl * 3, fs),
                    v_w_conv.reshape(1, nl * 3, fs), "adamw_conv", tr=nl * 3, tc=fs)
    wc_out = [a.reshape(nl, 3, fs) for a in wc_out]
    wfg_all = gsm[:, 0, r_rep + r_wc:].reshape(N_DEV, -1)[:, :dwfg.size].reshape(N_DEV, nf, d, n_heads)
    wfg_mine = lax.dynamic_slice_in_dim(wfg_all, me * orows, orows, axis=2)
    wfg_out = _adamw(wfg_mine, w_fg, m_w_fg, v_w_fg, "adamw_fgate", tr=orows, tc=n_heads)

    dmod_all = gsm[:, 0, :rep_rows[0]].reshape(N_DEV, -1)[:, :dmod.size].reshape(N_DEV, nl, N_DEV * mod_cols)
    dmod_mine = lax.dynamic_slice_in_dim(dmod_all, me * mod_cols, mod_cols, axis=2)
    gwmod = _wmod_grad(c_all.T, dmod_mine, "wmod_grad")
    wmod_out = _adamw(gwmod.reshape(1, nl, d, mod_cols), w_mod, m_w_mod, v_w_mod, "adamw_mod")

    per_weight = [wmod_out, None, None, None, upd["qkv"], upd["o"], wfg_out, None, None, None,
                  upd["gate"], upd["up"], wc_out, None, upd["down"]]
    rep_index = {1: 0, 2: 1, 3: 2, 8: 3, 9: 4, 13: 5, 7: 6}
    outs = [[], [], [], []]
    for pos, res in enumerate(per_weight):
        for kind in range(4):
            if res is None:
                outs[kind].append((rep_g, rep_d, rep_nm, rep_nv)[kind][rep_index[pos]])
            else:
                outs[kind].append(res[kind])
    return (loss, dx.reshape(1, s, d), *outs[0], *outs[1], *outs[2], *outs[3])
```

```python
import functools

import jax
import jax.numpy as jnp
from jax import lax
from jax.experimental import pallas as pl
from jax.experimental.pallas import tpu as pltpu

F32 = jnp.float32
BF16 = jnp.bfloat16

N_DEV = 8
HEAD_DIM = 64
LANES = 128
SUBLANES = 8
RMS_EPS = 1e-6
ADAM_LR = 0.001
ADAM_B1 = 0.9
ADAM_B2 = 0.999
ADAM_EPS = 1e-08
ADAM_WD = 0.01
ADAM_STEP = 10
VMEM_LIMIT_BYTES = 56 * 1024 * 1024
EXP_UNDERFLOW = -110.0
SUM_TERMS = 1
HALO = 16
MM_ROWS = 2048
ATTN_TILES = {"sb_fwd": (1024, 256), "sb_bwd": (1024, 256), "fox_fwd": (1024, 256), "fox_bwd": (1024, 256)}

NN = (((1,), (0,)), ((), ()))
NT = (((1,), (1,)), ((), ()))
TN = (((0,), (0,)), ((), ()))
MESH = pl.DeviceIdType.MESH


def _params(*sem):
    return pltpu.CompilerParams(dimension_semantics=sem, vmem_limit_bytes=VMEM_LIMIT_BYTES)


def _pick(n, pref, quantum):
    if n <= pref:
        return n
    t = (pref // quantum) * quantum
    while n % t:
        t -= quantum
    return t


def _mm(a, b, *, dims, grid, a_spec, b_spec, o_spec, out_shape, name, add=None):
    nk = grid[2]
    acc_shape = tuple(d for d in o_spec.block_shape if d is not None)
    o_dtype = out_shape.dtype

    def body(*refs):
        if add is None:
            a_ref, b_ref, o_ref, acc_ref = refs
            add_ref = None
        else:
            a_ref, b_ref, add_ref, o_ref, acc_ref = refs
        k = pl.program_id(2)
        part = lax.dot_general(a_ref[...], b_ref[...], dims, preferred_element_type=F32)

        def finish(total):
            if add_ref is not None:
                total = total + add_ref[...]
            o_ref[...] = total.astype(o_dtype)

        if nk == 1:
            finish(part)
            return

        @pl.when(k == 0)
        def _():
            acc_ref[...] = part

        @pl.when(jnp.logical_and(k > 0, k < nk - 1))
        def _():
            acc_ref[...] += part

        @pl.when(k == nk - 1)
        def _():
            finish(acc_ref[...] + part)

    operands = [a, b] if add is None else [a, b, add]
    in_specs = [a_spec, b_spec] if add is None else [a_spec, b_spec, o_spec]
    return pl.pallas_call(
        body, name=name, grid=grid, in_specs=in_specs, out_specs=o_spec, out_shape=out_shape,
        scratch_shapes=[pltpu.VMEM(acc_shape, F32)],
        compiler_params=_params("parallel", "parallel", "arbitrary"),
    )(*operands)


def _mm_rows(out_dtype, has_add):
    return MM_ROWS if (jnp.dtype(out_dtype).itemsize == 2 and not has_add) else MM_ROWS // 2


def _mm_nn(a, w, l, out_dtype, name, *, col0=0, n=None, split_out=None):
    m, kdim = a.shape
    n = w.shape[2] if n is None else n
    tm, tk = _pick(m, _mm_rows(out_dtype, False), SUBLANES), _pick(kdim, 1024, LANES)
    tn = _pick(n if split_out is None else split_out, 1024, LANES)
    jb = col0 // tn
    grid = (m // tm, n // tn, kdim // tk)
    a_spec = pl.BlockSpec((tm, tk), lambda i, j, k: (i, k))
    b_spec = pl.BlockSpec((None, tk, tn), lambda i, j, k: (l, k, j + jb))
    if split_out is None:
        o_spec = pl.BlockSpec((tm, tn), lambda i, j, k: (i, j))
        shape = jax.ShapeDtypeStruct((m, n), out_dtype)
    else:
        nj1 = split_out // tn
        o_spec = pl.BlockSpec((None, tm, tn), lambda i, j, k: (j // nj1, i, j % nj1))
        shape = jax.ShapeDtypeStruct((n // split_out, m, split_out), out_dtype)
    return _mm(a, w, dims=NN, grid=grid, a_spec=a_spec, b_spec=b_spec, o_spec=o_spec, out_shape=shape, name=name)


def _mm_nt(a, w, l, out_dtype, name, *, add=None):
    n, kdim = w.shape[1], w.shape[2]
    if a.ndim == 2:
        m = a.shape[0]
        tk = _pick(kdim, 1024, LANES)
        a_spec_of = lambda tm: pl.BlockSpec((tm, tk), lambda i, j, k: (i, k))
    else:
        m, seg = a.shape[1], a.shape[2]
        tk = _pick(seg, 1024, LANES)
        nk1 = seg // tk
        a_spec_of = lambda tm: pl.BlockSpec((None, tm, tk), lambda i, j, k: (k // nk1, i, k % nk1))
    tm, tn = _pick(m, _mm_rows(out_dtype, add is not None), SUBLANES), _pick(n, 1024, LANES)
    grid = (m // tm, n // tn, kdim // tk)
    b_spec = pl.BlockSpec((None, tn, tk), lambda i, j, k: (l, j, k))
    o_spec = pl.BlockSpec((tm, tn), lambda i, j, k: (i, j))
    return _mm(a, w, dims=NT, grid=grid, a_spec=a_spec_of(tm), b_spec=b_spec, o_spec=o_spec,
               out_shape=jax.ShapeDtypeStruct((m, n), out_dtype), name=name, add=add)


def _mm_tn(a, b, out_dtype, name):
    kdim, m = a.shape
    tk, tm = _pick(kdim, 1024, SUBLANES), _pick(m, _mm_rows(out_dtype, False), LANES)
    if b.ndim == 2:
        n = b.shape[1]
        tn = _pick(n, 1024, LANES)
        b_spec = pl.BlockSpec((tk, tn), lambda i, j, k: (k, j))
    else:
        seg = b.shape[2]
        n = b.shape[0] * seg
        tn = _pick(seg, 1024, LANES)
        nj1 = seg // tn
        b_spec = pl.BlockSpec((None, tk, tn), lambda i, j, k: (j // nj1, k, j % nj1))
    grid = (m // tm, n // tn, kdim // tk)
    a_spec = pl.BlockSpec((tk, tm), lambda i, j, k: (k, i))
    o_spec = pl.BlockSpec((tm, tn), lambda i, j, k: (i, j))
    return _mm(a, b, dims=TN, grid=grid, a_spec=a_spec, b_spec=b_spec, o_spec=o_spec,
               out_shape=jax.ShapeDtypeStruct((m, n), out_dtype), name=name)


def _rstd(x):
    return lax.rsqrt(jnp.mean(x * x, axis=-1, keepdims=True) + RMS_EPS)


def _norm_mod_fwd(x, g, shift, scale, name):
    s, d = x.shape
    tm = _pick(s, 512, SUBLANES)

    def body(x_ref, g_ref, sh_ref, sc_ref, h_ref):
        xv = x_ref[...]
        y = (xv * _rstd(xv)) * g_ref[...]
        h_ref[...] = (y * (1.0 + sc_ref[...]) + sh_ref[...]).astype(BF16)

    row = pl.BlockSpec((tm, d), lambda i: (i, 0))
    vec = pl.BlockSpec((1, d), lambda i: (0, 0))
    return pl.pallas_call(body, name=name, grid=(s // tm,), in_specs=[row, vec, vec, vec], out_specs=row,
                          out_shape=jax.ShapeDtypeStruct((s, d), BF16), compiler_params=_params("parallel"))(x, g, shift, scale)


def _norm_mod_bwd(dh, x, g, scale, dres, name):
    s, d = x.shape
    tm = _pick(s, 512, SUBLANES)

    def body(dh_ref, x_ref, g_ref, sc_ref, dres_ref, dx_ref, dsh_ref, dsc_ref, dg_ref):
        i = pl.program_id(0)
        xv, dhv, gv = x_ref[...], dh_ref[...], g_ref[...]
        r = _rstd(xv)
        xh = xv * r
        dn = dhv * (1.0 + sc_ref[...])
        gd = dn * gv
        dx_ref[...] = dres_ref[...] + r * (gd - xh * jnp.mean(gd * xh, axis=-1, keepdims=True))

        @pl.when(i == 0)
        def _():
            dsh_ref[...] = jnp.zeros_like(dsh_ref)
            dsc_ref[...] = jnp.zeros_like(dsc_ref)
            dg_ref[...] = jnp.zeros_like(dg_ref)

        dsh_ref[...] += jnp.sum(dhv, axis=0, keepdims=True)
        dsc_ref[...] += jnp.sum(dhv * (xh * gv), axis=0, keepdims=True)
        dg_ref[...] += jnp.sum(dn * xh, axis=0, keepdims=True)

    row = pl.BlockSpec((tm, d), lambda i: (i, 0))
    vec = pl.BlockSpec((1, d), lambda i: (0, 0))
    vshape = jax.ShapeDtypeStruct((1, d), F32)
    return pl.pallas_call(body, name=name, grid=(s // tm,), in_specs=[row, row, vec, vec, row],
                          out_specs=[row, vec, vec, vec],
                          out_shape=[jax.ShapeDtypeStruct((s, d), F32), vshape, vshape, vshape],
                          compiler_params=_params("arbitrary"))(dh, x, g, scale, dres)


def _post_fwd(x, u, g, gate, name):
    s, d = x.shape
    tm = _pick(s, 512, SUBLANES)

    def body(x_ref, u_ref, g_ref, gt_ref, o_ref):
        uv = u_ref[...]
        o_ref[...] = x_ref[...] + gt_ref[...] * ((uv * _rstd(uv)) * g_ref[...])

    row = pl.BlockSpec((tm, d), lambda i: (i, 0))
    vec = pl.BlockSpec((1, d), lambda i: (0, 0))
    return pl.pallas_call(body, name=name, grid=(s // tm,), in_specs=[row, row, vec, vec], out_specs=row,
                          out_shape=jax.ShapeDtypeStruct((s, d), F32), compiler_params=_params("parallel"))(x, u, g, gate)


def _post_bwd(dx, u, g, gate, name):
    s, d = u.shape
    tm = _pick(s, 512, SUBLANES)

    def body(dx_ref, u_ref, g_ref, gt_ref, du_ref, dgt_ref, dg_ref):
        i = pl.program_id(0)
        uv, dxv, gv = u_ref[...], dx_ref[...], g_ref[...]
        r = _rstd(uv)
        uh = uv * r
        dn = dxv * gt_ref[...]
        gd = dn * gv
        du_ref[...] = (r * (gd - uh * jnp.mean(gd * uh, axis=-1, keepdims=True))).astype(BF16)

        @pl.when(i == 0)
        def _():
            dgt_ref[...] = jnp.zeros_like(dgt_ref)
            dg_ref[...] = jnp.zeros_like(dg_ref)

        dgt_ref[...] += jnp.sum(dxv * (uh * gv), axis=0, keepdims=True)
        dg_ref[...] += jnp.sum(dn * uh, axis=0, keepdims=True)

    row = pl.BlockSpec((tm, d), lambda i: (i, 0))
    vec = pl.BlockSpec((1, d), lambda i: (0, 0))
    vshape = jax.ShapeDtypeStruct((1, d), F32)
    return pl.pallas_call(body, name=name, grid=(s // tm,), in_specs=[row, row, vec, vec],
                          out_specs=[row, vec, vec],
                          out_shape=[jax.ShapeDtypeStruct((s, d), BF16), vshape, vshape],
                          compiler_params=_params("arbitrary"))(dx, u, g, gate)


def _shift_rows(cur, prev8, k):
    rolled = pltpu.roll(cur, k, axis=0)
    rolled_prev = pltpu.roll(prev8, k, axis=0)
    i8 = lax.broadcasted_iota(jnp.int32, prev8.shape, 0)
    top = jnp.where(i8 < k, rolled_prev, rolled[:SUBLANES])
    return jnp.concatenate([top, rolled[SUBLANES:]], axis=0)


def _conv_pre(g, prev8, wc_ref, bc_ref):
    s1 = _shift_rows(g, prev8, 1)
    s2 = _shift_rows(g, prev8, 2)
    gc = bc_ref[...] + wc_ref[0:1, :] * s2 + wc_ref[1:2, :] * s1 + wc_ref[2:3, :] * g
    return gc, s1, s2


def _conv_act_fwd(gp, up, wc, bc, name):
    s, f = gp.shape
    tm, tf = _pick(s, 512, SUBLANES), _pick(f, 768, LANES)
    rh = tm // HALO

    def body(g_ref, gprev_ref, up_ref, wc_ref, bc_ref, a_ref):
        i = pl.program_id(1)
        prev = jnp.where(i == 0, 0.0, gprev_ref[...].astype(F32)[HALO - SUBLANES:])
        gc, _, _ = _conv_pre(g_ref[...].astype(F32), prev, wc_ref, bc_ref)
        a_ref[...] = ((gc * jax.nn.sigmoid(gc)) * up_ref[...].astype(F32)).astype(BF16)

    tile = pl.BlockSpec((tm, tf), lambda j, i: (i, j))
    prev = pl.BlockSpec((HALO, tf), lambda j, i: (jnp.maximum(i * rh - 1, 0), j))
    return pl.pallas_call(body, name=name, grid=(f // tf, s // tm),
                          in_specs=[tile, prev, tile, pl.BlockSpec((3, tf), lambda j, i: (0, j)),
                                    pl.BlockSpec((1, tf), lambda j, i: (0, j))],
                          out_specs=tile, out_shape=jax.ShapeDtypeStruct((s, f), BF16),
                          compiler_params=_params("parallel", "parallel"))(gp, gp, up, wc, bc)


def _conv_act_bwd(da, gp, up, wc, bc, name):
    s, f = gp.shape
    tm, tf = _pick(s, 512, SUBLANES), _pick(f, 768, LANES)
    rh = tm // HALO
    nrow = s // tm

    def body(da_ref, dan_ref, g_ref, gprev_ref, gn_ref, up_ref, upn_ref, wc_ref, bc_ref,
             dgp_ref, dup_ref, dwc_ref, dbc_ref):
        i = pl.program_id(1)
        last = i == nrow - 1
        head = lambda ref: ref[...].astype(F32)[:SUBLANES]
        prev = jnp.where(i == 0, 0.0, gprev_ref[...].astype(F32)[HALO - SUBLANES:])
        g_ext = jnp.concatenate([g_ref[...].astype(F32), head(gn_ref)], axis=0)
        up_ext = jnp.concatenate([up_ref[...].astype(F32), head(upn_ref)], axis=0)
        da_ext = jnp.concatenate([da_ref[...].astype(F32), jnp.where(last, 0.0, head(dan_ref))], axis=0)
        gc, s1, s2 = _conv_pre(g_ext, prev, wc_ref, bc_ref)
        sg = jax.nn.sigmoid(gc)
        dup_ref[...] = (da_ext * (gc * sg))[:tm].astype(BF16)
        dgc = da_ext * up_ext * (sg * (1.0 + gc * (1.0 - sg)))
        ext = tm + SUBLANES
        dgp = (wc_ref[2:3, :] * dgc + wc_ref[1:2, :] * pltpu.roll(dgc, ext - 1, axis=0)
               + wc_ref[0:1, :] * pltpu.roll(dgc, ext - 2, axis=0))
        dgp_ref[...] = dgp[:tm].astype(BF16)

        @pl.when(i == 0)
        def _():
            dwc_ref[...] = jnp.zeros_like(dwc_ref)
            dbc_ref[...] = jnp.zeros_like(dbc_ref)

        d0 = dgc[:tm]
        dwc_ref[0:1, :] += jnp.sum(d0 * s2[:tm], axis=0, keepdims=True)
        dwc_ref[1:2, :] += jnp.sum(d0 * s1[:tm], axis=0, keepdims=True)
        dwc_ref[2:3, :] += jnp.sum(d0 * g_ext[:tm], axis=0, keepdims=True)
        dbc_ref[...] += jnp.sum(d0, axis=0, keepdims=True)

    tile = pl.BlockSpec((tm, tf), lambda j, i: (i, j))
    prev = pl.BlockSpec((HALO, tf), lambda j, i: (jnp.maximum(i * rh - 1, 0), j))
    nxt = pl.BlockSpec((HALO, tf), lambda j, i: (jnp.minimum((i + 1) * rh, s // HALO - 1), j))
    return pl.pallas_call(body, name=name, grid=(f // tf, nrow),
                          in_specs=[tile, nxt, tile, prev, nxt, tile, nxt,
                                    pl.BlockSpec((3, tf), lambda j, i: (0, j)), pl.BlockSpec((1, tf), lambda j, i: (0, j))],
                          out_specs=[tile, tile, pl.BlockSpec((3, tf), lambda j, i: (0, j)),
                                     pl.BlockSpec((1, tf), lambda j, i: (0, j))],
                          out_shape=[jax.ShapeDtypeStruct((s, f), BF16), jax.ShapeDtypeStruct((s, f), BF16),
                                     jax.ShapeDtypeStruct((3, f), F32), jax.ShapeDtypeStruct((1, f), F32)],
                          compiler_params=_params("parallel", "arbitrary"))(da, da, gp, gp, gp, up, up, wc, bc)


def _split_bf16(v, parts):
    out, rem = [], v
    for _ in range(parts):
        t = rem.astype(BF16)
        out.append(t)
        rem = rem - t.astype(F32)
    return jnp.concatenate(out, axis=1)


def _tri(t, cmp, reps):
    row = lax.broadcasted_iota(jnp.int32, (t, t), 0)
    col = lax.broadcasted_iota(jnp.int32, (t, t), 1)
    m = cmp(row, col).astype(BF16)
    return jnp.concatenate([m] * reps, axis=0)


def _log_sigmoid(z):
    mn = jnp.minimum(z, 0.0)
    return mn - jnp.log(1.0 + jnp.exp(mn + (mn - z)))


def _log_one_minus_sigmoid(z):
    mn = jnp.minimum(z, 0.0)
    neg = mn - z
    return neg - jnp.log(1.0 + jnp.exp(mn + neg))


def _pair(xv):
    lane = lax.broadcasted_iota(jnp.int32, xv.shape, 1)
    zero = jnp.zeros_like(xv)
    return jnp.where(lane < HEAD_DIM, xv, zero), jnp.where(lane < HEAD_DIM, zero, xv)


def _attn_specs(s, tq):
    q_spec = pl.BlockSpec((None, tq, LANES), lambda hp, qi: (0, qi, hp))
    k_spec = pl.BlockSpec((None, s, LANES), lambda hp, qi: (1, 0, hp))
    v_spec = pl.BlockSpec((None, s, LANES), lambda hp, qi: (2, 0, hp))
    return q_spec, k_spec, v_spec


def _rel_mask(tq, tk, i, strict):
    row = lax.broadcasted_iota(jnp.int32, (tq, tk), 0)
    col = lax.broadcasted_iota(jnp.int32, (tq, tk), 1) + i * tk
    return col < row if strict else col <= row


def _key_bound(qkv, name):
    _, s, d = qkv.shape

    def body(k_ref, o_ref):
        sq = jnp.square(k_ref[...].astype(F32))
        low = lax.broadcasted_iota(jnp.int32, sq.shape, 1) < HEAD_DIM
        n0 = jnp.max(jnp.sum(jnp.where(low, sq, 0.0), axis=1, keepdims=True))
        n1 = jnp.max(jnp.sum(jnp.where(low, 0.0, sq), axis=1, keepdims=True))
        low8 = lax.broadcasted_iota(jnp.int32, (SUBLANES, LANES), 1) < HEAD_DIM
        o_ref[...] = jnp.sqrt(jnp.where(low8, n0, n1))

    return pl.pallas_call(body, name=name, grid=(d // LANES,),
                          in_specs=[pl.BlockSpec((None, s, LANES), lambda hp: (1, 0, hp))],
                          out_specs=pl.BlockSpec((None, SUBLANES, LANES), lambda hp: (hp, 0, 0)),
                          out_shape=jax.ShapeDtypeStruct((d // LANES, SUBLANES, LANES), F32),
                          compiler_params=_params("parallel"))(qkv)


def _sb_fwd(qkv, kbound, tq, tk, name, rider=None):
    _, s, d = qkv.shape
    hp_n, nq, nd = d // LANES, s // tq, tq // tk
    scale = HEAD_DIM ** -0.5

    def body(q_ref, k_ref, v_ref, kb_ref, o_ref, lt_ref, nb_ref):
        qi = pl.program_id(1)
        lane = lax.broadcasted_iota(jnp.int32, (tq, LANES), 1)
        onward = _tri(tk, lambda j, sidx: j >= sidx, SUM_TERMS)
        qpair = _pair(q_ref[...] * scale)
        kbv = kb_ref[...]
        z_max = [jnp.sqrt(jnp.sum(jnp.square(qh.astype(F32)), axis=1, keepdims=True)) * (1.01 * km)
                 for qh, km in zip(qpair, (kbv[0:1, 0:1], kbv[0:1, HEAD_DIM:HEAD_DIM + 1]))]

        def alive(runs):
            return jnp.logical_or(jnp.max(runs[0] + z_max[0]) >= EXP_UNDERFLOW,
                                  jnp.max(runs[1] + z_max[1]) >= EXP_UNDERFLOW)

        def block(kb, carry, mask):
            acc, runs = carry[0], carry[1:]
            ks = pl.multiple_of(kb * tk, tk)
            kblk = k_ref[pl.ds(ks, tk), :]
            vpair = _pair(v_ref[pl.ds(ks, tk), :])
            probs, new_runs = [], []
            for qh, run in zip(qpair, runs):
                z = lax.dot_general(qh, kblk, NT, preferred_element_type=F32)
                l1 = _log_one_minus_sigmoid(z)
                if mask is not None:
                    l1 = jnp.where(mask, l1, 0.0)
                c = jnp.dot(_split_bf16(l1, SUM_TERMS), onward, preferred_element_type=F32)
                a = jnp.exp(z + c + run)
                if mask is not None:
                    a = jnp.where(mask, a, 0.0)
                probs.append(a.astype(BF16))
                new_runs.append(run + jnp.sum(l1, axis=1, keepdims=True))
            acc = acc + jnp.dot(jnp.concatenate(probs, axis=1), jnp.concatenate(vpair, axis=0),
                                preferred_element_type=F32)
            return (acc, *new_runs)

        zero1 = jnp.zeros((tq, 1), F32)
        carry = (jnp.zeros((tq, LANES), F32), zero1, zero1)
        for i in reversed(range(nd)):
            carry = block(qi * nd + i, carry, _rel_mask(tq, tk, i, True))

        def step(state):
            new = block(qi * nd - 1 - state[0], state[2:], None)
            return (state[0] + 1, alive(new[1:]), *new)

        done, _, acc, run0, run1 = lax.while_loop(lambda st: jnp.logical_and(st[0] < qi * nd, st[1]), step,
                                                  (jnp.int32(0), alive(carry[1:]), *carry))
        o_ref[...] = acc.astype(BF16)
        lt_ref[...] = jnp.where(lane < HEAD_DIM, run0, run1)
        nb_ref[...] = jnp.zeros((SUBLANES, LANES), F32) + done.astype(F32)

    q_spec, k_spec, v_spec = _attn_specs(s, tq)
    return _attn_call(
        body, name, (hp_n, nq),
        [q_spec, k_spec, v_spec, pl.BlockSpec((None, SUBLANES, LANES), lambda hp, qi: (hp, 0, 0))],
        [pl.BlockSpec((tq, LANES), lambda hp, qi: (qi, hp)),
         pl.BlockSpec((None, tq, LANES), lambda hp, qi: (hp, qi, 0)),
         pl.BlockSpec((None, None, SUBLANES, LANES), lambda hp, qi: (hp, qi, 0, 0))],
        [jax.ShapeDtypeStruct((s, d), BF16), jax.ShapeDtypeStruct((hp_n, s, LANES), F32),
         jax.ShapeDtypeStruct((hp_n, nq, SUBLANES, LANES), F32)],
        [], ("parallel", "parallel"), (qkv, qkv, qkv, kbound), rider)


def _sb_bwd(qkv, ltot, visited, do, tq, tk, name, rider=None):
    _, s, d = qkv.shape
    hp_n, nq, nd = d // LANES, s // tq, tq // tk
    scale = HEAD_DIM ** -0.5

    def body(q_ref, k_ref, v_ref, lt_ref, nb_ref, do_ref, dqkv_ref, dk_acc, dv_acc):
        qi = pl.program_id(1)
        first = qi * nd - jnp.max(nb_ref[...]).astype(jnp.int32)

        @pl.when(qi == 0)
        def _():
            dk_acc[...] = jnp.zeros_like(dk_acc)
            dv_acc[...] = jnp.zeros_like(dv_acc)

        onward = _tri(tk, lambda j, sidx: j >= sidx, SUM_TERMS)
        before = _tri(tk, lambda j, sidx: j < sidx, SUM_TERMS)
        qpair = _pair(q_ref[...] * scale)
        dopair = _pair(do_ref[...])
        ltv = lt_ref[...]
        lts = (ltv[:, 0:1], ltv[:, HEAD_DIM:HEAD_DIM + 1])

        def block(kb, carry, mask):
            dq, lpres, gpres = carry[0], carry[1:3], carry[3:5]
            ks = pl.multiple_of(kb * tk, tk)
            kblk, vblk = k_ref[pl.ds(ks, tk), :], v_ref[pl.ds(ks, tk), :]
            kpair = _pair(kblk * scale)
            dzs, probs, new_l, new_g = [], [], [], []
            for qh, doh, lt, lpre, gpre in zip(qpair, dopair, lts, lpres, gpres):
                z = lax.dot_general(qh, kblk, NT, preferred_element_type=F32)
                l1 = _log_one_minus_sigmoid(z)
                if mask is not None:
                    l1 = jnp.where(mask, l1, 0.0)
                lpre = lpre + jnp.sum(l1, axis=1, keepdims=True)
                c = jnp.dot(_split_bf16(l1, SUM_TERMS), onward, preferred_element_type=F32)
                a = jnp.exp(z + c + (lt - lpre))
                if mask is not None:
                    a = jnp.where(mask, a, 0.0)
                da = lax.dot_general(doh, vblk, NT, preferred_element_type=F32)
                g = da * a
                p = jnp.dot(_split_bf16(g, SUM_TERMS), before, preferred_element_type=F32) + gpre
                dz = g - jnp.exp(z + l1) * (g + p)
                if mask is not None:
                    dz = jnp.where(mask, dz, 0.0)
                dzs.append(dz.astype(BF16))
                probs.append(a.astype(BF16))
                new_l.append(lpre)
                new_g.append(gpre + jnp.sum(g, axis=1, keepdims=True))
            dq = dq + jnp.dot(jnp.concatenate(dzs, axis=1), jnp.concatenate(kpair, axis=0),
                              preferred_element_type=F32)
            dk_acc[pl.ds(ks, tk), :] += lax.dot_general(jnp.concatenate(dzs, axis=0), jnp.concatenate(qpair, axis=0),
                                                        TN, preferred_element_type=F32)
            dv_acc[pl.ds(ks, tk), :] += lax.dot_general(jnp.concatenate(probs, axis=0), jnp.concatenate(dopair, axis=0),
                                                        TN, preferred_element_type=F32)
            return (dq, *new_l, *new_g)

        zero1 = jnp.zeros((tq, 1), F32)
        init = (jnp.zeros((tq, LANES), F32), zero1, zero1, zero1, zero1)
        carry = lax.fori_loop(first, qi * nd, lambda j, cr: block(j, cr, None), init)
        for i in range(nd):
            carry = block(qi * nd + i, carry, _rel_mask(tq, tk, i, True))
        dqkv_ref[0, pl.ds(pl.multiple_of(qi * tq, tq), tq), :] = carry[0].astype(BF16)

        @pl.when(qi == nq - 1)
        def _():
            dqkv_ref[1] = dk_acc[...].astype(BF16)
            dqkv_ref[2] = dv_acc[...].astype(BF16)

    q_spec, k_spec, v_spec = _attn_specs(s, tq)
    return _attn_call(
        body, name, (hp_n, nq),
        [q_spec, k_spec, v_spec, pl.BlockSpec((None, tq, LANES), lambda hp, qi: (hp, qi, 0)),
         pl.BlockSpec((None, None, SUBLANES, LANES), lambda hp, qi: (hp, qi, 0, 0)),
         pl.BlockSpec((tq, LANES), lambda hp, qi: (qi, hp))],
        [pl.BlockSpec((3, s, LANES), lambda hp, qi: (0, 0, hp))], [jax.ShapeDtypeStruct((3, s, d), BF16)],
        [pltpu.VMEM((s, LANES), F32), pltpu.VMEM((s, LANES), F32)],
        ("parallel", "arbitrary"), (qkv, qkv, qkv, ltot, visited, do), rider)


def _fox_prep(fl, bias, name):
    s, w = fl.shape
    tb = _pick(s, 512, SUBLANES)

    def body(fl_ref, b_ref, cum_ref, carry_ref):
        i = pl.program_id(0)

        @pl.when(i == 0)
        def _():
            carry_ref[...] = jnp.zeros_like(carry_ref)

        logf = _log_sigmoid(fl_ref[...] + b_ref[...])
        row = lax.broadcasted_iota(jnp.int32, (tb, tb), 0)
        col = lax.broadcasted_iota(jnp.int32, (tb, tb), 1)
        incl = (col <= row).astype(BF16)
        tot = carry_ref[...]
        rem = logf
        for _ in range(3):
            part = rem.astype(BF16)
            tot = tot + jnp.dot(incl, part, preferred_element_type=F32)
            rem = rem - part.astype(F32)
        cum_ref[...] = tot
        carry_ref[...] = tot[tb - 1:tb, :]

    blk = pl.BlockSpec((tb, w), lambda i: (i, 0))
    return pl.pallas_call(body, name=name, grid=(s // tb,), in_specs=[blk, pl.BlockSpec((1, w), lambda i: (0, 0))],
                          out_specs=blk, out_shape=jax.ShapeDtypeStruct((s, w), F32),
                          scratch_shapes=[pltpu.VMEM((1, w), F32)], compiler_params=_params("arbitrary"))(fl, bias)


def _fox_gate_bwd(dcum, fl, bias, n_heads, name):
    s, w = fl.shape
    tb = _pick(s, 512, SUBLANES)
    nb = s // tb

    def body(dc_ref, fl_ref, b_ref, dfl_ref, db_ref, carry_ref):
        i = pl.program_id(0)

        @pl.when(i == 0)
        def _():
            carry_ref[...] = jnp.zeros_like(carry_ref)
            db_ref[...] = jnp.zeros_like(db_ref)

        row = lax.broadcasted_iota(jnp.int32, (tb, tb), 0)
        col = lax.broadcasted_iota(jnp.int32, (tb, tb), 1)
        incl = (col >= row).astype(BF16)
        tot = jnp.broadcast_to(carry_ref[...], (tb, w))
        rem = dc_ref[...]
        for _ in range(3):
            part = rem.astype(BF16)
            tot = tot + jnp.dot(incl, part, preferred_element_type=F32)
            rem = rem - part.astype(F32)
        carry_ref[...] = tot[0:1, :]
        xg = fl_ref[...] + b_ref[...]
        e = jnp.exp(-jnp.abs(xg))
        sig_neg = jnp.where(xg >= 0.0, e, 1.0) / (1.0 + e)
        lane = lax.broadcasted_iota(jnp.int32, (tb, w), 1)
        dfl = jnp.where(lane < n_heads, tot * sig_neg, 0.0)
        dfl_ref[...] = dfl.astype(BF16)
        db_ref[...] += jnp.sum(dfl, axis=0, keepdims=True)

    blk = pl.BlockSpec((tb, w), lambda i: (nb - 1 - i, 0))
    vec = pl.BlockSpec((1, w), lambda i: (0, 0))
    return pl.pallas_call(body, name=name, grid=(nb,), in_specs=[blk, blk, vec], out_specs=[blk, vec],
                          out_shape=[jax.ShapeDtypeStruct((s, w), BF16), jax.ShapeDtypeStruct((1, w), F32)],
                          scratch_shapes=[pltpu.VMEM((1, w), F32)], compiler_params=_params("arbitrary"))(dcum, fl, bias)


def _head_columns(cum_blk, hp):
    lane = lax.broadcasted_iota(jnp.int32, cum_blk.shape, 1)
    return tuple(jnp.sum(jnp.where(lane == 2 * hp + h, cum_blk, 0.0), axis=1, keepdims=True) for h in range(2))


def _fox_fwd(qkv, kbound, cum, ck, tq, tk, name, rider=None):
    _, s, d = qkv.shape
    hp_n, nq, nd = d // LANES, s // tq, tq // tk
    scale = HEAD_DIM ** -0.5

    def body(q_ref, k_ref, v_ref, kb_ref, cq_ref, ck_ref, o_ref, lse_ref, nb_ref):
        qi = pl.program_id(1)
        low = lax.broadcasted_iota(jnp.int32, (tq, LANES), 1) < HEAD_DIM
        qpair = _pair(q_ref[...] * scale)
        cqs = _head_columns(cq_ref[...], pl.program_id(0))
        kbv = kb_ref[...]
        reach = [jnp.sqrt(jnp.sum(jnp.square(qh.astype(F32)), axis=1, keepdims=True)) * (1.01 * km) + cqh
                 for qh, km, cqh in zip(qpair, (kbv[0:1, 0:1], kbv[0:1, HEAD_DIM:HEAD_DIM + 1]), cqs)]

        def alive(stats, kb):
            ks = pl.multiple_of(kb * tk, tk)
            return functools.reduce(jnp.logical_or, [
                jnp.max(reach[h] - ck_ref[h:h + 1, pl.ds(ks, tk)][:, 0:1] - stats[2 * h]) >= EXP_UNDERFLOW
                for h in range(2)])

        def block(kb, carry, mask):
            acc, stats = carry[0], carry[1:]
            ks = pl.multiple_of(kb * tk, tk)
            kblk = k_ref[pl.ds(ks, tk), :]
            vpair = _pair(v_ref[pl.ds(ks, tk), :])
            probs, alphas, new = [], [], []
            for h, (qh, cqh) in enumerate(zip(qpair, cqs)):
                m, lsum = stats[2 * h], stats[2 * h + 1]
                z = lax.dot_general(qh, kblk, NT, preferred_element_type=F32)
                sc = z + (cqh - ck_ref[h:h + 1, pl.ds(ks, tk)])
                if mask is not None:
                    sc = jnp.where(mask, sc, -jnp.inf)
                m_new = jnp.maximum(m, jnp.max(sc, axis=1, keepdims=True))
                alpha = jnp.exp(m - m_new)
                p = jnp.exp(sc - m_new)
                new += [m_new, alpha * lsum + jnp.sum(p, axis=1, keepdims=True)]
                probs.append(p.astype(BF16))
                alphas.append(alpha)
            acc = jnp.where(low, alphas[0], alphas[1]) * acc + jnp.dot(
                jnp.concatenate(probs, axis=1), jnp.concatenate(vpair, axis=0), preferred_element_type=F32)
            return (acc, *new)

        neg = jnp.full((tq, 1), -jnp.inf, F32)
        zero1 = jnp.zeros((tq, 1), F32)
        carry = (jnp.zeros((tq, LANES), F32), neg, zero1, neg, zero1)
        for i in range(nd):
            carry = block(qi * nd + i, carry, _rel_mask(tq, tk, i, False))

        def step(state):
            kb = qi * nd - 1 - state[0]
            new = block(kb, state[2:], None)
            return (state[0] + 1, alive(new[1:], kb), *new)

        done, _, acc, m0, l0, m1, l1 = lax.while_loop(lambda st: jnp.logical_and(st[0] < qi * nd, st[1]), step,
                                                      (jnp.int32(0), alive(carry[1:], qi * nd), *carry))
        o_ref[...] = (acc / jnp.where(low, l0, l1)).astype(BF16)
        lse_ref[...] = jnp.where(low, m0 + jnp.log(l0), m1 + jnp.log(l1))
        nb_ref[...] = jnp.zeros((SUBLANES, LANES), F32) + done.astype(F32)

    q_spec, k_spec, v_spec = _attn_specs(s, tq)
    pair_rows = pl.BlockSpec((None, tq, LANES), lambda hp, qi: (hp, qi, 0))
    return _attn_call(
        body, name, (hp_n, nq),
        [q_spec, k_spec, v_spec, pl.BlockSpec((None, SUBLANES, LANES), lambda hp, qi: (hp, 0, 0)),
         pl.BlockSpec((tq, LANES), lambda hp, qi: (qi, 0)), pl.BlockSpec((None, 2, s), lambda hp, qi: (hp, 0, 0))],
        [pl.BlockSpec((tq, LANES), lambda hp, qi: (qi, hp)), pair_rows,
         pl.BlockSpec((None, None, SUBLANES, LANES), lambda hp, qi: (hp, qi, 0, 0))],
        [jax.ShapeDtypeStruct((s, d), BF16), jax.ShapeDtypeStruct((hp_n, s, LANES), F32),
         jax.ShapeDtypeStruct((hp_n, nq, SUBLANES, LANES), F32)],
        [], ("parallel", "parallel"), (qkv, qkv, qkv, kbound, cum, ck), rider)


def _fox_bwd(qkv, o, do, lse, visited, cum, ck, tq, tk, name, rider=None):
    _, s, d = qkv.shape
    hp_n, nq, nd = d // LANES, s // tq, tq // tk
    scale = HEAD_DIM ** -0.5

    def body(q_ref, k_ref, v_ref, o_ref, do_ref, lse_ref, nb_ref, cq_ref, ck_ref, dqkv_ref, dcq_ref, dck_ref,
             dk_acc, dv_acc):
        qi = pl.program_id(1)
        first = qi * nd - jnp.max(nb_ref[...]).astype(jnp.int32)

        @pl.when(qi == 0)
        def _():
            dk_acc[...] = jnp.zeros_like(dk_acc)
            dv_acc[...] = jnp.zeros_like(dv_acc)
            dck_ref[...] = jnp.zeros_like(dck_ref)

        low = lax.broadcasted_iota(jnp.int32, (tq, LANES), 1) < HEAD_DIM
        dov, lsev = do_ref[...], lse_ref[...]
        qpair = _pair(q_ref[...] * scale)
        dopair = _pair(dov)
        prod = dov.astype(F32) * o_ref[...].astype(F32)
        deltas = (jnp.sum(jnp.where(low, prod, 0.0), axis=1, keepdims=True),
                  jnp.sum(jnp.where(low, 0.0, prod), axis=1, keepdims=True))
        cqs = _head_columns(cq_ref[...], pl.program_id(0))
        lses = (lsev[:, 0:1], lsev[:, HEAD_DIM:HEAD_DIM + 1])

        def block(kb, carry, mask):
            dq, rowsums = carry[0], carry[1:]
            ks = pl.multiple_of(kb * tk, tk)
            kblk, vblk = k_ref[pl.ds(ks, tk), :], v_ref[pl.ds(ks, tk), :]
            kpair = _pair(kblk * scale)
            dss, probs, new_rows = [], [], []
            for h, (qh, doh) in enumerate(zip(qpair, dopair)):
                z = lax.dot_general(qh, kblk, NT, preferred_element_type=F32)
                sc = z + (cqs[h] - ck_ref[h:h + 1, pl.ds(ks, tk)])
                p = jnp.exp(sc - lses[h])
                if mask is not None:
                    p = jnp.where(mask, p, 0.0)
                dp = lax.dot_general(doh, vblk, NT, preferred_element_type=F32)
                ds = p * (dp - deltas[h])
                dck_ref[h:h + 1, pl.ds(ks, tk)] -= jnp.sum(ds, axis=0, keepdims=True)
                new_rows.append(rowsums[h] + jnp.sum(ds, axis=1, keepdims=True))
                dss.append(ds.astype(BF16))
                probs.append(p.astype(BF16))
            dq = dq + jnp.dot(jnp.concatenate(dss, axis=1), jnp.concatenate(kpair, axis=0),
                              preferred_element_type=F32)
            dk_acc[pl.ds(ks, tk), :] += lax.dot_general(jnp.concatenate(dss, axis=0), jnp.concatenate(qpair, axis=0),
                                                       TN, preferred_element_type=F32)
            dv_acc[pl.ds(ks, tk), :] += lax.dot_general(jnp.concatenate(probs, axis=0), jnp.concatenate(dopair, axis=0),
                                                       TN, preferred_element_type=F32)
            return (dq, *new_rows)

        zero1 = jnp.zeros((tq, 1), F32)
        carry = lax.fori_loop(first, qi * nd, lambda j, cr: block(j, cr, None),
                              (jnp.zeros((tq, LANES), F32), zero1, zero1))
        for i in range(nd):
            carry = block(qi * nd + i, carry, _rel_mask(tq, tk, i, False))
        dq, rs0, rs1 = carry
        dqkv_ref[0, pl.ds(pl.multiple_of(qi * tq, tq), tq), :] = dq.astype(BF16)
        dcq_ref[...] = jnp.where(low, rs0, rs1)

        @pl.when(qi == nq - 1)
        def _():
            dqkv_ref[1] = dk_acc[...].astype(BF16)
            dqkv_ref[2] = dv_acc[...].astype(BF16)

    q_spec, k_spec, v_spec = _attn_specs(s, tq)
    pair_rows = pl.BlockSpec((None, tq, LANES), lambda hp, qi: (hp, qi, 0))
    tile = pl.BlockSpec((tq, LANES), lambda hp, qi: (qi, hp))
    keys = pl.BlockSpec((None, 2, s), lambda hp, qi: (hp, 0, 0))
    return _attn_call(
        body, name, (hp_n, nq),
        [q_spec, k_spec, v_spec, tile, tile, pair_rows,
         pl.BlockSpec((None, None, SUBLANES, LANES), lambda hp, qi: (hp, qi, 0, 0)),
         pl.BlockSpec((tq, LANES), lambda hp, qi: (qi, 0)), keys],
        [pl.BlockSpec((3, s, LANES), lambda hp, qi: (0, 0, hp)), pair_rows, keys],
        [jax.ShapeDtypeStruct((3, s, d), BF16), jax.ShapeDtypeStruct((hp_n, s, LANES), F32),
         jax.ShapeDtypeStruct((hp_n, 2, s), F32)],
        [pltpu.VMEM((s, LANES), F32), pltpu.VMEM((s, LANES), F32)],
        ("parallel", "arbitrary"), (qkv, qkv, qkv, o, do, lse, visited, cum, ck), rider)


def _loss_head(y, target, name):
    s, d = y.shape
    tm = _pick(s, 512, SUBLANES)

    def body(y_ref, t_ref, dy_ref, sq_ref):
        i = pl.program_id(0)
        diff = y_ref[...] - t_ref[...]
        dy_ref[...] = diff / d

        @pl.when(i == 0)
        def _():
            sq_ref[...] = jnp.zeros_like(sq_ref)

        sq_ref[...] += jnp.sum(diff * diff, axis=0, keepdims=True)

    row = pl.BlockSpec((tm, d), lambda i: (i, 0))
    vec = pl.BlockSpec((1, d), lambda i: (0, 0))
    return pl.pallas_call(body, name=name, grid=(s // tm,), in_specs=[row, row], out_specs=[row, vec],
                          out_shape=[jax.ShapeDtypeStruct((s, d), F32), jax.ShapeDtypeStruct((1, d), F32)],
                          compiler_params=_params("arbitrary"))(y, target)


def _mod_fwd(c_all, w_mod, b_mod_cols, name):
    nl, d, cols = w_mod.shape
    nb = c_all.shape[0]

    def body(c_ref, w_ref, b_ref, o_ref):
        cv = c_ref[...]
        act = (cv * jax.nn.sigmoid(cv)).astype(BF16)
        o_ref[...] = jnp.dot(act, w_ref[...].astype(BF16), preferred_element_type=F32) + b_ref[...]

    return pl.pallas_call(
        body, name=name, grid=(nl,),
        in_specs=[pl.BlockSpec((nb, d), lambda l: (0, 0)), pl.BlockSpec((None, d, cols), lambda l: (l, 0, 0)),
                  pl.BlockSpec((None, 1, cols), lambda l: (l, 0, 0))],
        out_specs=pl.BlockSpec((None, nb, cols), lambda l: (l, 0, 0)),
        out_shape=jax.ShapeDtypeStruct((nl, nb, cols), F32), compiler_params=_params("parallel"))(c_all, w_mod, b_mod_cols)


def _wmod_grad(c_all_t, dmod, name):
    d, nb = c_all_t.shape
    _, nl, cols = dmod.shape

    def body(c_ref, dm_ref, o_ref):
        cv = c_ref[...]
        act = cv * jax.nn.sigmoid(cv)
        tot = act[:, 0:1] * dm_ref[0]
        for b in range(1, nb):
            tot = tot + act[:, b:b + 1] * dm_ref[b]
        o_ref[...] = tot

    return pl.pallas_call(
        body, name=name, grid=(nl,),
        in_specs=[pl.BlockSpec((d, nb), lambda l: (0, 0)), pl.BlockSpec((nb, None, 1, cols), lambda l: (0, l, 0, 0))],
        out_specs=pl.BlockSpec((None, d, cols), lambda l: (l, 0, 0)),
        out_shape=jax.ShapeDtypeStruct((nl, d, cols), F32), compiler_params=_params("parallel"))(
            c_all_t, dmod.reshape(nb, nl, 1, cols))


def _adamw(recv, w, m, v, name, *, tr=256, tc=512):
    nq, nl, rp, cp = recv.shape
    _, r, c = w.shape
    br = _pick(r, tr, SUBLANES) if rp == r else r
    bc = _pick(c, tc, LANES) if cp == c else c
    rbr = br if rp == r else rp
    rbc = bc if cp == c else cp

    def body(rv_ref, w_ref, m_ref, v_ref, g_ref, d_ref, nm_ref, nv_ref):
        g = rv_ref[0, :br, :bc].astype(F32)
        for qd in range(1, nq):
            g = g + rv_ref[qd, :br, :bc].astype(F32)
        m_new = ADAM_B1 * m_ref[...] + (1.0 - ADAM_B1) * g
        v_new = ADAM_B2 * v_ref[...] + (1.0 - ADAM_B2) * jnp.square(g)
        m_hat = m_new / (1.0 - ADAM_B1 ** ADAM_STEP)
        v_hat = v_new / (1.0 - ADAM_B2 ** ADAM_STEP)
        g_ref[...] = g
        d_ref[...] = -ADAM_LR * (m_hat / (jnp.sqrt(v_hat) + ADAM_EPS) + ADAM_WD * w_ref[...])
        nm_ref[...] = m_new
        nv_ref[...] = v_new

    blk = pl.BlockSpec((None, br, bc), lambda l, i, j: (l, i, j))
    rblk = pl.BlockSpec((nq, None, rbr, rbc), lambda l, i, j: (0, l, i, j))
    shape = jax.ShapeDtypeStruct(w.shape, F32)
    return pl.pallas_call(body, name=name, grid=(nl, r // br, c // bc), in_specs=[rblk, blk, blk, blk],
                          out_specs=[blk] * 4, out_shape=[shape] * 4,
                          compiler_params=_params("parallel", "parallel", "parallel"))(recv, w, m, v)


def _mesh_pos():
    return lax.axis_index("x"), lax.axis_index("y"), lax.axis_index("c")


def _allgather_small(block, name):
    m_per, n = block.shape

    def body(x_ref, out_ref, send_sems, recv_sems, local_sem):
        x, y, c = _mesh_pos()
        me, sibling = (x, y, c), (x, y, 1 - c)
        chips = [(1 - x, y), (x, 1 - y), (1 - x, 1 - y)]

        def rows(px, py, pc):
            return out_ref.at[pl.ds((4 * px + 2 * py + pc) * m_per, m_per), :]

        def copy(k, blk, to, src=None):
            return pltpu.make_async_remote_copy(
                src_ref=rows(*blk) if src is None else src, dst_ref=rows(*blk),
                send_sem=send_sems.at[k], recv_sem=recv_sems.at[k], device_id=to, device_id_type=MESH)

        mine = pltpu.make_async_copy(x_ref, rows(*me), local_sem)
        mine.start()
        first = [copy(0, me, sibling, src=x_ref)]
        first += [copy(1 + j, me, (*chip, c), src=x_ref) for j, chip in enumerate(chips)]
        for cp in first:
            cp.start()
        passed = [copy(4 + j, (*chip, c), sibling) for j, chip in enumerate(chips)]
        for j, chip in enumerate(chips):
            copy(1 + j, (*chip, c), me).wait_recv()
            passed[j].start()
        copy(0, sibling, me).wait_recv()
        for j, chip in enumerate(chips):
            copy(4 + j, (*chip, 1 - c), me).wait_recv()
        for cp in first + passed:
            cp.wait_send()
        mine.wait()

    return pl.pallas_call(
        body, name=name, out_shape=jax.ShapeDtypeStruct((N_DEV * m_per, n), block.dtype),
        in_specs=[pl.BlockSpec(memory_space=pltpu.VMEM)], out_specs=pl.BlockSpec(memory_space=pltpu.VMEM),
        scratch_shapes=[pltpu.SemaphoreType.DMA((7,)), pltpu.SemaphoreType.DMA((7,)), pltpu.SemaphoreType.DMA],
        compiler_params=pltpu.CompilerParams(vmem_limit_bytes=VMEM_LIMIT_BYTES))(block)


def _exchange(srcs, dsts, jobs, name):
    n_src, n_job, n_dst = len(srcs), len(jobs), len(dsts)

    def body(*refs):
        src_refs, dst_refs = refs[:n_src], refs[n_src + n_dst:n_src + 2 * n_dst]
        send_sems, recv_sems, local_sems = refs[n_src + 2 * n_dst:]
        x, y, c = _mesh_pos()
        me = 4 * x + 2 * y + c
        pending = []
        for t, (si, di, src_slice, dst_slice) in enumerate(jobs):
            src, dst = src_refs[si], dst_refs[di]
            lc = pltpu.make_async_copy(src_slice(src, me), dst_slice(dst, me), local_sems.at[t])
            lc.start()
            pending.append(lc)
            for dd in range(1, N_DEV):
                px = 1 - x if dd & 4 else x
                py = 1 - y if dd & 2 else y
                pc = 1 - c if dd & 1 else c
                cp = pltpu.make_async_remote_copy(
                    src_ref=src_slice(src, 4 * px + 2 * py + pc), dst_ref=dst_slice(dst, me),
                    send_sem=send_sems.at[t, dd - 1], recv_sem=recv_sems.at[t, dd - 1],
                    device_id=(px, py, pc), device_id_type=MESH)
                cp.start()
                pending.append(cp)
        for cp in pending:
            cp.wait()

    hbm = pl.BlockSpec(memory_space=pl.ANY)
    return pl.pallas_call(
        body, name=name, out_shape=[jax.ShapeDtypeStruct(a.shape, a.dtype) for a in dsts],
        in_specs=[hbm] * (n_src + n_dst), out_specs=[hbm] * n_dst,
        input_output_aliases={n_src + i: i for i in range(n_dst)},
        scratch_shapes=[pltpu.SemaphoreType.DMA((n_job, N_DEV - 1)), pltpu.SemaphoreType.DMA((n_job, N_DEV - 1)),
                        pltpu.SemaphoreType.DMA((n_job,))])(*srcs, *dsts)


def _attn_call(body, name, grid, in_specs, out_specs, out_shape, scratch_shapes, semantics, operands, rider):
    out_specs, out_shape, scratch_shapes = list(out_specs), list(out_shape), list(scratch_shapes)
    if rider is None:
        res = pl.pallas_call(body, name=name, grid=grid, in_specs=list(in_specs), out_specs=out_specs,
                             out_shape=out_shape, scratch_shapes=scratch_shapes,
                             compiler_params=_params(*semantics))(*operands)
        return list(res), None
    srcs, dsts, jobs = rider
    n_in, n_out, n_scr, n_src, n_dst = len(operands), len(out_shape), len(scratch_shapes), len(srcs), len(dsts)

    def carrying(*refs):
        src_refs = refs[n_in:n_in + n_src]
        at = n_in + n_src + n_dst
        dst_refs = refs[at + n_out:at + n_out + n_dst]
        scratch = refs[at + n_out + n_dst:at + n_out + n_dst + n_scr]
        send_sems, recv_sems = refs[-2], refs[-1]
        ids = [pl.program_id(a) for a in range(len(grid))]
        first = functools.reduce(jnp.logical_and, [i == 0 for i in ids])
        last = functools.reduce(jnp.logical_and, [i == g - 1 for i, g in zip(ids, grid)])

        @pl.when(first)
        def _():
            local, remote = _exchange_copies(jobs, src_refs, dst_refs, send_sems, recv_sems, True)
            for cp in remote + local:
                cp.start()

        body(*refs[:n_in], *refs[at:at + n_out], *scratch)

        @pl.when(last)
        def _():
            local, remote = _exchange_copies(jobs, src_refs, dst_refs, send_sems, recv_sems, False)
            for cp in local:
                cp.wait()
            for cp in remote:
                cp.wait_send()
                cp.wait_recv()

    hbm = pl.BlockSpec(memory_space=pl.ANY)
    sems = pltpu.SemaphoreType.DMA((len(jobs) * N_DEV,))
    res = pl.pallas_call(
        carrying, name=name, grid=grid, in_specs=list(in_specs) + [hbm] * (n_src + n_dst),
        out_specs=out_specs + [hbm] * n_dst,
        out_shape=out_shape + [jax.ShapeDtypeStruct(a.shape, a.dtype) for a in dsts],
        input_output_aliases={n_in + n_src + i: n_out + i for i in range(n_dst)},
        scratch_shapes=scratch_shapes + [sems, sems],
        compiler_params=_params(*["arbitrary"] * len(grid)))(*operands, *srcs, *dsts)
    return list(res[:n_out]), list(res[n_out:])


def _peer(x, y, c, dd):
    return (1 - x if dd & 4 else x, 1 - y if dd & 2 else y, 1 - c if dd & 1 else c)


def _exchange_copies(jobs, src_refs, dst_refs, send_sems, recv_sems, sending, only=None):
    x, y, c = _mesh_pos()
    me = 4 * x + 2 * y + c
    local, remote = [], []
    for t, (si, di, src_slice, dst_slice) in enumerate(jobs):
        if only is not None and t not in only:
            continue
        local.append(pltpu.make_async_copy(src_slice(src_refs[si], me), dst_slice(dst_refs[di], me),
                                           send_sems.at[t * N_DEV]))
        for dd in range(1, N_DEV):
            px, py, pc = _peer(x, y, c, dd)
            p = 4 * px + 2 * py + pc
            remote.append(pltpu.make_async_remote_copy(
                src_ref=src_slice(src_refs[si], p), dst_ref=dst_slice(dst_refs[di], me if sending else p),
                send_sem=send_sems.at[t * N_DEV + dd], recv_sem=recv_sems.at[t * N_DEV + dd],
                device_id=(px, py, pc), device_id_type=MESH))
    return local, remote


def _exchange_start(srcs, dsts, jobs, name, after=None):
    n_src, n_dst, n_job = len(srcs), len(dsts), len(jobs)
    n_in = n_src + n_dst + (after is not None)

    def body(*refs):
        src_refs, dst_refs = refs[:n_src], refs[n_src:n_src + n_dst]
        send_sems, recv_sems = refs[n_in], refs[n_in + 1]
        local, remote = _exchange_copies(jobs, src_refs, dst_refs, send_sems, recv_sems, True)
        for cp in remote + local:
            cp.start()
        refs[-1][...] = jnp.zeros((SUBLANES, LANES), F32)

    hbm = pl.BlockSpec(memory_space=pltpu.HBM)
    sem = pl.BlockSpec(memory_space=pltpu.SEMAPHORE)
    operands = [pltpu.with_memory_space_constraint(a, pltpu.HBM) for a in (*srcs, *dsts)]
    extra_specs = [] if after is None else [pl.BlockSpec(memory_space=pl.ANY)]
    extra = [] if after is None else [after]
    res = pl.pallas_call(
        body, name=name, in_specs=[hbm] * (n_src + n_dst) + extra_specs,
        out_specs=(sem, sem, *[hbm] * (n_src + n_dst), pl.BlockSpec(memory_space=pltpu.VMEM)),
        out_shape=(pltpu.SemaphoreType.DMA((n_job * N_DEV,)), pltpu.SemaphoreType.DMA((n_job * N_DEV,)),
                   *[pltpu.HBM(a.shape, a.dtype) for a in (*srcs, *dsts)],
                   jax.ShapeDtypeStruct((SUBLANES, LANES), F32)),
        input_output_aliases={i: 2 + i for i in range(n_src + n_dst)},
        compiler_params=pltpu.CompilerParams(has_side_effects=pltpu.SideEffectType.DATAFLOW_SIDE_EFFECTING))(
            *operands, *extra)
    return res[0], res[1], list(res[2:2 + n_src]), list(res[2 + n_src:2 + n_src + n_dst]), res[-1]


def _exchange_wait(parts, dsts, after, name):
    n_dst = len(dsts)
    n_src = sum(len(p[2]) for p in parts)

    def body(*refs):
        dst_refs = refs[n_src:n_src + n_dst]
        sem_refs = refs[n_src + n_dst:n_src + n_dst + 2 * len(parts)]
        at = 0
        for k, (_, _, part_srcs, jobs, only) in enumerate(parts):
            src_refs = refs[at:at + len(part_srcs)]
            at += len(part_srcs)
            local, remote = _exchange_copies(jobs, src_refs, dst_refs, sem_refs[2 * k], sem_refs[2 * k + 1], False, only)
            for cp in local:
                cp.wait()
            for cp in remote:
                cp.wait_send()
                cp.wait_recv()

    hbm = pl.BlockSpec(memory_space=pltpu.HBM)
    sem = pl.BlockSpec(memory_space=pltpu.SEMAPHORE)
    srcs = [a for p in parts for a in p[2]]
    sems = [s for p in parts for s in (p[0], p[1])]
    res = pl.pallas_call(
        body, name=name, in_specs=[hbm] * (n_src + n_dst) + [sem] * len(sems) + [pl.BlockSpec(memory_space=pl.ANY)],
        out_specs=[hbm] * (n_src + n_dst),
        out_shape=[pltpu.HBM(a.shape, a.dtype) for a in (*srcs, *dsts)],
        input_output_aliases={i: i for i in range(n_src + n_dst)},
        compiler_params=pltpu.CompilerParams(has_side_effects=pltpu.SideEffectType.DATAFLOW_SIDE_EFFECTING))(
            *srcs, *dsts, *sems, after)
    return list(res[:n_src]), list(res[n_src:])


def _whole(ref, p):
    return ref


def _layer_of(layer, inner):
    return lambda ref, p: inner(ref.at[layer], p)


def _cols_of(width):
    def take(ref, p):
        lead = (slice(None),) * (len(ref.shape) - 1)
        return ref.at[lead + (pl.ds(pl.multiple_of(p * width, LANES), width),)]
    return take


def _rows_of(height):
    def take(ref, p):
        lead = (slice(None),) * (len(ref.shape) - 2)
        return ref.at[lead + (pl.ds(pl.multiple_of(p * height, SUBLANES), height), slice(None))]
    return take


def _slot(layer=None):
    if layer is None:
        return lambda ref, p: ref.at[p]
    return lambda ref, p: ref.at[p, layer]


def _local_step(x0, target, mod, g_mix_pre, g_mix_post, g_ffn_pre, g_ffn_post, wqkv, wo, wg, wu, wd,
                wfg, bfg, wconv, bconv, tiles, comm=None):
    s, d = x0.shape
    nl = mod.shape[0]
    n_heads = d // HEAD_DIM
    hp_n = d // LANES
    assert tiles["sb_fwd"] == tiles["sb_bwd"] and tiles["fox_fwd"] == tiles["fox_bwd"]
    vec = lambda a: a.reshape(1, -1)
    saved = []
    xcur = x0
    for l in range(nl):
        if comm is not None:
            wqkv, wo, wg, wu, wd = comm["weights"]()
        sh_a, sc_a, gt_a, sh_f, sc_f, gt_f = (vec(mod[l, i * d:(i + 1) * d]) for i in range(6))
        fox = l % 2 == 1
        h1 = _norm_mod_fwd(xcur, vec(g_mix_pre[l]), sh_a, sc_a, f"norm_mix_fwd_{l}")
        qkv = _mm_nn(h1, wqkv, l, BF16, f"qkv_fwd_{l}", split_out=d)
        rider = None if comm is None else comm["fwd_rider"](l)
        if fox:
            j = l // 2
            fl = _mm_nn(h1, wfg, j, F32, f"fgate_fwd_{l}")
            cum = _fox_prep(fl, bfg[j], f"fox_prep_{l}")
            ck = cum[:, :n_heads].T.reshape(hp_n, 2, s)
            (o, *stat), moved = _fox_fwd(qkv, _key_bound(qkv, f"key_bound_{l}"), cum, ck, *tiles["fox_fwd"],
                                         f"fox_fwd_{l}", rider)
            extra = (fl, cum, ck)
        else:
            (o, *stat), moved = _sb_fwd(qkv, _key_bound(qkv, f"key_bound_{l}"), *tiles["sb_fwd"], f"sb_fwd_{l}", rider)
            extra = None
        if moved is not None:
            comm["done"](moved)
            wqkv, wo, wg, wu, wd = comm["weights"]()
        u = _mm_nn(o, wo, l, F32, f"attn_out_fwd_{l}")
        x2 = _post_fwd(xcur, u, vec(g_mix_post[l]), gt_a, f"post_mix_fwd_{l}")
        h2 = _norm_mod_fwd(x2, vec(g_ffn_pre[l]), sh_f, sc_f, f"norm_ffn_fwd_{l}")
        gp = _mm_nn(h2, wg, l, BF16, f"ffn_gate_fwd_{l}")
        up = _mm_nn(h2, wu, l, BF16, f"ffn_up_fwd_{l}")
        act = _conv_act_fwd(gp, up, wconv[l], bconv[l], f"conv_act_fwd_{l}")
        yv = _mm_nn(act, wd, l, F32, f"ffn_down_fwd_{l}")
        x3 = _post_fwd(x2, yv, vec(g_ffn_post[l]), gt_f, f"post_ffn_fwd_{l}")
        saved.append((xcur, h1, qkv, o, stat, extra, u, x2, h2, gp, up, act, yv))
        xcur = x3

    dx, sq = _loss_head(xcur, target, "loss_head")
    small, big = [None] * nl, [None] * nl
    for l in reversed(range(nl)):
        xin, h1, qkv, o, stat, extra, u, x2, h2, gp, up, act, yv = saved[l]
        sc_a, gt_a, sc_f, gt_f = (vec(mod[l, i * d:(i + 1) * d]) for i in (1, 2, 4, 5))
        fox = l % 2 == 1
        dy, dgt_f, dg4 = _post_bwd(dx, yv, vec(g_ffn_post[l]), gt_f, f"post_ffn_bwd_{l}")
        dact = _mm_nt(dy, wd, l, BF16, f"ffn_down_dx_{l}")
        dwd = _mm_tn(act, dy, BF16, f"ffn_down_dw_{l}")
        dgp, dup, dwc, dbc = _conv_act_bwd(dact, gp, up, wconv[l], bconv[l], f"conv_act_bwd_{l}")
        dh2 = _mm_nt(dgp, wg, l, F32, f"ffn_gate_dx_{l}")
        dh2 = _mm_nt(dup, wu, l, F32, f"ffn_up_dx_{l}", add=dh2)
        dwg = _mm_tn(h2, dgp, BF16, f"ffn_gate_dw_{l}")
        dwu = _mm_tn(h2, dup, BF16, f"ffn_up_dw_{l}")
        dx2, dsh_f, dsc_f, dg3 = _norm_mod_bwd(dh2, x2, vec(g_ffn_pre[l]), sc_f, dx, f"norm_ffn_bwd_{l}")
        du, dgt_a, dg2 = _post_bwd(dx2, u, vec(g_mix_post[l]), gt_a, f"post_mix_bwd_{l}")
        do = _mm_nt(du, wo, l, BF16, f"attn_out_dx_{l}")
        dwo = _mm_tn(o, du, BF16, f"attn_out_dw_{l}")
        rider = None
        if comm is not None:
            comm["grads"](l, dict(gate=dwg, up=dwu, down=dwd))
            rider = comm["bwd_rider"]()
        if fox:
            j = l // 2
            fl, cum, ck = extra
            (dqkv, dcq, dck), moved = _fox_bwd(qkv, o, do, *stat, cum, ck, *tiles["fox_bwd"], f"fox_bwd_{l}", rider)
            dcq = jnp.max(dcq.reshape(hp_n, s, 2, HEAD_DIM), axis=3)
            dcum = dcq.transpose(1, 0, 2).reshape(s, n_heads) + dck.reshape(n_heads, s).T
            dcum = jnp.pad(dcum, ((0, 0), (0, LANES - n_heads)))
            dfl, dbfg = _fox_gate_bwd(dcum, fl, bfg[j], n_heads, f"fox_gate_bwd_{l}")
            dh1 = _mm_nt(dfl, wfg, j, F32, f"fgate_dx_{l}")
            dh1 = _mm_nt(dqkv, wqkv, l, F32, f"qkv_dx_{l}", add=dh1)
            dwfg = _mm_tn(h1, dfl, F32, f"fgate_dw_{l}")[:, :n_heads]
            dbfg = dbfg[0, :n_heads]
        else:
            (dqkv,), moved = _sb_bwd(qkv, *stat, do, *tiles["sb_bwd"], f"sb_bwd_{l}", rider)
            dh1 = _mm_nt(dqkv, wqkv, l, F32, f"qkv_dx_{l}")
            dwfg = dbfg = None
        if moved is not None:
            comm["done"](moved)
        dwqkv = _mm_tn(h1, dqkv, BF16, f"qkv_dw_{l}")
        dx, dsh_a, dsc_a, dg1 = _norm_mod_bwd(dh1, xin, vec(g_mix_pre[l]), sc_a, dx2, f"norm_mix_bwd_{l}")
        dmod = jnp.concatenate([dsh_a, dsc_a, dgt_a, dsh_f, dsc_f, dgt_f], axis=1)[0]
        small[l] = dict(dmod=dmod, dg1=dg1[0], dg2=dg2[0], dg3=dg3[0], dg4=dg4[0], dbc=dbc[0], dwc=dwc,
                        dbfg=dbfg, dwfg=dwfg)
        big[l] = dict(qkv=dwqkv, o=dwo, gate=dwg, up=dwu, down=dwd)
        if comm is not None:
            comm["grads"](l, dict(qkv=dwqkv, o=dwo))
    return sq, dx, small, big


def _rows128(a, rows):
    flat = a.reshape(-1)
    return jnp.pad(flat, (0, rows * LANES - flat.shape[0])).reshape(rows, LANES)


def _ceil8(n_elems):
    rows = -(-n_elems // LANES)
    return -(-rows // SUBLANES) * SUBLANES


def kernel(x, c, w_mod, b_mod, g_mix_pre, g_mix_post, w_qkv, w_o, w_fg, b_fg, g_ffn_pre, g_ffn_post, w_ffn_gate, w_ffn_up, w_conv, b_conv, w_ffn_down, loss_target, m_w_mod, m_b_mod, m_g_mix_pre, m_g_mix_post, m_w_qkv, m_w_o, m_w_fg, m_b_fg, m_g_ffn_pre, m_g_ffn_post, m_w_ffn_gate, m_w_ffn_up, m_w_conv, m_b_conv, m_w_ffn_down, v_w_mod, v_b_mod, v_g_mix_pre, v_g_mix_post, v_w_qkv, v_w_o, v_w_fg, v_b_fg, v_g_ffn_pre, v_g_ffn_post, v_w_ffn_gate, v_w_ffn_up, v_w_conv, v_b_conv, v_w_ffn_down):
    _, s, d = x.shape
    nl = w_qkv.shape[0]
    nf = w_fg.shape[0]
    n_heads = w_fg.shape[2]
    fs = w_ffn_gate.shape[2]
    fp = -(-fs // LANES) * LANES
    f_full, f_pad = N_DEV * fs, N_DEV * fp
    mod_cols = w_mod.shape[2]
    qs, orows = w_qkv.shape[2], w_o.shape[1]
    xi, yi, ci = _mesh_pos()
    me = 4 * xi + 2 * yi + ci

    c_rows = d // LANES
    c_all = _allgather_small(jnp.pad(c.reshape(1, d), ((0, SUBLANES - 1), (0, 0))).reshape(SUBLANES * c_rows, LANES),
                             "gather_cond")
    c_all = c_all.reshape(N_DEV, SUBLANES, d)[:, 0, :]
    b_mod_cols = lax.dynamic_slice_in_dim(b_mod, me * mod_cols, mod_cols, axis=1).reshape(nl, 1, mod_cols)
    mod_part = _mod_fwd(c_all, w_mod, b_mod_cols, "mod_fwd")

    conv_pad = jnp.pad(w_conv, ((0, 0), (0, 0), (0, fp - fs)))
    r_mod, r_conv, r_fg = _ceil8(mod_part.size), _ceil8(conv_pad.size), _ceil8(w_fg.size)
    payload = jnp.concatenate([_rows128(mod_part, r_mod), _rows128(conv_pad, r_conv), _rows128(w_fg, r_fg)], axis=0)
    got = _allgather_small(payload, "gather_small_weights").reshape(N_DEV, r_mod + r_conv + r_fg, LANES)
    mod_g = got[:, :r_mod].reshape(N_DEV, -1)[:, :mod_part.size].reshape(N_DEV, nl, N_DEV, mod_cols)
    mod = lax.dynamic_index_in_dim(mod_g, me, axis=2, keepdims=False).transpose(1, 0, 2).reshape(nl, N_DEV * mod_cols)
    conv_g = got[:, r_mod:r_mod + r_conv].reshape(N_DEV, -1)[:, :conv_pad.size].reshape(N_DEV, nl, 3, fp)
    wconv_full = conv_g.transpose(1, 2, 0, 3).reshape(nl, 3, f_pad)
    fg_g = got[:, r_mod + r_conv:].reshape(N_DEV, -1)[:, :w_fg.size].reshape(N_DEV, nf, orows, n_heads)
    wfg_full = fg_g.transpose(1, 0, 2, 3).reshape(nf, d, n_heads)
    wfg_full = jnp.pad(wfg_full, ((0, 0), (0, 0), (0, LANES - n_heads))).astype(BF16)
    bfg_full = jnp.pad(b_fg, ((0, 0), (0, LANES - n_heads))).reshape(nf, 1, LANES)
    bconv_full = jnp.pad(b_conv.reshape(nl, N_DEV, fs), ((0, 0), (0, 0), (0, fp - fs))).reshape(nl, 1, f_pad)

    gate_sh = jnp.pad(w_ffn_gate, ((0, 0), (0, 0), (0, fp - fs))).astype(BF16)
    up_sh = jnp.pad(w_ffn_up, ((0, 0), (0, 0), (0, fp - fs))).astype(BF16)
    down_sh = jnp.pad(w_ffn_down, ((0, 0), (0, fp - fs), (0, 0))).astype(BF16)
    shards = [w_qkv.astype(BF16), w_o.astype(BF16), gate_sh, up_sh, down_sh]
    full_shapes = [jax.ShapeDtypeStruct((nl, d, N_DEV * qs), BF16), jax.ShapeDtypeStruct((nl, d, d), BF16),
                   jax.ShapeDtypeStruct((nl, d, f_pad), BF16), jax.ShapeDtypeStruct((nl, d, f_pad), BF16),
                   jax.ShapeDtypeStruct((nl, f_pad, d), BF16)]
    place = [_cols_of(qs), _rows_of(orows), _cols_of(fp), _cols_of(fp), _rows_of(fp)]
    mixer_w, ffn_w = (0, 1), (2, 3, 4)

    def gather_jobs(l, which):
        return [(i, i, _layer_of(l, _whole), _layer_of(l, place[i])) for i in which]

    order = ["qkv", "o", "gate", "up", "down"]
    send = {"qkv": _cols_of(qs), "o": _rows_of(orows), "gate": _cols_of(fp), "up": _cols_of(fp), "down": _rows_of(fp)}
    recv_shapes = [(N_DEV, nl, d, qs), (N_DEV, nl, orows, d), (N_DEV, nl, d, fp), (N_DEV, nl, d, fp), (N_DEV, nl, fp, d)]
    state = {"full": _exchange(shards, [lax.empty(sh.shape, sh.dtype) for sh in full_shapes],
                               gather_jobs(0, mixer_w), "gather_weights_0"),
             "recv": [lax.empty(sh, BF16) for sh in recv_shapes], "pending": [], "moving": None}

    def fwd_rider(l):
        state["moving"] = "full"
        return shards, state["full"], gather_jobs(l, ffn_w) + (gather_jobs(l + 1, mixer_w) if l + 1 < nl else [])

    def bwd_rider():
        waiting, state["pending"], state["moving"] = state["pending"], [], "recv"
        if not waiting:
            return None
        jobs = [(k, order.index(nm), send[nm], _slot(l)) for k, (l, nm, _) in enumerate(waiting)]
        return [g for _, _, g in waiting], state["recv"], jobs

    def done(dsts):
        state[state["moving"]] = dsts

    def grads(l, new):
        state["pending"] += [(l, nm, new[nm]) for nm in order if nm in new]

    comm = dict(weights=lambda: state["full"], fwd_rider=fwd_rider, bwd_rider=bwd_rider, done=done, grads=grads)
    sq, dx, small, big = _local_step(x[0], loss_target[0], mod, g_mix_pre, g_mix_post, g_ffn_pre, g_ffn_post,
                                     None, None, None, None, None, wfg_full, bfg_full, wconv_full, bconv_full,
                                     ATTN_TILES, comm)
    loss = lax.psum(0.5 * jnp.sum(sq) / d, ("x", "y", "c"))
    recv = _exchange(*bwd_rider(), "scatter_grads_last")
    upd = {}
    for nm, rv, wt, mt, vt in zip(order, recv, [w_qkv, w_o, w_ffn_gate, w_ffn_up, w_ffn_down],
                                  [m_w_qkv, m_w_o, m_w_ffn_gate, m_w_ffn_up, m_w_ffn_down],
                                  [v_w_qkv, v_w_o, v_w_ffn_gate, v_w_ffn_up, v_w_ffn_down]):
        upd[nm] = _adamw(rv, wt, mt, vt, f"adamw_{nm}")

    stack = lambda key: jnp.stack([small[l][key] for l in range(nl)])
    dmod = stack("dmod")
    dgs = [stack(k) for k in ("dg1", "dg2", "dg3", "dg4")]
    dbc = stack("dbc").reshape(nl, N_DEV, fp)[:, :, :fs].reshape(nl, f_full)
    dbfg = jnp.stack([small[l]["dbfg"] for l in range(nl) if l % 2 == 1])
    dwc = stack("dwc").reshape(nl, 3, N_DEV, fp)[:, :, :, :fs].reshape(nl, 3, f_full)
    dwfg = jnp.stack([small[l]["dwfg"] for l in range(nl) if l % 2 == 1])
    rep_parts = [dmod] + dgs + [dbc, dbfg]
    rep_rows = [_ceil8(p.size) for p in rep_parts]
    r_rep, r_wc, r_wfg = sum(rep_rows), _ceil8(dwc.size), _ceil8(dwfg.size)
    payload = jnp.concatenate([_rows128(p, r) for p, r in zip(rep_parts, rep_rows)]
                              + [_rows128(dwc, r_wc), _rows128(dwfg, r_wfg)], axis=0)
    gsm = _allgather_small(payload, "gather_small_grads").reshape(N_DEV, 1, r_rep + r_wc + r_wfg, LANES)

    def pack(parts):
        return jnp.concatenate([_rows128(p, r) for p, r in zip(parts, rep_rows)], axis=0).reshape(1, r_rep, LANES)

    rep_w = [b_mod, g_mix_pre, g_mix_post, g_ffn_pre, g_ffn_post, b_conv, b_fg]
    rep_m = [m_b_mod, m_g_mix_pre, m_g_mix_post, m_g_ffn_pre, m_g_ffn_post, m_b_conv, m_b_fg]
    rep_v = [v_b_mod, v_g_mix_pre, v_g_mix_post, v_g_ffn_pre, v_g_ffn_post, v_b_conv, v_b_fg]
    rep_out = _adamw(gsm[:, :, :r_rep], pack(rep_w), pack(rep_m), pack(rep_v), "adamw_replicated", tr=r_rep, tc=LANES)

    def unpack(packed):
        outs, at = [], 0
        for p, r in zip(rep_w, rep_rows):
            outs.append(packed[0, at:at + r].reshape(-1)[:p.size].reshape(p.shape))
            at += r
        return outs

    rep_g, rep_d, rep_nm, rep_nv = (unpack(a) for a in rep_out)

    wc_all = gsm[:, 0, r_rep:r_rep + r_wc].reshape(N_DEV, -1)[:, :dwc.size].reshape(N_DEV, 1, nl * 3, f_full)
    wc_mine = lax.dynamic_slice_in_dim(wc_all, me * fs, fs, axis=3)
    wc_out = _adamw(wc_mine, w_conv.reshape(1, nl * 3, fs), m_w_conv.reshape(1, nl * 3, fs),
                    v_w_conv.reshape(1, nl * 3, fs), "adamw_conv", tr=nl * 3, tc=fs)
    wc_out = [a.reshape(nl, 3, fs) for a in wc_out]
    wfg_all = gsm[:, 0, r_rep + r_wc:].reshape(N_DEV, -1)[:, :dwfg.size].reshape(N_DEV, nf, d, n_heads)
    wfg_mine = lax.dynamic_slice_in_dim(wfg_all, me * orows, orows, axis=2)
    wfg_out = _adamw(wfg_mine, w_fg, m_w_fg, v_w_fg, "adamw_fgate", tr=orows, tc=n_heads)

    dmod_all = gsm[:, 0, :rep_rows[0]].reshape(N_DEV, -1)[:, :dmod.size].reshape(N_DEV, nl, N_DEV * mod_cols)
    dmod_mine = lax.dynamic_slice_in_dim(dmod_all, me * mod_cols, mod_cols, axis=2)
    gwmod = _wmod_grad(c_all.T, dmod_mine, "wmod_grad")
    wmod_out = _adamw(gwmod.reshape(1, nl, d, mod_cols), w_mod, m_w_mod, v_w_mod, "adamw_mod")

    per_weight = [wmod_out, None, None, None, upd["qkv"], upd["o"], wfg_out, None, None, None,
                  upd["gate"], upd["up"], wc_out, None, upd["down"]]
    rep_index = {1: 0, 2: 1, 3: 2, 8: 3, 9: 4, 13: 5, 7: 6}
    outs = [[], [], [], []]
    for pos, res in enumerate(per_weight):
        for kind in range(4):
            if res is None:
                outs[kind].append((rep_g, rep_d, rep_nm, rep_nv)[kind][rep_index[pos]])
            else:
                outs[kind].append(res[kind])
    return (loss, dx.reshape(1, s, d), *outs[0], *outs[1], *outs[2], *outs[3])
```

```python
import functools

import jax
import jax.numpy as jnp
from jax import lax
from jax.experimental import pallas as pl
from jax.experimental.pallas import tpu as pltpu

F32 = jnp.float32
BF16 = jnp.bfloat16

N_DEV = 8
HEAD_DIM = 64
LANES = 128
SUBLANES = 8
RMS_EPS = 1e-6
ADAM_LR = 0.001
ADAM_B1 = 0.9
ADAM_B2 = 0.999
ADAM_EPS = 1e-08
ADAM_WD = 0.01
ADAM_STEP = 10
VMEM_LIMIT_BYTES = 56 * 1024 * 1024
EXP_UNDERFLOW = -110.0
SUM_TERMS = 1
HALO = 16
MM_ROWS = 2048
ATTN_TILES = {"sb_fwd": (512, 256), "sb_bwd": (512, 256), "fox_fwd": (1024, 256), "fox_bwd": (1024, 256)}

NN = (((1,), (0,)), ((), ()))
NT = (((1,), (1,)), ((), ()))
TN = (((0,), (0,)), ((), ()))
MESH = pl.DeviceIdType.MESH


def _params(*sem):
    return pltpu.CompilerParams(dimension_semantics=sem, vmem_limit_bytes=VMEM_LIMIT_BYTES)


def _pick(n, pref, quantum):
    if n <= pref:
        return n
    t = (pref // quantum) * quantum
    while n % t:
        t -= quantum
    return t


def _mm(a, b, *, dims, grid, a_spec, b_spec, o_spec, out_shape, name, add=None):
    nk = grid[2]
    acc_shape = tuple(d for d in o_spec.block_shape if d is not None)
    o_dtype = out_shape.dtype

    def body(*refs):
        if add is None:
            a_ref, b_ref, o_ref, acc_ref = refs
            add_ref = None
        else:
            a_ref, b_ref, add_ref, o_ref, acc_ref = refs
        k = pl.program_id(2)
        part = lax.dot_general(a_ref[...], b_ref[...], dims, preferred_element_type=F32)

        def finish(total):
            if add_ref is not None:
                total = total + add_ref[...]
            o_ref[...] = total.astype(o_dtype)

        if nk == 1:
            finish(part)
            return

        @pl.when(k == 0)
        def _():
            acc_ref[...] = part

        @pl.when(jnp.logical_and(k > 0, k < nk - 1))
        def _():
            acc_ref[...] += part

        @pl.when(k == nk - 1)
        def _():
            finish(acc_ref[...] + part)

    operands = [a, b] if add is None else [a, b, add]
    in_specs = [a_spec, b_spec] if add is None else [a_spec, b_spec, o_spec]
    return pl.pallas_call(
        body, name=name, grid=grid, in_specs=in_specs, out_specs=o_spec, out_shape=out_shape,
        scratch_shapes=[pltpu.VMEM(acc_shape, F32)],
        compiler_params=_params("parallel", "parallel", "arbitrary"),
    )(*operands)


def _mm_rows(out_dtype, has_add):
    return MM_ROWS if (jnp.dtype(out_dtype).itemsize == 2 and not has_add) else MM_ROWS // 2


def _mm_nn(a, w, l, out_dtype, name, *, col0=0, n=None, split_out=None):
    m, kdim = a.shape
    n = w.shape[2] if n is None else n
    tm, tk = _pick(m, _mm_rows(out_dtype, False), SUBLANES), _pick(kdim, 1024, LANES)
    tn = _pick(n if split_out is None else split_out, 1024, LANES)
    jb = col0 // tn
    grid = (m // tm, n // tn, kdim // tk)
    a_spec = pl.BlockSpec((tm, tk), lambda i, j, k: (i, k))
    b_spec = pl.BlockSpec((None, tk, tn), lambda i, j, k: (l, k, j + jb))
    if split_out is None:
        o_spec = pl.BlockSpec((tm, tn), lambda i, j, k: (i, j))
        shape = jax.ShapeDtypeStruct((m, n), out_dtype)
    else:
        nj1 = split_out // tn
        o_spec = pl.BlockSpec((None, tm, tn), lambda i, j, k: (j // nj1, i, j % nj1))
        shape = jax.ShapeDtypeStruct((n // split_out, m, split_out), out_dtype)
    return _mm(a, w, dims=NN, grid=grid, a_spec=a_spec, b_spec=b_spec, o_spec=o_spec, out_shape=shape, name=name)


def _mm_nt(a, w, l, out_dtype, name, *, add=None):
    n, kdim = w.shape[1], w.shape[2]
    if a.ndim == 2:
        m = a.shape[0]
        tk = _pick(kdim, 1024, LANES)
        a_spec_of = lambda tm: pl.BlockSpec((tm, tk), lambda i, j, k: (i, k))
    else:
        m, seg = a.shape[1], a.shape[2]
        tk = _pick(seg, 1024, LANES)
        nk1 = seg // tk
        a_spec_of = lambda tm: pl.BlockSpec((None, tm, tk), lambda i, j, k: (k // nk1, i, k % nk1))
    tm, tn = _pick(m, _mm_rows(out_dtype, add is not None), SUBLANES), _pick(n, 1024, LANES)
    grid = (m // tm, n // tn, kdim // tk)
    b_spec = pl.BlockSpec((None, tn, tk), lambda i, j, k: (l, j, k))
    o_spec = pl.BlockSpec((tm, tn), lambda i, j, k: (i, j))
    return _mm(a, w, dims=NT, grid=grid, a_spec=a_spec_of(tm), b_spec=b_spec, o_spec=o_spec,
               out_shape=jax.ShapeDtypeStruct((m, n), out_dtype), name=name, add=add)


def _mm_tn(a, b, out_dtype, name):
    kdim, m = a.shape
    tk, tm = _pick(kdim, 1024, SUBLANES), _pick(m, _mm_rows(out_dtype, False), LANES)
    if b.ndim == 2:
        n = b.shape[1]
        tn = _pick(n, 1024, LANES)
        b_spec = pl.BlockSpec((tk, tn), lambda i, j, k: (k, j))
    else:
        seg = b.shape[2]
        n = b.shape[0] * seg
        tn = _pick(seg, 1024, LANES)
        nj1 = seg // tn
        b_spec = pl.BlockSpec((None, tk, tn), lambda i, j, k: (j // nj1, k, j % nj1))
    grid = (m // tm, n // tn, kdim // tk)
    a_spec = pl.BlockSpec((tk, tm), lambda i, j, k: (k, i))
    o_spec = pl.BlockSpec((tm, tn), lambda i, j, k: (i, j))
    return _mm(a, b, dims=TN, grid=grid, a_spec=a_spec, b_spec=b_spec, o_spec=o_spec,
               out_shape=jax.ShapeDtypeStruct((m, n), out_dtype), name=name)


def _rstd(x):
    return lax.rsqrt(jnp.mean(x * x, axis=-1, keepdims=True) + RMS_EPS)


def _norm_mod_fwd(x, g, shift, scale, name):
    s, d = x.shape
    tm = _pick(s, 512, SUBLANES)

    def body(x_ref, g_ref, sh_ref, sc_ref, h_ref):
        xv = x_ref[...]
        y = (xv * _rstd(xv)) * g_ref[...]
        h_ref[...] = (y * (1.0 + sc_ref[...]) + sh_ref[...]).astype(BF16)

    row = pl.BlockSpec((tm, d), lambda i: (i, 0))
    vec = pl.BlockSpec((1, d), lambda i: (0, 0))
    return pl.pallas_call(body, name=name, grid=(s // tm,), in_specs=[row, vec, vec, vec], out_specs=row,
                          out_shape=jax.ShapeDtypeStruct((s, d), BF16), compiler_params=_params("parallel"))(x, g, shift, scale)


def _norm_mod_bwd(dh, x, g, scale, dres, name):
    s, d = x.shape
    tm = _pick(s, 512, SUBLANES)

    def body(dh_ref, x_ref, g_ref, sc_ref, dres_ref, dx_ref, dsh_ref, dsc_ref, dg_ref):
        i = pl.program_id(0)
        xv, dhv, gv = x_ref[...], dh_ref[...], g_ref[...]
        r = _rstd(xv)
        xh = xv * r
        dn = dhv * (1.0 + sc_ref[...])
        gd = dn * gv
        dx_ref[...] = dres_ref[...] + r * (gd - xh * jnp.mean(gd * xh, axis=-1, keepdims=True))

        @pl.when(i == 0)
        def _():
            dsh_ref[...] = jnp.zeros_like(dsh_ref)
            dsc_ref[...] = jnp.zeros_like(dsc_ref)
            dg_ref[...] = jnp.zeros_like(dg_ref)

        dsh_ref[...] += jnp.sum(dhv, axis=0, keepdims=True)
        dsc_ref[...] += jnp.sum(dhv * (xh * gv), axis=0, keepdims=True)
        dg_ref[...] += jnp.sum(dn * xh, axis=0, keepdims=True)

    row = pl.BlockSpec((tm, d), lambda i: (i, 0))
    vec = pl.BlockSpec((1, d), lambda i: (0, 0))
    vshape = jax.ShapeDtypeStruct((1, d), F32)
    return pl.pallas_call(body, name=name, grid=(s // tm,), in_specs=[row, row, vec, vec, row],
                          out_specs=[row, vec, vec, vec],
                          out_shape=[jax.ShapeDtypeStruct((s, d), F32), vshape, vshape, vshape],
                          compiler_params=_params("arbitrary"))(dh, x, g, scale, dres)


def _post_fwd(x, u, g, gate, name):
    s, d = x.shape
    tm = _pick(s, 512, SUBLANES)

    def body(x_ref, u_ref, g_ref, gt_ref, o_ref):
        uv = u_ref[...]
        o_ref[...] = x_ref[...] + gt_ref[...] * ((uv * _rstd(uv)) * g_ref[...])

    row = pl.BlockSpec((tm, d), lambda i: (i, 0))
    vec = pl.BlockSpec((1, d), lambda i: (0, 0))
    return pl.pallas_call(body, name=name, grid=(s // tm,), in_specs=[row, row, vec, vec], out_specs=row,
                          out_shape=jax.ShapeDtypeStruct((s, d), F32), compiler_params=_params("parallel"))(x, u, g, gate)


def _post_bwd(dx, u, g, gate, name):
    s, d = u.shape
    tm = _pick(s, 512, SUBLANES)

    def body(dx_ref, u_ref, g_ref, gt_ref, du_ref, dgt_ref, dg_ref):
        i = pl.program_id(0)
        uv, dxv, gv = u_ref[...], dx_ref[...], g_ref[...]
        r = _rstd(uv)
        uh = uv * r
        dn = dxv * gt_ref[...]
        gd = dn * gv
        du_ref[...] = (r * (gd - uh * jnp.mean(gd * uh, axis=-1, keepdims=True))).astype(BF16)

        @pl.when(i == 0)
        def _():
            dgt_ref[...] = jnp.zeros_like(dgt_ref)
            dg_ref[...] = jnp.zeros_like(dg_ref)

        dgt_ref[...] += jnp.sum(dxv * (uh * gv), axis=0, keepdims=True)
        dg_ref[...] += jnp.sum(dn * uh, axis=0, keepdims=True)

    row = pl.BlockSpec((tm, d), lambda i: (i, 0))
    vec = pl.BlockSpec((1, d), lambda i: (0, 0))
    vshape = jax.ShapeDtypeStruct((1, d), F32)
    return pl.pallas_call(body, name=name, grid=(s // tm,), in_specs=[row, row, vec, vec],
                          out_specs=[row, vec, vec],
                          out_shape=[jax.ShapeDtypeStruct((s, d), BF16), vshape, vshape],
                          compiler_params=_params("arbitrary"))(dx, u, g, gate)


def _shift_rows(cur, prev8, k):
    rolled = pltpu.roll(cur, k, axis=0)
    rolled_prev = pltpu.roll(prev8, k, axis=0)
    i8 = lax.broadcasted_iota(jnp.int32, prev8.shape, 0)
    top = jnp.where(i8 < k, rolled_prev, rolled[:SUBLANES])
    return jnp.concatenate([top, rolled[SUBLANES:]], axis=0)


def _conv_pre(g, prev8, wc_ref, bc_ref):
    s1 = _shift_rows(g, prev8, 1)
    s2 = _shift_rows(g, prev8, 2)
    gc = bc_ref[...] + wc_ref[0:1, :] * s2 + wc_ref[1:2, :] * s1 + wc_ref[2:3, :] * g
    return gc, s1, s2


def _conv_act_fwd(gp, up, wc, bc, name):
    s, f = gp.shape
    tm, tf = _pick(s, 512, SUBLANES), _pick(f, 768, LANES)
    rh = tm // HALO

    def body(g_ref, gprev_ref, up_ref, wc_ref, bc_ref, a_ref):
        i = pl.program_id(1)
        prev = jnp.where(i == 0, 0.0, gprev_ref[...].astype(F32)[HALO - SUBLANES:])
        gc, _, _ = _conv_pre(g_ref[...].astype(F32), prev, wc_ref, bc_ref)
        a_ref[...] = ((gc * jax.nn.sigmoid(gc)) * up_ref[...].astype(F32)).astype(BF16)

    tile = pl.BlockSpec((tm, tf), lambda j, i: (i, j))
    prev = pl.BlockSpec((HALO, tf), lambda j, i: (jnp.maximum(i * rh - 1, 0), j))
    return pl.pallas_call(body, name=name, grid=(f // tf, s // tm),
                          in_specs=[tile, prev, tile, pl.BlockSpec((3, tf), lambda j, i: (0, j)),
                                    pl.BlockSpec((1, tf), lambda j, i: (0, j))],
                          out_specs=tile, out_shape=jax.ShapeDtypeStruct((s, f), BF16),
                          compiler_params=_params("parallel", "parallel"))(gp, gp, up, wc, bc)


def _conv_act_bwd(da, gp, up, wc, bc, name):
    s, f = gp.shape
    tm, tf = _pick(s, 512, SUBLANES), _pick(f, 768, LANES)
    rh = tm // HALO
    nrow = s // tm

    def body(da_ref, dan_ref, g_ref, gprev_ref, gn_ref, up_ref, upn_ref, wc_ref, bc_ref,
             dgp_ref, dup_ref, dwc_ref, dbc_ref):
        i = pl.program_id(1)
        last = i == nrow - 1
        head = lambda ref: ref[...].astype(F32)[:SUBLANES]
        prev = jnp.where(i == 0, 0.0, gprev_ref[...].astype(F32)[HALO - SUBLANES:])
        g_ext = jnp.concatenate([g_ref[...].astype(F32), head(gn_ref)], axis=0)
        up_ext = jnp.concatenate([up_ref[...].astype(F32), head(upn_ref)], axis=0)
        da_ext = jnp.concatenate([da_ref[...].astype(F32), jnp.where(last, 0.0, head(dan_ref))], axis=0)
        gc, s1, s2 = _conv_pre(g_ext, prev, wc_ref, bc_ref)
        sg = jax.nn.sigmoid(gc)
        dup_ref[...] = (da_ext * (gc * sg))[:tm].astype(BF16)
        dgc = da_ext * up_ext * (sg * (1.0 + gc * (1.0 - sg)))
        ext = tm + SUBLANES
        dgp = (wc_ref[2:3, :] * dgc + wc_ref[1:2, :] * pltpu.roll(dgc, ext - 1, axis=0)
               + wc_ref[0:1, :] * pltpu.roll(dgc, ext - 2, axis=0))
        dgp_ref[...] = dgp[:tm].astype(BF16)

        @pl.when(i == 0)
        def _():
            dwc_ref[...] = jnp.zeros_like(dwc_ref)
            dbc_ref[...] = jnp.zeros_like(dbc_ref)

        d0 = dgc[:tm]
        dwc_ref[0:1, :] += jnp.sum(d0 * s2[:tm], axis=0, keepdims=True)
        dwc_ref[1:2, :] += jnp.sum(d0 * s1[:tm], axis=0, keepdims=True)
        dwc_ref[2:3, :] += jnp.sum(d0 * g_ext[:tm], axis=0, keepdims=True)
        dbc_ref[...] += jnp.sum(d0, axis=0, keepdims=True)

    tile = pl.BlockSpec((tm, tf), lambda j, i: (i, j))
    prev = pl.BlockSpec((HALO, tf), lambda j, i: (jnp.maximum(i * rh - 1, 0), j))
    nxt = pl.BlockSpec((HALO, tf), lambda j, i: (jnp.minimum((i + 1) * rh, s // HALO - 1), j))
    return pl.pallas_call(body, name=name, grid=(f // tf, nrow),
                          in_specs=[tile, nxt, tile, prev, nxt, tile, nxt,
                                    pl.BlockSpec((3, tf), lambda j, i: (0, j)), pl.BlockSpec((1, tf), lambda j, i: (0, j))],
                          out_specs=[tile, tile, pl.BlockSpec((3, tf), lambda j, i: (0, j)),
                                     pl.BlockSpec((1, tf), lambda j, i: (0, j))],
                          out_shape=[jax.ShapeDtypeStruct((s, f), BF16), jax.ShapeDtypeStruct((s, f), BF16),
                                     jax.ShapeDtypeStruct((3, f), F32), jax.ShapeDtypeStruct((1, f), F32)],
                          compiler_params=_params("parallel", "arbitrary"))(da, da, gp, gp, gp, up, up, wc, bc)


def _split_bf16(v, parts):
    out, rem = [], v
    for _ in range(parts):
        t = rem.astype(BF16)
        out.append(t)
        rem = rem - t.astype(F32)
    return jnp.concatenate(out, axis=1)


def _tri(t, cmp, reps):
    row = lax.broadcasted_iota(jnp.int32, (t, t), 0)
    col = lax.broadcasted_iota(jnp.int32, (t, t), 1)
    m = cmp(row, col).astype(BF16)
    return jnp.concatenate([m] * reps, axis=0)


def _log_sigmoid(z):
    mn = jnp.minimum(z, 0.0)
    return mn - jnp.log(1.0 + jnp.exp(mn + (mn - z)))


def _log_one_minus_sigmoid(z):
    mn = jnp.minimum(z, 0.0)
    neg = mn - z
    return neg - jnp.log(1.0 + jnp.exp(mn + neg))


def _pair(xv):
    lane = lax.broadcasted_iota(jnp.int32, xv.shape, 1)
    zero = jnp.zeros_like(xv)
    return jnp.where(lane < HEAD_DIM, xv, zero), jnp.where(lane < HEAD_DIM, zero, xv)


def _attn_specs(s, tq):
    q_spec = pl.BlockSpec((None, tq, LANES), lambda hp, qi: (0, qi, hp))
    k_spec = pl.BlockSpec((None, s, LANES), lambda hp, qi: (1, 0, hp))
    v_spec = pl.BlockSpec((None, s, LANES), lambda hp, qi: (2, 0, hp))
    return q_spec, k_spec, v_spec


def _rel_mask(tq, tk, i, strict):
    row = lax.broadcasted_iota(jnp.int32, (tq, tk), 0)
    col = lax.broadcasted_iota(jnp.int32, (tq, tk), 1) + i * tk
    return col < row if strict else col <= row


def _key_bound(qkv, name):
    _, s, d = qkv.shape

    def body(k_ref, o_ref):
        sq = jnp.square(k_ref[...].astype(F32))
        low = lax.broadcasted_iota(jnp.int32, sq.shape, 1) < HEAD_DIM
        n0 = jnp.max(jnp.sum(jnp.where(low, sq, 0.0), axis=1, keepdims=True))
        n1 = jnp.max(jnp.sum(jnp.where(low, 0.0, sq), axis=1, keepdims=True))
        low8 = lax.broadcasted_iota(jnp.int32, (SUBLANES, LANES), 1) < HEAD_DIM
        o_ref[...] = jnp.sqrt(jnp.where(low8, n0, n1))

    return pl.pallas_call(body, name=name, grid=(d // LANES,),
                          in_specs=[pl.BlockSpec((None, s, LANES), lambda hp: (1, 0, hp))],
                          out_specs=pl.BlockSpec((None, SUBLANES, LANES), lambda hp: (hp, 0, 0)),
                          out_shape=jax.ShapeDtypeStruct((d // LANES, SUBLANES, LANES), F32),
                          compiler_params=_params("parallel"))(qkv)


def _sb_fwd(qkv, kbound, tq, tk, name, rider=None):
    _, s, d = qkv.shape
    hp_n, nq, nd = d // LANES, s // tq, tq // tk
    scale = HEAD_DIM ** -0.5

    def body(q_ref, k_ref, v_ref, kb_ref, o_ref, lt_ref, nb_ref):
        qi = pl.program_id(1)
        lane = lax.broadcasted_iota(jnp.int32, (tq, LANES), 1)
        onward = _tri(tk, lambda j, sidx: j >= sidx, SUM_TERMS)
        qpair = _pair(q_ref[...] * scale)
        kbv = kb_ref[...]
        z_max = [jnp.sqrt(jnp.sum(jnp.square(qh.astype(F32)), axis=1, keepdims=True)) * (1.01 * km)
                 for qh, km in zip(qpair, (kbv[0:1, 0:1], kbv[0:1, HEAD_DIM:HEAD_DIM + 1]))]

        def alive(runs):
            return jnp.logical_or(jnp.max(runs[0] + z_max[0]) >= EXP_UNDERFLOW,
                                  jnp.max(runs[1] + z_max[1]) >= EXP_UNDERFLOW)

        def block(kb, carry, mask):
            acc, runs = carry[0], carry[1:]
            ks = pl.multiple_of(kb * tk, tk)
            kblk = k_ref[pl.ds(ks, tk), :]
            vpair = _pair(v_ref[pl.ds(ks, tk), :])
            probs, new_runs = [], []
            for qh, run in zip(qpair, runs):
                z = lax.dot_general(qh, kblk, NT, preferred_element_type=F32)
                l1 = _log_one_minus_sigmoid(z)
                if mask is not None:
                    l1 = jnp.where(mask, l1, 0.0)
                c = jnp.dot(_split_bf16(l1, SUM_TERMS), onward, preferred_element_type=F32)
                a = jnp.exp(z + c + run)
                if mask is not None:
                    a = jnp.where(mask, a, 0.0)
                probs.append(a.astype(BF16))
                new_runs.append(run + jnp.sum(l1, axis=1, keepdims=True))
            acc = acc + jnp.dot(jnp.concatenate(probs, axis=1), jnp.concatenate(vpair, axis=0),
                                preferred_element_type=F32)
            return (acc, *new_runs)

        zero1 = jnp.zeros((tq, 1), F32)
        carry = (jnp.zeros((tq, LANES), F32), zero1, zero1)
        for i in reversed(range(nd)):
            carry = block(qi * nd + i, carry, _rel_mask(tq, tk, i, True))

        def step(state):
            new = block(qi * nd - 1 - state[0], state[2:], None)
            return (state[0] + 1, alive(new[1:]), *new)

        done, _, acc, run0, run1 = lax.while_loop(lambda st: jnp.logical_and(st[0] < qi * nd, st[1]), step,
                                                  (jnp.int32(0), alive(carry[1:]), *carry))
        o_ref[...] = acc.astype(BF16)
        lt_ref[...] = jnp.where(lane < HEAD_DIM, run0, run1)
        nb_ref[...] = jnp.zeros((SUBLANES, LANES), F32) + done.astype(F32)

    q_spec, k_spec, v_spec = _attn_specs(s, tq)
    return _attn_call(
        body, name, (hp_n, nq),
        [q_spec, k_spec, v_spec, pl.BlockSpec((None, SUBLANES, LANES), lambda hp, qi: (hp, 0, 0))],
        [pl.BlockSpec((tq, LANES), lambda hp, qi: (qi, hp)),
         pl.BlockSpec((None, tq, LANES), lambda hp, qi: (hp, qi, 0)),
         pl.BlockSpec((None, None, SUBLANES, LANES), lambda hp, qi: (hp, qi, 0, 0))],
        [jax.ShapeDtypeStruct((s, d), BF16), jax.ShapeDtypeStruct((hp_n, s, LANES), F32),
         jax.ShapeDtypeStruct((hp_n, nq, SUBLANES, LANES), F32)],
        [], ("parallel", "parallel"), (qkv, qkv, qkv, kbound), rider)


def _sb_bwd(qkv, ltot, visited, do, tq, tk, name, rider=None):
    _, s, d = qkv.shape
    hp_n, nq, nd = d // LANES, s // tq, tq // tk
    scale = HEAD_DIM ** -0.5

    def body(q_ref, k_ref, v_ref, lt_ref, nb_ref, do_ref, dqkv_ref, dk_acc, dv_acc):
        qi = pl.program_id(1)
        first = qi * nd - jnp.max(nb_ref[...]).astype(jnp.int32)

        @pl.when(qi == 0)
        def _():
            dk_acc[...] = jnp.zeros_like(dk_acc)
            dv_acc[...] = jnp.zeros_like(dv_acc)

        onward = _tri(tk, lambda j, sidx: j >= sidx, SUM_TERMS)
        before = _tri(tk, lambda j, sidx: j < sidx, SUM_TERMS)
        qpair = _pair(q_ref[...] * scale)
        dopair = _pair(do_ref[...])
        ltv = lt_ref[...]
        lts = (ltv[:, 0:1], ltv[:, HEAD_DIM:HEAD_DIM + 1])

        def block(kb, carry, mask):
            dq, lpres, gpres = carry[0], carry[1:3], carry[3:5]
            ks = pl.multiple_of(kb * tk, tk)
            kblk, vblk = k_ref[pl.ds(ks, tk), :], v_ref[pl.ds(ks, tk), :]
            kpair = _pair(kblk * scale)
            dzs, probs, new_l, new_g = [], [], [], []
            for qh, doh, lt, lpre, gpre in zip(qpair, dopair, lts, lpres, gpres):
                z = lax.dot_general(qh, kblk, NT, preferred_element_type=F32)
                l1 = _log_one_minus_sigmoid(z)
                if mask is not None:
                    l1 = jnp.where(mask, l1, 0.0)
                lpre = lpre + jnp.sum(l1, axis=1, keepdims=True)
                c = jnp.dot(_split_bf16(l1, SUM_TERMS), onward, preferred_element_type=F32)
                a = jnp.exp(z + c + (lt - lpre))
                if mask is not None:
                    a = jnp.where(mask, a, 0.0)
                da = lax.dot_general(doh, vblk, NT, preferred_element_type=F32)
                g = da * a
                p = jnp.dot(_split_bf16(g, SUM_TERMS), before, preferred_element_type=F32) + gpre
                dz = g - jnp.exp(z + l1) * (g + p)
                if mask is not None:
                    dz = jnp.where(mask, dz, 0.0)
                dzs.append(dz.astype(BF16))
                probs.append(a.astype(BF16))
                new_l.append(lpre)
                new_g.append(gpre + jnp.sum(g, axis=1, keepdims=True))
            dq = dq + jnp.dot(jnp.concatenate(dzs, axis=1), jnp.concatenate(kpair, axis=0),
                              preferred_element_type=F32)
            dk_acc[pl.ds(ks, tk), :] += lax.dot_general(jnp.concatenate(dzs, axis=0), jnp.concatenate(qpair, axis=0),
                                                        TN, preferred_element_type=F32)
            dv_acc[pl.ds(ks, tk), :] += lax.dot_general(jnp.concatenate(probs, axis=0), jnp.concatenate(dopair, axis=0),
                                                        TN, preferred_element_type=F32)
            return (dq, *new_l, *new_g)

        zero1 = jnp.zeros((tq, 1), F32)
        init = (jnp.zeros((tq, LANES), F32), zero1, zero1, zero1, zero1)
        carry = lax.fori_loop(first, qi * nd, lambda j, cr: block(j, cr, None), init)
        for i in range(nd):
            carry = block(qi * nd + i, carry, _rel_mask(tq, tk, i, True))
        dqkv_ref[0, pl.ds(pl.multiple_of(qi * tq, tq), tq), :] = carry[0].astype(BF16)

        @pl.when(qi == nq - 1)
        def _():
            dqkv_ref[1] = dk_acc[...].astype(BF16)
            dqkv_ref[2] = dv_acc[...].astype(BF16)

    q_spec, k_spec, v_spec = _attn_specs(s, tq)
    return _attn_call(
        body, name, (hp_n, nq),
        [q_spec, k_spec, v_spec, pl.BlockSpec((None, tq, LANES), lambda hp, qi: (hp, qi, 0)),
         pl.BlockSpec((None, None, SUBLANES, LANES), lambda hp, qi: (hp, qi, 0, 0)),
         pl.BlockSpec((tq, LANES), lambda hp, qi: (qi, hp))],
        [pl.BlockSpec((3, s, LANES), lambda hp, qi: (0, 0, hp))], [jax.ShapeDtypeStruct((3, s, d), BF16)],
        [pltpu.VMEM((s, LANES), F32), pltpu.VMEM((s, LANES), F32)],
        ("parallel", "arbitrary"), (qkv, qkv, qkv, ltot, visited, do), rider)


def _fox_prep(fl, bias, name):
    s, w = fl.shape
    tb = _pick(s, 512, SUBLANES)

    def body(fl_ref, b_ref, cum_ref, carry_ref):
        i = pl.program_id(0)

        @pl.when(i == 0)
        def _():
            carry_ref[...] = jnp.zeros_like(carry_ref)

        logf = _log_sigmoid(fl_ref[...] + b_ref[...])
        row = lax.broadcasted_iota(jnp.int32, (tb, tb), 0)
        col = lax.broadcasted_iota(jnp.int32, (tb, tb), 1)
        incl = (col <= row).astype(BF16)
        tot = carry_ref[...]
        rem = logf
        for _ in range(3):
            part = rem.astype(BF16)
            tot = tot + jnp.dot(incl, part, preferred_element_type=F32)
            rem = rem - part.astype(F32)
        cum_ref[...] = tot
        carry_ref[...] = tot[tb - 1:tb, :]

    blk = pl.BlockSpec((tb, w), lambda i: (i, 0))
    return pl.pallas_call(body, name=name, grid=(s // tb,), in_specs=[blk, pl.BlockSpec((1, w), lambda i: (0, 0))],
                          out_specs=blk, out_shape=jax.ShapeDtypeStruct((s, w), F32),
                          scratch_shapes=[pltpu.VMEM((1, w), F32)], compiler_params=_params("arbitrary"))(fl, bias)


def _fox_gate_bwd(dcum, fl, bias, n_heads, name):
    s, w = fl.shape
    tb = _pick(s, 512, SUBLANES)
    nb = s // tb

    def body(dc_ref, fl_ref, b_ref, dfl_ref, db_ref, carry_ref):
        i = pl.program_id(0)

        @pl.when(i == 0)
        def _():
            carry_ref[...] = jnp.zeros_like(carry_ref)
            db_ref[...] = jnp.zeros_like(db_ref)

        row = lax.broadcasted_iota(jnp.int32, (tb, tb), 0)
        col = lax.broadcasted_iota(jnp.int32, (tb, tb), 1)
        incl = (col >= row).astype(BF16)
        tot = jnp.broadcast_to(carry_ref[...], (tb, w))
        rem = dc_ref[...]
        for _ in range(3):
            part = rem.astype(BF16)
            tot = tot + jnp.dot(incl, part, preferred_element_type=F32)
            rem = rem - part.astype(F32)
        carry_ref[...] = tot[0:1, :]
        xg = fl_ref[...] + b_ref[...]
        e = jnp.exp(-jnp.abs(xg))
        sig_neg = jnp.where(xg >= 0.0, e, 1.0) / (1.0 + e)
        lane = lax.broadcasted_iota(jnp.int32, (tb, w), 1)
        dfl = jnp.where(lane < n_heads, tot * sig_neg, 0.0)
        dfl_ref[...] = dfl.astype(BF16)
        db_ref[...] += jnp.sum(dfl, axis=0, keepdims=True)

    blk = pl.BlockSpec((tb, w), lambda i: (nb - 1 - i, 0))
    vec = pl.BlockSpec((1, w), lambda i: (0, 0))
    return pl.pallas_call(body, name=name, grid=(nb,), in_specs=[blk, blk, vec], out_specs=[blk, vec],
                          out_shape=[jax.ShapeDtypeStruct((s, w), BF16), jax.ShapeDtypeStruct((1, w), F32)],
                          scratch_shapes=[pltpu.VMEM((1, w), F32)], compiler_params=_params("arbitrary"))(dcum, fl, bias)


def _head_columns(cum_blk, hp):
    lane = lax.broadcasted_iota(jnp.int32, cum_blk.shape, 1)
    return tuple(jnp.sum(jnp.where(lane == 2 * hp + h, cum_blk, 0.0), axis=1, keepdims=True) for h in range(2))


def _fox_fwd(qkv, kbound, cum, ck, tq, tk, name, rider=None):
    _, s, d = qkv.shape
    hp_n, nq, nd = d // LANES, s // tq, tq // tk
    scale = HEAD_DIM ** -0.5

    def body(q_ref, k_ref, v_ref, kb_ref, cq_ref, ck_ref, o_ref, lse_ref, nb_ref):
        qi = pl.program_id(1)
        low = lax.broadcasted_iota(jnp.int32, (tq, LANES), 1) < HEAD_DIM
        qpair = _pair(q_ref[...] * scale)
        cqs = _head_columns(cq_ref[...], pl.program_id(0))
        kbv = kb_ref[...]
        reach = [jnp.sqrt(jnp.sum(jnp.square(qh.astype(F32)), axis=1, keepdims=True)) * (1.01 * km) + cqh
                 for qh, km, cqh in zip(qpair, (kbv[0:1, 0:1], kbv[0:1, HEAD_DIM:HEAD_DIM + 1]), cqs)]

        def alive(stats, kb):
            ks = pl.multiple_of(kb * tk, tk)
            return functools.reduce(jnp.logical_or, [
                jnp.max(reach[h] - ck_ref[h:h + 1, pl.ds(ks, tk)][:, 0:1] - stats[2 * h]) >= EXP_UNDERFLOW
                for h in range(2)])

        def block(kb, carry, mask):
            acc, stats = carry[0], carry[1:]
            ks = pl.multiple_of(kb * tk, tk)
            kblk = k_ref[pl.ds(ks, tk), :]
            vpair = _pair(v_ref[pl.ds(ks, tk), :])
            probs, alphas, new = [], [], []
            for h, (qh, cqh) in enumerate(zip(qpair, cqs)):
                m, lsum = stats[2 * h], stats[2 * h + 1]
                z = lax.dot_general(qh, kblk, NT, preferred_element_type=F32)
                sc = z + (cqh - ck_ref[h:h + 1, pl.ds(ks, tk)])
                if mask is not None:
                    sc = jnp.where(mask, sc, -jnp.inf)
                m_new = jnp.maximum(m, jnp.max(sc, axis=1, keepdims=True))
                alpha = jnp.exp(m - m_new)
                p = jnp.exp(sc - m_new)
                new += [m_new, alpha * lsum + jnp.sum(p, axis=1, keepdims=True)]
                probs.append(p.astype(BF16))
                alphas.append(alpha)
            acc = jnp.where(low, alphas[0], alphas[1]) * acc + jnp.dot(
                jnp.concatenate(probs, axis=1), jnp.concatenate(vpair, axis=0), preferred_element_type=F32)
            return (acc, *new)

        neg = jnp.full((tq, 1), -jnp.inf, F32)
        zero1 = jnp.zeros((tq, 1), F32)
        carry = (jnp.zeros((tq, LANES), F32), neg, zero1, neg, zero1)
        for i in range(nd):
            carry = block(qi * nd + i, carry, _rel_mask(tq, tk, i, False))

        def step(state):
            kb = qi * nd - 1 - state[0]
            new = block(kb, state[2:], None)
            return (state[0] + 1, alive(new[1:], kb), *new)

        done, _, acc, m0, l0, m1, l1 = lax.while_loop(lambda st: jnp.logical_and(st[0] < qi * nd, st[1]), step,
                                                      (jnp.int32(0), alive(carry[1:], qi * nd), *carry))
        o_ref[...] = (acc / jnp.where(low, l0, l1)).astype(BF16)
        lse_ref[...] = jnp.where(low, m0 + jnp.log(l0), m1 + jnp.log(l1))
        nb_ref[...] = jnp.zeros((SUBLANES, LANES), F32) + done.astype(F32)

    q_spec, k_spec, v_spec = _attn_specs(s, tq)
    pair_rows = pl.BlockSpec((None, tq, LANES), lambda hp, qi: (hp, qi, 0))
    return _attn_call(
        body, name, (hp_n, nq),
        [q_spec, k_spec, v_spec, pl.BlockSpec((None, SUBLANES, LANES), lambda hp, qi: (hp, 0, 0)),
         pl.BlockSpec((tq, LANES), lambda hp, qi: (qi, 0)), pl.BlockSpec((None, 2, s), lambda hp, qi: (hp, 0, 0))],
        [pl.BlockSpec((tq, LANES), lambda hp, qi: (qi, hp)), pair_rows,
         pl.BlockSpec((None, None, SUBLANES, LANES), lambda hp, qi: (hp, qi, 0, 0))],
        [jax.ShapeDtypeStruct((s, d), BF16), jax.ShapeDtypeStruct((hp_n, s, LANES), F32),
         jax.ShapeDtypeStruct((hp_n, nq, SUBLANES, LANES), F32)],
        [], ("parallel", "parallel"), (qkv, qkv, qkv, kbound, cum, ck), rider)


def _fox_bwd(qkv, o, do, lse, visited, cum, ck, tq, tk, name, rider=None):
    _, s, d = qkv.shape
    hp_n, nq, nd = d // LANES, s // tq, tq // tk
    scale = HEAD_DIM ** -0.5

    def body(q_ref, k_ref, v_ref, o_ref, do_ref, lse_ref, nb_ref, cq_ref, ck_ref, dqkv_ref, dcq_ref, dck_ref,
             dk_acc, dv_acc):
        qi = pl.program_id(1)
        first = qi * nd - jnp.max(nb_ref[...]).astype(jnp.int32)

        @pl.when(qi == 0)
        def _():
            dk_acc[...] = jnp.zeros_like(dk_acc)
            dv_acc[...] = jnp.zeros_like(dv_acc)
            dck_ref[...] = jnp.zeros_like(dck_ref)

        low = lax.broadcasted_iota(jnp.int32, (tq, LANES), 1) < HEAD_DIM
        dov, lsev = do_ref[...], lse_ref[...]
        qpair = _pair(q_ref[...] * scale)
        dopair = _pair(dov)
        prod = dov.astype(F32) * o_ref[...].astype(F32)
        deltas = (jnp.sum(jnp.where(low, prod, 0.0), axis=1, keepdims=True),
                  jnp.sum(jnp.where(low, 0.0, prod), axis=1, keepdims=True))
        cqs = _head_columns(cq_ref[...], pl.program_id(0))
        lses = (lsev[:, 0:1], lsev[:, HEAD_DIM:HEAD_DIM + 1])

        def block(kb, carry, mask):
            dq, rowsums = carry[0], carry[1:]
            ks = pl.multiple_of(kb * tk, tk)
            kblk, vblk = k_ref[pl.ds(ks, tk), :], v_ref[pl.ds(ks, tk), :]
            kpair = _pair(kblk * scale)
            dss, probs, new_rows = [], [], []
            for h, (qh, doh) in enumerate(zip(qpair, dopair)):
                z = lax.dot_general(qh, kblk, NT, preferred_element_type=F32)
                sc = z + (cqs[h] - ck_ref[h:h + 1, pl.ds(ks, tk)])
                p = jnp.exp(sc - lses[h])
                if mask is not None:
                    p = jnp.where(mask, p, 0.0)
                dp = lax.dot_general(doh, vblk, NT, preferred_element_type=F32)
                ds = p * (dp - deltas[h])
                dck_ref[h:h + 1, pl.ds(ks, tk)] -= jnp.sum(ds, axis=0, keepdims=True)
                new_rows.append(rowsums[h] + jnp.sum(ds, axis=1, keepdims=True))
                dss.append(ds.astype(BF16))
                probs.append(p.astype(BF16))
            dq = dq + jnp.dot(jnp.concatenate(dss, axis=1), jnp.concatenate(kpair, axis=0),
                              preferred_element_type=F32)
            dk_acc[pl.ds(ks, tk), :] += lax.dot_general(jnp.concatenate(dss, axis=0), jnp.concatenate(qpair, axis=0),
                                                       TN, preferred_element_type=F32)
            dv_acc[pl.ds(ks, tk), :] += lax.dot_general(jnp.concatenate(probs, axis=0), jnp.concatenate(dopair, axis=0),
                                                       TN, preferred_element_type=F32)
            return (dq, *new_rows)

        zero1 = jnp.zeros((tq, 1), F32)
        carry = lax.fori_loop(first, qi * nd, lambda j, cr: block(j, cr, None),
                              (jnp.zeros((tq, LANES), F32), zero1, zero1))
        for i in range(nd):
            carry = block(qi * nd + i, carry, _rel_mask(tq, tk, i, False))
        dq, rs0, rs1 = carry
        dqkv_ref[0, pl.ds(pl.multiple_of(qi * tq, tq), tq), :] = dq.astype(BF16)
        dcq_ref[...] = jnp.where(low, rs0, rs1)

        @pl.when(qi == nq - 1)
        def _():
            dqkv_ref[1] = dk_acc[...].astype(BF16)
            dqkv_ref[2] = dv_acc[...].astype(BF16)

    q_spec, k_spec, v_spec = _attn_specs(s, tq)
    pair_rows = pl.BlockSpec((None, tq, LANES), lambda hp, qi: (hp, qi, 0))
    tile = pl.BlockSpec((tq, LANES), lambda hp, qi: (qi, hp))
    keys = pl.BlockSpec((None, 2, s), lambda hp, qi: (hp, 0, 0))
    return _attn_call(
        body, name, (hp_n, nq),
        [q_spec, k_spec, v_spec, tile, tile, pair_rows,
         pl.BlockSpec((None, None, SUBLANES, LANES), lambda hp, qi: (hp, qi, 0, 0)),
         pl.BlockSpec((tq, LANES), lambda hp, qi: (qi, 0)), keys],
        [pl.BlockSpec((3, s, LANES), lambda hp, qi: (0, 0, hp)), pair_rows, keys],
        [jax.ShapeDtypeStruct((3, s, d), BF16), jax.ShapeDtypeStruct((hp_n, s, LANES), F32),
         jax.ShapeDtypeStruct((hp_n, 2, s), F32)],
        [pltpu.VMEM((s, LANES), F32), pltpu.VMEM((s, LANES), F32)],
        ("parallel", "arbitrary"), (qkv, qkv, qkv, o, do, lse, visited, cum, ck), rider)


def _loss_head(y, target, name):
    s, d = y.shape
    tm = _pick(s, 512, SUBLANES)

    def body(y_ref, t_ref, dy_ref, sq_ref):
        i = pl.program_id(0)
        diff = y_ref[...] - t_ref[...]
        dy_ref[...] = diff / d

        @pl.when(i == 0)
        def _():
            sq_ref[...] = jnp.zeros_like(sq_ref)

        sq_ref[...] += jnp.sum(diff * diff, axis=0, keepdims=True)

    row = pl.BlockSpec((tm, d), lambda i: (i, 0))
    vec = pl.BlockSpec((1, d), lambda i: (0, 0))
    return pl.pallas_call(body, name=name, grid=(s // tm,), in_specs=[row, row], out_specs=[row, vec],
                          out_shape=[jax.ShapeDtypeStruct((s, d), F32), jax.ShapeDtypeStruct((1, d), F32)],
                          compiler_params=_params("arbitrary"))(y, target)


def _mod_fwd(c_all, w_mod, b_mod_cols, name):
    nl, d, cols = w_mod.shape
    nb = c_all.shape[0]

    def body(c_ref, w_ref, b_ref, o_ref):
        cv = c_ref[...]
        act = (cv * jax.nn.sigmoid(cv)).astype(BF16)
        o_ref[...] = jnp.dot(act, w_ref[...].astype(BF16), preferred_element_type=F32) + b_ref[...]

    return pl.pallas_call(
        body, name=name, grid=(nl,),
        in_specs=[pl.BlockSpec((nb, d), lambda l: (0, 0)), pl.BlockSpec((None, d, cols), lambda l: (l, 0, 0)),
                  pl.BlockSpec((None, 1, cols), lambda l: (l, 0, 0))],
        out_specs=pl.BlockSpec((None, nb, cols), lambda l: (l, 0, 0)),
        out_shape=jax.ShapeDtypeStruct((nl, nb, cols), F32), compiler_params=_params("parallel"))(c_all, w_mod, b_mod_cols)


def _wmod_grad(c_all_t, dmod, name):
    d, nb = c_all_t.shape
    _, nl, cols = dmod.shape

    def body(c_ref, dm_ref, o_ref):
        cv = c_ref[...]
        act = cv * jax.nn.sigmoid(cv)
        tot = act[:, 0:1] * dm_ref[0]
        for b in range(1, nb):
            tot = tot + act[:, b:b + 1] * dm_ref[b]
        o_ref[...] = tot

    return pl.pallas_call(
        body, name=name, grid=(nl,),
        in_specs=[pl.BlockSpec((d, nb), lambda l: (0, 0)), pl.BlockSpec((nb, None, 1, cols), lambda l: (0, l, 0, 0))],
        out_specs=pl.BlockSpec((None, d, cols), lambda l: (l, 0, 0)),
        out_shape=jax.ShapeDtypeStruct((nl, d, cols), F32), compiler_params=_params("parallel"))(
            c_all_t, dmod.reshape(nb, nl, 1, cols))


def _adamw(recv, w, m, v, name, *, tr=256, tc=512):
    nq, nl, rp, cp = recv.shape
    _, r, c = w.shape
    br = _pick(r, tr, SUBLANES) if rp == r else r
    bc = _pick(c, tc, LANES) if cp == c else c
    rbr = br if rp == r else rp
    rbc = bc if cp == c else cp

    def body(rv_ref, w_ref, m_ref, v_ref, g_ref, d_ref, nm_ref, nv_ref):
        g = rv_ref[0, :br, :bc].astype(F32)
        for qd in range(1, nq):
            g = g + rv_ref[qd, :br, :bc].astype(F32)
        m_new = ADAM_B1 * m_ref[...] + (1.0 - ADAM_B1) * g
        v_new = ADAM_B2 * v_ref[...] + (1.0 - ADAM_B2) * jnp.square(g)
        m_hat = m_new / (1.0 - ADAM_B1 ** ADAM_STEP)
        v_hat = v_new / (1.0 - ADAM_B2 ** ADAM_STEP)
        g_ref[...] = g
        d_ref[...] = -ADAM_LR * (m_hat / (jnp.sqrt(v_hat) + ADAM_EPS) + ADAM_WD * w_ref[...])
        nm_ref[...] = m_new
        nv_ref[...] = v_new

    blk = pl.BlockSpec((None, br, bc), lambda l, i, j: (l, i, j))
    rblk = pl.BlockSpec((nq, None, rbr, rbc), lambda l, i, j: (0, l, i, j))
    shape = jax.ShapeDtypeStruct(w.shape, F32)
    return pl.pallas_call(body, name=name, grid=(nl, r // br, c // bc), in_specs=[rblk, blk, blk, blk],
                          out_specs=[blk] * 4, out_shape=[shape] * 4,
                          compiler_params=_params("parallel", "parallel", "parallel"))(recv, w, m, v)


def _mesh_pos():
    return lax.axis_index("x"), lax.axis_index("y"), lax.axis_index("c")


def _allgather_small(block, name):
    m_per, n = block.shape

    def body(x_ref, out_ref, send_sems, recv_sems, local_sem):
        x, y, c = _mesh_pos()
        me, sibling = (x, y, c), (x, y, 1 - c)
        chips = [(1 - x, y), (x, 1 - y), (1 - x, 1 - y)]

        def rows(px, py, pc):
            return out_ref.at[pl.ds((4 * px + 2 * py + pc) * m_per, m_per), :]

        def copy(k, blk, to, src=None):
            return pltpu.make_async_remote_copy(
                src_ref=rows(*blk) if src is None else src, dst_ref=rows(*blk),
                send_sem=send_sems.at[k], recv_sem=recv_sems.at[k], device_id=to, device_id_type=MESH)

        mine = pltpu.make_async_copy(x_ref, rows(*me), local_sem)
        mine.start()
        first = [copy(0, me, sibling, src=x_ref)]
        first += [copy(1 + j, me, (*chip, c), src=x_ref) for j, chip in enumerate(chips)]
        for cp in first:
            cp.start()
        passed = [copy(4 + j, (*chip, c), sibling) for j, chip in enumerate(chips)]
        for j, chip in enumerate(chips):
            copy(1 + j, (*chip, c), me).wait_recv()
            passed[j].start()
        copy(0, sibling, me).wait_recv()
        for j, chip in enumerate(chips):
            copy(4 + j, (*chip, 1 - c), me).wait_recv()
        for cp in first + passed:
            cp.wait_send()
        mine.wait()

    return pl.pallas_call(
        body, name=name, out_shape=jax.ShapeDtypeStruct((N_DEV * m_per, n), block.dtype),
        in_specs=[pl.BlockSpec(memory_space=pltpu.VMEM)], out_specs=pl.BlockSpec(memory_space=pltpu.VMEM),
        scratch_shapes=[pltpu.SemaphoreType.DMA((7,)), pltpu.SemaphoreType.DMA((7,)), pltpu.SemaphoreType.DMA],
        compiler_params=pltpu.CompilerParams(vmem_limit_bytes=VMEM_LIMIT_BYTES))(block)


def _exchange(srcs, dsts, jobs, name):
    n_src, n_job, n_dst = len(srcs), len(jobs), len(dsts)

    def body(*refs):
        src_refs, dst_refs = refs[:n_src], refs[n_src + n_dst:n_src + 2 * n_dst]
        send_sems, recv_sems, local_sems = refs[n_src + 2 * n_dst:]
        x, y, c = _mesh_pos()
        me = 4 * x + 2 * y + c
        pending = []
        for t, (si, di, src_slice, dst_slice) in enumerate(jobs):
            src, dst = src_refs[si], dst_refs[di]
            lc = pltpu.make_async_copy(src_slice(src, me), dst_slice(dst, me), local_sems.at[t])
            lc.start()
            pending.append(lc)
            for dd in range(1, N_DEV):
                px = 1 - x if dd & 4 else x
                py = 1 - y if dd & 2 else y
                pc = 1 - c if dd & 1 else c
                cp = pltpu.make_async_remote_copy(
                    src_ref=src_slice(src, 4 * px + 2 * py + pc), dst_ref=dst_slice(dst, me),
                    send_sem=send_sems.at[t, dd - 1], recv_sem=recv_sems.at[t, dd - 1],
                    device_id=(px, py, pc), device_id_type=MESH)
                cp.start()
                pending.append(cp)
        for cp in pending:
            cp.wait()

    hbm = pl.BlockSpec(memory_space=pl.ANY)
    return pl.pallas_call(
        body, name=name, out_shape=[jax.ShapeDtypeStruct(a.shape, a.dtype) for a in dsts],
        in_specs=[hbm] * (n_src + n_dst), out_specs=[hbm] * n_dst,
        input_output_aliases={n_src + i: i for i in range(n_dst)},
        scratch_shapes=[pltpu.SemaphoreType.DMA((n_job, N_DEV - 1)), pltpu.SemaphoreType.DMA((n_job, N_DEV - 1)),
                        pltpu.SemaphoreType.DMA((n_job,))])(*srcs, *dsts)


def _attn_call(body, name, grid, in_specs, out_specs, out_shape, scratch_shapes, semantics, operands, rider):
    out_specs, out_shape, scratch_shapes = list(out_specs), list(out_shape), list(scratch_shapes)
    if rider is None:
        res = pl.pallas_call(body, name=name, grid=grid, in_specs=list(in_specs), out_specs=out_specs,
                             out_shape=out_shape, scratch_shapes=scratch_shapes,
                             compiler_params=_params(*semantics))(*operands)
        return list(res), None
    srcs, dsts, jobs = rider
    n_in, n_out, n_scr, n_src, n_dst = len(operands), len(out_shape), len(scratch_shapes), len(srcs), len(dsts)

    def carrying(*refs):
        src_refs = refs[n_in:n_in + n_src]
        at = n_in + n_src + n_dst
        dst_refs = refs[at + n_out:at + n_out + n_dst]
        scratch = refs[at + n_out + n_dst:at + n_out + n_dst + n_scr]
        send_sems, recv_sems = refs[-2], refs[-1]
        ids = [pl.program_id(a) for a in range(len(grid))]
        first = functools.reduce(jnp.logical_and, [i == 0 for i in ids])
        last = functools.reduce(jnp.logical_and, [i == g - 1 for i, g in zip(ids, grid)])

        @pl.when(first)
        def _():
            local, remote = _exchange_copies(jobs, src_refs, dst_refs, send_sems, recv_sems, True)
            for cp in remote + local:
                cp.start()

        body(*refs[:n_in], *refs[at:at + n_out], *scratch)

        @pl.when(last)
        def _():
            local, remote = _exchange_copies(jobs, src_refs, dst_refs, send_sems, recv_sems, False)
            for cp in local:
                cp.wait()
            for cp in remote:
                cp.wait_send()
                cp.wait_recv()

    hbm = pl.BlockSpec(memory_space=pl.ANY)
    sems = pltpu.SemaphoreType.DMA((len(jobs) * N_DEV,))
    res = pl.pallas_call(
        carrying, name=name, grid=grid, in_specs=list(in_specs) + [hbm] * (n_src + n_dst),
        out_specs=out_specs + [hbm] * n_dst,
        out_shape=out_shape + [jax.ShapeDtypeStruct(a.shape, a.dtype) for a in dsts],
        input_output_aliases={n_in + n_src + i: n_out + i for i in range(n_dst)},
        scratch_shapes=scratch_shapes + [sems, sems],
        compiler_params=_params(*["arbitrary"] * len(grid)))(*operands, *srcs, *dsts)
    return list(res[:n_out]), list(res[n_out:])


def _peer(x, y, c, dd):
    return (1 - x if dd & 4 else x, 1 - y if dd & 2 else y, 1 - c if dd & 1 else c)


def _exchange_copies(jobs, src_refs, dst_refs, send_sems, recv_sems, sending):
    x, y, c = _mesh_pos()
    me = 4 * x + 2 * y + c
    local, remote = [], []
    for t, (si, di, src_slice, dst_slice) in enumerate(jobs):
        local.append(pltpu.make_async_copy(src_slice(src_refs[si], me), dst_slice(dst_refs[di], me),
                                           send_sems.at[t * N_DEV]))
        for dd in range(1, N_DEV):
            px, py, pc = _peer(x, y, c, dd)
            p = 4 * px + 2 * py + pc
            remote.append(pltpu.make_async_remote_copy(
                src_ref=src_slice(src_refs[si], p), dst_ref=dst_slice(dst_refs[di], me if sending else p),
                send_sem=send_sems.at[t * N_DEV + dd], recv_sem=recv_sems.at[t * N_DEV + dd],
                device_id=(px, py, pc), device_id_type=MESH))
    return local, remote


def _whole(ref, p):
    return ref


def _layer_of(layer, inner):
    return lambda ref, p: inner(ref.at[layer], p)


def _cols_of(width):
    def take(ref, p):
        lead = (slice(None),) * (len(ref.shape) - 1)
        return ref.at[lead + (pl.ds(pl.multiple_of(p * width, LANES), width),)]
    return take


def _rows_of(height):
    def take(ref, p):
        lead = (slice(None),) * (len(ref.shape) - 2)
        return ref.at[lead + (pl.ds(pl.multiple_of(p * height, SUBLANES), height), slice(None))]
    return take


def _slot(layer=None):
    if layer is None:
        return lambda ref, p: ref.at[p]
    return lambda ref, p: ref.at[p, layer]


def _local_step(x0, target, mod, g_mix_pre, g_mix_post, g_ffn_pre, g_ffn_post, wqkv, wo, wg, wu, wd,
                wfg, bfg, wconv, bconv, tiles, comm=None):
    s, d = x0.shape
    nl = mod.shape[0]
    n_heads = d // HEAD_DIM
    hp_n = d // LANES
    assert tiles["sb_fwd"] == tiles["sb_bwd"] and tiles["fox_fwd"] == tiles["fox_bwd"]
    vec = lambda a: a.reshape(1, -1)
    saved = []
    xcur = x0
    for l in range(nl):
        if comm is not None:
            wqkv, wo, wg, wu, wd = comm["weights"]()
        sh_a, sc_a, gt_a, sh_f, sc_f, gt_f = (vec(mod[l, i * d:(i + 1) * d]) for i in range(6))
        fox = l % 2 == 1
        h1 = _norm_mod_fwd(xcur, vec(g_mix_pre[l]), sh_a, sc_a, f"norm_mix_fwd_{l}")
        qkv = _mm_nn(h1, wqkv, l, BF16, f"qkv_fwd_{l}", split_out=d)
        rider = None if comm is None else comm["fwd_rider"](l)
        if fox:
            j = l // 2
            fl = _mm_nn(h1, wfg, j, F32, f"fgate_fwd_{l}")
            cum = _fox_prep(fl, bfg[j], f"fox_prep_{l}")
            ck = cum[:, :n_heads].T.reshape(hp_n, 2, s)
            (o, *stat), moved = _fox_fwd(qkv, _key_bound(qkv, f"key_bound_{l}"), cum, ck, *tiles["fox_fwd"],
                                         f"fox_fwd_{l}", rider)
            extra = (fl, cum, ck)
        else:
            (o, *stat), moved = _sb_fwd(qkv, _key_bound(qkv, f"key_bound_{l}"), *tiles["sb_fwd"], f"sb_fwd_{l}", rider)
            extra = None
        if moved is not None:
            comm["done"](moved)
            wqkv, wo, wg, wu, wd = comm["weights"]()
        u = _mm_nn(o, wo, l, F32, f"attn_out_fwd_{l}")
        x2 = _post_fwd(xcur, u, vec(g_mix_post[l]), gt_a, f"post_mix_fwd_{l}")
        h2 = _norm_mod_fwd(x2, vec(g_ffn_pre[l]), sh_f, sc_f, f"norm_ffn_fwd_{l}")
        gp = _mm_nn(h2, wg, l, BF16, f"ffn_gate_fwd_{l}")
        up = _mm_nn(h2, wu, l, BF16, f"ffn_up_fwd_{l}")
        act = _conv_act_fwd(gp, up, wconv[l], bconv[l], f"conv_act_fwd_{l}")
        yv = _mm_nn(act, wd, l, F32, f"ffn_down_fwd_{l}")
        x3 = _post_fwd(x2, yv, vec(g_ffn_post[l]), gt_f, f"post_ffn_fwd_{l}")
        saved.append((xcur, h1, qkv, o, stat, extra, u, x2, h2, gp, up, act, yv))
        xcur = x3

    dx, sq = _loss_head(xcur, target, "loss_head")
    small, big = [None] * nl, [None] * nl
    for l in reversed(range(nl)):
        xin, h1, qkv, o, stat, extra, u, x2, h2, gp, up, act, yv = saved[l]
        sc_a, gt_a, sc_f, gt_f = (vec(mod[l, i * d:(i + 1) * d]) for i in (1, 2, 4, 5))
        fox = l % 2 == 1
        dy, dgt_f, dg4 = _post_bwd(dx, yv, vec(g_ffn_post[l]), gt_f, f"post_ffn_bwd_{l}")
        dact = _mm_nt(dy, wd, l, BF16, f"ffn_down_dx_{l}")
        dwd = _mm_tn(act, dy, BF16, f"ffn_down_dw_{l}")
        dgp, dup, dwc, dbc = _conv_act_bwd(dact, gp, up, wconv[l], bconv[l], f"conv_act_bwd_{l}")
        dh2 = _mm_nt(dgp, wg, l, F32, f"ffn_gate_dx_{l}")
        dh2 = _mm_nt(dup, wu, l, F32, f"ffn_up_dx_{l}", add=dh2)
        dwg = _mm_tn(h2, dgp, BF16, f"ffn_gate_dw_{l}")
        dwu = _mm_tn(h2, dup, BF16, f"ffn_up_dw_{l}")
        dx2, dsh_f, dsc_f, dg3 = _norm_mod_bwd(dh2, x2, vec(g_ffn_pre[l]), sc_f, dx, f"norm_ffn_bwd_{l}")
        du, dgt_a, dg2 = _post_bwd(dx2, u, vec(g_mix_post[l]), gt_a, f"post_mix_bwd_{l}")
        do = _mm_nt(du, wo, l, BF16, f"attn_out_dx_{l}")
        dwo = _mm_tn(o, du, BF16, f"attn_out_dw_{l}")
        rider = None
        if comm is not None:
            comm["grads"](l, dict(gate=dwg, up=dwu, down=dwd))
            rider = comm["bwd_rider"]()
        if fox:
            j = l // 2
            fl, cum, ck = extra
            (dqkv, dcq, dck), moved = _fox_bwd(qkv, o, do, *stat, cum, ck, *tiles["fox_bwd"], f"fox_bwd_{l}", rider)
            dcq = jnp.max(dcq.reshape(hp_n, s, 2, HEAD_DIM), axis=3)
            dcum = dcq.transpose(1, 0, 2).reshape(s, n_heads) + dck.reshape(n_heads, s).T
            dcum = jnp.pad(dcum, ((0, 0), (0, LANES - n_heads)))
            dfl, dbfg = _fox_gate_bwd(dcum, fl, bfg[j], n_heads, f"fox_gate_bwd_{l}")
            dh1 = _mm_nt(dfl, wfg, j, F32, f"fgate_dx_{l}")
            dh1 = _mm_nt(dqkv, wqkv, l, F32, f"qkv_dx_{l}", add=dh1)
            dwfg = _mm_tn(h1, dfl, F32, f"fgate_dw_{l}")[:, :n_heads]
            dbfg = dbfg[0, :n_heads]
        else:
            (dqkv,), moved = _sb_bwd(qkv, *stat, do, *tiles["sb_bwd"], f"sb_bwd_{l}", rider)
            dh1 = _mm_nt(dqkv, wqkv, l, F32, f"qkv_dx_{l}")
            dwfg = dbfg = None
        if moved is not None:
            comm["done"](moved)
        dwqkv = _mm_tn(h1, dqkv, BF16, f"qkv_dw_{l}")
        dx, dsh_a, dsc_a, dg1 = _norm_mod_bwd(dh1, xin, vec(g_mix_pre[l]), sc_a, dx2, f"norm_mix_bwd_{l}")
        dmod = jnp.concatenate([dsh_a, dsc_a, dgt_a, dsh_f, dsc_f, dgt_f], axis=1)[0]
        small[l] = dict(dmod=dmod, dg1=dg1[0], dg2=dg2[0], dg3=dg3[0], dg4=dg4[0], dbc=dbc[0], dwc=dwc,
                        dbfg=dbfg, dwfg=dwfg)
        big[l] = dict(qkv=dwqkv, o=dwo, gate=dwg, up=dwu, down=dwd)
        if comm is not None:
            comm["grads"](l, dict(qkv=dwqkv, o=dwo))
    return sq, dx, small, big


def _rows128(a, rows):
    flat = a.reshape(-1)
    return jnp.pad(flat, (0, rows * LANES - flat.shape[0])).reshape(rows, LANES)


def _ceil8(n_elems):
    rows = -(-n_elems // LANES)
    return -(-rows // SUBLANES) * SUBLANES


def kernel(x, c, w_mod, b_mod, g_mix_pre, g_mix_post, w_qkv, w_o, w_fg, b_fg, g_ffn_pre, g_ffn_post, w_ffn_gate, w_ffn_up, w_conv, b_conv, w_ffn_down, loss_target, m_w_mod, m_b_mod, m_g_mix_pre, m_g_mix_post, m_w_qkv, m_w_o, m_w_fg, m_b_fg, m_g_ffn_pre, m_g_ffn_post, m_w_ffn_gate, m_w_ffn_up, m_w_conv, m_b_conv, m_w_ffn_down, v_w_mod, v_b_mod, v_g_mix_pre, v_g_mix_post, v_w_qkv, v_w_o, v_w_fg, v_b_fg, v_g_ffn_pre, v_g_ffn_post, v_w_ffn_gate, v_w_ffn_up, v_w_conv, v_b_conv, v_w_ffn_down):
    _, s, d = x.shape
    nl = w_qkv.shape[0]
    nf = w_fg.shape[0]
    n_heads = w_fg.shape[2]
    fs = w_ffn_gate.shape[2]
    fp = -(-fs // LANES) * LANES
    f_full, f_pad = N_DEV * fs, N_DEV * fp
    mod_cols = w_mod.shape[2]
    qs, orows = w_qkv.shape[2], w_o.shape[1]
    xi, yi, ci = _mesh_pos()
    me = 4 * xi + 2 * yi + ci

    c_rows = d // LANES
    c_all = _allgather_small(jnp.pad(c.reshape(1, d), ((0, SUBLANES - 1), (0, 0))).reshape(SUBLANES * c_rows, LANES),
                             "gather_cond")
    c_all = c_all.reshape(N_DEV, SUBLANES, d)[:, 0, :]
    b_mod_cols = lax.dynamic_slice_in_dim(b_mod, me * mod_cols, mod_cols, axis=1).reshape(nl, 1, mod_cols)
    mod_part = _mod_fwd(c_all, w_mod, b_mod_cols, "mod_fwd")

    conv_pad = jnp.pad(w_conv, ((0, 0), (0, 0), (0, fp - fs)))
    r_mod, r_conv, r_fg = _ceil8(mod_part.size), _ceil8(conv_pad.size), _ceil8(w_fg.size)
    payload = jnp.concatenate([_rows128(mod_part, r_mod), _rows128(conv_pad, r_conv), _rows128(w_fg, r_fg)], axis=0)
    got = _allgather_small(payload, "gather_small_weights").reshape(N_DEV, r_mod + r_conv + r_fg, LANES)
    mod_g = got[:, :r_mod].reshape(N_DEV, -1)[:, :mod_part.size].reshape(N_DEV, nl, N_DEV, mod_cols)
    mod = lax.dynamic_index_in_dim(mod_g, me, axis=2, keepdims=False).transpose(1, 0, 2).reshape(nl, N_DEV * mod_cols)
    conv_g = got[:, r_mod:r_mod + r_conv].reshape(N_DEV, -1)[:, :conv_pad.size].reshape(N_DEV, nl, 3, fp)
    wconv_full = conv_g.transpose(1, 2, 0, 3).reshape(nl, 3, f_pad)
    fg_g = got[:, r_mod + r_conv:].reshape(N_DEV, -1)[:, :w_fg.size].reshape(N_DEV, nf, orows, n_heads)
    wfg_full = fg_g.transpose(1, 0, 2, 3).reshape(nf, d, n_heads)
    wfg_full = jnp.pad(wfg_full, ((0, 0), (0, 0), (0, LANES - n_heads))).astype(BF16)
    bfg_full = jnp.pad(b_fg, ((0, 0), (0, LANES - n_heads))).reshape(nf, 1, LANES)
    bconv_full = jnp.pad(b_conv.reshape(nl, N_DEV, fs), ((0, 0), (0, 0), (0, fp - fs))).reshape(nl, 1, f_pad)

    gate_sh = jnp.pad(w_ffn_gate, ((0, 0), (0, 0), (0, fp - fs))).astype(BF16)
    up_sh = jnp.pad(w_ffn_up, ((0, 0), (0, 0), (0, fp - fs))).astype(BF16)
    down_sh = jnp.pad(w_ffn_down, ((0, 0), (0, fp - fs), (0, 0))).astype(BF16)
    shards = [w_qkv.astype(BF16), w_o.astype(BF16), gate_sh, up_sh, down_sh]
    full_shapes = [jax.ShapeDtypeStruct((nl, d, N_DEV * qs), BF16), jax.ShapeDtypeStruct((nl, d, d), BF16),
                   jax.ShapeDtypeStruct((nl, d, f_pad), BF16), jax.ShapeDtypeStruct((nl, d, f_pad), BF16),
                   jax.ShapeDtypeStruct((nl, f_pad, d), BF16)]
    place = [_cols_of(qs), _rows_of(orows), _cols_of(fp), _cols_of(fp), _rows_of(fp)]
    mixer_w, ffn_w = (0, 1), (2, 3, 4)

    def gather_jobs(l, which):
        return [(i, i, _layer_of(l, _whole), _layer_of(l, place[i])) for i in which]

    order = ["qkv", "o", "gate", "up", "down"]
    send = {"qkv": _cols_of(qs), "o": _rows_of(orows), "gate": _cols_of(fp), "up": _cols_of(fp), "down": _rows_of(fp)}
    recv_shapes = [(N_DEV, nl, d, qs), (N_DEV, nl, orows, d), (N_DEV, nl, d, fp), (N_DEV, nl, d, fp), (N_DEV, nl, fp, d)]
    state = {"full": _exchange(shards, [lax.empty(sh.shape, sh.dtype) for sh in full_shapes],
                               gather_jobs(0, mixer_w), "gather_weights_0"),
             "recv": [lax.empty(sh, BF16) for sh in recv_shapes], "pending": [], "moving": None}

    def mixer_host(t):
        fox_before = [l for l in range(t) if l % 2 == 1]
        return fox_before[-1] if fox_before else t - 1

    def fwd_rider(l):
        state["moving"] = "full"
        jobs = gather_jobs(l, ffn_w)
        for t in range(1, nl):
            if mixer_host(t) == l:
                jobs += gather_jobs(t, mixer_w)
        return shards, state["full"], jobs

    def bwd_rider():
        waiting, state["pending"], state["moving"] = state["pending"], [], "recv"
        if not waiting:
            return None
        jobs = [(k, order.index(nm), send[nm], _slot(l)) for k, (l, nm, _) in enumerate(waiting)]
        return [g for _, _, g in waiting], state["recv"], jobs

    def done(dsts):
        state[state["moving"]] = dsts

    def grads(l, new):
        state["pending"] += [(l, nm, new[nm]) for nm in order if nm in new]

    comm = dict(weights=lambda: state["full"], fwd_rider=fwd_rider, bwd_rider=bwd_rider, done=done, grads=grads)
    sq, dx, small, big = _local_step(x[0], loss_target[0], mod, g_mix_pre, g_mix_post, g_ffn_pre, g_ffn_post,
                                     None, None, None, None, None, wfg_full, bfg_full, wconv_full, bconv_full,
                                     ATTN_TILES, comm)
    loss = lax.psum(0.5 * jnp.sum(sq) / d, ("x", "y", "c"))
    recv = _exchange(*bwd_rider(), "scatter_grads_last")
    upd = {}
    for nm, rv, wt, mt, vt in zip(order, recv, [w_qkv, w_o, w_ffn_gate, w_ffn_up, w_ffn_down],
                                  [m_w_qkv, m_w_o, m_w_ffn_gate, m_w_ffn_up, m_w_ffn_down],
                                  [v_w_qkv, v_w_o, v_w_ffn_gate, v_w_ffn_up, v_w_ffn_down]):
        upd[nm] = _adamw(rv, wt, mt, vt, f"adamw_{nm}")

    stack = lambda key: jnp.stack([small[l][key] for l in range(nl)])
    dmod = stack("dmod")
    dgs = [stack(k) for k in ("dg1", "dg2", "dg3", "dg4")]
    dbc = stack("dbc").reshape(nl, N_DEV, fp)[:, :, :fs].reshape(nl, f_full)
    dbfg = jnp.stack([small[l]["dbfg"] for l in range(nl) if l % 2 == 1])
    dwc = stack("dwc").reshape(nl, 3, N_DEV, fp)[:, :, :, :fs].reshape(nl, 3, f_full)
    dwfg = jnp.stack([small[l]["dwfg"] for l in range(nl) if l % 2 == 1])
    rep_parts = [dmod] + dgs + [dbc, dbfg]
    rep_rows = [_ceil8(p.size) for p in rep_parts]
    r_rep, r_wc, r_wfg = sum(rep_rows), _ceil8(dwc.size), _ceil8(dwfg.size)
    payload = jnp.concatenate([_rows128(p, r) for p, r in zip(rep_parts, rep_rows)]
                              + [_rows128(dwc, r_wc), _rows128(dwfg, r_wfg)], axis=0)
    gsm = _allgather_small(payload, "gather_small_grads").reshape(N_DEV, 1, r_rep + r_wc + r_wfg, LANES)

    def pack(parts):
        return jnp.concatenate([_rows128(p, r) for p, r in zip(parts, rep_rows)], axis=0).reshape(1, r_rep, LANES)

    rep_w = [b_mod, g_mix_pre, g_mix_post, g_ffn_pre, g_ffn_post, b_conv, b_fg]
    rep_m = [m_b_mod, m_g_mix_pre, m_g_mix_post, m_g_ffn_pre, m_g_ffn_post, m_b_conv, m_b_fg]
    rep_v = [v_b_mod, v_g_mix_pre, v_g_mix_post, v_g_ffn_pre, v_g_ffn_post, v_b_conv, v_b_fg]
    rep_out = _adamw(gsm[:, :, :r_rep], pack(rep_w), pack(rep_m), pack(rep_v), "adamw_replicated", tr=r_rep, tc=LANES)

    def unpack(packed):
        outs, at = [], 0
        for p, r in zip(rep_w, rep_rows):
            outs.append(packed[0, at:at + r].reshape(-1)[:p.size].reshape(p.shape))
            at += r
        return outs

    rep_g, rep_d, rep_nm, rep_nv = (unpack(a) for a in rep_out)

    wc_all = gsm[:, 0, r_rep:r_rep + r_wc].reshape(N_DEV, -1)[:, :dwc.size].reshape(N_DEV, 1, nl * 3, f_full)
    wc_mine = lax.dynamic_slice_in_dim(wc_all, me * fs, fs, axis=3)
    wc_out = _adamw(wc_mine, w_conv.reshape(1, nl * 3, fs), m_w_conv.reshape(1, nl * 3, fs),
                    v_w_conv.reshape(1, nl * 3, fs), "adamw_conv", tr=nl * 3, tc=fs)
    wc_out = [a.reshape(nl, 3, fs) for a in wc_out]
    wfg_all = gsm[:, 0, r_rep + r_wc:].reshape(N_DEV, -1)[:, :dwfg.size].reshape(N_DEV, nf, d, n_heads)
    wfg_mine = lax.dynamic_slice_in_dim(wfg_all, me * orows, orows, axis=2)
    wfg_out = _adamw(wfg_mine, w_fg, m_w_fg, v_w_fg, "adamw_fgate", tr=orows, tc=n_heads)

    dmod_all = gsm[:, 0, :rep_rows[0]].reshape(N_DEV, -1)[:, :dmod.size].reshape(N_DEV, nl, N_DEV * mod_cols)
    dmod_mine = lax.dynamic_slice_in_dim(dmod_all, me * mod_cols, mod_cols, axis=2)
    gwmod = _wmod_grad(c_all.T, dmod_mine, "wmod_grad")
    wmod_out = _adamw(gwmod.reshape(1, nl, d, mod_cols), w_mod, m_w_mod, v_w_mod, "adamw_mod")

    per_weight = [wmod_out, None, None, None, upd["qkv"], upd["o"], wfg_out, None, None, None,
                  upd["gate"], upd["up"], wc_out, None, upd["down"]]
    rep_index = {1: 0, 2: 1, 3: 2, 8: 3, 9: 4, 13: 5, 7: 6}
    outs = [[], [], [], []]
    for pos, res in enumerate(per_weight):
        for kind in range(4):
            if res is None:
                outs[kind].append((rep_g, rep_d, rep_nm, rep_nv)[kind][rep_index[pos]])
            else:
                outs[kind].append(res[kind])
    return (loss, dx.reshape(1, s, d), *outs[0], *outs[1], *outs[2], *outs[3])
```

```python
import functools

import jax
import jax.numpy as jnp
from jax import lax
from jax.experimental import pallas as pl
from jax.experimental.pallas import tpu as pltpu

F32 = jnp.float32
BF16 = jnp.bfloat16

N_DEV = 8
HEAD_DIM = 64
LANES = 128
SUBLANES = 8
RMS_EPS = 1e-6
ADAM_LR = 0.001
ADAM_B1 = 0.9
ADAM_B2 = 0.999
ADAM_EPS = 1e-08
ADAM_WD = 0.01
ADAM_STEP = 10
VMEM_LIMIT_BYTES = 56 * 1024 * 1024
EXP_UNDERFLOW = -110.0
SUM_TERMS = 1
HALO = 16
MM_ROWS = 2048
ATTN_TILES = {"sb_fwd": (512, 256), "sb_bwd": (512, 256), "fox_fwd": (512, 512), "fox_bwd": (512, 512)}

NN = (((1,), (0,)), ((), ()))
NT = (((1,), (1,)), ((), ()))
TN = (((0,), (0,)), ((), ()))
MESH = pl.DeviceIdType.MESH


def _params(*sem):
    return pltpu.CompilerParams(dimension_semantics=sem, vmem_limit_bytes=VMEM_LIMIT_BYTES)


def _pick(n, pref, quantum):
    if n <= pref:
        return n
    t = (pref // quantum) * quantum
    while n % t:
        t -= quantum
    return t


def _mm(a, b, *, dims, grid, a_spec, b_spec, o_spec, out_shape, name, add=None):
    nk = grid[2]
    acc_shape = tuple(d for d in o_spec.block_shape if d is not None)
    o_dtype = out_shape.dtype

    def body(*refs):
        if add is None:
            a_ref, b_ref, o_ref, acc_ref = refs
            add_ref = None
        else:
            a_ref, b_ref, add_ref, o_ref, acc_ref = refs
        k = pl.program_id(2)
        part = lax.dot_general(a_ref[...], b_ref[...], dims, preferred_element_type=F32)

        def finish(total):
            if add_ref is not None:
                total = total + add_ref[...]
            o_ref[...] = total.astype(o_dtype)

        if nk == 1:
            finish(part)
            return

        @pl.when(k == 0)
        def _():
            acc_ref[...] = part

        @pl.when(jnp.logical_and(k > 0, k < nk - 1))
        def _():
            acc_ref[...] += part

        @pl.when(k == nk - 1)
        def _():
            finish(acc_ref[...] + part)

    operands = [a, b] if add is None else [a, b, add]
    in_specs = [a_spec, b_spec] if add is None else [a_spec, b_spec, o_spec]
    return pl.pallas_call(
        body, name=name, grid=grid, in_specs=in_specs, out_specs=o_spec, out_shape=out_shape,
        scratch_shapes=[pltpu.VMEM(acc_shape, F32)],
        compiler_params=_params("parallel", "parallel", "arbitrary"),
    )(*operands)


def _mm_rows(out_dtype, has_add):
    return MM_ROWS if (jnp.dtype(out_dtype).itemsize == 2 and not has_add) else MM_ROWS // 2


def _mm_nn(a, w, l, out_dtype, name, *, col0=0, n=None, split_out=None):
    m, kdim = a.shape
    n = w.shape[2] if n is None else n
    tm, tk = _pick(m, _mm_rows(out_dtype, False), SUBLANES), _pick(kdim, 1024, LANES)
    tn = _pick(n if split_out is None else split_out, 1024, LANES)
    jb = col0 // tn
    grid = (m // tm, n // tn, kdim // tk)
    a_spec = pl.BlockSpec((tm, tk), lambda i, j, k: (i, k))
    b_spec = pl.BlockSpec((None, tk, tn), lambda i, j, k: (l, k, j + jb))
    if split_out is None:
        o_spec = pl.BlockSpec((tm, tn), lambda i, j, k: (i, j))
        shape = jax.ShapeDtypeStruct((m, n), out_dtype)
    else:
        nj1 = split_out // tn
        o_spec = pl.BlockSpec((None, tm, tn), lambda i, j, k: (j // nj1, i, j % nj1))
        shape = jax.ShapeDtypeStruct((n // split_out, m, split_out), out_dtype)
    return _mm(a, w, dims=NN, grid=grid, a_spec=a_spec, b_spec=b_spec, o_spec=o_spec, out_shape=shape, name=name)


def _mm_nt(a, w, l, out_dtype, name, *, add=None):
    n, kdim = w.shape[1], w.shape[2]
    if a.ndim == 2:
        m = a.shape[0]
        tk = _pick(kdim, 1024, LANES)
        a_spec_of = lambda tm: pl.BlockSpec((tm, tk), lambda i, j, k: (i, k))
    else:
        m, seg = a.shape[1], a.shape[2]
        tk = _pick(seg, 1024, LANES)
        nk1 = seg // tk
        a_spec_of = lambda tm: pl.BlockSpec((None, tm, tk), lambda i, j, k: (k // nk1, i, k % nk1))
    tm, tn = _pick(m, _mm_rows(out_dtype, add is not None), SUBLANES), _pick(n, 1024, LANES)
    grid = (m // tm, n // tn, kdim // tk)
    b_spec = pl.BlockSpec((None, tn, tk), lambda i, j, k: (l, j, k))
    o_spec = pl.BlockSpec((tm, tn), lambda i, j, k: (i, j))
    return _mm(a, w, dims=NT, grid=grid, a_spec=a_spec_of(tm), b_spec=b_spec, o_spec=o_spec,
               out_shape=jax.ShapeDtypeStruct((m, n), out_dtype), name=name, add=add)


def _mm_tn(a, b, out_dtype, name):
    kdim, m = a.shape
    tk, tm = _pick(kdim, 1024, SUBLANES), _pick(m, _mm_rows(out_dtype, False), LANES)
    if b.ndim == 2:
        n = b.shape[1]
        tn = _pick(n, 1024, LANES)
        b_spec = pl.BlockSpec((tk, tn), lambda i, j, k: (k, j))
    else:
        seg = b.shape[2]
        n = b.shape[0] * seg
        tn = _pick(seg, 1024, LANES)
        nj1 = seg // tn
        b_spec = pl.BlockSpec((None, tk, tn), lambda i, j, k: (j // nj1, k, j % nj1))
    grid = (m // tm, n // tn, kdim // tk)
    a_spec = pl.BlockSpec((tk, tm), lambda i, j, k: (k, i))
    o_spec = pl.BlockSpec((tm, tn), lambda i, j, k: (i, j))
    return _mm(a, b, dims=TN, grid=grid, a_spec=a_spec, b_spec=b_spec, o_spec=o_spec,
               out_shape=jax.ShapeDtypeStruct((m, n), out_dtype), name=name)


def _rstd(x):
    return lax.rsqrt(jnp.mean(x * x, axis=-1, keepdims=True) + RMS_EPS)


def _norm_mod_fwd(x, g, shift, scale, name):
    s, d = x.shape
    tm = _pick(s, 512, SUBLANES)

    def body(x_ref, g_ref, sh_ref, sc_ref, h_ref):
        xv = x_ref[...]
        y = (xv * _rstd(xv)) * g_ref[...]
        h_ref[...] = (y * (1.0 + sc_ref[...]) + sh_ref[...]).astype(BF16)

    row = pl.BlockSpec((tm, d), lambda i: (i, 0))
    vec = pl.BlockSpec((1, d), lambda i: (0, 0))
    return pl.pallas_call(body, name=name, grid=(s // tm,), in_specs=[row, vec, vec, vec], out_specs=row,
                          out_shape=jax.ShapeDtypeStruct((s, d), BF16), compiler_params=_params("parallel"))(x, g, shift, scale)


def _norm_mod_bwd(dh, x, g, scale, dres, name):
    s, d = x.shape
    tm = _pick(s, 512, SUBLANES)

    def body(dh_ref, x_ref, g_ref, sc_ref, dres_ref, dx_ref, dsh_ref, dsc_ref, dg_ref):
        i = pl.program_id(0)
        xv, dhv, gv = x_ref[...], dh_ref[...], g_ref[...]
        r = _rstd(xv)
        xh = xv * r
        dn = dhv * (1.0 + sc_ref[...])
        gd = dn * gv
        dx_ref[...] = dres_ref[...] + r * (gd - xh * jnp.mean(gd * xh, axis=-1, keepdims=True))

        @pl.when(i == 0)
        def _():
            dsh_ref[...] = jnp.zeros_like(dsh_ref)
            dsc_ref[...] = jnp.zeros_like(dsc_ref)
            dg_ref[...] = jnp.zeros_like(dg_ref)

        dsh_ref[...] += jnp.sum(dhv, axis=0, keepdims=True)
        dsc_ref[...] += jnp.sum(dhv * (xh * gv), axis=0, keepdims=True)
        dg_ref[...] += jnp.sum(dn * xh, axis=0, keepdims=True)

    row = pl.BlockSpec((tm, d), lambda i: (i, 0))
    vec = pl.BlockSpec((1, d), lambda i: (0, 0))
    vshape = jax.ShapeDtypeStruct((1, d), F32)
    return pl.pallas_call(body, name=name, grid=(s // tm,), in_specs=[row, row, vec, vec, row],
                          out_specs=[row, vec, vec, vec],
                          out_shape=[jax.ShapeDtypeStruct((s, d), F32), vshape, vshape, vshape],
                          compiler_params=_params("arbitrary"))(dh, x, g, scale, dres)


def _post_fwd(x, u, g, gate, name):
    s, d = x.shape
    tm = _pick(s, 512, SUBLANES)

    def body(x_ref, u_ref, g_ref, gt_ref, o_ref):
        uv = u_ref[...]
        o_ref[...] = x_ref[...] + gt_ref[...] * ((uv * _rstd(uv)) * g_ref[...])

    row = pl.BlockSpec((tm, d), lambda i: (i, 0))
    vec = pl.BlockSpec((1, d), lambda i: (0, 0))
    return pl.pallas_call(body, name=name, grid=(s // tm,), in_specs=[row, row, vec, vec], out_specs=row,
                          out_shape=jax.ShapeDtypeStruct((s, d), F32), compiler_params=_params("parallel"))(x, u, g, gate)


def _post_bwd(dx, u, g, gate, name):
    s, d = u.shape
    tm = _pick(s, 512, SUBLANES)

    def body(dx_ref, u_ref, g_ref, gt_ref, du_ref, dgt_ref, dg_ref):
        i = pl.program_id(0)
        uv, dxv, gv = u_ref[...], dx_ref[...], g_ref[...]
        r = _rstd(uv)
        uh = uv * r
        dn = dxv * gt_ref[...]
        gd = dn * gv
        du_ref[...] = (r * (gd - uh * jnp.mean(gd * uh, axis=-1, keepdims=True))).astype(BF16)

        @pl.when(i == 0)
        def _():
            dgt_ref[...] = jnp.zeros_like(dgt_ref)
            dg_ref[...] = jnp.zeros_like(dg_ref)

        dgt_ref[...] += jnp.sum(dxv * (uh * gv), axis=0, keepdims=True)
        dg_ref[...] += jnp.sum(dn * uh, axis=0, keepdims=True)

    row = pl.BlockSpec((tm, d), lambda i: (i, 0))
    vec = pl.BlockSpec((1, d), lambda i: (0, 0))
    vshape = jax.ShapeDtypeStruct((1, d), F32)
    return pl.pallas_call(body, name=name, grid=(s // tm,), in_specs=[row, row, vec, vec],
                          out_specs=[row, vec, vec],
                          out_shape=[jax.ShapeDtypeStruct((s, d), BF16), vshape, vshape],
                          compiler_params=_params("arbitrary"))(dx, u, g, gate)


def _shift_rows(cur, prev8, k):
    rolled = pltpu.roll(cur, k, axis=0)
    rolled_prev = pltpu.roll(prev8, k, axis=0)
    i8 = lax.broadcasted_iota(jnp.int32, prev8.shape, 0)
    top = jnp.where(i8 < k, rolled_prev, rolled[:SUBLANES])
    return jnp.concatenate([top, rolled[SUBLANES:]], axis=0)


def _conv_pre(g, prev8, wc_ref, bc_ref):
    s1 = _shift_rows(g, prev8, 1)
    s2 = _shift_rows(g, prev8, 2)
    gc = bc_ref[...] + wc_ref[0:1, :] * s2 + wc_ref[1:2, :] * s1 + wc_ref[2:3, :] * g
    return gc, s1, s2


def _conv_act_fwd(gp, up, wc, bc, name):
    s, f = gp.shape
    tm, tf = _pick(s, 512, SUBLANES), _pick(f, 768, LANES)
    rh = tm // HALO

    def body(g_ref, gprev_ref, up_ref, wc_ref, bc_ref, a_ref):
        i = pl.program_id(1)
        prev = jnp.where(i == 0, 0.0, gprev_ref[...].astype(F32)[HALO - SUBLANES:])
        gc, _, _ = _conv_pre(g_ref[...].astype(F32), prev, wc_ref, bc_ref)
        a_ref[...] = ((gc * jax.nn.sigmoid(gc)) * up_ref[...].astype(F32)).astype(BF16)

    tile = pl.BlockSpec((tm, tf), lambda j, i: (i, j))
    prev = pl.BlockSpec((HALO, tf), lambda j, i: (jnp.maximum(i * rh - 1, 0), j))
    return pl.pallas_call(body, name=name, grid=(f // tf, s // tm),
                          in_specs=[tile, prev, tile, pl.BlockSpec((3, tf), lambda j, i: (0, j)),
                                    pl.BlockSpec((1, tf), lambda j, i: (0, j))],
                          out_specs=tile, out_shape=jax.ShapeDtypeStruct((s, f), BF16),
                          compiler_params=_params("parallel", "parallel"))(gp, gp, up, wc, bc)


def _conv_act_bwd(da, gp, up, wc, bc, name):
    s, f = gp.shape
    tm, tf = _pick(s, 512, SUBLANES), _pick(f, 768, LANES)
    rh = tm // HALO
    nrow = s // tm

    def body(da_ref, dan_ref, g_ref, gprev_ref, gn_ref, up_ref, upn_ref, wc_ref, bc_ref,
             dgp_ref, dup_ref, dwc_ref, dbc_ref):
        i = pl.program_id(1)
        last = i == nrow - 1
        head = lambda ref: ref[...].astype(F32)[:SUBLANES]
        prev = jnp.where(i == 0, 0.0, gprev_ref[...].astype(F32)[HALO - SUBLANES:])
        g_ext = jnp.concatenate([g_ref[...].astype(F32), head(gn_ref)], axis=0)
        up_ext = jnp.concatenate([up_ref[...].astype(F32), head(upn_ref)], axis=0)
        da_ext = jnp.concatenate([da_ref[...].astype(F32), jnp.where(last, 0.0, head(dan_ref))], axis=0)
        gc, s1, s2 = _conv_pre(g_ext, prev, wc_ref, bc_ref)
        sg = jax.nn.sigmoid(gc)
        dup_ref[...] = (da_ext * (gc * sg))[:tm].astype(BF16)
        dgc = da_ext * up_ext * (sg * (1.0 + gc * (1.0 - sg)))
        ext = tm + SUBLANES
        dgp = (wc_ref[2:3, :] * dgc + wc_ref[1:2, :] * pltpu.roll(dgc, ext - 1, axis=0)
               + wc_ref[0:1, :] * pltpu.roll(dgc, ext - 2, axis=0))
        dgp_ref[...] = dgp[:tm].astype(BF16)

        @pl.when(i == 0)
        def _():
            dwc_ref[...] = jnp.zeros_like(dwc_ref)
            dbc_ref[...] = jnp.zeros_like(dbc_ref)

        d0 = dgc[:tm]
        dwc_ref[0:1, :] += jnp.sum(d0 * s2[:tm], axis=0, keepdims=True)
        dwc_ref[1:2, :] += jnp.sum(d0 * s1[:tm], axis=0, keepdims=True)
        dwc_ref[2:3, :] += jnp.sum(d0 * g_ext[:tm], axis=0, keepdims=True)
        dbc_ref[...] += jnp.sum(d0, axis=0, keepdims=True)

    tile = pl.BlockSpec((tm, tf), lambda j, i: (i, j))
    prev = pl.BlockSpec((HALO, tf), lambda j, i: (jnp.maximum(i * rh - 1, 0), j))
    nxt = pl.BlockSpec((HALO, tf), lambda j, i: (jnp.minimum((i + 1) * rh, s // HALO - 1), j))
    return pl.pallas_call(body, name=name, grid=(f // tf, nrow),
                          in_specs=[tile, nxt, tile, prev, nxt, tile, nxt,
                                    pl.BlockSpec((3, tf), lambda j, i: (0, j)), pl.BlockSpec((1, tf), lambda j, i: (0, j))],
                          out_specs=[tile, tile, pl.BlockSpec((3, tf), lambda j, i: (0, j)),
                                     pl.BlockSpec((1, tf), lambda j, i: (0, j))],
                          out_shape=[jax.ShapeDtypeStruct((s, f), BF16), jax.ShapeDtypeStruct((s, f), BF16),
                                     jax.ShapeDtypeStruct((3, f), F32), jax.ShapeDtypeStruct((1, f), F32)],
                          compiler_params=_params("parallel", "arbitrary"))(da, da, gp, gp, gp, up, up, wc, bc)


def _split_bf16(v, parts):
    out, rem = [], v
    for _ in range(parts):
        t = rem.astype(BF16)
        out.append(t)
        rem = rem - t.astype(F32)
    return jnp.concatenate(out, axis=1)


def _tri(t, cmp, reps):
    row = lax.broadcasted_iota(jnp.int32, (t, t), 0)
    col = lax.broadcasted_iota(jnp.int32, (t, t), 1)
    m = cmp(row, col).astype(BF16)
    return jnp.concatenate([m] * reps, axis=0)


def _log_sigmoid(z):
    mn = jnp.minimum(z, 0.0)
    return mn - jnp.log(1.0 + jnp.exp(mn + (mn - z)))


def _log_one_minus_sigmoid(z):
    mn = jnp.minimum(z, 0.0)
    neg = mn - z
    return neg - jnp.log(1.0 + jnp.exp(mn + neg))


def _pair(xv):
    lane = lax.broadcasted_iota(jnp.int32, xv.shape, 1)
    zero = jnp.zeros_like(xv)
    return jnp.where(lane < HEAD_DIM, xv, zero), jnp.where(lane < HEAD_DIM, zero, xv)


def _attn_specs(s, tq):
    q_spec = pl.BlockSpec((None, tq, LANES), lambda hp, qi: (0, qi, hp))
    k_spec = pl.BlockSpec((None, s, LANES), lambda hp, qi: (1, 0, hp))
    v_spec = pl.BlockSpec((None, s, LANES), lambda hp, qi: (2, 0, hp))
    return q_spec, k_spec, v_spec


def _rel_mask(tq, tk, i, strict):
    row = lax.broadcasted_iota(jnp.int32, (tq, tk), 0)
    col = lax.broadcasted_iota(jnp.int32, (tq, tk), 1) + i * tk
    return col < row if strict else col <= row


def _key_bound(qkv, name):
    _, s, d = qkv.shape

    def body(k_ref, o_ref):
        sq = jnp.square(k_ref[...].astype(F32))
        low = lax.broadcasted_iota(jnp.int32, sq.shape, 1) < HEAD_DIM
        n0 = jnp.max(jnp.sum(jnp.where(low, sq, 0.0), axis=1, keepdims=True))
        n1 = jnp.max(jnp.sum(jnp.where(low, 0.0, sq), axis=1, keepdims=True))
        low8 = lax.broadcasted_iota(jnp.int32, (SUBLANES, LANES), 1) < HEAD_DIM
        o_ref[...] = jnp.sqrt(jnp.where(low8, n0, n1))

    return pl.pallas_call(body, name=name, grid=(d // LANES,),
                          in_specs=[pl.BlockSpec((None, s, LANES), lambda hp: (1, 0, hp))],
                          out_specs=pl.BlockSpec((None, SUBLANES, LANES), lambda hp: (hp, 0, 0)),
                          out_shape=jax.ShapeDtypeStruct((d // LANES, SUBLANES, LANES), F32),
                          compiler_params=_params("parallel"))(qkv)


def _sb_fwd(qkv, kbound, tq, tk, name, rider=None):
    _, s, d = qkv.shape
    hp_n, nq, nd = d // LANES, s // tq, tq // tk
    scale = HEAD_DIM ** -0.5

    def body(q_ref, k_ref, v_ref, kb_ref, o_ref, lt_ref, nb_ref):
        qi = pl.program_id(1)
        lane = lax.broadcasted_iota(jnp.int32, (tq, LANES), 1)
        onward = _tri(tk, lambda j, sidx: j >= sidx, SUM_TERMS)
        qpair = _pair(q_ref[...] * scale)
        kbv = kb_ref[...]
        z_max = [jnp.sqrt(jnp.sum(jnp.square(qh.astype(F32)), axis=1, keepdims=True)) * (1.01 * km)
                 for qh, km in zip(qpair, (kbv[0:1, 0:1], kbv[0:1, HEAD_DIM:HEAD_DIM + 1]))]

        def alive(runs):
            return jnp.logical_or(jnp.max(runs[0] + z_max[0]) >= EXP_UNDERFLOW,
                                  jnp.max(runs[1] + z_max[1]) >= EXP_UNDERFLOW)

        def block(kb, carry, mask):
            acc, runs = carry[0], carry[1:]
            ks = pl.multiple_of(kb * tk, tk)
            kblk = k_ref[pl.ds(ks, tk), :]
            vpair = _pair(v_ref[pl.ds(ks, tk), :])
            probs, new_runs = [], []
            for qh, run in zip(qpair, runs):
                z = lax.dot_general(qh, kblk, NT, preferred_element_type=F32)
                l1 = _log_one_minus_sigmoid(z)
                if mask is not None:
                    l1 = jnp.where(mask, l1, 0.0)
                c = jnp.dot(_split_bf16(l1, SUM_TERMS), onward, preferred_element_type=F32)
                a = jnp.exp(z + c + run)
                if mask is not None:
                    a = jnp.where(mask, a, 0.0)
                probs.append(a.astype(BF16))
                new_runs.append(run + jnp.sum(l1, axis=1, keepdims=True))
            acc = acc + jnp.dot(jnp.concatenate(probs, axis=1), jnp.concatenate(vpair, axis=0),
                                preferred_element_type=F32)
            return (acc, *new_runs)

        zero1 = jnp.zeros((tq, 1), F32)
        carry = (jnp.zeros((tq, LANES), F32), zero1, zero1)
        for i in reversed(range(nd)):
            carry = block(qi * nd + i, carry, _rel_mask(tq, tk, i, True))

        def step(state):
            new = block(qi * nd - 1 - state[0], state[2:], None)
            return (state[0] + 1, alive(new[1:]), *new)

        done, _, acc, run0, run1 = lax.while_loop(lambda st: jnp.logical_and(st[0] < qi * nd, st[1]), step,
                                                  (jnp.int32(0), alive(carry[1:]), *carry))
        o_ref[...] = acc.astype(BF16)
        lt_ref[...] = jnp.where(lane < HEAD_DIM, run0, run1)
        nb_ref[...] = jnp.zeros((SUBLANES, LANES), F32) + done.astype(F32)

    q_spec, k_spec, v_spec = _attn_specs(s, tq)
    return _attn_call(
        body, name, (hp_n, nq),
        [q_spec, k_spec, v_spec, pl.BlockSpec((None, SUBLANES, LANES), lambda hp, qi: (hp, 0, 0))],
        [pl.BlockSpec((tq, LANES), lambda hp, qi: (qi, hp)),
         pl.BlockSpec((None, tq, LANES), lambda hp, qi: (hp, qi, 0)),
         pl.BlockSpec((None, None, SUBLANES, LANES), lambda hp, qi: (hp, qi, 0, 0))],
        [jax.ShapeDtypeStruct((s, d), BF16), jax.ShapeDtypeStruct((hp_n, s, LANES), F32),
         jax.ShapeDtypeStruct((hp_n, nq, SUBLANES, LANES), F32)],
        [], ("parallel", "parallel"), (qkv, qkv, qkv, kbound), rider)


def _sb_bwd(qkv, ltot, visited, do, tq, tk, name, rider=None):
    _, s, d = qkv.shape
    hp_n, nq, nd = d // LANES, s // tq, tq // tk
    scale = HEAD_DIM ** -0.5

    def body(q_ref, k_ref, v_ref, lt_ref, nb_ref, do_ref, dqkv_ref, dk_acc, dv_acc):
        qi = pl.program_id(1)
        first = qi * nd - jnp.max(nb_ref[...]).astype(jnp.int32)

        @pl.when(qi == 0)
        def _():
            dk_acc[...] = jnp.zeros_like(dk_acc)
            dv_acc[...] = jnp.zeros_like(dv_acc)

        onward = _tri(tk, lambda j, sidx: j >= sidx, SUM_TERMS)
        before = _tri(tk, lambda j, sidx: j < sidx, SUM_TERMS)
        qpair = _pair(q_ref[...] * scale)
        dopair = _pair(do_ref[...])
        ltv = lt_ref[...]
        lts = (ltv[:, 0:1], ltv[:, HEAD_DIM:HEAD_DIM + 1])

        def block(kb, carry, mask):
            dq, lpres, gpres = carry[0], carry[1:3], carry[3:5]
            ks = pl.multiple_of(kb * tk, tk)
            kblk, vblk = k_ref[pl.ds(ks, tk), :], v_ref[pl.ds(ks, tk), :]
            kpair = _pair(kblk * scale)
            dzs, probs, new_l, new_g = [], [], [], []
            for qh, doh, lt, lpre, gpre in zip(qpair, dopair, lts, lpres, gpres):
                z = lax.dot_general(qh, kblk, NT, preferred_element_type=F32)
                l1 = _log_one_minus_sigmoid(z)
                if mask is not None:
                    l1 = jnp.where(mask, l1, 0.0)
                lpre = lpre + jnp.sum(l1, axis=1, keepdims=True)
                c = jnp.dot(_split_bf16(l1, SUM_TERMS), onward, preferred_element_type=F32)
                a = jnp.exp(z + c + (lt - lpre))
                if mask is not None:
                    a = jnp.where(mask, a, 0.0)
                da = lax.dot_general(doh, vblk, NT, preferred_element_type=F32)
                g = da * a
                p = jnp.dot(_split_bf16(g, SUM_TERMS), before, preferred_element_type=F32) + gpre
                dz = g - jnp.exp(z + l1) * (g + p)
                if mask is not None:
                    dz = jnp.where(mask, dz, 0.0)
                dzs.append(dz.astype(BF16))
                probs.append(a.astype(BF16))
                new_l.append(lpre)
                new_g.append(gpre + jnp.sum(g, axis=1, keepdims=True))
            dq = dq + jnp.dot(jnp.concatenate(dzs, axis=1), jnp.concatenate(kpair, axis=0),
                              preferred_element_type=F32)
            dk_acc[pl.ds(ks, tk), :] += lax.dot_general(jnp.concatenate(dzs, axis=0), jnp.concatenate(qpair, axis=0),
                                                        TN, preferred_element_type=F32)
            dv_acc[pl.ds(ks, tk), :] += lax.dot_general(jnp.concatenate(probs, axis=0), jnp.concatenate(dopair, axis=0),
                                                        TN, preferred_element_type=F32)
            return (dq, *new_l, *new_g)

        zero1 = jnp.zeros((tq, 1), F32)
        init = (jnp.zeros((tq, LANES), F32), zero1, zero1, zero1, zero1)
        carry = lax.fori_loop(first, qi * nd, lambda j, cr: block(j, cr, None), init)
        for i in range(nd):
            carry = block(qi * nd + i, carry, _rel_mask(tq, tk, i, True))
        dqkv_ref[0, pl.ds(pl.multiple_of(qi * tq, tq), tq), :] = carry[0].astype(BF16)

        @pl.when(qi == nq - 1)
        def _():
            dqkv_ref[1] = dk_acc[...].astype(BF16)
            dqkv_ref[2] = dv_acc[...].astype(BF16)

    q_spec, k_spec, v_spec = _attn_specs(s, tq)
    return _attn_call(
        body, name, (hp_n, nq),
        [q_spec, k_spec, v_spec, pl.BlockSpec((None, tq, LANES), lambda hp, qi: (hp, qi, 0)),
         pl.BlockSpec((None, None, SUBLANES, LANES), lambda hp, qi: (hp, qi, 0, 0)),
         pl.BlockSpec((tq, LANES), lambda hp, qi: (qi, hp))],
        [pl.BlockSpec((3, s, LANES), lambda hp, qi: (0, 0, hp))], [jax.ShapeDtypeStruct((3, s, d), BF16)],
        [pltpu.VMEM((s, LANES), F32), pltpu.VMEM((s, LANES), F32)],
        ("parallel", "arbitrary"), (qkv, qkv, qkv, ltot, visited, do), rider)


def _fox_prep(fl, bias, name):
    s, w = fl.shape
    tb = _pick(s, 512, SUBLANES)

    def body(fl_ref, b_ref, cum_ref, carry_ref):
        i = pl.program_id(0)

        @pl.when(i == 0)
        def _():
            carry_ref[...] = jnp.zeros_like(carry_ref)

        logf = _log_sigmoid(fl_ref[...] + b_ref[...])
        row = lax.broadcasted_iota(jnp.int32, (tb, tb), 0)
        col = lax.broadcasted_iota(jnp.int32, (tb, tb), 1)
        incl = (col <= row).astype(BF16)
        tot = carry_ref[...]
        rem = logf
        for _ in range(3):
            part = rem.astype(BF16)
            tot = tot + jnp.dot(incl, part, preferred_element_type=F32)
            rem = rem - part.astype(F32)
        cum_ref[...] = tot
        carry_ref[...] = tot[tb - 1:tb, :]

    blk = pl.BlockSpec((tb, w), lambda i: (i, 0))
    return pl.pallas_call(body, name=name, grid=(s // tb,), in_specs=[blk, pl.BlockSpec((1, w), lambda i: (0, 0))],
                          out_specs=blk, out_shape=jax.ShapeDtypeStruct((s, w), F32),
                          scratch_shapes=[pltpu.VMEM((1, w), F32)], compiler_params=_params("arbitrary"))(fl, bias)


def _fox_gate_bwd(dcum, fl, bias, n_heads, name):
    s, w = fl.shape
    tb = _pick(s, 512, SUBLANES)
    nb = s // tb

    def body(dc_ref, fl_ref, b_ref, dfl_ref, db_ref, carry_ref):
        i = pl.program_id(0)

        @pl.when(i == 0)
        def _():
            carry_ref[...] = jnp.zeros_like(carry_ref)
            db_ref[...] = jnp.zeros_like(db_ref)

        row = lax.broadcasted_iota(jnp.int32, (tb, tb), 0)
        col = lax.broadcasted_iota(jnp.int32, (tb, tb), 1)
        incl = (col >= row).astype(BF16)
        tot = jnp.broadcast_to(carry_ref[...], (tb, w))
        rem = dc_ref[...]
        for _ in range(3):
            part = rem.astype(BF16)
            tot = tot + jnp.dot(incl, part, preferred_element_type=F32)
            rem = rem - part.astype(F32)
        carry_ref[...] = tot[0:1, :]
        xg = fl_ref[...] + b_ref[...]
        e = jnp.exp(-jnp.abs(xg))
        sig_neg = jnp.where(xg >= 0.0, e, 1.0) / (1.0 + e)
        lane = lax.broadcasted_iota(jnp.int32, (tb, w), 1)
        dfl = jnp.where(lane < n_heads, tot * sig_neg, 0.0)
        dfl_ref[...] = dfl.astype(BF16)
        db_ref[...] += jnp.sum(dfl, axis=0, keepdims=True)

    blk = pl.BlockSpec((tb, w), lambda i: (nb - 1 - i, 0))
    vec = pl.BlockSpec((1, w), lambda i: (0, 0))
    return pl.pallas_call(body, name=name, grid=(nb,), in_specs=[blk, blk, vec], out_specs=[blk, vec],
                          out_shape=[jax.ShapeDtypeStruct((s, w), BF16), jax.ShapeDtypeStruct((1, w), F32)],
                          scratch_shapes=[pltpu.VMEM((1, w), F32)], compiler_params=_params("arbitrary"))(dcum, fl, bias)


def _head_columns(cum_blk, hp):
    lane = lax.broadcasted_iota(jnp.int32, cum_blk.shape, 1)
    return tuple(jnp.sum(jnp.where(lane == 2 * hp + h, cum_blk, 0.0), axis=1, keepdims=True) for h in range(2))


def _fox_fwd(qkv, kbound, cum, ck, tq, tk, name, rider=None):
    _, s, d = qkv.shape
    hp_n, nq, nd = d // LANES, s // tq, tq // tk
    scale = HEAD_DIM ** -0.5

    def body(q_ref, k_ref, v_ref, kb_ref, cq_ref, ck_ref, o_ref, lse_ref, nb_ref):
        qi = pl.program_id(1)
        low = lax.broadcasted_iota(jnp.int32, (tq, LANES), 1) < HEAD_DIM
        qpair = _pair(q_ref[...] * scale)
        cqs = _head_columns(cq_ref[...], pl.program_id(0))
        kbv = kb_ref[...]
        reach = [jnp.sqrt(jnp.sum(jnp.square(qh.astype(F32)), axis=1, keepdims=True)) * (1.01 * km) + cqh
                 for qh, km, cqh in zip(qpair, (kbv[0:1, 0:1], kbv[0:1, HEAD_DIM:HEAD_DIM + 1]), cqs)]

        def alive(stats, kb):
            ks = pl.multiple_of(kb * tk, tk)
            return functools.reduce(jnp.logical_or, [
                jnp.max(reach[h] - ck_ref[h:h + 1, pl.ds(ks, tk)][:, 0:1] - stats[2 * h]) >= EXP_UNDERFLOW
                for h in range(2)])

        def block(kb, carry, mask):
            acc, stats = carry[0], carry[1:]
            ks = pl.multiple_of(kb * tk, tk)
            kblk = k_ref[pl.ds(ks, tk), :]
            vpair = _pair(v_ref[pl.ds(ks, tk), :])
            probs, alphas, new = [], [], []
            for h, (qh, cqh) in enumerate(zip(qpair, cqs)):
                m, lsum = stats[2 * h], stats[2 * h + 1]
                z = lax.dot_general(qh, kblk, NT, preferred_element_type=F32)
                sc = z + (cqh - ck_ref[h:h + 1, pl.ds(ks, tk)])
                if mask is not None:
                    sc = jnp.where(mask, sc, -jnp.inf)
                m_new = jnp.maximum(m, jnp.max(sc, axis=1, keepdims=True))
                alpha = jnp.exp(m - m_new)
                p = jnp.exp(sc - m_new)
                new += [m_new, alpha * lsum + jnp.sum(p, axis=1, keepdims=True)]
                probs.append(p.astype(BF16))
                alphas.append(alpha)
            acc = jnp.where(low, alphas[0], alphas[1]) * acc + jnp.dot(
                jnp.concatenate(probs, axis=1), jnp.concatenate(vpair, axis=0), preferred_element_type=F32)
            return (acc, *new)

        neg = jnp.full((tq, 1), -jnp.inf, F32)
        zero1 = jnp.zeros((tq, 1), F32)
        carry = (jnp.zeros((tq, LANES), F32), neg, zero1, neg, zero1)
        for i in range(nd):
            carry = block(qi * nd + i, carry, _rel_mask(tq, tk, i, False))

        def step(state):
            kb = qi * nd - 1 - state[0]
            new = block(kb, state[2:], None)
            return (state[0] + 1, alive(new[1:], kb), *new)

        done, _, acc, m0, l0, m1, l1 = lax.while_loop(lambda st: jnp.logical_and(st[0] < qi * nd, st[1]), step,
                                                      (jnp.int32(0), alive(carry[1:], qi * nd), *carry))
        o_ref[...] = (acc / jnp.where(low, l0, l1)).astype(BF16)
        lse_ref[...] = jnp.where(low, m0 + jnp.log(l0), m1 + jnp.log(l1))
        nb_ref[...] = jnp.zeros((SUBLANES, LANES), F32) + done.astype(F32)

    q_spec, k_spec, v_spec = _attn_specs(s, tq)
    pair_rows = pl.BlockSpec((None, tq, LANES), lambda hp, qi: (hp, qi, 0))
    return _attn_call(
        body, name, (hp_n, nq),
        [q_spec, k_spec, v_spec, pl.BlockSpec((None, SUBLANES, LANES), lambda hp, qi: (hp, 0, 0)),
         pl.BlockSpec((tq, LANES), lambda hp, qi: (qi, 0)), pl.BlockSpec((None, 2, s), lambda hp, qi: (hp, 0, 0))],
        [pl.BlockSpec((tq, LANES), lambda hp, qi: (qi, hp)), pair_rows,
         pl.BlockSpec((None, None, SUBLANES, LANES), lambda hp, qi: (hp, qi, 0, 0))],
        [jax.ShapeDtypeStruct((s, d), BF16), jax.ShapeDtypeStruct((hp_n, s, LANES), F32),
         jax.ShapeDtypeStruct((hp_n, nq, SUBLANES, LANES), F32)],
        [], ("parallel", "parallel"), (qkv, qkv, qkv, kbound, cum, ck), rider)


def _fox_bwd(qkv, o, do, lse, visited, cum, ck, tq, tk, name, rider=None):
    _, s, d = qkv.shape
    hp_n, nq, nd = d // LANES, s // tq, tq // tk
    scale = HEAD_DIM ** -0.5

    def body(q_ref, k_ref, v_ref, o_ref, do_ref, lse_ref, nb_ref, cq_ref, ck_ref, dqkv_ref, dcq_ref, dck_ref,
             dk_acc, dv_acc):
        qi = pl.program_id(1)
        first = qi * nd - jnp.max(nb_ref[...]).astype(jnp.int32)

        @pl.when(qi == 0)
        def _():
            dk_acc[...] = jnp.zeros_like(dk_acc)
            dv_acc[...] = jnp.zeros_like(dv_acc)
            dck_ref[...] = jnp.zeros_like(dck_ref)

        low = lax.broadcasted_iota(jnp.int32, (tq, LANES), 1) < HEAD_DIM
        dov, lsev = do_ref[...], lse_ref[...]
        qpair = _pair(q_ref[...] * scale)
        dopair = _pair(dov)
        prod = dov.astype(F32) * o_ref[...].astype(F32)
        deltas = (jnp.sum(jnp.where(low, prod, 0.0), axis=1, keepdims=True),
                  jnp.sum(jnp.where(low, 0.0, prod), axis=1, keepdims=True))
        cqs = _head_columns(cq_ref[...], pl.program_id(0))
        lses = (lsev[:, 0:1], lsev[:, HEAD_DIM:HEAD_DIM + 1])

        def block(kb, carry, mask):
            dq, rowsums = carry[0], carry[1:]
            ks = pl.multiple_of(kb * tk, tk)
            kblk, vblk = k_ref[pl.ds(ks, tk), :], v_ref[pl.ds(ks, tk), :]
            kpair = _pair(kblk * scale)
            dss, probs, new_rows = [], [], []
            for h, (qh, doh) in enumerate(zip(qpair, dopair)):
                z = lax.dot_general(qh, kblk, NT, preferred_element_type=F32)
                sc = z + (cqs[h] - ck_ref[h:h + 1, pl.ds(ks, tk)])
                p = jnp.exp(sc - lses[h])
                if mask is not None:
                    p = jnp.where(mask, p, 0.0)
                dp = lax.dot_general(doh, vblk, NT, preferred_element_type=F32)
                ds = p * (dp - deltas[h])
                dck_ref[h:h + 1, pl.ds(ks, tk)] -= jnp.sum(ds, axis=0, keepdims=True)
                new_rows.append(rowsums[h] + jnp.sum(ds, axis=1, keepdims=True))
                dss.append(ds.astype(BF16))
                probs.append(p.astype(BF16))
            dq = dq + jnp.dot(jnp.concatenate(dss, axis=1), jnp.concatenate(kpair, axis=0),
                              preferred_element_type=F32)
            dk_acc[pl.ds(ks, tk), :] += lax.dot_general(jnp.concatenate(dss, axis=0), jnp.concatenate(qpair, axis=0),
                                                       TN, preferred_element_type=F32)
            dv_acc[pl.ds(ks, tk), :] += lax.dot_general(jnp.concatenate(probs, axis=0), jnp.concatenate(dopair, axis=0),
                                                       TN, preferred_element_type=F32)
            return (dq, *new_rows)

        zero1 = jnp.zeros((tq, 1), F32)
        carry = lax.fori_loop(first, qi * nd, lambda j, cr: block(j, cr, None),
                              (jnp.zeros((tq, LANES), F32), zero1, zero1))
        for i in range(nd):
            carry = block(qi * nd + i, carry, _rel_mask(tq, tk, i, False))
        dq, rs0, rs1 = carry
        dqkv_ref[0, pl.ds(pl.multiple_of(qi * tq, tq), tq), :] = dq.astype(BF16)
        dcq_ref[...] = jnp.where(low, rs0, rs1)

        @pl.when(qi == nq - 1)
        def _():
            dqkv_ref[1] = dk_acc[...].astype(BF16)
            dqkv_ref[2] = dv_acc[...].astype(BF16)

    q_spec, k_spec, v_spec = _attn_specs(s, tq)
    pair_rows = pl.BlockSpec((None, tq, LANES), lambda hp, qi: (hp, qi, 0))
    tile = pl.BlockSpec((tq, LANES), lambda hp, qi: (qi, hp))
    keys = pl.BlockSpec((None, 2, s), lambda hp, qi: (hp, 0, 0))
    return _attn_call(
        body, name, (hp_n, nq),
        [q_spec, k_spec, v_spec, tile, tile, pair_rows,
         pl.BlockSpec((None, None, SUBLANES, LANES), lambda hp, qi: (hp, qi, 0, 0)),
         pl.BlockSpec((tq, LANES), lambda hp, qi: (qi, 0)), keys],
        [pl.BlockSpec((3, s, LANES), lambda hp, qi: (0, 0, hp)), pair_rows, keys],
        [jax.ShapeDtypeStruct((3, s, d), BF16), jax.ShapeDtypeStruct((hp_n, s, LANES), F32),
         jax.ShapeDtypeStruct((hp_n, 2, s), F32)],
        [pltpu.VMEM((s, LANES), F32), pltpu.VMEM((s, LANES), F32)],
        ("parallel", "arbitrary"), (qkv, qkv, qkv, o, do, lse, visited, cum, ck), rider)


def _loss_head(y, target, name):
    s, d = y.shape
    tm = _pick(s, 512, SUBLANES)

    def body(y_ref, t_ref, dy_ref, sq_ref):
        i = pl.program_id(0)
        diff = y_ref[...] - t_ref[...]
        dy_ref[...] = diff / d

        @pl.when(i == 0)
        def _():
            sq_ref[...] = jnp.zeros_like(sq_ref)

        sq_ref[...] += jnp.sum(diff * diff, axis=0, keepdims=True)

    row = pl.BlockSpec((tm, d), lambda i: (i, 0))
    vec = pl.BlockSpec((1, d), lambda i: (0, 0))
    return pl.pallas_call(body, name=name, grid=(s // tm,), in_specs=[row, row], out_specs=[row, vec],
                          out_shape=[jax.ShapeDtypeStruct((s, d), F32), jax.ShapeDtypeStruct((1, d), F32)],
                          compiler_params=_params("arbitrary"))(y, target)


def _mod_fwd(c_all, w_mod, b_mod_cols, name):
    nl, d, cols = w_mod.shape
    nb = c_all.shape[0]

    def body(c_ref, w_ref, b_ref, o_ref):
        cv = c_ref[...]
        act = (cv * jax.nn.sigmoid(cv)).astype(BF16)
        o_ref[...] = jnp.dot(act, w_ref[...].astype(BF16), preferred_element_type=F32) + b_ref[...]

    return pl.pallas_call(
        body, name=name, grid=(nl,),
        in_specs=[pl.BlockSpec((nb, d), lambda l: (0, 0)), pl.BlockSpec((None, d, cols), lambda l: (l, 0, 0)),
                  pl.BlockSpec((None, 1, cols), lambda l: (l, 0, 0))],
        out_specs=pl.BlockSpec((None, nb, cols), lambda l: (l, 0, 0)),
        out_shape=jax.ShapeDtypeStruct((nl, nb, cols), F32), compiler_params=_params("parallel"))(c_all, w_mod, b_mod_cols)


def _wmod_grad(c_all_t, dmod, name):
    d, nb = c_all_t.shape
    _, nl, cols = dmod.shape

    def body(c_ref, dm_ref, o_ref):
        cv = c_ref[...]
        act = cv * jax.nn.sigmoid(cv)
        tot = act[:, 0:1] * dm_ref[0]
        for b in range(1, nb):
            tot = tot + act[:, b:b + 1] * dm_ref[b]
        o_ref[...] = tot

    return pl.pallas_call(
        body, name=name, grid=(nl,),
        in_specs=[pl.BlockSpec((d, nb), lambda l: (0, 0)), pl.BlockSpec((nb, None, 1, cols), lambda l: (0, l, 0, 0))],
        out_specs=pl.BlockSpec((None, d, cols), lambda l: (l, 0, 0)),
        out_shape=jax.ShapeDtypeStruct((nl, d, cols), F32), compiler_params=_params("parallel"))(
            c_all_t, dmod.reshape(nb, nl, 1, cols))


def _adamw(recv, w, m, v, name, *, tr=256, tc=512):
    nq, nl, rp, cp = recv.shape
    _, r, c = w.shape
    br = _pick(r, tr, SUBLANES) if rp == r else r
    bc = _pick(c, tc, LANES) if cp == c else c
    rbr = br if rp == r else rp
    rbc = bc if cp == c else cp

    def body(rv_ref, w_ref, m_ref, v_ref, g_ref, d_ref, nm_ref, nv_ref):
        g = rv_ref[0, :br, :bc].astype(F32)
        for qd in range(1, nq):
            g = g + rv_ref[qd, :br, :bc].astype(F32)
        m_new = ADAM_B1 * m_ref[...] + (1.0 - ADAM_B1) * g
        v_new = ADAM_B2 * v_ref[...] + (1.0 - ADAM_B2) * jnp.square(g)
        m_hat = m_new / (1.0 - ADAM_B1 ** ADAM_STEP)
        v_hat = v_new / (1.0 - ADAM_B2 ** ADAM_STEP)
        g_ref[...] = g
        d_ref[...] = -ADAM_LR * (m_hat / (jnp.sqrt(v_hat) + ADAM_EPS) + ADAM_WD * w_ref[...])
        nm_ref[...] = m_new
        nv_ref[...] = v_new

    blk = pl.BlockSpec((None, br, bc), lambda l, i, j: (l, i, j))
    rblk = pl.BlockSpec((nq, None, rbr, rbc), lambda l, i, j: (0, l, i, j))
    shape = jax.ShapeDtypeStruct(w.shape, F32)
    return pl.pallas_call(body, name=name, grid=(nl, r // br, c // bc), in_specs=[rblk, blk, blk, blk],
                          out_specs=[blk] * 4, out_shape=[shape] * 4,
                          compiler_params=_params("parallel", "parallel", "parallel"))(recv, w, m, v)


def _mesh_pos():
    return lax.axis_index("x"), lax.axis_index("y"), lax.axis_index("c")


def _allgather_small(block, name):
    m_per, n = block.shape

    def body(x_ref, out_ref, send_sems, recv_sems, local_sem):
        x, y, c = _mesh_pos()
        me, sibling = (x, y, c), (x, y, 1 - c)
        chips = [(1 - x, y), (x, 1 - y), (1 - x, 1 - y)]

        def rows(px, py, pc):
            return out_ref.at[pl.ds((4 * px + 2 * py + pc) * m_per, m_per), :]

        def copy(k, blk, to, src=None):
            return pltpu.make_async_remote_copy(
                src_ref=rows(*blk) if src is None else src, dst_ref=rows(*blk),
                send_sem=send_sems.at[k], recv_sem=recv_sems.at[k], device_id=to, device_id_type=MESH)

        mine = pltpu.make_async_copy(x_ref, rows(*me), local_sem)
        mine.start()
        first = [copy(0, me, sibling, src=x_ref)]
        first += [copy(1 + j, me, (*chip, c), src=x_ref) for j, chip in enumerate(chips)]
        for cp in first:
            cp.start()
        passed = [copy(4 + j, (*chip, c), sibling) for j, chip in enumerate(chips)]
        for j, chip in enumerate(chips):
            copy(1 + j, (*chip, c), me).wait_recv()
            passed[j].start()
        copy(0, sibling, me).wait_recv()
        for j, chip in enumerate(chips):
            copy(4 + j, (*chip, 1 - c), me).wait_recv()
        for cp in first + passed:
            cp.wait_send()
        mine.wait()

    return pl.pallas_call(
        body, name=name, out_shape=jax.ShapeDtypeStruct((N_DEV * m_per, n), block.dtype),
        in_specs=[pl.BlockSpec(memory_space=pltpu.VMEM)], out_specs=pl.BlockSpec(memory_space=pltpu.VMEM),
        scratch_shapes=[pltpu.SemaphoreType.DMA((7,)), pltpu.SemaphoreType.DMA((7,)), pltpu.SemaphoreType.DMA],
        compiler_params=pltpu.CompilerParams(vmem_limit_bytes=VMEM_LIMIT_BYTES))(block)


def _exchange(srcs, dsts, jobs, name):
    n_src, n_job, n_dst = len(srcs), len(jobs), len(dsts)

    def body(*refs):
        src_refs, dst_refs = refs[:n_src], refs[n_src + n_dst:n_src + 2 * n_dst]
        send_sems, recv_sems, local_sems = refs[n_src + 2 * n_dst:]
        x, y, c = _mesh_pos()
        me = 4 * x + 2 * y + c
        pending = []
        for t, (si, di, src_slice, dst_slice) in enumerate(jobs):
            src, dst = src_refs[si], dst_refs[di]
            lc = pltpu.make_async_copy(src_slice(src, me), dst_slice(dst, me), local_sems.at[t])
            lc.start()
            pending.append(lc)
            for dd in range(1, N_DEV):
                px = 1 - x if dd & 4 else x
                py = 1 - y if dd & 2 else y
                pc = 1 - c if dd & 1 else c
                cp = pltpu.make_async_remote_copy(
                    src_ref=src_slice(src, 4 * px + 2 * py + pc), dst_ref=dst_slice(dst, me),
                    send_sem=send_sems.at[t, dd - 1], recv_sem=recv_sems.at[t, dd - 1],
                    device_id=(px, py, pc), device_id_type=MESH)
                cp.start()
                pending.append(cp)
        for cp in pending:
            cp.wait()

    hbm = pl.BlockSpec(memory_space=pl.ANY)
    return pl.pallas_call(
        body, name=name, out_shape=[jax.ShapeDtypeStruct(a.shape, a.dtype) for a in dsts],
        in_specs=[hbm] * (n_src + n_dst), out_specs=[hbm] * n_dst,
        input_output_aliases={n_src + i: i for i in range(n_dst)},
        scratch_shapes=[pltpu.SemaphoreType.DMA((n_job, N_DEV - 1)), pltpu.SemaphoreType.DMA((n_job, N_DEV - 1)),
                        pltpu.SemaphoreType.DMA((n_job,))])(*srcs, *dsts)


def _attn_call(body, name, grid, in_specs, out_specs, out_shape, scratch_shapes, semantics, operands, rider):
    out_specs, out_shape, scratch_shapes = list(out_specs), list(out_shape), list(scratch_shapes)
    if rider is None:
        res = pl.pallas_call(body, name=name, grid=grid, in_specs=list(in_specs), out_specs=out_specs,
                             out_shape=out_shape, scratch_shapes=scratch_shapes,
                             compiler_params=_params(*semantics))(*operands)
        return list(res), None
    srcs, dsts, jobs = rider
    n_in, n_out, n_scr, n_src, n_dst = len(operands), len(out_shape), len(scratch_shapes), len(srcs), len(dsts)

    def carrying(*refs):
        src_refs = refs[n_in:n_in + n_src]
        at = n_in + n_src + n_dst
        dst_refs = refs[at + n_out:at + n_out + n_dst]
        scratch = refs[at + n_out + n_dst:at + n_out + n_dst + n_scr]
        send_sems, recv_sems = refs[-2], refs[-1]
        ids = [pl.program_id(a) for a in range(len(grid))]
        first = functools.reduce(jnp.logical_and, [i == 0 for i in ids])
        last = functools.reduce(jnp.logical_and, [i == g - 1 for i, g in zip(ids, grid)])

        @pl.when(first)
        def _():
            local, remote = _exchange_copies(jobs, src_refs, dst_refs, send_sems, recv_sems, True)
            for cp in remote + local:
                cp.start()

        body(*refs[:n_in], *refs[at:at + n_out], *scratch)

        @pl.when(last)
        def _():
            local, remote = _exchange_copies(jobs, src_refs, dst_refs, send_sems, recv_sems, False)
            for cp in local:
                cp.wait()
            for cp in remote:
                cp.wait_send()
                cp.wait_recv()

    hbm = pl.BlockSpec(memory_space=pl.ANY)
    sems = pltpu.SemaphoreType.DMA((len(jobs) * N_DEV,))
    res = pl.pallas_call(
        carrying, name=name, grid=grid, in_specs=list(in_specs) + [hbm] * (n_src + n_dst),
        out_specs=out_specs + [hbm] * n_dst,
        out_shape=out_shape + [jax.ShapeDtypeStruct(a.shape, a.dtype) for a in dsts],
        input_output_aliases={n_in + n_src + i: n_out + i for i in range(n_dst)},
        scratch_shapes=scratch_shapes + [sems, sems],
        compiler_params=_params(*["arbitrary"] * len(grid)))(*operands, *srcs, *dsts)
    return list(res[:n_out]), list(res[n_out:])


def _peer(x, y, c, dd):
    return (1 - x if dd & 4 else x, 1 - y if dd & 2 else y, 1 - c if dd & 1 else c)


def _exchange_copies(jobs, src_refs, dst_refs, send_sems, recv_sems, sending):
    x, y, c = _mesh_pos()
    me = 4 * x + 2 * y + c
    local, remote = [], []
    for t, (si, di, src_slice, dst_slice) in enumerate(jobs):
        local.append(pltpu.make_async_copy(src_slice(src_refs[si], me), dst_slice(dst_refs[di], me),
                                           send_sems.at[t * N_DEV]))
        for dd in range(1, N_DEV):
            px, py, pc = _peer(x, y, c, dd)
            p = 4 * px + 2 * py + pc
            remote.append(pltpu.make_async_remote_copy(
                src_ref=src_slice(src_refs[si], p), dst_ref=dst_slice(dst_refs[di], me if sending else p),
                send_sem=send_sems.at[t * N_DEV + dd], recv_sem=recv_sems.at[t * N_DEV + dd],
                device_id=(px, py, pc), device_id_type=MESH))
    return local, remote


def _whole(ref, p):
    return ref


def _layer_of(layer, inner):
    return lambda ref, p: inner(ref.at[layer], p)


def _cols_of(width):
    def take(ref, p):
        lead = (slice(None),) * (len(ref.shape) - 1)
        return ref.at[lead + (pl.ds(pl.multiple_of(p * width, LANES), width),)]
    return take


def _rows_of(height):
    def take(ref, p):
        lead = (slice(None),) * (len(ref.shape) - 2)
        return ref.at[lead + (pl.ds(pl.multiple_of(p * height, SUBLANES), height), slice(None))]
    return take


def _slot(layer=None):
    if layer is None:
        return lambda ref, p: ref.at[p]
    return lambda ref, p: ref.at[p, layer]


def _local_step(x0, target, mod, g_mix_pre, g_mix_post, g_ffn_pre, g_ffn_post, wqkv, wo, wg, wu, wd,
                wfg, bfg, wconv, bconv, tiles, comm=None):
    s, d = x0.shape
    nl = mod.shape[0]
    n_heads = d // HEAD_DIM
    hp_n = d // LANES
    assert tiles["sb_fwd"] == tiles["sb_bwd"] and tiles["fox_fwd"] == tiles["fox_bwd"]
    vec = lambda a: a.reshape(1, -1)
    saved = []
    xcur = x0
    for l in range(nl):
        if comm is not None:
            wqkv, wo, wg, wu, wd = comm["weights"]()
        sh_a, sc_a, gt_a, sh_f, sc_f, gt_f = (vec(mod[l, i * d:(i + 1) * d]) for i in range(6))
        fox = l % 2 == 1
        h1 = _norm_mod_fwd(xcur, vec(g_mix_pre[l]), sh_a, sc_a, f"norm_mix_fwd_{l}")
        qkv = _mm_nn(h1, wqkv, l, BF16, f"qkv_fwd_{l}", split_out=d)
        rider = None if comm is None else comm["fwd_rider"](l)
        if fox:
            j = l // 2
            fl = _mm_nn(h1, wfg, j, F32, f"fgate_fwd_{l}")
            cum = _fox_prep(fl, bfg[j], f"fox_prep_{l}")
            ck = cum[:, :n_heads].T.reshape(hp_n, 2, s)
            (o, *stat), moved = _fox_fwd(qkv, _key_bound(qkv, f"key_bound_{l}"), cum, ck, *tiles["fox_fwd"],
                                         f"fox_fwd_{l}", rider)
            extra = (fl, cum, ck)
        else:
            (o, *stat), moved = _sb_fwd(qkv, _key_bound(qkv, f"key_bound_{l}"), *tiles["sb_fwd"], f"sb_fwd_{l}", rider)
            extra = None
        if moved is not None:
            comm["done"](moved)
            wqkv, wo, wg, wu, wd = comm["weights"]()
        u = _mm_nn(o, wo, l, F32, f"attn_out_fwd_{l}")
        x2 = _post_fwd(xcur, u, vec(g_mix_post[l]), gt_a, f"post_mix_fwd_{l}")
        h2 = _norm_mod_fwd(x2, vec(g_ffn_pre[l]), sh_f, sc_f, f"norm_ffn_fwd_{l}")
        gp = _mm_nn(h2, wg, l, BF16, f"ffn_gate_fwd_{l}")
        up = _mm_nn(h2, wu, l, BF16, f"ffn_up_fwd_{l}")
        act = _conv_act_fwd(gp, up, wconv[l], bconv[l], f"conv_act_fwd_{l}")
        yv = _mm_nn(act, wd, l, F32, f"ffn_down_fwd_{l}")
        x3 = _post_fwd(x2, yv, vec(g_ffn_post[l]), gt_f, f"post_ffn_fwd_{l}")
        saved.append((xcur, h1, qkv, o, stat, extra, u, x2, h2, gp, up, act, yv))
        xcur = x3

    dx, sq = _loss_head(xcur, target, "loss_head")
    small, big = [None] * nl, [None] * nl
    for l in reversed(range(nl)):
        xin, h1, qkv, o, stat, extra, u, x2, h2, gp, up, act, yv = saved[l]
        sc_a, gt_a, sc_f, gt_f = (vec(mod[l, i * d:(i + 1) * d]) for i in (1, 2, 4, 5))
        fox = l % 2 == 1
        dy, dgt_f, dg4 = _post_bwd(dx, yv, vec(g_ffn_post[l]), gt_f, f"post_ffn_bwd_{l}")
        dact = _mm_nt(dy, wd, l, BF16, f"ffn_down_dx_{l}")
        dwd = _mm_tn(act, dy, BF16, f"ffn_down_dw_{l}")
        dgp, dup, dwc, dbc = _conv_act_bwd(dact, gp, up, wconv[l], bconv[l], f"conv_act_bwd_{l}")
        dh2 = _mm_nt(dgp, wg, l, F32, f"ffn_gate_dx_{l}")
        dh2 = _mm_nt(dup, wu, l, F32, f"ffn_up_dx_{l}", add=dh2)
        dwg = _mm_tn(h2, dgp, BF16, f"ffn_gate_dw_{l}")
        dwu = _mm_tn(h2, dup, BF16, f"ffn_up_dw_{l}")
        dx2, dsh_f, dsc_f, dg3 = _norm_mod_bwd(dh2, x2, vec(g_ffn_pre[l]), sc_f, dx, f"norm_ffn_bwd_{l}")
        du, dgt_a, dg2 = _post_bwd(dx2, u, vec(g_mix_post[l]), gt_a, f"post_mix_bwd_{l}")
        do = _mm_nt(du, wo, l, BF16, f"attn_out_dx_{l}")
        dwo = _mm_tn(o, du, BF16, f"attn_out_dw_{l}")
        rider = None
        if comm is not None:
            comm["grads"](l, dict(gate=dwg, up=dwu, down=dwd))
            rider = comm["bwd_rider"]()
        if fox:
            j = l // 2
            fl, cum, ck = extra
            (dqkv, dcq, dck), moved = _fox_bwd(qkv, o, do, *stat, cum, ck, *tiles["fox_bwd"], f"fox_bwd_{l}", rider)
            dcq = jnp.max(dcq.reshape(hp_n, s, 2, HEAD_DIM), axis=3)
            dcum = dcq.transpose(1, 0, 2).reshape(s, n_heads) + dck.reshape(n_heads, s).T
            dcum = jnp.pad(dcum, ((0, 0), (0, LANES - n_heads)))
            dfl, dbfg = _fox_gate_bwd(dcum, fl, bfg[j], n_heads, f"fox_gate_bwd_{l}")
            dh1 = _mm_nt(dfl, wfg, j, F32, f"fgate_dx_{l}")
            dh1 = _mm_nt(dqkv, wqkv, l, F32, f"qkv_dx_{l}", add=dh1)
            dwfg = _mm_tn(h1, dfl, F32, f"fgate_dw_{l}")[:, :n_heads]
            dbfg = dbfg[0, :n_heads]
        else:
            (dqkv,), moved = _sb_bwd(qkv, *stat, do, *tiles["sb_bwd"], f"sb_bwd_{l}", rider)
            dh1 = _mm_nt(dqkv, wqkv, l, F32, f"qkv_dx_{l}")
            dwfg = dbfg = None
        if moved is not None:
            comm["done"](moved)
        dwqkv = _mm_tn(h1, dqkv, BF16, f"qkv_dw_{l}")
        dx, dsh_a, dsc_a, dg1 = _norm_mod_bwd(dh1, xin, vec(g_mix_pre[l]), sc_a, dx2, f"norm_mix_bwd_{l}")
        dmod = jnp.concatenate([dsh_a, dsc_a, dgt_a, dsh_f, dsc_f, dgt_f], axis=1)[0]
        small[l] = dict(dmod=dmod, dg1=dg1[0], dg2=dg2[0], dg3=dg3[0], dg4=dg4[0], dbc=dbc[0], dwc=dwc,
                        dbfg=dbfg, dwfg=dwfg)
        big[l] = dict(qkv=dwqkv, o=dwo, gate=dwg, up=dwu, down=dwd)
        if comm is not None:
            comm["grads"](l, dict(qkv=dwqkv, o=dwo))
    return sq, dx, small, big


def _rows128(a, rows):
    flat = a.reshape(-1)
    return jnp.pad(flat, (0, rows * LANES - flat.shape[0])).reshape(rows, LANES)


def _ceil8(n_elems):
    rows = -(-n_elems // LANES)
    return -(-rows // SUBLANES) * SUBLANES


def kernel(x, c, w_mod, b_mod, g_mix_pre, g_mix_post, w_qkv, w_o, w_fg, b_fg, g_ffn_pre, g_ffn_post, w_ffn_gate, w_ffn_up, w_conv, b_conv, w_ffn_down, loss_target, m_w_mod, m_b_mod, m_g_mix_pre, m_g_mix_post, m_w_qkv, m_w_o, m_w_fg, m_b_fg, m_g_ffn_pre, m_g_ffn_post, m_w_ffn_gate, m_w_ffn_up, m_w_conv, m_b_conv, m_w_ffn_down, v_w_mod, v_b_mod, v_g_mix_pre, v_g_mix_post, v_w_qkv, v_w_o, v_w_fg, v_b_fg, v_g_ffn_pre, v_g_ffn_post, v_w_ffn_gate, v_w_ffn_up, v_w_conv, v_b_conv, v_w_ffn_down):
    _, s, d = x.shape
    nl = w_qkv.shape[0]
    nf = w_fg.shape[0]
    n_heads = w_fg.shape[2]
    fs = w_ffn_gate.shape[2]
    fp = -(-fs // LANES) * LANES
    f_full, f_pad = N_DEV * fs, N_DEV * fp
    mod_cols = w_mod.shape[2]
    qs, orows = w_qkv.shape[2], w_o.shape[1]
    xi, yi, ci = _mesh_pos()
    me = 4 * xi + 2 * yi + ci

    c_rows = d // LANES
    c_all = _allgather_small(jnp.pad(c.reshape(1, d), ((0, SUBLANES - 1), (0, 0))).reshape(SUBLANES * c_rows, LANES),
                             "gather_cond")
    c_all = c_all.reshape(N_DEV, SUBLANES, d)[:, 0, :]
    b_mod_cols = lax.dynamic_slice_in_dim(b_mod, me * mod_cols, mod_cols, axis=1).reshape(nl, 1, mod_cols)
    mod_part = _mod_fwd(c_all, w_mod, b_mod_cols, "mod_fwd")

    conv_pad = jnp.pad(w_conv, ((0, 0), (0, 0), (0, fp - fs)))
    r_mod, r_conv, r_fg = _ceil8(mod_part.size), _ceil8(conv_pad.size), _ceil8(w_fg.size)
    payload = jnp.concatenate([_rows128(mod_part, r_mod), _rows128(conv_pad, r_conv), _rows128(w_fg, r_fg)], axis=0)
    got = _allgather_small(payload, "gather_small_weights").reshape(N_DEV, r_mod + r_conv + r_fg, LANES)
    mod_g = got[:, :r_mod].reshape(N_DEV, -1)[:, :mod_part.size].reshape(N_DEV, nl, N_DEV, mod_cols)
    mod = lax.dynamic_index_in_dim(mod_g, me, axis=2, keepdims=False).transpose(1, 0, 2).reshape(nl, N_DEV * mod_cols)
    conv_g = got[:, r_mod:r_mod + r_conv].reshape(N_DEV, -1)[:, :conv_pad.size].reshape(N_DEV, nl, 3, fp)
    wconv_full = conv_g.transpose(1, 2, 0, 3).reshape(nl, 3, f_pad)
    fg_g = got[:, r_mod + r_conv:].reshape(N_DEV, -1)[:, :w_fg.size].reshape(N_DEV, nf, orows, n_heads)
    wfg_full = fg_g.transpose(1, 0, 2, 3).reshape(nf, d, n_heads)
    wfg_full = jnp.pad(wfg_full, ((0, 0), (0, 0), (0, LANES - n_heads))).astype(BF16)
    bfg_full = jnp.pad(b_fg, ((0, 0), (0, LANES - n_heads))).reshape(nf, 1, LANES)
    bconv_full = jnp.pad(b_conv.reshape(nl, N_DEV, fs), ((0, 0), (0, 0), (0, fp - fs))).reshape(nl, 1, f_pad)

    gate_sh = jnp.pad(w_ffn_gate, ((0, 0), (0, 0), (0, fp - fs))).astype(BF16)
    up_sh = jnp.pad(w_ffn_up, ((0, 0), (0, 0), (0, fp - fs))).astype(BF16)
    down_sh = jnp.pad(w_ffn_down, ((0, 0), (0, fp - fs), (0, 0))).astype(BF16)
    shards = [w_qkv.astype(BF16), w_o.astype(BF16), gate_sh, up_sh, down_sh]
    full_shapes = [jax.ShapeDtypeStruct((nl, d, N_DEV * qs), BF16), jax.ShapeDtypeStruct((nl, d, d), BF16),
                   jax.ShapeDtypeStruct((nl, d, f_pad), BF16), jax.ShapeDtypeStruct((nl, d, f_pad), BF16),
                   jax.ShapeDtypeStruct((nl, f_pad, d), BF16)]
    place = [_cols_of(qs), _rows_of(orows), _cols_of(fp), _cols_of(fp), _rows_of(fp)]
    mixer_w, ffn_w = (0, 1), (2, 3, 4)

    def gather_jobs(l, which):
        return [(i, i, _layer_of(l, _whole), _layer_of(l, place[i])) for i in which]

    order = ["qkv", "o", "gate", "up", "down"]
    send = {"qkv": _cols_of(qs), "o": _rows_of(orows), "gate": _cols_of(fp), "up": _cols_of(fp), "down": _rows_of(fp)}
    recv_shapes = [(N_DEV, nl, d, qs), (N_DEV, nl, orows, d), (N_DEV, nl, d, fp), (N_DEV, nl, d, fp), (N_DEV, nl, fp, d)]
    state = {"full": _exchange(shards, [lax.empty(sh.shape, sh.dtype) for sh in full_shapes],
                               gather_jobs(0, mixer_w), "gather_weights_0"),
             "recv": [lax.empty(sh, BF16) for sh in recv_shapes], "pending": [], "moving": None}

    def mixer_host(t):
        fox_before = [l for l in range(t) if l % 2 == 1]
        return fox_before[-1] if fox_before else t - 1

    def fwd_rider(l):
        state["moving"] = "full"
        jobs = gather_jobs(l, ffn_w)
        for t in range(1, nl):
            if mixer_host(t) == l:
                jobs += gather_jobs(t, mixer_w)
        return shards, state["full"], jobs

    def bwd_rider():
        waiting, state["pending"], state["moving"] = state["pending"], [], "recv"
        if not waiting:
            return None
        jobs = [(k, order.index(nm), send[nm], _slot(l)) for k, (l, nm, _) in enumerate(waiting)]
        return [g for _, _, g in waiting], state["recv"], jobs

    def done(dsts):
        state[state["moving"]] = dsts

    def grads(l, new):
        state["pending"] += [(l, nm, new[nm]) for nm in order if nm in new]

    comm = dict(weights=lambda: state["full"], fwd_rider=fwd_rider, bwd_rider=bwd_rider, done=done, grads=grads)
    sq, dx, small, big = _local_step(x[0], loss_target[0], mod, g_mix_pre, g_mix_post, g_ffn_pre, g_ffn_post,
                                     None, None, None, None, None, wfg_full, bfg_full, wconv_full, bconv_full,
                                     ATTN_TILES, comm)
    loss = lax.psum(0.5 * jnp.sum(sq) / d, ("x", "y", "c"))
    recv = _exchange(*bwd_rider(), "scatter_grads_last")
    upd = {}
    for nm, rv, wt, mt, vt in zip(order, recv, [w_qkv, w_o, w_ffn_gate, w_ffn_up, w_ffn_down],
                                  [m_w_qkv, m_w_o, m_w_ffn_gate, m_w_ffn_up, m_w_ffn_down],
                                  [v_w_qkv, v_w_o, v_w_ffn_gate, v_w_ffn_up, v_w_ffn_down]):
        upd[nm] = _adamw(rv, wt, mt, vt, f"adamw_{nm}")

    stack = lambda key: jnp.stack([small[l][key] for l in range(nl)])
    dmod = stack("dmod")
    dgs = [stack(k) for k in ("dg1", "dg2", "dg3", "dg4")]
    dbc = stack("dbc").reshape(nl, N_DEV, fp)[:, :, :fs].reshape(nl, f_full)
    dbfg = jnp.stack([small[l]["dbfg"] for l in range(nl) if l % 2 == 1])
    dwc = stack("dwc").reshape(nl, 3, N_DEV, fp)[:, :, :, :fs].reshape(nl, 3, f_full)
    dwfg = jnp.stack([small[l]["dwfg"] for l in range(nl) if l % 2 == 1])
    rep_parts = [dmod] + dgs + [dbc, dbfg]
    rep_rows = [_ceil8(p.size) for p in rep_parts]
    r_rep, r_wc, r_wfg = sum(rep_rows), _ceil8(dwc.size), _ceil8(dwfg.size)
    payload = jnp.concatenate([_rows128(p, r) for p, r in zip(rep_parts, rep_rows)]
                              + [_rows128(dwc, r_wc), _rows128(dwfg, r_wfg)], axis=0)
    gsm = _allgather_small(payload, "gather_small_grads").reshape(N_DEV, 1, r_rep + r_wc + r_wfg, LANES)

    def pack(parts):
        return jnp.concatenate([_rows128(p, r) for p, r in zip(parts, rep_rows)], axis=0).reshape(1, r_rep, LANES)

    rep_w = [b_mod, g_mix_pre, g_mix_post, g_ffn_pre, g_ffn_post, b_conv, b_fg]
    rep_m = [m_b_mod, m_g_mix_pre, m_g_mix_post, m_g_ffn_pre, m_g_ffn_post, m_b_conv, m_b_fg]
    rep_v = [v_b_mod, v_g_mix_pre, v_g_mix_post, v_g_ffn_pre, v_g_ffn_post, v_b_conv, v_b_fg]
    rep_out = _adamw(gsm[:, :, :r_rep], pack(rep_w), pack(rep_m), pack(rep_v), "adamw_replicated", tr=r_rep, tc=LANES)

    def unpack(packed):
        outs, at = [], 0
        for p, r in zip(rep_w, rep_rows):
            outs.append(packed[0, at:at + r].reshape(-1)[:p.size].reshape(p.shape))
            at += r
        return outs

    rep_g, rep_d, rep_nm, rep_nv = (unpack(a) for a in rep_out)

    wc_all = gsm[:, 0, r_rep:r_rep + r_wc].reshape(N_DEV, -1)[:, :dwc.size].reshape(N_DEV, 1, nl * 3, f_full)
    wc_mine = lax.dynamic_slice_in_dim(wc_all, me * fs, fs, axis=3)
    wc_out = _adamw(wc_mine, w_conv.reshape(1, nl * 3, fs), m_w_conv.reshape(1, nl * 3, fs),
                    v_w_conv.reshape(1, nl * 3, fs), "adamw_conv", tr=nl * 3, tc=fs)
    wc_out = [a.reshape(nl, 3, fs) for a in wc_out]
    wfg_all = gsm[:, 0, r_rep + r_wc:].reshape(N_DEV, -1)[:, :dwfg.size].reshape(N_DEV, nf, d, n_heads)
    wfg_mine = lax.dynamic_slice_in_dim(wfg_all, me * orows, orows, axis=2)
    wfg_out = _adamw(wfg_mine, w_fg, m_w_fg, v_w_fg, "adamw_fgate", tr=orows, tc=n_heads)

    dmod_all = gsm[:, 0, :rep_rows[0]].reshape(N_DEV, -1)[:, :dmod.size].reshape(N_DEV, nl, N_DEV * mod_cols)
    dmod_mine = lax.dynamic_slice_in_dim(dmod_all, me * mod_cols, mod_cols, axis=2)
    gwmod = _wmod_grad(c_all.T, dmod_mine, "wmod_grad")
    wmod_out = _adamw(gwmod.reshape(1, nl, d, mod_cols), w_mod, m_w_mod, v_w_mod, "adamw_mod")

    per_weight = [wmod_out, None, None, None, upd["qkv"], upd["o"], wfg_out, None, None, None,
                  upd["gate"], upd["up"], wc_out, None, upd["down"]]
    rep_index = {1: 0, 2: 1, 3: 2, 8: 3, 9: 4, 13: 5, 7: 6}
    outs = [[], [], [], []]
    for pos, res in enumerate(per_weight):
        for kind in range(4):
            if res is None:
                outs[kind].append((rep_g, rep_d, rep_nm, rep_nv)[kind][rep_index[pos]])
            else:
                outs[kind].append(res[kind])
    return (loss, dx.reshape(1, s, d), *outs[0], *outs[1], *outs[2], *outs[3])
```

```python
import functools

import jax
import jax.numpy as jnp
from jax import lax
from jax.experimental import pallas as pl
from jax.experimental.pallas import tpu as pltpu

F32 = jnp.float32
BF16 = jnp.bfloat16

N_DEV = 8
HEAD_DIM = 64
LANES = 128
SUBLANES = 8
RMS_EPS = 1e-6
ADAM_LR = 0.001
ADAM_B1 = 0.9
ADAM_B2 = 0.999
ADAM_EPS = 1e-08
ADAM_WD = 0.01
ADAM_STEP = 10
VMEM_LIMIT_BYTES = 56 * 1024 * 1024
EXP_UNDERFLOW = -110.0
SUM_TERMS = 1
HALO = 16
MM_ROWS = 2048
ATTN_TILES = {"sb_fwd": (512, 256), "sb_bwd": (512, 256), "fox_fwd": (512, 512), "fox_bwd": (512, 512)}

NN = (((1,), (0,)), ((), ()))
NT = (((1,), (1,)), ((), ()))
TN = (((0,), (0,)), ((), ()))
MESH = pl.DeviceIdType.MESH


def _params(*sem):
    return pltpu.CompilerParams(dimension_semantics=sem, vmem_limit_bytes=VMEM_LIMIT_BYTES)


def _pick(n, pref, quantum):
    if n <= pref:
        return n
    t = (pref // quantum) * quantum
    while n % t:
        t -= quantum
    return t


def _mm(a, b, *, dims, grid, a_spec, b_spec, o_spec, out_shape, name, add=None):
    nk = grid[2]
    acc_shape = tuple(d for d in o_spec.block_shape if d is not None)
    o_dtype = out_shape.dtype

    def body(*refs):
        if add is None:
            a_ref, b_ref, o_ref, acc_ref = refs
            add_ref = None
        else:
            a_ref, b_ref, add_ref, o_ref, acc_ref = refs
        k = pl.program_id(2)
        part = lax.dot_general(a_ref[...], b_ref[...], dims, preferred_element_type=F32)

        def finish(total):
            if add_ref is not None:
                total = total + add_ref[...]
            o_ref[...] = total.astype(o_dtype)

        if nk == 1:
            finish(part)
            return

        @pl.when(k == 0)
        def _():
            acc_ref[...] = part

        @pl.when(jnp.logical_and(k > 0, k < nk - 1))
        def _():
            acc_ref[...] += part

        @pl.when(k == nk - 1)
        def _():
            finish(acc_ref[...] + part)

    operands = [a, b] if add is None else [a, b, add]
    in_specs = [a_spec, b_spec] if add is None else [a_spec, b_spec, o_spec]
    return pl.pallas_call(
        body, name=name, grid=grid, in_specs=in_specs, out_specs=o_spec, out_shape=out_shape,
        scratch_shapes=[pltpu.VMEM(acc_shape, F32)],
        compiler_params=_params("parallel", "parallel", "arbitrary"),
    )(*operands)


def _mm_rows(out_dtype, has_add):
    return MM_ROWS if (jnp.dtype(out_dtype).itemsize == 2 and not has_add) else MM_ROWS // 2


def _mm_nn(a, w, l, out_dtype, name, *, col0=0, n=None, split_out=None):
    m, kdim = a.shape
    n = w.shape[2] if n is None else n
    tm, tk = _pick(m, _mm_rows(out_dtype, False), SUBLANES), _pick(kdim, 1024, LANES)
    tn = _pick(n if split_out is None else split_out, 1024, LANES)
    jb = col0 // tn
    grid = (m // tm, n // tn, kdim // tk)
    a_spec = pl.BlockSpec((tm, tk), lambda i, j, k: (i, k))
    b_spec = pl.BlockSpec((None, tk, tn), lambda i, j, k: (l, k, j + jb))
    if split_out is None:
        o_spec = pl.BlockSpec((tm, tn), lambda i, j, k: (i, j))
        shape = jax.ShapeDtypeStruct((m, n), out_dtype)
    else:
        nj1 = split_out // tn
        o_spec = pl.BlockSpec((None, tm, tn), lambda i, j, k: (j // nj1, i, j % nj1))
        shape = jax.ShapeDtypeStruct((n // split_out, m, split_out), out_dtype)
    return _mm(a, w, dims=NN, grid=grid, a_spec=a_spec, b_spec=b_spec, o_spec=o_spec, out_shape=shape, name=name)


def _mm_nt(a, w, l, out_dtype, name, *, add=None):
    n, kdim = w.shape[1], w.shape[2]
    if a.ndim == 2:
        m = a.shape[0]
        tk = _pick(kdim, 1024, LANES)
        a_spec_of = lambda tm: pl.BlockSpec((tm, tk), lambda i, j, k: (i, k))
    else:
        m, seg = a.shape[1], a.shape[2]
        tk = _pick(seg, 1024, LANES)
        nk1 = seg // tk
        a_spec_of = lambda tm: pl.BlockSpec((None, tm, tk), lambda i, j, k: (k // nk1, i, k % nk1))
    tm, tn = _pick(m, _mm_rows(out_dtype, add is not None), SUBLANES), _pick(n, 1024, LANES)
    grid = (m // tm, n // tn, kdim // tk)
    b_spec = pl.BlockSpec((None, tn, tk), lambda i, j, k: (l, j, k))
    o_spec = pl.BlockSpec((tm, tn), lambda i, j, k: (i, j))
    return _mm(a, w, dims=NT, grid=grid, a_spec=a_spec_of(tm), b_spec=b_spec, o_spec=o_spec,
               out_shape=jax.ShapeDtypeStruct((m, n), out_dtype), name=name, add=add)


def _mm_tn(a, b, out_dtype, name):
    kdim, m = a.shape
    tk, tm = _pick(kdim, 1024, SUBLANES), _pick(m, _mm_rows(out_dtype, False), LANES)
    if b.ndim == 2:
        n = b.shape[1]
        tn = _pick(n, 1024, LANES)
        b_spec = pl.BlockSpec((tk, tn), lambda i, j, k: (k, j))
    else:
        seg = b.shape[2]
        n = b.shape[0] * seg
        tn = _pick(seg, 1024, LANES)
        nj1 = seg // tn
        b_spec = pl.BlockSpec((None, tk, tn), lambda i, j, k: (j // nj1, k, j % nj1))
    grid = (m // tm, n // tn, kdim // tk)
    a_spec = pl.BlockSpec((tk, tm), lambda i, j, k: (k, i))
    o_spec = pl.BlockSpec((tm, tn), lambda i, j, k: (i, j))
    return _mm(a, b, dims=TN, grid=grid, a_spec=a_spec, b_spec=b_spec, o_spec=o_spec,
               out_shape=jax.ShapeDtypeStruct((m, n), out_dtype), name=name)


def _rstd(x):
    return lax.rsqrt(jnp.mean(x * x, axis=-1, keepdims=True) + RMS_EPS)


def _norm_mod_fwd(x, g, shift, scale, name):
    s, d = x.shape
    tm = _pick(s, 512, SUBLANES)

    def body(x_ref, g_ref, sh_ref, sc_ref, h_ref):
        xv = x_ref[...]
        y = (xv * _rstd(xv)) * g_ref[...]
        h_ref[...] = (y * (1.0 + sc_ref[...]) + sh_ref[...]).astype(BF16)

    row = pl.BlockSpec((tm, d), lambda i: (i, 0))
    vec = pl.BlockSpec((1, d), lambda i: (0, 0))
    return pl.pallas_call(body, name=name, grid=(s // tm,), in_specs=[row, vec, vec, vec], out_specs=row,
                          out_shape=jax.ShapeDtypeStruct((s, d), BF16), compiler_params=_params("parallel"))(x, g, shift, scale)


def _norm_mod_bwd(dh, x, g, scale, dres, name):
    s, d = x.shape
    tm = _pick(s, 512, SUBLANES)

    def body(dh_ref, x_ref, g_ref, sc_ref, dres_ref, dx_ref, dsh_ref, dsc_ref, dg_ref):
        i = pl.program_id(0)
        xv, dhv, gv = x_ref[...], dh_ref[...], g_ref[...]
        r = _rstd(xv)
        xh = xv * r
        dn = dhv * (1.0 + sc_ref[...])
        gd = dn * gv
        dx_ref[...] = dres_ref[...] + r * (gd - xh * jnp.mean(gd * xh, axis=-1, keepdims=True))

        @pl.when(i == 0)
        def _():
            dsh_ref[...] = jnp.zeros_like(dsh_ref)
            dsc_ref[...] = jnp.zeros_like(dsc_ref)
            dg_ref[...] = jnp.zeros_like(dg_ref)

        dsh_ref[...] += jnp.sum(dhv, axis=0, keepdims=True)
        dsc_ref[...] += jnp.sum(dhv * (xh * gv), axis=0, keepdims=True)
        dg_ref[...] += jnp.sum(dn * xh, axis=0, keepdims=True)

    row = pl.BlockSpec((tm, d), lambda i: (i, 0))
    vec = pl.BlockSpec((1, d), lambda i: (0, 0))
    vshape = jax.ShapeDtypeStruct((1, d), F32)
    return pl.pallas_call(body, name=name, grid=(s // tm,), in_specs=[row, row, vec, vec, row],
                          out_specs=[row, vec, vec, vec],
                          out_shape=[jax.ShapeDtypeStruct((s, d), F32), vshape, vshape, vshape],
                          compiler_params=_params("arbitrary"))(dh, x, g, scale, dres)


def _post_fwd(x, u, g, gate, name):
    s, d = x.shape
    tm = _pick(s, 512, SUBLANES)

    def body(x_ref, u_ref, g_ref, gt_ref, o_ref):
        uv = u_ref[...]
        o_ref[...] = x_ref[...] + gt_ref[...] * ((uv * _rstd(uv)) * g_ref[...])

    row = pl.BlockSpec((tm, d), lambda i: (i, 0))
    vec = pl.BlockSpec((1, d), lambda i: (0, 0))
    return pl.pallas_call(body, name=name, grid=(s // tm,), in_specs=[row, row, vec, vec], out_specs=row,
                          out_shape=jax.ShapeDtypeStruct((s, d), F32), compiler_params=_params("parallel"))(x, u, g, gate)


def _post_norm_fwd(x, u, g, gate, g_next, shift, scale, name):
    s, d = x.shape
    tm = _pick(s, 512, SUBLANES)

    def body(x_ref, u_ref, g_ref, gt_ref, gn_ref, sh_ref, sc_ref, x2_ref, h_ref):
        uv = u_ref[...]
        x2 = x_ref[...] + gt_ref[...] * ((uv * _rstd(uv)) * g_ref[...])
        x2_ref[...] = x2
        y = (x2 * _rstd(x2)) * gn_ref[...]
        h_ref[...] = (y * (1.0 + sc_ref[...]) + sh_ref[...]).astype(BF16)

    row = pl.BlockSpec((tm, d), lambda i: (i, 0))
    vec = pl.BlockSpec((1, d), lambda i: (0, 0))
    return pl.pallas_call(body, name=name, grid=(s // tm,), in_specs=[row, row, vec, vec, vec, vec, vec],
                          out_specs=[row, row],
                          out_shape=[jax.ShapeDtypeStruct((s, d), F32), jax.ShapeDtypeStruct((s, d), BF16)],
                          compiler_params=_params("parallel"))(x, u, g, gate, g_next, shift, scale)


def _post_bwd(dx, u, g, gate, name):
    s, d = u.shape
    tm = _pick(s, 512, SUBLANES)

    def body(dx_ref, u_ref, g_ref, gt_ref, du_ref, dgt_ref, dg_ref):
        i = pl.program_id(0)
        uv, dxv, gv = u_ref[...], dx_ref[...], g_ref[...]
        r = _rstd(uv)
        uh = uv * r
        dn = dxv * gt_ref[...]
        gd = dn * gv
        du_ref[...] = (r * (gd - uh * jnp.mean(gd * uh, axis=-1, keepdims=True))).astype(BF16)

        @pl.when(i == 0)
        def _():
            dgt_ref[...] = jnp.zeros_like(dgt_ref)
            dg_ref[...] = jnp.zeros_like(dg_ref)

        dgt_ref[...] += jnp.sum(dxv * (uh * gv), axis=0, keepdims=True)
        dg_ref[...] += jnp.sum(dn * uh, axis=0, keepdims=True)

    row = pl.BlockSpec((tm, d), lambda i: (i, 0))
    vec = pl.BlockSpec((1, d), lambda i: (0, 0))
    vshape = jax.ShapeDtypeStruct((1, d), F32)
    return pl.pallas_call(body, name=name, grid=(s // tm,), in_specs=[row, row, vec, vec],
                          out_specs=[row, vec, vec],
                          out_shape=[jax.ShapeDtypeStruct((s, d), BF16), vshape, vshape],
                          compiler_params=_params("arbitrary"))(dx, u, g, gate)


def _shift_rows(cur, prev8, k):
    rolled = pltpu.roll(cur, k, axis=0)
    rolled_prev = pltpu.roll(prev8, k, axis=0)
    i8 = lax.broadcasted_iota(jnp.int32, prev8.shape, 0)
    top = jnp.where(i8 < k, rolled_prev, rolled[:SUBLANES])
    return jnp.concatenate([top, rolled[SUBLANES:]], axis=0)


def _conv_pre(g, prev8, wc_ref, bc_ref):
    s1 = _shift_rows(g, prev8, 1)
    s2 = _shift_rows(g, prev8, 2)
    gc = bc_ref[...] + wc_ref[0:1, :] * s2 + wc_ref[1:2, :] * s1 + wc_ref[2:3, :] * g
    return gc, s1, s2


def _conv_act_fwd(gp, up, wc, bc, name):
    s, f = gp.shape
    tm, tf = _pick(s, 512, SUBLANES), _pick(f, 768, LANES)
    rh = tm // HALO

    def body(g_ref, gprev_ref, up_ref, wc_ref, bc_ref, a_ref):
        i = pl.program_id(1)
        prev = jnp.where(i == 0, 0.0, gprev_ref[...].astype(F32)[HALO - SUBLANES:])
        gc, _, _ = _conv_pre(g_ref[...].astype(F32), prev, wc_ref, bc_ref)
        a_ref[...] = ((gc * jax.nn.sigmoid(gc)) * up_ref[...].astype(F32)).astype(BF16)

    tile = pl.BlockSpec((tm, tf), lambda j, i: (i, j))
    prev = pl.BlockSpec((HALO, tf), lambda j, i: (jnp.maximum(i * rh - 1, 0), j))
    return pl.pallas_call(body, name=name, grid=(f // tf, s // tm),
                          in_specs=[tile, prev, tile, pl.BlockSpec((3, tf), lambda j, i: (0, j)),
                                    pl.BlockSpec((1, tf), lambda j, i: (0, j))],
                          out_specs=tile, out_shape=jax.ShapeDtypeStruct((s, f), BF16),
                          compiler_params=_params("parallel", "parallel"))(gp, gp, up, wc, bc)


def _conv_act_bwd(da, gp, up, wc, bc, name):
    s, f = gp.shape
    tm, tf = _pick(s, 512, SUBLANES), _pick(f, 768, LANES)
    rh = tm // HALO
    nrow = s // tm

    def body(da_ref, dan_ref, g_ref, gprev_ref, gn_ref, up_ref, upn_ref, wc_ref, bc_ref,
             dgp_ref, dup_ref, dwc_ref, dbc_ref):
        i = pl.program_id(1)
        last = i == nrow - 1
        head = lambda ref: ref[...].astype(F32)[:SUBLANES]
        prev = jnp.where(i == 0, 0.0, gprev_ref[...].astype(F32)[HALO - SUBLANES:])
        g_ext = jnp.concatenate([g_ref[...].astype(F32), head(gn_ref)], axis=0)
        up_ext = jnp.concatenate([up_ref[...].astype(F32), head(upn_ref)], axis=0)
        da_ext = jnp.concatenate([da_ref[...].astype(F32), jnp.where(last, 0.0, head(dan_ref))], axis=0)
        gc, s1, s2 = _conv_pre(g_ext, prev, wc_ref, bc_ref)
        sg = jax.nn.sigmoid(gc)
        dup_ref[...] = (da_ext * (gc * sg))[:tm].astype(BF16)
        dgc = da_ext * up_ext * (sg * (1.0 + gc * (1.0 - sg)))
        ext = tm + SUBLANES
        dgp = (wc_ref[2:3, :] * dgc + wc_ref[1:2, :] * pltpu.roll(dgc, ext - 1, axis=0)
               + wc_ref[0:1, :] * pltpu.roll(dgc, ext - 2, axis=0))
        dgp_ref[...] = dgp[:tm].astype(BF16)

        @pl.when(i == 0)
        def _():
            dwc_ref[...] = jnp.zeros_like(dwc_ref)
            dbc_ref[...] = jnp.zeros_like(dbc_ref)

        d0 = dgc[:tm]
        dwc_ref[0:1, :] += jnp.sum(d0 * s2[:tm], axis=0, keepdims=True)
        dwc_ref[1:2, :] += jnp.sum(d0 * s1[:tm], axis=0, keepdims=True)
        dwc_ref[2:3, :] += jnp.sum(d0 * g_ext[:tm], axis=0, keepdims=True)
        dbc_ref[...] += jnp.sum(d0, axis=0, keepdims=True)

    tile = pl.BlockSpec((tm, tf), lambda j, i: (i, j))
    prev = pl.BlockSpec((HALO, tf), lambda j, i: (jnp.maximum(i * rh - 1, 0), j))
    nxt = pl.BlockSpec((HALO, tf), lambda j, i: (jnp.minimum((i + 1) * rh, s // HALO - 1), j))
    return pl.pallas_call(body, name=name, grid=(f // tf, nrow),
                          in_specs=[tile, nxt, tile, prev, nxt, tile, nxt,
                                    pl.BlockSpec((3, tf), lambda j, i: (0, j)), pl.BlockSpec((1, tf), lambda j, i: (0, j))],
                          out_specs=[tile, tile, pl.BlockSpec((3, tf), lambda j, i: (0, j)),
                                     pl.BlockSpec((1, tf), lambda j, i: (0, j))],
                          out_shape=[jax.ShapeDtypeStruct((s, f), BF16), jax.ShapeDtypeStruct((s, f), BF16),
                                     jax.ShapeDtypeStruct((3, f), F32), jax.ShapeDtypeStruct((1, f), F32)],
                          compiler_params=_params("parallel", "arbitrary"))(da, da, gp, gp, gp, up, up, wc, bc)


def _split_bf16(v, parts):
    out, rem = [], v
    for _ in range(parts):
        t = rem.astype(BF16)
        out.append(t)
        rem = rem - t.astype(F32)
    return jnp.concatenate(out, axis=1)


def _tri(t, cmp, reps):
    row = lax.broadcasted_iota(jnp.int32, (t, t), 0)
    col = lax.broadcasted_iota(jnp.int32, (t, t), 1)
    m = cmp(row, col).astype(BF16)
    return jnp.concatenate([m] * reps, axis=0)


def _log_sigmoid(z):
    mn = jnp.minimum(z, 0.0)
    return mn - jnp.log(1.0 + jnp.exp(mn + (mn - z)))


def _log_one_minus_sigmoid(z):
    mn = jnp.minimum(z, 0.0)
    neg = mn - z
    return neg - jnp.log(1.0 + jnp.exp(mn + neg))


def _pair(xv):
    lane = lax.broadcasted_iota(jnp.int32, xv.shape, 1)
    zero = jnp.zeros_like(xv)
    return jnp.where(lane < HEAD_DIM, xv, zero), jnp.where(lane < HEAD_DIM, zero, xv)


def _attn_specs(s, tq):
    q_spec = pl.BlockSpec((None, tq, LANES), lambda hp, qi: (0, qi, hp))
    k_spec = pl.BlockSpec((None, s, LANES), lambda hp, qi: (1, 0, hp))
    v_spec = pl.BlockSpec((None, s, LANES), lambda hp, qi: (2, 0, hp))
    return q_spec, k_spec, v_spec


def _rel_mask(tq, tk, i, strict):
    row = lax.broadcasted_iota(jnp.int32, (tq, tk), 0)
    col = lax.broadcasted_iota(jnp.int32, (tq, tk), 1) + i * tk
    return col < row if strict else col <= row


def _key_bound(qkv, name):
    _, s, d = qkv.shape

    def body(k_ref, o_ref):
        sq = jnp.square(k_ref[...].astype(F32))
        low = lax.broadcasted_iota(jnp.int32, sq.shape, 1) < HEAD_DIM
        n0 = jnp.max(jnp.sum(jnp.where(low, sq, 0.0), axis=1, keepdims=True))
        n1 = jnp.max(jnp.sum(jnp.where(low, 0.0, sq), axis=1, keepdims=True))
        low8 = lax.broadcasted_iota(jnp.int32, (SUBLANES, LANES), 1) < HEAD_DIM
        o_ref[...] = jnp.sqrt(jnp.where(low8, n0, n1))

    return pl.pallas_call(body, name=name, grid=(d // LANES,),
                          in_specs=[pl.BlockSpec((None, s, LANES), lambda hp: (1, 0, hp))],
                          out_specs=pl.BlockSpec((None, SUBLANES, LANES), lambda hp: (hp, 0, 0)),
                          out_shape=jax.ShapeDtypeStruct((d // LANES, SUBLANES, LANES), F32),
                          compiler_params=_params("parallel"))(qkv)


def _sb_fwd(qkv, kbound, tq, tk, name, rider=None):
    _, s, d = qkv.shape
    hp_n, nq, nd = d // LANES, s // tq, tq // tk
    scale = HEAD_DIM ** -0.5

    def body(q_ref, k_ref, v_ref, kb_ref, o_ref, lt_ref, nb_ref):
        qi = pl.program_id(1)
        lane = lax.broadcasted_iota(jnp.int32, (tq, LANES), 1)
        onward = _tri(tk, lambda j, sidx: j >= sidx, SUM_TERMS)
        qpair = _pair(q_ref[...] * scale)
        kbv = kb_ref[...]
        z_max = [jnp.sqrt(jnp.sum(jnp.square(qh.astype(F32)), axis=1, keepdims=True)) * (1.01 * km)
                 for qh, km in zip(qpair, (kbv[0:1, 0:1], kbv[0:1, HEAD_DIM:HEAD_DIM + 1]))]

        def alive(runs):
            return jnp.logical_or(jnp.max(runs[0] + z_max[0]) >= EXP_UNDERFLOW,
                                  jnp.max(runs[1] + z_max[1]) >= EXP_UNDERFLOW)

        def block(kb, carry, mask):
            acc, runs = carry[0], carry[1:]
            ks = pl.multiple_of(kb * tk, tk)
            kblk = k_ref[pl.ds(ks, tk), :]
            vpair = _pair(v_ref[pl.ds(ks, tk), :])
            probs, new_runs = [], []
            for qh, run in zip(qpair, runs):
                z = lax.dot_general(qh, kblk, NT, preferred_element_type=F32)
                l1 = _log_one_minus_sigmoid(z)
                if mask is not None:
                    l1 = jnp.where(mask, l1, 0.0)
                c = jnp.dot(_split_bf16(l1, SUM_TERMS), onward, preferred_element_type=F32)
                a = jnp.exp(z + c + run)
                if mask is not None:
                    a = jnp.where(mask, a, 0.0)
                probs.append(a.astype(BF16))
                new_runs.append(run + jnp.sum(l1, axis=1, keepdims=True))
            acc = acc + jnp.dot(jnp.concatenate(probs, axis=1), jnp.concatenate(vpair, axis=0),
                                preferred_element_type=F32)
            return (acc, *new_runs)

        zero1 = jnp.zeros((tq, 1), F32)
        carry = (jnp.zeros((tq, LANES), F32), zero1, zero1)
        for i in reversed(range(nd)):
            carry = block(qi * nd + i, carry, _rel_mask(tq, tk, i, True))

        def step(state):
            new = block(qi * nd - 1 - state[0], state[2:], None)
            return (state[0] + 1, alive(new[1:]), *new)

        done, _, acc, run0, run1 = lax.while_loop(lambda st: jnp.logical_and(st[0] < qi * nd, st[1]), step,
                                                  (jnp.int32(0), alive(carry[1:]), *carry))
        o_ref[...] = acc.astype(BF16)
        lt_ref[...] = jnp.where(lane < HEAD_DIM, run0, run1)
        nb_ref[...] = jnp.zeros((SUBLANES, LANES), F32) + done.astype(F32)

    q_spec, k_spec, v_spec = _attn_specs(s, tq)
    return _attn_call(
        body, name, (hp_n, nq),
        [q_spec, k_spec, v_spec, pl.BlockSpec((None, SUBLANES, LANES), lambda hp, qi: (hp, 0, 0))],
        [pl.BlockSpec((tq, LANES), lambda hp, qi: (qi, hp)),
         pl.BlockSpec((None, tq, LANES), lambda hp, qi: (hp, qi, 0)),
         pl.BlockSpec((None, None, SUBLANES, LANES), lambda hp, qi: (hp, qi, 0, 0))],
        [jax.ShapeDtypeStruct((s, d), BF16), jax.ShapeDtypeStruct((hp_n, s, LANES), F32),
         jax.ShapeDtypeStruct((hp_n, nq, SUBLANES, LANES), F32)],
        [], ("parallel", "parallel"), (qkv, qkv, qkv, kbound), rider)


def _sb_bwd(qkv, ltot, visited, do, tq, tk, name, rider=None):
    _, s, d = qkv.shape
    hp_n, nq, nd = d // LANES, s // tq, tq // tk
    scale = HEAD_DIM ** -0.5

    def body(q_ref, k_ref, v_ref, lt_ref, nb_ref, do_ref, dqkv_ref, dk_acc, dv_acc):
        qi = pl.program_id(1)
        first = qi * nd - jnp.max(nb_ref[...]).astype(jnp.int32)

        @pl.when(qi == 0)
        def _():
            dk_acc[...] = jnp.zeros_like(dk_acc)
            dv_acc[...] = jnp.zeros_like(dv_acc)

        onward = _tri(tk, lambda j, sidx: j >= sidx, SUM_TERMS)
        before = _tri(tk, lambda j, sidx: j < sidx, SUM_TERMS)
        qpair = _pair(q_ref[...] * scale)
        dopair = _pair(do_ref[...])
        ltv = lt_ref[...]
        lts = (ltv[:, 0:1], ltv[:, HEAD_DIM:HEAD_DIM + 1])

        def block(kb, carry, mask):
            dq, lpres, gpres = carry[0], carry[1:3], carry[3:5]
            ks = pl.multiple_of(kb * tk, tk)
            kblk, vblk = k_ref[pl.ds(ks, tk), :], v_ref[pl.ds(ks, tk), :]
            kpair = _pair(kblk * scale)
            dzs, probs, new_l, new_g = [], [], [], []
            for qh, doh, lt, lpre, gpre in zip(qpair, dopair, lts, lpres, gpres):
                z = lax.dot_general(qh, kblk, NT, preferred_element_type=F32)
                l1 = _log_one_minus_sigmoid(z)
                if mask is not None:
                    l1 = jnp.where(mask, l1, 0.0)
                lpre = lpre + jnp.sum(l1, axis=1, keepdims=True)
                c = jnp.dot(_split_bf16(l1, SUM_TERMS), onward, preferred_element_type=F32)
                a = jnp.exp(z + c + (lt - lpre))
                if mask is not None:
                    a = jnp.where(mask, a, 0.0)
                da = lax.dot_general(doh, vblk, NT, preferred_element_type=F32)
                g = da * a
                p = jnp.dot(_split_bf16(g, SUM_TERMS), before, preferred_element_type=F32) + gpre
                dz = g - jnp.exp(z + l1) * (g + p)
                if mask is not None:
                    dz = jnp.where(mask, dz, 0.0)
                dzs.append(dz.astype(BF16))
                probs.append(a.astype(BF16))
                new_l.append(lpre)
                new_g.append(gpre + jnp.sum(g, axis=1, keepdims=True))
            dq = dq + jnp.dot(jnp.concatenate(dzs, axis=1), jnp.concatenate(kpair, axis=0),
                              preferred_element_type=F32)
            dk_acc[pl.ds(ks, tk), :] += lax.dot_general(jnp.concatenate(dzs, axis=0), jnp.concatenate(qpair, axis=0),
                                                        TN, preferred_element_type=F32)
            dv_acc[pl.ds(ks, tk), :] += lax.dot_general(jnp.concatenate(probs, axis=0), jnp.concatenate(dopair, axis=0),
                                                        TN, preferred_element_type=F32)
            return (dq, *new_l, *new_g)

        zero1 = jnp.zeros((tq, 1), F32)
        init = (jnp.zeros((tq, LANES), F32), zero1, zero1, zero1, zero1)
        carry = lax.fori_loop(first, qi * nd, lambda j, cr: block(j, cr, None), init)
        for i in range(nd):
            carry = block(qi * nd + i, carry, _rel_mask(tq, tk, i, True))
        dqkv_ref[0, pl.ds(pl.multiple_of(qi * tq, tq), tq), :] = carry[0].astype(BF16)

        @pl.when(qi == nq - 1)
        def _():
            dqkv_ref[1] = dk_acc[...].astype(BF16)
            dqkv_ref[2] = dv_acc[...].astype(BF16)

    q_spec, k_spec, v_spec = _attn_specs(s, tq)
    return _attn_call(
        body, name, (hp_n, nq),
        [q_spec, k_spec, v_spec, pl.BlockSpec((None, tq, LANES), lambda hp, qi: (hp, qi, 0)),
         pl.BlockSpec((None, None, SUBLANES, LANES), lambda hp, qi: (hp, qi, 0, 0)),
         pl.BlockSpec((tq, LANES), lambda hp, qi: (qi, hp))],
        [pl.BlockSpec((3, s, LANES), lambda hp, qi: (0, 0, hp))], [jax.ShapeDtypeStruct((3, s, d), BF16)],
        [pltpu.VMEM((s, LANES), F32), pltpu.VMEM((s, LANES), F32)],
        ("parallel", "arbitrary"), (qkv, qkv, qkv, ltot, visited, do), rider)


def _fox_prep(fl, bias, name):
    s, w = fl.shape
    tb = _pick(s, 512, SUBLANES)

    def body(fl_ref, b_ref, cum_ref, carry_ref):
        i = pl.program_id(0)

        @pl.when(i == 0)
        def _():
            carry_ref[...] = jnp.zeros_like(carry_ref)

        logf = _log_sigmoid(fl_ref[...] + b_ref[...])
        row = lax.broadcasted_iota(jnp.int32, (tb, tb), 0)
        col = lax.broadcasted_iota(jnp.int32, (tb, tb), 1)
        incl = (col <= row).astype(BF16)
        tot = carry_ref[...]
        rem = logf
        for _ in range(3):
            part = rem.astype(BF16)
            tot = tot + jnp.dot(incl, part, preferred_element_type=F32)
            rem = rem - part.astype(F32)
        cum_ref[...] = tot
        carry_ref[...] = tot[tb - 1:tb, :]

    blk = pl.BlockSpec((tb, w), lambda i: (i, 0))
    return pl.pallas_call(body, name=name, grid=(s // tb,), in_specs=[blk, pl.BlockSpec((1, w), lambda i: (0, 0))],
                          out_specs=blk, out_shape=jax.ShapeDtypeStruct((s, w), F32),
                          scratch_shapes=[pltpu.VMEM((1, w), F32)], compiler_params=_params("arbitrary"))(fl, bias)


def _fox_gate_bwd(dcum, fl, bias, n_heads, name):
    s, w = fl.shape
    tb = _pick(s, 512, SUBLANES)
    nb = s // tb

    def body(dc_ref, fl_ref, b_ref, dfl_ref, db_ref, carry_ref):
        i = pl.program_id(0)

        @pl.when(i == 0)
        def _():
            carry_ref[...] = jnp.zeros_like(carry_ref)
            db_ref[...] = jnp.zeros_like(db_ref)

        row = lax.broadcasted_iota(jnp.int32, (tb, tb), 0)
        col = lax.broadcasted_iota(jnp.int32, (tb, tb), 1)
        incl = (col >= row).astype(BF16)
        tot = jnp.broadcast_to(carry_ref[...], (tb, w))
        rem = dc_ref[...]
        for _ in range(3):
            part = rem.astype(BF16)
            tot = tot + jnp.dot(incl, part, preferred_element_type=F32)
            rem = rem - part.astype(F32)
        carry_ref[...] = tot[0:1, :]
        xg = fl_ref[...] + b_ref[...]
        e = jnp.exp(-jnp.abs(xg))
        sig_neg = jnp.where(xg >= 0.0, e, 1.0) / (1.0 + e)
        lane = lax.broadcasted_iota(jnp.int32, (tb, w), 1)
        dfl = jnp.where(lane < n_heads, tot * sig_neg, 0.0)
        dfl_ref[...] = dfl.astype(BF16)
        db_ref[...] += jnp.sum(dfl, axis=0, keepdims=True)

    blk = pl.BlockSpec((tb, w), lambda i: (nb - 1 - i, 0))
    vec = pl.BlockSpec((1, w), lambda i: (0, 0))
    return pl.pallas_call(body, name=name, grid=(nb,), in_specs=[blk, blk, vec], out_specs=[blk, vec],
                          out_shape=[jax.ShapeDtypeStruct((s, w), BF16), jax.ShapeDtypeStruct((1, w), F32)],
                          scratch_shapes=[pltpu.VMEM((1, w), F32)], compiler_params=_params("arbitrary"))(dcum, fl, bias)


def _head_columns(cum_blk, hp):
    lane = lax.broadcasted_iota(jnp.int32, cum_blk.shape, 1)
    return tuple(jnp.sum(jnp.where(lane == 2 * hp + h, cum_blk, 0.0), axis=1, keepdims=True) for h in range(2))


def _fox_fwd(qkv, kbound, cum, ck, tq, tk, name, rider=None):
    _, s, d = qkv.shape
    hp_n, nq, nd = d // LANES, s // tq, tq // tk
    scale = HEAD_DIM ** -0.5

    def body(q_ref, k_ref, v_ref, kb_ref, cq_ref, ck_ref, o_ref, lse_ref, nb_ref):
        qi = pl.program_id(1)
        low = lax.broadcasted_iota(jnp.int32, (tq, LANES), 1) < HEAD_DIM
        qpair = _pair(q_ref[...] * scale)
        cqs = _head_columns(cq_ref[...], pl.program_id(0))
        kbv = kb_ref[...]
        reach = [jnp.sqrt(jnp.sum(jnp.square(qh.astype(F32)), axis=1, keepdims=True)) * (1.01 * km) + cqh
                 for qh, km, cqh in zip(qpair, (kbv[0:1, 0:1], kbv[0:1, HEAD_DIM:HEAD_DIM + 1]), cqs)]

        def alive(stats, kb):
            ks = pl.multiple_of(kb * tk, tk)
            return functools.reduce(jnp.logical_or, [
                jnp.max(reach[h] - ck_ref[h:h + 1, pl.ds(ks, tk)][:, 0:1] - stats[2 * h]) >= EXP_UNDERFLOW
                for h in range(2)])

        def block(kb, carry, mask):
            acc, stats = carry[0], carry[1:]
            ks = pl.multiple_of(kb * tk, tk)
            kblk = k_ref[pl.ds(ks, tk), :]
            vpair = _pair(v_ref[pl.ds(ks, tk), :])
            probs, alphas, new = [], [], []
            for h, (qh, cqh) in enumerate(zip(qpair, cqs)):
                m, lsum = stats[2 * h], stats[2 * h + 1]
                z = lax.dot_general(qh, kblk, NT, preferred_element_type=F32)
                sc = z + (cqh - ck_ref[h:h + 1, pl.ds(ks, tk)])
                if mask is not None:
                    sc = jnp.where(mask, sc, -jnp.inf)
                m_new = jnp.maximum(m, jnp.max(sc, axis=1, keepdims=True))
                alpha = jnp.exp(m - m_new)
                p = jnp.exp(sc - m_new)
                new += [m_new, alpha * lsum + jnp.sum(p, axis=1, keepdims=True)]
                probs.append(p.astype(BF16))
                alphas.append(alpha)
            acc = jnp.where(low, alphas[0], alphas[1]) * acc + jnp.dot(
                jnp.concatenate(probs, axis=1), jnp.concatenate(vpair, axis=0), preferred_element_type=F32)
            return (acc, *new)

        neg = jnp.full((tq, 1), -jnp.inf, F32)
        zero1 = jnp.zeros((tq, 1), F32)
        carry = (jnp.zeros((tq, LANES), F32), neg, zero1, neg, zero1)
        for i in range(nd):
            carry = block(qi * nd + i, carry, _rel_mask(tq, tk, i, False))

        def step(state):
            kb = qi * nd - 1 - state[0]
            new = block(kb, state[2:], None)
            return (state[0] + 1, alive(new[1:], kb), *new)

        done, _, acc, m0, l0, m1, l1 = lax.while_loop(lambda st: jnp.logical_and(st[0] < qi * nd, st[1]), step,
                                                      (jnp.int32(0), alive(carry[1:], qi * nd), *carry))
        o_ref[...] = (acc / jnp.where(low, l0, l1)).astype(BF16)
        lse_ref[...] = jnp.where(low, m0 + jnp.log(l0), m1 + jnp.log(l1))
        nb_ref[...] = jnp.zeros((SUBLANES, LANES), F32) + done.astype(F32)

    q_spec, k_spec, v_spec = _attn_specs(s, tq)
    pair_rows = pl.BlockSpec((None, tq, LANES), lambda hp, qi: (hp, qi, 0))
    return _attn_call(
        body, name, (hp_n, nq),
        [q_spec, k_spec, v_spec, pl.BlockSpec((None, SUBLANES, LANES), lambda hp, qi: (hp, 0, 0)),
         pl.BlockSpec((tq, LANES), lambda hp, qi: (qi, 0)), pl.BlockSpec((None, 2, s), lambda hp, qi: (hp, 0, 0))],
        [pl.BlockSpec((tq, LANES), lambda hp, qi: (qi, hp)), pair_rows,
         pl.BlockSpec((None, None, SUBLANES, LANES), lambda hp, qi: (hp, qi, 0, 0))],
        [jax.ShapeDtypeStruct((s, d), BF16), jax.ShapeDtypeStruct((hp_n, s, LANES), F32),
         jax.ShapeDtypeStruct((hp_n, nq, SUBLANES, LANES), F32)],
        [], ("parallel", "parallel"), (qkv, qkv, qkv, kbound, cum, ck), rider)


def _fox_bwd(qkv, o, do, lse, visited, cum, ck, tq, tk, name, rider=None):
    _, s, d = qkv.shape
    hp_n, nq, nd = d // LANES, s // tq, tq // tk
    scale = HEAD_DIM ** -0.5

    def body(q_ref, k_ref, v_ref, o_ref, do_ref, lse_ref, nb_ref, cq_ref, ck_ref, dqkv_ref, dcq_ref, dck_ref,
             dk_acc, dv_acc):
        qi = pl.program_id(1)
        first = qi * nd - jnp.max(nb_ref[...]).astype(jnp.int32)

        @pl.when(qi == 0)
        def _():
            dk_acc[...] = jnp.zeros_like(dk_acc)
            dv_acc[...] = jnp.zeros_like(dv_acc)
            dck_ref[...] = jnp.zeros_like(dck_ref)

        low = lax.broadcasted_iota(jnp.int32, (tq, LANES), 1) < HEAD_DIM
        dov, lsev = do_ref[...], lse_ref[...]
        qpair = _pair(q_ref[...] * scale)
        dopair = _pair(dov)
        prod = dov.astype(F32) * o_ref[...].astype(F32)
        deltas = (jnp.sum(jnp.where(low, prod, 0.0), axis=1, keepdims=True),
                  jnp.sum(jnp.where(low, 0.0, prod), axis=1, keepdims=True))
        cqs = _head_columns(cq_ref[...], pl.program_id(0))
        lses = (lsev[:, 0:1], lsev[:, HEAD_DIM:HEAD_DIM + 1])

        def block(kb, carry, mask):
            dq, rowsums = carry[0], carry[1:]
            ks = pl.multiple_of(kb * tk, tk)
            kblk, vblk = k_ref[pl.ds(ks, tk), :], v_ref[pl.ds(ks, tk), :]
            kpair = _pair(kblk * scale)
            dss, probs, new_rows = [], [], []
            for h, (qh, doh) in enumerate(zip(qpair, dopair)):
                z = lax.dot_general(qh, kblk, NT, preferred_element_type=F32)
                sc = z + (cqs[h] - ck_ref[h:h + 1, pl.ds(ks, tk)])
                p = jnp.exp(sc - lses[h])
                if mask is not None:
                    p = jnp.where(mask, p, 0.0)
                dp = lax.dot_general(doh, vblk, NT, preferred_element_type=F32)
                ds = p * (dp - deltas[h])
                dck_ref[h:h + 1, pl.ds(ks, tk)] -= jnp.sum(ds, axis=0, keepdims=True)
                new_rows.append(rowsums[h] + jnp.sum(ds, axis=1, keepdims=True))
                dss.append(ds.astype(BF16))
                probs.append(p.astype(BF16))
            dq = dq + jnp.dot(jnp.concatenate(dss, axis=1), jnp.concatenate(kpair, axis=0),
                              preferred_element_type=F32)
            dk_acc[pl.ds(ks, tk), :] += lax.dot_general(jnp.concatenate(dss, axis=0), jnp.concatenate(qpair, axis=0),
                                                       TN, preferred_element_type=F32)
            dv_acc[pl.ds(ks, tk), :] += lax.dot_general(jnp.concatenate(probs, axis=0), jnp.concatenate(dopair, axis=0),
                                                       TN, preferred_element_type=F32)
            return (dq, *new_rows)

        zero1 = jnp.zeros((tq, 1), F32)
        carry = lax.fori_loop(first, qi * nd, lambda j, cr: block(j, cr, None),
                              (jnp.zeros((tq, LANES), F32), zero1, zero1))
        for i in range(nd):
            carry = block(qi * nd + i, carry, _rel_mask(tq, tk, i, False))
        dq, rs0, rs1 = carry
        dqkv_ref[0, pl.ds(pl.multiple_of(qi * tq, tq), tq), :] = dq.astype(BF16)
        dcq_ref[...] = jnp.where(low, rs0, rs1)

        @pl.when(qi == nq - 1)
        def _():
            dqkv_ref[1] = dk_acc[...].astype(BF16)
            dqkv_ref[2] = dv_acc[...].astype(BF16)

    q_spec, k_spec, v_spec = _attn_specs(s, tq)
    pair_rows = pl.BlockSpec((None, tq, LANES), lambda hp, qi: (hp, qi, 0))
    tile = pl.BlockSpec((tq, LANES), lambda hp, qi: (qi, hp))
    keys = pl.BlockSpec((None, 2, s), lambda hp, qi: (hp, 0, 0))
    return _attn_call(
        body, name, (hp_n, nq),
        [q_spec, k_spec, v_spec, tile, tile, pair_rows,
         pl.BlockSpec((None, None, SUBLANES, LANES), lambda hp, qi: (hp, qi, 0, 0)),
         pl.BlockSpec((tq, LANES), lambda hp, qi: (qi, 0)), keys],
        [pl.BlockSpec((3, s, LANES), lambda hp, qi: (0, 0, hp)), pair_rows, keys],
        [jax.ShapeDtypeStruct((3, s, d), BF16), jax.ShapeDtypeStruct((hp_n, s, LANES), F32),
         jax.ShapeDtypeStruct((hp_n, 2, s), F32)],
        [pltpu.VMEM((s, LANES), F32), pltpu.VMEM((s, LANES), F32)],
        ("parallel", "arbitrary"), (qkv, qkv, qkv, o, do, lse, visited, cum, ck), rider)


def _loss_head(y, target, name):
    s, d = y.shape
    tm = _pick(s, 512, SUBLANES)

    def body(y_ref, t_ref, dy_ref, sq_ref):
        i = pl.program_id(0)
        diff = y_ref[...] - t_ref[...]
        dy_ref[...] = diff / d

        @pl.when(i == 0)
        def _():
            sq_ref[...] = jnp.zeros_like(sq_ref)

        sq_ref[...] += jnp.sum(diff * diff, axis=0, keepdims=True)

    row = pl.BlockSpec((tm, d), lambda i: (i, 0))
    vec = pl.BlockSpec((1, d), lambda i: (0, 0))
    return pl.pallas_call(body, name=name, grid=(s // tm,), in_specs=[row, row], out_specs=[row, vec],
                          out_shape=[jax.ShapeDtypeStruct((s, d), F32), jax.ShapeDtypeStruct((1, d), F32)],
                          compiler_params=_params("arbitrary"))(y, target)


def _mod_fwd(c_all, w_mod, b_mod_cols, name):
    nl, d, cols = w_mod.shape
    nb = c_all.shape[0]

    def body(c_ref, w_ref, b_ref, o_ref):
        cv = c_ref[...]
        act = (cv * jax.nn.sigmoid(cv)).astype(BF16)
        o_ref[...] = jnp.dot(act, w_ref[...].astype(BF16), preferred_element_type=F32) + b_ref[...]

    return pl.pallas_call(
        body, name=name, grid=(nl,),
        in_specs=[pl.BlockSpec((nb, d), lambda l: (0, 0)), pl.BlockSpec((None, d, cols), lambda l: (l, 0, 0)),
                  pl.BlockSpec((None, 1, cols), lambda l: (l, 0, 0))],
        out_specs=pl.BlockSpec((None, nb, cols), lambda l: (l, 0, 0)),
        out_shape=jax.ShapeDtypeStruct((nl, nb, cols), F32), compiler_params=_params("parallel"))(c_all, w_mod, b_mod_cols)


def _wmod_grad(c_all_t, dmod, name):
    d, nb = c_all_t.shape
    _, nl, cols = dmod.shape

    def body(c_ref, dm_ref, o_ref):
        cv = c_ref[...]
        act = cv * jax.nn.sigmoid(cv)
        tot = act[:, 0:1] * dm_ref[0]
        for b in range(1, nb):
            tot = tot + act[:, b:b + 1] * dm_ref[b]
        o_ref[...] = tot

    return pl.pallas_call(
        body, name=name, grid=(nl,),
        in_specs=[pl.BlockSpec((d, nb), lambda l: (0, 0)), pl.BlockSpec((nb, None, 1, cols), lambda l: (0, l, 0, 0))],
        out_specs=pl.BlockSpec((None, d, cols), lambda l: (l, 0, 0)),
        out_shape=jax.ShapeDtypeStruct((nl, d, cols), F32), compiler_params=_params("parallel"))(
            c_all_t, dmod.reshape(nb, nl, 1, cols))


def _adamw(recv, w, m, v, name, *, tr=256, tc=512):
    nq, nl, rp, cp = recv.shape
    _, r, c = w.shape
    br = _pick(r, tr, SUBLANES) if rp == r else r
    bc = _pick(c, tc, LANES) if cp == c else c
    rbr = br if rp == r else rp
    rbc = bc if cp == c else cp

    def body(rv_ref, w_ref, m_ref, v_ref, g_ref, d_ref, nm_ref, nv_ref):
        g = rv_ref[0, :br, :bc].astype(F32)
        for qd in range(1, nq):
            g = g + rv_ref[qd, :br, :bc].astype(F32)
        m_new = ADAM_B1 * m_ref[...] + (1.0 - ADAM_B1) * g
        v_new = ADAM_B2 * v_ref[...] + (1.0 - ADAM_B2) * jnp.square(g)
        m_hat = m_new / (1.0 - ADAM_B1 ** ADAM_STEP)
        v_hat = v_new / (1.0 - ADAM_B2 ** ADAM_STEP)
        g_ref[...] = g
        d_ref[...] = -ADAM_LR * (m_hat / (jnp.sqrt(v_hat) + ADAM_EPS) + ADAM_WD * w_ref[...])
        nm_ref[...] = m_new
        nv_ref[...] = v_new

    blk = pl.BlockSpec((None, br, bc), lambda l, i, j: (l, i, j))
    rblk = pl.BlockSpec((nq, None, rbr, rbc), lambda l, i, j: (0, l, i, j))
    shape = jax.ShapeDtypeStruct(w.shape, F32)
    return pl.pallas_call(body, name=name, grid=(nl, r // br, c // bc), in_specs=[rblk, blk, blk, blk],
                          out_specs=[blk] * 4, out_shape=[shape] * 4,
                          compiler_params=_params("parallel", "parallel", "parallel"))(recv, w, m, v)


def _mesh_pos():
    return lax.axis_index("x"), lax.axis_index("y"), lax.axis_index("c")


def _allgather_small(block, name):
    m_per, n = block.shape

    def body(x_ref, out_ref, send_sems, recv_sems, local_sem):
        x, y, c = _mesh_pos()
        me, sibling = (x, y, c), (x, y, 1 - c)
        chips = [(1 - x, y), (x, 1 - y), (1 - x, 1 - y)]

        def rows(px, py, pc):
            return out_ref.at[pl.ds((4 * px + 2 * py + pc) * m_per, m_per), :]

        def copy(k, blk, to, src=None):
            return pltpu.make_async_remote_copy(
                src_ref=rows(*blk) if src is None else src, dst_ref=rows(*blk),
                send_sem=send_sems.at[k], recv_sem=recv_sems.at[k], device_id=to, device_id_type=MESH)

        mine = pltpu.make_async_copy(x_ref, rows(*me), local_sem)
        mine.start()
        first = [copy(0, me, sibling, src=x_ref)]
        first += [copy(1 + j, me, (*chip, c), src=x_ref) for j, chip in enumerate(chips)]
        for cp in first:
            cp.start()
        passed = [copy(4 + j, (*chip, c), sibling) for j, chip in enumerate(chips)]
        for j, chip in enumerate(chips):
            copy(1 + j, (*chip, c), me).wait_recv()
            passed[j].start()
        copy(0, sibling, me).wait_recv()
        for j, chip in enumerate(chips):
            copy(4 + j, (*chip, 1 - c), me).wait_recv()
        for cp in first + passed:
            cp.wait_send()
        mine.wait()

    return pl.pallas_call(
        body, name=name, out_shape=jax.ShapeDtypeStruct((N_DEV * m_per, n), block.dtype),
        in_specs=[pl.BlockSpec(memory_space=pltpu.VMEM)], out_specs=pl.BlockSpec(memory_space=pltpu.VMEM),
        scratch_shapes=[pltpu.SemaphoreType.DMA((7,)), pltpu.SemaphoreType.DMA((7,)), pltpu.SemaphoreType.DMA],
        compiler_params=pltpu.CompilerParams(vmem_limit_bytes=VMEM_LIMIT_BYTES))(block)


def _exchange(srcs, dsts, jobs, name):
    n_src, n_job, n_dst = len(srcs), len(jobs), len(dsts)

    def body(*refs):
        src_refs, dst_refs = refs[:n_src], refs[n_src + n_dst:n_src + 2 * n_dst]
        send_sems, recv_sems, local_sems = refs[n_src + 2 * n_dst:]
        x, y, c = _mesh_pos()
        me = 4 * x + 2 * y + c
        pending = []
        for t, (si, di, src_slice, dst_slice) in enumerate(jobs):
            src, dst = src_refs[si], dst_refs[di]
            lc = pltpu.make_async_copy(src_slice(src, me), dst_slice(dst, me), local_sems.at[t])
            lc.start()
            pending.append(lc)
            for dd in range(1, N_DEV):
                px = 1 - x if dd & 4 else x
                py = 1 - y if dd & 2 else y
                pc = 1 - c if dd & 1 else c
                cp = pltpu.make_async_remote_copy(
                    src_ref=src_slice(src, 4 * px + 2 * py + pc), dst_ref=dst_slice(dst, me),
                    send_sem=send_sems.at[t, dd - 1], recv_sem=recv_sems.at[t, dd - 1],
                    device_id=(px, py, pc), device_id_type=MESH)
                cp.start()
                pending.append(cp)
        for cp in pending:
            cp.wait()

    hbm = pl.BlockSpec(memory_space=pl.ANY)
    return pl.pallas_call(
        body, name=name, out_shape=[jax.ShapeDtypeStruct(a.shape, a.dtype) for a in dsts],
        in_specs=[hbm] * (n_src + n_dst), out_specs=[hbm] * n_dst,
        input_output_aliases={n_src + i: i for i in range(n_dst)},
        scratch_shapes=[pltpu.SemaphoreType.DMA((n_job, N_DEV - 1)), pltpu.SemaphoreType.DMA((n_job, N_DEV - 1)),
                        pltpu.SemaphoreType.DMA((n_job,))])(*srcs, *dsts)


def _attn_call(body, name, grid, in_specs, out_specs, out_shape, scratch_shapes, semantics, operands, rider):
    out_specs, out_shape, scratch_shapes = list(out_specs), list(out_shape), list(scratch_shapes)
    if rider is None:
        res = pl.pallas_call(body, name=name, grid=grid, in_specs=list(in_specs), out_specs=out_specs,
                             out_shape=out_shape, scratch_shapes=scratch_shapes,
                             compiler_params=_params(*semantics))(*operands)
        return list(res), None
    srcs, dsts, jobs = rider
    n_in, n_out, n_scr, n_src, n_dst = len(operands), len(out_shape), len(scratch_shapes), len(srcs), len(dsts)

    def carrying(*refs):
        src_refs = refs[n_in:n_in + n_src]
        at = n_in + n_src + n_dst
        dst_refs = refs[at + n_out:at + n_out + n_dst]
        scratch = refs[at + n_out + n_dst:at + n_out + n_dst + n_scr]
        send_sems, recv_sems = refs[-2], refs[-1]
        ids = [pl.program_id(a) for a in range(len(grid))]
        first = functools.reduce(jnp.logical_and, [i == 0 for i in ids])
        last = functools.reduce(jnp.logical_and, [i == g - 1 for i, g in zip(ids, grid)])

        @pl.when(first)
        def _():
            local, remote = _exchange_copies(jobs, src_refs, dst_refs, send_sems, recv_sems, True)
            for cp in remote + local:
                cp.start()

        body(*refs[:n_in], *refs[at:at + n_out], *scratch)

        @pl.when(last)
        def _():
            local, remote = _exchange_copies(jobs, src_refs, dst_refs, send_sems, recv_sems, False)
            for cp in local:
                cp.wait()
            for cp in remote:
                cp.wait_send()
                cp.wait_recv()

    hbm = pl.BlockSpec(memory_space=pl.ANY)
    sems = pltpu.SemaphoreType.DMA((len(jobs) * N_DEV,))
    res = pl.pallas_call(
        carrying, name=name, grid=grid, in_specs=list(in_specs) + [hbm] * (n_src + n_dst),
        out_specs=out_specs + [hbm] * n_dst,
        out_shape=out_shape + [jax.ShapeDtypeStruct(a.shape, a.dtype) for a in dsts],
        input_output_aliases={n_in + n_src + i: n_out + i for i in range(n_dst)},
        scratch_shapes=scratch_shapes + [sems, sems],
        compiler_params=_params(*["arbitrary"] * len(grid)))(*operands, *srcs, *dsts)
    return list(res[:n_out]), list(res[n_out:])


def _peer(x, y, c, dd):
    return (1 - x if dd & 4 else x, 1 - y if dd & 2 else y, 1 - c if dd & 1 else c)


def _exchange_copies(jobs, src_refs, dst_refs, send_sems, recv_sems, sending):
    x, y, c = _mesh_pos()
    me = 4 * x + 2 * y + c
    local, remote = [], []
    for t, (si, di, src_slice, dst_slice) in enumerate(jobs):
        local.append(pltpu.make_async_copy(src_slice(src_refs[si], me), dst_slice(dst_refs[di], me),
                                           send_sems.at[t * N_DEV]))
        for dd in range(1, N_DEV):
            px, py, pc = _peer(x, y, c, dd)
            p = 4 * px + 2 * py + pc
            remote.append(pltpu.make_async_remote_copy(
                src_ref=src_slice(src_refs[si], p), dst_ref=dst_slice(dst_refs[di], me if sending else p),
                send_sem=send_sems.at[t * N_DEV + dd], recv_sem=recv_sems.at[t * N_DEV + dd],
                device_id=(px, py, pc), device_id_type=MESH))
    return local, remote


def _whole(ref, p):
    return ref


def _layer_of(layer, inner):
    return lambda ref, p: inner(ref.at[layer], p)


def _cols_of(width):
    def take(ref, p):
        lead = (slice(None),) * (len(ref.shape) - 1)
        return ref.at[lead + (pl.ds(pl.multiple_of(p * width, LANES), width),)]
    return take


def _rows_of(height):
    def take(ref, p):
        lead = (slice(None),) * (len(ref.shape) - 2)
        return ref.at[lead + (pl.ds(pl.multiple_of(p * height, SUBLANES), height), slice(None))]
    return take


def _slot(layer=None):
    if layer is None:
        return lambda ref, p: ref.at[p]
    return lambda ref, p: ref.at[p, layer]


def _local_step(x0, target, mod, g_mix_pre, g_mix_post, g_ffn_pre, g_ffn_post, wqkv, wo, wg, wu, wd,
                wfg, bfg, wconv, bconv, tiles, comm=None):
    s, d = x0.shape
    nl = mod.shape[0]
    n_heads = d // HEAD_DIM
    hp_n = d // LANES
    assert tiles["sb_fwd"] == tiles["sb_bwd"] and tiles["fox_fwd"] == tiles["fox_bwd"]
    vec = lambda a: a.reshape(1, -1)
    saved = []
    xcur = x0
    for l in range(nl):
        if comm is not None:
            wqkv, wo, wg, wu, wd = comm["weights"]()
        sh_a, sc_a, gt_a, sh_f, sc_f, gt_f = (vec(mod[l, i * d:(i + 1) * d]) for i in range(6))
        fox = l % 2 == 1
        if l == 0:
            h1 = _norm_mod_fwd(xcur, vec(g_mix_pre[l]), sh_a, sc_a, f"norm_mix_fwd_{l}")
        qkv = _mm_nn(h1, wqkv, l, BF16, f"qkv_fwd_{l}", split_out=d)
        rider = None if comm is None else comm["fwd_rider"](l)
        if fox:
            j = l // 2
            fl = _mm_nn(h1, wfg, j, F32, f"fgate_fwd_{l}")
            cum = _fox_prep(fl, bfg[j], f"fox_prep_{l}")
            ck = cum[:, :n_heads].T.reshape(hp_n, 2, s)
            (o, *stat), moved = _fox_fwd(qkv, _key_bound(qkv, f"key_bound_{l}"), cum, ck, *tiles["fox_fwd"],
                                         f"fox_fwd_{l}", rider)
            extra = (fl, cum, ck)
        else:
            (o, *stat), moved = _sb_fwd(qkv, _key_bound(qkv, f"key_bound_{l}"), *tiles["sb_fwd"], f"sb_fwd_{l}", rider)
            extra = None
        if moved is not None:
            comm["done"](moved)
            wqkv, wo, wg, wu, wd = comm["weights"]()
        u = _mm_nn(o, wo, l, F32, f"attn_out_fwd_{l}")
        x2, h2 = _post_norm_fwd(xcur, u, vec(g_mix_post[l]), gt_a, vec(g_ffn_pre[l]), sh_f, sc_f, f"post_mix_fwd_{l}")
        gp = _mm_nn(h2, wg, l, BF16, f"ffn_gate_fwd_{l}")
        up = _mm_nn(h2, wu, l, BF16, f"ffn_up_fwd_{l}")
        act = _conv_act_fwd(gp, up, wconv[l], bconv[l], f"conv_act_fwd_{l}")
        yv = _mm_nn(act, wd, l, F32, f"ffn_down_fwd_{l}")
        saved_h1 = h1
        if l + 1 < nl:
            sh_n, sc_n = (vec(mod[l + 1, i * d:(i + 1) * d]) for i in (0, 1))
            x3, h1 = _post_norm_fwd(x2, yv, vec(g_ffn_post[l]), gt_f, vec(g_mix_pre[l + 1]), sh_n, sc_n,
                                    f"post_ffn_fwd_{l}")
        else:
            x3 = _post_fwd(x2, yv, vec(g_ffn_post[l]), gt_f, f"post_ffn_fwd_{l}")
        saved.append((xcur, saved_h1, qkv, o, stat, extra, u, x2, h2, gp, up, act, yv))
        xcur = x3

    dx, sq = _loss_head(xcur, target, "loss_head")
    small, big = [None] * nl, [None] * nl
    for l in reversed(range(nl)):
        xin, h1, qkv, o, stat, extra, u, x2, h2, gp, up, act, yv = saved[l]
        sc_a, gt_a, sc_f, gt_f = (vec(mod[l, i * d:(i + 1) * d]) for i in (1, 2, 4, 5))
        fox = l % 2 == 1
        dy, dgt_f, dg4 = _post_bwd(dx, yv, vec(g_ffn_post[l]), gt_f, f"post_ffn_bwd_{l}")
        dact = _mm_nt(dy, wd, l, BF16, f"ffn_down_dx_{l}")
        dwd = _mm_tn(act, dy, BF16, f"ffn_down_dw_{l}")
        dgp, dup, dwc, dbc = _conv_act_bwd(dact, gp, up, wconv[l], bconv[l], f"conv_act_bwd_{l}")
        dh2 = _mm_nt(dgp, wg, l, F32, f"ffn_gate_dx_{l}")
        dh2 = _mm_nt(dup, wu, l, F32, f"ffn_up_dx_{l}", add=dh2)
        dwg = _mm_tn(h2, dgp, BF16, f"ffn_gate_dw_{l}")
        dwu = _mm_tn(h2, dup, BF16, f"ffn_up_dw_{l}")
        dx2, dsh_f, dsc_f, dg3 = _norm_mod_bwd(dh2, x2, vec(g_ffn_pre[l]), sc_f, dx, f"norm_ffn_bwd_{l}")
        du, dgt_a, dg2 = _post_bwd(dx2, u, vec(g_mix_post[l]), gt_a, f"post_mix_bwd_{l}")
        do = _mm_nt(du, wo, l, BF16, f"attn_out_dx_{l}")
        dwo = _mm_tn(o, du, BF16, f"attn_out_dw_{l}")
        rider = None
        if comm is not None:
            comm["grads"](l, dict(gate=dwg, up=dwu, down=dwd))
            rider = comm["bwd_rider"]()
        if fox:
            j = l // 2
            fl, cum, ck = extra
            (dqkv, dcq, dck), moved = _fox_bwd(qkv, o, do, *stat, cum, ck, *tiles["fox_bwd"], f"fox_bwd_{l}", rider)
            dcq = jnp.max(dcq.reshape(hp_n, s, 2, HEAD_DIM), axis=3)
            dcum = dcq.transpose(1, 0, 2).reshape(s, n_heads) + dck.reshape(n_heads, s).T
            dcum = jnp.pad(dcum, ((0, 0), (0, LANES - n_heads)))
            dfl, dbfg = _fox_gate_bwd(dcum, fl, bfg[j], n_heads, f"fox_gate_bwd_{l}")
            dh1 = _mm_nt(dfl, wfg, j, F32, f"fgate_dx_{l}")
            dh1 = _mm_nt(dqkv, wqkv, l, F32, f"qkv_dx_{l}", add=dh1)
            dwfg = _mm_tn(h1, dfl, F32, f"fgate_dw_{l}")[:, :n_heads]
            dbfg = dbfg[0, :n_heads]
        else:
            (dqkv,), moved = _sb_bwd(qkv, *stat, do, *tiles["sb_bwd"], f"sb_bwd_{l}", rider)
            dh1 = _mm_nt(dqkv, wqkv, l, F32, f"qkv_dx_{l}")
            dwfg = dbfg = None
        if moved is not None:
            comm["done"](moved)
        dwqkv = _mm_tn(h1, dqkv, BF16, f"qkv_dw_{l}")
        dx, dsh_a, dsc_a, dg1 = _norm_mod_bwd(dh1, xin, vec(g_mix_pre[l]), sc_a, dx2, f"norm_mix_bwd_{l}")
        dmod = jnp.concatenate([dsh_a, dsc_a, dgt_a, dsh_f, dsc_f, dgt_f], axis=1)[0]
        small[l] = dict(dmod=dmod, dg1=dg1[0], dg2=dg2[0], dg3=dg3[0], dg4=dg4[0], dbc=dbc[0], dwc=dwc,
                        dbfg=dbfg, dwfg=dwfg)
        big[l] = dict(qkv=dwqkv, o=dwo, gate=dwg, up=dwu, down=dwd)
        if comm is not None:
            comm["grads"](l, dict(qkv=dwqkv, o=dwo))
    return sq, dx, small, big


def _rows128(a, rows):
    flat = a.reshape(-1)
    return jnp.pad(flat, (0, rows * LANES - flat.shape[0])).reshape(rows, LANES)


def _ceil8(n_elems):
    rows = -(-n_elems // LANES)
    return -(-rows // SUBLANES) * SUBLANES


def kernel(x, c, w_mod, b_mod, g_mix_pre, g_mix_post, w_qkv, w_o, w_fg, b_fg, g_ffn_pre, g_ffn_post, w_ffn_gate, w_ffn_up, w_conv, b_conv, w_ffn_down, loss_target, m_w_mod, m_b_mod, m_g_mix_pre, m_g_mix_post, m_w_qkv, m_w_o, m_w_fg, m_b_fg, m_g_ffn_pre, m_g_ffn_post, m_w_ffn_gate, m_w_ffn_up, m_w_conv, m_b_conv, m_w_ffn_down, v_w_mod, v_b_mod, v_g_mix_pre, v_g_mix_post, v_w_qkv, v_w_o, v_w_fg, v_b_fg, v_g_ffn_pre, v_g_ffn_post, v_w_ffn_gate, v_w_ffn_up, v_w_conv, v_b_conv, v_w_ffn_down):
    _, s, d = x.shape
    nl = w_qkv.shape[0]
    nf = w_fg.shape[0]
    n_heads = w_fg.shape[2]
    fs = w_ffn_gate.shape[2]
    fp = -(-fs // LANES) * LANES
    f_full, f_pad = N_DEV * fs, N_DEV * fp
    mod_cols = w_mod.shape[2]
    qs, orows = w_qkv.shape[2], w_o.shape[1]
    xi, yi, ci = _mesh_pos()
    me = 4 * xi + 2 * yi + ci

    c_rows = d // LANES
    c_all = _allgather_small(jnp.pad(c.reshape(1, d), ((0, SUBLANES - 1), (0, 0))).reshape(SUBLANES * c_rows, LANES),
                             "gather_cond")
    c_all = c_all.reshape(N_DEV, SUBLANES, d)[:, 0, :]
    b_mod_cols = lax.dynamic_slice_in_dim(b_mod, me * mod_cols, mod_cols, axis=1).reshape(nl, 1, mod_cols)
    mod_part = _mod_fwd(c_all, w_mod, b_mod_cols, "mod_fwd")

    conv_pad = jnp.pad(w_conv, ((0, 0), (0, 0), (0, fp - fs)))
    r_mod, r_conv, r_fg = _ceil8(mod_part.size), _ceil8(conv_pad.size), _ceil8(w_fg.size)
    payload = jnp.concatenate([_rows128(mod_part, r_mod), _rows128(conv_pad, r_conv), _rows128(w_fg, r_fg)], axis=0)
    got = _allgather_small(payload, "gather_small_weights").reshape(N_DEV, r_mod + r_conv + r_fg, LANES)
    mod_g = got[:, :r_mod].reshape(N_DEV, -1)[:, :mod_part.size].reshape(N_DEV, nl, N_DEV, mod_cols)
    mod = lax.dynamic_index_in_dim(mod_g, me, axis=2, keepdims=False).transpose(1, 0, 2).reshape(nl, N_DEV * mod_cols)
    conv_g = got[:, r_mod:r_mod + r_conv].reshape(N_DEV, -1)[:, :conv_pad.size].reshape(N_DEV, nl, 3, fp)
    wconv_full = conv_g.transpose(1, 2, 0, 3).reshape(nl, 3, f_pad)
    fg_g = got[:, r_mod + r_conv:].reshape(N_DEV, -1)[:, :w_fg.size].reshape(N_DEV, nf, orows, n_heads)
    wfg_full = fg_g.transpose(1, 0, 2, 3).reshape(nf, d, n_heads)
    wfg_full = jnp.pad(wfg_full, ((0, 0), (0, 0), (0, LANES - n_heads))).astype(BF16)
    bfg_full = jnp.pad(b_fg, ((0, 0), (0, LANES - n_heads))).reshape(nf, 1, LANES)
    bconv_full = jnp.pad(b_conv.reshape(nl, N_DEV, fs), ((0, 0), (0, 0), (0, fp - fs))).reshape(nl, 1, f_pad)

    gate_sh = jnp.pad(w_ffn_gate, ((0, 0), (0, 0), (0, fp - fs))).astype(BF16)
    up_sh = jnp.pad(w_ffn_up, ((0, 0), (0, 0), (0, fp - fs))).astype(BF16)
    down_sh = jnp.pad(w_ffn_down, ((0, 0), (0, fp - fs), (0, 0))).astype(BF16)
    shards = [w_qkv.astype(BF16), w_o.astype(BF16), gate_sh, up_sh, down_sh]
    full_shapes = [jax.ShapeDtypeStruct((nl, d, N_DEV * qs), BF16), jax.ShapeDtypeStruct((nl, d, d), BF16),
                   jax.ShapeDtypeStruct((nl, d, f_pad), BF16), jax.ShapeDtypeStruct((nl, d, f_pad), BF16),
                   jax.ShapeDtypeStruct((nl, f_pad, d), BF16)]
    place = [_cols_of(qs), _rows_of(orows), _cols_of(fp), _cols_of(fp), _rows_of(fp)]
    mixer_w, ffn_w = (0, 1), (2, 3, 4)

    def gather_jobs(l, which):
        return [(i, i, _layer_of(l, _whole), _layer_of(l, place[i])) for i in which]

    order = ["qkv", "o", "gate", "up", "down"]
    send = {"qkv": _cols_of(qs), "o": _rows_of(orows), "gate": _cols_of(fp), "up": _cols_of(fp), "down": _rows_of(fp)}
    recv_shapes = [(N_DEV, nl, d, qs), (N_DEV, nl, orows, d), (N_DEV, nl, d, fp), (N_DEV, nl, d, fp), (N_DEV, nl, fp, d)]
    state = {"full": _exchange(shards, [lax.empty(sh.shape, sh.dtype) for sh in full_shapes],
                               gather_jobs(0, mixer_w), "gather_weights_0"),
             "recv": [lax.empty(sh, BF16) for sh in recv_shapes], "pending": [], "moving": None}

    def mixer_host(t):
        fox_before = [l for l in range(t) if l % 2 == 1]
        return fox_before[-1] if fox_before else t - 1

    def fwd_rider(l):
        state["moving"] = "full"
        jobs = gather_jobs(l, ffn_w)
        for t in range(1, nl):
            if mixer_host(t) == l:
                jobs += gather_jobs(t, mixer_w)
        return shards, state["full"], jobs

    def bwd_rider():
        waiting, state["pending"], state["moving"] = state["pending"], [], "recv"
        if not waiting:
            return None
        jobs = [(k, order.index(nm), send[nm], _slot(l)) for k, (l, nm, _) in enumerate(waiting)]
        return [g for _, _, g in waiting], state["recv"], jobs

    def done(dsts):
        state[state["moving"]] = dsts

    def grads(l, new):
        state["pending"] += [(l, nm, new[nm]) for nm in order if nm in new]

    comm = dict(weights=lambda: state["full"], fwd_rider=fwd_rider, bwd_rider=bwd_rider, done=done, grads=grads)
    sq, dx, small, big = _local_step(x[0], loss_target[0], mod, g_mix_pre, g_mix_post, g_ffn_pre, g_ffn_post,
                                     None, None, None, None, None, wfg_full, bfg_full, wconv_full, bconv_full,
                                     ATTN_TILES, comm)
    loss = lax.psum(0.5 * jnp.sum(sq) / d, ("x", "y", "c"))
    recv = _exchange(*bwd_rider(), "scatter_grads_last")
    upd = {}
    for nm, rv, wt, mt, vt in zip(order, recv, [w_qkv, w_o, w_ffn_gate, w_ffn_up, w_ffn_down],
                                  [m_w_qkv, m_w_o, m_w_ffn_gate, m_w_ffn_up, m_w_ffn_down],
                                  [v_w_qkv, v_w_o, v_w_ffn_gate, v_w_ffn_up, v_w_ffn_down]):
        upd[nm] = _adamw(rv, wt, mt, vt, f"adamw_{nm}")

    stack = lambda key: jnp.stack([small[l][key] for l in range(nl)])
    dmod = stack("dmod")
    dgs = [stack(k) for k in ("dg1", "dg2", "dg3", "dg4")]
    dbc = stack("dbc").reshape(nl, N_DEV, fp)[:, :, :fs].reshape(nl, f_full)
    dbfg = jnp.stack([small[l]["dbfg"] for l in range(nl) if l % 2 == 1])
    dwc = stack("dwc").reshape(nl, 3, N_DEV, fp)[:, :, :, :fs].reshape(nl, 3, f_full)
    dwfg = jnp.stack([small[l]["dwfg"] for l in range(nl) if l % 2 == 1])
    rep_parts = [dmod] + dgs + [dbc, dbfg]
    rep_rows = [_ceil8(p.size) for p in rep_parts]
    r_rep, r_wc, r_wfg = sum(rep_rows), _ceil8(dwc.size), _ceil8(dwfg.size)
    payload = jnp.concatenate([_rows128(p, r) for p, r in zip(rep_parts, rep_rows)]
                              + [_rows128(dwc, r_wc), _rows128(dwfg, r_wfg)], axis=0)
    gsm = _allgather_small(payload, "gather_small_grads").reshape(N_DEV, 1, r_rep + r_wc + r_wfg, LANES)

    def pack(parts):
        return jnp.concatenate([_rows128(p, r) for p, r in zip(parts, rep_rows)], axis=0).reshape(1, r_rep, LANES)

    rep_w = [b_mod, g_mix_pre, g_mix_post, g_ffn_pre, g_ffn_post, b_conv, b_fg]
    rep_m = [m_b_mod, m_g_mix_pre, m_g_mix_post, m_g_ffn_pre, m_g_ffn_post, m_b_conv, m_b_fg]
    rep_v = [v_b_mod, v_g_mix_pre, v_g_mix_post, v_g_ffn_pre, v_g_ffn_post, v_b_conv, v_b_fg]
    rep_out = _adamw(gsm[:, :, :r_rep], pack(rep_w), pack(rep_m), pack(rep_v), "adamw_replicated", tr=r_rep, tc=LANES)

    def unpack(packed):
        outs, at = [], 0
        for p, r in zip(rep_w, rep_rows):
            outs.append(packed[0, at:at + r].reshape(-1)[:p.size].reshape(p.shape))
            at += r
        return outs

    rep_g, rep_d, rep_nm, rep_nv = (unpack(a) for a in rep_out)

    wc_all = gsm[:, 0, r_rep:r_rep + r_wc].reshape(N_DEV, -1)[:, :dwc.size].reshape(N_DEV, 1, nl * 3, f_full)
    wc_mine = lax.dynamic_slice_in_dim(wc_all, me * fs, fs, axis=3)
    wc_out = _adamw(wc_mine, w_conv.reshape(1, nl * 3, fs), m_w_conv.reshape(1, nl * 3, fs),
                    v_w_conv.reshape(1, nl * 3, fs), "adamw_conv", tr=nl * 3, tc=fs)
    wc_out = [a.reshape(nl, 3, fs) for a in wc_out]
    wfg_all = gsm[:, 0, r_rep + r_wc:].reshape(N_DEV, -1)[:, :dwfg.size].reshape(N_DEV, nf, d, n_heads)
    wfg_mine = lax.dynamic_slice_in_dim(wfg_all, me * orows, orows, axis=2)
    wfg_out = _adamw(wfg_mine, w_fg, m_w_fg, v_w_fg, "adamw_fgate", tr=orows, tc=n_heads)

    dmod_all = gsm[:, 0, :rep_rows[0]].reshape(N_DEV, -1)[:, :dmod.size].reshape(N_DEV, nl, N_DEV * mod_cols)
    dmod_mine = lax.dynamic_slice_in_dim(dmod_all, me * mod_cols, mod_cols, axis=2)
    gwmod = _wmod_grad(c_all.T, dmod_mine, "wmod_grad")
    wmod_out = _adamw(gwmod.reshape(1, nl, d, mod_cols), w_mod, m_w_mod, v_w_mod, "adamw_mod")

    per_weight = [wmod_out, None, None, None, upd["qkv"], upd["o"], wfg_out, None, None, None,
                  upd["gate"], upd["up"], wc_out, None, upd["down"]]
    rep_index = {1: 0, 2: 1, 3: 2, 8: 3, 9: 4, 13: 5, 7: 6}
    outs = [[], [], [], []]
    for pos, res in enumerate(per_weight):
        for kind in range(4):
            if res is None:
                outs[kind].append((rep_g, rep_d, rep_nm, rep_nv)[kind][rep_index[pos]])
            else:
                outs[kind].append(res[kind])
    return (loss, dx.reshape(1, s, d), *outs[0], *outs[1], *outs[2], *outs[3])
```

```python
import functools

import jax
import jax.numpy as jnp
from jax import lax
from jax.experimental import pallas as pl
from jax.experimental.pallas import tpu as pltpu

F32 = jnp.float32
BF16 = jnp.bfloat16

N_DEV = 8
HEAD_DIM = 64
LANES = 128
SUBLANES = 8
RMS_EPS = 1e-6
ADAM_LR = 0.001
ADAM_B1 = 0.9
ADAM_B2 = 0.999
ADAM_EPS = 1e-08
ADAM_WD = 0.01
ADAM_STEP = 10
VMEM_LIMIT_BYTES = 56 * 1024 * 1024
EXP_UNDERFLOW = -110.0
SUM_TERMS = 1
HALO = 16
MM_ROWS = 2048
ATTN_TILES = {"sb_fwd": (512, 256), "sb_bwd": (512, 256), "fox_fwd": (512, 512), "fox_bwd": (512, 512)}

NN = (((1,), (0,)), ((), ()))
NT = (((1,), (1,)), ((), ()))
TN = (((0,), (0,)), ((), ()))
MESH = pl.DeviceIdType.MESH


def _params(*sem):
    return pltpu.CompilerParams(dimension_semantics=sem, vmem_limit_bytes=VMEM_LIMIT_BYTES)


def _pick(n, pref, quantum):
    if n <= pref:
        return n
    t = (pref // quantum) * quantum
    while n % t:
        t -= quantum
    return t


def _mm(a, b, *, dims, grid, a_spec, b_spec, o_spec, out_shape, name, add=None):
    nk = grid[2]
    acc_shape = tuple(d for d in o_spec.block_shape if d is not None)
    o_dtype = out_shape.dtype

    def body(*refs):
        if add is None:
            a_ref, b_ref, o_ref, acc_ref = refs
            add_ref = None
        else:
            a_ref, b_ref, add_ref, o_ref, acc_ref = refs
        k = pl.program_id(2)
        part = lax.dot_general(a_ref[...], b_ref[...], dims, preferred_element_type=F32)

        def finish(total):
            if add_ref is not None:
                total = total + add_ref[...]
            o_ref[...] = total.astype(o_dtype)

        if nk == 1:
            finish(part)
            return

        @pl.when(k == 0)
        def _():
            acc_ref[...] = part

        @pl.when(jnp.logical_and(k > 0, k < nk - 1))
        def _():
            acc_ref[...] += part

        @pl.when(k == nk - 1)
        def _():
            finish(acc_ref[...] + part)

    operands = [a, b] if add is None else [a, b, add]
    in_specs = [a_spec, b_spec] if add is None else [a_spec, b_spec, o_spec]
    return pl.pallas_call(
        body, name=name, grid=grid, in_specs=in_specs, out_specs=o_spec, out_shape=out_shape,
        scratch_shapes=[pltpu.VMEM(acc_shape, F32)],
        compiler_params=_params("parallel", "parallel", "arbitrary"),
    )(*operands)


def _mm_rows(out_dtype, has_add):
    return MM_ROWS if (jnp.dtype(out_dtype).itemsize == 2 and not has_add) else MM_ROWS // 2


def _mm_nn(a, w, l, out_dtype, name, *, col0=0, n=None, split_out=None):
    m, kdim = a.shape
    n = w.shape[2] if n is None else n
    tm, tk = _pick(m, _mm_rows(out_dtype, False), SUBLANES), _pick(kdim, 1024, LANES)
    tn = _pick(n if split_out is None else split_out, 1024, LANES)
    jb = col0 // tn
    grid = (m // tm, n // tn, kdim // tk)
    a_spec = pl.BlockSpec((tm, tk), lambda i, j, k: (i, k))
    b_spec = pl.BlockSpec((None, tk, tn), lambda i, j, k: (l, k, j + jb))
    if split_out is None:
        o_spec = pl.BlockSpec((tm, tn), lambda i, j, k: (i, j))
        shape = jax.ShapeDtypeStruct((m, n), out_dtype)
    else:
        nj1 = split_out // tn
        o_spec = pl.BlockSpec((None, tm, tn), lambda i, j, k: (j // nj1, i, j % nj1))
        shape = jax.ShapeDtypeStruct((n // split_out, m, split_out), out_dtype)
    return _mm(a, w, dims=NN, grid=grid, a_spec=a_spec, b_spec=b_spec, o_spec=o_spec, out_shape=shape, name=name)


def _mm_nt(a, w, l, out_dtype, name, *, add=None):
    n, kdim = w.shape[1], w.shape[2]
    if a.ndim == 2:
        m = a.shape[0]
        tk = _pick(kdim, 1024, LANES)
        a_spec_of = lambda tm: pl.BlockSpec((tm, tk), lambda i, j, k: (i, k))
    else:
        m, seg = a.shape[1], a.shape[2]
        tk = _pick(seg, 1024, LANES)
        nk1 = seg // tk
        a_spec_of = lambda tm: pl.BlockSpec((None, tm, tk), lambda i, j, k: (k // nk1, i, k % nk1))
    tm, tn = _pick(m, _mm_rows(out_dtype, add is not None), SUBLANES), _pick(n, 1024, LANES)
    grid = (m // tm, n // tn, kdim // tk)
    b_spec = pl.BlockSpec((None, tn, tk), lambda i, j, k: (l, j, k))
    o_spec = pl.BlockSpec((tm, tn), lambda i, j, k: (i, j))
    return _mm(a, w, dims=NT, grid=grid, a_spec=a_spec_of(tm), b_spec=b_spec, o_spec=o_spec,
               out_shape=jax.ShapeDtypeStruct((m, n), out_dtype), name=name, add=add)


def _mm_tn(a, b, out_dtype, name):
    kdim, m = a.shape
    tk, tm = _pick(kdim, 1024, SUBLANES), _pick(m, _mm_rows(out_dtype, False), LANES)
    if b.ndim == 2:
        n = b.shape[1]
        tn = _pick(n, 1024, LANES)
        b_spec = pl.BlockSpec((tk, tn), lambda i, j, k: (k, j))
    else:
        seg = b.shape[2]
        n = b.shape[0] * seg
        tn = _pick(seg, 1024, LANES)
        nj1 = seg // tn
        b_spec = pl.BlockSpec((None, tk, tn), lambda i, j, k: (j // nj1, k, j % nj1))
    grid = (m // tm, n // tn, kdim // tk)
    a_spec = pl.BlockSpec((tk, tm), lambda i, j, k: (k, i))
    o_spec = pl.BlockSpec((tm, tn), lambda i, j, k: (i, j))
    return _mm(a, b, dims=TN, grid=grid, a_spec=a_spec, b_spec=b_spec, o_spec=o_spec,
               out_shape=jax.ShapeDtypeStruct((m, n), out_dtype), name=name)


def _rstd(x):
    return lax.rsqrt(jnp.mean(x * x, axis=-1, keepdims=True) + RMS_EPS)


def _norm_mod_fwd(x, g, shift, scale, name):
    s, d = x.shape
    tm = _pick(s, 512, SUBLANES)

    def body(x_ref, g_ref, sh_ref, sc_ref, h_ref):
        xv = x_ref[...]
        y = (xv * _rstd(xv)) * g_ref[...]
        h_ref[...] = (y * (1.0 + sc_ref[...]) + sh_ref[...]).astype(BF16)

    row = pl.BlockSpec((tm, d), lambda i: (i, 0))
    vec = pl.BlockSpec((1, d), lambda i: (0, 0))
    return pl.pallas_call(body, name=name, grid=(s // tm,), in_specs=[row, vec, vec, vec], out_specs=row,
                          out_shape=jax.ShapeDtypeStruct((s, d), BF16), compiler_params=_params("parallel"))(x, g, shift, scale)


def _norm_mod_bwd(dh, x, g, scale, dres, name):
    s, d = x.shape
    tm = _pick(s, 512, SUBLANES)

    def body(dh_ref, x_ref, g_ref, sc_ref, dres_ref, dx_ref, dsh_ref, dsc_ref, dg_ref):
        i = pl.program_id(0)
        xv, dhv, gv = x_ref[...], dh_ref[...], g_ref[...]
        r = _rstd(xv)
        xh = xv * r
        dn = dhv * (1.0 + sc_ref[...])
        gd = dn * gv
        dx_ref[...] = dres_ref[...] + r * (gd - xh * jnp.mean(gd * xh, axis=-1, keepdims=True))

        @pl.when(i == 0)
        def _():
            dsh_ref[...] = jnp.zeros_like(dsh_ref)
            dsc_ref[...] = jnp.zeros_like(dsc_ref)
            dg_ref[...] = jnp.zeros_like(dg_ref)

        dsh_ref[...] += jnp.sum(dhv, axis=0, keepdims=True)
        dsc_ref[...] += jnp.sum(dhv * (xh * gv), axis=0, keepdims=True)
        dg_ref[...] += jnp.sum(dn * xh, axis=0, keepdims=True)

    row = pl.BlockSpec((tm, d), lambda i: (i, 0))
    vec = pl.BlockSpec((1, d), lambda i: (0, 0))
    vshape = jax.ShapeDtypeStruct((1, d), F32)
    return pl.pallas_call(body, name=name, grid=(s // tm,), in_specs=[row, row, vec, vec, row],
                          out_specs=[row, vec, vec, vec],
                          out_shape=[jax.ShapeDtypeStruct((s, d), F32), vshape, vshape, vshape],
                          compiler_params=_params("arbitrary"))(dh, x, g, scale, dres)


def _post_fwd(x, u, g, gate, name):
    s, d = x.shape
    tm = _pick(s, 512, SUBLANES)

    def body(x_ref, u_ref, g_ref, gt_ref, o_ref):
        uv = u_ref[...]
        o_ref[...] = x_ref[...] + gt_ref[...] * ((uv * _rstd(uv)) * g_ref[...])

    row = pl.BlockSpec((tm, d), lambda i: (i, 0))
    vec = pl.BlockSpec((1, d), lambda i: (0, 0))
    return pl.pallas_call(body, name=name, grid=(s // tm,), in_specs=[row, row, vec, vec], out_specs=row,
                          out_shape=jax.ShapeDtypeStruct((s, d), F32), compiler_params=_params("parallel"))(x, u, g, gate)


def _norm_post_bwd(dh, x, g, scale, dres, u, g_post, gate, name):
    s, d = x.shape
    tm = _pick(s, 512, SUBLANES)

    def body(dh_ref, x_ref, g_ref, sc_ref, dres_ref, u_ref, gp_ref, gt_ref,
             dx_ref, dsh_ref, dsc_ref, dg_ref, du_ref, dgt_ref, dgp_ref):
        i = pl.program_id(0)
        xv, dhv, gv = x_ref[...], dh_ref[...], g_ref[...]
        r = _rstd(xv)
        xh = xv * r
        dn = dhv * (1.0 + sc_ref[...])
        gd = dn * gv
        dxv = dres_ref[...] + r * (gd - xh * jnp.mean(gd * xh, axis=-1, keepdims=True))
        dx_ref[...] = dxv
        uv, gpv = u_ref[...], gp_ref[...]
        ru = _rstd(uv)
        uh = uv * ru
        dnu = dxv * gt_ref[...]
        gdu = dnu * gpv
        du_ref[...] = (ru * (gdu - uh * jnp.mean(gdu * uh, axis=-1, keepdims=True))).astype(BF16)

        @pl.when(i == 0)
        def _():
            for ref in (dsh_ref, dsc_ref, dg_ref, dgt_ref, dgp_ref):
                ref[...] = jnp.zeros_like(ref)

        dsh_ref[...] += jnp.sum(dhv, axis=0, keepdims=True)
        dsc_ref[...] += jnp.sum(dhv * (xh * gv), axis=0, keepdims=True)
        dg_ref[...] += jnp.sum(dn * xh, axis=0, keepdims=True)
        dgt_ref[...] += jnp.sum(dxv * (uh * gpv), axis=0, keepdims=True)
        dgp_ref[...] += jnp.sum(dnu * uh, axis=0, keepdims=True)

    row = pl.BlockSpec((tm, d), lambda i: (i, 0))
    vec = pl.BlockSpec((1, d), lambda i: (0, 0))
    vshape = jax.ShapeDtypeStruct((1, d), F32)
    return pl.pallas_call(body, name=name, grid=(s // tm,), in_specs=[row, row, vec, vec, row, row, vec, vec],
                          out_specs=[row, vec, vec, vec, row, vec, vec],
                          out_shape=[jax.ShapeDtypeStruct((s, d), F32), vshape, vshape, vshape,
                                     jax.ShapeDtypeStruct((s, d), BF16), vshape, vshape],
                          compiler_params=_params("arbitrary"))(dh, x, g, scale, dres, u, g_post, gate)


def _post_norm_fwd(x, u, g, gate, g_next, shift, scale, name):
    s, d = x.shape
    tm = _pick(s, 512, SUBLANES)

    def body(x_ref, u_ref, g_ref, gt_ref, gn_ref, sh_ref, sc_ref, x2_ref, h_ref):
        uv = u_ref[...]
        x2 = x_ref[...] + gt_ref[...] * ((uv * _rstd(uv)) * g_ref[...])
        x2_ref[...] = x2
        y = (x2 * _rstd(x2)) * gn_ref[...]
        h_ref[...] = (y * (1.0 + sc_ref[...]) + sh_ref[...]).astype(BF16)

    row = pl.BlockSpec((tm, d), lambda i: (i, 0))
    vec = pl.BlockSpec((1, d), lambda i: (0, 0))
    return pl.pallas_call(body, name=name, grid=(s // tm,), in_specs=[row, row, vec, vec, vec, vec, vec],
                          out_specs=[row, row],
                          out_shape=[jax.ShapeDtypeStruct((s, d), F32), jax.ShapeDtypeStruct((s, d), BF16)],
                          compiler_params=_params("parallel"))(x, u, g, gate, g_next, shift, scale)


def _post_bwd(dx, u, g, gate, name):
    s, d = u.shape
    tm = _pick(s, 512, SUBLANES)

    def body(dx_ref, u_ref, g_ref, gt_ref, du_ref, dgt_ref, dg_ref):
        i = pl.program_id(0)
        uv, dxv, gv = u_ref[...], dx_ref[...], g_ref[...]
        r = _rstd(uv)
        uh = uv * r
        dn = dxv * gt_ref[...]
        gd = dn * gv
        du_ref[...] = (r * (gd - uh * jnp.mean(gd * uh, axis=-1, keepdims=True))).astype(BF16)

        @pl.when(i == 0)
        def _():
            dgt_ref[...] = jnp.zeros_like(dgt_ref)
            dg_ref[...] = jnp.zeros_like(dg_ref)

        dgt_ref[...] += jnp.sum(dxv * (uh * gv), axis=0, keepdims=True)
        dg_ref[...] += jnp.sum(dn * uh, axis=0, keepdims=True)

    row = pl.BlockSpec((tm, d), lambda i: (i, 0))
    vec = pl.BlockSpec((1, d), lambda i: (0, 0))
    vshape = jax.ShapeDtypeStruct((1, d), F32)
    return pl.pallas_call(body, name=name, grid=(s // tm,), in_specs=[row, row, vec, vec],
                          out_specs=[row, vec, vec],
                          out_shape=[jax.ShapeDtypeStruct((s, d), BF16), vshape, vshape],
                          compiler_params=_params("arbitrary"))(dx, u, g, gate)


def _shift_rows(cur, prev8, k):
    rolled = pltpu.roll(cur, k, axis=0)
    rolled_prev = pltpu.roll(prev8, k, axis=0)
    i8 = lax.broadcasted_iota(jnp.int32, prev8.shape, 0)
    top = jnp.where(i8 < k, rolled_prev, rolled[:SUBLANES])
    return jnp.concatenate([top, rolled[SUBLANES:]], axis=0)


def _conv_pre(g, prev8, wc_ref, bc_ref):
    s1 = _shift_rows(g, prev8, 1)
    s2 = _shift_rows(g, prev8, 2)
    gc = bc_ref[...] + wc_ref[0:1, :] * s2 + wc_ref[1:2, :] * s1 + wc_ref[2:3, :] * g
    return gc, s1, s2


def _conv_act_fwd(gp, up, wc, bc, name):
    s, f = gp.shape
    tm, tf = _pick(s, 512, SUBLANES), _pick(f, 768, LANES)
    rh = tm // HALO

    def body(g_ref, gprev_ref, up_ref, wc_ref, bc_ref, a_ref):
        i = pl.program_id(1)
        prev = jnp.where(i == 0, 0.0, gprev_ref[...].astype(F32)[HALO - SUBLANES:])
        gc, _, _ = _conv_pre(g_ref[...].astype(F32), prev, wc_ref, bc_ref)
        a_ref[...] = ((gc * jax.nn.sigmoid(gc)) * up_ref[...].astype(F32)).astype(BF16)

    tile = pl.BlockSpec((tm, tf), lambda j, i: (i, j))
    prev = pl.BlockSpec((HALO, tf), lambda j, i: (jnp.maximum(i * rh - 1, 0), j))
    return pl.pallas_call(body, name=name, grid=(f // tf, s // tm),
                          in_specs=[tile, prev, tile, pl.BlockSpec((3, tf), lambda j, i: (0, j)),
                                    pl.BlockSpec((1, tf), lambda j, i: (0, j))],
                          out_specs=tile, out_shape=jax.ShapeDtypeStruct((s, f), BF16),
                          compiler_params=_params("parallel", "parallel"))(gp, gp, up, wc, bc)


def _conv_act_bwd(da, gp, up, wc, bc, name):
    s, f = gp.shape
    tm, tf = _pick(s, 512, SUBLANES), _pick(f, 768, LANES)
    rh = tm // HALO
    nrow = s // tm

    def body(da_ref, dan_ref, g_ref, gprev_ref, gn_ref, up_ref, upn_ref, wc_ref, bc_ref,
             dgp_ref, dup_ref, dwc_ref, dbc_ref):
        i = pl.program_id(1)
        last = i == nrow - 1
        head = lambda ref: ref[...].astype(F32)[:SUBLANES]
        prev = jnp.where(i == 0, 0.0, gprev_ref[...].astype(F32)[HALO - SUBLANES:])
        g_ext = jnp.concatenate([g_ref[...].astype(F32), head(gn_ref)], axis=0)
        up_ext = jnp.concatenate([up_ref[...].astype(F32), head(upn_ref)], axis=0)
        da_ext = jnp.concatenate([da_ref[...].astype(F32), jnp.where(last, 0.0, head(dan_ref))], axis=0)
        gc, s1, s2 = _conv_pre(g_ext, prev, wc_ref, bc_ref)
        sg = jax.nn.sigmoid(gc)
        dup_ref[...] = (da_ext * (gc * sg))[:tm].astype(BF16)
        dgc = da_ext * up_ext * (sg * (1.0 + gc * (1.0 - sg)))
        ext = tm + SUBLANES
        dgp = (wc_ref[2:3, :] * dgc + wc_ref[1:2, :] * pltpu.roll(dgc, ext - 1, axis=0)
               + wc_ref[0:1, :] * pltpu.roll(dgc, ext - 2, axis=0))
        dgp_ref[...] = dgp[:tm].astype(BF16)

        @pl.when(i == 0)
        def _():
            dwc_ref[...] = jnp.zeros_like(dwc_ref)
            dbc_ref[...] = jnp.zeros_like(dbc_ref)

        d0 = dgc[:tm]
        dwc_ref[0:1, :] += jnp.sum(d0 * s2[:tm], axis=0, keepdims=True)
        dwc_ref[1:2, :] += jnp.sum(d0 * s1[:tm], axis=0, keepdims=True)
        dwc_ref[2:3, :] += jnp.sum(d0 * g_ext[:tm], axis=0, keepdims=True)
        dbc_ref[...] += jnp.sum(d0, axis=0, keepdims=True)

    tile = pl.BlockSpec((tm, tf), lambda j, i: (i, j))
    prev = pl.BlockSpec((HALO, tf), lambda j, i: (jnp.maximum(i * rh - 1, 0), j))
    nxt = pl.BlockSpec((HALO, tf), lambda j, i: (jnp.minimum((i + 1) * rh, s // HALO - 1), j))
    return pl.pallas_call(body, name=name, grid=(f // tf, nrow),
                          in_specs=[tile, nxt, tile, prev, nxt, tile, nxt,
                                    pl.BlockSpec((3, tf), lambda j, i: (0, j)), pl.BlockSpec((1, tf), lambda j, i: (0, j))],
                          out_specs=[tile, tile, pl.BlockSpec((3, tf), lambda j, i: (0, j)),
                                     pl.BlockSpec((1, tf), lambda j, i: (0, j))],
                          out_shape=[jax.ShapeDtypeStruct((s, f), BF16), jax.ShapeDtypeStruct((s, f), BF16),
                                     jax.ShapeDtypeStruct((3, f), F32), jax.ShapeDtypeStruct((1, f), F32)],
                          compiler_params=_params("parallel", "arbitrary"))(da, da, gp, gp, gp, up, up, wc, bc)


def _split_bf16(v, parts):
    out, rem = [], v
    for _ in range(parts):
        t = rem.astype(BF16)
        out.append(t)
        rem = rem - t.astype(F32)
    return jnp.concatenate(out, axis=1)


def _tri(t, cmp, reps):
    row = lax.broadcasted_iota(jnp.int32, (t, t), 0)
    col = lax.broadcasted_iota(jnp.int32, (t, t), 1)
    m = cmp(row, col).astype(BF16)
    return jnp.concatenate([m] * reps, axis=0)


def _log_sigmoid(z):
    mn = jnp.minimum(z, 0.0)
    return mn - jnp.log(1.0 + jnp.exp(mn + (mn - z)))


def _log_one_minus_sigmoid(z):
    mn = jnp.minimum(z, 0.0)
    neg = mn - z
    return neg - jnp.log(1.0 + jnp.exp(mn + neg))


def _pair(xv):
    lane = lax.broadcasted_iota(jnp.int32, xv.shape, 1)
    zero = jnp.zeros_like(xv)
    return jnp.where(lane < HEAD_DIM, xv, zero), jnp.where(lane < HEAD_DIM, zero, xv)


def _attn_specs(s, tq):
    q_spec = pl.BlockSpec((None, tq, LANES), lambda hp, qi: (0, qi, hp))
    k_spec = pl.BlockSpec((None, s, LANES), lambda hp, qi: (1, 0, hp))
    v_spec = pl.BlockSpec((None, s, LANES), lambda hp, qi: (2, 0, hp))
    return q_spec, k_spec, v_spec


def _rel_mask(tq, tk, i, strict):
    row = lax.broadcasted_iota(jnp.int32, (tq, tk), 0)
    col = lax.broadcasted_iota(jnp.int32, (tq, tk), 1) + i * tk
    return col < row if strict else col <= row


def _key_bound(qkv, name):
    _, s, d = qkv.shape

    def body(k_ref, o_ref):
        sq = jnp.square(k_ref[...].astype(F32))
        low = lax.broadcasted_iota(jnp.int32, sq.shape, 1) < HEAD_DIM
        n0 = jnp.max(jnp.sum(jnp.where(low, sq, 0.0), axis=1, keepdims=True))
        n1 = jnp.max(jnp.sum(jnp.where(low, 0.0, sq), axis=1, keepdims=True))
        low8 = lax.broadcasted_iota(jnp.int32, (SUBLANES, LANES), 1) < HEAD_DIM
        o_ref[...] = jnp.sqrt(jnp.where(low8, n0, n1))

    return pl.pallas_call(body, name=name, grid=(d // LANES,),
                          in_specs=[pl.BlockSpec((None, s, LANES), lambda hp: (1, 0, hp))],
                          out_specs=pl.BlockSpec((None, SUBLANES, LANES), lambda hp: (hp, 0, 0)),
                          out_shape=jax.ShapeDtypeStruct((d // LANES, SUBLANES, LANES), F32),
                          compiler_params=_params("parallel"))(qkv)


def _sb_fwd(qkv, kbound, tq, tk, name, rider=None):
    _, s, d = qkv.shape
    hp_n, nq, nd = d // LANES, s // tq, tq // tk
    scale = HEAD_DIM ** -0.5

    def body(q_ref, k_ref, v_ref, kb_ref, o_ref, lt_ref, nb_ref):
        qi = pl.program_id(1)
        lane = lax.broadcasted_iota(jnp.int32, (tq, LANES), 1)
        onward = _tri(tk, lambda j, sidx: j >= sidx, SUM_TERMS)
        qpair = _pair(q_ref[...] * scale)
        kbv = kb_ref[...]
        z_max = [jnp.sqrt(jnp.sum(jnp.square(qh.astype(F32)), axis=1, keepdims=True)) * (1.01 * km)
                 for qh, km in zip(qpair, (kbv[0:1, 0:1], kbv[0:1, HEAD_DIM:HEAD_DIM + 1]))]

        def alive(runs):
            return jnp.logical_or(jnp.max(runs[0] + z_max[0]) >= EXP_UNDERFLOW,
                                  jnp.max(runs[1] + z_max[1]) >= EXP_UNDERFLOW)

        def block(kb, carry, mask):
            acc, runs = carry[0], carry[1:]
            ks = pl.multiple_of(kb * tk, tk)
            kblk = k_ref[pl.ds(ks, tk), :]
            vpair = _pair(v_ref[pl.ds(ks, tk), :])
            probs, new_runs = [], []
            for qh, run in zip(qpair, runs):
                z = lax.dot_general(qh, kblk, NT, preferred_element_type=F32)
                l1 = _log_one_minus_sigmoid(z)
                if mask is not None:
                    l1 = jnp.where(mask, l1, 0.0)
                c = jnp.dot(_split_bf16(l1, SUM_TERMS), onward, preferred_element_type=F32)
                a = jnp.exp(z + c + run)
                if mask is not None:
                    a = jnp.where(mask, a, 0.0)
                probs.append(a.astype(BF16))
                new_runs.append(run + jnp.sum(l1, axis=1, keepdims=True))
            acc = acc + jnp.dot(jnp.concatenate(probs, axis=1), jnp.concatenate(vpair, axis=0),
                                preferred_element_type=F32)
            return (acc, *new_runs)

        zero1 = jnp.zeros((tq, 1), F32)
        carry = (jnp.zeros((tq, LANES), F32), zero1, zero1)
        for i in reversed(range(nd)):
            carry = block(qi * nd + i, carry, _rel_mask(tq, tk, i, True))

        def step(state):
            new = block(qi * nd - 1 - state[0], state[2:], None)
            return (state[0] + 1, alive(new[1:]), *new)

        done, _, acc, run0, run1 = lax.while_loop(lambda st: jnp.logical_and(st[0] < qi * nd, st[1]), step,
                                                  (jnp.int32(0), alive(carry[1:]), *carry))
        o_ref[...] = acc.astype(BF16)
        lt_ref[...] = jnp.where(lane < HEAD_DIM, run0, run1)
        nb_ref[...] = jnp.zeros((SUBLANES, LANES), F32) + done.astype(F32)

    q_spec, k_spec, v_spec = _attn_specs(s, tq)
    return _attn_call(
        body, name, (hp_n, nq),
        [q_spec, k_spec, v_spec, pl.BlockSpec((None, SUBLANES, LANES), lambda hp, qi: (hp, 0, 0))],
        [pl.BlockSpec((tq, LANES), lambda hp, qi: (qi, hp)),
         pl.BlockSpec((None, tq, LANES), lambda hp, qi: (hp, qi, 0)),
         pl.BlockSpec((None, None, SUBLANES, LANES), lambda hp, qi: (hp, qi, 0, 0))],
        [jax.ShapeDtypeStruct((s, d), BF16), jax.ShapeDtypeStruct((hp_n, s, LANES), F32),
         jax.ShapeDtypeStruct((hp_n, nq, SUBLANES, LANES), F32)],
        [], ("parallel", "parallel"), (qkv, qkv, qkv, kbound), rider)


def _sb_bwd(qkv, ltot, visited, do, tq, tk, name, rider=None):
    _, s, d = qkv.shape
    hp_n, nq, nd = d // LANES, s // tq, tq // tk
    scale = HEAD_DIM ** -0.5

    def body(q_ref, k_ref, v_ref, lt_ref, nb_ref, do_ref, dqkv_ref, dk_acc, dv_acc):
        qi = pl.program_id(1)
        first = qi * nd - jnp.max(nb_ref[...]).astype(jnp.int32)

        @pl.when(qi == 0)
        def _():
            dk_acc[...] = jnp.zeros_like(dk_acc)
            dv_acc[...] = jnp.zeros_like(dv_acc)

        onward = _tri(tk, lambda j, sidx: j >= sidx, SUM_TERMS)
        before = _tri(tk, lambda j, sidx: j < sidx, SUM_TERMS)
        qpair = _pair(q_ref[...] * scale)
        dopair = _pair(do_ref[...])
        ltv = lt_ref[...]
        lts = (ltv[:, 0:1], ltv[:, HEAD_DIM:HEAD_DIM + 1])

        def block(kb, carry, mask):
            dq, lpres, gpres = carry[0], carry[1:3], carry[3:5]
            ks = pl.multiple_of(kb * tk, tk)
            kblk, vblk = k_ref[pl.ds(ks, tk), :], v_ref[pl.ds(ks, tk), :]
            kpair = _pair(kblk * scale)
            dzs, probs, new_l, new_g = [], [], [], []
            for qh, doh, lt, lpre, gpre in zip(qpair, dopair, lts, lpres, gpres):
                z = lax.dot_general(qh, kblk, NT, preferred_element_type=F32)
                l1 = _log_one_minus_sigmoid(z)
                if mask is not None:
                    l1 = jnp.where(mask, l1, 0.0)
                lpre = lpre + jnp.sum(l1, axis=1, keepdims=True)
                c = jnp.dot(_split_bf16(l1, SUM_TERMS), onward, preferred_element_type=F32)
                a = jnp.exp(z + c + (lt - lpre))
                if mask is not None:
                    a = jnp.where(mask, a, 0.0)
                da = lax.dot_general(doh, vblk, NT, preferred_element_type=F32)
                g = da * a
                p = jnp.dot(_split_bf16(g, SUM_TERMS), before, preferred_element_type=F32) + gpre
                dz = g - jnp.exp(z + l1) * (g + p)
                if mask is not None:
                    dz = jnp.where(mask, dz, 0.0)
                dzs.append(dz.astype(BF16))
                probs.append(a.astype(BF16))
                new_l.append(lpre)
                new_g.append(gpre + jnp.sum(g, axis=1, keepdims=True))
            dq = dq + jnp.dot(jnp.concatenate(dzs, axis=1), jnp.concatenate(kpair, axis=0),
                              preferred_element_type=F32)
            dk_acc[pl.ds(ks, tk), :] += lax.dot_general(jnp.concatenate(dzs, axis=0), jnp.concatenate(qpair, axis=0),
                                                        TN, preferred_element_type=F32)
            dv_acc[pl.ds(ks, tk), :] += lax.dot_general(jnp.concatenate(probs, axis=0), jnp.concatenate(dopair, axis=0),
                                                        TN, preferred_element_type=F32)
            return (dq, *new_l, *new_g)

        zero1 = jnp.zeros((tq, 1), F32)
        init = (jnp.zeros((tq, LANES), F32), zero1, zero1, zero1, zero1)
        carry = lax.fori_loop(first, qi * nd, lambda j, cr: block(j, cr, None), init)
        for i in range(nd):
            carry = block(qi * nd + i, carry, _rel_mask(tq, tk, i, True))
        dqkv_ref[0, pl.ds(pl.multiple_of(qi * tq, tq), tq), :] = carry[0].astype(BF16)

        @pl.when(qi == nq - 1)
        def _():
            dqkv_ref[1] = dk_acc[...].astype(BF16)
            dqkv_ref[2] = dv_acc[...].astype(BF16)

    q_spec, k_spec, v_spec = _attn_specs(s, tq)
    return _attn_call(
        body, name, (hp_n, nq),
        [q_spec, k_spec, v_spec, pl.BlockSpec((None, tq, LANES), lambda hp, qi: (hp, qi, 0)),
         pl.BlockSpec((None, None, SUBLANES, LANES), lambda hp, qi: (hp, qi, 0, 0)),
         pl.BlockSpec((tq, LANES), lambda hp, qi: (qi, hp))],
        [pl.BlockSpec((3, s, LANES), lambda hp, qi: (0, 0, hp))], [jax.ShapeDtypeStruct((3, s, d), BF16)],
        [pltpu.VMEM((s, LANES), F32), pltpu.VMEM((s, LANES), F32)],
        ("parallel", "arbitrary"), (qkv, qkv, qkv, ltot, visited, do), rider)


def _fox_prep(fl, bias, name):
    s, w = fl.shape
    tb = _pick(s, 512, SUBLANES)

    def body(fl_ref, b_ref, cum_ref, carry_ref):
        i = pl.program_id(0)

        @pl.when(i == 0)
        def _():
            carry_ref[...] = jnp.zeros_like(carry_ref)

        logf = _log_sigmoid(fl_ref[...] + b_ref[...])
        row = lax.broadcasted_iota(jnp.int32, (tb, tb), 0)
        col = lax.broadcasted_iota(jnp.int32, (tb, tb), 1)
        incl = (col <= row).astype(BF16)
        tot = carry_ref[...]
        rem = logf
        for _ in range(3):
            part = rem.astype(BF16)
            tot = tot + jnp.dot(incl, part, preferred_element_type=F32)
            rem = rem - part.astype(F32)
        cum_ref[...] = tot
        carry_ref[...] = tot[tb - 1:tb, :]

    blk = pl.BlockSpec((tb, w), lambda i: (i, 0))
    return pl.pallas_call(body, name=name, grid=(s // tb,), in_specs=[blk, pl.BlockSpec((1, w), lambda i: (0, 0))],
                          out_specs=blk, out_shape=jax.ShapeDtypeStruct((s, w), F32),
                          scratch_shapes=[pltpu.VMEM((1, w), F32)], compiler_params=_params("arbitrary"))(fl, bias)


def _fox_gate_bwd(dcum, fl, bias, n_heads, name):
    s, w = fl.shape
    tb = _pick(s, 512, SUBLANES)
    nb = s // tb

    def body(dc_ref, fl_ref, b_ref, dfl_ref, db_ref, carry_ref):
        i = pl.program_id(0)

        @pl.when(i == 0)
        def _():
            carry_ref[...] = jnp.zeros_like(carry_ref)
            db_ref[...] = jnp.zeros_like(db_ref)

        row = lax.broadcasted_iota(jnp.int32, (tb, tb), 0)
        col = lax.broadcasted_iota(jnp.int32, (tb, tb), 1)
        incl = (col >= row).astype(BF16)
        tot = jnp.broadcast_to(carry_ref[...], (tb, w))
        rem = dc_ref[...]
        for _ in range(3):
            part = rem.astype(BF16)
            tot = tot + jnp.dot(incl, part, preferred_element_type=F32)
            rem = rem - part.astype(F32)
        carry_ref[...] = tot[0:1, :]
        xg = fl_ref[...] + b_ref[...]
        e = jnp.exp(-jnp.abs(xg))
        sig_neg = jnp.where(xg >= 0.0, e, 1.0) / (1.0 + e)
        lane = lax.broadcasted_iota(jnp.int32, (tb, w), 1)
        dfl = jnp.where(lane < n_heads, tot * sig_neg, 0.0)
        dfl_ref[...] = dfl.astype(BF16)
        db_ref[...] += jnp.sum(dfl, axis=0, keepdims=True)

    blk = pl.BlockSpec((tb, w), lambda i: (nb - 1 - i, 0))
    vec = pl.BlockSpec((1, w), lambda i: (0, 0))
    return pl.pallas_call(body, name=name, grid=(nb,), in_specs=[blk, blk, vec], out_specs=[blk, vec],
                          out_shape=[jax.ShapeDtypeStruct((s, w), BF16), jax.ShapeDtypeStruct((1, w), F32)],
                          scratch_shapes=[pltpu.VMEM((1, w), F32)], compiler_params=_params("arbitrary"))(dcum, fl, bias)


def _head_columns(cum_blk, hp):
    lane = lax.broadcasted_iota(jnp.int32, cum_blk.shape, 1)
    return tuple(jnp.sum(jnp.where(lane == 2 * hp + h, cum_blk, 0.0), axis=1, keepdims=True) for h in range(2))


def _fox_fwd(qkv, kbound, cum, ck, tq, tk, name, rider=None):
    _, s, d = qkv.shape
    hp_n, nq, nd = d // LANES, s // tq, tq // tk
    scale = HEAD_DIM ** -0.5

    def body(q_ref, k_ref, v_ref, kb_ref, cq_ref, ck_ref, o_ref, lse_ref, nb_ref):
        qi = pl.program_id(1)
        low = lax.broadcasted_iota(jnp.int32, (tq, LANES), 1) < HEAD_DIM
        qpair = _pair(q_ref[...] * scale)
        cqs = _head_columns(cq_ref[...], pl.program_id(0))
        kbv = kb_ref[...]
        reach = [jnp.sqrt(jnp.sum(jnp.square(qh.astype(F32)), axis=1, keepdims=True)) * (1.01 * km) + cqh
                 for qh, km, cqh in zip(qpair, (kbv[0:1, 0:1], kbv[0:1, HEAD_DIM:HEAD_DIM + 1]), cqs)]

        def alive(stats, kb):
            ks = pl.multiple_of(kb * tk, tk)
            return functools.reduce(jnp.logical_or, [
                jnp.max(reach[h] - ck_ref[h:h + 1, pl.ds(ks, tk)][:, 0:1] - stats[2 * h]) >= EXP_UNDERFLOW
                for h in range(2)])

        def block(kb, carry, mask):
            acc, stats = carry[0], carry[1:]
            ks = pl.multiple_of(kb * tk, tk)
            kblk = k_ref[pl.ds(ks, tk), :]
            vpair = _pair(v_ref[pl.ds(ks, tk), :])
            probs, alphas, new = [], [], []
            for h, (qh, cqh) in enumerate(zip(qpair, cqs)):
                m, lsum = stats[2 * h], stats[2 * h + 1]
                z = lax.dot_general(qh, kblk, NT, preferred_element_type=F32)
                sc = z + (cqh - ck_ref[h:h + 1, pl.ds(ks, tk)])
                if mask is not None:
                    sc = jnp.where(mask, sc, -jnp.inf)
                m_new = jnp.maximum(m, jnp.max(sc, axis=1, keepdims=True))
                alpha = jnp.exp(m - m_new)
                p = jnp.exp(sc - m_new)
                new += [m_new, alpha * lsum + jnp.sum(p, axis=1, keepdims=True)]
                probs.append(p.astype(BF16))
                alphas.append(alpha)
            acc = jnp.where(low, alphas[0], alphas[1]) * acc + jnp.dot(
                jnp.concatenate(probs, axis=1), jnp.concatenate(vpair, axis=0), preferred_element_type=F32)
            return (acc, *new)

        neg = jnp.full((tq, 1), -jnp.inf, F32)
        zero1 = jnp.zeros((tq, 1), F32)
        carry = (jnp.zeros((tq, LANES), F32), neg, zero1, neg, zero1)
        for i in range(nd):
            carry = block(qi * nd + i, carry, _rel_mask(tq, tk, i, False))

        def step(state):
            kb = qi * nd - 1 - state[0]
            new = block(kb, state[2:], None)
            return (state[0] + 1, alive(new[1:], kb), *new)

        done, _, acc, m0, l0, m1, l1 = lax.while_loop(lambda st: jnp.logical_and(st[0] < qi * nd, st[1]), step,
                                                      (jnp.int32(0), alive(carry[1:], qi * nd), *carry))
        o_ref[...] = (acc / jnp.where(low, l0, l1)).astype(BF16)
        lse_ref[...] = jnp.where(low, m0 + jnp.log(l0), m1 + jnp.log(l1))
        nb_ref[...] = jnp.zeros((SUBLANES, LANES), F32) + done.astype(F32)

    q_spec, k_spec, v_spec = _attn_specs(s, tq)
    pair_rows = pl.BlockSpec((None, tq, LANES), lambda hp, qi: (hp, qi, 0))
    return _attn_call(
        body, name, (hp_n, nq),
        [q_spec, k_spec, v_spec, pl.BlockSpec((None, SUBLANES, LANES), lambda hp, qi: (hp, 0, 0)),
         pl.BlockSpec((tq, LANES), lambda hp, qi: (qi, 0)), pl.BlockSpec((None, 2, s), lambda hp, qi: (hp, 0, 0))],
        [pl.BlockSpec((tq, LANES), lambda hp, qi: (qi, hp)), pair_rows,
         pl.BlockSpec((None, None, SUBLANES, LANES), lambda hp, qi: (hp, qi, 0, 0))],
        [jax.ShapeDtypeStruct((s, d), BF16), jax.ShapeDtypeStruct((hp_n, s, LANES), F32),
         jax.ShapeDtypeStruct((hp_n, nq, SUBLANES, LANES), F32)],
        [], ("parallel", "parallel"), (qkv, qkv, qkv, kbound, cum, ck), rider)


def _fox_bwd(qkv, o, do, lse, visited, cum, ck, tq, tk, name, rider=None):
    _, s, d = qkv.shape
    hp_n, nq, nd = d // LANES, s // tq, tq // tk
    scale = HEAD_DIM ** -0.5

    def body(q_ref, k_ref, v_ref, o_ref, do_ref, lse_ref, nb_ref, cq_ref, ck_ref, dqkv_ref, dcq_ref, dck_ref,
             dk_acc, dv_acc):
        qi = pl.program_id(1)
        first = qi * nd - jnp.max(nb_ref[...]).astype(jnp.int32)

        @pl.when(qi == 0)
        def _():
            dk_acc[...] = jnp.zeros_like(dk_acc)
            dv_acc[...] = jnp.zeros_like(dv_acc)
            dck_ref[...] = jnp.zeros_like(dck_ref)

        low = lax.broadcasted_iota(jnp.int32, (tq, LANES), 1) < HEAD_DIM
        dov, lsev = do_ref[...], lse_ref[...]
        qpair = _pair(q_ref[...] * scale)
        dopair = _pair(dov)
        prod = dov.astype(F32) * o_ref[...].astype(F32)
        deltas = (jnp.sum(jnp.where(low, prod, 0.0), axis=1, keepdims=True),
                  jnp.sum(jnp.where(low, 0.0, prod), axis=1, keepdims=True))
        cqs = _head_columns(cq_ref[...], pl.program_id(0))
        lses = (lsev[:, 0:1], lsev[:, HEAD_DIM:HEAD_DIM + 1])

        def block(kb, carry, mask):
            dq, rowsums = carry[0], carry[1:]
            ks = pl.multiple_of(kb * tk, tk)
            kblk, vblk = k_ref[pl.ds(ks, tk), :], v_ref[pl.ds(ks, tk), :]
            kpair = _pair(kblk * scale)
            dss, probs, new_rows = [], [], []
            for h, (qh, doh) in enumerate(zip(qpair, dopair)):
                z = lax.dot_general(qh, kblk, NT, preferred_element_type=F32)
                sc = z + (cqs[h] - ck_ref[h:h + 1, pl.ds(ks, tk)])
                p = jnp.exp(sc - lses[h])
                if mask is not None:
                    p = jnp.where(mask, p, 0.0)
                dp = lax.dot_general(doh, vblk, NT, preferred_element_type=F32)
                ds = p * (dp - deltas[h])
                dck_ref[h:h + 1, pl.ds(ks, tk)] -= jnp.sum(ds, axis=0, keepdims=True)
                new_rows.append(rowsums[h] + jnp.sum(ds, axis=1, keepdims=True))
                dss.append(ds.astype(BF16))
                probs.append(p.astype(BF16))
            dq = dq + jnp.dot(jnp.concatenate(dss, axis=1), jnp.concatenate(kpair, axis=0),
                              preferred_element_type=F32)
            dk_acc[pl.ds(ks, tk), :] += lax.dot_general(jnp.concatenate(dss, axis=0), jnp.concatenate(qpair, axis=0),
                                                       TN, preferred_element_type=F32)
            dv_acc[pl.ds(ks, tk), :] += lax.dot_general(jnp.concatenate(probs, axis=0), jnp.concatenate(dopair, axis=0),
                                                       TN, preferred_element_type=F32)
            return (dq, *new_rows)

        zero1 = jnp.zeros((tq, 1), F32)
        carry = lax.fori_loop(first, qi * nd, lambda j, cr: block(j, cr, None),
                              (jnp.zeros((tq, LANES), F32), zero1, zero1))
        for i in range(nd):
            carry = block(qi * nd + i, carry, _rel_mask(tq, tk, i, False))
        dq, rs0, rs1 = carry
        dqkv_ref[0, pl.ds(pl.multiple_of(qi * tq, tq), tq), :] = dq.astype(BF16)
        dcq_ref[...] = jnp.where(low, rs0, rs1)

        @pl.when(qi == nq - 1)
        def _():
            dqkv_ref[1] = dk_acc[...].astype(BF16)
            dqkv_ref[2] = dv_acc[...].astype(BF16)

    q_spec, k_spec, v_spec = _attn_specs(s, tq)
    pair_rows = pl.BlockSpec((None, tq, LANES), lambda hp, qi: (hp, qi, 0))
    tile = pl.BlockSpec((tq, LANES), lambda hp, qi: (qi, hp))
    keys = pl.BlockSpec((None, 2, s), lambda hp, qi: (hp, 0, 0))
    return _attn_call(
        body, name, (hp_n, nq),
        [q_spec, k_spec, v_spec, tile, tile, pair_rows,
         pl.BlockSpec((None, None, SUBLANES, LANES), lambda hp, qi: (hp, qi, 0, 0)),
         pl.BlockSpec((tq, LANES), lambda hp, qi: (qi, 0)), keys],
        [pl.BlockSpec((3, s, LANES), lambda hp, qi: (0, 0, hp)), pair_rows, keys],
        [jax.ShapeDtypeStruct((3, s, d), BF16), jax.ShapeDtypeStruct((hp_n, s, LANES), F32),
         jax.ShapeDtypeStruct((hp_n, 2, s), F32)],
        [pltpu.VMEM((s, LANES), F32), pltpu.VMEM((s, LANES), F32)],
        ("parallel", "arbitrary"), (qkv, qkv, qkv, o, do, lse, visited, cum, ck), rider)


def _loss_head(y, target, name):
    s, d = y.shape
    tm = _pick(s, 512, SUBLANES)

    def body(y_ref, t_ref, dy_ref, sq_ref):
        i = pl.program_id(0)
        diff = y_ref[...] - t_ref[...]
        dy_ref[...] = diff / d

        @pl.when(i == 0)
        def _():
            sq_ref[...] = jnp.zeros_like(sq_ref)

        sq_ref[...] += jnp.sum(diff * diff, axis=0, keepdims=True)

    row = pl.BlockSpec((tm, d), lambda i: (i, 0))
    vec = pl.BlockSpec((1, d), lambda i: (0, 0))
    return pl.pallas_call(body, name=name, grid=(s // tm,), in_specs=[row, row], out_specs=[row, vec],
                          out_shape=[jax.ShapeDtypeStruct((s, d), F32), jax.ShapeDtypeStruct((1, d), F32)],
                          compiler_params=_params("arbitrary"))(y, target)


def _mod_fwd(c_all, w_mod, b_mod_cols, name):
    nl, d, cols = w_mod.shape
    nb = c_all.shape[0]

    def body(c_ref, w_ref, b_ref, o_ref):
        cv = c_ref[...]
        act = (cv * jax.nn.sigmoid(cv)).astype(BF16)
        o_ref[...] = jnp.dot(act, w_ref[...].astype(BF16), preferred_element_type=F32) + b_ref[...]

    return pl.pallas_call(
        body, name=name, grid=(nl,),
        in_specs=[pl.BlockSpec((nb, d), lambda l: (0, 0)), pl.BlockSpec((None, d, cols), lambda l: (l, 0, 0)),
                  pl.BlockSpec((None, 1, cols), lambda l: (l, 0, 0))],
        out_specs=pl.BlockSpec((None, nb, cols), lambda l: (l, 0, 0)),
        out_shape=jax.ShapeDtypeStruct((nl, nb, cols), F32), compiler_params=_params("parallel"))(c_all, w_mod, b_mod_cols)


def _wmod_grad(c_all_t, dmod, name):
    d, nb = c_all_t.shape
    _, nl, cols = dmod.shape

    def body(c_ref, dm_ref, o_ref):
        cv = c_ref[...]
        act = cv * jax.nn.sigmoid(cv)
        tot = act[:, 0:1] * dm_ref[0]
        for b in range(1, nb):
            tot = tot + act[:, b:b + 1] * dm_ref[b]
        o_ref[...] = tot

    return pl.pallas_call(
        body, name=name, grid=(nl,),
        in_specs=[pl.BlockSpec((d, nb), lambda l: (0, 0)), pl.BlockSpec((nb, None, 1, cols), lambda l: (0, l, 0, 0))],
        out_specs=pl.BlockSpec((None, d, cols), lambda l: (l, 0, 0)),
        out_shape=jax.ShapeDtypeStruct((nl, d, cols), F32), compiler_params=_params("parallel"))(
            c_all_t, dmod.reshape(nb, nl, 1, cols))


def _adamw(recv, w, m, v, name, *, tr=256, tc=512):
    nq, nl, rp, cp = recv.shape
    _, r, c = w.shape
    br = _pick(r, tr, SUBLANES) if rp == r else r
    bc = _pick(c, tc, LANES) if cp == c else c
    rbr = br if rp == r else rp
    rbc = bc if cp == c else cp

    def body(rv_ref, w_ref, m_ref, v_ref, g_ref, d_ref, nm_ref, nv_ref):
        g = rv_ref[0, :br, :bc].astype(F32)
        for qd in range(1, nq):
            g = g + rv_ref[qd, :br, :bc].astype(F32)
        m_new = ADAM_B1 * m_ref[...] + (1.0 - ADAM_B1) * g
        v_new = ADAM_B2 * v_ref[...] + (1.0 - ADAM_B2) * jnp.square(g)
        m_hat = m_new / (1.0 - ADAM_B1 ** ADAM_STEP)
        v_hat = v_new / (1.0 - ADAM_B2 ** ADAM_STEP)
        g_ref[...] = g
        d_ref[...] = -ADAM_LR * (m_hat / (jnp.sqrt(v_hat) + ADAM_EPS) + ADAM_WD * w_ref[...])
        nm_ref[...] = m_new
        nv_ref[...] = v_new

    blk = pl.BlockSpec((None, br, bc), lambda l, i, j: (l, i, j))
    rblk = pl.BlockSpec((nq, None, rbr, rbc), lambda l, i, j: (0, l, i, j))
    shape = jax.ShapeDtypeStruct(w.shape, F32)
    return pl.pallas_call(body, name=name, grid=(nl, r // br, c // bc), in_specs=[rblk, blk, blk, blk],
                          out_specs=[blk] * 4, out_shape=[shape] * 4,
                          compiler_params=_params("parallel", "parallel", "parallel"))(recv, w, m, v)


def _mesh_pos():
    return lax.axis_index("x"), lax.axis_index("y"), lax.axis_index("c")


def _allgather_small(block, name):
    m_per, n = block.shape

    def body(x_ref, out_ref, send_sems, recv_sems, local_sem):
        x, y, c = _mesh_pos()
        me, sibling = (x, y, c), (x, y, 1 - c)
        chips = [(1 - x, y), (x, 1 - y), (1 - x, 1 - y)]

        def rows(px, py, pc):
            return out_ref.at[pl.ds((4 * px + 2 * py + pc) * m_per, m_per), :]

        def copy(k, blk, to, src=None):
            return pltpu.make_async_remote_copy(
                src_ref=rows(*blk) if src is None else src, dst_ref=rows(*blk),
                send_sem=send_sems.at[k], recv_sem=recv_sems.at[k], device_id=to, device_id_type=MESH)

        mine = pltpu.make_async_copy(x_ref, rows(*me), local_sem)
        mine.start()
        first = [copy(0, me, sibling, src=x_ref)]
        first += [copy(1 + j, me, (*chip, c), src=x_ref) for j, chip in enumerate(chips)]
        for cp in first:
            cp.start()
        passed = [copy(4 + j, (*chip, c), sibling) for j, chip in enumerate(chips)]
        for j, chip in enumerate(chips):
            copy(1 + j, (*chip, c), me).wait_recv()
            passed[j].start()
        copy(0, sibling, me).wait_recv()
        for j, chip in enumerate(chips):
            copy(4 + j, (*chip, 1 - c), me).wait_recv()
        for cp in first + passed:
            cp.wait_send()
        mine.wait()

    return pl.pallas_call(
        body, name=name, out_shape=jax.ShapeDtypeStruct((N_DEV * m_per, n), block.dtype),
        in_specs=[pl.BlockSpec(memory_space=pltpu.VMEM)], out_specs=pl.BlockSpec(memory_space=pltpu.VMEM),
        scratch_shapes=[pltpu.SemaphoreType.DMA((7,)), pltpu.SemaphoreType.DMA((7,)), pltpu.SemaphoreType.DMA],
        compiler_params=pltpu.CompilerParams(vmem_limit_bytes=VMEM_LIMIT_BYTES))(block)


def _exchange(srcs, dsts, jobs, name):
    n_src, n_job, n_dst = len(srcs), len(jobs), len(dsts)

    def body(*refs):
        src_refs, dst_refs = refs[:n_src], refs[n_src + n_dst:n_src + 2 * n_dst]
        send_sems, recv_sems, local_sems = refs[n_src + 2 * n_dst:]
        x, y, c = _mesh_pos()
        me = 4 * x + 2 * y + c
        pending = []
        for t, (si, di, src_slice, dst_slice) in enumerate(jobs):
            src, dst = src_refs[si], dst_refs[di]
            lc = pltpu.make_async_copy(src_slice(src, me), dst_slice(dst, me), local_sems.at[t])
            lc.start()
            pending.append(lc)
            for dd in range(1, N_DEV):
                px = 1 - x if dd & 4 else x
                py = 1 - y if dd & 2 else y
                pc = 1 - c if dd & 1 else c
                cp = pltpu.make_async_remote_copy(
                    src_ref=src_slice(src, 4 * px + 2 * py + pc), dst_ref=dst_slice(dst, me),
                    send_sem=send_sems.at[t, dd - 1], recv_sem=recv_sems.at[t, dd - 1],
                    device_id=(px, py, pc), device_id_type=MESH)
                cp.start()
                pending.append(cp)
        for cp in pending:
            cp.wait()

    hbm = pl.BlockSpec(memory_space=pl.ANY)
    return pl.pallas_call(
        body, name=name, out_shape=[jax.ShapeDtypeStruct(a.shape, a.dtype) for a in dsts],
        in_specs=[hbm] * (n_src + n_dst), out_specs=[hbm] * n_dst,
        input_output_aliases={n_src + i: i for i in range(n_dst)},
        scratch_shapes=[pltpu.SemaphoreType.DMA((n_job, N_DEV - 1)), pltpu.SemaphoreType.DMA((n_job, N_DEV - 1)),
                        pltpu.SemaphoreType.DMA((n_job,))])(*srcs, *dsts)


def _attn_call(body, name, grid, in_specs, out_specs, out_shape, scratch_shapes, semantics, operands, rider):
    out_specs, out_shape, scratch_shapes = list(out_specs), list(out_shape), list(scratch_shapes)
    if rider is None:
        res = pl.pallas_call(body, name=name, grid=grid, in_specs=list(in_specs), out_specs=out_specs,
                             out_shape=out_shape, scratch_shapes=scratch_shapes,
                             compiler_params=_params(*semantics))(*operands)
        return list(res), None
    srcs, dsts, jobs = rider
    n_in, n_out, n_scr, n_src, n_dst = len(operands), len(out_shape), len(scratch_shapes), len(srcs), len(dsts)

    def carrying(*refs):
        src_refs = refs[n_in:n_in + n_src]
        at = n_in + n_src + n_dst
        dst_refs = refs[at + n_out:at + n_out + n_dst]
        scratch = refs[at + n_out + n_dst:at + n_out + n_dst + n_scr]
        send_sems, recv_sems = refs[-2], refs[-1]
        ids = [pl.program_id(a) for a in range(len(grid))]
        first = functools.reduce(jnp.logical_and, [i == 0 for i in ids])
        last = functools.reduce(jnp.logical_and, [i == g - 1 for i, g in zip(ids, grid)])

        @pl.when(first)
        def _():
            local, remote = _exchange_copies(jobs, src_refs, dst_refs, send_sems, recv_sems, True)
            for cp in remote + local:
                cp.start()

        body(*refs[:n_in], *refs[at:at + n_out], *scratch)

        @pl.when(last)
        def _():
            local, remote = _exchange_copies(jobs, src_refs, dst_refs, send_sems, recv_sems, False)
            for cp in local:
                cp.wait()
            for cp in remote:
                cp.wait_send()
                cp.wait_recv()

    hbm = pl.BlockSpec(memory_space=pl.ANY)
    sems = pltpu.SemaphoreType.DMA((len(jobs) * N_DEV,))
    res = pl.pallas_call(
        carrying, name=name, grid=grid, in_specs=list(in_specs) + [hbm] * (n_src + n_dst),
        out_specs=out_specs + [hbm] * n_dst,
        out_shape=out_shape + [jax.ShapeDtypeStruct(a.shape, a.dtype) for a in dsts],
        input_output_aliases={n_in + n_src + i: n_out + i for i in range(n_dst)},
        scratch_shapes=scratch_shapes + [sems, sems],
        compiler_params=_params(*["arbitrary"] * len(grid)))(*operands, *srcs, *dsts)
    return list(res[:n_out]), list(res[n_out:])


def _peer(x, y, c, dd):
    return (1 - x if dd & 4 else x, 1 - y if dd & 2 else y, 1 - c if dd & 1 else c)


def _exchange_copies(jobs, src_refs, dst_refs, send_sems, recv_sems, sending):
    x, y, c = _mesh_pos()
    me = 4 * x + 2 * y + c
    local, remote = [], []
    for t, (si, di, src_slice, dst_slice) in enumerate(jobs):
        local.append(pltpu.make_async_copy(src_slice(src_refs[si], me), dst_slice(dst_refs[di], me),
                                           send_sems.at[t * N_DEV]))
        for dd in range(1, N_DEV):
            px, py, pc = _peer(x, y, c, dd)
            p = 4 * px + 2 * py + pc
            remote.append(pltpu.make_async_remote_copy(
                src_ref=src_slice(src_refs[si], p), dst_ref=dst_slice(dst_refs[di], me if sending else p),
                send_sem=send_sems.at[t * N_DEV + dd], recv_sem=recv_sems.at[t * N_DEV + dd],
                device_id=(px, py, pc), device_id_type=MESH))
    return local, remote


def _whole(ref, p):
    return ref


def _layer_of(layer, inner):
    return lambda ref, p: inner(ref.at[layer], p)


def _cols_of(width):
    def take(ref, p):
        lead = (slice(None),) * (len(ref.shape) - 1)
        return ref.at[lead + (pl.ds(pl.multiple_of(p * width, LANES), width),)]
    return take


def _rows_of(height):
    def take(ref, p):
        lead = (slice(None),) * (len(ref.shape) - 2)
        return ref.at[lead + (pl.ds(pl.multiple_of(p * height, SUBLANES), height), slice(None))]
    return take


def _slot(layer=None):
    if layer is None:
        return lambda ref, p: ref.at[p]
    return lambda ref, p: ref.at[p, layer]


def _local_step(x0, target, mod, g_mix_pre, g_mix_post, g_ffn_pre, g_ffn_post, wqkv, wo, wg, wu, wd,
                wfg, bfg, wconv, bconv, tiles, comm=None):
    s, d = x0.shape
    nl = mod.shape[0]
    n_heads = d // HEAD_DIM
    hp_n = d // LANES
    assert tiles["sb_fwd"] == tiles["sb_bwd"] and tiles["fox_fwd"] == tiles["fox_bwd"]
    vec = lambda a: a.reshape(1, -1)
    saved = []
    xcur = x0
    for l in range(nl):
        if comm is not None:
            wqkv, wo, wg, wu, wd = comm["weights"]()
        sh_a, sc_a, gt_a, sh_f, sc_f, gt_f = (vec(mod[l, i * d:(i + 1) * d]) for i in range(6))
        fox = l % 2 == 1
        if l == 0:
            h1 = _norm_mod_fwd(xcur, vec(g_mix_pre[l]), sh_a, sc_a, f"norm_mix_fwd_{l}")
        qkv = _mm_nn(h1, wqkv, l, BF16, f"qkv_fwd_{l}", split_out=d)
        rider = None if comm is None else comm["fwd_rider"](l)
        if fox:
            j = l // 2
            fl = _mm_nn(h1, wfg, j, F32, f"fgate_fwd_{l}")
            cum = _fox_prep(fl, bfg[j], f"fox_prep_{l}")
            ck = cum[:, :n_heads].T.reshape(hp_n, 2, s)
            (o, *stat), moved = _fox_fwd(qkv, _key_bound(qkv, f"key_bound_{l}"), cum, ck, *tiles["fox_fwd"],
                                         f"fox_fwd_{l}", rider)
            extra = (fl, cum, ck)
        else:
            (o, *stat), moved = _sb_fwd(qkv, _key_bound(qkv, f"key_bound_{l}"), *tiles["sb_fwd"], f"sb_fwd_{l}", rider)
            extra = None
        if moved is not None:
            comm["done"](moved)
            wqkv, wo, wg, wu, wd = comm["weights"]()
        u = _mm_nn(o, wo, l, F32, f"attn_out_fwd_{l}")
        x2, h2 = _post_norm_fwd(xcur, u, vec(g_mix_post[l]), gt_a, vec(g_ffn_pre[l]), sh_f, sc_f, f"post_mix_fwd_{l}")
        gp = _mm_nn(h2, wg, l, BF16, f"ffn_gate_fwd_{l}")
        up = _mm_nn(h2, wu, l, BF16, f"ffn_up_fwd_{l}")
        act = _conv_act_fwd(gp, up, wconv[l], bconv[l], f"conv_act_fwd_{l}")
        yv = _mm_nn(act, wd, l, F32, f"ffn_down_fwd_{l}")
        saved_h1 = h1
        if l + 1 < nl:
            sh_n, sc_n = (vec(mod[l + 1, i * d:(i + 1) * d]) for i in (0, 1))
            x3, h1 = _post_norm_fwd(x2, yv, vec(g_ffn_post[l]), gt_f, vec(g_mix_pre[l + 1]), sh_n, sc_n,
                                    f"post_ffn_fwd_{l}")
        else:
            x3 = _post_fwd(x2, yv, vec(g_ffn_post[l]), gt_f, f"post_ffn_fwd_{l}")
        saved.append((xcur, saved_h1, qkv, o, stat, extra, u, x2, h2, gp, up, act, yv))
        xcur = x3

    dx, sq = _loss_head(xcur, target, "loss_head")
    small, big = [None] * nl, [None] * nl
    for l in reversed(range(nl)):
        xin, h1, qkv, o, stat, extra, u, x2, h2, gp, up, act, yv = saved[l]
        sc_a, gt_a, sc_f, gt_f = (vec(mod[l, i * d:(i + 1) * d]) for i in (1, 2, 4, 5))
        fox = l % 2 == 1
        dy, dgt_f, dg4 = _post_bwd(dx, yv, vec(g_ffn_post[l]), gt_f, f"post_ffn_bwd_{l}")
        dact = _mm_nt(dy, wd, l, BF16, f"ffn_down_dx_{l}")
        dwd = _mm_tn(act, dy, BF16, f"ffn_down_dw_{l}")
        dgp, dup, dwc, dbc = _conv_act_bwd(dact, gp, up, wconv[l], bconv[l], f"conv_act_bwd_{l}")
        dh2 = _mm_nt(dgp, wg, l, F32, f"ffn_gate_dx_{l}")
        dh2 = _mm_nt(dup, wu, l, F32, f"ffn_up_dx_{l}", add=dh2)
        dwg = _mm_tn(h2, dgp, BF16, f"ffn_gate_dw_{l}")
        dwu = _mm_tn(h2, dup, BF16, f"ffn_up_dw_{l}")
        dx2, dsh_f, dsc_f, dg3, du, dgt_a, dg2 = _norm_post_bwd(dh2, x2, vec(g_ffn_pre[l]), sc_f, dx, u,
                                                                vec(g_mix_post[l]), gt_a, f"norm_ffn_bwd_{l}")
        do = _mm_nt(du, wo, l, BF16, f"attn_out_dx_{l}")
        dwo = _mm_tn(o, du, BF16, f"attn_out_dw_{l}")
        rider = None
        if comm is not None:
            comm["grads"](l, dict(gate=dwg, up=dwu, down=dwd))
            rider = comm["bwd_rider"]()
        if fox:
            j = l // 2
            fl, cum, ck = extra
            (dqkv, dcq, dck), moved = _fox_bwd(qkv, o, do, *stat, cum, ck, *tiles["fox_bwd"], f"fox_bwd_{l}", rider)
            dcq = jnp.max(dcq.reshape(hp_n, s, 2, HEAD_DIM), axis=3)
            dcum = dcq.transpose(1, 0, 2).reshape(s, n_heads) + dck.reshape(n_heads, s).T
            dcum = jnp.pad(dcum, ((0, 0), (0, LANES - n_heads)))
            dfl, dbfg = _fox_gate_bwd(dcum, fl, bfg[j], n_heads, f"fox_gate_bwd_{l}")
            dh1 = _mm_nt(dfl, wfg, j, F32, f"fgate_dx_{l}")
            dh1 = _mm_nt(dqkv, wqkv, l, F32, f"qkv_dx_{l}", add=dh1)
            dwfg = _mm_tn(h1, dfl, F32, f"fgate_dw_{l}")[:, :n_heads]
            dbfg = dbfg[0, :n_heads]
        else:
            (dqkv,), moved = _sb_bwd(qkv, *stat, do, *tiles["sb_bwd"], f"sb_bwd_{l}", rider)
            dh1 = _mm_nt(dqkv, wqkv, l, F32, f"qkv_dx_{l}")
            dwfg = dbfg = None
        if moved is not None:
            comm["done"](moved)
        dwqkv = _mm_tn(h1, dqkv, BF16, f"qkv_dw_{l}")
        dx, dsh_a, dsc_a, dg1 = _norm_mod_bwd(dh1, xin, vec(g_mix_pre[l]), sc_a, dx2, f"norm_mix_bwd_{l}")
        dmod = jnp.concatenate([dsh_a, dsc_a, dgt_a, dsh_f, dsc_f, dgt_f], axis=1)[0]
        small[l] = dict(dmod=dmod, dg1=dg1[0], dg2=dg2[0], dg3=dg3[0], dg4=dg4[0], dbc=dbc[0], dwc=dwc,
                        dbfg=dbfg, dwfg=dwfg)
        big[l] = dict(qkv=dwqkv, o=dwo, gate=dwg, up=dwu, down=dwd)
        if comm is not None:
            comm["grads"](l, dict(qkv=dwqkv, o=dwo))
    return sq, dx, small, big


def _rows128(a, rows):
    flat = a.reshape(-1)
    return jnp.pad(flat, (0, rows * LANES - flat.shape[0])).reshape(rows, LANES)


def _ceil8(n_elems):
    rows = -(-n_elems // LANES)
    return -(-rows // SUBLANES) * SUBLANES


def kernel(x, c, w_mod, b_mod, g_mix_pre, g_mix_post, w_qkv, w_o, w_fg, b_fg, g_ffn_pre, g_ffn_post, w_ffn_gate, w_ffn_up, w_conv, b_conv, w_ffn_down, loss_target, m_w_mod, m_b_mod, m_g_mix_pre, m_g_mix_post, m_w_qkv, m_w_o, m_w_fg, m_b_fg, m_g_ffn_pre, m_g_ffn_post, m_w_ffn_gate, m_w_ffn_up, m_w_conv, m_b_conv, m_w_ffn_down, v_w_mod, v_b_mod, v_g_mix_pre, v_g_mix_post, v_w_qkv, v_w_o, v_w_fg, v_b_fg, v_g_ffn_pre, v_g_ffn_post, v_w_ffn_gate, v_w_ffn_up, v_w_conv, v_b_conv, v_w_ffn_down):
    _, s, d = x.shape
    nl = w_qkv.shape[0]
    nf = w_fg.shape[0]
    n_heads = w_fg.shape[2]
    fs = w_ffn_gate.shape[2]
    fp = -(-fs // LANES) * LANES
    f_full, f_pad = N_DEV * fs, N_DEV * fp
    mod_cols = w_mod.shape[2]
    qs, orows = w_qkv.shape[2], w_o.shape[1]
    xi, yi, ci = _mesh_pos()
    me = 4 * xi + 2 * yi + ci

    c_rows = d // LANES
    c_all = _allgather_small(jnp.pad(c.reshape(1, d), ((0, SUBLANES - 1), (0, 0))).reshape(SUBLANES * c_rows, LANES),
                             "gather_cond")
    c_all = c_all.reshape(N_DEV, SUBLANES, d)[:, 0, :]
    b_mod_cols = lax.dynamic_slice_in_dim(b_mod, me * mod_cols, mod_cols, axis=1).reshape(nl, 1, mod_cols)
    mod_part = _mod_fwd(c_all, w_mod, b_mod_cols, "mod_fwd")

    conv_pad = jnp.pad(w_conv, ((0, 0), (0, 0), (0, fp - fs)))
    r_mod, r_conv, r_fg = _ceil8(mod_part.size), _ceil8(conv_pad.size), _ceil8(w_fg.size)
    payload = jnp.concatenate([_rows128(mod_part, r_mod), _rows128(conv_pad, r_conv), _rows128(w_fg, r_fg)], axis=0)
    got = _allgather_small(payload, "gather_small_weights").reshape(N_DEV, r_mod + r_conv + r_fg, LANES)
    mod_g = got[:, :r_mod].reshape(N_DEV, -1)[:, :mod_part.size].reshape(N_DEV, nl, N_DEV, mod_cols)
    mod = lax.dynamic_index_in_dim(mod_g, me, axis=2, keepdims=False).transpose(1, 0, 2).reshape(nl, N_DEV * mod_cols)
    conv_g = got[:, r_mod:r_mod + r_conv].reshape(N_DEV, -1)[:, :conv_pad.size].reshape(N_DEV, nl, 3, fp)
    wconv_full = conv_g.transpose(1, 2, 0, 3).reshape(nl, 3, f_pad)
    fg_g = got[:, r_mod + r_conv:].reshape(N_DEV, -1)[:, :w_fg.size].reshape(N_DEV, nf, orows, n_heads)
    wfg_full = fg_g.transpose(1, 0, 2, 3).reshape(nf, d, n_heads)
    wfg_full = jnp.pad(wfg_full, ((0, 0), (0, 0), (0, LANES - n_heads))).astype(BF16)
    bfg_full = jnp.pad(b_fg, ((0, 0), (0, LANES - n_heads))).reshape(nf, 1, LANES)
    bconv_full = jnp.pad(b_conv.reshape(nl, N_DEV, fs), ((0, 0), (0, 0), (0, fp - fs))).reshape(nl, 1, f_pad)

    gate_sh = jnp.pad(w_ffn_gate, ((0, 0), (0, 0), (0, fp - fs))).astype(BF16)
    up_sh = jnp.pad(w_ffn_up, ((0, 0), (0, 0), (0, fp - fs))).astype(BF16)
    down_sh = jnp.pad(w_ffn_down, ((0, 0), (0, fp - fs), (0, 0))).astype(BF16)
    shards = [w_qkv.astype(BF16), w_o.astype(BF16), gate_sh, up_sh, down_sh]
    full_shapes = [jax.ShapeDtypeStruct((nl, d, N_DEV * qs), BF16), jax.ShapeDtypeStruct((nl, d, d), BF16),
                   jax.ShapeDtypeStruct((nl, d, f_pad), BF16), jax.ShapeDtypeStruct((nl, d, f_pad), BF16),
                   jax.ShapeDtypeStruct((nl, f_pad, d), BF16)]
    place = [_cols_of(qs), _rows_of(orows), _cols_of(fp), _cols_of(fp), _rows_of(fp)]
    mixer_w, ffn_w = (0, 1), (2, 3, 4)

    def gather_jobs(l, which):
        return [(i, i, _layer_of(l, _whole), _layer_of(l, place[i])) for i in which]

    order = ["qkv", "o", "gate", "up", "down"]
    send = {"qkv": _cols_of(qs), "o": _rows_of(orows), "gate": _cols_of(fp), "up": _cols_of(fp), "down": _rows_of(fp)}
    recv_shapes = [(N_DEV, nl, d, qs), (N_DEV, nl, orows, d), (N_DEV, nl, d, fp), (N_DEV, nl, d, fp), (N_DEV, nl, fp, d)]
    state = {"full": _exchange(shards, [lax.empty(sh.shape, sh.dtype) for sh in full_shapes],
                               gather_jobs(0, mixer_w), "gather_weights_0"),
             "recv": [lax.empty(sh, BF16) for sh in recv_shapes], "pending": [], "moving": None}

    def mixer_host(t):
        fox_before = [l for l in range(t) if l % 2 == 1]
        return fox_before[-1] if fox_before else t - 1

    def fwd_rider(l):
        state["moving"] = "full"
        jobs = gather_jobs(l, ffn_w)
        for t in range(1, nl):
            if mixer_host(t) == l:
                jobs += gather_jobs(t, mixer_w)
        return shards, state["full"], jobs

    def bwd_rider():
        waiting, state["pending"], state["moving"] = state["pending"], [], "recv"
        if not waiting:
            return None
        jobs = [(k, order.index(nm), send[nm], _slot(l)) for k, (l, nm, _) in enumerate(waiting)]
        return [g for _, _, g in waiting], state["recv"], jobs

    def done(dsts):
        state[state["moving"]] = dsts

    def grads(l, new):
        state["pending"] += [(l, nm, new[nm]) for nm in order if nm in new]

    comm = dict(weights=lambda: state["full"], fwd_rider=fwd_rider, bwd_rider=bwd_rider, done=done, grads=grads)
    sq, dx, small, big = _local_step(x[0], loss_target[0], mod, g_mix_pre, g_mix_post, g_ffn_pre, g_ffn_post,
                                     None, None, None, None, None, wfg_full, bfg_full, wconv_full, bconv_full,
                                     ATTN_TILES, comm)
    loss = lax.psum(0.5 * jnp.sum(sq) / d, ("x", "y", "c"))
    recv = _exchange(*bwd_rider(), "scatter_grads_last")
    upd = {}
    for nm, rv, wt, mt, vt in zip(order, recv, [w_qkv, w_o, w_ffn_gate, w_ffn_up, w_ffn_down],
                                  [m_w_qkv, m_w_o, m_w_ffn_gate, m_w_ffn_up, m_w_ffn_down],
                                  [v_w_qkv, v_w_o, v_w_ffn_gate, v_w_ffn_up, v_w_ffn_down]):
        upd[nm] = _adamw(rv, wt, mt, vt, f"adamw_{nm}")

    stack = lambda key: jnp.stack([small[l][key] for l in range(nl)])
    dmod = stack("dmod")
    dgs = [stack(k) for k in ("dg1", "dg2", "dg3", "dg4")]
    dbc = stack("dbc").reshape(nl, N_DEV, fp)[:, :, :fs].reshape(nl, f_full)
    dbfg = jnp.stack([small[l]["dbfg"] for l in range(nl) if l % 2 == 1])
    dwc = stack("dwc").reshape(nl, 3, N_DEV, fp)[:, :, :, :fs].reshape(nl, 3, f_full)
    dwfg = jnp.stack([small[l]["dwfg"] for l in range(nl) if l % 2 == 1])
    rep_parts = [dmod] + dgs + [dbc, dbfg]
    rep_rows = [_ceil8(p.size) for p in rep_parts]
    r_rep, r_wc, r_wfg = sum(rep_rows), _ceil8(dwc.size), _ceil8(dwfg.size)
    payload = jnp.concatenate([_rows128(p, r) for p, r in zip(rep_parts, rep_rows)]
                              + [_rows128(dwc, r_wc), _rows128(dwfg, r_wfg)], axis=0)
    gsm = _allgather_small(payload, "gather_small_grads").reshape(N_DEV, 1, r_rep + r_wc + r_wfg, LANES)

    def pack(parts):
        return jnp.concatenate([_rows128(p, r) for p, r in zip(parts, rep_rows)], axis=0).reshape(1, r_rep, LANES)

    rep_w = [b_mod, g_mix_pre, g_mix_post, g_ffn_pre, g_ffn_post, b_conv, b_fg]
    rep_m = [m_b_mod, m_g_mix_pre, m_g_mix_post, m_g_ffn_pre, m_g_ffn_post, m_b_conv, m_b_fg]
    rep_v = [v_b_mod, v_g_mix_pre, v_g_mix_post, v_g_ffn_pre, v_g_ffn_post, v_b_conv, v_b_fg]
    rep_out = _adamw(gsm[:, :, :r_rep], pack(rep_w), pack(rep_m), pack(rep_v), "adamw_replicated", tr=r_rep, tc=LANES)

    def unpack(packed):
        outs, at = [], 0
        for p, r in zip(rep_w, rep_rows):
            outs.append(packed[0, at:at + r].reshape(-1)[:p.size].reshape(p.shape))
            at += r
        return outs

    rep_g, rep_d, rep_nm, rep_nv = (unpack(a) for a in rep_out)

    wc_all = gsm[:, 0, r_rep:r_rep + r_wc].reshape(N_DEV, -1)[:, :dwc.size].reshape(N_DEV, 1, nl * 3, f_full)
    wc_mine = lax.dynamic_slice_in_dim(wc_all, me * fs, fs, axis=3)
    wc_out = _adamw(wc_mine, w_conv.reshape(1, nl * 3, fs), m_w_conv.reshape(1, nl * 3, fs),
                    v_w_conv.reshape(1, nl * 3, fs), "adamw_conv", tr=nl * 3, tc=fs)
    wc_out = [a.reshape(nl, 3, fs) for a in wc_out]
    wfg_all = gsm[:, 0, r_rep + r_wc:].reshape(N_DEV, -1)[:, :dwfg.size].reshape(N_DEV, nf, d, n_heads)
    wfg_mine = lax.dynamic_slice_in_dim(wfg_all, me * orows, orows, axis=2)
    wfg_out = _adamw(wfg_mine, w_fg, m_w_fg, v_w_fg, "adamw_fgate", tr=orows, tc=n_heads)

    dmod_all = gsm[:, 0, :rep_rows[0]].reshape(N_DEV, -1)[:, :dmod.size].reshape(N_DEV, nl, N_DEV * mod_cols)
    dmod_mine = lax.dynamic_slice_in_dim(dmod_all, me * mod_cols, mod_cols, axis=2)
    gwmod = _wmod_grad(c_all.T, dmod_mine, "wmod_grad")
    wmod_out = _adamw(gwmod.reshape(1, nl, d, mod_cols), w_mod, m_w_mod, v_w_mod, "adamw_mod")

    per_weight = [wmod_out, None, None, None, upd["qkv"], upd["o"], wfg_out, None, None, None,
                  upd["gate"], upd["up"], wc_out, None, upd["down"]]
    rep_index = {1: 0, 2: 1, 3: 2, 8: 3, 9: 4, 13: 5, 7: 6}
    outs = [[], [], [], []]
    for pos, res in enumerate(per_weight):
        for kind in range(4):
            if res is None:
                outs[kind].append((rep_g, rep_d, rep_nm, rep_nv)[kind][rep_index[pos]])
            else:
                outs[kind].append(res[kind])
    return (loss, dx.reshape(1, s, d), *outs[0], *outs[1], *outs[2], *outs[3])
```
